```python
import math
import jax, jax.numpy as jnp
from jax import lax
import numpy as np

D_MODEL = 1024
BATCH = 4
SEQ = 4096
DEPTH = 2
DEC_BATCH = 4
DEC_SEQ = 8192
PAST_LEN = 128

MIX_WIDTH = D_MODEL
GROUP_WIDTH = MIX_WIDTH // 4
HEAD_DIM = 64
S5_GROUP_DIM = 16
S5_GROUPS = GROUP_WIDTH // S5_GROUP_DIM
S5_STATE = 64
GQA_HEADS = GROUP_WIDTH // HEAD_DIM
GQA_KV_HEADS = 2
GQA_REP = GQA_HEADS // GQA_KV_HEADS
DIFF_HEADS = GROUP_WIDTH // HEAD_DIM
HY_WIDTH = GROUP_WIDTH
HY_ORDER = 2
HY_BANDS = 16
HY_EMB = 2 * HY_BANDS + 1
HY_FFN = 64
HY_DECAY_MIN = math.log(100.0) / 1.5
HY_DECAY_MAX = math.log(100.0) / 0.3
N_EXPERTS = 32
TOP_K = 4
D_FF_EXPERT = D_MODEL
SWIGLU_LIMIT = 7.0
SWIGLU_ALPHA = 1.702
NUM_BUCKETS = 32
MAX_DISTANCE = 128
GRID_W = 64
ROPE_THETA = 10000.0
ROPE_AXIS_DIM = HEAD_DIM // 2
Q_BLOCK = 128
PLE_DIM = 256
NORM_EPS = 1e-6

S5_COLS = GROUP_WIDTH
GQA_QW = GQA_HEADS * HEAD_DIM
GQA_KW = GQA_KV_HEADS * HEAD_DIM
GQA_COLS = GQA_QW + 2 * GQA_KW
DIFF_QW = DIFF_HEADS * 2 * HEAD_DIM
DIFF_COLS = 2 * DIFF_QW + DIFF_HEADS * HEAD_DIM
HY_COLS = (HY_ORDER + 1) * HY_WIDTH
OFF_GQA = S5_COLS
OFF_DIFF = OFF_GQA + GQA_COLS
OFF_HY = OFF_DIFF + DIFF_COLS
IN_COLS = OFF_HY + HY_COLS

kernel_name = 'hymba_style_s5_gqa_diff_hyena_moe_encoder'


def rms_norm(x, g=None):
    xf = x.astype(jnp.float32)
    y = xf * lax.rsqrt(jnp.mean(jnp.square(xf), -1, keepdims=True) + NORM_EPS)
    if g is not None:
        y = y * g.astype(jnp.float32)
    return y.astype(x.dtype)


def to_blocks(q):
    B, L = q.shape[:2]
    return jnp.moveaxis(q.reshape(B, L // Q_BLOCK, Q_BLOCK, *q.shape[2:]), 1, 0)


def from_blocks(o):
    nb, B, qb = o.shape[:3]
    return jnp.moveaxis(o, 0, 1).reshape(B, nb * qb, *o.shape[3:])


def axial_rope(L):
    rows = L // GRID_W
    row = jnp.repeat(jnp.arange(rows, dtype=jnp.float32), GRID_W)
    col = jnp.tile(jnp.arange(GRID_W, dtype=jnp.float32), rows)
    freq = ROPE_THETA ** (-jnp.arange(0, ROPE_AXIS_DIM, 2, dtype=jnp.float32) / ROPE_AXIS_DIM)
    ang = jnp.concatenate([row[:, None] * freq, col[:, None] * freq], -1)
    return jnp.cos(ang), jnp.sin(ang)


def apply_rope(x, cos, sin):
    xf = x.astype(jnp.float32).reshape(*x.shape[:-1], HEAD_DIM // 2, 2)
    x0, x1 = xf[..., 0], xf[..., 1]
    c = cos[None, :, None, :]
    s = sin[None, :, None, :]
    return jnp.stack([x0 * c - x1 * s, x0 * s + x1 * c], -1).reshape(x.shape).astype(x.dtype)


def t5_bucket(rel):
    half = NUM_BUCKETS // 2
    max_exact = half // 2
    ret = jnp.where(rel > 0, half, 0)
    n = jnp.abs(rel)
    nf = jnp.maximum(n, 1).astype(jnp.float32)
    large = max_exact + (jnp.log(nf / max_exact) / math.log(MAX_DISTANCE / max_exact) * (half - max_exact)).astype(jnp.int32)
    large = jnp.minimum(large, half - 1)
    return ret + jnp.where(n < max_exact, n, large)


def cplx_combine(e1, e2):
    a1r, a1i, b1r, b1i = e1
    a2r, a2i, b2r, b2i = e2
    return (a2r * a1r - a2i * a1i, a2r * a1i + a2i * a1r,
            a2r * b1r - a2i * b1i + b2r, a2r * b1i + a2i * b1r + b2i)


def s5_mixer(u, lam_re, lam_im, log_dt, b_re, b_im, c_re, c_im, d_skip, w_glu, b_glu):
    B, L, _ = u.shape
    f32 = jnp.float32
    uf = u.astype(f32).reshape(B, L, S5_GROUPS, S5_GROUP_DIM)
    y = uf * d_skip.astype(f32).reshape(S5_GROUPS, S5_GROUP_DIM)
    for dr in range(2):
        lr = jnp.minimum(lam_re[dr].astype(f32), -1e-4)
        li = lam_im[dr].astype(f32)
        dt = jnp.exp(log_dt[dr].astype(f32))[:, None]
        mag = jnp.exp(lr * dt)
        ar = mag * jnp.cos(li * dt)
        ai = mag * jnp.sin(li * dt)
        den = lr * lr + li * li
        cr = ((ar - 1.0) * lr + ai * li) / den
        ci = (ai * lr - (ar - 1.0) * li) / den
        br = b_re[dr].astype(f32)
        bi = b_im[dr].astype(f32)
        bbr = cr[..., None] * br - ci[..., None] * bi
        bbi = cr[..., None] * bi + ci[..., None] * br
        bur = jnp.einsum('blgh,gnh->blgn', uf, bbr)
        bui = jnp.einsum('blgh,gnh->blgn', uf, bbi)
        _, _, xr, xi = lax.associative_scan(
            cplx_combine,
            (jnp.broadcast_to(ar, bur.shape), jnp.broadcast_to(ai, bur.shape), bur, bui),
            reverse=(dr == 1), axis=1)
        y = y + jnp.einsum('blgn,ghn->blgh', xr, c_re[dr].astype(f32)) - jnp.einsum('blgn,ghn->blgh', xi, c_im[dr].astype(f32))
    y = jax.nn.gelu(y.reshape(B, L, S5_COLS))
    return (y * jax.nn.sigmoid(y @ w_glu.astype(f32) + b_glu.astype(f32))).astype(u.dtype)


def gqa_mixer(z, q_g, k_g, cos, sin):
    B, L, _ = z.shape
    q = z[..., :GQA_QW].reshape(B, L, GQA_HEADS, HEAD_DIM)
    k = z[..., GQA_QW:GQA_QW + GQA_KW].reshape(B, L, GQA_KV_HEADS, HEAD_DIM)
    v = z[..., GQA_QW + GQA_KW:].reshape(B, L, GQA_KV_HEADS, HEAD_DIM)
    q = apply_rope(rms_norm(q, q_g), cos, sin).reshape(B, L, GQA_KV_HEADS, GQA_REP, HEAD_DIM)
    k = apply_rope(rms_norm(k, k_g), cos, sin)
    scale = HEAD_DIM ** -0.5

    def block(qb):
        s = jnp.einsum('bqkgd,bskd->bkgqs', qb, k, preferred_element_type=jnp.float32) * scale
        p = jax.nn.softmax(s, axis=-1).astype(v.dtype)
        return jnp.einsum('bkgqs,bskd->bqkgd', p, v)

    o = from_blocks(lax.map(block, to_blocks(q)))
    return o.reshape(B, L, GQA_QW)


def diff_mixer(z, q_g, k_g, lam_q1, lam_k1, lam_q2, lam_k2, subln_g, rel_bias, lambda_init):
    B, L, _ = z.shape
    f32 = jnp.float32
    q = rms_norm(z[..., :DIFF_QW].reshape(B, L, DIFF_HEADS, 2, HEAD_DIM), q_g)
    k = rms_norm(z[..., DIFF_QW:2 * DIFF_QW].reshape(B, L, DIFF_HEADS, 2, HEAD_DIM), k_g)
    v = z[..., 2 * DIFF_QW:].reshape(B, L, DIFF_HEADS, HEAD_DIM)
    lam = (jnp.exp(jnp.sum(lam_q1.astype(f32) * lam_k1.astype(f32)))
           - jnp.exp(jnp.sum(lam_q2.astype(f32) * lam_k2.astype(f32))) + lambda_init)
    kpos = jnp.arange(L, dtype=jnp.int32)
    scale = HEAD_DIM ** -0.5

    def block(args):
        qb, start = args
        qpos = start + jnp.arange(Q_BLOCK, dtype=jnp.int32)
        bias = jnp.transpose(rel_bias[t5_bucket(kpos[None, :] - qpos[:, None])], (2, 0, 1)).astype(f32)
        s = jnp.einsum('bqhmd,bshmd->bmhqs', qb, k, preferred_element_type=f32) * scale + bias[None, None]
        a = jax.nn.softmax(s, axis=-1)
        w = a[:, 0] - lam * a[:, 1]
        return jnp.einsum('bhqs,bshd->bqhd', w.astype(v.dtype), v)

    starts = jnp.arange(L // Q_BLOCK, dtype=jnp.int32) * Q_BLOCK
    o = from_blocks(lax.map(block, (to_blocks(q), starts)))
    o = rms_norm(o, subln_g) * (1.0 - lambda_init)
    return o.reshape(B, L, DIFF_HEADS * HEAD_DIM)


def hyena_filters(L, w1, b1, freq1, w2, b2, freq2, w3, decay):
    f32 = jnp.float32
    j = jnp.arange(L, dtype=f32)
    t = j / L
    ang = 2.0 * jnp.pi * t[:, None] * jnp.arange(1, HY_BANDS + 1, dtype=f32)
    feat = jnp.concatenate([t[:, None], jnp.cos(ang), jnp.sin(ang)], -1)
    hid = jnp.sin(freq1.astype(f32) * (feat @ w1.astype(f32) + b1.astype(f32)))
    hid = jnp.sin(freq2.astype(f32) * (hid @ w2.astype(f32) + b2.astype(f32)))
    h = (hid @ w3.astype(f32)).reshape(L, HY_ORDER, HY_WIDTH)
    half = L // 2
    dist = jnp.abs(j - half) / half
    h = h * jnp.exp(-dist[:, None, None] * jnp.abs(decay.astype(f32)))
    return h / jnp.sum(jnp.abs(h), axis=0, keepdims=True)


def hyena_mixer(z, conv_w, conv_b, w1, b1, freq1, w2, b2, freq2, w3, decay, f_bias):
    B, L, _ = z.shape
    f32 = jnp.float32
    zp = jnp.pad(z, ((0, 0), (1, 1), (0, 0)))
    zc = conv_w[0] * zp[:, :-2] + conv_w[1] * zp[:, 1:-1] + conv_w[2] * zp[:, 2:] + conv_b
    v = zc[..., :HY_WIDTH]
    gates = (zc[..., HY_WIDTH:2 * HY_WIDTH], zc[..., 2 * HY_WIDTH:])
    h = hyena_filters(L, w1, b1, freq1, w2, b2, freq2, w3, decay)
    hf = jnp.fft.rfft(h, n=2 * L, axis=0)
    y = v.astype(f32)
    for o in range(HY_ORDER):
        yf = jnp.fft.rfft(y, n=2 * L, axis=1)
        c = jnp.fft.irfft(yf * hf[:, o], n=2 * L, axis=1)[:, L // 2:L // 2 + L]
        y = gates[o].astype(f32) * (c + f_bias[o].astype(f32) * y)
    return y.astype(z.dtype)


def moe(v, w_router, b_router, w_gu, b_gu, w_down, b_down):
    B, L, Dm = v.shape
    xt = v.reshape(B * L, Dm)
    logits = (xt @ w_router + b_router).astype(jnp.float32)
    top_v, top_i = lax.top_k(logits, TOP_K)
    top_w = jax.nn.softmax(top_v, axis=-1)
    gates = jnp.einsum('tk,tke->te', top_w, jax.nn.one_hot(top_i, N_EXPERTS, dtype=jnp.float32))
    out = jnp.zeros((B * L, Dm), jnp.float32)
    for e in range(N_EXPERTS):
        hgu = xt @ w_gu[e] + b_gu[e]
        glu = jnp.minimum(hgu[:, ::2], SWIGLU_LIMIT)
        lin = jnp.clip(hgu[:, 1::2], -SWIGLU_LIMIT, SWIGLU_LIMIT)
        act = glu * jax.nn.sigmoid(SWIGLU_ALPHA * glu) * (lin + 1.0)
        out = out + gates[:, e:e + 1] * (act @ w_down[e] + b_down[e])
    return out.astype(v.dtype).reshape(B, L, Dm)


def run_trunk(x, p, prm):
    B, L, _ = x.shape
    cos, sin = axial_rope(L)
    h = x
    for i in range(DEPTH):
        u = rms_norm(h, prm['norm1_g'][i])
        z = u @ prm['w_in'][i]
        y_s5 = s5_mixer(z[..., :OFF_GQA], prm['s5_lam_re'][i], prm['s5_lam_im'][i], prm['s5_log_dt'][i],
                        prm['s5_b_re'][i], prm['s5_b_im'][i], prm['s5_c_re'][i], prm['s5_c_im'][i],
                        prm['s5_d'][i], prm['s5_w_glu'][i], prm['s5_b_glu'][i])
        y_gqa = gqa_mixer(z[..., OFF_GQA:OFF_DIFF], prm['gqa_q_g'][i], prm['gqa_k_g'][i], cos, sin)
        y_diff = diff_mixer(z[..., OFF_DIFF:OFF_HY], prm['diff_q_g'][i], prm['diff_k_g'][i],
                            prm['diff_lam_q1'][i], prm['diff_lam_k1'][i], prm['diff_lam_q2'][i], prm['diff_lam_k2'][i],
                            prm['diff_subln_g'][i], prm['rel_bias'], 0.8 - 0.6 * math.exp(-0.3 * i))
        y_hy = hyena_mixer(z[..., OFF_HY:], prm['hy_conv_w'][i], prm['hy_conv_b'][i], prm['hy_w1'][i], prm['hy_b1'][i],
                           prm['hy_freq1'][i], prm['hy_w2'][i], prm['hy_b2'][i], prm['hy_freq2'][i], prm['hy_w3'][i],
                           prm['hy_decay'][i], prm['hy_bias'][i])
        mixed = jnp.concatenate([rms_norm(y_s5, prm['out_g_s5'][i]), rms_norm(y_gqa, prm['out_g_gqa'][i]),
                                 y_diff, rms_norm(y_hy, prm['out_g_hy'][i])], axis=-1)
        h = h + mixed @ prm['w_out'][i]
        h = h + moe(rms_norm(h, prm['norm2_g'][i]), prm['w_router'][i], prm['b_router'][i],
                    prm['w_gu'][i], prm['b_gu'][i], prm['w_down'][i], prm['b_down'][i])
        h = h + (p[i] @ prm['w_ple'][i]) * jax.nn.sigmoid(rms_norm(h) @ prm['w_ple_gate'][i])
    return h


def setup_inputs(seed: int = 0) -> dict:
    key = jax.random.key(seed)
    ks = iter(jax.random.split(key, 64))
    f32 = jnp.float32

    def nrm(shape, scale):
        return scale * jax.random.normal(next(ks), shape, f32)

    def gain(shape):
        return 1.0 + nrm(shape, 0.02)

    lam_im0 = jnp.pi * jnp.arange(S5_STATE, dtype=f32)
    decay0 = jnp.linspace(HY_DECAY_MIN, HY_DECAY_MAX, HY_WIDTH, dtype=f32)
    return {
        'x_prompt': nrm((BATCH, SEQ, D_MODEL), 1.0),
        'x_sample': nrm((DEC_BATCH, DEC_SEQ, D_MODEL), 1.0),
        'p_prompt': nrm((DEPTH, BATCH, SEQ, PLE_DIM), 1.0),
        'p_sample': nrm((DEPTH, DEC_BATCH, DEC_SEQ, PLE_DIM), 1.0),
        'rel_bias': nrm((NUM_BUCKETS, DIFF_HEADS), 0.3),
        'norm1_g': gain((DEPTH, D_MODEL)),
        'w_in': nrm((DEPTH, D_MODEL, IN_COLS), D_MODEL ** -0.5),
        's5_lam_re': -0.5 + nrm((DEPTH, 2, S5_GROUPS, S5_STATE), 0.01),
        's5_lam_im': lam_im0 + nrm((DEPTH, 2, S5_GROUPS, S5_STATE), 0.01),
        's5_log_dt': jax.random.uniform(next(ks), (DEPTH, 2, S5_GROUPS), f32, math.log(1e-3), math.log(1e-1)),
        's5_b_re': nrm((DEPTH, 2, S5_GROUPS, S5_STATE, S5_GROUP_DIM), (2 * S5_GROUP_DIM) ** -0.5),
        's5_b_im': nrm((DEPTH, 2, S5_GROUPS, S5_STATE, S5_GROUP_DIM), (2 * S5_GROUP_DIM) ** -0.5),
        's5_c_re': nrm((DEPTH, 2, S5_GROUPS, S5_GROUP_DIM, S5_STATE), S5_STATE ** -0.5),
        's5_c_im': nrm((DEPTH, 2, S5_GROUPS, S5_GROUP_DIM, S5_STATE), S5_STATE ** -0.5),
        's5_d': nrm((DEPTH, S5_COLS), 1.0),
        's5_w_glu': nrm((DEPTH, S5_COLS, S5_COLS), S5_COLS ** -0.5),
        's5_b_glu': nrm((DEPTH, S5_COLS), 0.01),
        'gqa_q_g': gain((DEPTH, HEAD_DIM)),
        'gqa_k_g': gain((DEPTH, HEAD_DIM)),
        'diff_q_g': gain((DEPTH, HEAD_DIM)),
        'diff_k_g': gain((DEPTH, HEAD_DIM)),
        'diff_lam_q1': nrm((DEPTH, HEAD_DIM), 0.1),
        'diff_lam_k1': nrm((DEPTH, HEAD_DIM), 0.1),
        'diff_lam_q2': nrm((DEPTH, HEAD_DIM), 0.1),
        'diff_lam_k2': nrm((DEPTH, HEAD_DIM), 0.1),
        'diff_subln_g': gain((DEPTH, HEAD_DIM)),
        'hy_conv_w': nrm((DEPTH, 3, HY_COLS), 0.5),
        'hy_conv_b': nrm((DEPTH, HY_COLS), 0.01),
        'hy_w1': nrm((DEPTH, HY_EMB, HY_FFN), HY_EMB ** -0.5),
        'hy_b1': nrm((DEPTH, HY_FFN), 0.1),
        'hy_freq1': gain((DEPTH, HY_FFN)),
        'hy_w2': nrm((DEPTH, HY_FFN, HY_FFN), HY_FFN ** -0.5),
        'hy_b2': nrm((DEPTH, HY_FFN), 0.1),
        'hy_freq2': gain((DEPTH, HY_FFN)),
        'hy_w3': nrm((DEPTH, HY_FFN, HY_ORDER * HY_WIDTH), HY_FFN ** -0.5),
        'hy_decay': decay0 + nrm((DEPTH, HY_ORDER, HY_WIDTH), 0.1),
        'hy_bias': nrm((DEPTH, HY_ORDER, HY_WIDTH), 0.5),
        'out_g_s5': gain((DEPTH, GROUP_WIDTH)),
        'out_g_gqa': gain((DEPTH, GROUP_WIDTH)),
        'out_g_hy': gain((DEPTH, GROUP_WIDTH)),
        'w_out': nrm((DEPTH, MIX_WIDTH, D_MODEL), MIX_WIDTH ** -0.5),
        'norm2_g': gain((DEPTH, D_MODEL)),
        'w_router': nrm((DEPTH, D_MODEL, N_EXPERTS), D_MODEL ** -0.5),
        'b_router': nrm((DEPTH, N_EXPERTS), 0.01),
        'w_gu': nrm((DEPTH, N_EXPERTS, D_MODEL, 2 * D_FF_EXPERT), D_MODEL ** -0.5),
        'b_gu': nrm((DEPTH, N_EXPERTS, 2 * D_FF_EXPERT), 0.01),
        'w_down': nrm((DEPTH, N_EXPERTS, D_FF_EXPERT, D_MODEL), D_FF_EXPERT ** -0.5),
        'b_down': nrm((DEPTH, N_EXPERTS, D_MODEL), 0.01),
        'w_ple': nrm((DEPTH, PLE_DIM, D_MODEL), PLE_DIM ** -0.5),
        'w_ple_gate': nrm((DEPTH, D_MODEL, D_MODEL), D_MODEL ** -0.5),
    }


def reference(x_prompt, x_sample, p_prompt, p_sample, rel_bias, norm1_g, w_in, s5_lam_re, s5_lam_im, s5_log_dt,
              s5_b_re, s5_b_im, s5_c_re, s5_c_im, s5_d, s5_w_glu, s5_b_glu, gqa_q_g, gqa_k_g, diff_q_g, diff_k_g,
              diff_lam_q1, diff_lam_k1, diff_lam_q2, diff_lam_k2, diff_subln_g, hy_conv_w, hy_conv_b, hy_w1, hy_b1,
              hy_freq1, hy_w2, hy_b2, hy_freq2, hy_w3, hy_decay, hy_bias, out_g_s5, out_g_gqa, out_g_hy, w_out,
              norm2_g, w_router, b_router, w_gu, b_gu, w_down, b_down, w_ple, w_ple_gate):
    prm = dict(rel_bias=rel_bias, norm1_g=norm1_g, w_in=w_in, s5_lam_re=s5_lam_re, s5_lam_im=s5_lam_im,
               s5_log_dt=s5_log_dt, s5_b_re=s5_b_re, s5_b_im=s5_b_im, s5_c_re=s5_c_re, s5_c_im=s5_c_im,
               s5_d=s5_d, s5_w_glu=s5_w_glu, s5_b_glu=s5_b_glu, gqa_q_g=gqa_q_g, gqa_k_g=gqa_k_g,
               diff_q_g=diff_q_g, diff_k_g=diff_k_g, diff_lam_q1=diff_lam_q1, diff_lam_k1=diff_lam_k1,
               diff_lam_q2=diff_lam_q2, diff_lam_k2=diff_lam_k2, diff_subln_g=diff_subln_g,
               hy_conv_w=hy_conv_w, hy_conv_b=hy_conv_b, hy_w1=hy_w1, hy_b1=hy_b1, hy_freq1=hy_freq1,
               hy_w2=hy_w2, hy_b2=hy_b2, hy_freq2=hy_freq2, hy_w3=hy_w3, hy_decay=hy_decay, hy_bias=hy_bias,
               out_g_s5=out_g_s5, out_g_gqa=out_g_gqa, out_g_hy=out_g_hy, w_out=w_out, norm2_g=norm2_g,
               w_router=w_router, b_router=b_router, w_gu=w_gu, b_gu=b_gu, w_down=w_down, b_down=b_down,
               w_ple=w_ple, w_ple_gate=w_ple_gate)
    y_prompt = run_trunk(x_prompt, p_prompt, prm)
    y_sample = run_trunk(x_sample, p_sample, prm)
    return (y_prompt, y_sample)
```

```python
import functools
import math

import numpy as np
import jax
import jax.numpy as jnp
from jax import lax
from jax.experimental import pallas as pl
from jax.experimental.pallas import tpu as pltpu

F32 = jnp.float32
BF16 = jnp.bfloat16
I32 = jnp.int32
HI = lax.Precision.HIGHEST

D_MODEL = 1024
DEPTH = 2
GROUP_WIDTH = 256
HEAD_DIM = 64
S5_GROUP_DIM = 16
S5_GROUPS = 16
S5_STATE = 64
S5_NSTATE = S5_GROUPS * S5_STATE
GQA_HEADS = 4
GQA_KV_HEADS = 2
DIFF_HEADS = 4
HY_WIDTH = 256
HY_ORDER = 2
HY_BANDS = 16
HY_EMB = 2 * HY_BANDS + 1
HY_FFN = 64
N_EXPERTS = 32
TOP_K = 4
SWIGLU_LIMIT = 7.0
SWIGLU_ALPHA = 1.702
NUM_BUCKETS = 32
MAX_DISTANCE = 128
GRID_W = 64
ROPE_THETA = 10000.0
ROPE_AXIS_DIM = HEAD_DIM // 2
PLE_DIM = 256
NORM_EPS = 1e-6
ATTN_SCALE = HEAD_DIM ** -0.5

S5_COLS = GROUP_WIDTH
GQA_QW = GQA_HEADS * HEAD_DIM
GQA_KW = GQA_KV_HEADS * HEAD_DIM
GQA_COLS = GQA_QW + 2 * GQA_KW
DIFF_QW = DIFF_HEADS * 2 * HEAD_DIM
DIFF_COLS = 2 * DIFF_QW + DIFF_HEADS * HEAD_DIM
HY_COLS = (HY_ORDER + 1) * HY_WIDTH
OFF_GQA = S5_COLS
OFF_DIFF = OFF_GQA + GQA_COLS
OFF_HY = OFF_DIFF + DIFF_COLS
IN_COLS = OFF_HY + HY_COLS
IN_SPLITS = ((0, OFF_GQA), (OFF_GQA, OFF_DIFF), (OFF_DIFF, OFF_HY), (OFF_HY, IN_COLS))

LANES = 128
SUBLANES = 8
ROW_TILES = D_MODEL // LANES
FFT_N2 = 128
VMEM_LIMIT = 48 * 1024 * 1024

ROW_TILE = 256
ATTN_TILE = 512
S5_TIME_BLOCK = 64
MOE_TILE = 512
MOE_TOK_TILE = 256
FFT_COL_TILE = 4096


def _cp(*sem):
    return pltpu.CompilerParams(dimension_semantics=sem, vmem_limit_bytes=VMEM_LIMIT)


def _sds(shape, dtype=F32):
    return jax.ShapeDtypeStruct(shape, dtype)


def _rms(x, g=None):
    y = x * lax.rsqrt(jnp.mean(x * x, axis=-1, keepdims=True) + NORM_EPS)
    return y if g is None else y * g


def _inproj_body(x_ref, g_ref, w_ref, *o_refs):
    u = _rms(x_ref[...], g_ref[...]).astype(BF16)
    for o_ref, (lo, hi) in zip(o_refs, IN_SPLITS):
        o_ref[...] = jnp.dot(u, w_ref[:, lo:hi], preferred_element_type=F32)


def in_proj(h, g, w):
    t = h.shape[0]
    tm = min(ROW_TILE, t)
    return pl.pallas_call(
        _inproj_body,
        grid=(t // tm,),
        in_specs=[pl.BlockSpec((tm, D_MODEL), lambda i: (i, 0)),
                  pl.BlockSpec((1, D_MODEL), lambda i: (0, 0)),
                  pl.BlockSpec((D_MODEL, IN_COLS), lambda i: (0, 0))],
        out_specs=[pl.BlockSpec((tm, hi - lo), lambda i: (i, 0)) for lo, hi in IN_SPLITS],
        out_shape=[_sds((t, hi - lo)) for lo, hi in IN_SPLITS],
        compiler_params=_cp("parallel"),
        name="in_proj",
    )(h, g, w)


def s5_tables(lam_re, lam_im, log_dt, b_re, b_im, c_re, c_im):
    lr = jnp.minimum(lam_re, -1e-4)
    li = lam_im
    dt = jnp.exp(log_dt)[..., None]
    mag = jnp.exp(lr * dt)
    ar = mag * jnp.cos(li * dt)
    ai = mag * jnp.sin(li * dt)
    den = lr * lr + li * li
    cr = ((ar - 1.0) * lr + ai * li) / den
    ci = (ai * lr - (ar - 1.0) * li) / den
    bbr = cr[..., None] * b_re - ci[..., None] * b_im
    bbi = cr[..., None] * b_im + ci[..., None] * b_re
    eye = jnp.eye(S5_GROUPS, dtype=F32)
    bb = jnp.stack([bbr, bbi], axis=1)
    wb = jnp.einsum("dpgnh,gk->dghpkn", bb, eye).reshape(2 * S5_COLS, 2 * S5_NSTATE)
    cc = jnp.stack([c_re, -c_im], axis=0)
    wc = jnp.einsum("pdghn,gk->pgndkh", cc, eye).reshape(2 * S5_NSTATE, 2 * S5_COLS)
    a_re = jnp.repeat(ar.reshape(2, S5_NSTATE), 4, axis=0)
    a_im = jnp.repeat(ai.reshape(2, S5_NSTATE), 4, axis=0)
    return wb.astype(BF16), wc.astype(BF16), a_re, a_im


def _s5_body(u_ref, wb_ref, wc_ref, ar_ref, ai_ref, y_ref, bu_ref, xr_ref, xi_ref, *, tb):
    @pl.when(pl.program_id(0) == 0)
    def _():
        xr_ref[...] = jnp.zeros_like(xr_ref)
        xi_ref[...] = jnp.zeros_like(xi_ref)

    u = u_ref[...]
    fwd = (lax.broadcasted_iota(I32, (tb * SUBLANES, 1), 0) % SUBLANES) < 4
    uu = jnp.concatenate([jnp.where(fwd, u, 0.0), jnp.where(fwd, 0.0, u)], axis=1).astype(BF16)
    bu_ref[...] = jnp.dot(uu, wb_ref[...], preferred_element_type=F32)
    ar = ar_ref[...]
    ai = ai_ref[...]

    def step(t, carry):
        xr, xi = carry
        r0 = pl.multiple_of(t * SUBLANES, SUBLANES)
        rows = pl.ds(r0, SUBLANES)
        nxr = ar * xr - ai * xi + bu_ref[rows, 0:S5_NSTATE]
        nxi = ar * xi + ai * xr + bu_ref[rows, S5_NSTATE:2 * S5_NSTATE]
        bu_ref[rows, 0:S5_NSTATE] = nxr
        bu_ref[rows, S5_NSTATE:2 * S5_NSTATE] = nxi
        return nxr, nxi

    xr, xi = lax.fori_loop(0, tb, step, (xr_ref[...], xi_ref[...]))
    xr_ref[...] = xr
    xi_ref[...] = xi
    y2 = jnp.dot(bu_ref[...].astype(BF16), wc_ref[...], preferred_element_type=F32)
    y_ref[...] = jnp.where(fwd, y2[:, :S5_COLS], y2[:, S5_COLS:])


def s5_scan(u8, wb, wc, a_re, a_im, seq):
    tb = min(S5_TIME_BLOCK, seq)
    rows = tb * SUBLANES
    const = lambda i: (0, 0)
    return pl.pallas_call(
        functools.partial(_s5_body, tb=tb),
        grid=(seq // tb,),
        in_specs=[pl.BlockSpec((rows, S5_COLS), lambda i: (i, 0)),
                  pl.BlockSpec((2 * S5_COLS, 2 * S5_NSTATE), const),
                  pl.BlockSpec((2 * S5_NSTATE, 2 * S5_COLS), const),
                  pl.BlockSpec((SUBLANES, S5_NSTATE), const),
                  pl.BlockSpec((SUBLANES, S5_NSTATE), const)],
        out_specs=pl.BlockSpec((rows, S5_COLS), lambda i: (i, 0)),
        out_shape=_sds((seq * SUBLANES, S5_COLS)),
        scratch_shapes=[pltpu.VMEM((rows, 2 * S5_NSTATE), F32),
                        pltpu.VMEM((SUBLANES, S5_NSTATE), F32),
                        pltpu.VMEM((SUBLANES, S5_NSTATE), F32)],
        compiler_params=_cp("arbitrary"),
        name="s5_scan",
    )(u8, wb, wc, a_re, a_im)


def _s5_out_body(yf_ref, yb_ref, u_ref, d_ref, w_ref, b_ref, o_ref):
    y = jax.nn.gelu(yf_ref[...] + yb_ref[...] + u_ref[...] * d_ref[...])
    gate = jnp.dot(y.astype(BF16), w_ref[...], preferred_element_type=F32) + b_ref[...]
    o_ref[...] = y * jax.nn.sigmoid(gate)


def s5_out(yf, yb, u, d, w_glu, b_glu):
    t = u.shape[0]
    tm = min(ROW_TILE, t)
    row = pl.BlockSpec((tm, S5_COLS), lambda i: (i, 0))
    vec = pl.BlockSpec((1, S5_COLS), lambda i: (0, 0))
    return pl.pallas_call(
        _s5_out_body,
        grid=(t // tm,),
        in_specs=[row, row, row, vec, pl.BlockSpec((S5_COLS, S5_COLS), lambda i: (0, 0)), vec],
        out_specs=row,
        out_shape=_sds((t, S5_COLS)),
        compiler_params=_cp("parallel"),
        name="s5_out",
    )(yf, yb, u, d, w_glu, b_glu)


def s5_mixer(z_s5, tabs, d, w_glu, b_glu, batch, seq):
    assert batch == 4, "the scan packs 4 sequences x 2 directions onto the 8 sublanes"
    u = z_s5.reshape(batch, seq, S5_COLS)
    u8 = jnp.concatenate([u.transpose(1, 0, 2), u[:, ::-1].transpose(1, 0, 2)], axis=1)
    y8 = s5_scan(u8.reshape(seq * SUBLANES, S5_COLS), *tabs, seq).reshape(seq, SUBLANES, S5_COLS)
    yf = y8[:, :4].transpose(1, 0, 2).reshape(batch * seq, S5_COLS)
    yb = y8[::-1, 4:].transpose(1, 0, 2).reshape(batch * seq, S5_COLS)
    return s5_out(yf, yb, z_s5, d, w_glu, b_glu)


def _prep_body(z_ref, qg_ref, kg_ref, c_ref, s_ref, q_ref, k_ref, v_ref, *, nq, nk, nv, rope):
    def rot(x):
        if not rope:
            return x
        half = HEAD_DIM // 2
        swapped = jnp.concatenate([x[:, half:], x[:, :half]], axis=1)
        return x * c_ref[...] + swapped * s_ref[...]

    qg = qg_ref[...]
    kg = kg_ref[...]
    for h in range(nq):
        x = z_ref[:, h * HEAD_DIM:(h + 1) * HEAD_DIM]
        q_ref[0, h] = (rot(_rms(x, qg)) * ATTN_SCALE).astype(BF16)
    for h in range(nk):
        x = z_ref[:, (nq + h) * HEAD_DIM:(nq + h + 1) * HEAD_DIM]
        k_ref[0, h] = rot(_rms(x, kg)).astype(BF16)
    for h in range(nv):
        lo = (nq + nk + h) * HEAD_DIM
        v_ref[0, h] = z_ref[:, lo:lo + HEAD_DIM].astype(BF16)


def attn_prep(z, qg, kg, cos_t, sin_t, batch, seq, nq, nk, nv, rope):
    width = (nq + nk + nv) * HEAD_DIM
    tm = min(ROW_TILE, seq)
    nb = seq // tm
    vec = pl.BlockSpec((1, HEAD_DIM), lambda b, i: (0, 0))
    tab = pl.BlockSpec((tm, HEAD_DIM), lambda b, i: (i, 0))
    heads = lambda n: pl.BlockSpec((1, n, tm, HEAD_DIM), lambda b, i: (b, 0, i, 0))
    return pl.pallas_call(
        functools.partial(_prep_body, nq=nq, nk=nk, nv=nv, rope=rope),
        grid=(batch, nb),
        in_specs=[pl.BlockSpec((tm, width), lambda b, i: (b * nb + i, 0)), vec, vec, tab, tab],
        out_specs=[heads(nq), heads(nk), heads(nv)],
        out_shape=[_sds((batch, n, seq, HEAD_DIM), BF16) for n in (nq, nk, nv)],
        compiler_params=_cp("parallel", "parallel"),
        name="attn_prep",
    )(z, qg, kg, cos_t, sin_t)


def _flash_body(q_ref, k_ref, v_ref, *rest, has_bias):
    if has_bias:
        b_ref, o_ref, m_ref, l_ref, acc_ref = rest
    else:
        o_ref, m_ref, l_ref, acc_ref = rest
    j = pl.program_id(3)

    @pl.when(j == 0)
    def _():
        m_ref[...] = jnp.full_like(m_ref, -jnp.inf)
        l_ref[...] = jnp.zeros_like(l_ref)
        acc_ref[...] = jnp.zeros_like(acc_ref)

    s = lax.dot_general(q_ref[0, 0], k_ref[0, 0], (((1,), (1,)), ((), ())), preferred_element_type=F32)
    if has_bias:
        s = s + b_ref[0, 0]
    m_prev = m_ref[...]
    m_new = jnp.maximum(m_prev, jnp.max(s, axis=-1, keepdims=True))
    alpha = jnp.exp(m_prev - m_new)
    p = jnp.exp(s - m_new)
    l_ref[...] = alpha * l_ref[...] + jnp.sum(p, axis=-1, keepdims=True)
    acc_ref[...] = alpha * acc_ref[...] + jnp.dot(p.astype(BF16), v_ref[0, 0], preferred_element_type=F32)
    m_ref[...] = m_new

    @pl.when(j == pl.num_programs(3) - 1)
    def _():
        o_ref[0, 0] = acc_ref[...] / l_ref[...]


def flash_attention(q, k, v, k_of_q, v_of_q, bias=None):
    batch, nq, seq, _ = q.shape
    t = min(ATTN_TILE, seq)
    nb = seq // t
    in_specs = [pl.BlockSpec((1, 1, t, HEAD_DIM), lambda b, h, i, j: (b, h, i, 0)),
                pl.BlockSpec((1, 1, t, HEAD_DIM), lambda b, h, i, j: (b, k_of_q(h), j, 0)),
                pl.BlockSpec((1, 1, t, HEAD_DIM), lambda b, h, i, j: (b, v_of_q(h), j, 0))]
    args = [q, k, v]
    if bias is not None:
        in_specs.append(pl.BlockSpec(
            (1, 1, t, t), lambda b, h, i, j: (v_of_q(h), jnp.clip(j - i, -2, 2) + 2, 0, 0)))
        args.append(bias)
    return pl.pallas_call(
        functools.partial(_flash_body, has_bias=bias is not None),
        grid=(batch, nq, nb, nb),
        in_specs=in_specs,
        out_specs=pl.BlockSpec((1, 1, t, HEAD_DIM), lambda b, h, i, j: (b, h, i, 0)),
        out_shape=_sds((batch, nq, seq, HEAD_DIM)),
        scratch_shapes=[pltpu.VMEM((t, 1), F32), pltpu.VMEM((t, 1), F32), pltpu.VMEM((t, HEAD_DIM), F32)],
        compiler_params=_cp("parallel", "parallel", "parallel", "arbitrary"),
        name="flash_bias" if bias is not None else "flash",
    )(*args)


def _t5_bucket_table(t):
    rel = np.arange(-(3 * t - 1), 3 * t)
    half = NUM_BUCKETS // 2
    max_exact = half // 2
    ret = np.where(rel > 0, half, 0)
    n = np.abs(rel)
    nf = np.maximum(n, 1).astype(np.float64)
    large = max_exact + (np.log(nf / max_exact) / math.log(MAX_DISTANCE / max_exact) * (half - max_exact)).astype(np.int64)
    large = np.minimum(large, half - 1)
    return ret + np.where(n < max_exact, n, large)


def diff_bias_tiles(rel_bias, t):
    assert t >= MAX_DISTANCE, "offsets of two or more tiles must lie in the saturated buckets"
    buckets = _t5_bucket_table(t)
    r = np.arange(t)
    idx = np.stack([buckets[(d * t + r[None, :] - r[:, None]) + (3 * t - 1)] for d in range(-2, 3)])
    return jnp.transpose(rel_bias[jnp.asarray(idx, dtype=I32)], (3, 0, 1, 2)).astype(F32)


def rope_tables(seq):
    rows = seq // GRID_W
    row = jnp.repeat(jnp.arange(rows, dtype=F32), GRID_W)
    col = jnp.tile(jnp.arange(GRID_W, dtype=F32), rows)
    freq = ROPE_THETA ** (-jnp.arange(0, ROPE_AXIS_DIM, 2, dtype=F32) / ROPE_AXIS_DIM)
    ang = jnp.concatenate([row[:, None] * freq, col[:, None] * freq], -1)
    cos, sin = jnp.cos(ang), jnp.sin(ang)
    return jnp.concatenate([cos, cos], -1), jnp.concatenate([-sin, sin], -1)


def _conv3_body(x0, x1, x2, w0, w1, w2, b0, b1, b2, o0, o1, o2):
    for x_ref, w_ref, b_ref, o_ref in ((x0, w0, b0, o0), (x1, w1, b1, o1), (x2, w2, b2, o2)):
        x = x_ref[0]
        n = x.shape[0]
        t = lax.broadcasted_iota(I32, (n, 1), 0)
        prev = jnp.where(t == 0, 0.0, pltpu.roll(x, 1, 0))
        nxt = jnp.where(t == n - 1, 0.0, pltpu.roll(x, n - 1, 0))
        o_ref[0] = w_ref[0:1] * prev + w_ref[1:2] * x + w_ref[2:3] * nxt + b_ref[...]


def hyena_conv3(z_hy, conv_w, conv_b, batch, seq):
    z = z_hy.reshape(batch, seq, HY_COLS)
    nc = HY_WIDTH // LANES
    x_spec = lambda g: pl.BlockSpec((1, seq, LANES), lambda b, c: (b, 0, g * nc + c))
    w_spec = lambda g: pl.BlockSpec((3, LANES), lambda b, c: (0, g * nc + c))
    b_spec = lambda g: pl.BlockSpec((1, LANES), lambda b, c: (0, g * nc + c))
    out = pl.BlockSpec((1, seq, LANES), lambda b, c: (b, 0, c))
    return pl.pallas_call(
        _conv3_body,
        grid=(batch, nc),
        in_specs=[x_spec(0), x_spec(1), x_spec(2), w_spec(0), w_spec(1), w_spec(2),
                  b_spec(0), b_spec(1), b_spec(2)],
        out_specs=[out, out, out],
        out_shape=[_sds((batch, seq, HY_WIDTH))] * 3,
        compiler_params=_cp("parallel", "parallel"),
        name="hyena_conv3",
    )(z, z, z, conv_w, conv_w, conv_w, conv_b, conv_b, conv_b)


def _filt_body(feat_ref, dist_ref, w1, b1, f1, w2, b2, f2, w3, dec, h_ref, s_ref):
    dot = functools.partial(jnp.dot, precision=HI, preferred_element_type=F32)
    hid = jnp.sin(f1[...] * (dot(feat_ref[...], w1[...]) + b1[...]))
    hid = jnp.sin(f2[...] * (dot(hid, w2[...]) + b2[...]))
    h = dot(hid, w3[...]) * jnp.exp(-dist_ref[...] * jnp.abs(dec[...]))
    h_ref[...] = h

    @pl.when(pl.program_id(0) == 0)
    def _():
        s_ref[...] = jnp.zeros_like(s_ref)

    s_ref[...] += jnp.sum(jnp.abs(h), axis=0, keepdims=True)


def hyena_filter(seq, w1, b1, f1, w2, b2, f2, w3, decay):
    j = jnp.arange(seq, dtype=F32)
    tt = j / seq
    ang = 2.0 * jnp.pi * tt[:, None] * jnp.arange(1, HY_BANDS + 1, dtype=F32)
    feat = jnp.concatenate([tt[:, None], jnp.cos(ang), jnp.sin(ang)], -1)
    feat = jnp.pad(feat, ((0, 0), (0, LANES - HY_EMB)))
    half = seq // 2
    dist = (jnp.abs(j - half) / half)[:, None]
    w1p = jnp.pad(w1, ((0, LANES - HY_EMB), (0, 0)))
    tm = min(1024, seq)
    hw = HY_ORDER * HY_WIDTH
    const = lambda i: (0, 0)
    return pl.pallas_call(
        _filt_body,
        grid=(seq // tm,),
        in_specs=[pl.BlockSpec((tm, LANES), lambda i: (i, 0)), pl.BlockSpec((tm, 1), lambda i: (i, 0)),
                  pl.BlockSpec((LANES, HY_FFN), const), pl.BlockSpec((1, HY_FFN), const),
                  pl.BlockSpec((1, HY_FFN), const), pl.BlockSpec((HY_FFN, HY_FFN), const),
                  pl.BlockSpec((1, HY_FFN), const), pl.BlockSpec((1, HY_FFN), const),
                  pl.BlockSpec((HY_FFN, hw), const), pl.BlockSpec((1, hw), const)],
        out_specs=[pl.BlockSpec((tm, hw), lambda i: (i, 0)), pl.BlockSpec((1, hw), const)],
        out_shape=[_sds((seq, hw)), _sds((1, hw))],
        compiler_params=_cp("arbitrary"),
        name="hyena_filter",
    )(feat, dist, w1p, b1[None], f1[None], w2, b2[None], f2[None], w3, decay.reshape(1, hw))


def dft_tables(seq):
    n = 2 * seq
    n2 = FFT_N2
    n1 = n // n2
    n1h = n1 // 2
    k1 = jnp.arange(n1, dtype=I32)
    ang = (2.0 * jnp.pi / n1) * ((k1[:, None] * jnp.arange(n1h, dtype=I32)[None, :]) % n1).astype(F32)
    outer_fwd = jnp.concatenate([jnp.cos(ang), -jnp.sin(ang)], axis=0)
    t1 = jnp.arange(n1h, dtype=I32) + n1 // 4
    ang = (2.0 * jnp.pi / n1) * ((t1[:, None] * k1[None, :]) % n1).astype(F32)
    outer_inv = jnp.concatenate([jnp.cos(ang), -jnp.sin(ang)], axis=1) / n
    k2 = jnp.arange(n2, dtype=I32)
    t2 = jnp.arange(n2, dtype=I32)
    phase = (t2[None, None, :] * (k2[None, :, None] * n1 + k1[:, None, None])) % n
    ang = (2.0 * jnp.pi / n) * phase.astype(F32)
    fr, fi = jnp.cos(ang), -jnp.sin(ang)
    inner = jnp.concatenate([jnp.concatenate([fr, -fi], axis=2),
                             jnp.concatenate([fi, fr], axis=2)], axis=1)
    return outer_fwd, outer_inv, inner, jnp.swapaxes(inner, 1, 2)


def _fft_outer_body(f_ref, x_ref, o_ref):
    o_ref[0] = jnp.dot(f_ref[...], x_ref[0], precision=HI, preferred_element_type=F32)


def fft_outer(x, table):
    batch, n1h, cols = x.shape
    rows = table.shape[0]
    tc = min(FFT_COL_TILE, cols)
    return pl.pallas_call(
        _fft_outer_body,
        grid=(batch, cols // tc),
        in_specs=[pl.BlockSpec((rows, n1h), lambda b, c: (0, 0)),
                  pl.BlockSpec((1, n1h, tc), lambda b, c: (b, 0, c))],
        out_specs=pl.BlockSpec((1, rows, tc), lambda b, c: (b, 0, c)),
        out_shape=_sds((batch, rows, cols)),
        compiler_params=_cp("parallel", "parallel"),
        name="fft_outer",
    )(table, x)


def _fft_filter_body(m_ref, s_ref, a_ref, o_ref):
    n2 = FFT_N2
    a = a_ref[0, :, 0].reshape(2 * n2, HY_WIDTH)
    x = jnp.dot(m_ref[0], a, precision=HI, preferred_element_type=F32) * (1.0 / s_ref[...])
    o_ref[:, 0] = x.reshape(2, n2, HY_WIDTH)


def fft_filter(a5, inner, s):
    _, _, n1, n2, ch = a5.shape
    return pl.pallas_call(
        _fft_filter_body,
        grid=(n1, ch // HY_WIDTH),
        in_specs=[pl.BlockSpec((1, 2 * n2, 2 * n2), lambda k, c: (k, 0, 0)),
                  pl.BlockSpec((1, HY_WIDTH), lambda k, c: (0, c)),
                  pl.BlockSpec((1, 2, 1, n2, HY_WIDTH), lambda k, c: (0, 0, k, 0, c))],
        out_specs=pl.BlockSpec((2, 1, n2, HY_WIDTH), lambda k, c: (0, k, 0, c)),
        out_shape=_sds((2, n1, n2, ch)),
        compiler_params=_cp("parallel", "parallel"),
        name="fft_filter",
    )(inner, s, a5)


def _fft_mid_body(m_ref, mt_ref, h_ref, a_ref, o_ref, *, nb):
    n2 = FFT_N2
    dot = functools.partial(jnp.dot, precision=HI, preferred_element_type=F32)
    m = m_ref[0]
    mt = mt_ref[0]
    hr = h_ref[0, 0]
    hi = h_ref[1, 0]
    for b in range(nb):
        x = dot(m, a_ref[b, :, 0].reshape(2 * n2, HY_WIDTH))
        xr, xi = x[:n2], x[n2:]
        y = jnp.concatenate([xr * hr - xi * hi, xr * hi + xi * hr], axis=0)
        o_ref[b, :, 0] = dot(mt, y).reshape(2, n2, HY_WIDTH)


def fft_mid(a5, inner, inner_t, hf, order):
    nb, _, n1, n2, ch = a5.shape
    mat = pl.BlockSpec((1, 2 * n2, 2 * n2), lambda k: (k, 0, 0))
    blk = pl.BlockSpec((nb, 2, 1, n2, ch), lambda k: (0, 0, k, 0, 0))
    return pl.pallas_call(
        functools.partial(_fft_mid_body, nb=nb),
        grid=(n1,),
        in_specs=[mat, mat, pl.BlockSpec((2, 1, n2, ch), lambda k: (0, k, 0, order)), blk],
        out_specs=blk,
        out_shape=_sds(a5.shape),
        compiler_params=_cp("parallel"),
        name="fft_mid",
    )(inner, inner_t, hf, a5)


def _fft_inv_body(g_ref, b_ref, gate_ref, y_ref, fb_ref, o_ref):
    c = jnp.dot(g_ref[...], b_ref[0], precision=HI, preferred_element_type=F32)
    o_ref[0] = gate_ref[0] * (c + fb_ref[...] * y_ref[0])


def fft_inv_gate(bmat, table, gate, y, fbias):
    batch, rows, cols = bmat.shape
    n1h = table.shape[0]
    tc = min(FFT_COL_TILE, cols)
    fb = jnp.tile(fbias.reshape(1, HY_WIDTH), (1, tc // HY_WIDTH))
    blk = pl.BlockSpec((1, n1h, tc), lambda b, c: (b, 0, c))
    return pl.pallas_call(
        _fft_inv_body,
        grid=(batch, cols // tc),
        in_specs=[pl.BlockSpec((n1h, rows), lambda b, c: (0, 0)),
                  pl.BlockSpec((1, rows, tc), lambda b, c: (b, 0, c)), blk, blk,
                  pl.BlockSpec((1, tc), lambda b, c: (0, 0))],
        out_specs=blk,
        out_shape=_sds((batch, n1h, cols)),
        compiler_params=_cp("parallel", "parallel"),
        name="fft_inv_gate",
    )(table, bmat, gate, y, fb)


def hyena_spectrum(seq, tabs, w1, b1, f1, w2, b2, f2, w3, decay):
    outer_fwd, _, inner, _ = tabs
    n1 = outer_fwd.shape[0] // 2
    hw = HY_ORDER * HY_WIDTH
    h, s = hyena_filter(seq, w1, b1, f1, w2, b2, f2, w3, decay)
    a = fft_outer(h.reshape(1, n1 // 2, FFT_N2 * hw), outer_fwd)
    return fft_filter(a.reshape(1, 2, n1, FFT_N2, hw), inner, s)


def hyena_mixer(z_hy, conv_w, conv_b, hf, f_bias, tabs, batch, seq):
    outer_fwd, outer_inv, inner, inner_t = tabs
    n1 = outer_fwd.shape[0] // 2
    cols = FFT_N2 * HY_WIDTH
    v, g0, g1 = hyena_conv3(z_hy, conv_w, conv_b, batch, seq)
    y = v.reshape(batch, n1 // 2, cols)
    for order, gate in enumerate((g0, g1)):
        a = fft_outer(y, outer_fwd).reshape(batch, 2, n1, FFT_N2, HY_WIDTH)
        bm = fft_mid(a, inner, inner_t, hf, order).reshape(batch, 2 * n1, cols)
        y = fft_inv_gate(bm, outer_inv, gate.reshape(batch, n1 // 2, cols), y, f_bias[order])
    return y.reshape(batch * seq, HY_WIDTH)


def _mix_body(lam_ref, h_ref, s5_ref, gqa_ref, diff_ref, hy_ref, gs5, ggqa, gsub, ghy, w_ref, o_ref, *, keep):
    lam = lam_ref[0]
    a = _rms(s5_ref[...], gs5[...])
    b = _rms(jnp.concatenate([gqa_ref[0, h] for h in range(GQA_HEADS)], axis=1), ggqa[...])
    c = jnp.concatenate(
        [_rms(diff_ref[0, 2 * h] - lam * diff_ref[0, 2 * h + 1], gsub[...]) * keep for h in range(DIFF_HEADS)],
        axis=1)
    d = _rms(hy_ref[...], ghy[...])
    mixed = jnp.concatenate([a, b, c, d], axis=1).astype(BF16)
    o_ref[...] = h_ref[...] + jnp.dot(mixed, w_ref[...], preferred_element_type=F32)


def mix_out(lam, h, y_s5, o_gqa, o_diff, y_hy, gs5, ggqa, gsub, ghy, w_out, batch, seq, lambda_init):
    tm = min(ROW_TILE, seq)
    nb = seq // tm
    row = lambda w: pl.BlockSpec((tm, w), lambda b, i: (b * nb + i, 0))
    vec = lambda w: pl.BlockSpec((1, w), lambda b, i: (0, 0))
    heads = lambda n: pl.BlockSpec((1, n, tm, HEAD_DIM), lambda b, i: (b, 0, i, 0))
    return pl.pallas_call(
        functools.partial(_mix_body, keep=1.0 - lambda_init),
        grid=(batch, nb),
        in_specs=[pl.BlockSpec(memory_space=pltpu.SMEM), row(D_MODEL), row(GROUP_WIDTH), heads(GQA_HEADS),
                  heads(2 * DIFF_HEADS), row(GROUP_WIDTH), vec(GROUP_WIDTH), vec(GROUP_WIDTH), vec(HEAD_DIM),
                  vec(GROUP_WIDTH), pl.BlockSpec((D_MODEL, D_MODEL), lambda b, i: (0, 0))],
        out_specs=row(D_MODEL),
        out_shape=_sds((batch * seq, D_MODEL)),
        compiler_params=_cp("parallel", "parallel"),
        name="mix_out",
    )(lam, h, y_s5, o_gqa, o_diff, y_hy, gs5, ggqa, gsub, ghy, w_out)


def _router_body(h_ref, g_ref, w_ref, b_ref, xn_ref, ti_ref, tw_ref, cnt_ref):
    xn = _rms(h_ref[...], g_ref[...])
    xn_ref[...] = xn
    logits = jnp.dot(xn, w_ref[...], precision=HI, preferred_element_type=F32) + b_ref[...]
    lane = lax.broadcasted_iota(I32, logits.shape, 1)
    vals = logits
    tops, idxs = [], []
    hot = jnp.zeros(logits.shape, F32)
    for _ in range(TOP_K):
        m = jnp.max(vals, axis=-1, keepdims=True)
        idx = jnp.min(jnp.where(vals == m, lane, N_EXPERTS), axis=-1, keepdims=True)
        sel = lane == idx
        tops.append(m)
        idxs.append(idx)
        hot = hot + sel.astype(F32)
        vals = jnp.where(sel, -jnp.inf, vals)
    es = [jnp.exp(t - tops[0]) for t in tops]
    den = es[0] + es[1] + es[2] + es[3]
    ti_ref[...] = jnp.concatenate(idxs, axis=1)
    tw_ref[...] = jnp.concatenate([e / den for e in es], axis=1)

    @pl.when(pl.program_id(0) == 0)
    def _():
        cnt_ref[...] = jnp.zeros_like(cnt_ref)

    cnt_ref[...] += jnp.sum(hot, axis=0, keepdims=True)


def moe_router(h, g, w_router, b_router):
    t = h.shape[0]
    tm = min(ROW_TILE, t)
    const = lambda i: (0, 0)
    return pl.pallas_call(
        _router_body,
        grid=(t // tm,),
        in_specs=[pl.BlockSpec((tm, D_MODEL), lambda i: (i, 0)), pl.BlockSpec((1, D_MODEL), const),
                  pl.BlockSpec((D_MODEL, N_EXPERTS), const), pl.BlockSpec((1, N_EXPERTS), const)],
        out_specs=[pl.BlockSpec((tm, D_MODEL), lambda i: (i, 0)), pl.BlockSpec((tm, TOP_K), lambda i: (i, 0)),
                   pl.BlockSpec((tm, TOP_K), lambda i: (i, 0)), pl.BlockSpec((1, N_EXPERTS), const)],
        out_shape=[_sds((t, D_MODEL)), _sds((t, TOP_K), I32), _sds((t, TOP_K)), _sds((1, N_EXPERTS))],
        compiler_params=_cp("arbitrary"),
        name="moe_router",
    )(h, g, w_router, b_router)


def _rank_body(ti_ref, off_ref, pos_ref, carry_ref):
    @pl.when(pl.program_id(0) == 0)
    def _():
        carry_ref[...] = jnp.zeros_like(carry_ref)

    ti = ti_ref[...]
    tm = ti.shape[0]
    lane = lax.broadcasted_iota(I32, (tm, N_EXPERTS), 1)
    hots = [lane == ti[:, k:k + 1] for k in range(TOP_K)]
    hot = sum(h.astype(F32) for h in hots)
    r = lax.broadcasted_iota(I32, (tm, tm), 0)
    c = lax.broadcasted_iota(I32, (tm, tm), 1)
    below = jnp.where(r > c, 1.0, 0.0).astype(BF16)
    before = jnp.dot(below, hot.astype(BF16), preferred_element_type=F32) + carry_ref[...] + off_ref[...]
    pos = [jnp.sum(jnp.where(h, before, 0.0), axis=-1, keepdims=True) for h in hots]
    pos_ref[...] = jnp.concatenate(pos, axis=1).astype(I32)
    carry_ref[...] += jnp.sum(hot, axis=0, keepdims=True)


def moe_rank(top_i, offsets):
    t = top_i.shape[0]
    tm = min(ROW_TILE, t)
    return pl.pallas_call(
        _rank_body,
        grid=(t // tm,),
        in_specs=[pl.BlockSpec((tm, TOP_K), lambda i: (i, 0)), pl.BlockSpec((1, N_EXPERTS), lambda i: (0, 0))],
        out_specs=pl.BlockSpec((tm, TOP_K), lambda i: (i, 0)),
        out_shape=_sds((t, TOP_K), I32),
        scratch_shapes=[pltpu.VMEM((1, N_EXPERTS), F32)],
        compiler_params=_cp("arbitrary"),
        name="moe_rank",
    )(top_i, offsets)


def _row_copy(src, dst, sem):
    return pltpu.make_async_copy(src, dst, sem)


def _scatter_body(pos_ref, x_hbm, zero_hbm, xs_hbm, sem, *, tm):
    del zero_hbm
    base = pl.program_id(0) * tm

    def start(r, c):
        for k in range(TOP_K):
            _row_copy(x_hbm.at[base + r], xs_hbm.at[pos_ref[r * TOP_K + k]], sem).start()
        return c

    lax.fori_loop(0, tm, start, 0)

    def wait(r, c):
        for k in range(TOP_K):
            _row_copy(x_hbm.at[base + r], xs_hbm.at[pos_ref[r * TOP_K + k]], sem).wait()
        return c

    lax.fori_loop(0, tm, wait, 0)


def moe_scatter(pos_flat, x3, n_slots):
    t = x3.shape[0]
    tm = min(MOE_TOK_TILE, t)
    zeros = jnp.zeros((n_slots, ROW_TILES, LANES), F32)
    return pl.pallas_call(
        functools.partial(_scatter_body, tm=tm),
        grid=(t // tm,),
        in_specs=[pl.BlockSpec((tm * TOP_K,), lambda i: (i,), memory_space=pltpu.SMEM),
                  pl.BlockSpec(memory_space=pl.ANY), pl.BlockSpec(memory_space=pl.ANY)],
        out_specs=pl.BlockSpec(memory_space=pl.ANY),
        out_shape=_sds((n_slots, ROW_TILES, LANES)),
        scratch_shapes=[pltpu.SemaphoreType.DMA(())],
        input_output_aliases={2: 0},
        compiler_params=_cp("arbitrary"),
        name="moe_scatter",
    )(pos_flat, x3, zeros)


def _experts_body(te_ref, nv_ref, x_ref, wgu_ref, bgu_ref, wd_ref, bd_ref, o_ref):
    del te_ref
    j = pl.program_id(0)
    dff = wd_ref.shape[1]

    @pl.when(j < nv_ref[0])
    def _():
        x = jnp.concatenate([x_ref[:, s, :] for s in range(ROW_TILES)], axis=1).astype(BF16)
        hgu = jnp.dot(x, wgu_ref[0], preferred_element_type=F32) + bgu_ref[0]
        glu = jnp.minimum(hgu[:, :dff], SWIGLU_LIMIT)
        lin = jnp.clip(hgu[:, dff:], -SWIGLU_LIMIT, SWIGLU_LIMIT)
        act = glu * jax.nn.sigmoid(SWIGLU_ALPHA * glu) * (lin + 1.0)
        y = jnp.dot(act.astype(BF16), wd_ref[0], preferred_element_type=F32) + bd_ref[0]
        for s in range(ROW_TILES):
            o_ref[:, s, :] = y[:, s * LANES:(s + 1) * LANES]

    @pl.when(j >= nv_ref[0])
    def _():
        o_ref[...] = jnp.zeros_like(o_ref)


def moe_experts(tile_expert, n_valid, xs3, w_gu, b_gu, w_down, b_down):
    n_slots = xs3.shape[0]
    tm = MOE_TILE
    dff = w_down.shape[1]
    blk = pl.BlockSpec((tm, ROW_TILES, LANES), lambda j, te, nv: (j, 0, 0))
    return pl.pallas_call(
        _experts_body,
        grid_spec=pltpu.PrefetchScalarGridSpec(
            num_scalar_prefetch=2,
            grid=(n_slots // tm,),
            in_specs=[blk,
                      pl.BlockSpec((1, D_MODEL, 2 * dff), lambda j, te, nv: (te[j], 0, 0)),
                      pl.BlockSpec((1, 1, 2 * dff), lambda j, te, nv: (te[j], 0, 0)),
                      pl.BlockSpec((1, dff, D_MODEL), lambda j, te, nv: (te[j], 0, 0)),
                      pl.BlockSpec((1, 1, D_MODEL), lambda j, te, nv: (te[j], 0, 0))],
            out_specs=blk),
        out_shape=_sds((n_slots, ROW_TILES, LANES)),
        compiler_params=_cp("arbitrary"),
        name="moe_experts",
    )(tile_expert, n_valid, xs3, w_gu, b_gu, w_down, b_down)


def _combine_body(pos_ref, w_ref, ys_hbm, o_ref, b0, b1, b2, b3, sem, *, tm):
    bufs = (b0, b1, b2, b3)

    def start(r, c):
        for k in range(TOP_K):
            _row_copy(ys_hbm.at[pos_ref[r * TOP_K + k]], bufs[k].at[r], sem).start()
        return c

    lax.fori_loop(0, tm, start, 0)

    def wait(r, c):
        for k in range(TOP_K):
            _row_copy(ys_hbm.at[pos_ref[r * TOP_K + k]], bufs[k].at[r], sem).wait()
        return c

    lax.fori_loop(0, tm, wait, 0)

    def mix(r, c):
        acc = w_ref[r * TOP_K] * b0[r]
        for k in range(1, TOP_K):
            acc = acc + w_ref[r * TOP_K + k] * bufs[k][r]
        o_ref[r] = acc
        return c

    lax.fori_loop(0, tm, mix, 0)


def moe_combine(pos_flat, w_flat, ys3, t):
    tm = min(MOE_TOK_TILE, t)
    smem = pl.BlockSpec((tm * TOP_K,), lambda i: (i,), memory_space=pltpu.SMEM)
    return pl.pallas_call(
        functools.partial(_combine_body, tm=tm),
        grid=(t // tm,),
        in_specs=[smem, smem, pl.BlockSpec(memory_space=pl.ANY)],
        out_specs=pl.BlockSpec((tm, ROW_TILES, LANES), lambda i: (i, 0, 0)),
        out_shape=_sds((t, ROW_TILES, LANES)),
        scratch_shapes=[pltpu.VMEM((tm, ROW_TILES, LANES), F32)] * TOP_K + [pltpu.SemaphoreType.DMA(())],
        compiler_params=_cp("arbitrary"),
        name="moe_combine",
    )(pos_flat, w_flat, ys3)


def moe(h, g, w_router, b_router, w_gu, b_gu, w_down, b_down):
    t = h.shape[0]
    xn, top_i, top_w, counts = moe_router(h, g, w_router, b_router)
    n_tiles = (t * TOP_K) // MOE_TILE + N_EXPERTS
    padded = jnp.ceil(counts[0] / MOE_TILE) * MOE_TILE
    ends = jnp.cumsum(padded)
    offsets = (ends - padded)[None]
    tile_start = jnp.arange(n_tiles, dtype=F32) * MOE_TILE
    tile_expert = jnp.minimum(jnp.sum(ends[None, :] <= tile_start[:, None], axis=1), N_EXPERTS - 1).astype(I32)
    n_valid = (ends[-1:] / MOE_TILE).astype(I32)
    pos = moe_rank(top_i, offsets).reshape(t * TOP_K)
    xs3 = moe_scatter(pos, xn.reshape(t, ROW_TILES, LANES), n_tiles * MOE_TILE)
    ys3 = moe_experts(tile_expert, n_valid, xs3, w_gu, b_gu, w_down, b_down)
    return moe_combine(pos, top_w.reshape(t * TOP_K), ys3, t).reshape(t, D_MODEL)


def _ple_body(h_ref, m_ref, p_ref, wp_ref, wg_ref, o_ref):
    h = h_ref[...] + m_ref[...]
    e = jnp.dot(p_ref[...].astype(BF16), wp_ref[...], preferred_element_type=F32)
    gate = jnp.dot(_rms(h).astype(BF16), wg_ref[...], preferred_element_type=F32)
    o_ref[...] = h + e * jax.nn.sigmoid(gate)


def ple(h, moe_out, p, w_ple, w_gate):
    t = h.shape[0]
    tm = min(ROW_TILE, t)
    row = pl.BlockSpec((tm, D_MODEL), lambda i: (i, 0))
    return pl.pallas_call(
        _ple_body,
        grid=(t // tm,),
        in_specs=[row, row, pl.BlockSpec((tm, PLE_DIM), lambda i: (i, 0)),
                  pl.BlockSpec((PLE_DIM, D_MODEL), lambda i: (0, 0)),
                  pl.BlockSpec((D_MODEL, D_MODEL), lambda i: (0, 0))],
        out_specs=row,
        out_shape=_sds((t, D_MODEL)),
        compiler_params=_cp("parallel"),
        name="ple",
    )(h, moe_out, p, w_ple, w_gate)


def _deinterleave(n):
    return np.concatenate([np.arange(0, n, 2), np.arange(1, n, 2)])


def _halves(w):
    pairs = w.reshape(*w.shape[:-1], w.shape[-1] // 2, 2)
    return jnp.concatenate([pairs[..., 0], pairs[..., 1]], axis=-1)


def prepare_layer(i, prm):
    pair = _deinterleave(HEAD_DIM)
    cols = np.arange(IN_COLS)
    for h in range(GQA_HEADS + GQA_KV_HEADS):
        lo = OFF_GQA + h * HEAD_DIM
        cols[lo:lo + HEAD_DIM] = lo + pair
    lam = (jnp.exp(jnp.sum(prm["diff_lam_q1"][i] * prm["diff_lam_k1"][i]))
           - jnp.exp(jnp.sum(prm["diff_lam_q2"][i] * prm["diff_lam_k2"][i])) + _lambda_init(i))
    return dict(
        norm1_g=prm["norm1_g"][i][None],
        w_in=prm["w_in"][i][:, cols].astype(BF16),
        s5=s5_tables(prm["s5_lam_re"][i], prm["s5_lam_im"][i], prm["s5_log_dt"][i], prm["s5_b_re"][i],
                     prm["s5_b_im"][i], prm["s5_c_re"][i], prm["s5_c_im"][i]),
        s5_d=prm["s5_d"][i][None],
        s5_w_glu=prm["s5_w_glu"][i].astype(BF16),
        s5_b_glu=prm["s5_b_glu"][i][None],
        gqa_q_g=prm["gqa_q_g"][i][pair][None],
        gqa_k_g=prm["gqa_k_g"][i][pair][None],
        diff_q_g=prm["diff_q_g"][i][None],
        diff_k_g=prm["diff_k_g"][i][None],
        diff_lam=lam.reshape(1).astype(F32),
        diff_subln_g=prm["diff_subln_g"][i][None],
        out_g_s5=prm["out_g_s5"][i][None],
        out_g_gqa=prm["out_g_gqa"][i][None],
        out_g_hy=prm["out_g_hy"][i][None],
        w_out=prm["w_out"][i].astype(BF16),
        norm2_g=prm["norm2_g"][i][None],
        w_router=prm["w_router"][i],
        b_router=prm["b_router"][i][None],
        w_gu=_halves(prm["w_gu"][i]).astype(BF16),
        b_gu=_halves(prm["b_gu"][i])[:, None, :],
        w_down=prm["w_down"][i].astype(BF16),
        b_down=prm["b_down"][i][:, None, :],
        w_ple=prm["w_ple"][i].astype(BF16),
        w_ple_gate=prm["w_ple_gate"][i].astype(BF16),
    )


def _lambda_init(i):
    return 0.8 - 0.6 * math.exp(-0.3 * i)


def run_trunk(x, p, prm, layers):
    batch, seq, _ = x.shape
    t = batch * seq
    cos_t, sin_t = rope_tables(seq)
    tabs = dft_tables(seq)
    bias = diff_bias_tiles(prm["rel_bias"], min(ATTN_TILE, seq))
    h = x.reshape(t, D_MODEL)
    for i, lw in enumerate(layers):
        z_s5, z_gqa, z_diff, z_hy = in_proj(h, lw["norm1_g"], lw["w_in"])
        y_s5 = s5_mixer(z_s5, lw["s5"], lw["s5_d"], lw["s5_w_glu"], lw["s5_b_glu"], batch, seq)
        q, k, v = attn_prep(z_gqa, lw["gqa_q_g"], lw["gqa_k_g"], cos_t, sin_t, batch, seq,
                            GQA_HEADS, GQA_KV_HEADS, GQA_KV_HEADS, True)
        rep = GQA_HEADS // GQA_KV_HEADS
        o_gqa = flash_attention(q, k, v, lambda hq: hq // rep, lambda hq: hq // rep)
        q, k, v = attn_prep(z_diff, lw["diff_q_g"], lw["diff_k_g"], cos_t, sin_t, batch, seq,
                            2 * DIFF_HEADS, 2 * DIFF_HEADS, DIFF_HEADS, False)
        o_diff = flash_attention(q, k, v, lambda hq: hq, lambda hq: hq // 2, bias)
        hf = hyena_spectrum(seq, tabs, prm["hy_w1"][i], prm["hy_b1"][i], prm["hy_freq1"][i], prm["hy_w2"][i],
                            prm["hy_b2"][i], prm["hy_freq2"][i], prm["hy_w3"][i], prm["hy_decay"][i])
        y_hy = hyena_mixer(z_hy, prm["hy_conv_w"][i], prm["hy_conv_b"][i][None], hf, prm["hy_bias"][i],
                           tabs, batch, seq)
        h = mix_out(lw["diff_lam"], h, y_s5, o_gqa, o_diff, y_hy, lw["out_g_s5"], lw["out_g_gqa"],
                    lw["diff_subln_g"], lw["out_g_hy"], lw["w_out"], batch, seq, _lambda_init(i))
        m = moe(h, lw["norm2_g"], lw["w_router"], lw["b_router"], lw["w_gu"], lw["b_gu"], lw["w_down"],
                lw["b_down"])
        h = ple(h, m, p[i].reshape(t, PLE_DIM), lw["w_ple"], lw["w_ple_gate"])
    return h.reshape(batch, seq, D_MODEL)


def kernel(x_prompt, x_sample, p_prompt, p_sample, rel_bias, norm1_g, w_in, s5_lam_re, s5_lam_im, s5_log_dt,
           s5_b_re, s5_b_im, s5_c_re, s5_c_im, s5_d, s5_w_glu, s5_b_glu, gqa_q_g, gqa_k_g, diff_q_g, diff_k_g,
           diff_lam_q1, diff_lam_k1, diff_lam_q2, diff_lam_k2, diff_subln_g, hy_conv_w, hy_conv_b, hy_w1, hy_b1,
           hy_freq1, hy_w2, hy_b2, hy_freq2, hy_w3, hy_decay, hy_bias, out_g_s5, out_g_gqa, out_g_hy, w_out,
           norm2_g, w_router, b_router, w_gu, b_gu, w_down, b_down, w_ple, w_ple_gate):
    prm = dict(rel_bias=rel_bias, norm1_g=norm1_g, w_in=w_in, s5_lam_re=s5_lam_re, s5_lam_im=s5_lam_im,
               s5_log_dt=s5_log_dt, s5_b_re=s5_b_re, s5_b_im=s5_b_im, s5_c_re=s5_c_re, s5_c_im=s5_c_im,
               s5_d=s5_d, s5_w_glu=s5_w_glu, s5_b_glu=s5_b_glu, gqa_q_g=gqa_q_g, gqa_k_g=gqa_k_g,
               diff_q_g=diff_q_g, diff_k_g=diff_k_g, diff_lam_q1=diff_lam_q1, diff_lam_k1=diff_lam_k1,
               diff_lam_q2=diff_lam_q2, diff_lam_k2=diff_lam_k2, diff_subln_g=diff_subln_g,
               hy_conv_w=hy_conv_w, hy_conv_b=hy_conv_b, hy_w1=hy_w1, hy_b1=hy_b1, hy_freq1=hy_freq1,
               hy_w2=hy_w2, hy_b2=hy_b2, hy_freq2=hy_freq2, hy_w3=hy_w3, hy_decay=hy_decay, hy_bias=hy_bias,
               out_g_s5=out_g_s5, out_g_gqa=out_g_gqa, out_g_hy=out_g_hy, w_out=w_out, norm2_g=norm2_g,
               w_router=w_router, b_router=b_router, w_gu=w_gu, b_gu=b_gu, w_down=w_down, b_down=b_down,
               w_ple=w_ple, w_ple_gate=w_ple_gate)
    layers = [prepare_layer(i, prm) for i in range(DEPTH)]
    return (run_trunk(x_prompt, p_prompt, prm, layers), run_trunk(x_sample, p_sample, prm, layers))
```

```python
import functools
import math

import numpy as np
import jax
import jax.numpy as jnp
from jax import lax
from jax.experimental import pallas as pl
from jax.experimental.pallas import tpu as pltpu

F32 = jnp.float32
BF16 = jnp.bfloat16
I32 = jnp.int32
HI = lax.Precision.HIGHEST

D_MODEL = 1024
DEPTH = 2
GROUP_WIDTH = 256
HEAD_DIM = 64
S5_GROUP_DIM = 16
S5_GROUPS = 16
S5_STATE = 64
S5_NSTATE = S5_GROUPS * S5_STATE
GQA_HEADS = 4
GQA_KV_HEADS = 2
DIFF_HEADS = 4
HY_WIDTH = 256
HY_ORDER = 2
HY_BANDS = 16
HY_EMB = 2 * HY_BANDS + 1
HY_FFN = 64
N_EXPERTS = 32
TOP_K = 4
SWIGLU_LIMIT = 7.0
SWIGLU_ALPHA = 1.702
NUM_BUCKETS = 32
MAX_DISTANCE = 128
GRID_W = 64
ROPE_THETA = 10000.0
ROPE_AXIS_DIM = HEAD_DIM // 2
PLE_DIM = 256
NORM_EPS = 1e-6
ATTN_SCALE = HEAD_DIM ** -0.5
LOG2E = math.log2(math.e)

S5_COLS = GROUP_WIDTH
GQA_QW = GQA_HEADS * HEAD_DIM
GQA_KW = GQA_KV_HEADS * HEAD_DIM
GQA_COLS = GQA_QW + 2 * GQA_KW
DIFF_QW = DIFF_HEADS * 2 * HEAD_DIM
DIFF_COLS = 2 * DIFF_QW + DIFF_HEADS * HEAD_DIM
HY_COLS = (HY_ORDER + 1) * HY_WIDTH
OFF_GQA = S5_COLS
OFF_DIFF = OFF_GQA + GQA_COLS
OFF_HY = OFF_DIFF + DIFF_COLS
IN_COLS = OFF_HY + HY_COLS
IN_SPLITS = ((0, OFF_GQA), (OFF_GQA, OFF_DIFF), (OFF_DIFF, OFF_HY), (OFF_HY, IN_COLS))

LANES = 128
SUBLANES = 8
ROW_TILES = D_MODEL // LANES
FFT_N2 = 128
VMEM_LIMIT = 48 * 1024 * 1024

ROW_TILE = 256
ATTN_Q_TILE = 1024
ATTN_K_TILE = 1024
ATTN_UNIT = 512
ATTN_SUB = 256
S5_TIME_BLOCK = 64
MOE_TILE = 512
MOE_TOK_TILE = 256
FFT_COL_TILE = 4096


def _cp(*sem):
    return pltpu.CompilerParams(dimension_semantics=sem, vmem_limit_bytes=VMEM_LIMIT)


def _sds(shape, dtype=F32):
    return jax.ShapeDtypeStruct(shape, dtype)


def _rms(x, g=None):
    y = x * lax.rsqrt(jnp.mean(x * x, axis=-1, keepdims=True) + NORM_EPS)
    return y if g is None else y * g


def _inproj_body(x_ref, g_ref, w_ref, *o_refs):
    u = _rms(x_ref[...], g_ref[...]).astype(BF16)
    for o_ref, (lo, hi) in zip(o_refs, IN_SPLITS):
        o_ref[...] = jnp.dot(u, w_ref[:, lo:hi], preferred_element_type=F32)


def in_proj(h, g, w):
    t = h.shape[0]
    tm = min(ROW_TILE, t)
    return pl.pallas_call(
        _inproj_body,
        grid=(t // tm,),
        in_specs=[pl.BlockSpec((tm, D_MODEL), lambda i: (i, 0)),
                  pl.BlockSpec((1, D_MODEL), lambda i: (0, 0)),
                  pl.BlockSpec((D_MODEL, IN_COLS), lambda i: (0, 0))],
        out_specs=[pl.BlockSpec((tm, hi - lo), lambda i: (i, 0)) for lo, hi in IN_SPLITS],
        out_shape=[_sds((t, hi - lo)) for lo, hi in IN_SPLITS],
        compiler_params=_cp("parallel"),
        name="in_proj",
    )(h, g, w)


def s5_tables(lam_re, lam_im, log_dt, b_re, b_im, c_re, c_im):
    lr = jnp.minimum(lam_re, -1e-4)
    li = lam_im
    dt = jnp.exp(log_dt)[..., None]
    mag = jnp.exp(lr * dt)
    ar = mag * jnp.cos(li * dt)
    ai = mag * jnp.sin(li * dt)
    den = lr * lr + li * li
    cr = ((ar - 1.0) * lr + ai * li) / den
    ci = (ai * lr - (ar - 1.0) * li) / den
    bbr = cr[..., None] * b_re - ci[..., None] * b_im
    bbi = cr[..., None] * b_im + ci[..., None] * b_re
    eye = jnp.eye(S5_GROUPS, dtype=F32)
    bb = jnp.stack([bbr, bbi], axis=1)
    wb = jnp.einsum("dpgnh,gk->dghpkn", bb, eye).reshape(2 * S5_COLS, 2 * S5_NSTATE)
    cc = jnp.stack([c_re, -c_im], axis=0)
    wc = jnp.einsum("pdghn,gk->pgndkh", cc, eye).reshape(2 * S5_NSTATE, 2 * S5_COLS)
    a_re = jnp.repeat(ar.reshape(2, S5_NSTATE), 4, axis=0)
    a_im = jnp.repeat(ai.reshape(2, S5_NSTATE), 4, axis=0)
    return wb.astype(BF16), wc.astype(BF16), a_re, a_im


def _s5_body(u_ref, wb_ref, wc_ref, ar_ref, ai_ref, y_ref, bu_ref, xr_ref, xi_ref, *, tb):
    @pl.when(pl.program_id(0) == 0)
    def _():
        xr_ref[...] = jnp.zeros_like(xr_ref)
        xi_ref[...] = jnp.zeros_like(xi_ref)

    u = u_ref[...]
    fwd = (lax.broadcasted_iota(I32, (tb * SUBLANES, 1), 0) % SUBLANES) < 4
    uu = jnp.concatenate([jnp.where(fwd, u, 0.0), jnp.where(fwd, 0.0, u)], axis=1).astype(BF16)
    bu_ref[...] = jnp.dot(uu, wb_ref[...], preferred_element_type=F32)
    ar = ar_ref[...]
    ai = ai_ref[...]

    def step(t, carry):
        xr, xi = carry
        r0 = pl.multiple_of(t * SUBLANES, SUBLANES)
        rows = pl.ds(r0, SUBLANES)
        nxr = ar * xr - ai * xi + bu_ref[rows, 0:S5_NSTATE]
        nxi = ar * xi + ai * xr + bu_ref[rows, S5_NSTATE:2 * S5_NSTATE]
        bu_ref[rows, 0:S5_NSTATE] = nxr
        bu_ref[rows, S5_NSTATE:2 * S5_NSTATE] = nxi
        return nxr, nxi

    xr, xi = lax.fori_loop(0, tb, step, (xr_ref[...], xi_ref[...]))
    xr_ref[...] = xr
    xi_ref[...] = xi
    y2 = jnp.dot(bu_ref[...].astype(BF16), wc_ref[...], preferred_element_type=F32)
    y_ref[...] = jnp.where(fwd, y2[:, :S5_COLS], y2[:, S5_COLS:])


def s5_scan(u8, wb, wc, a_re, a_im, seq):
    tb = min(S5_TIME_BLOCK, seq)
    rows = tb * SUBLANES
    const = lambda i: (0, 0)
    return pl.pallas_call(
        functools.partial(_s5_body, tb=tb),
        grid=(seq // tb,),
        in_specs=[pl.BlockSpec((rows, S5_COLS), lambda i: (i, 0)),
                  pl.BlockSpec((2 * S5_COLS, 2 * S5_NSTATE), const),
                  pl.BlockSpec((2 * S5_NSTATE, 2 * S5_COLS), const),
                  pl.BlockSpec((SUBLANES, S5_NSTATE), const),
                  pl.BlockSpec((SUBLANES, S5_NSTATE), const)],
        out_specs=pl.BlockSpec((rows, S5_COLS), lambda i: (i, 0)),
        out_shape=_sds((seq * SUBLANES, S5_COLS)),
        scratch_shapes=[pltpu.VMEM((rows, 2 * S5_NSTATE), F32),
                        pltpu.VMEM((SUBLANES, S5_NSTATE), F32),
                        pltpu.VMEM((SUBLANES, S5_NSTATE), F32)],
        compiler_params=_cp("arbitrary"),
        name="s5_scan",
    )(u8, wb, wc, a_re, a_im)


def _s5_out_body(yf_ref, yb_ref, u_ref, d_ref, w_ref, b_ref, o_ref):
    y = jax.nn.gelu(yf_ref[...] + yb_ref[...] + u_ref[...] * d_ref[...])
    gate = jnp.dot(y.astype(BF16), w_ref[...], preferred_element_type=F32) + b_ref[...]
    o_ref[...] = y * jax.nn.sigmoid(gate)


def s5_out(yf, yb, u, d, w_glu, b_glu):
    t = u.shape[0]
    tm = min(ROW_TILE, t)
    row = pl.BlockSpec((tm, S5_COLS), lambda i: (i, 0))
    vec = pl.BlockSpec((1, S5_COLS), lambda i: (0, 0))
    return pl.pallas_call(
        _s5_out_body,
        grid=(t // tm,),
        in_specs=[row, row, row, vec, pl.BlockSpec((S5_COLS, S5_COLS), lambda i: (0, 0)), vec],
        out_specs=row,
        out_shape=_sds((t, S5_COLS)),
        compiler_params=_cp("parallel"),
        name="s5_out",
    )(yf, yb, u, d, w_glu, b_glu)


def s5_mixer(z_s5, tabs, d, w_glu, b_glu, batch, seq):
    assert batch == 4, "the scan packs 4 sequences x 2 directions onto the 8 sublanes"
    u = z_s5.reshape(batch, seq, S5_COLS)
    u8 = jnp.concatenate([u.transpose(1, 0, 2), u[:, ::-1].transpose(1, 0, 2)], axis=1)
    y8 = s5_scan(u8.reshape(seq * SUBLANES, S5_COLS), *tabs, seq).reshape(seq, SUBLANES, S5_COLS)
    yf = y8[:, :4].transpose(1, 0, 2).reshape(batch * seq, S5_COLS)
    yb = y8[::-1, 4:].transpose(1, 0, 2).reshape(batch * seq, S5_COLS)
    return s5_out(yf, yb, z_s5, d, w_glu, b_glu)


def _prep_body(z_ref, qg_ref, kg_ref, c_ref, s_ref, q_ref, k_ref, v_ref, *, nq, nk, nv, rope):
    def rot(x):
        if not rope:
            return x
        half = HEAD_DIM // 2
        swapped = jnp.concatenate([x[:, half:], x[:, :half]], axis=1)
        return x * c_ref[...] + swapped * s_ref[...]

    qg = qg_ref[...]
    kg = kg_ref[...]
    for h in range(nq):
        x = z_ref[:, h * HEAD_DIM:(h + 1) * HEAD_DIM]
        q_ref[0, h] = (rot(_rms(x, qg)) * (ATTN_SCALE * LOG2E)).astype(BF16)
    for h in range(nk):
        x = z_ref[:, (nq + h) * HEAD_DIM:(nq + h + 1) * HEAD_DIM]
        k_ref[0, h] = rot(_rms(x, kg)).astype(BF16)
    tm = z_ref.shape[0]
    ones_col = jnp.where(lax.broadcasted_iota(I32, (tm, LANES - HEAD_DIM), 1) == 0, 1.0, 0.0)
    for h in range(nv):
        lo = (nq + nk + h) * HEAD_DIM
        v_ref[0, h] = jnp.concatenate([z_ref[:, lo:lo + HEAD_DIM], ones_col], axis=1).astype(BF16)


def attn_prep(z, qg, kg, cos_t, sin_t, batch, seq, nq, nk, nv, rope):
    width = (nq + nk + nv) * HEAD_DIM
    tm = min(ROW_TILE, seq)
    nb = seq // tm
    vec = pl.BlockSpec((1, HEAD_DIM), lambda b, i: (0, 0))
    tab = pl.BlockSpec((tm, HEAD_DIM), lambda b, i: (i, 0))
    heads = lambda n, w: pl.BlockSpec((1, n, tm, w), lambda b, i: (b, 0, i, 0))
    return pl.pallas_call(
        functools.partial(_prep_body, nq=nq, nk=nk, nv=nv, rope=rope),
        grid=(batch, nb),
        in_specs=[pl.BlockSpec((tm, width), lambda b, i: (b * nb + i, 0)), vec, vec, tab, tab],
        out_specs=[heads(nq, HEAD_DIM), heads(nk, HEAD_DIM), heads(nv, LANES)],
        out_shape=[_sds((batch, nq, seq, HEAD_DIM), BF16), _sds((batch, nk, seq, HEAD_DIM), BF16),
                   _sds((batch, nv, seq, LANES), BF16)],
        compiler_params=_cp("parallel", "parallel"),
        name="attn_prep",
    )(z, qg, kg, cos_t, sin_t)


def _flash_body(q_ref, k_ref, v_ref, *rest, unit, sub, has_bias):
    if has_bias:
        b_ref, o_ref, m_ref, acc_ref = rest
    else:
        o_ref, m_ref, acc_ref = rest
    i = pl.program_id(2)
    j = pl.program_id(3)
    tq = q_ref.shape[2]
    tk = k_ref.shape[2]

    @pl.when(j == 0)
    def _():
        m_ref[...] = jnp.full_like(m_ref, -jnp.inf)
        acc_ref[...] = jnp.zeros_like(acc_ref)

    k = k_ref[0, 0]
    v = v_ref[0, 0]
    for g in range(tq // sub):
        r0 = g * sub
        s = lax.dot_general(q_ref[0, 0, r0:r0 + sub, :], k, (((1,), (1,)), ((), ())),
                            preferred_element_type=F32)
        if has_bias:
            row_unit = i * (tq // unit) + r0 // unit
            tiles = []
            for c in range(tk // unit):
                sel = jnp.clip(j * (tk // unit) + c - row_unit, -2, 2) + 2
                tiles.append(b_ref[0, sel, r0 % unit:r0 % unit + sub, :])
            s = s + (tiles[0] if len(tiles) == 1 else jnp.concatenate(tiles, axis=1))
        blocks = [s[:, b * LANES:(b + 1) * LANES] for b in range(tk // LANES)]
        bmax = blocks[0]
        for blk in blocks[1:]:
            bmax = jnp.maximum(bmax, blk)
        m_prev = m_ref[r0:r0 + sub, :]
        m_new = jnp.maximum(m_prev, jnp.max(bmax, axis=-1, keepdims=True))
        alpha = jnp.exp2(m_prev - m_new)
        p = jnp.concatenate([jnp.exp2(blk - m_new) for blk in blocks], axis=1).astype(BF16)
        acc_ref[r0:r0 + sub, :] = alpha * acc_ref[r0:r0 + sub, :] + jnp.dot(p, v, preferred_element_type=F32)
        m_ref[r0:r0 + sub, :] = m_new

    @pl.when(j == pl.num_programs(3) - 1)
    def _():
        acc = acc_ref[...]
        o_ref[0, 0] = acc[:, :HEAD_DIM] / acc[:, HEAD_DIM:HEAD_DIM + 1]


def flash_attention(q, k, v, k_of_q, v_of_q, bias=None):
    batch, nq, seq, _ = q.shape
    tq = min(ATTN_Q_TILE, seq)
    tk = min(ATTN_K_TILE, seq)
    unit = min(ATTN_UNIT, seq)
    sub = min(ATTN_SUB, seq)
    in_specs = [pl.BlockSpec((1, 1, tq, HEAD_DIM), lambda b, h, i, j: (b, h, i, 0)),
                pl.BlockSpec((1, 1, tk, HEAD_DIM), lambda b, h, i, j: (b, k_of_q(h), j, 0)),
                pl.BlockSpec((1, 1, tk, LANES), lambda b, h, i, j: (b, v_of_q(h), j, 0))]
    args = [q, k, v]
    if bias is not None:
        assert bias.shape[2] == unit
        in_specs.append(pl.BlockSpec((1, 5, unit, unit), lambda b, h, i, j: (v_of_q(h), 0, 0, 0)))
        args.append(bias)
    return pl.pallas_call(
        functools.partial(_flash_body, unit=unit, sub=sub, has_bias=bias is not None),
        grid=(batch, nq, seq // tq, seq // tk),
        in_specs=in_specs,
        out_specs=pl.BlockSpec((1, 1, tq, HEAD_DIM), lambda b, h, i, j: (b, h, i, 0)),
        out_shape=_sds((batch, nq, seq, HEAD_DIM)),
        scratch_shapes=[pltpu.VMEM((tq, LANES), F32), pltpu.VMEM((tq, LANES), F32)],
        compiler_params=_cp("parallel", "parallel", "parallel", "arbitrary"),
        name="flash_bias" if bias is not None else "flash",
    )(*args)


def _t5_bucket_table(t):
    rel = np.arange(-(3 * t - 1), 3 * t)
    half = NUM_BUCKETS // 2
    max_exact = half // 2
    ret = np.where(rel > 0, half, 0)
    n = np.abs(rel)
    nf = np.maximum(n, 1).astype(np.float64)
    large = max_exact + (np.log(nf / max_exact) / math.log(MAX_DISTANCE / max_exact) * (half - max_exact)).astype(np.int64)
    large = np.minimum(large, half - 1)
    return ret + np.where(n < max_exact, n, large)


def diff_bias_tiles(rel_bias, t):
    assert t >= MAX_DISTANCE, "offsets of two or more tiles must lie in the saturated buckets"
    heads = rel_bias.shape[1]
    vec = rel_bias[jnp.asarray(_t5_bucket_table(t), dtype=I32)].T * LOG2E
    tiles = []
    for d in range(-2, 3):
        lo = d * t + 2 * t
        w = vec[:, lo:lo + 2 * t - 1]
        u = jnp.concatenate([w[:, t - 1:], jnp.zeros((heads, 1), F32), w[:, :t - 1]], axis=1)
        skew = jnp.tile(u, (1, t))[:, :t * (2 * t - 1)].reshape(heads, t, 2 * t - 1)
        tiles.append(skew[:, :, :t])
    return jnp.stack(tiles, axis=1)


def rope_tables(seq):
    rows = seq // GRID_W
    row = jnp.repeat(jnp.arange(rows, dtype=F32), GRID_W)
    col = jnp.tile(jnp.arange(GRID_W, dtype=F32), rows)
    freq = ROPE_THETA ** (-jnp.arange(0, ROPE_AXIS_DIM, 2, dtype=F32) / ROPE_AXIS_DIM)
    ang = jnp.concatenate([row[:, None] * freq, col[:, None] * freq], -1)
    cos, sin = jnp.cos(ang), jnp.sin(ang)
    return jnp.concatenate([cos, cos], -1), jnp.concatenate([-sin, sin], -1)


def _conv3_body(x0, x1, x2, w0, w1, w2, b0, b1, b2, o0, o1, o2):
    for x_ref, w_ref, b_ref, o_ref in ((x0, w0, b0, o0), (x1, w1, b1, o1), (x2, w2, b2, o2)):
        x = x_ref[0]
        n = x.shape[0]
        t = lax.broadcasted_iota(I32, (n, 1), 0)
        prev = jnp.where(t == 0, 0.0, pltpu.roll(x, 1, 0))
        nxt = jnp.where(t == n - 1, 0.0, pltpu.roll(x, n - 1, 0))
        o_ref[0] = w_ref[0:1] * prev + w_ref[1:2] * x + w_ref[2:3] * nxt + b_ref[...]


def hyena_conv3(z_hy, conv_w, conv_b, batch, seq):
    z = z_hy.reshape(batch, seq, HY_COLS)
    nc = HY_WIDTH // LANES
    x_spec = lambda g: pl.BlockSpec((1, seq, LANES), lambda b, c: (b, 0, g * nc + c))
    w_spec = lambda g: pl.BlockSpec((3, LANES), lambda b, c: (0, g * nc + c))
    b_spec = lambda g: pl.BlockSpec((1, LANES), lambda b, c: (0, g * nc + c))
    out = pl.BlockSpec((1, seq, LANES), lambda b, c: (b, 0, c))
    return pl.pallas_call(
        _conv3_body,
        grid=(batch, nc),
        in_specs=[x_spec(0), x_spec(1), x_spec(2), w_spec(0), w_spec(1), w_spec(2),
                  b_spec(0), b_spec(1), b_spec(2)],
        out_specs=[out, out, out],
        out_shape=[_sds((batch, seq, HY_WIDTH))] * 3,
        compiler_params=_cp("parallel", "parallel"),
        name="hyena_conv3",
    )(z, z, z, conv_w, conv_w, conv_w, conv_b, conv_b, conv_b)


def _filt_body(feat_ref, dist_ref, w1, b1, f1, w2, b2, f2, w3, dec, h_ref, s_ref):
    dot = functools.partial(jnp.dot, precision=HI, preferred_element_type=F32)
    hid = jnp.sin(f1[...] * (dot(feat_ref[...], w1[...]) + b1[...]))
    hid = jnp.sin(f2[...] * (dot(hid, w2[...]) + b2[...]))
    h = dot(hid, w3[...]) * jnp.exp(-dist_ref[...] * jnp.abs(dec[...]))
    h_ref[...] = h

    @pl.when(pl.program_id(0) == 0)
    def _():
        s_ref[...] = jnp.zeros_like(s_ref)

    s_ref[...] += jnp.sum(jnp.abs(h), axis=0, keepdims=True)


def hyena_filter(seq, w1, b1, f1, w2, b2, f2, w3, decay):
    j = jnp.arange(seq, dtype=F32)
    tt = j / seq
    ang = 2.0 * jnp.pi * tt[:, None] * jnp.arange(1, HY_BANDS + 1, dtype=F32)
    feat = jnp.concatenate([tt[:, None], jnp.cos(ang), jnp.sin(ang)], -1)
    feat = jnp.pad(feat, ((0, 0), (0, LANES - HY_EMB)))
    half = seq // 2
    dist = (jnp.abs(j - half) / half)[:, None]
    w1p = jnp.pad(w1, ((0, LANES - HY_EMB), (0, 0)))
    tm = min(1024, seq)
    hw = HY_ORDER * HY_WIDTH
    const = lambda i: (0, 0)
    return pl.pallas_call(
        _filt_body,
        grid=(seq // tm,),
        in_specs=[pl.BlockSpec((tm, LANES), lambda i: (i, 0)), pl.BlockSpec((tm, 1), lambda i: (i, 0)),
                  pl.BlockSpec((LANES, HY_FFN), const), pl.BlockSpec((1, HY_FFN), const),
                  pl.BlockSpec((1, HY_FFN), const), pl.BlockSpec((HY_FFN, HY_FFN), const),
                  pl.BlockSpec((1, HY_FFN), const), pl.BlockSpec((1, HY_FFN), const),
                  pl.BlockSpec((HY_FFN, hw), const), pl.BlockSpec((1, hw), const)],
        out_specs=[pl.BlockSpec((tm, hw), lambda i: (i, 0)), pl.BlockSpec((1, hw), const)],
        out_shape=[_sds((seq, hw)), _sds((1, hw))],
        compiler_params=_cp("arbitrary"),
        name="hyena_filter",
    )(feat, dist, w1p, b1[None], f1[None], w2, b2[None], f2[None], w3, decay.reshape(1, hw))


def dft_tables(seq):
    n = 2 * seq
    n2 = FFT_N2
    n1 = n // n2
    n1h = n1 // 2
    k1 = jnp.arange(n1, dtype=I32)
    ang = (2.0 * jnp.pi / n1) * ((k1[:, None] * jnp.arange(n1h, dtype=I32)[None, :]) % n1).astype(F32)
    outer_fwd = jnp.concatenate([jnp.cos(ang), -jnp.sin(ang)], axis=0)
    t1 = jnp.arange(n1h, dtype=I32) + n1 // 4
    ang = (2.0 * jnp.pi / n1) * ((t1[:, None] * k1[None, :]) % n1).astype(F32)
    outer_inv = jnp.concatenate([jnp.cos(ang), -jnp.sin(ang)], axis=1) / n
    k2 = jnp.arange(n2, dtype=I32)
    t2 = jnp.arange(n2, dtype=I32)
    phase = (t2[None, None, :] * (k2[None, :, None] * n1 + k1[:, None, None])) % n
    ang = (2.0 * jnp.pi / n) * phase.astype(F32)
    fr, fi = jnp.cos(ang), -jnp.sin(ang)
    inner = jnp.concatenate([jnp.concatenate([fr, -fi], axis=2),
                             jnp.concatenate([fi, fr], axis=2)], axis=1)
    return outer_fwd, outer_inv, inner, jnp.swapaxes(inner, 1, 2)


def _fft_outer_body(f_ref, x_ref, o_ref):
    o_ref[0] = jnp.dot(f_ref[...], x_ref[0], precision=HI, preferred_element_type=F32)


def fft_outer(x, table):
    batch, n1h, cols = x.shape
    rows = table.shape[0]
    tc = min(FFT_COL_TILE, cols)
    return pl.pallas_call(
        _fft_outer_body,
        grid=(batch, cols // tc),
        in_specs=[pl.BlockSpec((rows, n1h), lambda b, c: (0, 0)),
                  pl.BlockSpec((1, n1h, tc), lambda b, c: (b, 0, c))],
        out_specs=pl.BlockSpec((1, rows, tc), lambda b, c: (b, 0, c)),
        out_shape=_sds((batch, rows, cols)),
        compiler_params=_cp("parallel", "parallel"),
        name="fft_outer",
    )(table, x)


def _fft_filter_body(m_ref, s_ref, a_ref, o_ref):
    n2 = FFT_N2
    a = a_ref[0, :, 0].reshape(2 * n2, HY_WIDTH)
    x = jnp.dot(m_ref[0], a, precision=HI, preferred_element_type=F32) * (1.0 / s_ref[...])
    o_ref[:, 0] = x.reshape(2, n2, HY_WIDTH)


def fft_filter(a5, inner, s):
    _, _, n1, n2, ch = a5.shape
    return pl.pallas_call(
        _fft_filter_body,
        grid=(n1, ch // HY_WIDTH),
        in_specs=[pl.BlockSpec((1, 2 * n2, 2 * n2), lambda k, c: (k, 0, 0)),
                  pl.BlockSpec((1, HY_WIDTH), lambda k, c: (0, c)),
                  pl.BlockSpec((1, 2, 1, n2, HY_WIDTH), lambda k, c: (0, 0, k, 0, c))],
        out_specs=pl.BlockSpec((2, 1, n2, HY_WIDTH), lambda k, c: (0, k, 0, c)),
        out_shape=_sds((2, n1, n2, ch)),
        compiler_params=_cp("parallel", "parallel"),
        name="fft_filter",
    )(inner, s, a5)


def _fft_mid_body(m_ref, mt_ref, h_ref, a_ref, o_ref, *, nb):
    n2 = FFT_N2
    dot = functools.partial(jnp.dot, precision=HI, preferred_element_type=F32)
    m = m_ref[0]
    mt = mt_ref[0]
    hr = h_ref[0, 0]
    hi = h_ref[1, 0]
    for b in range(nb):
        x = dot(m, a_ref[b, :, 0].reshape(2 * n2, HY_WIDTH))
        xr, xi = x[:n2], x[n2:]
        y = jnp.concatenate([xr * hr - xi * hi, xr * hi + xi * hr], axis=0)
        o_ref[b, :, 0] = dot(mt, y).reshape(2, n2, HY_WIDTH)


def fft_mid(a5, inner, inner_t, hf, order):
    nb, _, n1, n2, ch = a5.shape
    mat = pl.BlockSpec((1, 2 * n2, 2 * n2), lambda k: (k, 0, 0))
    blk = pl.BlockSpec((nb, 2, 1, n2, ch), lambda k: (0, 0, k, 0, 0))
    return pl.pallas_call(
        functools.partial(_fft_mid_body, nb=nb),
        grid=(n1,),
        in_specs=[mat, mat, pl.BlockSpec((2, 1, n2, ch), lambda k: (0, k, 0, order)), blk],
        out_specs=blk,
        out_shape=_sds(a5.shape),
        compiler_params=_cp("parallel"),
        name="fft_mid",
    )(inner, inner_t, hf, a5)


def _fft_inv_body(g_ref, b_ref, gate_ref, y_ref, fb_ref, o_ref):
    c = jnp.dot(g_ref[...], b_ref[0], precision=HI, preferred_element_type=F32)
    o_ref[0] = gate_ref[0] * (c + fb_ref[...] * y_ref[0])


def fft_inv_gate(bmat, table, gate, y, fbias):
    batch, rows, cols = bmat.shape
    n1h = table.shape[0]
    tc = min(FFT_COL_TILE, cols)
    fb = jnp.tile(fbias.reshape(1, HY_WIDTH), (1, tc // HY_WIDTH))
    blk = pl.BlockSpec((1, n1h, tc), lambda b, c: (b, 0, c))
    return pl.pallas_call(
        _fft_inv_body,
        grid=(batch, cols // tc),
        in_specs=[pl.BlockSpec((n1h, rows), lambda b, c: (0, 0)),
                  pl.BlockSpec((1, rows, tc), lambda b, c: (b, 0, c)), blk, blk,
                  pl.BlockSpec((1, tc), lambda b, c: (0, 0))],
        out_specs=blk,
        out_shape=_sds((batch, n1h, cols)),
        compiler_params=_cp("parallel", "parallel"),
        name="fft_inv_gate",
    )(table, bmat, gate, y, fb)


def hyena_spectrum(seq, tabs, w1, b1, f1, w2, b2, f2, w3, decay):
    outer_fwd, _, inner, _ = tabs
    n1 = outer_fwd.shape[0] // 2
    hw = HY_ORDER * HY_WIDTH
    h, s = hyena_filter(seq, w1, b1, f1, w2, b2, f2, w3, decay)
    a = fft_outer(h.reshape(1, n1 // 2, FFT_N2 * hw), outer_fwd)
    return fft_filter(a.reshape(1, 2, n1, FFT_N2, hw), inner, s)


def hyena_mixer(z_hy, conv_w, conv_b, hf, f_bias, tabs, batch, seq):
    outer_fwd, outer_inv, inner, inner_t = tabs
    n1 = outer_fwd.shape[0] // 2
    cols = FFT_N2 * HY_WIDTH
    v, g0, g1 = hyena_conv3(z_hy, conv_w, conv_b, batch, seq)
    y = v.reshape(batch, n1 // 2, cols)
    for order, gate in enumerate((g0, g1)):
        a = fft_outer(y, outer_fwd).reshape(batch, 2, n1, FFT_N2, HY_WIDTH)
        bm = fft_mid(a, inner, inner_t, hf, order).reshape(batch, 2 * n1, cols)
        y = fft_inv_gate(bm, outer_inv, gate.reshape(batch, n1 // 2, cols), y, f_bias[order])
    return y.reshape(batch * seq, HY_WIDTH)


def _mix_body(lam_ref, h_ref, s5_ref, gqa_ref, diff_ref, hy_ref, gs5, ggqa, gsub, ghy, w_ref, o_ref, *, keep):
    lam = lam_ref[0]
    a = _rms(s5_ref[...], gs5[...])
    b = _rms(jnp.concatenate([gqa_ref[0, h] for h in range(GQA_HEADS)], axis=1), ggqa[...])
    c = jnp.concatenate(
        [_rms(diff_ref[0, 2 * h] - lam * diff_ref[0, 2 * h + 1], gsub[...]) * keep for h in range(DIFF_HEADS)],
        axis=1)
    d = _rms(hy_ref[...], ghy[...])
    mixed = jnp.concatenate([a, b, c, d], axis=1).astype(BF16)
    o_ref[...] = h_ref[...] + jnp.dot(mixed, w_ref[...], preferred_element_type=F32)


def mix_out(lam, h, y_s5, o_gqa, o_diff, y_hy, gs5, ggqa, gsub, ghy, w_out, batch, seq, lambda_init):
    tm = min(ROW_TILE, seq)
    nb = seq // tm
    row = lambda w: pl.BlockSpec((tm, w), lambda b, i: (b * nb + i, 0))
    vec = lambda w: pl.BlockSpec((1, w), lambda b, i: (0, 0))
    heads = lambda n: pl.BlockSpec((1, n, tm, HEAD_DIM), lambda b, i: (b, 0, i, 0))
    return pl.pallas_call(
        functools.partial(_mix_body, keep=1.0 - lambda_init),
        grid=(batch, nb),
        in_specs=[pl.BlockSpec(memory_space=pltpu.SMEM), row(D_MODEL), row(GROUP_WIDTH), heads(GQA_HEADS),
                  heads(2 * DIFF_HEADS), row(GROUP_WIDTH), vec(GROUP_WIDTH), vec(GROUP_WIDTH), vec(HEAD_DIM),
                  vec(GROUP_WIDTH), pl.BlockSpec((D_MODEL, D_MODEL), lambda b, i: (0, 0))],
        out_specs=row(D_MODEL),
        out_shape=_sds((batch * seq, D_MODEL)),
        compiler_params=_cp("parallel", "parallel"),
        name="mix_out",
    )(lam, h, y_s5, o_gqa, o_diff, y_hy, gs5, ggqa, gsub, ghy, w_out)


def _router_body(h_ref, g_ref, w_ref, b_ref, xn_ref, ti_ref, tw_ref, cnt_ref):
    xn = _rms(h_ref[...], g_ref[...])
    xn_ref[...] = xn
    logits = jnp.dot(xn, w_ref[...], precision=HI, preferred_element_type=F32) + b_ref[...]
    lane = lax.broadcasted_iota(I32, logits.shape, 1)
    vals = logits
    tops, idxs = [], []
    hot = jnp.zeros(logits.shape, F32)
    for _ in range(TOP_K):
        m = jnp.max(vals, axis=-1, keepdims=True)
        idx = jnp.min(jnp.where(vals == m, lane, N_EXPERTS), axis=-1, keepdims=True)
        sel = lane == idx
        tops.append(m)
        idxs.append(idx)
        hot = hot + sel.astype(F32)
        vals = jnp.where(sel, -jnp.inf, vals)
    es = [jnp.exp(t - tops[0]) for t in tops]
    den = es[0] + es[1] + es[2] + es[3]
    ti_ref[...] = jnp.concatenate(idxs, axis=1)
    tw_ref[...] = jnp.concatenate([e / den for e in es], axis=1)

    @pl.when(pl.program_id(0) == 0)
    def _():
        cnt_ref[...] = jnp.zeros_like(cnt_ref)

    cnt_ref[...] += jnp.sum(hot, axis=0, keepdims=True)


def moe_router(h, g, w_router, b_router):
    t = h.shape[0]
    tm = min(ROW_TILE, t)
    const = lambda i: (0, 0)
    return pl.pallas_call(
        _router_body,
        grid=(t // tm,),
        in_specs=[pl.BlockSpec((tm, D_MODEL), lambda i: (i, 0)), pl.BlockSpec((1, D_MODEL), const),
                  pl.BlockSpec((D_MODEL, N_EXPERTS), const), pl.BlockSpec((1, N_EXPERTS), const)],
        out_specs=[pl.BlockSpec((tm, D_MODEL), lambda i: (i, 0)), pl.BlockSpec((tm, TOP_K), lambda i: (i, 0)),
                   pl.BlockSpec((tm, TOP_K), lambda i: (i, 0)), pl.BlockSpec((1, N_EXPERTS), const)],
        out_shape=[_sds((t, D_MODEL)), _sds((t, TOP_K), I32), _sds((t, TOP_K)), _sds((1, N_EXPERTS))],
        compiler_params=_cp("arbitrary"),
        name="moe_router",
    )(h, g, w_router, b_router)


def _rank_body(ti_ref, off_ref, pos_ref, carry_ref):
    @pl.when(pl.program_id(0) == 0)
    def _():
        carry_ref[...] = jnp.zeros_like(carry_ref)

    ti = ti_ref[...]
    tm = ti.shape[0]
    lane = lax.broadcasted_iota(I32, (tm, N_EXPERTS), 1)
    hots = [lane == ti[:, k:k + 1] for k in range(TOP_K)]
    hot = sum(h.astype(F32) for h in hots)
    r = lax.broadcasted_iota(I32, (tm, tm), 0)
    c = lax.broadcasted_iota(I32, (tm, tm), 1)
    below = jnp.where(r > c, 1.0, 0.0).astype(BF16)
    before = jnp.dot(below, hot.astype(BF16), preferred_element_type=F32) + carry_ref[...] + off_ref[...]
    pos = [jnp.sum(jnp.where(h, before, 0.0), axis=-1, keepdims=True) for h in hots]
    pos_ref[...] = jnp.concatenate(pos, axis=1).astype(I32)
    carry_ref[...] += jnp.sum(hot, axis=0, keepdims=True)


def moe_rank(top_i, offsets):
    t = top_i.shape[0]
    tm = min(ROW_TILE, t)
    return pl.pallas_call(
        _rank_body,
        grid=(t // tm,),
        in_specs=[pl.BlockSpec((tm, TOP_K), lambda i: (i, 0)), pl.BlockSpec((1, N_EXPERTS), lambda i: (0, 0))],
        out_specs=pl.BlockSpec((tm, TOP_K), lambda i: (i, 0)),
        out_shape=_sds((t, TOP_K), I32),
        scratch_shapes=[pltpu.VMEM((1, N_EXPERTS), F32)],
        compiler_params=_cp("arbitrary"),
        name="moe_rank",
    )(top_i, offsets)


def _row_copy(src, dst, sem):
    return pltpu.make_async_copy(src, dst, sem)


def _scatter_body(pos_ref, x_ref, zero_hbm, xs_hbm, buf, sem, *, tm):
    del zero_hbm
    x = x_ref[...]
    for s in range(ROW_TILES):
        buf[:, s, :] = x[:, s * LANES:(s + 1) * LANES]

    def start(r, c):
        for k in range(TOP_K):
            _row_copy(buf.at[r], xs_hbm.at[pos_ref[r * TOP_K + k]], sem).start()
        return c

    lax.fori_loop(0, tm, start, 0)

    def wait(r, c):
        for k in range(TOP_K):
            _row_copy(buf.at[r], xs_hbm.at[pos_ref[r * TOP_K + k]], sem).wait()
        return c

    lax.fori_loop(0, tm, wait, 0)


def moe_scatter(pos_flat, x, n_slots):
    t = x.shape[0]
    tm = min(MOE_TOK_TILE, t)
    zeros = jnp.zeros((n_slots, ROW_TILES, LANES), F32)
    return pl.pallas_call(
        functools.partial(_scatter_body, tm=tm),
        grid=(t // tm,),
        in_specs=[pl.BlockSpec((tm * TOP_K,), lambda i: (i,), memory_space=pltpu.SMEM),
                  pl.BlockSpec((tm, D_MODEL), lambda i: (i, 0)), pl.BlockSpec(memory_space=pl.ANY)],
        out_specs=pl.BlockSpec(memory_space=pl.ANY),
        out_shape=_sds((n_slots, ROW_TILES, LANES)),
        scratch_shapes=[pltpu.VMEM((tm, ROW_TILES, LANES), F32), pltpu.SemaphoreType.DMA(())],
        input_output_aliases={2: 0},
        compiler_params=_cp("arbitrary"),
        name="moe_scatter",
    )(pos_flat, x, zeros)


def _experts_body(te_ref, nv_ref, x_ref, wgu_ref, bgu_ref, wd_ref, bd_ref, o_ref):
    del te_ref
    j = pl.program_id(0)
    dff = wd_ref.shape[1]

    @pl.when(j < nv_ref[0])
    def _():
        x = jnp.concatenate([x_ref[:, s, :] for s in range(ROW_TILES)], axis=1).astype(BF16)
        hgu = jnp.dot(x, wgu_ref[0], preferred_element_type=F32) + bgu_ref[0]
        glu = jnp.minimum(hgu[:, :dff], SWIGLU_LIMIT)
        lin = jnp.clip(hgu[:, dff:], -SWIGLU_LIMIT, SWIGLU_LIMIT)
        act = glu * jax.nn.sigmoid(SWIGLU_ALPHA * glu) * (lin + 1.0)
        y = jnp.dot(act.astype(BF16), wd_ref[0], preferred_element_type=F32) + bd_ref[0]
        for s in range(ROW_TILES):
            o_ref[:, s, :] = y[:, s * LANES:(s + 1) * LANES]

    @pl.when(j >= nv_ref[0])
    def _():
        o_ref[...] = jnp.zeros_like(o_ref)


def moe_experts(tile_expert, n_valid, xs3, w_gu, b_gu, w_down, b_down):
    n_slots = xs3.shape[0]
    tm = MOE_TILE
    dff = w_down.shape[1]
    blk = pl.BlockSpec((tm, ROW_TILES, LANES), lambda j, te, nv: (j, 0, 0))
    return pl.pallas_call(
        _experts_body,
        grid_spec=pltpu.PrefetchScalarGridSpec(
            num_scalar_prefetch=2,
            grid=(n_slots // tm,),
            in_specs=[blk,
                      pl.BlockSpec((1, D_MODEL, 2 * dff), lambda j, te, nv: (te[j], 0, 0)),
                      pl.BlockSpec((1, 1, 2 * dff), lambda j, te, nv: (te[j], 0, 0)),
                      pl.BlockSpec((1, dff, D_MODEL), lambda j, te, nv: (te[j], 0, 0)),
                      pl.BlockSpec((1, 1, D_MODEL), lambda j, te, nv: (te[j], 0, 0))],
            out_specs=blk),
        out_shape=_sds((n_slots, ROW_TILES, LANES)),
        compiler_params=_cp("arbitrary"),
        name="moe_experts",
    )(tile_expert, n_valid, xs3, w_gu, b_gu, w_down, b_down)


def _combine_body(pos_ref, w_ref, ys_hbm, o_ref, b0, b1, b2, b3, sem, *, tm):
    bufs = (b0, b1, b2, b3)

    def start(r, c):
        for k in range(TOP_K):
            _row_copy(ys_hbm.at[pos_ref[r * TOP_K + k]], bufs[k].at[r], sem).start()
        return c

    lax.fori_loop(0, tm, start, 0)

    def wait(r, c):
        for k in range(TOP_K):
            _row_copy(ys_hbm.at[pos_ref[r * TOP_K + k]], bufs[k].at[r], sem).wait()
        return c

    lax.fori_loop(0, tm, wait, 0)

    def mix(r, c):
        acc = w_ref[r * TOP_K] * b0[r]
        for k in range(1, TOP_K):
            acc = acc + w_ref[r * TOP_K + k] * bufs[k][r]
        o_ref[r] = acc
        return c

    lax.fori_loop(0, tm, mix, 0)


def moe_combine(pos_flat, w_flat, ys3, t):
    tm = min(MOE_TOK_TILE, t)
    smem = pl.BlockSpec((tm * TOP_K,), lambda i: (i,), memory_space=pltpu.SMEM)
    return pl.pallas_call(
        functools.partial(_combine_body, tm=tm),
        grid=(t // tm,),
        in_specs=[smem, smem, pl.BlockSpec(memory_space=pl.ANY)],
        out_specs=pl.BlockSpec((tm, ROW_TILES, LANES), lambda i: (i, 0, 0)),
        out_shape=_sds((t, ROW_TILES, LANES)),
        scratch_shapes=[pltpu.VMEM((tm, ROW_TILES, LANES), F32)] * TOP_K + [pltpu.SemaphoreType.DMA(())],
        compiler_params=_cp("arbitrary"),
        name="moe_combine",
    )(pos_flat, w_flat, ys3)


def moe(h, g, w_router, b_router, w_gu, b_gu, w_down, b_down):
    t = h.shape[0]
    xn, top_i, top_w, counts = moe_router(h, g, w_router, b_router)
    n_tiles = (t * TOP_K) // MOE_TILE + N_EXPERTS
    padded = jnp.ceil(counts[0] / MOE_TILE) * MOE_TILE
    ends = jnp.cumsum(padded)
    offsets = (ends - padded)[None]
    tile_start = jnp.arange(n_tiles, dtype=F32) * MOE_TILE
    tile_expert = jnp.minimum(jnp.sum(ends[None, :] <= tile_start[:, None], axis=1), N_EXPERTS - 1).astype(I32)
    n_valid = (ends[-1:] / MOE_TILE).astype(I32)
    pos = moe_rank(top_i, offsets).reshape(t * TOP_K)
    xs3 = moe_scatter(pos, xn, n_tiles * MOE_TILE)
    ys3 = moe_experts(tile_expert, n_valid, xs3, w_gu, b_gu, w_down, b_down)
    return moe_combine(pos, top_w.reshape(t * TOP_K), ys3, t).reshape(t, D_MODEL)


def _ple_body(h_ref, m_ref, p_ref, wp_ref, wg_ref, o_ref):
    h = h_ref[...] + m_ref[...]
    e = jnp.dot(p_ref[...].astype(BF16), wp_ref[...], preferred_element_type=F32)
    gate = jnp.dot(_rms(h).astype(BF16), wg_ref[...], preferred_element_type=F32)
    o_ref[...] = h + e * jax.nn.sigmoid(gate)


def ple(h, moe_out, p, w_ple, w_gate):
    t = h.shape[0]
    tm = min(ROW_TILE, t)
    row = pl.BlockSpec((tm, D_MODEL), lambda i: (i, 0))
    return pl.pallas_call(
        _ple_body,
        grid=(t // tm,),
        in_specs=[row, row, pl.BlockSpec((tm, PLE_DIM), lambda i: (i, 0)),
                  pl.BlockSpec((PLE_DIM, D_MODEL), lambda i: (0, 0)),
                  pl.BlockSpec((D_MODEL, D_MODEL), lambda i: (0, 0))],
        out_specs=row,
        out_shape=_sds((t, D_MODEL)),
        compiler_params=_cp("parallel"),
        name="ple",
    )(h, moe_out, p, w_ple, w_gate)


def _deinterleave(n):
    return np.concatenate([np.arange(0, n, 2), np.arange(1, n, 2)])


def _halves(w):
    pairs = w.reshape(*w.shape[:-1], w.shape[-1] // 2, 2)
    return jnp.concatenate([pairs[..., 0], pairs[..., 1]], axis=-1)


def prepare_layer(i, prm):
    pair = _deinterleave(HEAD_DIM)
    cols = np.arange(IN_COLS)
    for h in range(GQA_HEADS + GQA_KV_HEADS):
        lo = OFF_GQA + h * HEAD_DIM
        cols[lo:lo + HEAD_DIM] = lo + pair
    lam = (jnp.exp(jnp.sum(prm["diff_lam_q1"][i] * prm["diff_lam_k1"][i]))
           - jnp.exp(jnp.sum(prm["diff_lam_q2"][i] * prm["diff_lam_k2"][i])) + _lambda_init(i))
    return dict(
        norm1_g=prm["norm1_g"][i][None],
        w_in=prm["w_in"][i][:, cols].astype(BF16),
        s5=s5_tables(prm["s5_lam_re"][i], prm["s5_lam_im"][i], prm["s5_log_dt"][i], prm["s5_b_re"][i],
                     prm["s5_b_im"][i], prm["s5_c_re"][i], prm["s5_c_im"][i]),
        s5_d=prm["s5_d"][i][None],
        s5_w_glu=prm["s5_w_glu"][i].astype(BF16),
        s5_b_glu=prm["s5_b_glu"][i][None],
        gqa_q_g=prm["gqa_q_g"][i][pair][None],
        gqa_k_g=prm["gqa_k_g"][i][pair][None],
        diff_q_g=prm["diff_q_g"][i][None],
        diff_k_g=prm["diff_k_g"][i][None],
        diff_lam=lam.reshape(1).astype(F32),
        diff_subln_g=prm["diff_subln_g"][i][None],
        out_g_s5=prm["out_g_s5"][i][None],
        out_g_gqa=prm["out_g_gqa"][i][None],
        out_g_hy=prm["out_g_hy"][i][None],
        w_out=prm["w_out"][i].astype(BF16),
        norm2_g=prm["norm2_g"][i][None],
        w_router=prm["w_router"][i],
        b_router=prm["b_router"][i][None],
        w_gu=_halves(prm["w_gu"][i]).astype(BF16),
        b_gu=_halves(prm["b_gu"][i])[:, None, :],
        w_down=prm["w_down"][i].astype(BF16),
        b_down=prm["b_down"][i][:, None, :],
        w_ple=prm["w_ple"][i].astype(BF16),
        w_ple_gate=prm["w_ple_gate"][i].astype(BF16),
    )


def _lambda_init(i):
    return 0.8 - 0.6 * math.exp(-0.3 * i)


def run_trunk(x, p, prm, layers):
    batch, seq, _ = x.shape
    t = batch * seq
    cos_t, sin_t = rope_tables(seq)
    tabs = dft_tables(seq)
    bias = diff_bias_tiles(prm["rel_bias"], min(ATTN_UNIT, seq))
    h = x.reshape(t, D_MODEL)
    for i, lw in enumerate(layers):
        z_s5, z_gqa, z_diff, z_hy = in_proj(h, lw["norm1_g"], lw["w_in"])
        y_s5 = s5_mixer(z_s5, lw["s5"], lw["s5_d"], lw["s5_w_glu"], lw["s5_b_glu"], batch, seq)
        q, k, v = attn_prep(z_gqa, lw["gqa_q_g"], lw["gqa_k_g"], cos_t, sin_t, batch, seq,
                            GQA_HEADS, GQA_KV_HEADS, GQA_KV_HEADS, True)
        rep = GQA_HEADS // GQA_KV_HEADS
        o_gqa = flash_attention(q, k, v, lambda hq: hq // rep, lambda hq: hq // rep)
        q, k, v = attn_prep(z_diff, lw["diff_q_g"], lw["diff_k_g"], cos_t, sin_t, batch, seq,
                            2 * DIFF_HEADS, 2 * DIFF_HEADS, DIFF_HEADS, False)
        o_diff = flash_attention(q, k, v, lambda hq: hq, lambda hq: hq // 2, bias)
        hf = hyena_spectrum(seq, tabs, prm["hy_w1"][i], prm["hy_b1"][i], prm["hy_freq1"][i], prm["hy_w2"][i],
                            prm["hy_b2"][i], prm["hy_freq2"][i], prm["hy_w3"][i], prm["hy_decay"][i])
        y_hy = hyena_mixer(z_hy, prm["hy_conv_w"][i], prm["hy_conv_b"][i][None], hf, prm["hy_bias"][i],
                           tabs, batch, seq)
        h = mix_out(lw["diff_lam"], h, y_s5, o_gqa, o_diff, y_hy, lw["out_g_s5"], lw["out_g_gqa"],
                    lw["diff_subln_g"], lw["out_g_hy"], lw["w_out"], batch, seq, _lambda_init(i))
        m = moe(h, lw["norm2_g"], lw["w_router"], lw["b_router"], lw["w_gu"], lw["b_gu"], lw["w_down"],
                lw["b_down"])
        h = ple(h, m, p[i].reshape(t, PLE_DIM), lw["w_ple"], lw["w_ple_gate"])
    return h.reshape(batch, seq, D_MODEL)


def kernel(x_prompt, x_sample, p_prompt, p_sample, rel_bias, norm1_g, w_in, s5_lam_re, s5_lam_im, s5_log_dt,
           s5_b_re, s5_b_im, s5_c_re, s5_c_im, s5_d, s5_w_glu, s5_b_glu, gqa_q_g, gqa_k_g, diff_q_g, diff_k_g,
           diff_lam_q1, diff_lam_k1, diff_lam_q2, diff_lam_k2, diff_subln_g, hy_conv_w, hy_conv_b, hy_w1, hy_b1,
           hy_freq1, hy_w2, hy_b2, hy_freq2, hy_w3, hy_decay, hy_bias, out_g_s5, out_g_gqa, out_g_hy, w_out,
           norm2_g, w_router, b_router, w_gu, b_gu, w_down, b_down, w_ple, w_ple_gate):
    prm = dict(rel_bias=rel_bias, norm1_g=norm1_g, w_in=w_in, s5_lam_re=s5_lam_re, s5_lam_im=s5_lam_im,
               s5_log_dt=s5_log_dt, s5_b_re=s5_b_re, s5_b_im=s5_b_im, s5_c_re=s5_c_re, s5_c_im=s5_c_im,
               s5_d=s5_d, s5_w_glu=s5_w_glu, s5_b_glu=s5_b_glu, gqa_q_g=gqa_q_g, gqa_k_g=gqa_k_g,
               diff_q_g=diff_q_g, diff_k_g=diff_k_g, diff_lam_q1=diff_lam_q1, diff_lam_k1=diff_lam_k1,
               diff_lam_q2=diff_lam_q2, diff_lam_k2=diff_lam_k2, diff_subln_g=diff_subln_g,
               hy_conv_w=hy_conv_w, hy_conv_b=hy_conv_b, hy_w1=hy_w1, hy_b1=hy_b1, hy_freq1=hy_freq1,
               hy_w2=hy_w2, hy_b2=hy_b2, hy_freq2=hy_freq2, hy_w3=hy_w3, hy_decay=hy_decay, hy_bias=hy_bias,
               out_g_s5=out_g_s5, out_g_gqa=out_g_gqa, out_g_hy=out_g_hy, w_out=w_out, norm2_g=norm2_g,
               w_router=w_router, b_router=b_router, w_gu=w_gu, b_gu=b_gu, w_down=w_down, b_down=b_down,
               w_ple=w_ple, w_ple_gate=w_ple_gate)
    layers = [prepare_layer(i, prm) for i in range(DEPTH)]
    return (run_trunk(x_prompt, p_prompt, prm, layers), run_trunk(x_sample, p_sample, prm, layers))
```

```python
import functools
import math

import numpy as np
import jax
import jax.numpy as jnp
from jax import lax
from jax.experimental import pallas as pl
from jax.experimental.pallas import tpu as pltpu

F32 = jnp.float32
BF16 = jnp.bfloat16
I32 = jnp.int32
HI = lax.Precision.HIGHEST

D_MODEL = 1024
DEPTH = 2
GROUP_WIDTH = 256
HEAD_DIM = 64
S5_GROUP_DIM = 16
S5_GROUPS = 16
S5_STATE = 64
S5_NSTATE = S5_GROUPS * S5_STATE
GQA_HEADS = 4
GQA_KV_HEADS = 2
DIFF_HEADS = 4
HY_WIDTH = 256
HY_ORDER = 2
HY_BANDS = 16
HY_EMB = 2 * HY_BANDS + 1
HY_FFN = 64
N_EXPERTS = 32
TOP_K = 4
SWIGLU_LIMIT = 7.0
SWIGLU_ALPHA = 1.702
NUM_BUCKETS = 32
MAX_DISTANCE = 128
GRID_W = 64
ROPE_THETA = 10000.0
ROPE_AXIS_DIM = HEAD_DIM // 2
PLE_DIM = 256
NORM_EPS = 1e-6
ATTN_SCALE = HEAD_DIM ** -0.5
LOG2E = math.log2(math.e)

S5_COLS = GROUP_WIDTH
GQA_QW = GQA_HEADS * HEAD_DIM
GQA_KW = GQA_KV_HEADS * HEAD_DIM
GQA_COLS = GQA_QW + 2 * GQA_KW
DIFF_QW = DIFF_HEADS * 2 * HEAD_DIM
DIFF_COLS = 2 * DIFF_QW + DIFF_HEADS * HEAD_DIM
HY_COLS = (HY_ORDER + 1) * HY_WIDTH
OFF_GQA = S5_COLS
OFF_DIFF = OFF_GQA + GQA_COLS
OFF_HY = OFF_DIFF + DIFF_COLS
IN_COLS = OFF_HY + HY_COLS
IN_SPLITS = ((0, OFF_GQA), (OFF_GQA, OFF_DIFF), (OFF_DIFF, OFF_HY), (OFF_HY, IN_COLS))

LANES = 128
SUBLANES = 8
ROW_TILES = D_MODEL // LANES
FFT_N2 = 128
VMEM_LIMIT = 48 * 1024 * 1024

ROW_TILE = 256
ATTN_Q_TILE = 2048
ATTN_K_TILE = 2048
ATTN_UNIT = 512
ATTN_SUB = 512
S5_TIME_BLOCK = 64
MOE_TILE = 512
MOE_TOK_TILE = 256
FFT_COL_TILE = 4096


def _cp(*sem):
    return pltpu.CompilerParams(dimension_semantics=sem, vmem_limit_bytes=VMEM_LIMIT)


def _sds(shape, dtype=F32):
    return jax.ShapeDtypeStruct(shape, dtype)


def _rms(x, g=None):
    y = x * lax.rsqrt(jnp.mean(x * x, axis=-1, keepdims=True) + NORM_EPS)
    return y if g is None else y * g


def _inproj_body(x_ref, g_ref, w_ref, *o_refs):
    u = _rms(x_ref[...], g_ref[...]).astype(BF16)
    for o_ref, (lo, hi) in zip(o_refs, IN_SPLITS):
        o_ref[...] = jnp.dot(u, w_ref[:, lo:hi], preferred_element_type=F32)


def in_proj(h, g, w):
    t = h.shape[0]
    tm = min(ROW_TILE, t)
    return pl.pallas_call(
        _inproj_body,
        grid=(t // tm,),
        in_specs=[pl.BlockSpec((tm, D_MODEL), lambda i: (i, 0)),
                  pl.BlockSpec((1, D_MODEL), lambda i: (0, 0)),
                  pl.BlockSpec((D_MODEL, IN_COLS), lambda i: (0, 0))],
        out_specs=[pl.BlockSpec((tm, hi - lo), lambda i: (i, 0)) for lo, hi in IN_SPLITS],
        out_shape=[_sds((t, hi - lo)) for lo, hi in IN_SPLITS],
        compiler_params=_cp("parallel"),
        name="in_proj",
    )(h, g, w)


def s5_tables(lam_re, lam_im, log_dt, b_re, b_im, c_re, c_im):
    lr = jnp.minimum(lam_re, -1e-4)
    li = lam_im
    dt = jnp.exp(log_dt)[..., None]
    mag = jnp.exp(lr * dt)
    ar = mag * jnp.cos(li * dt)
    ai = mag * jnp.sin(li * dt)
    den = lr * lr + li * li
    cr = ((ar - 1.0) * lr + ai * li) / den
    ci = (ai * lr - (ar - 1.0) * li) / den
    bbr = cr[..., None] * b_re - ci[..., None] * b_im
    bbi = cr[..., None] * b_im + ci[..., None] * b_re
    eye = jnp.eye(S5_GROUPS, dtype=F32)
    bb = jnp.stack([bbr, bbi], axis=1)
    wb = jnp.einsum("dpgnh,gk->dghpkn", bb, eye).reshape(2 * S5_COLS, 2 * S5_NSTATE)
    cc = jnp.stack([c_re, -c_im], axis=0)
    wc = jnp.einsum("pdghn,gk->pgndkh", cc, eye).reshape(2 * S5_NSTATE, 2 * S5_COLS)
    a_re = jnp.repeat(ar.reshape(2, S5_NSTATE), 4, axis=0)
    a_im = jnp.repeat(ai.reshape(2, S5_NSTATE), 4, axis=0)
    return wb.astype(BF16), wc.astype(BF16), a_re, a_im


def _s5_body(u_ref, wb_ref, wc_ref, ar_ref, ai_ref, y_ref, bu_ref, xr_ref, xi_ref, *, tb):
    @pl.when(pl.program_id(0) == 0)
    def _():
        xr_ref[...] = jnp.zeros_like(xr_ref)
        xi_ref[...] = jnp.zeros_like(xi_ref)

    u = u_ref[...]
    fwd = (lax.broadcasted_iota(I32, (tb * SUBLANES, 1), 0) % SUBLANES) < 4
    uu = jnp.concatenate([jnp.where(fwd, u, 0.0), jnp.where(fwd, 0.0, u)], axis=1).astype(BF16)
    bu_ref[...] = jnp.dot(uu, wb_ref[...], preferred_element_type=F32)
    ar = ar_ref[...]
    ai = ai_ref[...]

    def step(t, carry):
        xr, xi = carry
        r0 = pl.multiple_of(t * SUBLANES, SUBLANES)
        rows = pl.ds(r0, SUBLANES)
        nxr = ar * xr - ai * xi + bu_ref[rows, 0:S5_NSTATE]
        nxi = ar * xi + ai * xr + bu_ref[rows, S5_NSTATE:2 * S5_NSTATE]
        bu_ref[rows, 0:S5_NSTATE] = nxr
        bu_ref[rows, S5_NSTATE:2 * S5_NSTATE] = nxi
        return nxr, nxi

    xr, xi = lax.fori_loop(0, tb, step, (xr_ref[...], xi_ref[...]))
    xr_ref[...] = xr
    xi_ref[...] = xi
    y2 = jnp.dot(bu_ref[...].astype(BF16), wc_ref[...], preferred_element_type=F32)
    y_ref[...] = jnp.where(fwd, y2[:, :S5_COLS], y2[:, S5_COLS:])


def s5_scan(u8, wb, wc, a_re, a_im, seq):
    tb = min(S5_TIME_BLOCK, seq)
    rows = tb * SUBLANES
    const = lambda i: (0, 0)
    return pl.pallas_call(
        functools.partial(_s5_body, tb=tb),
        grid=(seq // tb,),
        in_specs=[pl.BlockSpec((rows, S5_COLS), lambda i: (i, 0)),
                  pl.BlockSpec((2 * S5_COLS, 2 * S5_NSTATE), const),
                  pl.BlockSpec((2 * S5_NSTATE, 2 * S5_COLS), const),
                  pl.BlockSpec((SUBLANES, S5_NSTATE), const),
                  pl.BlockSpec((SUBLANES, S5_NSTATE), const)],
        out_specs=pl.BlockSpec((rows, S5_COLS), lambda i: (i, 0)),
        out_shape=_sds((seq * SUBLANES, S5_COLS)),
        scratch_shapes=[pltpu.VMEM((rows, 2 * S5_NSTATE), F32),
                        pltpu.VMEM((SUBLANES, S5_NSTATE), F32),
                        pltpu.VMEM((SUBLANES, S5_NSTATE), F32)],
        compiler_params=_cp("arbitrary"),
        name="s5_scan",
    )(u8, wb, wc, a_re, a_im)


def _s5_out_body(yf_ref, yb_ref, u_ref, d_ref, w_ref, b_ref, o_ref):
    y = jax.nn.gelu(yf_ref[...] + yb_ref[...] + u_ref[...] * d_ref[...])
    gate = jnp.dot(y.astype(BF16), w_ref[...], preferred_element_type=F32) + b_ref[...]
    o_ref[...] = y * jax.nn.sigmoid(gate)


def s5_out(yf, yb, u, d, w_glu, b_glu):
    t = u.shape[0]
    tm = min(ROW_TILE, t)
    row = pl.BlockSpec((tm, S5_COLS), lambda i: (i, 0))
    vec = pl.BlockSpec((1, S5_COLS), lambda i: (0, 0))
    return pl.pallas_call(
        _s5_out_body,
        grid=(t // tm,),
        in_specs=[row, row, row, vec, pl.BlockSpec((S5_COLS, S5_COLS), lambda i: (0, 0)), vec],
        out_specs=row,
        out_shape=_sds((t, S5_COLS)),
        compiler_params=_cp("parallel"),
        name="s5_out",
    )(yf, yb, u, d, w_glu, b_glu)


def s5_mixer(z_s5, tabs, d, w_glu, b_glu, batch, seq):
    assert batch == 4, "the scan packs 4 sequences x 2 directions onto the 8 sublanes"
    u = z_s5.reshape(batch, seq, S5_COLS)
    u8 = jnp.concatenate([u.transpose(1, 0, 2), u[:, ::-1].transpose(1, 0, 2)], axis=1)
    y8 = s5_scan(u8.reshape(seq * SUBLANES, S5_COLS), *tabs, seq).reshape(seq, SUBLANES, S5_COLS)
    yf = y8[:, :4].transpose(1, 0, 2).reshape(batch * seq, S5_COLS)
    yb = y8[::-1, 4:].transpose(1, 0, 2).reshape(batch * seq, S5_COLS)
    return s5_out(yf, yb, z_s5, d, w_glu, b_glu)


def _prep_body(z_ref, qg_ref, kg_ref, c_ref, s_ref, q_ref, k_ref, v_ref, *, nq, nk, nv, rope):
    def rot(x):
        if not rope:
            return x
        half = HEAD_DIM // 2
        swapped = jnp.concatenate([x[:, half:], x[:, :half]], axis=1)
        return x * c_ref[...] + swapped * s_ref[...]

    qg = qg_ref[...]
    kg = kg_ref[...]
    for h in range(nq):
        x = z_ref[:, h * HEAD_DIM:(h + 1) * HEAD_DIM]
        q_ref[0, h] = (rot(_rms(x, qg)) * (ATTN_SCALE * LOG2E)).astype(BF16)
    for h in range(nk):
        x = z_ref[:, (nq + h) * HEAD_DIM:(nq + h + 1) * HEAD_DIM]
        k_ref[0, h] = rot(_rms(x, kg)).astype(BF16)
    tm = z_ref.shape[0]
    ones_col = jnp.where(lax.broadcasted_iota(I32, (tm, LANES - HEAD_DIM), 1) == 0, 1.0, 0.0)
    for h in range(nv):
        lo = (nq + nk + h) * HEAD_DIM
        v_ref[0, h] = jnp.concatenate([z_ref[:, lo:lo + HEAD_DIM], ones_col], axis=1).astype(BF16)


def attn_prep(z, qg, kg, cos_t, sin_t, batch, seq, nq, nk, nv, rope):
    width = (nq + nk + nv) * HEAD_DIM
    tm = min(ROW_TILE, seq)
    nb = seq // tm
    vec = pl.BlockSpec((1, HEAD_DIM), lambda b, i: (0, 0))
    tab = pl.BlockSpec((tm, HEAD_DIM), lambda b, i: (i, 0))
    heads = lambda n, w: pl.BlockSpec((1, n, tm, w), lambda b, i: (b, 0, i, 0))
    return pl.pallas_call(
        functools.partial(_prep_body, nq=nq, nk=nk, nv=nv, rope=rope),
        grid=(batch, nb),
        in_specs=[pl.BlockSpec((tm, width), lambda b, i: (b * nb + i, 0)), vec, vec, tab, tab],
        out_specs=[heads(nq, HEAD_DIM), heads(nk, HEAD_DIM), heads(nv, LANES)],
        out_shape=[_sds((batch, nq, seq, HEAD_DIM), BF16), _sds((batch, nk, seq, HEAD_DIM), BF16),
                   _sds((batch, nv, seq, LANES), BF16)],
        compiler_params=_cp("parallel", "parallel"),
        name="attn_prep",
    )(z, qg, kg, cos_t, sin_t)


def _flash_body(q_ref, k_ref, v_ref, *rest, unit, sub, has_bias):
    if has_bias:
        b_ref, o_ref, m_ref, acc_ref = rest
    else:
        o_ref, m_ref, acc_ref = rest
    i = pl.program_id(2)
    j = pl.program_id(3)
    tq = q_ref.shape[2]
    tk = k_ref.shape[2]

    @pl.when(j == 0)
    def _():
        m_ref[...] = jnp.full_like(m_ref, -jnp.inf)
        acc_ref[...] = jnp.zeros_like(acc_ref)

    k = k_ref[0, 0]
    v = v_ref[0, 0]
    for g in range(tq // sub):
        r0 = g * sub
        s = lax.dot_general(q_ref[0, 0, r0:r0 + sub, :], k, (((1,), (1,)), ((), ())),
                            preferred_element_type=F32)
        if has_bias:
            row_unit = i * (tq // unit) + r0 // unit
            tiles = []
            for c in range(tk // unit):
                sel = jnp.clip(j * (tk // unit) + c - row_unit, -2, 2) + 2
                tiles.append(b_ref[0, sel, r0 % unit:r0 % unit + sub, :])
            s = s + (tiles[0] if len(tiles) == 1 else jnp.concatenate(tiles, axis=1))
        blocks = [s[:, b * LANES:(b + 1) * LANES] for b in range(tk // LANES)]
        bmax = blocks[0]
        for blk in blocks[1:]:
            bmax = jnp.maximum(bmax, blk)
        m_prev = m_ref[r0:r0 + sub, :]
        m_new = jnp.maximum(m_prev, jnp.max(bmax, axis=-1, keepdims=True))
        alpha = jnp.exp2(m_prev - m_new)
        p = jnp.concatenate([jnp.exp2(blk - m_new) for blk in blocks], axis=1).astype(BF16)
        acc_ref[r0:r0 + sub, :] = alpha * acc_ref[r0:r0 + sub, :] + jnp.dot(p, v, preferred_element_type=F32)
        m_ref[r0:r0 + sub, :] = m_new

    @pl.when(j == pl.num_programs(3) - 1)
    def _():
        acc = acc_ref[...]
        o_ref[0, 0] = acc[:, :HEAD_DIM] / acc[:, HEAD_DIM:HEAD_DIM + 1]


def flash_attention(q, k, v, k_of_q, v_of_q, bias=None):
    batch, nq, seq, _ = q.shape
    tq = min(ATTN_Q_TILE, seq)
    tk = min(ATTN_K_TILE, seq)
    unit = min(ATTN_UNIT, seq)
    sub = min(ATTN_SUB, seq)
    in_specs = [pl.BlockSpec((1, 1, tq, HEAD_DIM), lambda b, h, i, j: (b, h, i, 0)),
                pl.BlockSpec((1, 1, tk, HEAD_DIM), lambda b, h, i, j: (b, k_of_q(h), j, 0)),
                pl.BlockSpec((1, 1, tk, LANES), lambda b, h, i, j: (b, v_of_q(h), j, 0))]
    args = [q, k, v]
    if bias is not None:
        assert bias.shape[2] == unit
        in_specs.append(pl.BlockSpec((1, 5, unit, unit), lambda b, h, i, j: (v_of_q(h), 0, 0, 0)))
        args.append(bias)
    return pl.pallas_call(
        functools.partial(_flash_body, unit=unit, sub=sub, has_bias=bias is not None),
        grid=(batch, nq, seq // tq, seq // tk),
        in_specs=in_specs,
        out_specs=pl.BlockSpec((1, 1, tq, HEAD_DIM), lambda b, h, i, j: (b, h, i, 0)),
        out_shape=_sds((batch, nq, seq, HEAD_DIM)),
        scratch_shapes=[pltpu.VMEM((tq, LANES), F32), pltpu.VMEM((tq, LANES), F32)],
        compiler_params=_cp("parallel", "parallel", "parallel", "arbitrary"),
        name="flash_bias" if bias is not None else "flash",
    )(*args)


def _t5_bucket_table(t):
    rel = np.arange(-(3 * t - 1), 3 * t)
    half = NUM_BUCKETS // 2
    max_exact = half // 2
    ret = np.where(rel > 0, half, 0)
    n = np.abs(rel)
    nf = np.maximum(n, 1).astype(np.float64)
    large = max_exact + (np.log(nf / max_exact) / math.log(MAX_DISTANCE / max_exact) * (half - max_exact)).astype(np.int64)
    large = np.minimum(large, half - 1)
    return ret + np.where(n < max_exact, n, large)


def diff_bias_tiles(rel_bias, t):
    assert t >= MAX_DISTANCE, "offsets of two or more tiles must lie in the saturated buckets"
    heads = rel_bias.shape[1]
    vec = rel_bias[jnp.asarray(_t5_bucket_table(t), dtype=I32)].T * LOG2E
    tiles = []
    for d in range(-2, 3):
        lo = d * t + 2 * t
        w = vec[:, lo:lo + 2 * t - 1]
        u = jnp.concatenate([w[:, t - 1:], jnp.zeros((heads, 1), F32), w[:, :t - 1]], axis=1)
        skew = jnp.tile(u, (1, t))[:, :t * (2 * t - 1)].reshape(heads, t, 2 * t - 1)
        tiles.append(skew[:, :, :t])
    return jnp.stack(tiles, axis=1)


def rope_tables(seq):
    rows = seq // GRID_W
    row = jnp.repeat(jnp.arange(rows, dtype=F32), GRID_W)
    col = jnp.tile(jnp.arange(GRID_W, dtype=F32), rows)
    freq = ROPE_THETA ** (-jnp.arange(0, ROPE_AXIS_DIM, 2, dtype=F32) / ROPE_AXIS_DIM)
    ang = jnp.concatenate([row[:, None] * freq, col[:, None] * freq], -1)
    cos, sin = jnp.cos(ang), jnp.sin(ang)
    return jnp.concatenate([cos, cos], -1), jnp.concatenate([-sin, sin], -1)


def _conv3_body(x0, x1, x2, w0, w1, w2, b0, b1, b2, o0, o1, o2):
    for x_ref, w_ref, b_ref, o_ref in ((x0, w0, b0, o0), (x1, w1, b1, o1), (x2, w2, b2, o2)):
        x = x_ref[0]
        n = x.shape[0]
        t = lax.broadcasted_iota(I32, (n, 1), 0)
        prev = jnp.where(t == 0, 0.0, pltpu.roll(x, 1, 0))
        nxt = jnp.where(t == n - 1, 0.0, pltpu.roll(x, n - 1, 0))
        o_ref[0] = w_ref[0:1] * prev + w_ref[1:2] * x + w_ref[2:3] * nxt + b_ref[...]


def hyena_conv3(z_hy, conv_w, conv_b, batch, seq):
    z = z_hy.reshape(batch, seq, HY_COLS)
    nc = HY_WIDTH // LANES
    x_spec = lambda g: pl.BlockSpec((1, seq, LANES), lambda b, c: (b, 0, g * nc + c))
    w_spec = lambda g: pl.BlockSpec((3, LANES), lambda b, c: (0, g * nc + c))
    b_spec = lambda g: pl.BlockSpec((1, LANES), lambda b, c: (0, g * nc + c))
    out = pl.BlockSpec((1, seq, LANES), lambda b, c: (b, 0, c))
    return pl.pallas_call(
        _conv3_body,
        grid=(batch, nc),
        in_specs=[x_spec(0), x_spec(1), x_spec(2), w_spec(0), w_spec(1), w_spec(2),
                  b_spec(0), b_spec(1), b_spec(2)],
        out_specs=[out, out, out],
        out_shape=[_sds((batch, seq, HY_WIDTH))] * 3,
        compiler_params=_cp("parallel", "parallel"),
        name="hyena_conv3",
    )(z, z, z, conv_w, conv_w, conv_w, conv_b, conv_b, conv_b)


def _filt_body(feat_ref, dist_ref, w1, b1, f1, w2, b2, f2, w3, dec, h_ref, s_ref):
    dot = functools.partial(jnp.dot, precision=HI, preferred_element_type=F32)
    hid = jnp.sin(f1[...] * (dot(feat_ref[...], w1[...]) + b1[...]))
    hid = jnp.sin(f2[...] * (dot(hid, w2[...]) + b2[...]))
    h = dot(hid, w3[...]) * jnp.exp(-dist_ref[...] * jnp.abs(dec[...]))
    h_ref[...] = h

    @pl.when(pl.program_id(0) == 0)
    def _():
        s_ref[...] = jnp.zeros_like(s_ref)

    s_ref[...] += jnp.sum(jnp.abs(h), axis=0, keepdims=True)


def hyena_filter(seq, w1, b1, f1, w2, b2, f2, w3, decay):
    j = jnp.arange(seq, dtype=F32)
    tt = j / seq
    ang = 2.0 * jnp.pi * tt[:, None] * jnp.arange(1, HY_BANDS + 1, dtype=F32)
    feat = jnp.concatenate([tt[:, None], jnp.cos(ang), jnp.sin(ang)], -1)
    feat = jnp.pad(feat, ((0, 0), (0, LANES - HY_EMB)))
    half = seq // 2
    dist = (jnp.abs(j - half) / half)[:, None]
    w1p = jnp.pad(w1, ((0, LANES - HY_EMB), (0, 0)))
    tm = min(1024, seq)
    hw = HY_ORDER * HY_WIDTH
    const = lambda i: (0, 0)
    return pl.pallas_call(
        _filt_body,
        grid=(seq // tm,),
        in_specs=[pl.BlockSpec((tm, LANES), lambda i: (i, 0)), pl.BlockSpec((tm, 1), lambda i: (i, 0)),
                  pl.BlockSpec((LANES, HY_FFN), const), pl.BlockSpec((1, HY_FFN), const),
                  pl.BlockSpec((1, HY_FFN), const), pl.BlockSpec((HY_FFN, HY_FFN), const),
                  pl.BlockSpec((1, HY_FFN), const), pl.BlockSpec((1, HY_FFN), const),
                  pl.BlockSpec((HY_FFN, hw), const), pl.BlockSpec((1, hw), const)],
        out_specs=[pl.BlockSpec((tm, hw), lambda i: (i, 0)), pl.BlockSpec((1, hw), const)],
        out_shape=[_sds((seq, hw)), _sds((1, hw))],
        compiler_params=_cp("arbitrary"),
        name="hyena_filter",
    )(feat, dist, w1p, b1[None], f1[None], w2, b2[None], f2[None], w3, decay.reshape(1, hw))


def dft_tables(seq):
    n = 2 * seq
    n2 = FFT_N2
    n1 = n // n2
    n1h = n1 // 2
    nk = -(-(n1h + 1) // 4) * 4
    k1 = jnp.arange(nk, dtype=I32)
    live = (k1 <= n1h).astype(F32)
    ang = (2.0 * jnp.pi / n1) * ((k1[:, None] * jnp.arange(n1h, dtype=I32)[None, :]) % n1).astype(F32)
    outer_fwd = jnp.concatenate([jnp.cos(ang), -jnp.sin(ang)], axis=0) * jnp.tile(live, 2)[:, None]
    t1 = jnp.arange(n1h, dtype=I32) + n1 // 4
    ang = (2.0 * jnp.pi / n1) * ((t1[:, None] * k1[None, :]) % n1).astype(F32)
    weight = live * jnp.where((k1 == 0) | (k1 == n1h), 1.0, 2.0) / n
    outer_inv = jnp.concatenate([jnp.cos(ang), -jnp.sin(ang)], axis=1) * jnp.tile(weight, 2)[None, :]
    k2 = jnp.arange(n2, dtype=I32)
    t2 = jnp.arange(n2, dtype=I32)
    phase = (t2[None, None, :] * (k2[None, :, None] * n1 + k1[:, None, None])) % n
    ang = (2.0 * jnp.pi / n) * phase.astype(F32)
    fr, fi = jnp.cos(ang), -jnp.sin(ang)
    inner = jnp.concatenate([jnp.concatenate([fr, -fi], axis=2),
                             jnp.concatenate([fi, fr], axis=2)], axis=1)
    return outer_fwd, outer_inv, inner, jnp.swapaxes(inner, 1, 2)


def _fft_outer_body(f_ref, x_ref, o_ref):
    o_ref[0] = jnp.dot(f_ref[...], x_ref[0], precision=HI, preferred_element_type=F32)


def fft_outer(x, table):
    batch, n1h, cols = x.shape
    rows = table.shape[0]
    tc = min(FFT_COL_TILE, cols)
    return pl.pallas_call(
        _fft_outer_body,
        grid=(batch, cols // tc),
        in_specs=[pl.BlockSpec((rows, n1h), lambda b, c: (0, 0)),
                  pl.BlockSpec((1, n1h, tc), lambda b, c: (b, 0, c))],
        out_specs=pl.BlockSpec((1, rows, tc), lambda b, c: (b, 0, c)),
        out_shape=_sds((batch, rows, cols)),
        compiler_params=_cp("parallel", "parallel"),
        name="fft_outer",
    )(table, x)


def _fft_filter_body(m_ref, s_ref, a_ref, o_ref):
    n2 = FFT_N2
    a = a_ref[0, :, 0].reshape(2 * n2, HY_WIDTH)
    x = jnp.dot(m_ref[0], a, precision=HI, preferred_element_type=F32) * (1.0 / s_ref[...])
    o_ref[:, 0] = x.reshape(2, n2, HY_WIDTH)


def fft_filter(a5, inner, s):
    _, _, n1, n2, ch = a5.shape
    return pl.pallas_call(
        _fft_filter_body,
        grid=(n1, ch // HY_WIDTH),
        in_specs=[pl.BlockSpec((1, 2 * n2, 2 * n2), lambda k, c: (k, 0, 0)),
                  pl.BlockSpec((1, HY_WIDTH), lambda k, c: (0, c)),
                  pl.BlockSpec((1, 2, 1, n2, HY_WIDTH), lambda k, c: (0, 0, k, 0, c))],
        out_specs=pl.BlockSpec((2, 1, n2, HY_WIDTH), lambda k, c: (0, k, 0, c)),
        out_shape=_sds((2, n1, n2, ch)),
        compiler_params=_cp("parallel", "parallel"),
        name="fft_filter",
    )(inner, s, a5)


def _fft_mid_body(m_ref, mt_ref, h_ref, a_ref, o_ref, *, nb):
    n2 = FFT_N2
    dot = functools.partial(jnp.dot, precision=HI, preferred_element_type=F32)
    m = m_ref[0]
    mt = mt_ref[0]
    hr = h_ref[0, 0]
    hi = h_ref[1, 0]
    for b in range(nb):
        x = dot(m, a_ref[b, :, 0].reshape(2 * n2, HY_WIDTH))
        xr, xi = x[:n2], x[n2:]
        y = jnp.concatenate([xr * hr - xi * hi, xr * hi + xi * hr], axis=0)
        o_ref[b, :, 0] = dot(mt, y).reshape(2, n2, HY_WIDTH)


def fft_mid(a5, inner, inner_t, hf, order):
    nb, _, n1, n2, ch = a5.shape
    mat = pl.BlockSpec((1, 2 * n2, 2 * n2), lambda k: (k, 0, 0))
    blk = pl.BlockSpec((nb, 2, 1, n2, ch), lambda k: (0, 0, k, 0, 0))
    return pl.pallas_call(
        functools.partial(_fft_mid_body, nb=nb),
        grid=(n1,),
        in_specs=[mat, mat, pl.BlockSpec((2, 1, n2, ch), lambda k: (0, k, 0, order)), blk],
        out_specs=blk,
        out_shape=_sds(a5.shape),
        compiler_params=_cp("parallel"),
        name="fft_mid",
    )(inner, inner_t, hf, a5)


def _fft_inv_body(g_ref, b_ref, gate_ref, y_ref, fb_ref, o_ref):
    c = jnp.dot(g_ref[...], b_ref[0], precision=HI, preferred_element_type=F32)
    o_ref[0] = gate_ref[0] * (c + fb_ref[...] * y_ref[0])


def fft_inv_gate(bmat, table, gate, y, fbias):
    batch, rows, cols = bmat.shape
    n1h = table.shape[0]
    tc = min(FFT_COL_TILE, cols)
    fb = jnp.tile(fbias.reshape(1, HY_WIDTH), (1, tc // HY_WIDTH))
    blk = pl.BlockSpec((1, n1h, tc), lambda b, c: (b, 0, c))
    return pl.pallas_call(
        _fft_inv_body,
        grid=(batch, cols // tc),
        in_specs=[pl.BlockSpec((n1h, rows), lambda b, c: (0, 0)),
                  pl.BlockSpec((1, rows, tc), lambda b, c: (b, 0, c)), blk, blk,
                  pl.BlockSpec((1, tc), lambda b, c: (0, 0))],
        out_specs=blk,
        out_shape=_sds((batch, n1h, cols)),
        compiler_params=_cp("parallel", "parallel"),
        name="fft_inv_gate",
    )(table, bmat, gate, y, fb)


def hyena_spectrum(seq, tabs, w1, b1, f1, w2, b2, f2, w3, decay):
    outer_fwd, _, inner, _ = tabs
    nk, n1h = outer_fwd.shape[0] // 2, outer_fwd.shape[1]
    hw = HY_ORDER * HY_WIDTH
    h, s = hyena_filter(seq, w1, b1, f1, w2, b2, f2, w3, decay)
    a = fft_outer(h.reshape(1, n1h, FFT_N2 * hw), outer_fwd)
    return fft_filter(a.reshape(1, 2, nk, FFT_N2, hw), inner, s)


def hyena_mixer(z_hy, conv_w, conv_b, hf, f_bias, tabs, batch, seq):
    outer_fwd, outer_inv, inner, inner_t = tabs
    nk, n1h = outer_fwd.shape[0] // 2, outer_fwd.shape[1]
    cols = FFT_N2 * HY_WIDTH
    v, g0, g1 = hyena_conv3(z_hy, conv_w, conv_b, batch, seq)
    y = v.reshape(batch, n1h, cols)
    for order, gate in enumerate((g0, g1)):
        a = fft_outer(y, outer_fwd).reshape(batch, 2, nk, FFT_N2, HY_WIDTH)
        bm = fft_mid(a, inner, inner_t, hf, order).reshape(batch, 2 * nk, cols)
        y = fft_inv_gate(bm, outer_inv, gate.reshape(batch, n1h, cols), y, f_bias[order])
    return y.reshape(batch * seq, HY_WIDTH)


def _mix_body(lam_ref, h_ref, s5_ref, gqa_ref, diff_ref, hy_ref, gs5, ggqa, gsub, ghy, w_ref, o_ref, *, keep):
    lam = lam_ref[0]
    a = _rms(s5_ref[...], gs5[...])
    b = _rms(jnp.concatenate([gqa_ref[0, h] for h in range(GQA_HEADS)], axis=1), ggqa[...])
    c = jnp.concatenate(
        [_rms(diff_ref[0, 2 * h] - lam * diff_ref[0, 2 * h + 1], gsub[...]) * keep for h in range(DIFF_HEADS)],
        axis=1)
    d = _rms(hy_ref[...], ghy[...])
    mixed = jnp.concatenate([a, b, c, d], axis=1).astype(BF16)
    o_ref[...] = h_ref[...] + jnp.dot(mixed, w_ref[...], preferred_element_type=F32)


def mix_out(lam, h, y_s5, o_gqa, o_diff, y_hy, gs5, ggqa, gsub, ghy, w_out, batch, seq, lambda_init):
    tm = min(ROW_TILE, seq)
    nb = seq // tm
    row = lambda w: pl.BlockSpec((tm, w), lambda b, i: (b * nb + i, 0))
    vec = lambda w: pl.BlockSpec((1, w), lambda b, i: (0, 0))
    heads = lambda n: pl.BlockSpec((1, n, tm, HEAD_DIM), lambda b, i: (b, 0, i, 0))
    return pl.pallas_call(
        functools.partial(_mix_body, keep=1.0 - lambda_init),
        grid=(batch, nb),
        in_specs=[pl.BlockSpec(memory_space=pltpu.SMEM), row(D_MODEL), row(GROUP_WIDTH), heads(GQA_HEADS),
                  heads(2 * DIFF_HEADS), row(GROUP_WIDTH), vec(GROUP_WIDTH), vec(GROUP_WIDTH), vec(HEAD_DIM),
                  vec(GROUP_WIDTH), pl.BlockSpec((D_MODEL, D_MODEL), lambda b, i: (0, 0))],
        out_specs=row(D_MODEL),
        out_shape=_sds((batch * seq, D_MODEL)),
        compiler_params=_cp("parallel", "parallel"),
        name="mix_out",
    )(lam, h, y_s5, o_gqa, o_diff, y_hy, gs5, ggqa, gsub, ghy, w_out)


def _router_body(h_ref, g_ref, w_ref, b_ref, xn_ref, ti_ref, tw_ref, cnt_ref):
    xn = _rms(h_ref[...], g_ref[...])
    xn_ref[...] = xn
    logits = jnp.dot(xn, w_ref[...], precision=HI, preferred_element_type=F32) + b_ref[...]
    lane = lax.broadcasted_iota(I32, logits.shape, 1)
    vals = logits
    tops, idxs = [], []
    hot = jnp.zeros(logits.shape, F32)
    for _ in range(TOP_K):
        m = jnp.max(vals, axis=-1, keepdims=True)
        idx = jnp.min(jnp.where(vals == m, lane, N_EXPERTS), axis=-1, keepdims=True)
        sel = lane == idx
        tops.append(m)
        idxs.append(idx)
        hot = hot + sel.astype(F32)
        vals = jnp.where(sel, -jnp.inf, vals)
    es = [jnp.exp(t - tops[0]) for t in tops]
    den = es[0] + es[1] + es[2] + es[3]
    ti_ref[...] = jnp.concatenate(idxs, axis=1)
    tw_ref[...] = jnp.concatenate([e / den for e in es], axis=1)

    @pl.when(pl.program_id(0) == 0)
    def _():
        cnt_ref[...] = jnp.zeros_like(cnt_ref)

    cnt_ref[...] += jnp.sum(hot, axis=0, keepdims=True)


def moe_router(h, g, w_router, b_router):
    t = h.shape[0]
    tm = min(ROW_TILE, t)
    const = lambda i: (0, 0)
    return pl.pallas_call(
        _router_body,
        grid=(t // tm,),
        in_specs=[pl.BlockSpec((tm, D_MODEL), lambda i: (i, 0)), pl.BlockSpec((1, D_MODEL), const),
                  pl.BlockSpec((D_MODEL, N_EXPERTS), const), pl.BlockSpec((1, N_EXPERTS), const)],
        out_specs=[pl.BlockSpec((tm, D_MODEL), lambda i: (i, 0)), pl.BlockSpec((tm, TOP_K), lambda i: (i, 0)),
                   pl.BlockSpec((tm, TOP_K), lambda i: (i, 0)), pl.BlockSpec((1, N_EXPERTS), const)],
        out_shape=[_sds((t, D_MODEL)), _sds((t, TOP_K), I32), _sds((t, TOP_K)), _sds((1, N_EXPERTS))],
        compiler_params=_cp("arbitrary"),
        name="moe_router",
    )(h, g, w_router, b_router)


def _rank_body(ti_ref, off_ref, pos_ref, carry_ref):
    @pl.when(pl.program_id(0) == 0)
    def _():
        carry_ref[...] = jnp.zeros_like(carry_ref)

    ti = ti_ref[...]
    tm = ti.shape[0]
    lane = lax.broadcasted_iota(I32, (tm, N_EXPERTS), 1)
    hots = [lane == ti[:, k:k + 1] for k in range(TOP_K)]
    hot = sum(h.astype(F32) for h in hots)
    r = lax.broadcasted_iota(I32, (tm, tm), 0)
    c = lax.broadcasted_iota(I32, (tm, tm), 1)
    below = jnp.where(r > c, 1.0, 0.0).astype(BF16)
    before = jnp.dot(below, hot.astype(BF16), preferred_element_type=F32) + carry_ref[...] + off_ref[...]
    pos = [jnp.sum(jnp.where(h, before, 0.0), axis=-1, keepdims=True) for h in hots]
    pos_ref[...] = jnp.concatenate(pos, axis=1).astype(I32)
    carry_ref[...] += jnp.sum(hot, axis=0, keepdims=True)


def moe_rank(top_i, offsets):
    t = top_i.shape[0]
    tm = min(ROW_TILE, t)
    return pl.pallas_call(
        _rank_body,
        grid=(t // tm,),
        in_specs=[pl.BlockSpec((tm, TOP_K), lambda i: (i, 0)), pl.BlockSpec((1, N_EXPERTS), lambda i: (0, 0))],
        out_specs=pl.BlockSpec((tm, TOP_K), lambda i: (i, 0)),
        out_shape=_sds((t, TOP_K), I32),
        scratch_shapes=[pltpu.VMEM((1, N_EXPERTS), F32)],
        compiler_params=_cp("arbitrary"),
        name="moe_rank",
    )(top_i, offsets)


def _row_copy(src, dst, sem):
    return pltpu.make_async_copy(src, dst, sem)


def _scatter_body(pos_ref, x_ref, zero_hbm, xs_hbm, buf, sem, *, tm):
    del zero_hbm
    x = x_ref[...]
    for s in range(ROW_TILES):
        buf[:, s, :] = x[:, s * LANES:(s + 1) * LANES]

    def start(r, c):
        for k in range(TOP_K):
            _row_copy(buf.at[r], xs_hbm.at[pos_ref[r * TOP_K + k]], sem).start()
        return c

    lax.fori_loop(0, tm, start, 0)

    def wait(r, c):
        for k in range(TOP_K):
            _row_copy(buf.at[r], xs_hbm.at[pos_ref[r * TOP_K + k]], sem).wait()
        return c

    lax.fori_loop(0, tm, wait, 0)


def moe_scatter(pos_flat, x, n_slots):
    t = x.shape[0]
    tm = min(MOE_TOK_TILE, t)
    zeros = jnp.zeros((n_slots, ROW_TILES, LANES), F32)
    return pl.pallas_call(
        functools.partial(_scatter_body, tm=tm),
        grid=(t // tm,),
        in_specs=[pl.BlockSpec((tm * TOP_K,), lambda i: (i,), memory_space=pltpu.SMEM),
                  pl.BlockSpec((tm, D_MODEL), lambda i: (i, 0)), pl.BlockSpec(memory_space=pl.ANY)],
        out_specs=pl.BlockSpec(memory_space=pl.ANY),
        out_shape=_sds((n_slots, ROW_TILES, LANES)),
        scratch_shapes=[pltpu.VMEM((tm, ROW_TILES, LANES), F32), pltpu.SemaphoreType.DMA(())],
        input_output_aliases={2: 0},
        compiler_params=_cp("arbitrary"),
        name="moe_scatter",
    )(pos_flat, x, zeros)


def _experts_body(te_ref, nv_ref, x_ref, wgu_ref, bgu_ref, wd_ref, bd_ref, o_ref):
    del te_ref
    j = pl.program_id(0)
    dff = wd_ref.shape[1]

    @pl.when(j < nv_ref[0])
    def _():
        x = jnp.concatenate([x_ref[:, s, :] for s in range(ROW_TILES)], axis=1).astype(BF16)
        hgu = jnp.dot(x, wgu_ref[0], preferred_element_type=F32) + bgu_ref[0]
        glu = jnp.minimum(hgu[:, :dff], SWIGLU_LIMIT)
        lin = jnp.clip(hgu[:, dff:], -SWIGLU_LIMIT, SWIGLU_LIMIT)
        act = glu * jax.nn.sigmoid(SWIGLU_ALPHA * glu) * (lin + 1.0)
        y = jnp.dot(act.astype(BF16), wd_ref[0], preferred_element_type=F32) + bd_ref[0]
        for s in range(ROW_TILES):
            o_ref[:, s, :] = y[:, s * LANES:(s + 1) * LANES]

    @pl.when(j >= nv_ref[0])
    def _():
        o_ref[...] = jnp.zeros_like(o_ref)


def moe_experts(tile_expert, n_valid, xs3, w_gu, b_gu, w_down, b_down):
    n_slots = xs3.shape[0]
    tm = MOE_TILE
    dff = w_down.shape[1]
    blk = pl.BlockSpec((tm, ROW_TILES, LANES), lambda j, te, nv: (j, 0, 0))
    return pl.pallas_call(
        _experts_body,
        grid_spec=pltpu.PrefetchScalarGridSpec(
            num_scalar_prefetch=2,
            grid=(n_slots // tm,),
            in_specs=[blk,
                      pl.BlockSpec((1, D_MODEL, 2 * dff), lambda j, te, nv: (te[j], 0, 0)),
                      pl.BlockSpec((1, 1, 2 * dff), lambda j, te, nv: (te[j], 0, 0)),
                      pl.BlockSpec((1, dff, D_MODEL), lambda j, te, nv: (te[j], 0, 0)),
                      pl.BlockSpec((1, 1, D_MODEL), lambda j, te, nv: (te[j], 0, 0))],
            out_specs=blk),
        out_shape=_sds((n_slots, ROW_TILES, LANES)),
        compiler_params=_cp("arbitrary"),
        name="moe_experts",
    )(tile_expert, n_valid, xs3, w_gu, b_gu, w_down, b_down)


def _combine_body(pos_ref, w_ref, ys_hbm, o_ref, b0, b1, b2, b3, sem, *, tm):
    bufs = (b0, b1, b2, b3)

    def start(r, c):
        for k in range(TOP_K):
            _row_copy(ys_hbm.at[pos_ref[r * TOP_K + k]], bufs[k].at[r], sem).start()
        return c

    lax.fori_loop(0, tm, start, 0)

    def wait(r, c):
        for k in range(TOP_K):
            _row_copy(ys_hbm.at[pos_ref[r * TOP_K + k]], bufs[k].at[r], sem).wait()
        return c

    lax.fori_loop(0, tm, wait, 0)

    def mix(r, c):
        acc = w_ref[r * TOP_K] * b0[r]
        for k in range(1, TOP_K):
            acc = acc + w_ref[r * TOP_K + k] * bufs[k][r]
        o_ref[r] = acc
        return c

    lax.fori_loop(0, tm, mix, 0)


def moe_combine(pos_flat, w_flat, ys3, t):
    tm = min(MOE_TOK_TILE, t)
    smem = pl.BlockSpec((tm * TOP_K,), lambda i: (i,), memory_space=pltpu.SMEM)
    return pl.pallas_call(
        functools.partial(_combine_body, tm=tm),
        grid=(t // tm,),
        in_specs=[smem, smem, pl.BlockSpec(memory_space=pl.ANY)],
        out_specs=pl.BlockSpec((tm, ROW_TILES, LANES), lambda i: (i, 0, 0)),
        out_shape=_sds((t, ROW_TILES, LANES)),
        scratch_shapes=[pltpu.VMEM((tm, ROW_TILES, LANES), F32)] * TOP_K + [pltpu.SemaphoreType.DMA(())],
        compiler_params=_cp("arbitrary"),
        name="moe_combine",
    )(pos_flat, w_flat, ys3)


def moe(h, g, w_router, b_router, w_gu, b_gu, w_down, b_down):
    t = h.shape[0]
    xn, top_i, top_w, counts = moe_router(h, g, w_router, b_router)
    n_tiles = (t * TOP_K) // MOE_TILE + N_EXPERTS
    padded = jnp.ceil(counts[0] / MOE_TILE) * MOE_TILE
    ends = jnp.cumsum(padded)
    offsets = (ends - padded)[None]
    tile_start = jnp.arange(n_tiles, dtype=F32) * MOE_TILE
    tile_expert = jnp.minimum(jnp.sum(ends[None, :] <= tile_start[:, None], axis=1), N_EXPERTS - 1).astype(I32)
    n_valid = (ends[-1:] / MOE_TILE).astype(I32)
    pos = moe_rank(top_i, offsets).reshape(t * TOP_K)
    xs3 = moe_scatter(pos, xn, n_tiles * MOE_TILE)
    ys3 = moe_experts(tile_expert, n_valid, xs3, w_gu, b_gu, w_down, b_down)
    return moe_combine(pos, top_w.reshape(t * TOP_K), ys3, t).reshape(t, D_MODEL)


def _ple_body(h_ref, m_ref, p_ref, wp_ref, wg_ref, o_ref):
    h = h_ref[...] + m_ref[...]
    e = jnp.dot(p_ref[...].astype(BF16), wp_ref[...], preferred_element_type=F32)
    gate = jnp.dot(_rms(h).astype(BF16), wg_ref[...], preferred_element_type=F32)
    o_ref[...] = h + e * jax.nn.sigmoid(gate)


def ple(h, moe_out, p, w_ple, w_gate):
    t = h.shape[0]
    tm = min(ROW_TILE, t)
    row = pl.BlockSpec((tm, D_MODEL), lambda i: (i, 0))
    return pl.pallas_call(
        _ple_body,
        grid=(t // tm,),
        in_specs=[row, row, pl.BlockSpec((tm, PLE_DIM), lambda i: (i, 0)),
                  pl.BlockSpec((PLE_DIM, D_MODEL), lambda i: (0, 0)),
                  pl.BlockSpec((D_MODEL, D_MODEL), lambda i: (0, 0))],
        out_specs=row,
        out_shape=_sds((t, D_MODEL)),
        compiler_params=_cp("parallel"),
        name="ple",
    )(h, moe_out, p, w_ple, w_gate)


def _deinterleave(n):
    return np.concatenate([np.arange(0, n, 2), np.arange(1, n, 2)])


def _halves(w):
    pairs = w.reshape(*w.shape[:-1], w.shape[-1] // 2, 2)
    return jnp.concatenate([pairs[..., 0], pairs[..., 1]], axis=-1)


def prepare_layer(i, prm):
    pair = _deinterleave(HEAD_DIM)
    cols = np.arange(IN_COLS)
    for h in range(GQA_HEADS + GQA_KV_HEADS):
        lo = OFF_GQA + h * HEAD_DIM
        cols[lo:lo + HEAD_DIM] = lo + pair
    lam = (jnp.exp(jnp.sum(prm["diff_lam_q1"][i] * prm["diff_lam_k1"][i]))
           - jnp.exp(jnp.sum(prm["diff_lam_q2"][i] * prm["diff_lam_k2"][i])) + _lambda_init(i))
    return dict(
        norm1_g=prm["norm1_g"][i][None],
        w_in=prm["w_in"][i][:, cols].astype(BF16),
        s5=s5_tables(prm["s5_lam_re"][i], prm["s5_lam_im"][i], prm["s5_log_dt"][i], prm["s5_b_re"][i],
                     prm["s5_b_im"][i], prm["s5_c_re"][i], prm["s5_c_im"][i]),
        s5_d=prm["s5_d"][i][None],
        s5_w_glu=prm["s5_w_glu"][i].astype(BF16),
        s5_b_glu=prm["s5_b_glu"][i][None],
        gqa_q_g=prm["gqa_q_g"][i][pair][None],
        gqa_k_g=prm["gqa_k_g"][i][pair][None],
        diff_q_g=prm["diff_q_g"][i][None],
        diff_k_g=prm["diff_k_g"][i][None],
        diff_lam=lam.reshape(1).astype(F32),
        diff_subln_g=prm["diff_subln_g"][i][None],
        out_g_s5=prm["out_g_s5"][i][None],
        out_g_gqa=prm["out_g_gqa"][i][None],
        out_g_hy=prm["out_g_hy"][i][None],
        w_out=prm["w_out"][i].astype(BF16),
        norm2_g=prm["norm2_g"][i][None],
        w_router=prm["w_router"][i],
        b_router=prm["b_router"][i][None],
        w_gu=_halves(prm["w_gu"][i]).astype(BF16),
        b_gu=_halves(prm["b_gu"][i])[:, None, :],
        w_down=prm["w_down"][i].astype(BF16),
        b_down=prm["b_down"][i][:, None, :],
        w_ple=prm["w_ple"][i].astype(BF16),
        w_ple_gate=prm["w_ple_gate"][i].astype(BF16),
    )


def _lambda_init(i):
    return 0.8 - 0.6 * math.exp(-0.3 * i)


def run_trunk(x, p, prm, layers):
    batch, seq, _ = x.shape
    t = batch * seq
    cos_t, sin_t = rope_tables(seq)
    tabs = dft_tables(seq)
    bias = diff_bias_tiles(prm["rel_bias"], min(ATTN_UNIT, seq))
    h = x.reshape(t, D_MODEL)
    for i, lw in enumerate(layers):
        z_s5, z_gqa, z_diff, z_hy = in_proj(h, lw["norm1_g"], lw["w_in"])
        y_s5 = s5_mixer(z_s5, lw["s5"], lw["s5_d"], lw["s5_w_glu"], lw["s5_b_glu"], batch, seq)
        q, k, v = attn_prep(z_gqa, lw["gqa_q_g"], lw["gqa_k_g"], cos_t, sin_t, batch, seq,
                            GQA_HEADS, GQA_KV_HEADS, GQA_KV_HEADS, True)
        rep = GQA_HEADS // GQA_KV_HEADS
        o_gqa = flash_attention(q, k, v, lambda hq: hq // rep, lambda hq: hq // rep)
        q, k, v = attn_prep(z_diff, lw["diff_q_g"], lw["diff_k_g"], cos_t, sin_t, batch, seq,
                            2 * DIFF_HEADS, 2 * DIFF_HEADS, DIFF_HEADS, False)
        o_diff = flash_attention(q, k, v, lambda hq: hq, lambda hq: hq // 2, bias)
        hf = hyena_spectrum(seq, tabs, prm["hy_w1"][i], prm["hy_b1"][i], prm["hy_freq1"][i], prm["hy_w2"][i],
                            prm["hy_b2"][i], prm["hy_freq2"][i], prm["hy_w3"][i], prm["hy_decay"][i])
        y_hy = hyena_mixer(z_hy, prm["hy_conv_w"][i], prm["hy_conv_b"][i][None], hf, prm["hy_bias"][i],
                           tabs, batch, seq)
        h = mix_out(lw["diff_lam"], h, y_s5, o_gqa, o_diff, y_hy, lw["out_g_s5"], lw["out_g_gqa"],
                    lw["diff_subln_g"], lw["out_g_hy"], lw["w_out"], batch, seq, _lambda_init(i))
        m = moe(h, lw["norm2_g"], lw["w_router"], lw["b_router"], lw["w_gu"], lw["b_gu"], lw["w_down"],
                lw["b_down"])
        h = ple(h, m, p[i].reshape(t, PLE_DIM), lw["w_ple"], lw["w_ple_gate"])
    return h.reshape(batch, seq, D_MODEL)


def kernel(x_prompt, x_sample, p_prompt, p_sample, rel_bias, norm1_g, w_in, s5_lam_re, s5_lam_im, s5_log_dt,
           s5_b_re, s5_b_im, s5_c_re, s5_c_im, s5_d, s5_w_glu, s5_b_glu, gqa_q_g, gqa_k_g, diff_q_g, diff_k_g,
           diff_lam_q1, diff_lam_k1, diff_lam_q2, diff_lam_k2, diff_subln_g, hy_conv_w, hy_conv_b, hy_w1, hy_b1,
           hy_freq1, hy_w2, hy_b2, hy_freq2, hy_w3, hy_decay, hy_bias, out_g_s5, out_g_gqa, out_g_hy, w_out,
           norm2_g, w_router, b_router, w_gu, b_gu, w_down, b_down, w_ple, w_ple_gate):
    prm = dict(rel_bias=rel_bias, norm1_g=norm1_g, w_in=w_in, s5_lam_re=s5_lam_re, s5_lam_im=s5_lam_im,
               s5_log_dt=s5_log_dt, s5_b_re=s5_b_re, s5_b_im=s5_b_im, s5_c_re=s5_c_re, s5_c_im=s5_c_im,
               s5_d=s5_d, s5_w_glu=s5_w_glu, s5_b_glu=s5_b_glu, gqa_q_g=gqa_q_g, gqa_k_g=gqa_k_g,
               diff_q_g=diff_q_g, diff_k_g=diff_k_g, diff_lam_q1=diff_lam_q1, diff_lam_k1=diff_lam_k1,
               diff_lam_q2=diff_lam_q2, diff_lam_k2=diff_lam_k2, diff_subln_g=diff_subln_g,
               hy_conv_w=hy_conv_w, hy_conv_b=hy_conv_b, hy_w1=hy_w1, hy_b1=hy_b1, hy_freq1=hy_freq1,
               hy_w2=hy_w2, hy_b2=hy_b2, hy_freq2=hy_freq2, hy_w3=hy_w3, hy_decay=hy_decay, hy_bias=hy_bias,
               out_g_s5=out_g_s5, out_g_gqa=out_g_gqa, out_g_hy=out_g_hy, w_out=w_out, norm2_g=norm2_g,
               w_router=w_router, b_router=b_router, w_gu=w_gu, b_gu=b_gu, w_down=w_down, b_down=b_down,
               w_ple=w_ple, w_ple_gate=w_ple_gate)
    layers = [prepare_layer(i, prm) for i in range(DEPTH)]
    return (run_trunk(x_prompt, p_prompt, prm, layers), run_trunk(x_sample, p_sample, prm, layers))
```

```python
import functools
import math

import numpy as np
import jax
import jax.numpy as jnp
from jax import lax
from jax.experimental import pallas as pl
from jax.experimental.pallas import tpu as pltpu

F32 = jnp.float32
BF16 = jnp.bfloat16
I32 = jnp.int32
HI = lax.Precision.HIGHEST

D_MODEL = 1024
DEPTH = 2
GROUP_WIDTH = 256
HEAD_DIM = 64
S5_GROUP_DIM = 16
S5_GROUPS = 16
S5_STATE = 64
S5_NSTATE = S5_GROUPS * S5_STATE
GQA_HEADS = 4
GQA_KV_HEADS = 2
DIFF_HEADS = 4
HY_WIDTH = 256
HY_ORDER = 2
HY_BANDS = 16
HY_EMB = 2 * HY_BANDS + 1
HY_FFN = 64
N_EXPERTS = 32
TOP_K = 4
SWIGLU_LIMIT = 7.0
SWIGLU_ALPHA = 1.702
NUM_BUCKETS = 32
MAX_DISTANCE = 128
GRID_W = 64
ROPE_THETA = 10000.0
ROPE_AXIS_DIM = HEAD_DIM // 2
PLE_DIM = 256
NORM_EPS = 1e-6
ATTN_SCALE = HEAD_DIM ** -0.5
LOG2E = math.log2(math.e)

S5_COLS = GROUP_WIDTH
GQA_QW = GQA_HEADS * HEAD_DIM
GQA_KW = GQA_KV_HEADS * HEAD_DIM
GQA_COLS = GQA_QW + 2 * GQA_KW
DIFF_QW = DIFF_HEADS * 2 * HEAD_DIM
DIFF_COLS = 2 * DIFF_QW + DIFF_HEADS * HEAD_DIM
HY_COLS = (HY_ORDER + 1) * HY_WIDTH
OFF_GQA = S5_COLS
OFF_DIFF = OFF_GQA + GQA_COLS
OFF_HY = OFF_DIFF + DIFF_COLS
IN_COLS = OFF_HY + HY_COLS
IN_SPLITS = ((0, OFF_GQA), (OFF_GQA, OFF_DIFF), (OFF_DIFF, OFF_HY), (OFF_HY, IN_COLS))

LANES = 128
SUBLANES = 8
ROW_TILES = D_MODEL // LANES
FFT_N2 = 128
VMEM_LIMIT = 48 * 1024 * 1024

ROW_TILE = 256
ATTN_Q_TILE = 2048
ATTN_K_TILE = 2048
ATTN_UNIT = 512
ATTN_SUB = 512
S5_TIME_BLOCK = 64
MOE_TILE = 512
MOE_TOK_TILE = 256
MIX_UNROLL = 4


def _cp(*sem):
    return pltpu.CompilerParams(dimension_semantics=sem, vmem_limit_bytes=VMEM_LIMIT)


def _sds(shape, dtype=F32):
    return jax.ShapeDtypeStruct(shape, dtype)


def _rms(x, g=None):
    y = x * lax.rsqrt(jnp.mean(x * x, axis=-1, keepdims=True) + NORM_EPS)
    return y if g is None else y * g


def _inproj_body(x_ref, g_ref, w_ref, *o_refs):
    u = _rms(x_ref[...], g_ref[...]).astype(BF16)
    for o_ref, (lo, hi) in zip(o_refs, IN_SPLITS):
        o_ref[...] = jnp.dot(u, w_ref[:, lo:hi], preferred_element_type=F32)


def in_proj(h, g, w):
    t = h.shape[0]
    tm = min(ROW_TILE, t)
    return pl.pallas_call(
        _inproj_body,
        grid=(t // tm,),
        in_specs=[pl.BlockSpec((tm, D_MODEL), lambda i: (i, 0)),
                  pl.BlockSpec((1, D_MODEL), lambda i: (0, 0)),
                  pl.BlockSpec((D_MODEL, IN_COLS), lambda i: (0, 0))],
        out_specs=[pl.BlockSpec((tm, hi - lo), lambda i: (i, 0)) for lo, hi in IN_SPLITS],
        out_shape=[_sds((t, hi - lo)) for lo, hi in IN_SPLITS],
        compiler_params=_cp("parallel"),
        name="in_proj",
    )(h, g, w)


def s5_tables(lam_re, lam_im, log_dt, b_re, b_im, c_re, c_im):
    lr = jnp.minimum(lam_re, -1e-4)
    li = lam_im
    dt = jnp.exp(log_dt)[..., None]
    mag = jnp.exp(lr * dt)
    ar = mag * jnp.cos(li * dt)
    ai = mag * jnp.sin(li * dt)
    den = lr * lr + li * li
    cr = ((ar - 1.0) * lr + ai * li) / den
    ci = (ai * lr - (ar - 1.0) * li) / den
    bbr = cr[..., None] * b_re - ci[..., None] * b_im
    bbi = cr[..., None] * b_im + ci[..., None] * b_re
    eye = jnp.eye(S5_GROUPS, dtype=F32)
    bb = jnp.stack([bbr, bbi], axis=1)
    wb = jnp.einsum("dpgnh,gk->dghpkn", bb, eye).reshape(2 * S5_COLS, 2 * S5_NSTATE)
    cc = jnp.stack([c_re, -c_im], axis=0)
    wc = jnp.einsum("pdghn,gk->pgndkh", cc, eye).reshape(2 * S5_NSTATE, 2 * S5_COLS)
    a_re = jnp.repeat(ar.reshape(2, S5_NSTATE), 4, axis=0)
    a_im = jnp.repeat(ai.reshape(2, S5_NSTATE), 4, axis=0)
    return wb.astype(BF16), wc.astype(BF16), a_re, a_im


def _s5_body(u_ref, wb_ref, wc_ref, ar_ref, ai_ref, y_ref, bu_ref, xr_ref, xi_ref, *, tb):
    @pl.when(pl.program_id(0) == 0)
    def _():
        xr_ref[...] = jnp.zeros_like(xr_ref)
        xi_ref[...] = jnp.zeros_like(xi_ref)

    u = u_ref[...]
    fwd = (lax.broadcasted_iota(I32, (tb * SUBLANES, 1), 0) % SUBLANES) < 4
    uu = jnp.concatenate([jnp.where(fwd, u, 0.0), jnp.where(fwd, 0.0, u)], axis=1).astype(BF16)
    bu_ref[...] = jnp.dot(uu, wb_ref[...], preferred_element_type=F32)
    ar = ar_ref[...]
    ai = ai_ref[...]

    def step(t, carry):
        xr, xi = carry
        r0 = pl.multiple_of(t * SUBLANES, SUBLANES)
        rows = pl.ds(r0, SUBLANES)
        nxr = ar * xr - ai * xi + bu_ref[rows, 0:S5_NSTATE]
        nxi = ar * xi + ai * xr + bu_ref[rows, S5_NSTATE:2 * S5_NSTATE]
        bu_ref[rows, 0:S5_NSTATE] = nxr
        bu_ref[rows, S5_NSTATE:2 * S5_NSTATE] = nxi
        return nxr, nxi

    xr, xi = lax.fori_loop(0, tb, step, (xr_ref[...], xi_ref[...]))
    xr_ref[...] = xr
    xi_ref[...] = xi
    y2 = jnp.dot(bu_ref[...].astype(BF16), wc_ref[...], preferred_element_type=F32)
    y_ref[...] = jnp.where(fwd, y2[:, :S5_COLS], y2[:, S5_COLS:])


def s5_scan(u8, wb, wc, a_re, a_im, seq):
    tb = min(S5_TIME_BLOCK, seq)
    rows = tb * SUBLANES
    const = lambda i: (0, 0)
    return pl.pallas_call(
        functools.partial(_s5_body, tb=tb),
        grid=(seq // tb,),
        in_specs=[pl.BlockSpec((rows, S5_COLS), lambda i: (i, 0)),
                  pl.BlockSpec((2 * S5_COLS, 2 * S5_NSTATE), const),
                  pl.BlockSpec((2 * S5_NSTATE, 2 * S5_COLS), const),
                  pl.BlockSpec((SUBLANES, S5_NSTATE), const),
                  pl.BlockSpec((SUBLANES, S5_NSTATE), const)],
        out_specs=pl.BlockSpec((rows, S5_COLS), lambda i: (i, 0)),
        out_shape=_sds((seq * SUBLANES, S5_COLS)),
        scratch_shapes=[pltpu.VMEM((rows, 2 * S5_NSTATE), F32),
                        pltpu.VMEM((SUBLANES, S5_NSTATE), F32),
                        pltpu.VMEM((SUBLANES, S5_NSTATE), F32)],
        compiler_params=_cp("arbitrary"),
        name="s5_scan",
    )(u8, wb, wc, a_re, a_im)


def _s5_out_body(yf_ref, yb_ref, u_ref, d_ref, w_ref, b_ref, o_ref):
    y = jax.nn.gelu(yf_ref[...] + yb_ref[...] + u_ref[...] * d_ref[...])
    gate = jnp.dot(y.astype(BF16), w_ref[...], preferred_element_type=F32) + b_ref[...]
    o_ref[...] = y * jax.nn.sigmoid(gate)


def s5_out(yf, yb, u, d, w_glu, b_glu):
    t = u.shape[0]
    tm = min(ROW_TILE, t)
    row = pl.BlockSpec((tm, S5_COLS), lambda i: (i, 0))
    vec = pl.BlockSpec((1, S5_COLS), lambda i: (0, 0))
    return pl.pallas_call(
        _s5_out_body,
        grid=(t // tm,),
        in_specs=[row, row, row, vec, pl.BlockSpec((S5_COLS, S5_COLS), lambda i: (0, 0)), vec],
        out_specs=row,
        out_shape=_sds((t, S5_COLS)),
        compiler_params=_cp("parallel"),
        name="s5_out",
    )(yf, yb, u, d, w_glu, b_glu)


def s5_mixer(z_s5, tabs, d, w_glu, b_glu, batch, seq):
    assert batch == 4, "the scan packs 4 sequences x 2 directions onto the 8 sublanes"
    u = z_s5.reshape(batch, seq, S5_COLS)
    u8 = jnp.concatenate([u.transpose(1, 0, 2), u[:, ::-1].transpose(1, 0, 2)], axis=1)
    y8 = s5_scan(u8.reshape(seq * SUBLANES, S5_COLS), *tabs, seq).reshape(seq, SUBLANES, S5_COLS)
    yf = y8[:, :4].transpose(1, 0, 2).reshape(batch * seq, S5_COLS)
    yb = y8[::-1, 4:].transpose(1, 0, 2).reshape(batch * seq, S5_COLS)
    return s5_out(yf, yb, z_s5, d, w_glu, b_glu)


def _prep_body(z_ref, qg_ref, kg_ref, c_ref, s_ref, q_ref, k_ref, v_ref, *, nq, nk, nv, rope):
    def rot(x):
        if not rope:
            return x
        half = HEAD_DIM // 2
        swapped = jnp.concatenate([x[:, half:], x[:, :half]], axis=1)
        return x * c_ref[...] + swapped * s_ref[...]

    qg = qg_ref[...]
    kg = kg_ref[...]
    for h in range(nq):
        x = z_ref[:, h * HEAD_DIM:(h + 1) * HEAD_DIM]
        q_ref[0, h] = (rot(_rms(x, qg)) * (ATTN_SCALE * LOG2E)).astype(BF16)
    for h in range(nk):
        x = z_ref[:, (nq + h) * HEAD_DIM:(nq + h + 1) * HEAD_DIM]
        k_ref[0, h] = rot(_rms(x, kg)).astype(BF16)
    tm = z_ref.shape[0]
    ones_col = jnp.where(lax.broadcasted_iota(I32, (tm, LANES - HEAD_DIM), 1) == 0, 1.0, 0.0)
    for h in range(nv):
        lo = (nq + nk + h) * HEAD_DIM
        v_ref[0, h] = jnp.concatenate([z_ref[:, lo:lo + HEAD_DIM], ones_col], axis=1).astype(BF16)


def attn_prep(z, qg, kg, cos_t, sin_t, batch, seq, nq, nk, nv, rope):
    width = (nq + nk + nv) * HEAD_DIM
    tm = min(ROW_TILE, seq)
    nb = seq // tm
    vec = pl.BlockSpec((1, HEAD_DIM), lambda b, i: (0, 0))
    tab = pl.BlockSpec((tm, HEAD_DIM), lambda b, i: (i, 0))
    heads = lambda n, w: pl.BlockSpec((1, n, tm, w), lambda b, i: (b, 0, i, 0))
    return pl.pallas_call(
        functools.partial(_prep_body, nq=nq, nk=nk, nv=nv, rope=rope),
        grid=(batch, nb),
        in_specs=[pl.BlockSpec((tm, width), lambda b, i: (b * nb + i, 0)), vec, vec, tab, tab],
        out_specs=[heads(nq, HEAD_DIM), heads(nk, HEAD_DIM), heads(nv, LANES)],
        out_shape=[_sds((batch, nq, seq, HEAD_DIM), BF16), _sds((batch, nk, seq, HEAD_DIM), BF16),
                   _sds((batch, nv, seq, LANES), BF16)],
        compiler_params=_cp("parallel", "parallel"),
        name="attn_prep",
    )(z, qg, kg, cos_t, sin_t)


def _flash_body(q_ref, k_ref, v_ref, *rest, unit, sub, has_bias):
    if has_bias:
        b_ref, o_ref, m_ref, acc_ref = rest
    else:
        o_ref, m_ref, acc_ref = rest
    i = pl.program_id(2)
    j = pl.program_id(3)
    tq = q_ref.shape[2]
    tk = k_ref.shape[2]

    @pl.when(j == 0)
    def _():
        m_ref[...] = jnp.full_like(m_ref, -jnp.inf)
        acc_ref[...] = jnp.zeros_like(acc_ref)

    k = k_ref[0, 0]
    v = v_ref[0, 0]
    for g in range(tq // sub):
        r0 = g * sub
        s = lax.dot_general(q_ref[0, 0, r0:r0 + sub, :], k, (((1,), (1,)), ((), ())),
                            preferred_element_type=F32)
        if has_bias:
            row_unit = i * (tq // unit) + r0 // unit
            tiles = []
            for c in range(tk // unit):
                sel = jnp.clip(j * (tk // unit) + c - row_unit, -2, 2) + 2
                tiles.append(b_ref[0, sel, r0 % unit:r0 % unit + sub, :])
            s = s + (tiles[0] if len(tiles) == 1 else jnp.concatenate(tiles, axis=1))
        blocks = [s[:, b * LANES:(b + 1) * LANES] for b in range(tk // LANES)]
        bmax = blocks[0]
        for blk in blocks[1:]:
            bmax = jnp.maximum(bmax, blk)
        m_prev = m_ref[r0:r0 + sub, :]
        m_new = jnp.maximum(m_prev, jnp.max(bmax, axis=-1, keepdims=True))
        alpha = jnp.exp2(m_prev - m_new)
        p = jnp.concatenate([jnp.exp2(blk - m_new) for blk in blocks], axis=1).astype(BF16)
        acc_ref[r0:r0 + sub, :] = alpha * acc_ref[r0:r0 + sub, :] + jnp.dot(p, v, preferred_element_type=F32)
        m_ref[r0:r0 + sub, :] = m_new

    @pl.when(j == pl.num_programs(3) - 1)
    def _():
        acc = acc_ref[...]
        o_ref[0, 0] = acc[:, :HEAD_DIM] / acc[:, HEAD_DIM:HEAD_DIM + 1]


def flash_attention(q, k, v, k_of_q, v_of_q, bias=None):
    batch, nq, seq, _ = q.shape
    tq = min(ATTN_Q_TILE, seq)
    tk = min(ATTN_K_TILE, seq)
    unit = min(ATTN_UNIT, seq)
    sub = min(ATTN_SUB, seq)
    in_specs = [pl.BlockSpec((1, 1, tq, HEAD_DIM), lambda b, h, i, j: (b, h, i, 0)),
                pl.BlockSpec((1, 1, tk, HEAD_DIM), lambda b, h, i, j: (b, k_of_q(h), j, 0)),
                pl.BlockSpec((1, 1, tk, LANES), lambda b, h, i, j: (b, v_of_q(h), j, 0))]
    args = [q, k, v]
    if bias is not None:
        assert bias.shape[2] == unit
        in_specs.append(pl.BlockSpec((1, 5, unit, unit), lambda b, h, i, j: (v_of_q(h), 0, 0, 0)))
        args.append(bias)
    return pl.pallas_call(
        functools.partial(_flash_body, unit=unit, sub=sub, has_bias=bias is not None),
        grid=(batch, nq, seq // tq, seq // tk),
        in_specs=in_specs,
        out_specs=pl.BlockSpec((1, 1, tq, HEAD_DIM), lambda b, h, i, j: (b, h, i, 0)),
        out_shape=_sds((batch, nq, seq, HEAD_DIM)),
        scratch_shapes=[pltpu.VMEM((tq, LANES), F32), pltpu.VMEM((tq, LANES), F32)],
        compiler_params=_cp("parallel", "parallel", "parallel", "arbitrary"),
        name="flash_bias" if bias is not None else "flash",
    )(*args)


def _t5_bucket_table(t):
    rel = np.arange(-(3 * t - 1), 3 * t)
    half = NUM_BUCKETS // 2
    max_exact = half // 2
    ret = np.where(rel > 0, half, 0)
    n = np.abs(rel)
    nf = np.maximum(n, 1).astype(np.float64)
    large = max_exact + (np.log(nf / max_exact) / math.log(MAX_DISTANCE / max_exact) * (half - max_exact)).astype(np.int64)
    large = np.minimum(large, half - 1)
    return ret + np.where(n < max_exact, n, large)


def diff_bias_tiles(rel_bias, t):
    assert t >= MAX_DISTANCE, "offsets of two or more tiles must lie in the saturated buckets"
    heads = rel_bias.shape[1]
    vec = rel_bias[jnp.asarray(_t5_bucket_table(t), dtype=I32)].T * LOG2E
    tiles = []
    for d in range(-2, 3):
        lo = d * t + 2 * t
        w = vec[:, lo:lo + 2 * t - 1]
        u = jnp.concatenate([w[:, t - 1:], jnp.zeros((heads, 1), F32), w[:, :t - 1]], axis=1)
        skew = jnp.tile(u, (1, t))[:, :t * (2 * t - 1)].reshape(heads, t, 2 * t - 1)
        tiles.append(skew[:, :, :t])
    return jnp.stack(tiles, axis=1)


def rope_tables(seq):
    rows = seq // GRID_W
    row = jnp.repeat(jnp.arange(rows, dtype=F32), GRID_W)
    col = jnp.tile(jnp.arange(GRID_W, dtype=F32), rows)
    freq = ROPE_THETA ** (-jnp.arange(0, ROPE_AXIS_DIM, 2, dtype=F32) / ROPE_AXIS_DIM)
    ang = jnp.concatenate([row[:, None] * freq, col[:, None] * freq], -1)
    cos, sin = jnp.cos(ang), jnp.sin(ang)
    return jnp.concatenate([cos, cos], -1), jnp.concatenate([-sin, sin], -1)


def _conv3_body(x0, x1, x2, w0, w1, w2, b0, b1, b2, o0, o1, o2):
    for x_ref, w_ref, b_ref, o_ref in ((x0, w0, b0, o0), (x1, w1, b1, o1), (x2, w2, b2, o2)):
        x = x_ref[0]
        n = x.shape[0]
        t = lax.broadcasted_iota(I32, (n, 1), 0)
        prev = jnp.where(t == 0, 0.0, pltpu.roll(x, 1, 0))
        nxt = jnp.where(t == n - 1, 0.0, pltpu.roll(x, n - 1, 0))
        o_ref[0] = w_ref[0:1] * prev + w_ref[1:2] * x + w_ref[2:3] * nxt + b_ref[...]


def hyena_conv3(z_hy, conv_w, conv_b, batch, seq):
    z = z_hy.reshape(batch, seq, HY_COLS)
    nc = HY_WIDTH // LANES
    x_spec = lambda g: pl.BlockSpec((1, seq, LANES), lambda b, c: (b, 0, g * nc + c))
    w_spec = lambda g: pl.BlockSpec((3, LANES), lambda b, c: (0, g * nc + c))
    b_spec = lambda g: pl.BlockSpec((1, LANES), lambda b, c: (0, g * nc + c))
    out = pl.BlockSpec((1, seq, LANES), lambda b, c: (b, 0, c))
    return pl.pallas_call(
        _conv3_body,
        grid=(batch, nc),
        in_specs=[x_spec(0), x_spec(1), x_spec(2), w_spec(0), w_spec(1), w_spec(2),
                  b_spec(0), b_spec(1), b_spec(2)],
        out_specs=[out, out, out],
        out_shape=[_sds((batch, seq, HY_WIDTH))] * 3,
        compiler_params=_cp("parallel", "parallel"),
        name="hyena_conv3",
    )(z, z, z, conv_w, conv_w, conv_w, conv_b, conv_b, conv_b)


def _filt_body(feat_ref, dist_ref, w1, b1, f1, w2, b2, f2, w3, dec, h_ref, s_ref):
    dot = functools.partial(jnp.dot, precision=HI, preferred_element_type=F32)
    hid = jnp.sin(f1[...] * (dot(feat_ref[...], w1[...]) + b1[...]))
    hid = jnp.sin(f2[...] * (dot(hid, w2[...]) + b2[...]))
    h = dot(hid, w3[...]) * jnp.exp(-dist_ref[...] * jnp.abs(dec[...]))
    h_ref[...] = h

    @pl.when(pl.program_id(0) == 0)
    def _():
        s_ref[...] = jnp.zeros_like(s_ref)

    s_ref[...] += jnp.sum(jnp.abs(h), axis=0, keepdims=True)


def hyena_filter(seq, w1, b1, f1, w2, b2, f2, w3, decay):
    j = jnp.arange(seq, dtype=F32)
    tt = j / seq
    ang = 2.0 * jnp.pi * tt[:, None] * jnp.arange(1, HY_BANDS + 1, dtype=F32)
    feat = jnp.concatenate([tt[:, None], jnp.cos(ang), jnp.sin(ang)], -1)
    feat = jnp.pad(feat, ((0, 0), (0, LANES - HY_EMB)))
    half = seq // 2
    dist = (jnp.abs(j - half) / half)[:, None]
    w1p = jnp.pad(w1, ((0, LANES - HY_EMB), (0, 0)))
    tm = min(1024, seq)
    hw = HY_ORDER * HY_WIDTH
    const = lambda i: (0, 0)
    return pl.pallas_call(
        _filt_body,
        grid=(seq // tm,),
        in_specs=[pl.BlockSpec((tm, LANES), lambda i: (i, 0)), pl.BlockSpec((tm, 1), lambda i: (i, 0)),
                  pl.BlockSpec((LANES, HY_FFN), const), pl.BlockSpec((1, HY_FFN), const),
                  pl.BlockSpec((1, HY_FFN), const), pl.BlockSpec((HY_FFN, HY_FFN), const),
                  pl.BlockSpec((1, HY_FFN), const), pl.BlockSpec((1, HY_FFN), const),
                  pl.BlockSpec((HY_FFN, hw), const), pl.BlockSpec((1, hw), const)],
        out_specs=[pl.BlockSpec((tm, hw), lambda i: (i, 0)), pl.BlockSpec((1, hw), const)],
        out_shape=[_sds((seq, hw)), _sds((1, hw))],
        compiler_params=_cp("arbitrary"),
        name="hyena_filter",
    )(feat, dist, w1p, b1[None], f1[None], w2, b2[None], f2[None], w3, decay.reshape(1, hw))


def dft_tables(seq):
    n = 2 * seq
    n2 = FFT_N2
    n1 = n // n2
    n1h = n1 // 2
    nk = -(-(n1h + 1) // SUBLANES) * SUBLANES
    k1 = jnp.arange(nk, dtype=I32)
    live = (k1 <= n1h).astype(F32)
    ang = (2.0 * jnp.pi / n1) * ((k1[:, None] * jnp.arange(n1h, dtype=I32)[None, :]) % n1).astype(F32)
    outer_fwd = jnp.concatenate([jnp.cos(ang), -jnp.sin(ang)], axis=0) * jnp.tile(live, 2)[:, None]
    t1 = jnp.arange(n1h, dtype=I32) + n1 // 4
    ang = (2.0 * jnp.pi / n1) * ((t1[:, None] * k1[None, :]) % n1).astype(F32)
    weight = live * jnp.where((k1 == 0) | (k1 == n1h), 1.0, 2.0) / n
    outer_inv = jnp.concatenate([jnp.cos(ang), -jnp.sin(ang)], axis=1) * jnp.tile(weight, 2)[None, :]
    k2 = jnp.arange(n2, dtype=I32)
    t2 = jnp.arange(n2, dtype=I32)
    phase = (t2[None, None, :] * (k2[None, :, None] * n1 + k1[:, None, None])) % n
    ang = (2.0 * jnp.pi / n) * phase.astype(F32)
    fr, fi = jnp.cos(ang), -jnp.sin(ang)
    inner = jnp.concatenate([jnp.concatenate([fr, -fi], axis=2),
                             jnp.concatenate([fi, fr], axis=2)], axis=1)
    return outer_fwd, outer_inv, inner, jnp.swapaxes(inner, 1, 2)


def _fft_outer_body(f_ref, x_ref, o_ref):
    nk = o_ref.shape[2]
    ch = x_ref.shape[3]
    x = jnp.concatenate([x_ref[0, :, s, :] for s in range(SUBLANES)], axis=1)
    r = jnp.dot(f_ref[...], x, precision=HI, preferred_element_type=F32)
    for part in range(2):
        for s in range(SUBLANES):
            o_ref[0, part, :, s, :] = r[part * nk:(part + 1) * nk, s * ch:(s + 1) * ch]


def fft_outer(x4, table):
    batch, n1h, n2, ch = x4.shape
    nk = table.shape[0] // 2
    return pl.pallas_call(
        _fft_outer_body,
        grid=(batch, n2 // SUBLANES),
        in_specs=[pl.BlockSpec((2 * nk, n1h), lambda b, c: (0, 0)),
                  pl.BlockSpec((1, n1h, SUBLANES, ch), lambda b, c: (b, 0, c, 0))],
        out_specs=pl.BlockSpec((1, 2, nk, SUBLANES, ch), lambda b, c: (b, 0, 0, c, 0)),
        out_shape=_sds((batch, 2, nk, n2, ch)),
        compiler_params=_cp("parallel", "parallel"),
        name="fft_outer",
    )(table, x4)


def _fft_filter_body(m_ref, s_ref, a_ref, o_ref):
    n2 = FFT_N2
    a = a_ref[0, :, 0].reshape(2 * n2, HY_WIDTH)
    x = jnp.dot(m_ref[0], a, precision=HI, preferred_element_type=F32) * (1.0 / s_ref[...])
    o_ref[:, 0] = x.reshape(2, n2, HY_WIDTH)


def fft_filter(a5, inner, s):
    _, _, n1, n2, ch = a5.shape
    return pl.pallas_call(
        _fft_filter_body,
        grid=(n1, ch // HY_WIDTH),
        in_specs=[pl.BlockSpec((1, 2 * n2, 2 * n2), lambda k, c: (k, 0, 0)),
                  pl.BlockSpec((1, HY_WIDTH), lambda k, c: (0, c)),
                  pl.BlockSpec((1, 2, 1, n2, HY_WIDTH), lambda k, c: (0, 0, k, 0, c))],
        out_specs=pl.BlockSpec((2, 1, n2, HY_WIDTH), lambda k, c: (0, k, 0, c)),
        out_shape=_sds((2, n1, n2, ch)),
        compiler_params=_cp("parallel", "parallel"),
        name="fft_filter",
    )(inner, s, a5)


def _fft_mid_body(m_ref, mt_ref, h_ref, a_ref, o_ref, *, nb):
    n2 = FFT_N2
    dot = functools.partial(jnp.dot, precision=HI, preferred_element_type=F32)
    m = m_ref[0]
    mt = mt_ref[0]
    hr = h_ref[0, 0]
    hi = h_ref[1, 0]
    for b in range(nb):
        x = dot(m, a_ref[b, :, 0].reshape(2 * n2, HY_WIDTH))
        xr, xi = x[:n2], x[n2:]
        y = jnp.concatenate([xr * hr - xi * hi, xr * hi + xi * hr], axis=0)
        o_ref[b, :, 0] = dot(mt, y).reshape(2, n2, HY_WIDTH)


def fft_mid(a5, inner, inner_t, hf, order):
    nb, _, n1, n2, ch = a5.shape
    mat = pl.BlockSpec((1, 2 * n2, 2 * n2), lambda k: (k, 0, 0))
    blk = pl.BlockSpec((nb, 2, 1, n2, ch), lambda k: (0, 0, k, 0, 0))
    return pl.pallas_call(
        functools.partial(_fft_mid_body, nb=nb),
        grid=(n1,),
        in_specs=[mat, mat, pl.BlockSpec((2, 1, n2, ch), lambda k: (0, k, 0, order)), blk],
        out_specs=blk,
        out_shape=_sds(a5.shape),
        compiler_params=_cp("parallel"),
        name="fft_mid",
    )(inner, inner_t, hf, a5)


def _fft_inv_body(g_ref, b_ref, gate_ref, y_ref, fb_ref, o_ref):
    nk = b_ref.shape[2]
    ch = b_ref.shape[4]
    bm = jnp.concatenate([b_ref[0, :, :, s, :].reshape(2 * nk, ch) for s in range(SUBLANES)], axis=1)
    c = jnp.dot(g_ref[...], bm, precision=HI, preferred_element_type=F32)
    fb = fb_ref[...]
    for s in range(SUBLANES):
        o_ref[0, :, s, :] = gate_ref[0, :, s, :] * (c[:, s * ch:(s + 1) * ch] + fb * y_ref[0, :, s, :])


def fft_inv_gate(b5, table, gate4, y4, fbias):
    batch, _, nk, n2, ch = b5.shape
    n1h = table.shape[0]
    blk = pl.BlockSpec((1, n1h, SUBLANES, ch), lambda b, c: (b, 0, c, 0))
    return pl.pallas_call(
        _fft_inv_body,
        grid=(batch, n2 // SUBLANES),
        in_specs=[pl.BlockSpec((n1h, 2 * nk), lambda b, c: (0, 0)),
                  pl.BlockSpec((1, 2, nk, SUBLANES, ch), lambda b, c: (b, 0, 0, c, 0)), blk, blk,
                  pl.BlockSpec((1, ch), lambda b, c: (0, 0))],
        out_specs=blk,
        out_shape=_sds((batch, n1h, n2, ch)),
        compiler_params=_cp("parallel", "parallel"),
        name="fft_inv_gate",
    )(table, b5, gate4, y4, fbias.reshape(1, ch))


def hyena_spectrum(seq, tabs, w1, b1, f1, w2, b2, f2, w3, decay):
    outer_fwd, _, inner, _ = tabs
    n1h = outer_fwd.shape[1]
    hw = HY_ORDER * HY_WIDTH
    h, s = hyena_filter(seq, w1, b1, f1, w2, b2, f2, w3, decay)
    return fft_filter(fft_outer(h.reshape(1, n1h, FFT_N2, hw), outer_fwd), inner, s)


def hyena_mixer(z_hy, conv_w, conv_b, hf, f_bias, tabs, batch, seq):
    outer_fwd, outer_inv, inner, inner_t = tabs
    n1h = outer_fwd.shape[1]
    shape4 = (batch, n1h, FFT_N2, HY_WIDTH)
    v, g0, g1 = hyena_conv3(z_hy, conv_w, conv_b, batch, seq)
    y = v.reshape(shape4)
    for order, gate in enumerate((g0, g1)):
        bm = fft_mid(fft_outer(y, outer_fwd), inner, inner_t, hf, order)
        y = fft_inv_gate(bm, outer_inv, gate.reshape(shape4), y, f_bias[order])
    return y.reshape(batch * seq, HY_WIDTH)


def _mix_body(lam_ref, h_ref, s5_ref, gqa_ref, diff_ref, hy_ref, gs5, ggqa, gsub, ghy, w_ref, o_ref, *, keep):
    lam = lam_ref[0]
    a = _rms(s5_ref[...], gs5[...])
    b = _rms(jnp.concatenate([gqa_ref[0, h] for h in range(GQA_HEADS)], axis=1), ggqa[...])
    c = jnp.concatenate(
        [_rms(diff_ref[0, 2 * h] - lam * diff_ref[0, 2 * h + 1], gsub[...]) * keep for h in range(DIFF_HEADS)],
        axis=1)
    d = _rms(hy_ref[...], ghy[...])
    mixed = jnp.concatenate([a, b, c, d], axis=1).astype(BF16)
    o_ref[...] = h_ref[...] + jnp.dot(mixed, w_ref[...], preferred_element_type=F32)


def mix_out(lam, h, y_s5, o_gqa, o_diff, y_hy, gs5, ggqa, gsub, ghy, w_out, batch, seq, lambda_init):
    tm = min(ROW_TILE, seq)
    nb = seq // tm
    row = lambda w: pl.BlockSpec((tm, w), lambda b, i: (b * nb + i, 0))
    vec = lambda w: pl.BlockSpec((1, w), lambda b, i: (0, 0))
    heads = lambda n: pl.BlockSpec((1, n, tm, HEAD_DIM), lambda b, i: (b, 0, i, 0))
    return pl.pallas_call(
        functools.partial(_mix_body, keep=1.0 - lambda_init),
        grid=(batch, nb),
        in_specs=[pl.BlockSpec(memory_space=pltpu.SMEM), row(D_MODEL), row(GROUP_WIDTH), heads(GQA_HEADS),
                  heads(2 * DIFF_HEADS), row(GROUP_WIDTH), vec(GROUP_WIDTH), vec(GROUP_WIDTH), vec(HEAD_DIM),
                  vec(GROUP_WIDTH), pl.BlockSpec((D_MODEL, D_MODEL), lambda b, i: (0, 0))],
        out_specs=row(D_MODEL),
        out_shape=_sds((batch * seq, D_MODEL)),
        compiler_params=_cp("parallel", "parallel"),
        name="mix_out",
    )(lam, h, y_s5, o_gqa, o_diff, y_hy, gs5, ggqa, gsub, ghy, w_out)


def _router_body(h_ref, g_ref, w_ref, b_ref, xn_ref, ti_ref, tw_ref, cnt_ref):
    xn = _rms(h_ref[...], g_ref[...])
    xn_ref[...] = xn
    logits = jnp.dot(xn, w_ref[...], precision=HI, preferred_element_type=F32) + b_ref[...]
    lane = lax.broadcasted_iota(I32, logits.shape, 1)
    vals = logits
    tops, idxs = [], []
    hot = jnp.zeros(logits.shape, F32)
    for _ in range(TOP_K):
        m = jnp.max(vals, axis=-1, keepdims=True)
        idx = jnp.min(jnp.where(vals == m, lane, N_EXPERTS), axis=-1, keepdims=True)
        sel = lane == idx
        tops.append(m)
        idxs.append(idx)
        hot = hot + sel.astype(F32)
        vals = jnp.where(sel, -jnp.inf, vals)
    es = [jnp.exp(t - tops[0]) for t in tops]
    den = es[0] + es[1] + es[2] + es[3]
    ti_ref[...] = jnp.concatenate(idxs, axis=1)
    tw_ref[...] = jnp.concatenate([e / den for e in es], axis=1)

    @pl.when(pl.program_id(0) == 0)
    def _():
        cnt_ref[...] = jnp.zeros_like(cnt_ref)

    cnt_ref[...] += jnp.sum(hot, axis=0, keepdims=True)


def moe_router(h, g, w_router, b_router):
    t = h.shape[0]
    tm = min(ROW_TILE, t)
    const = lambda i: (0, 0)
    return pl.pallas_call(
        _router_body,
        grid=(t // tm,),
        in_specs=[pl.BlockSpec((tm, D_MODEL), lambda i: (i, 0)), pl.BlockSpec((1, D_MODEL), const),
                  pl.BlockSpec((D_MODEL, N_EXPERTS), const), pl.BlockSpec((1, N_EXPERTS), const)],
        out_specs=[pl.BlockSpec((tm, D_MODEL), lambda i: (i, 0)), pl.BlockSpec((tm, TOP_K), lambda i: (i, 0)),
                   pl.BlockSpec((tm, TOP_K), lambda i: (i, 0)), pl.BlockSpec((1, N_EXPERTS), const)],
        out_shape=[_sds((t, D_MODEL)), _sds((t, TOP_K), I32), _sds((t, TOP_K)), _sds((1, N_EXPERTS))],
        compiler_params=_cp("arbitrary"),
        name="moe_router",
    )(h, g, w_router, b_router)


def _rank_body(ti_ref, off_ref, pos_ref, carry_ref):
    @pl.when(pl.program_id(0) == 0)
    def _():
        carry_ref[...] = jnp.zeros_like(carry_ref)

    ti = ti_ref[...]
    tm = ti.shape[0]
    lane = lax.broadcasted_iota(I32, (tm, N_EXPERTS), 1)
    hots = [lane == ti[:, k:k + 1] for k in range(TOP_K)]
    hot = sum(h.astype(F32) for h in hots)
    r = lax.broadcasted_iota(I32, (tm, tm), 0)
    c = lax.broadcasted_iota(I32, (tm, tm), 1)
    below = jnp.where(r > c, 1.0, 0.0).astype(BF16)
    before = jnp.dot(below, hot.astype(BF16), preferred_element_type=F32) + carry_ref[...] + off_ref[...]
    pos = [jnp.sum(jnp.where(h, before, 0.0), axis=-1, keepdims=True) for h in hots]
    pos_ref[...] = jnp.concatenate(pos, axis=1).astype(I32)
    carry_ref[...] += jnp.sum(hot, axis=0, keepdims=True)


def moe_rank(top_i, offsets):
    t = top_i.shape[0]
    tm = min(ROW_TILE, t)
    return pl.pallas_call(
        _rank_body,
        grid=(t // tm,),
        in_specs=[pl.BlockSpec((tm, TOP_K), lambda i: (i, 0)), pl.BlockSpec((1, N_EXPERTS), lambda i: (0, 0))],
        out_specs=pl.BlockSpec((tm, TOP_K), lambda i: (i, 0)),
        out_shape=_sds((t, TOP_K), I32),
        scratch_shapes=[pltpu.VMEM((1, N_EXPERTS), F32)],
        compiler_params=_cp("arbitrary"),
        name="moe_rank",
    )(top_i, offsets)


def _row_copy(src, dst, sem):
    return pltpu.make_async_copy(src, dst, sem)


def _scatter_body(pos_ref, x_ref, zero_hbm, xs_hbm, buf, sem, *, tm):
    del zero_hbm
    x = x_ref[...]
    for s in range(ROW_TILES):
        buf[:, s, :] = x[:, s * LANES:(s + 1) * LANES]

    def start(r, c):
        for k in range(TOP_K):
            _row_copy(buf.at[r], xs_hbm.at[pos_ref[r * TOP_K + k]], sem).start()
        return c

    lax.fori_loop(0, tm, start, 0)
    for _ in range(TOP_K):
        _row_copy(buf, xs_hbm.at[pl.ds(0, tm)], sem).wait()


def moe_scatter(pos_flat, x, n_slots):
    t = x.shape[0]
    tm = min(MOE_TOK_TILE, t)
    zeros = jnp.zeros((n_slots, ROW_TILES, LANES), F32)
    return pl.pallas_call(
        functools.partial(_scatter_body, tm=tm),
        grid=(t // tm,),
        in_specs=[pl.BlockSpec((tm * TOP_K,), lambda i: (i,), memory_space=pltpu.SMEM),
                  pl.BlockSpec((tm, D_MODEL), lambda i: (i, 0)), pl.BlockSpec(memory_space=pl.ANY)],
        out_specs=pl.BlockSpec(memory_space=pl.ANY),
        out_shape=_sds((n_slots, ROW_TILES, LANES)),
        scratch_shapes=[pltpu.VMEM((tm, ROW_TILES, LANES), F32), pltpu.SemaphoreType.DMA(())],
        input_output_aliases={2: 0},
        compiler_params=_cp("arbitrary"),
        name="moe_scatter",
    )(pos_flat, x, zeros)


def _experts_body(te_ref, nv_ref, x_ref, wgu_ref, bgu_ref, wd_ref, bd_ref, o_ref):
    del te_ref
    j = pl.program_id(0)
    dff = wd_ref.shape[1]

    @pl.when(j < nv_ref[0])
    def _():
        x = jnp.concatenate([x_ref[:, s, :] for s in range(ROW_TILES)], axis=1).astype(BF16)
        hgu = jnp.dot(x, wgu_ref[0], preferred_element_type=F32) + bgu_ref[0]
        glu = jnp.minimum(hgu[:, :dff], SWIGLU_LIMIT)
        lin = jnp.clip(hgu[:, dff:], -SWIGLU_LIMIT, SWIGLU_LIMIT)
        act = glu * jax.nn.sigmoid(SWIGLU_ALPHA * glu) * (lin + 1.0)
        y = jnp.dot(act.astype(BF16), wd_ref[0], preferred_element_type=F32) + bd_ref[0]
        for s in range(ROW_TILES):
            o_ref[:, s, :] = y[:, s * LANES:(s + 1) * LANES]

    @pl.when(j >= nv_ref[0])
    def _():
        o_ref[...] = jnp.zeros_like(o_ref)


def moe_experts(tile_expert, n_valid, xs3, w_gu, b_gu, w_down, b_down):
    n_slots = xs3.shape[0]
    tm = MOE_TILE
    dff = w_down.shape[1]
    blk = pl.BlockSpec((tm, ROW_TILES, LANES), lambda j, te, nv: (j, 0, 0))
    return pl.pallas_call(
        _experts_body,
        grid_spec=pltpu.PrefetchScalarGridSpec(
            num_scalar_prefetch=2,
            grid=(n_slots // tm,),
            in_specs=[blk,
                      pl.BlockSpec((1, D_MODEL, 2 * dff), lambda j, te, nv: (te[j], 0, 0)),
                      pl.BlockSpec((1, 1, 2 * dff), lambda j, te, nv: (te[j], 0, 0)),
                      pl.BlockSpec((1, dff, D_MODEL), lambda j, te, nv: (te[j], 0, 0)),
                      pl.BlockSpec((1, 1, D_MODEL), lambda j, te, nv: (te[j], 0, 0))],
            out_specs=blk),
        out_shape=_sds((n_slots, ROW_TILES, LANES)),
        compiler_params=_cp("arbitrary"),
        name="moe_experts",
    )(tile_expert, n_valid, xs3, w_gu, b_gu, w_down, b_down)


def _combine_body(pos_ref, w_ref, ys_hbm, o_ref, b0, b1, b2, b3, sem, *, tm):
    bufs = (b0, b1, b2, b3)

    def start(r, c):
        for k in range(TOP_K):
            _row_copy(ys_hbm.at[pos_ref[r * TOP_K + k]], bufs[k].at[r], sem).start()
        return c

    lax.fori_loop(0, tm, start, 0)
    for k in range(TOP_K):
        _row_copy(ys_hbm.at[pl.ds(0, tm)], bufs[k], sem).wait()

    def mix(q, c):
        for u in range(MIX_UNROLL):
            r = q * MIX_UNROLL + u
            acc = w_ref[r * TOP_K] * b0[r]
            for k in range(1, TOP_K):
                acc = acc + w_ref[r * TOP_K + k] * bufs[k][r]
            o_ref[r] = acc
        return c

    lax.fori_loop(0, tm // MIX_UNROLL, mix, 0)


def moe_combine(pos_flat, w_flat, ys3, t):
    tm = min(MOE_TOK_TILE, t)
    smem = pl.BlockSpec((tm * TOP_K,), lambda i: (i,), memory_space=pltpu.SMEM)
    return pl.pallas_call(
        functools.partial(_combine_body, tm=tm),
        grid=(t // tm,),
        in_specs=[smem, smem, pl.BlockSpec(memory_space=pl.ANY)],
        out_specs=pl.BlockSpec((tm, ROW_TILES, LANES), lambda i: (i, 0, 0)),
        out_shape=_sds((t, ROW_TILES, LANES)),
        scratch_shapes=[pltpu.VMEM((tm, ROW_TILES, LANES), F32)] * TOP_K + [pltpu.SemaphoreType.DMA(())],
        compiler_params=_cp("arbitrary"),
        name="moe_combine",
    )(pos_flat, w_flat, ys3)


def moe(h, g, w_router, b_router, w_gu, b_gu, w_down, b_down):
    t = h.shape[0]
    xn, top_i, top_w, counts = moe_router(h, g, w_router, b_router)
    n_tiles = (t * TOP_K) // MOE_TILE + N_EXPERTS
    padded = jnp.ceil(counts[0] / MOE_TILE) * MOE_TILE
    ends = jnp.cumsum(padded)
    offsets = (ends - padded)[None]
    tile_start = jnp.arange(n_tiles, dtype=F32) * MOE_TILE
    tile_expert = jnp.minimum(jnp.sum(ends[None, :] <= tile_start[:, None], axis=1), N_EXPERTS - 1).astype(I32)
    n_valid = (ends[-1:] / MOE_TILE).astype(I32)
    pos = moe_rank(top_i, offsets).reshape(t * TOP_K)
    xs3 = moe_scatter(pos, xn, n_tiles * MOE_TILE)
    ys3 = moe_experts(tile_expert, n_valid, xs3, w_gu, b_gu, w_down, b_down)
    return moe_combine(pos, top_w.reshape(t * TOP_K), ys3, t).reshape(t, D_MODEL)


def _ple_body(h_ref, m_ref, p_ref, wp_ref, wg_ref, o_ref):
    h = h_ref[...] + m_ref[...]
    e = jnp.dot(p_ref[...].astype(BF16), wp_ref[...], preferred_element_type=F32)
    gate = jnp.dot(_rms(h).astype(BF16), wg_ref[...], preferred_element_type=F32)
    o_ref[...] = h + e * jax.nn.sigmoid(gate)


def ple(h, moe_out, p, w_ple, w_gate):
    t = h.shape[0]
    tm = min(ROW_TILE, t)
    row = pl.BlockSpec((tm, D_MODEL), lambda i: (i, 0))
    return pl.pallas_call(
        _ple_body,
        grid=(t // tm,),
        in_specs=[row, row, pl.BlockSpec((tm, PLE_DIM), lambda i: (i, 0)),
                  pl.BlockSpec((PLE_DIM, D_MODEL), lambda i: (0, 0)),
                  pl.BlockSpec((D_MODEL, D_MODEL), lambda i: (0, 0))],
        out_specs=row,
        out_shape=_sds((t, D_MODEL)),
        compiler_params=_cp("parallel"),
        name="ple",
    )(h, moe_out, p, w_ple, w_gate)


def _deinterleave(n):
    return np.concatenate([np.arange(0, n, 2), np.arange(1, n, 2)])


def _halves(w):
    pairs = w.reshape(*w.shape[:-1], w.shape[-1] // 2, 2)
    return jnp.concatenate([pairs[..., 0], pairs[..., 1]], axis=-1)


def prepare_layer(i, prm):
    pair = _deinterleave(HEAD_DIM)
    cols = np.arange(IN_COLS)
    for h in range(GQA_HEADS + GQA_KV_HEADS):
        lo = OFF_GQA + h * HEAD_DIM
        cols[lo:lo + HEAD_DIM] = lo + pair
    lam = (jnp.exp(jnp.sum(prm["diff_lam_q1"][i] * prm["diff_lam_k1"][i]))
           - jnp.exp(jnp.sum(prm["diff_lam_q2"][i] * prm["diff_lam_k2"][i])) + _lambda_init(i))
    return dict(
        norm1_g=prm["norm1_g"][i][None],
        w_in=prm["w_in"][i][:, cols].astype(BF16),
        s5=s5_tables(prm["s5_lam_re"][i], prm["s5_lam_im"][i], prm["s5_log_dt"][i], prm["s5_b_re"][i],
                     prm["s5_b_im"][i], prm["s5_c_re"][i], prm["s5_c_im"][i]),
        s5_d=prm["s5_d"][i][None],
        s5_w_glu=prm["s5_w_glu"][i].astype(BF16),
        s5_b_glu=prm["s5_b_glu"][i][None],
        gqa_q_g=prm["gqa_q_g"][i][pair][None],
        gqa_k_g=prm["gqa_k_g"][i][pair][None],
        diff_q_g=prm["diff_q_g"][i][None],
        diff_k_g=prm["diff_k_g"][i][None],
        diff_lam=lam.reshape(1).astype(F32),
        diff_subln_g=prm["diff_subln_g"][i][None],
        out_g_s5=prm["out_g_s5"][i][None],
        out_g_gqa=prm["out_g_gqa"][i][None],
        out_g_hy=prm["out_g_hy"][i][None],
        w_out=prm["w_out"][i].astype(BF16),
        norm2_g=prm["norm2_g"][i][None],
        w_router=prm["w_router"][i],
        b_router=prm["b_router"][i][None],
        w_gu=_halves(prm["w_gu"][i]).astype(BF16),
        b_gu=_halves(prm["b_gu"][i])[:, None, :],
        w_down=prm["w_down"][i].astype(BF16),
        b_down=prm["b_down"][i][:, None, :],
        w_ple=prm["w_ple"][i].astype(BF16),
        w_ple_gate=prm["w_ple_gate"][i].astype(BF16),
    )


def _lambda_init(i):
    return 0.8 - 0.6 * math.exp(-0.3 * i)


def run_trunk(x, p, prm, layers):
    batch, seq, _ = x.shape
    t = batch * seq
    cos_t, sin_t = rope_tables(seq)
    tabs = dft_tables(seq)
    bias = diff_bias_tiles(prm["rel_bias"], min(ATTN_UNIT, seq))
    h = x.reshape(t, D_MODEL)
    for i, lw in enumerate(layers):
        z_s5, z_gqa, z_diff, z_hy = in_proj(h, lw["norm1_g"], lw["w_in"])
        y_s5 = s5_mixer(z_s5, lw["s5"], lw["s5_d"], lw["s5_w_glu"], lw["s5_b_glu"], batch, seq)
        q, k, v = attn_prep(z_gqa, lw["gqa_q_g"], lw["gqa_k_g"], cos_t, sin_t, batch, seq,
                            GQA_HEADS, GQA_KV_HEADS, GQA_KV_HEADS, True)
        rep = GQA_HEADS // GQA_KV_HEADS
        o_gqa = flash_attention(q, k, v, lambda hq: hq // rep, lambda hq: hq // rep)
        q, k, v = attn_prep(z_diff, lw["diff_q_g"], lw["diff_k_g"], cos_t, sin_t, batch, seq,
                            2 * DIFF_HEADS, 2 * DIFF_HEADS, DIFF_HEADS, False)
        o_diff = flash_attention(q, k, v, lambda hq: hq, lambda hq: hq // 2, bias)
        hf = hyena_spectrum(seq, tabs, prm["hy_w1"][i], prm["hy_b1"][i], prm["hy_freq1"][i], prm["hy_w2"][i],
                            prm["hy_b2"][i], prm["hy_freq2"][i], prm["hy_w3"][i], prm["hy_decay"][i])
        y_hy = hyena_mixer(z_hy, prm["hy_conv_w"][i], prm["hy_conv_b"][i][None], hf, prm["hy_bias"][i],
                           tabs, batch, seq)
        h = mix_out(lw["diff_lam"], h, y_s5, o_gqa, o_diff, y_hy, lw["out_g_s5"], lw["out_g_gqa"],
                    lw["diff_subln_g"], lw["out_g_hy"], lw["w_out"], batch, seq, _lambda_init(i))
        m = moe(h, lw["norm2_g"], lw["w_router"], lw["b_router"], lw["w_gu"], lw["b_gu"], lw["w_down"],
                lw["b_down"])
        h = ple(h, m, p[i].reshape(t, PLE_DIM), lw["w_ple"], lw["w_ple_gate"])
    return h.reshape(batch, seq, D_MODEL)


def kernel(x_prompt, x_sample, p_prompt, p_sample, rel_bias, norm1_g, w_in, s5_lam_re, s5_lam_im, s5_log_dt,
           s5_b_re, s5_b_im, s5_c_re, s5_c_im, s5_d, s5_w_glu, s5_b_glu, gqa_q_g, gqa_k_g, diff_q_g, diff_k_g,
           diff_lam_q1, diff_lam_k1, diff_lam_q2, diff_lam_k2, diff_subln_g, hy_conv_w, hy_conv_b, hy_w1, hy_b1,
           hy_freq1, hy_w2, hy_b2, hy_freq2, hy_w3, hy_decay, hy_bias, out_g_s5, out_g_gqa, out_g_hy, w_out,
           norm2_g, w_router, b_router, w_gu, b_gu, w_down, b_down, w_ple, w_ple_gate):
    prm = dict(rel_bias=rel_bias, norm1_g=norm1_g, w_in=w_in, s5_lam_re=s5_lam_re, s5_lam_im=s5_lam_im,
               s5_log_dt=s5_log_dt, s5_b_re=s5_b_re, s5_b_im=s5_b_im, s5_c_re=s5_c_re, s5_c_im=s5_c_im,
               s5_d=s5_d, s5_w_glu=s5_w_glu, s5_b_glu=s5_b_glu, gqa_q_g=gqa_q_g, gqa_k_g=gqa_k_g,
               diff_q_g=diff_q_g, diff_k_g=diff_k_g, diff_lam_q1=diff_lam_q1, diff_lam_k1=diff_lam_k1,
               diff_lam_q2=diff_lam_q2, diff_lam_k2=diff_lam_k2, diff_subln_g=diff_subln_g,
               hy_conv_w=hy_conv_w, hy_conv_b=hy_conv_b, hy_w1=hy_w1, hy_b1=hy_b1, hy_freq1=hy_freq1,
               hy_w2=hy_w2, hy_b2=hy_b2, hy_freq2=hy_freq2, hy_w3=hy_w3, hy_decay=hy_decay, hy_bias=hy_bias,
               out_g_s5=out_g_s5, out_g_gqa=out_g_gqa, out_g_hy=out_g_hy, w_out=w_out, norm2_g=norm2_g,
               w_router=w_router, b_router=b_router, w_gu=w_gu, b_gu=b_gu, w_down=w_down, b_down=b_down,
               w_ple=w_ple, w_ple_gate=w_ple_gate)
    layers = [prepare_layer(i, prm) for i in range(DEPTH)]
    return (run_trunk(x_prompt, p_prompt, prm, layers), run_trunk(x_sample, p_sample, prm, layers))
```

```python
import functools
import math

import numpy as np
import jax
import jax.numpy as jnp
from jax import lax
from jax.experimental import pallas as pl
from jax.experimental.pallas import tpu as pltpu

F32 = jnp.float32
BF16 = jnp.bfloat16
I32 = jnp.int32
HI = lax.Precision.HIGHEST

D_MODEL = 1024
DEPTH = 2
GROUP_WIDTH = 256
HEAD_DIM = 64
S5_GROUP_DIM = 16
S5_GROUPS = 16
S5_STATE = 64
S5_NSTATE = S5_GROUPS * S5_STATE
GQA_HEADS = 4
GQA_KV_HEADS = 2
DIFF_HEADS = 4
HY_WIDTH = 256
HY_ORDER = 2
HY_BANDS = 16
HY_EMB = 2 * HY_BANDS + 1
HY_FFN = 64
N_EXPERTS = 32
TOP_K = 4
SWIGLU_LIMIT = 7.0
SWIGLU_ALPHA = 1.702
NUM_BUCKETS = 32
MAX_DISTANCE = 128
GRID_W = 64
ROPE_THETA = 10000.0
ROPE_AXIS_DIM = HEAD_DIM // 2
PLE_DIM = 256
NORM_EPS = 1e-6
ATTN_SCALE = HEAD_DIM ** -0.5
LOG2E = math.log2(math.e)

S5_COLS = GROUP_WIDTH
GQA_QW = GQA_HEADS * HEAD_DIM
GQA_KW = GQA_KV_HEADS * HEAD_DIM
GQA_COLS = GQA_QW + 2 * GQA_KW
DIFF_QW = DIFF_HEADS * 2 * HEAD_DIM
DIFF_COLS = 2 * DIFF_QW + DIFF_HEADS * HEAD_DIM
HY_COLS = (HY_ORDER + 1) * HY_WIDTH
OFF_GQA = S5_COLS
OFF_DIFF = OFF_GQA + GQA_COLS
OFF_HY = OFF_DIFF + DIFF_COLS
IN_COLS = OFF_HY + HY_COLS
IN_SPLITS = ((0, OFF_GQA), (OFF_GQA, OFF_DIFF), (OFF_DIFF, OFF_HY), (OFF_HY, IN_COLS))

LANES = 128
SUBLANES = 8
ROW_TILES = D_MODEL // LANES
FFT_N2 = 128
VMEM_LIMIT = 48 * 1024 * 1024

ROW_TILE = 256
ATTN_Q_TILE = 2048
ATTN_K_TILE = 2048
ATTN_UNIT = 512
ATTN_SUB = 512
S5_TIME_BLOCK = 64
MOE_TILE = 512
MOE_TOK_TILE = 256
MIX_UNROLL = 4


def _cp(*sem):
    return pltpu.CompilerParams(dimension_semantics=sem, vmem_limit_bytes=VMEM_LIMIT)


def _sds(shape, dtype=F32):
    return jax.ShapeDtypeStruct(shape, dtype)


def _rms(x, g=None):
    y = x * lax.rsqrt(jnp.mean(x * x, axis=-1, keepdims=True) + NORM_EPS)
    return y if g is None else y * g


def _inproj_body(x_ref, g_ref, w_ref, *o_refs):
    u = _rms(x_ref[...], g_ref[...]).astype(BF16)
    for o_ref, (lo, hi) in zip(o_refs, IN_SPLITS):
        o_ref[...] = jnp.dot(u, w_ref[:, lo:hi], preferred_element_type=F32)


def in_proj(h, g, w):
    t = h.shape[0]
    tm = min(ROW_TILE, t)
    return pl.pallas_call(
        _inproj_body,
        grid=(t // tm,),
        in_specs=[pl.BlockSpec((tm, D_MODEL), lambda i: (i, 0)),
                  pl.BlockSpec((1, D_MODEL), lambda i: (0, 0)),
                  pl.BlockSpec((D_MODEL, IN_COLS), lambda i: (0, 0))],
        out_specs=[pl.BlockSpec((tm, hi - lo), lambda i: (i, 0)) for lo, hi in IN_SPLITS],
        out_shape=[_sds((t, hi - lo)) for lo, hi in IN_SPLITS],
        compiler_params=_cp("parallel"),
        name="in_proj",
    )(h, g, w)


def s5_tables(lam_re, lam_im, log_dt, b_re, b_im, c_re, c_im):
    lr = jnp.minimum(lam_re, -1e-4)
    li = lam_im
    dt = jnp.exp(log_dt)[..., None]
    mag = jnp.exp(lr * dt)
    ar = mag * jnp.cos(li * dt)
    ai = mag * jnp.sin(li * dt)
    den = lr * lr + li * li
    cr = ((ar - 1.0) * lr + ai * li) / den
    ci = (ai * lr - (ar - 1.0) * li) / den
    bbr = cr[..., None] * b_re - ci[..., None] * b_im
    bbi = cr[..., None] * b_im + ci[..., None] * b_re
    eye = jnp.eye(S5_GROUPS, dtype=F32)
    bb = jnp.stack([bbr, bbi], axis=1)
    wb = jnp.einsum("dpgnh,gk->dghpkn", bb, eye).reshape(2 * S5_COLS, 2 * S5_NSTATE)
    cc = jnp.stack([c_re, -c_im], axis=0)
    wc = jnp.einsum("pdghn,gk->pgndkh", cc, eye).reshape(2 * S5_NSTATE, 2 * S5_COLS)
    a_re = jnp.repeat(ar.reshape(2, S5_NSTATE), 4, axis=0)
    a_im = jnp.repeat(ai.reshape(2, S5_NSTATE), 4, axis=0)
    return wb.astype(BF16), wc.astype(BF16), a_re, a_im


def _s5_body(u_ref, wb_ref, wc_ref, ar_ref, ai_ref, y_ref, bu_ref, xr_ref, xi_ref, *, tb):
    @pl.when(pl.program_id(0) == 0)
    def _():
        xr_ref[...] = jnp.zeros_like(xr_ref)
        xi_ref[...] = jnp.zeros_like(xi_ref)

    u = u_ref[...]
    fwd = (lax.broadcasted_iota(I32, (tb * SUBLANES, 1), 0) % SUBLANES) < 4
    uu = jnp.concatenate([jnp.where(fwd, u, 0.0), jnp.where(fwd, 0.0, u)], axis=1).astype(BF16)
    bu_ref[...] = jnp.dot(uu, wb_ref[...], preferred_element_type=F32)
    ar = ar_ref[...]
    ai = ai_ref[...]

    def step(t, carry):
        xr, xi = carry
        r0 = pl.multiple_of(t * SUBLANES, SUBLANES)
        rows = pl.ds(r0, SUBLANES)
        nxr = ar * xr - ai * xi + bu_ref[rows, 0:S5_NSTATE]
        nxi = ar * xi + ai * xr + bu_ref[rows, S5_NSTATE:2 * S5_NSTATE]
        bu_ref[rows, 0:S5_NSTATE] = nxr
        bu_ref[rows, S5_NSTATE:2 * S5_NSTATE] = nxi
        return nxr, nxi

    xr, xi = lax.fori_loop(0, tb, step, (xr_ref[...], xi_ref[...]))
    xr_ref[...] = xr
    xi_ref[...] = xi
    y2 = jnp.dot(bu_ref[...].astype(BF16), wc_ref[...], preferred_element_type=F32)
    y_ref[...] = jnp.where(fwd, y2[:, :S5_COLS], y2[:, S5_COLS:])


def s5_scan(u8, wb, wc, a_re, a_im, seq):
    tb = min(S5_TIME_BLOCK, seq)
    rows = tb * SUBLANES
    const = lambda i: (0, 0)
    return pl.pallas_call(
        functools.partial(_s5_body, tb=tb),
        grid=(seq // tb,),
        in_specs=[pl.BlockSpec((rows, S5_COLS), lambda i: (i, 0)),
                  pl.BlockSpec((2 * S5_COLS, 2 * S5_NSTATE), const),
                  pl.BlockSpec((2 * S5_NSTATE, 2 * S5_COLS), const),
                  pl.BlockSpec((SUBLANES, S5_NSTATE), const),
                  pl.BlockSpec((SUBLANES, S5_NSTATE), const)],
        out_specs=pl.BlockSpec((rows, S5_COLS), lambda i: (i, 0)),
        out_shape=_sds((seq * SUBLANES, S5_COLS)),
        scratch_shapes=[pltpu.VMEM((rows, 2 * S5_NSTATE), F32),
                        pltpu.VMEM((SUBLANES, S5_NSTATE), F32),
                        pltpu.VMEM((SUBLANES, S5_NSTATE), F32)],
        compiler_params=_cp("arbitrary"),
        name="s5_scan",
    )(u8, wb, wc, a_re, a_im)


def _s5_out_body(yf_ref, yb_ref, u_ref, d_ref, w_ref, b_ref, o_ref):
    y = jax.nn.gelu(yf_ref[...] + yb_ref[...] + u_ref[...] * d_ref[...])
    gate = jnp.dot(y.astype(BF16), w_ref[...], preferred_element_type=F32) + b_ref[...]
    o_ref[...] = y * jax.nn.sigmoid(gate)


def s5_out(yf, yb, u, d, w_glu, b_glu):
    t = u.shape[0]
    tm = min(ROW_TILE, t)
    row = pl.BlockSpec((tm, S5_COLS), lambda i: (i, 0))
    vec = pl.BlockSpec((1, S5_COLS), lambda i: (0, 0))
    return pl.pallas_call(
        _s5_out_body,
        grid=(t // tm,),
        in_specs=[row, row, row, vec, pl.BlockSpec((S5_COLS, S5_COLS), lambda i: (0, 0)), vec],
        out_specs=row,
        out_shape=_sds((t, S5_COLS)),
        compiler_params=_cp("parallel"),
        name="s5_out",
    )(yf, yb, u, d, w_glu, b_glu)


def s5_mixer(z_s5, tabs, d, w_glu, b_glu, batch, seq):
    assert batch == 4, "the scan packs 4 sequences x 2 directions onto the 8 sublanes"
    u = z_s5.reshape(batch, seq, S5_COLS)
    u8 = jnp.concatenate([u.transpose(1, 0, 2), u[:, ::-1].transpose(1, 0, 2)], axis=1)
    y8 = s5_scan(u8.reshape(seq * SUBLANES, S5_COLS), *tabs, seq).reshape(seq, SUBLANES, S5_COLS)
    yf = y8[:, :4].transpose(1, 0, 2).reshape(batch * seq, S5_COLS)
    yb = y8[::-1, 4:].transpose(1, 0, 2).reshape(batch * seq, S5_COLS)
    return s5_out(yf, yb, z_s5, d, w_glu, b_glu)


def _prep_body(z_ref, qg_ref, kg_ref, c_ref, s_ref, q_ref, k_ref, v_ref, *, nq, nk, nv, rope):
    def rot(x):
        if not rope:
            return x
        half = HEAD_DIM // 2
        swapped = jnp.concatenate([x[:, half:], x[:, :half]], axis=1)
        return x * c_ref[...] + swapped * s_ref[...]

    qg = qg_ref[...]
    kg = kg_ref[...]
    for h in range(nq):
        x = z_ref[:, h * HEAD_DIM:(h + 1) * HEAD_DIM]
        q_ref[0, h] = (rot(_rms(x, qg)) * (ATTN_SCALE * LOG2E)).astype(BF16)
    for h in range(nk):
        x = z_ref[:, (nq + h) * HEAD_DIM:(nq + h + 1) * HEAD_DIM]
        k_ref[0, h] = rot(_rms(x, kg)).astype(BF16)
    tm = z_ref.shape[0]
    ones_col = jnp.where(lax.broadcasted_iota(I32, (tm, LANES - HEAD_DIM), 1) == 0, 1.0, 0.0)
    for h in range(nv):
        lo = (nq + nk + h) * HEAD_DIM
        v_ref[0, h] = jnp.concatenate([z_ref[:, lo:lo + HEAD_DIM], ones_col], axis=1).astype(BF16)


def attn_prep(z, qg, kg, cos_t, sin_t, batch, seq, nq, nk, nv, rope):
    width = (nq + nk + nv) * HEAD_DIM
    tm = min(ROW_TILE, seq)
    nb = seq // tm
    vec = pl.BlockSpec((1, HEAD_DIM), lambda b, i: (0, 0))
    tab = pl.BlockSpec((tm, HEAD_DIM), lambda b, i: (i, 0))
    heads = lambda n, w: pl.BlockSpec((1, n, tm, w), lambda b, i: (b, 0, i, 0))
    return pl.pallas_call(
        functools.partial(_prep_body, nq=nq, nk=nk, nv=nv, rope=rope),
        grid=(batch, nb),
        in_specs=[pl.BlockSpec((tm, width), lambda b, i: (b * nb + i, 0)), vec, vec, tab, tab],
        out_specs=[heads(nq, HEAD_DIM), heads(nk, HEAD_DIM), heads(nv, LANES)],
        out_shape=[_sds((batch, nq, seq, HEAD_DIM), BF16), _sds((batch, nk, seq, HEAD_DIM), BF16),
                   _sds((batch, nv, seq, LANES), BF16)],
        compiler_params=_cp("parallel", "parallel"),
        name="attn_prep",
    )(z, qg, kg, cos_t, sin_t)


def _flash_body(q_ref, k_ref, v_ref, *rest, unit, sub, has_bias):
    if has_bias:
        b_ref, o_ref, m_ref, acc_ref = rest
    else:
        o_ref, m_ref, acc_ref = rest
    i = pl.program_id(2)
    j = pl.program_id(3)
    tq = q_ref.shape[2]
    tk = k_ref.shape[2]

    @pl.when(j == 0)
    def _():
        m_ref[...] = jnp.full_like(m_ref, -jnp.inf)
        acc_ref[...] = jnp.zeros_like(acc_ref)

    k = k_ref[0, 0]
    v = v_ref[0, 0]
    for g in range(tq // sub):
        r0 = g * sub
        s = lax.dot_general(q_ref[0, 0, r0:r0 + sub, :], k, (((1,), (1,)), ((), ())),
                            preferred_element_type=F32)
        if has_bias:
            row_unit = i * (tq // unit) + r0 // unit
            tiles = []
            for c in range(tk // unit):
                sel = jnp.clip(j * (tk // unit) + c - row_unit, -2, 2) + 2
                tiles.append(b_ref[0, sel, r0 % unit:r0 % unit + sub, :])
            s = s + (tiles[0] if len(tiles) == 1 else jnp.concatenate(tiles, axis=1))
        blocks = [s[:, b * LANES:(b + 1) * LANES] for b in range(tk // LANES)]
        bmax = blocks[0]
        for blk in blocks[1:]:
            bmax = jnp.maximum(bmax, blk)
        m_prev = m_ref[r0:r0 + sub, :]
        m_new = jnp.maximum(m_prev, jnp.max(bmax, axis=-1, keepdims=True))
        alpha = jnp.exp2(m_prev - m_new)
        p = jnp.concatenate([jnp.exp2(blk - m_new) for blk in blocks], axis=1).astype(BF16)
        acc_ref[r0:r0 + sub, :] = alpha * acc_ref[r0:r0 + sub, :] + jnp.dot(p, v, preferred_element_type=F32)
        m_ref[r0:r0 + sub, :] = m_new

    @pl.when(j == pl.num_programs(3) - 1)
    def _():
        acc = acc_ref[...]
        o_ref[0, 0] = acc[:, :HEAD_DIM] / acc[:, HEAD_DIM:HEAD_DIM + 1]


def flash_attention(q, k, v, k_of_q, v_of_q, bias=None):
    batch, nq, seq, _ = q.shape
    tq = min(ATTN_Q_TILE, seq)
    tk = min(ATTN_K_TILE, seq)
    unit = min(ATTN_UNIT, seq)
    sub = min(ATTN_SUB, seq)
    in_specs = [pl.BlockSpec((1, 1, tq, HEAD_DIM), lambda b, h, i, j: (b, h, i, 0)),
                pl.BlockSpec((1, 1, tk, HEAD_DIM), lambda b, h, i, j: (b, k_of_q(h), j, 0)),
                pl.BlockSpec((1, 1, tk, LANES), lambda b, h, i, j: (b, v_of_q(h), j, 0))]
    args = [q, k, v]
    if bias is not None:
        assert bias.shape[2] == unit
        in_specs.append(pl.BlockSpec((1, 5, unit, unit), lambda b, h, i, j: (v_of_q(h), 0, 0, 0)))
        args.append(bias)
    return pl.pallas_call(
        functools.partial(_flash_body, unit=unit, sub=sub, has_bias=bias is not None),
        grid=(batch, nq, seq // tq, seq // tk),
        in_specs=in_specs,
        out_specs=pl.BlockSpec((1, 1, tq, HEAD_DIM), lambda b, h, i, j: (b, h, i, 0)),
        out_shape=_sds((batch, nq, seq, HEAD_DIM)),
        scratch_shapes=[pltpu.VMEM((tq, LANES), F32), pltpu.VMEM((tq, LANES), F32)],
        compiler_params=_cp("parallel", "parallel", "parallel", "arbitrary"),
        name="flash_bias" if bias is not None else "flash",
    )(*args)


def _t5_bucket_table(t):
    rel = np.arange(-(3 * t - 1), 3 * t)
    half = NUM_BUCKETS // 2
    max_exact = half // 2
    ret = np.where(rel > 0, half, 0)
    n = np.abs(rel)
    nf = np.maximum(n, 1).astype(np.float64)
    large = max_exact + (np.log(nf / max_exact) / math.log(MAX_DISTANCE / max_exact) * (half - max_exact)).astype(np.int64)
    large = np.minimum(large, half - 1)
    return ret + np.where(n < max_exact, n, large)


def diff_bias_tiles(rel_bias, t):
    assert t >= MAX_DISTANCE, "offsets of two or more tiles must lie in the saturated buckets"
    heads = rel_bias.shape[1]
    vec = rel_bias[jnp.asarray(_t5_bucket_table(t), dtype=I32)].T * LOG2E
    tiles = []
    for d in range(-2, 3):
        lo = d * t + 2 * t
        w = vec[:, lo:lo + 2 * t - 1]
        u = jnp.concatenate([w[:, t - 1:], jnp.zeros((heads, 1), F32), w[:, :t - 1]], axis=1)
        skew = jnp.tile(u, (1, t))[:, :t * (2 * t - 1)].reshape(heads, t, 2 * t - 1)
        tiles.append(skew[:, :, :t])
    return jnp.stack(tiles, axis=1)


def rope_tables(seq):
    rows = seq // GRID_W
    row = jnp.repeat(jnp.arange(rows, dtype=F32), GRID_W)
    col = jnp.tile(jnp.arange(GRID_W, dtype=F32), rows)
    freq = ROPE_THETA ** (-jnp.arange(0, ROPE_AXIS_DIM, 2, dtype=F32) / ROPE_AXIS_DIM)
    ang = jnp.concatenate([row[:, None] * freq, col[:, None] * freq], -1)
    cos, sin = jnp.cos(ang), jnp.sin(ang)
    return jnp.concatenate([cos, cos], -1), jnp.concatenate([-sin, sin], -1)


def _conv3_body(x0, x1, x2, w0, w1, w2, b0, b1, b2, o0, o1, o2):
    for x_ref, w_ref, b_ref, o_ref in ((x0, w0, b0, o0), (x1, w1, b1, o1), (x2, w2, b2, o2)):
        x = x_ref[0]
        n = x.shape[0]
        t = lax.broadcasted_iota(I32, (n, 1), 0)
        prev = jnp.where(t == 0, 0.0, pltpu.roll(x, 1, 0))
        nxt = jnp.where(t == n - 1, 0.0, pltpu.roll(x, n - 1, 0))
        o_ref[0] = w_ref[0:1] * prev + w_ref[1:2] * x + w_ref[2:3] * nxt + b_ref[...]


def hyena_conv3(z_hy, conv_w, conv_b, batch, seq):
    z = z_hy.reshape(batch, seq, HY_COLS)
    nc = HY_WIDTH // LANES
    x_spec = lambda g: pl.BlockSpec((1, seq, LANES), lambda b, c: (b, 0, g * nc + c))
    w_spec = lambda g: pl.BlockSpec((3, LANES), lambda b, c: (0, g * nc + c))
    b_spec = lambda g: pl.BlockSpec((1, LANES), lambda b, c: (0, g * nc + c))
    out = pl.BlockSpec((1, seq, LANES), lambda b, c: (b, 0, c))
    return pl.pallas_call(
        _conv3_body,
        grid=(batch, nc),
        in_specs=[x_spec(0), x_spec(1), x_spec(2), w_spec(0), w_spec(1), w_spec(2),
                  b_spec(0), b_spec(1), b_spec(2)],
        out_specs=[out, out, out],
        out_shape=[_sds((batch, seq, HY_WIDTH))] * 3,
        compiler_params=_cp("parallel", "parallel"),
        name="hyena_conv3",
    )(z, z, z, conv_w, conv_w, conv_w, conv_b, conv_b, conv_b)


def _filt_body(feat_ref, dist_ref, w1, b1, f1, w2, b2, f2, w3, dec, h_ref, s_ref):
    dot = functools.partial(jnp.dot, precision=HI, preferred_element_type=F32)
    hid = jnp.sin(f1[...] * (dot(feat_ref[...], w1[...]) + b1[...]))
    hid = jnp.sin(f2[...] * (dot(hid, w2[...]) + b2[...]))
    h = dot(hid, w3[...]) * jnp.exp(-dist_ref[...] * jnp.abs(dec[...]))
    h_ref[...] = h

    @pl.when(pl.program_id(0) == 0)
    def _():
        s_ref[...] = jnp.zeros_like(s_ref)

    s_ref[...] += jnp.sum(jnp.abs(h), axis=0, keepdims=True)


def hyena_filter(seq, w1, b1, f1, w2, b2, f2, w3, decay):
    j = jnp.arange(seq, dtype=F32)
    tt = j / seq
    ang = 2.0 * jnp.pi * tt[:, None] * jnp.arange(1, HY_BANDS + 1, dtype=F32)
    feat = jnp.concatenate([tt[:, None], jnp.cos(ang), jnp.sin(ang)], -1)
    feat = jnp.pad(feat, ((0, 0), (0, LANES - HY_EMB)))
    half = seq // 2
    dist = (jnp.abs(j - half) / half)[:, None]
    w1p = jnp.pad(w1, ((0, LANES - HY_EMB), (0, 0)))
    tm = min(1024, seq)
    hw = HY_ORDER * HY_WIDTH
    const = lambda i: (0, 0)
    return pl.pallas_call(
        _filt_body,
        grid=(seq // tm,),
        in_specs=[pl.BlockSpec((tm, LANES), lambda i: (i, 0)), pl.BlockSpec((tm, 1), lambda i: (i, 0)),
                  pl.BlockSpec((LANES, HY_FFN), const), pl.BlockSpec((1, HY_FFN), const),
                  pl.BlockSpec((1, HY_FFN), const), pl.BlockSpec((HY_FFN, HY_FFN), const),
                  pl.BlockSpec((1, HY_FFN), const), pl.BlockSpec((1, HY_FFN), const),
                  pl.BlockSpec((HY_FFN, hw), const), pl.BlockSpec((1, hw), const)],
        out_specs=[pl.BlockSpec((tm, hw), lambda i: (i, 0)), pl.BlockSpec((1, hw), const)],
        out_shape=[_sds((seq, hw)), _sds((1, hw))],
        compiler_params=_cp("arbitrary"),
        name="hyena_filter",
    )(feat, dist, w1p, b1[None], f1[None], w2, b2[None], f2[None], w3, decay.reshape(1, hw))


def dft_tables(seq):
    n = 2 * seq
    n2 = FFT_N2
    n1 = n // n2
    n1h = n1 // 2
    nk = -(-(n1h + 1) // SUBLANES) * SUBLANES
    k1 = jnp.arange(nk, dtype=I32)
    live = (k1 <= n1h).astype(F32)
    ang = (2.0 * jnp.pi / n1) * ((k1[:, None] * jnp.arange(n1h, dtype=I32)[None, :]) % n1).astype(F32)
    outer_fwd = jnp.concatenate([jnp.cos(ang), -jnp.sin(ang)], axis=0) * jnp.tile(live, 2)[:, None]
    t1 = jnp.arange(n1h, dtype=I32) + n1 // 4
    ang = (2.0 * jnp.pi / n1) * ((t1[:, None] * k1[None, :]) % n1).astype(F32)
    weight = live * jnp.where((k1 == 0) | (k1 == n1h), 1.0, 2.0) / n
    outer_inv = jnp.concatenate([jnp.cos(ang), -jnp.sin(ang)], axis=1) * jnp.tile(weight, 2)[None, :]
    k2 = jnp.arange(n2, dtype=I32)
    t2 = jnp.arange(n2, dtype=I32)
    phase = (t2[None, None, :] * (k2[None, :, None] * n1 + k1[:, None, None])) % n
    ang = (2.0 * jnp.pi / n) * phase.astype(F32)
    fr, fi = jnp.cos(ang), -jnp.sin(ang)
    inner = jnp.concatenate([jnp.concatenate([fr, -fi], axis=2),
                             jnp.concatenate([fi, fr], axis=2)], axis=1)
    return outer_fwd, outer_inv, inner, jnp.swapaxes(inner, 1, 2)


def _tile_rows(ref, s, n):
    return ref[pl.ds(s, n, stride=SUBLANES), :]


def _fft_outer_body(f_ref, x_ref, o_ref):
    n1h = x_ref.shape[0]
    rows = f_ref.shape[0]
    x2 = x_ref.reshape(n1h * SUBLANES, LANES)
    o2 = o_ref.reshape(rows * SUBLANES, LANES)
    x = jnp.concatenate([_tile_rows(x2, s, n1h) for s in range(SUBLANES)], axis=1)
    r = jnp.dot(f_ref[...], x, precision=HI, preferred_element_type=F32)
    for s in range(SUBLANES):
        o2[pl.ds(s, rows, stride=SUBLANES), :] = r[:, s * LANES:(s + 1) * LANES]


def fft_outer(x4, table):
    batch, n1h, n2, ch = x4.shape
    nk = table.shape[0] // 2
    return pl.pallas_call(
        _fft_outer_body,
        grid=(batch, n2 // SUBLANES, ch // LANES),
        in_specs=[pl.BlockSpec((2 * nk, n1h), lambda b, c, l: (0, 0)),
                  pl.BlockSpec((None, n1h, SUBLANES, LANES), lambda b, c, l: (b, 0, c, l))],
        out_specs=pl.BlockSpec((None, 2, nk, SUBLANES, LANES), lambda b, c, l: (b, 0, 0, c, l)),
        out_shape=_sds((batch, 2, nk, n2, ch)),
        compiler_params=_cp("parallel", "parallel", "parallel"),
        name="fft_outer",
    )(table, x4)


def _fft_filter_body(m_ref, s_ref, a_ref, o_ref):
    n2 = FFT_N2
    a = a_ref[0, :, 0].reshape(2 * n2, HY_WIDTH)
    x = jnp.dot(m_ref[0], a, precision=HI, preferred_element_type=F32) * (1.0 / s_ref[...])
    o_ref[:, 0] = x.reshape(2, n2, HY_WIDTH)


def fft_filter(a5, inner, s):
    _, _, n1, n2, ch = a5.shape
    return pl.pallas_call(
        _fft_filter_body,
        grid=(n1, ch // HY_WIDTH),
        in_specs=[pl.BlockSpec((1, 2 * n2, 2 * n2), lambda k, c: (k, 0, 0)),
                  pl.BlockSpec((1, HY_WIDTH), lambda k, c: (0, c)),
                  pl.BlockSpec((1, 2, 1, n2, HY_WIDTH), lambda k, c: (0, 0, k, 0, c))],
        out_specs=pl.BlockSpec((2, 1, n2, HY_WIDTH), lambda k, c: (0, k, 0, c)),
        out_shape=_sds((2, n1, n2, ch)),
        compiler_params=_cp("parallel", "parallel"),
        name="fft_filter",
    )(inner, s, a5)


def _fft_mid_body(m_ref, mt_ref, h_ref, a_ref, o_ref, *, nb):
    n2 = FFT_N2
    dot = functools.partial(jnp.dot, precision=HI, preferred_element_type=F32)
    m = m_ref[0]
    mt = mt_ref[0]
    hr = h_ref[0, 0]
    hi = h_ref[1, 0]
    for b in range(nb):
        x = dot(m, a_ref[b, :, 0].reshape(2 * n2, HY_WIDTH))
        xr, xi = x[:n2], x[n2:]
        y = jnp.concatenate([xr * hr - xi * hi, xr * hi + xi * hr], axis=0)
        o_ref[b, :, 0] = dot(mt, y).reshape(2, n2, HY_WIDTH)


def fft_mid(a5, inner, inner_t, hf, order):
    nb, _, n1, n2, ch = a5.shape
    mat = pl.BlockSpec((1, 2 * n2, 2 * n2), lambda k: (k, 0, 0))
    blk = pl.BlockSpec((nb, 2, 1, n2, ch), lambda k: (0, 0, k, 0, 0))
    return pl.pallas_call(
        functools.partial(_fft_mid_body, nb=nb),
        grid=(n1,),
        in_specs=[mat, mat, pl.BlockSpec((2, 1, n2, ch), lambda k: (0, k, 0, order)), blk],
        out_specs=blk,
        out_shape=_sds(a5.shape),
        compiler_params=_cp("parallel"),
        name="fft_mid",
    )(inner, inner_t, hf, a5)


def _fft_inv_body(g_ref, b_ref, gate_ref, y_ref, fb_ref, o_ref):
    n1h, rows = g_ref.shape
    b2 = b_ref.reshape(rows * SUBLANES, LANES)
    gate2 = gate_ref.reshape(n1h * SUBLANES, LANES)
    y2 = y_ref.reshape(n1h * SUBLANES, LANES)
    o2 = o_ref.reshape(n1h * SUBLANES, LANES)
    bm = jnp.concatenate([_tile_rows(b2, s, rows) for s in range(SUBLANES)], axis=1)
    c = jnp.dot(g_ref[...], bm, precision=HI, preferred_element_type=F32)
    fb = fb_ref[...]
    for s in range(SUBLANES):
        conv = c[:, s * LANES:(s + 1) * LANES]
        o2[pl.ds(s, n1h, stride=SUBLANES), :] = _tile_rows(gate2, s, n1h) * (conv + fb * _tile_rows(y2, s, n1h))


def fft_inv_gate(b5, table, gate4, y4, fbias):
    batch, _, nk, n2, ch = b5.shape
    n1h = table.shape[0]
    blk = pl.BlockSpec((None, n1h, SUBLANES, LANES), lambda b, c, l: (b, 0, c, l))
    return pl.pallas_call(
        _fft_inv_body,
        grid=(batch, n2 // SUBLANES, ch // LANES),
        in_specs=[pl.BlockSpec((n1h, 2 * nk), lambda b, c, l: (0, 0)),
                  pl.BlockSpec((None, 2, nk, SUBLANES, LANES), lambda b, c, l: (b, 0, 0, c, l)), blk, blk,
                  pl.BlockSpec((1, LANES), lambda b, c, l: (0, l))],
        out_specs=blk,
        out_shape=_sds((batch, n1h, n2, ch)),
        compiler_params=_cp("parallel", "parallel", "parallel"),
        name="fft_inv_gate",
    )(table, b5, gate4, y4, fbias.reshape(1, ch))


def hyena_spectrum(seq, tabs, w1, b1, f1, w2, b2, f2, w3, decay):
    outer_fwd, _, inner, _ = tabs
    n1h = outer_fwd.shape[1]
    hw = HY_ORDER * HY_WIDTH
    h, s = hyena_filter(seq, w1, b1, f1, w2, b2, f2, w3, decay)
    return fft_filter(fft_outer(h.reshape(1, n1h, FFT_N2, hw), outer_fwd), inner, s)


def hyena_mixer(z_hy, conv_w, conv_b, hf, f_bias, tabs, batch, seq):
    outer_fwd, outer_inv, inner, inner_t = tabs
    n1h = outer_fwd.shape[1]
    shape4 = (batch, n1h, FFT_N2, HY_WIDTH)
    v, g0, g1 = hyena_conv3(z_hy, conv_w, conv_b, batch, seq)
    y = v.reshape(shape4)
    for order, gate in enumerate((g0, g1)):
        bm = fft_mid(fft_outer(y, outer_fwd), inner, inner_t, hf, order)
        y = fft_inv_gate(bm, outer_inv, gate.reshape(shape4), y, f_bias[order])
    return y.reshape(batch * seq, HY_WIDTH)


def _mix_body(lam_ref, h_ref, s5_ref, gqa_ref, diff_ref, hy_ref, gs5, ggqa, gsub, ghy, w_ref, o_ref, *, keep):
    lam = lam_ref[0]
    a = _rms(s5_ref[...], gs5[...])
    b = _rms(jnp.concatenate([gqa_ref[0, h] for h in range(GQA_HEADS)], axis=1), ggqa[...])
    c = jnp.concatenate(
        [_rms(diff_ref[0, 2 * h] - lam * diff_ref[0, 2 * h + 1], gsub[...]) * keep for h in range(DIFF_HEADS)],
        axis=1)
    d = _rms(hy_ref[...], ghy[...])
    mixed = jnp.concatenate([a, b, c, d], axis=1).astype(BF16)
    o_ref[...] = h_ref[...] + jnp.dot(mixed, w_ref[...], preferred_element_type=F32)


def mix_out(lam, h, y_s5, o_gqa, o_diff, y_hy, gs5, ggqa, gsub, ghy, w_out, batch, seq, lambda_init):
    tm = min(ROW_TILE, seq)
    nb = seq // tm
    row = lambda w: pl.BlockSpec((tm, w), lambda b, i: (b * nb + i, 0))
    vec = lambda w: pl.BlockSpec((1, w), lambda b, i: (0, 0))
    heads = lambda n: pl.BlockSpec((1, n, tm, HEAD_DIM), lambda b, i: (b, 0, i, 0))
    return pl.pallas_call(
        functools.partial(_mix_body, keep=1.0 - lambda_init),
        grid=(batch, nb),
        in_specs=[pl.BlockSpec(memory_space=pltpu.SMEM), row(D_MODEL), row(GROUP_WIDTH), heads(GQA_HEADS),
                  heads(2 * DIFF_HEADS), row(GROUP_WIDTH), vec(GROUP_WIDTH), vec(GROUP_WIDTH), vec(HEAD_DIM),
                  vec(GROUP_WIDTH), pl.BlockSpec((D_MODEL, D_MODEL), lambda b, i: (0, 0))],
        out_specs=row(D_MODEL),
        out_shape=_sds((batch * seq, D_MODEL)),
        compiler_params=_cp("parallel", "parallel"),
        name="mix_out",
    )(lam, h, y_s5, o_gqa, o_diff, y_hy, gs5, ggqa, gsub, ghy, w_out)


def _router_body(h_ref, g_ref, w_ref, b_ref, xn_ref, ti_ref, tw_ref, cnt_ref):
    xn = _rms(h_ref[...], g_ref[...])
    xn_ref[...] = xn
    logits = jnp.dot(xn, w_ref[...], precision=HI, preferred_element_type=F32) + b_ref[...]
    lane = lax.broadcasted_iota(I32, logits.shape, 1)
    vals = logits
    tops, idxs = [], []
    hot = jnp.zeros(logits.shape, F32)
    for _ in range(TOP_K):
        m = jnp.max(vals, axis=-1, keepdims=True)
        idx = jnp.min(jnp.where(vals == m, lane, N_EXPERTS), axis=-1, keepdims=True)
        sel = lane == idx
        tops.append(m)
        idxs.append(idx)
        hot = hot + sel.astype(F32)
        vals = jnp.where(sel, -jnp.inf, vals)
    es = [jnp.exp(t - tops[0]) for t in tops]
    den = es[0] + es[1] + es[2] + es[3]
    ti_ref[...] = jnp.concatenate(idxs, axis=1)
    tw_ref[...] = jnp.concatenate([e / den for e in es], axis=1)

    @pl.when(pl.program_id(0) == 0)
    def _():
        cnt_ref[...] = jnp.zeros_like(cnt_ref)

    cnt_ref[...] += jnp.sum(hot, axis=0, keepdims=True)


def moe_router(h, g, w_router, b_router):
    t = h.shape[0]
    tm = min(ROW_TILE, t)
    const = lambda i: (0, 0)
    return pl.pallas_call(
        _router_body,
        grid=(t // tm,),
        in_specs=[pl.BlockSpec((tm, D_MODEL), lambda i: (i, 0)), pl.BlockSpec((1, D_MODEL), const),
                  pl.BlockSpec((D_MODEL, N_EXPERTS), const), pl.BlockSpec((1, N_EXPERTS), const)],
        out_specs=[pl.BlockSpec((tm, D_MODEL), lambda i: (i, 0)), pl.BlockSpec((tm, TOP_K), lambda i: (i, 0)),
                   pl.BlockSpec((tm, TOP_K), lambda i: (i, 0)), pl.BlockSpec((1, N_EXPERTS), const)],
        out_shape=[_sds((t, D_MODEL)), _sds((t, TOP_K), I32), _sds((t, TOP_K)), _sds((1, N_EXPERTS))],
        compiler_params=_cp("arbitrary"),
        name="moe_router",
    )(h, g, w_router, b_router)


def _rank_body(ti_ref, off_ref, pos_ref, carry_ref):
    @pl.when(pl.program_id(0) == 0)
    def _():
        carry_ref[...] = jnp.zeros_like(carry_ref)

    ti = ti_ref[...]
    tm = ti.shape[0]
    lane = lax.broadcasted_iota(I32, (tm, N_EXPERTS), 1)
    hots = [lane == ti[:, k:k + 1] for k in range(TOP_K)]
    hot = sum(h.astype(F32) for h in hots)
    r = lax.broadcasted_iota(I32, (tm, tm), 0)
    c = lax.broadcasted_iota(I32, (tm, tm), 1)
    below = jnp.where(r > c, 1.0, 0.0).astype(BF16)
    before = jnp.dot(below, hot.astype(BF16), preferred_element_type=F32) + carry_ref[...] + off_ref[...]
    pos = [jnp.sum(jnp.where(h, before, 0.0), axis=-1, keepdims=True) for h in hots]
    pos_ref[...] = jnp.concatenate(pos, axis=1).astype(I32)
    carry_ref[...] += jnp.sum(hot, axis=0, keepdims=True)


def moe_rank(top_i, offsets):
    t = top_i.shape[0]
    tm = min(ROW_TILE, t)
    return pl.pallas_call(
        _rank_body,
        grid=(t // tm,),
        in_specs=[pl.BlockSpec((tm, TOP_K), lambda i: (i, 0)), pl.BlockSpec((1, N_EXPERTS), lambda i: (0, 0))],
        out_specs=pl.BlockSpec((tm, TOP_K), lambda i: (i, 0)),
        out_shape=_sds((t, TOP_K), I32),
        scratch_shapes=[pltpu.VMEM((1, N_EXPERTS), F32)],
        compiler_params=_cp("arbitrary"),
        name="moe_rank",
    )(top_i, offsets)


def _row_copy(src, dst, sem):
    return pltpu.make_async_copy(src, dst, sem)


def _slot_rows(ref, slot):
    return ref.at[pl.ds(pl.multiple_of(slot * ROW_TILES, ROW_TILES), ROW_TILES)]


def _scatter_body(pos_ref, x_ref, zero_hbm, xs_hbm, buf, sem, *, tm):
    del zero_hbm
    x = x_ref[...]
    for s in range(ROW_TILES):
        buf[pl.ds(s, tm, stride=ROW_TILES), :] = x[:, s * LANES:(s + 1) * LANES]

    def start(r, c):
        for k in range(TOP_K):
            _row_copy(_slot_rows(buf, r), _slot_rows(xs_hbm, pos_ref[r * TOP_K + k]), sem).start()
        return c

    lax.fori_loop(0, tm, start, 0)
    for _ in range(TOP_K):
        _row_copy(buf, xs_hbm.at[pl.ds(0, tm * ROW_TILES)], sem).wait()


def moe_scatter(pos_flat, x, n_slots):
    t = x.shape[0]
    tm = min(MOE_TOK_TILE, t)
    zeros = jnp.zeros((n_slots * ROW_TILES, LANES), F32)
    return pl.pallas_call(
        functools.partial(_scatter_body, tm=tm),
        grid=(t // tm,),
        in_specs=[pl.BlockSpec((tm * TOP_K,), lambda i: (i,), memory_space=pltpu.SMEM),
                  pl.BlockSpec((tm, D_MODEL), lambda i: (i, 0)), pl.BlockSpec(memory_space=pl.ANY)],
        out_specs=pl.BlockSpec(memory_space=pl.ANY),
        out_shape=_sds((n_slots * ROW_TILES, LANES)),
        scratch_shapes=[pltpu.VMEM((tm * ROW_TILES, LANES), F32), pltpu.SemaphoreType.DMA(())],
        input_output_aliases={2: 0},
        compiler_params=_cp("arbitrary"),
        name="moe_scatter",
    )(pos_flat, x, zeros)


def _experts_body(te_ref, nv_ref, x_ref, wgu_ref, bgu_ref, wd_ref, bd_ref, o_ref):
    del te_ref
    j = pl.program_id(0)
    dff = wd_ref.shape[1]

    @pl.when(j < nv_ref[0])
    def _():
        tm = x_ref.shape[0] // ROW_TILES
        x = jnp.concatenate([x_ref[pl.ds(s, tm, stride=ROW_TILES), :] for s in range(ROW_TILES)],
                            axis=1).astype(BF16)
        hgu = jnp.dot(x, wgu_ref[0], preferred_element_type=F32) + bgu_ref[0]
        glu = jnp.minimum(hgu[:, :dff], SWIGLU_LIMIT)
        lin = jnp.clip(hgu[:, dff:], -SWIGLU_LIMIT, SWIGLU_LIMIT)
        act = glu * jax.nn.sigmoid(SWIGLU_ALPHA * glu) * (lin + 1.0)
        y = jnp.dot(act.astype(BF16), wd_ref[0], preferred_element_type=F32) + bd_ref[0]
        for s in range(ROW_TILES):
            o_ref[pl.ds(s, tm, stride=ROW_TILES), :] = y[:, s * LANES:(s + 1) * LANES]

    @pl.when(j >= nv_ref[0])
    def _():
        o_ref[...] = jnp.zeros_like(o_ref)


def moe_experts(tile_expert, n_valid, xs, w_gu, b_gu, w_down, b_down):
    n_slots = xs.shape[0] // ROW_TILES
    tm = MOE_TILE
    dff = w_down.shape[1]
    blk = pl.BlockSpec((tm * ROW_TILES, LANES), lambda j, te, nv: (j, 0))
    return pl.pallas_call(
        _experts_body,
        grid_spec=pltpu.PrefetchScalarGridSpec(
            num_scalar_prefetch=2,
            grid=(n_slots // tm,),
            in_specs=[blk,
                      pl.BlockSpec((1, D_MODEL, 2 * dff), lambda j, te, nv: (te[j], 0, 0)),
                      pl.BlockSpec((1, 1, 2 * dff), lambda j, te, nv: (te[j], 0, 0)),
                      pl.BlockSpec((1, dff, D_MODEL), lambda j, te, nv: (te[j], 0, 0)),
                      pl.BlockSpec((1, 1, D_MODEL), lambda j, te, nv: (te[j], 0, 0))],
            out_specs=blk),
        out_shape=_sds((n_slots * ROW_TILES, LANES)),
        compiler_params=_cp("arbitrary"),
        name="moe_experts",
    )(tile_expert, n_valid, xs, w_gu, b_gu, w_down, b_down)


def _combine_body(pos_ref, w_ref, ys_hbm, o_ref, b0, b1, b2, b3, mixed, sem, *, tm):
    bufs = (b0, b1, b2, b3)

    def start(r, c):
        for k in range(TOP_K):
            _row_copy(_slot_rows(ys_hbm, pos_ref[r * TOP_K + k]), _slot_rows(bufs[k], r), sem).start()
        return c

    lax.fori_loop(0, tm, start, 0)
    for k in range(TOP_K):
        _row_copy(ys_hbm.at[pl.ds(0, tm * ROW_TILES)], bufs[k], sem).wait()

    def mix(q, c):
        for u in range(MIX_UNROLL):
            r = q * MIX_UNROLL + u
            rows = pl.ds(pl.multiple_of(r * ROW_TILES, ROW_TILES), ROW_TILES)
            acc = w_ref[r * TOP_K] * b0[rows, :]
            for k in range(1, TOP_K):
                acc = acc + w_ref[r * TOP_K + k] * bufs[k][rows, :]
            mixed[rows, :] = acc
        return c

    lax.fori_loop(0, tm // MIX_UNROLL, mix, 0)
    o_ref[...] = jnp.concatenate([mixed[pl.ds(s, tm, stride=ROW_TILES), :] for s in range(ROW_TILES)], axis=1)


def moe_combine(pos_flat, w_flat, ys, t):
    tm = min(MOE_TOK_TILE, t)
    smem = pl.BlockSpec((tm * TOP_K,), lambda i: (i,), memory_space=pltpu.SMEM)
    tile_buf = pltpu.VMEM((tm * ROW_TILES, LANES), F32)
    return pl.pallas_call(
        functools.partial(_combine_body, tm=tm),
        grid=(t // tm,),
        in_specs=[smem, smem, pl.BlockSpec(memory_space=pl.ANY)],
        out_specs=pl.BlockSpec((tm, D_MODEL), lambda i: (i, 0)),
        out_shape=_sds((t, D_MODEL)),
        scratch_shapes=[tile_buf] * (TOP_K + 1) + [pltpu.SemaphoreType.DMA(())],
        compiler_params=_cp("arbitrary"),
        name="moe_combine",
    )(pos_flat, w_flat, ys)


def moe(h, g, w_router, b_router, w_gu, b_gu, w_down, b_down):
    t = h.shape[0]
    xn, top_i, top_w, counts = moe_router(h, g, w_router, b_router)
    n_tiles = (t * TOP_K) // MOE_TILE + N_EXPERTS
    padded = jnp.ceil(counts[0] / MOE_TILE) * MOE_TILE
    ends = jnp.cumsum(padded)
    offsets = (ends - padded)[None]
    tile_start = jnp.arange(n_tiles, dtype=F32) * MOE_TILE
    tile_expert = jnp.minimum(jnp.sum(ends[None, :] <= tile_start[:, None], axis=1), N_EXPERTS - 1).astype(I32)
    n_valid = (ends[-1:] / MOE_TILE).astype(I32)
    pos = moe_rank(top_i, offsets).reshape(t * TOP_K)
    xs = moe_scatter(pos, xn, n_tiles * MOE_TILE)
    ys = moe_experts(tile_expert, n_valid, xs, w_gu, b_gu, w_down, b_down)
    return moe_combine(pos, top_w.reshape(t * TOP_K), ys, t)


def _ple_body(h_ref, m_ref, p_ref, wp_ref, wg_ref, o_ref):
    h = h_ref[...] + m_ref[...]
    e = jnp.dot(p_ref[...].astype(BF16), wp_ref[...], preferred_element_type=F32)
    gate = jnp.dot(_rms(h).astype(BF16), wg_ref[...], preferred_element_type=F32)
    o_ref[...] = h + e * jax.nn.sigmoid(gate)


def ple(h, moe_out, p, w_ple, w_gate):
    t = h.shape[0]
    tm = min(ROW_TILE, t)
    row = pl.BlockSpec((tm, D_MODEL), lambda i: (i, 0))
    return pl.pallas_call(
        _ple_body,
        grid=(t // tm,),
        in_specs=[row, row, pl.BlockSpec((tm, PLE_DIM), lambda i: (i, 0)),
                  pl.BlockSpec((PLE_DIM, D_MODEL), lambda i: (0, 0)),
                  pl.BlockSpec((D_MODEL, D_MODEL), lambda i: (0, 0))],
        out_specs=row,
        out_shape=_sds((t, D_MODEL)),
        compiler_params=_cp("parallel"),
        name="ple",
    )(h, moe_out, p, w_ple, w_gate)


def _deinterleave(n):
    return np.concatenate([np.arange(0, n, 2), np.arange(1, n, 2)])


def _halves(w):
    pairs = w.reshape(*w.shape[:-1], w.shape[-1] // 2, 2)
    return jnp.concatenate([pairs[..., 0], pairs[..., 1]], axis=-1)


def prepare_layer(i, prm):
    pair = _deinterleave(HEAD_DIM)
    cols = np.arange(IN_COLS)
    for h in range(GQA_HEADS + GQA_KV_HEADS):
        lo = OFF_GQA + h * HEAD_DIM
        cols[lo:lo + HEAD_DIM] = lo + pair
    lam = (jnp.exp(jnp.sum(prm["diff_lam_q1"][i] * prm["diff_lam_k1"][i]))
           - jnp.exp(jnp.sum(prm["diff_lam_q2"][i] * prm["diff_lam_k2"][i])) + _lambda_init(i))
    return dict(
        norm1_g=prm["norm1_g"][i][None],
        w_in=prm["w_in"][i][:, cols].astype(BF16),
        s5=s5_tables(prm["s5_lam_re"][i], prm["s5_lam_im"][i], prm["s5_log_dt"][i], prm["s5_b_re"][i],
                     prm["s5_b_im"][i], prm["s5_c_re"][i], prm["s5_c_im"][i]),
        s5_d=prm["s5_d"][i][None],
        s5_w_glu=prm["s5_w_glu"][i].astype(BF16),
        s5_b_glu=prm["s5_b_glu"][i][None],
        gqa_q_g=prm["gqa_q_g"][i][pair][None],
        gqa_k_g=prm["gqa_k_g"][i][pair][None],
        diff_q_g=prm["diff_q_g"][i][None],
        diff_k_g=prm["diff_k_g"][i][None],
        diff_lam=lam.reshape(1).astype(F32),
        diff_subln_g=prm["diff_subln_g"][i][None],
        out_g_s5=prm["out_g_s5"][i][None],
        out_g_gqa=prm["out_g_gqa"][i][None],
        out_g_hy=prm["out_g_hy"][i][None],
        w_out=prm["w_out"][i].astype(BF16),
        norm2_g=prm["norm2_g"][i][None],
        w_router=prm["w_router"][i],
        b_router=prm["b_router"][i][None],
        w_gu=_halves(prm["w_gu"][i]).astype(BF16),
        b_gu=_halves(prm["b_gu"][i])[:, None, :],
        w_down=prm["w_down"][i].astype(BF16),
        b_down=prm["b_down"][i][:, None, :],
        w_ple=prm["w_ple"][i].astype(BF16),
        w_ple_gate=prm["w_ple_gate"][i].astype(BF16),
    )


def _lambda_init(i):
    return 0.8 - 0.6 * math.exp(-0.3 * i)


def run_trunk(x, p, prm, layers):
    batch, seq, _ = x.shape
    t = batch * seq
    cos_t, sin_t = rope_tables(seq)
    tabs = dft_tables(seq)
    bias = diff_bias_tiles(prm["rel_bias"], min(ATTN_UNIT, seq))
    h = x.reshape(t, D_MODEL)
    for i, lw in enumerate(layers):
        z_s5, z_gqa, z_diff, z_hy = in_proj(h, lw["norm1_g"], lw["w_in"])
        y_s5 = s5_mixer(z_s5, lw["s5"], lw["s5_d"], lw["s5_w_glu"], lw["s5_b_glu"], batch, seq)
        q, k, v = attn_prep(z_gqa, lw["gqa_q_g"], lw["gqa_k_g"], cos_t, sin_t, batch, seq,
                            GQA_HEADS, GQA_KV_HEADS, GQA_KV_HEADS, True)
        rep = GQA_HEADS // GQA_KV_HEADS
        o_gqa = flash_attention(q, k, v, lambda hq: hq // rep, lambda hq: hq // rep)
        q, k, v = attn_prep(z_diff, lw["diff_q_g"], lw["diff_k_g"], cos_t, sin_t, batch, seq,
                            2 * DIFF_HEADS, 2 * DIFF_HEADS, DIFF_HEADS, False)
        o_diff = flash_attention(q, k, v, lambda hq: hq, lambda hq: hq // 2, bias)
        hf = hyena_spectrum(seq, tabs, prm["hy_w1"][i], prm["hy_b1"][i], prm["hy_freq1"][i], prm["hy_w2"][i],
                            prm["hy_b2"][i], prm["hy_freq2"][i], prm["hy_w3"][i], prm["hy_decay"][i])
        y_hy = hyena_mixer(z_hy, prm["hy_conv_w"][i], prm["hy_conv_b"][i][None], hf, prm["hy_bias"][i],
                           tabs, batch, seq)
        h = mix_out(lw["diff_lam"], h, y_s5, o_gqa, o_diff, y_hy, lw["out_g_s5"], lw["out_g_gqa"],
                    lw["diff_subln_g"], lw["out_g_hy"], lw["w_out"], batch, seq, _lambda_init(i))
        m = moe(h, lw["norm2_g"], lw["w_router"], lw["b_router"], lw["w_gu"], lw["b_gu"], lw["w_down"],
                lw["b_down"])
        h = ple(h, m, p[i].reshape(t, PLE_DIM), lw["w_ple"], lw["w_ple_gate"])
    return h.reshape(batch, seq, D_MODEL)


def kernel(x_prompt, x_sample, p_prompt, p_sample, rel_bias, norm1_g, w_in, s5_lam_re, s5_lam_im, s5_log_dt,
           s5_b_re, s5_b_im, s5_c_re, s5_c_im, s5_d, s5_w_glu, s5_b_glu, gqa_q_g, gqa_k_g, diff_q_g, diff_k_g,
           diff_lam_q1, diff_lam_k1, diff_lam_q2, diff_lam_k2, diff_subln_g, hy_conv_w, hy_conv_b, hy_w1, hy_b1,
           hy_freq1, hy_w2, hy_b2, hy_freq2, hy_w3, hy_decay, hy_bias, out_g_s5, out_g_gqa, out_g_hy, w_out,
           norm2_g, w_router, b_router, w_gu, b_gu, w_down, b_down, w_ple, w_ple_gate):
    prm = dict(rel_bias=rel_bias, norm1_g=norm1_g, w_in=w_in, s5_lam_re=s5_lam_re, s5_lam_im=s5_lam_im,
               s5_log_dt=s5_log_dt, s5_b_re=s5_b_re, s5_b_im=s5_b_im, s5_c_re=s5_c_re, s5_c_im=s5_c_im,
               s5_d=s5_d, s5_w_glu=s5_w_glu, s5_b_glu=s5_b_glu, gqa_q_g=gqa_q_g, gqa_k_g=gqa_k_g,
               diff_q_g=diff_q_g, diff_k_g=diff_k_g, diff_lam_q1=diff_lam_q1, diff_lam_k1=diff_lam_k1,
               diff_lam_q2=diff_lam_q2, diff_lam_k2=diff_lam_k2, diff_subln_g=diff_subln_g,
               hy_conv_w=hy_conv_w, hy_conv_b=hy_conv_b, hy_w1=hy_w1, hy_b1=hy_b1, hy_freq1=hy_freq1,
               hy_w2=hy_w2, hy_b2=hy_b2, hy_freq2=hy_freq2, hy_w3=hy_w3, hy_decay=hy_decay, hy_bias=hy_bias,
               out_g_s5=out_g_s5, out_g_gqa=out_g_gqa, out_g_hy=out_g_hy, w_out=w_out, norm2_g=norm2_g,
               w_router=w_router, b_router=b_router, w_gu=w_gu, b_gu=b_gu, w_down=w_down, b_down=b_down,
               w_ple=w_ple, w_ple_gate=w_ple_gate)
    layers = [prepare_layer(i, prm) for i in range(DEPTH)]
    return (run_trunk(x_prompt, p_prompt, prm, layers), run_trunk(x_sample, p_sample, prm, layers))
```

```python
import functools
import math

import numpy as np
import jax
import jax.numpy as jnp
from jax import lax
from jax.experimental import pallas as pl
from jax.experimental.pallas import tpu as pltpu

F32 = jnp.float32
BF16 = jnp.bfloat16
I32 = jnp.int32
HI = lax.Precision.HIGHEST

D_MODEL = 1024
DEPTH = 2
GROUP_WIDTH = 256
HEAD_DIM = 64
S5_GROUP_DIM = 16
S5_GROUPS = 16
S5_STATE = 64
S5_NSTATE = S5_GROUPS * S5_STATE
GQA_HEADS = 4
GQA_KV_HEADS = 2
DIFF_HEADS = 4
HY_WIDTH = 256
HY_ORDER = 2
HY_BANDS = 16
HY_EMB = 2 * HY_BANDS + 1
HY_FFN = 64
N_EXPERTS = 32
TOP_K = 4
SWIGLU_LIMIT = 7.0
SWIGLU_ALPHA = 1.702
NUM_BUCKETS = 32
MAX_DISTANCE = 128
GRID_W = 64
ROPE_THETA = 10000.0
ROPE_AXIS_DIM = HEAD_DIM // 2
PLE_DIM = 256
NORM_EPS = 1e-6
ATTN_SCALE = HEAD_DIM ** -0.5
LOG2E = math.log2(math.e)

S5_COLS = GROUP_WIDTH
GQA_QW = GQA_HEADS * HEAD_DIM
GQA_KW = GQA_KV_HEADS * HEAD_DIM
GQA_COLS = GQA_QW + 2 * GQA_KW
DIFF_QW = DIFF_HEADS * 2 * HEAD_DIM
DIFF_COLS = 2 * DIFF_QW + DIFF_HEADS * HEAD_DIM
HY_COLS = (HY_ORDER + 1) * HY_WIDTH
OFF_GQA = S5_COLS
OFF_DIFF = OFF_GQA + GQA_COLS
OFF_HY = OFF_DIFF + DIFF_COLS
IN_COLS = OFF_HY + HY_COLS
IN_SPLITS = ((0, OFF_GQA), (OFF_GQA, OFF_DIFF), (OFF_DIFF, OFF_HY), (OFF_HY, IN_COLS))

LANES = 128
SUBLANES = 8
ROW_TILES = D_MODEL // LANES
FFT_N2 = 128
FFT_ROWS = 32
VMEM_LIMIT = 48 * 1024 * 1024

ROW_TILE = 256
ATTN_Q_TILE = 2048
ATTN_K_TILE = 2048
ATTN_UNIT = 512
ATTN_SUB = 512
S5_TIME_BLOCK = 64
MOE_TILE = 512
MOE_TOK_TILE = 256
MIX_UNROLL = 4


def _cp(*sem):
    return pltpu.CompilerParams(dimension_semantics=sem, vmem_limit_bytes=VMEM_LIMIT)


def _sds(shape, dtype=F32):
    return jax.ShapeDtypeStruct(shape, dtype)


def _rms(x, g=None):
    y = x * lax.rsqrt(jnp.mean(x * x, axis=-1, keepdims=True) + NORM_EPS)
    return y if g is None else y * g


def _inproj_body(x_ref, g_ref, w_ref, *o_refs):
    u = _rms(x_ref[...], g_ref[...]).astype(BF16)
    for o_ref, (lo, hi) in zip(o_refs, IN_SPLITS):
        o_ref[...] = jnp.dot(u, w_ref[:, lo:hi], preferred_element_type=F32)


def in_proj(h, g, w):
    t = h.shape[0]
    tm = min(ROW_TILE, t)
    return pl.pallas_call(
        _inproj_body,
        grid=(t // tm,),
        in_specs=[pl.BlockSpec((tm, D_MODEL), lambda i: (i, 0)),
                  pl.BlockSpec((1, D_MODEL), lambda i: (0, 0)),
                  pl.BlockSpec((D_MODEL, IN_COLS), lambda i: (0, 0))],
        out_specs=[pl.BlockSpec((tm, hi - lo), lambda i: (i, 0)) for lo, hi in IN_SPLITS],
        out_shape=[_sds((t, hi - lo)) for lo, hi in IN_SPLITS],
        compiler_params=_cp("parallel"),
        name="in_proj",
    )(h, g, w)


def s5_tables(lam_re, lam_im, log_dt, b_re, b_im, c_re, c_im):
    lr = jnp.minimum(lam_re, -1e-4)
    li = lam_im
    dt = jnp.exp(log_dt)[..., None]
    mag = jnp.exp(lr * dt)
    ar = mag * jnp.cos(li * dt)
    ai = mag * jnp.sin(li * dt)
    den = lr * lr + li * li
    cr = ((ar - 1.0) * lr + ai * li) / den
    ci = (ai * lr - (ar - 1.0) * li) / den
    bbr = cr[..., None] * b_re - ci[..., None] * b_im
    bbi = cr[..., None] * b_im + ci[..., None] * b_re
    eye = jnp.eye(S5_GROUPS, dtype=F32)
    bb = jnp.stack([bbr, bbi], axis=1)
    wb = jnp.einsum("dpgnh,gk->dghpkn", bb, eye).reshape(2 * S5_COLS, 2 * S5_NSTATE)
    cc = jnp.stack([c_re, -c_im], axis=0)
    wc = jnp.einsum("pdghn,gk->pgndkh", cc, eye).reshape(2 * S5_NSTATE, 2 * S5_COLS)
    a_re = jnp.repeat(ar.reshape(2, S5_NSTATE), 4, axis=0)
    a_im = jnp.repeat(ai.reshape(2, S5_NSTATE), 4, axis=0)
    return wb.astype(BF16), wc.astype(BF16), a_re, a_im


def _s5_body(u_ref, wb_ref, wc_ref, ar_ref, ai_ref, y_ref, bu_ref, xr_ref, xi_ref, *, tb):
    @pl.when(pl.program_id(0) == 0)
    def _():
        xr_ref[...] = jnp.zeros_like(xr_ref)
        xi_ref[...] = jnp.zeros_like(xi_ref)

    u = u_ref[...]
    fwd = (lax.broadcasted_iota(I32, (tb * SUBLANES, 1), 0) % SUBLANES) < 4
    uu = jnp.concatenate([jnp.where(fwd, u, 0.0), jnp.where(fwd, 0.0, u)], axis=1).astype(BF16)
    bu_ref[...] = jnp.dot(uu, wb_ref[...], preferred_element_type=F32)
    ar = ar_ref[...]
    ai = ai_ref[...]

    def step(t, carry):
        xr, xi = carry
        r0 = pl.multiple_of(t * SUBLANES, SUBLANES)
        rows = pl.ds(r0, SUBLANES)
        nxr = ar * xr - ai * xi + bu_ref[rows, 0:S5_NSTATE]
        nxi = ar * xi + ai * xr + bu_ref[rows, S5_NSTATE:2 * S5_NSTATE]
        bu_ref[rows, 0:S5_NSTATE] = nxr
        bu_ref[rows, S5_NSTATE:2 * S5_NSTATE] = nxi
        return nxr, nxi

    xr, xi = lax.fori_loop(0, tb, step, (xr_ref[...], xi_ref[...]))
    xr_ref[...] = xr
    xi_ref[...] = xi
    y2 = jnp.dot(bu_ref[...].astype(BF16), wc_ref[...], preferred_element_type=F32)
    y_ref[...] = jnp.where(fwd, y2[:, :S5_COLS], y2[:, S5_COLS:])


def s5_scan(u8, wb, wc, a_re, a_im, seq):
    tb = min(S5_TIME_BLOCK, seq)
    rows = tb * SUBLANES
    const = lambda i: (0, 0)
    return pl.pallas_call(
        functools.partial(_s5_body, tb=tb),
        grid=(seq // tb,),
        in_specs=[pl.BlockSpec((rows, S5_COLS), lambda i: (i, 0)),
                  pl.BlockSpec((2 * S5_COLS, 2 * S5_NSTATE), const),
                  pl.BlockSpec((2 * S5_NSTATE, 2 * S5_COLS), const),
                  pl.BlockSpec((SUBLANES, S5_NSTATE), const),
                  pl.BlockSpec((SUBLANES, S5_NSTATE), const)],
        out_specs=pl.BlockSpec((rows, S5_COLS), lambda i: (i, 0)),
        out_shape=_sds((seq * SUBLANES, S5_COLS)),
        scratch_shapes=[pltpu.VMEM((rows, 2 * S5_NSTATE), F32),
                        pltpu.VMEM((SUBLANES, S5_NSTATE), F32),
                        pltpu.VMEM((SUBLANES, S5_NSTATE), F32)],
        compiler_params=_cp("arbitrary"),
        name="s5_scan",
    )(u8, wb, wc, a_re, a_im)


def _s5_out_body(yf_ref, yb_ref, u_ref, d_ref, w_ref, b_ref, o_ref):
    y = jax.nn.gelu(yf_ref[...] + yb_ref[...] + u_ref[...] * d_ref[...])
    gate = jnp.dot(y.astype(BF16), w_ref[...], preferred_element_type=F32) + b_ref[...]
    o_ref[...] = y * jax.nn.sigmoid(gate)


def s5_out(yf, yb, u, d, w_glu, b_glu):
    t = u.shape[0]
    tm = min(ROW_TILE, t)
    row = pl.BlockSpec((tm, S5_COLS), lambda i: (i, 0))
    vec = pl.BlockSpec((1, S5_COLS), lambda i: (0, 0))
    return pl.pallas_call(
        _s5_out_body,
        grid=(t // tm,),
        in_specs=[row, row, row, vec, pl.BlockSpec((S5_COLS, S5_COLS), lambda i: (0, 0)), vec],
        out_specs=row,
        out_shape=_sds((t, S5_COLS)),
        compiler_params=_cp("parallel"),
        name="s5_out",
    )(yf, yb, u, d, w_glu, b_glu)


def s5_mixer(z_s5, tabs, d, w_glu, b_glu, batch, seq):
    assert batch == 4, "the scan packs 4 sequences x 2 directions onto the 8 sublanes"
    u = z_s5.reshape(batch, seq, S5_COLS)
    u8 = jnp.concatenate([u.transpose(1, 0, 2), u[:, ::-1].transpose(1, 0, 2)], axis=1)
    y8 = s5_scan(u8.reshape(seq * SUBLANES, S5_COLS), *tabs, seq).reshape(seq, SUBLANES, S5_COLS)
    yf = y8[:, :4].transpose(1, 0, 2).reshape(batch * seq, S5_COLS)
    yb = y8[::-1, 4:].transpose(1, 0, 2).reshape(batch * seq, S5_COLS)
    return s5_out(yf, yb, z_s5, d, w_glu, b_glu)


def _prep_body(z_ref, qg_ref, kg_ref, c_ref, s_ref, q_ref, k_ref, v_ref, *, nq, nk, nv, rope, k_src):
    tm = z_ref.shape[0]
    lane = lax.broadcasted_iota(I32, (tm, LANES), 1)
    low = lane < HEAD_DIM

    def norm_pair(x, g):
        sq = x * x
        ss_lo = jnp.sum(jnp.where(low, sq, 0.0), axis=-1, keepdims=True)
        ss_hi = jnp.sum(jnp.where(low, 0.0, sq), axis=-1, keepdims=True)
        y = x * lax.rsqrt(jnp.where(low, ss_lo, ss_hi) * (1.0 / HEAD_DIM) + NORM_EPS) * g
        if rope:
            half = HEAD_DIM // 2
            partner = jnp.where(lane % HEAD_DIM < half, pltpu.roll(y, LANES - half, 1), pltpu.roll(y, half, 1))
            y = y * c_ref[...] + partner * s_ref[...]
        return y

    qg = qg_ref[...]
    kg = kg_ref[...]
    for p in range(nq // 2):
        q_ref[0, p] = (norm_pair(z_ref[:, p * LANES:(p + 1) * LANES], qg) * (ATTN_SCALE * LOG2E)).astype(BF16)
    k_lo = nq * HEAD_DIM
    pairs = [norm_pair(z_ref[:, k_lo + p * LANES:k_lo + (p + 1) * LANES], kg) for p in range(nk // 2)]
    for h in range(nq):
        src = k_src(h)
        x = pairs[src // 2]
        if src % 2 != h % 2:
            x = pltpu.roll(x, HEAD_DIM, 1)
        k_ref[0, h] = jnp.where(low if h % 2 == 0 else jnp.logical_not(low), x, 0.0).astype(BF16)
    ones_col = jnp.where(lax.broadcasted_iota(I32, (tm, LANES - HEAD_DIM), 1) == 0, 1.0, 0.0)
    for h in range(nv):
        lo = (nq + nk + h) * HEAD_DIM
        v_ref[0, h] = jnp.concatenate([z_ref[:, lo:lo + HEAD_DIM], ones_col], axis=1).astype(BF16)


def attn_prep(z, qg, kg, cos_t, sin_t, batch, seq, nq, nk, nv, rope, k_src):
    width = (nq + nk + nv) * HEAD_DIM
    tm = min(ROW_TILE, seq)
    nb = seq // tm
    vec = pl.BlockSpec((1, LANES), lambda b, i: (0, 0))
    tab = pl.BlockSpec((tm, LANES), lambda b, i: (i, 0))
    heads = lambda n: pl.BlockSpec((1, n, tm, LANES), lambda b, i: (b, 0, i, 0))
    return pl.pallas_call(
        functools.partial(_prep_body, nq=nq, nk=nk, nv=nv, rope=rope, k_src=k_src),
        grid=(batch, nb),
        in_specs=[pl.BlockSpec((tm, width), lambda b, i: (b * nb + i, 0)), vec, vec, tab, tab],
        out_specs=[heads(nq // 2), heads(nq), heads(nv)],
        out_shape=[_sds((batch, nq // 2, seq, LANES), BF16), _sds((batch, nq, seq, LANES), BF16),
                   _sds((batch, nv, seq, LANES), BF16)],
        compiler_params=_cp("parallel", "parallel"),
        name="attn_prep",
    )(z, jnp.tile(qg, (1, 2)), jnp.tile(kg, (1, 2)), cos_t, sin_t)


def _flash_body(q_ref, k_ref, v_ref, *rest, unit, sub, has_bias):
    if has_bias:
        b_ref, o_ref, m_ref, acc_ref = rest
    else:
        o_ref, m_ref, acc_ref = rest
    i = pl.program_id(2)
    j = pl.program_id(3)
    tq = q_ref.shape[2]
    tk = k_ref.shape[2]

    @pl.when(j == 0)
    def _():
        m_ref[...] = jnp.full_like(m_ref, -jnp.inf)
        acc_ref[...] = jnp.zeros_like(acc_ref)

    k = k_ref[0, 0]
    v = v_ref[0, 0]
    for g in range(tq // sub):
        r0 = g * sub
        s = lax.dot_general(q_ref[0, 0, r0:r0 + sub, :], k, (((1,), (1,)), ((), ())),
                            preferred_element_type=F32)
        if has_bias:
            row_unit = i * (tq // unit) + r0 // unit
            tiles = []
            for c in range(tk // unit):
                sel = jnp.clip(j * (tk // unit) + c - row_unit, -2, 2) + 2
                tiles.append(b_ref[0, sel, r0 % unit:r0 % unit + sub, :])
            s = s + (tiles[0] if len(tiles) == 1 else jnp.concatenate(tiles, axis=1))
        blocks = [s[:, b * LANES:(b + 1) * LANES] for b in range(tk // LANES)]
        bmax = blocks[0]
        for blk in blocks[1:]:
            bmax = jnp.maximum(bmax, blk)
        m_prev = m_ref[r0:r0 + sub, :]
        m_new = jnp.maximum(m_prev, jnp.max(bmax, axis=-1, keepdims=True))
        alpha = jnp.exp2(m_prev - m_new)
        p = jnp.concatenate([jnp.exp2(blk - m_new) for blk in blocks], axis=1).astype(BF16)
        acc_ref[r0:r0 + sub, :] = alpha * acc_ref[r0:r0 + sub, :] + jnp.dot(p, v, preferred_element_type=F32)
        m_ref[r0:r0 + sub, :] = m_new

    @pl.when(j == pl.num_programs(3) - 1)
    def _():
        acc = acc_ref[...]
        o_ref[0, 0] = acc[:, :HEAD_DIM] / acc[:, HEAD_DIM:HEAD_DIM + 1]


def flash_attention(q, k, v, v_of_q, bias=None):
    batch, nq, seq, _ = k.shape
    tq = min(ATTN_Q_TILE, seq)
    tk = min(ATTN_K_TILE, seq)
    unit = min(ATTN_UNIT, seq)
    sub = min(ATTN_SUB, seq)
    in_specs = [pl.BlockSpec((1, 1, tq, LANES), lambda b, h, i, j: (b, h // 2, i, 0)),
                pl.BlockSpec((1, 1, tk, LANES), lambda b, h, i, j: (b, h, j, 0)),
                pl.BlockSpec((1, 1, tk, LANES), lambda b, h, i, j: (b, v_of_q(h), j, 0))]
    args = [q, k, v]
    if bias is not None:
        assert bias.shape[2] == unit
        in_specs.append(pl.BlockSpec((1, 5, unit, unit), lambda b, h, i, j: (v_of_q(h), 0, 0, 0)))
        args.append(bias)
    return pl.pallas_call(
        functools.partial(_flash_body, unit=unit, sub=sub, has_bias=bias is not None),
        grid=(batch, nq, seq // tq, seq // tk),
        in_specs=in_specs,
        out_specs=pl.BlockSpec((1, 1, tq, HEAD_DIM), lambda b, h, i, j: (b, h, i, 0)),
        out_shape=_sds((batch, nq, seq, HEAD_DIM)),
        scratch_shapes=[pltpu.VMEM((tq, LANES), F32), pltpu.VMEM((tq, LANES), F32)],
        compiler_params=_cp("parallel", "parallel", "parallel", "arbitrary"),
        name="flash_bias" if bias is not None else "flash",
    )(*args)


def _t5_bucket_table(t):
    rel = np.arange(-(3 * t - 1), 3 * t)
    half = NUM_BUCKETS // 2
    max_exact = half // 2
    ret = np.where(rel > 0, half, 0)
    n = np.abs(rel)
    nf = np.maximum(n, 1).astype(np.float64)
    large = max_exact + (np.log(nf / max_exact) / math.log(MAX_DISTANCE / max_exact) * (half - max_exact)).astype(np.int64)
    large = np.minimum(large, half - 1)
    return ret + np.where(n < max_exact, n, large)


def diff_bias_tiles(rel_bias, t):
    assert t >= MAX_DISTANCE, "offsets of two or more tiles must lie in the saturated buckets"
    heads = rel_bias.shape[1]
    vec = rel_bias[jnp.asarray(_t5_bucket_table(t), dtype=I32)].T * LOG2E
    tiles = []
    for d in range(-2, 3):
        lo = d * t + 2 * t
        w = vec[:, lo:lo + 2 * t - 1]
        u = jnp.concatenate([w[:, t - 1:], jnp.zeros((heads, 1), F32), w[:, :t - 1]], axis=1)
        skew = jnp.tile(u, (1, t))[:, :t * (2 * t - 1)].reshape(heads, t, 2 * t - 1)
        tiles.append(skew[:, :, :t])
    return jnp.stack(tiles, axis=1)


def rope_tables(seq):
    rows = seq // GRID_W
    row = jnp.repeat(jnp.arange(rows, dtype=F32), GRID_W)
    col = jnp.tile(jnp.arange(GRID_W, dtype=F32), rows)
    freq = ROPE_THETA ** (-jnp.arange(0, ROPE_AXIS_DIM, 2, dtype=F32) / ROPE_AXIS_DIM)
    ang = jnp.concatenate([row[:, None] * freq, col[:, None] * freq], -1)
    cos, sin = jnp.cos(ang), jnp.sin(ang)
    return jnp.concatenate([cos, cos] * 2, -1), jnp.concatenate([-sin, sin] * 2, -1)


def _conv3_body(x0, x1, x2, w0, w1, w2, b0, b1, b2, o0, o1, o2):
    for x_ref, w_ref, b_ref, o_ref in ((x0, w0, b0, o0), (x1, w1, b1, o1), (x2, w2, b2, o2)):
        x = x_ref[0]
        n = x.shape[0]
        t = lax.broadcasted_iota(I32, (n, 1), 0)
        prev = jnp.where(t == 0, 0.0, pltpu.roll(x, 1, 0))
        nxt = jnp.where(t == n - 1, 0.0, pltpu.roll(x, n - 1, 0))
        o_ref[0] = w_ref[0:1] * prev + w_ref[1:2] * x + w_ref[2:3] * nxt + b_ref[...]


def hyena_conv3(z_hy, conv_w, conv_b, batch, seq):
    z = z_hy.reshape(batch, seq, HY_COLS)
    nc = HY_WIDTH // LANES
    x_spec = lambda g: pl.BlockSpec((1, seq, LANES), lambda b, c: (b, 0, g * nc + c))
    w_spec = lambda g: pl.BlockSpec((3, LANES), lambda b, c: (0, g * nc + c))
    b_spec = lambda g: pl.BlockSpec((1, LANES), lambda b, c: (0, g * nc + c))
    out = pl.BlockSpec((1, seq, LANES), lambda b, c: (b, 0, c))
    return pl.pallas_call(
        _conv3_body,
        grid=(batch, nc),
        in_specs=[x_spec(0), x_spec(1), x_spec(2), w_spec(0), w_spec(1), w_spec(2),
                  b_spec(0), b_spec(1), b_spec(2)],
        out_specs=[out, out, out],
        out_shape=[_sds((batch, seq, HY_WIDTH))] * 3,
        compiler_params=_cp("parallel", "parallel"),
        name="hyena_conv3",
    )(z, z, z, conv_w, conv_w, conv_w, conv_b, conv_b, conv_b)


def _filt_body(feat_ref, dist_ref, w1, b1, f1, w2, b2, f2, w3, dec, h_ref, s_ref):
    dot = functools.partial(jnp.dot, precision=HI, preferred_element_type=F32)
    hid = jnp.sin(f1[...] * (dot(feat_ref[...], w1[...]) + b1[...]))
    hid = jnp.sin(f2[...] * (dot(hid, w2[...]) + b2[...]))
    h = dot(hid, w3[...]) * jnp.exp(-dist_ref[...] * jnp.abs(dec[...]))
    h_ref[...] = h

    @pl.when(pl.program_id(0) == 0)
    def _():
        s_ref[...] = jnp.zeros_like(s_ref)

    s_ref[...] += jnp.sum(jnp.abs(h), axis=0, keepdims=True)


def hyena_filter(seq, w1, b1, f1, w2, b2, f2, w3, decay):
    j = jnp.arange(seq, dtype=F32)
    tt = j / seq
    ang = 2.0 * jnp.pi * tt[:, None] * jnp.arange(1, HY_BANDS + 1, dtype=F32)
    feat = jnp.concatenate([tt[:, None], jnp.cos(ang), jnp.sin(ang)], -1)
    feat = jnp.pad(feat, ((0, 0), (0, LANES - HY_EMB)))
    half = seq // 2
    dist = (jnp.abs(j - half) / half)[:, None]
    w1p = jnp.pad(w1, ((0, LANES - HY_EMB), (0, 0)))
    tm = min(1024, seq)
    hw = HY_ORDER * HY_WIDTH
    const = lambda i: (0, 0)
    return pl.pallas_call(
        _filt_body,
        grid=(seq // tm,),
        in_specs=[pl.BlockSpec((tm, LANES), lambda i: (i, 0)), pl.BlockSpec((tm, 1), lambda i: (i, 0)),
                  pl.BlockSpec((LANES, HY_FFN), const), pl.BlockSpec((1, HY_FFN), const),
                  pl.BlockSpec((1, HY_FFN), const), pl.BlockSpec((HY_FFN, HY_FFN), const),
                  pl.BlockSpec((1, HY_FFN), const), pl.BlockSpec((1, HY_FFN), const),
                  pl.BlockSpec((HY_FFN, hw), const), pl.BlockSpec((1, hw), const)],
        out_specs=[pl.BlockSpec((tm, hw), lambda i: (i, 0)), pl.BlockSpec((1, hw), const)],
        out_shape=[_sds((seq, hw)), _sds((1, hw))],
        compiler_params=_cp("arbitrary"),
        name="hyena_filter",
    )(feat, dist, w1p, b1[None], f1[None], w2, b2[None], f2[None], w3, decay.reshape(1, hw))


def dft_tables(seq):
    n = 2 * seq
    n2 = FFT_N2
    n1 = n // n2
    n1h = n1 // 2
    nk = -(-(n1h + 1) // SUBLANES) * SUBLANES
    k1 = jnp.arange(nk, dtype=I32)
    live = (k1 <= n1h).astype(F32)
    ang = (2.0 * jnp.pi / n1) * ((k1[:, None] * jnp.arange(n1h, dtype=I32)[None, :]) % n1).astype(F32)
    outer_fwd = jnp.concatenate([jnp.cos(ang), -jnp.sin(ang)], axis=0) * jnp.tile(live, 2)[:, None]
    t1 = jnp.arange(n1h, dtype=I32) + n1 // 4
    ang = (2.0 * jnp.pi / n1) * ((t1[:, None] * k1[None, :]) % n1).astype(F32)
    weight = live * jnp.where((k1 == 0) | (k1 == n1h), 1.0, 2.0) / n
    outer_inv = jnp.concatenate([jnp.cos(ang), -jnp.sin(ang)], axis=1) * jnp.tile(weight, 2)[None, :]
    k2 = jnp.arange(n2, dtype=I32)
    t2 = jnp.arange(n2, dtype=I32)
    phase = (t2[None, None, :] * (k2[None, :, None] * n1 + k1[:, None, None])) % n
    ang = (2.0 * jnp.pi / n) * phase.astype(F32)
    fr, fi = jnp.cos(ang), -jnp.sin(ang)
    inner = jnp.concatenate([jnp.concatenate([fr, -fi], axis=2),
                             jnp.concatenate([fi, fr], axis=2)], axis=1)
    return outer_fwd, outer_inv, _split_bf16(inner), _split_bf16(jnp.swapaxes(inner, 1, 2))


def _split_bf16(x):
    hi = x.astype(BF16)
    return hi, (x - hi.astype(F32)).astype(BF16)


def _dot_split(w, a):
    w_hi, w_lo = w
    a_hi, a_lo = _split_bf16(a)
    dot = functools.partial(jnp.dot, preferred_element_type=F32)
    return dot(w_hi, a_hi) + (dot(w_hi, a_lo) + dot(w_lo, a_hi))


def _every(ref, s, n):
    return ref[pl.ds(s, n, stride=FFT_ROWS), :]


def _fft_outer_body(f_ref, x_ref, o_ref):
    n1h = x_ref.shape[0]
    rows = f_ref.shape[0]
    x2 = x_ref.reshape(n1h * FFT_ROWS, LANES)
    o2 = o_ref.reshape(rows * FFT_ROWS, LANES)
    x = jnp.concatenate([_every(x2, s, n1h) for s in range(FFT_ROWS)], axis=1)
    r = jnp.dot(f_ref[...], x, precision=HI, preferred_element_type=F32)
    for s in range(FFT_ROWS):
        o2[pl.ds(s, rows, stride=FFT_ROWS), :] = r[:, s * LANES:(s + 1) * LANES]


def fft_outer(x4, table):
    batch, n1h, n2, ch = x4.shape
    nk = table.shape[0] // 2
    return pl.pallas_call(
        _fft_outer_body,
        grid=(batch, n2 // FFT_ROWS, ch // LANES),
        in_specs=[pl.BlockSpec((2 * nk, n1h), lambda b, c, l: (0, 0)),
                  pl.BlockSpec((None, n1h, FFT_ROWS, LANES), lambda b, c, l: (b, 0, c, l))],
        out_specs=pl.BlockSpec((None, 2, nk, FFT_ROWS, LANES), lambda b, c, l: (b, 0, 0, c, l)),
        out_shape=_sds((batch, 2, nk, n2, ch)),
        compiler_params=_cp("parallel", "parallel", "parallel"),
        name="fft_outer",
    )(table, x4)


def _fft_filter_body(mh_ref, ml_ref, s_ref, a_ref, o_ref):
    n2 = FFT_N2
    a = a_ref[0, :, 0].reshape(2 * n2, HY_WIDTH)
    x = _dot_split((mh_ref[0], ml_ref[0]), a) * (1.0 / s_ref[...])
    o_ref[:, 0] = x.reshape(2, n2, HY_WIDTH)


def fft_filter(a5, inner, s):
    _, _, n1, n2, ch = a5.shape
    return pl.pallas_call(
        _fft_filter_body,
        grid=(n1, ch // HY_WIDTH),
        in_specs=[pl.BlockSpec((1, 2 * n2, 2 * n2), lambda k, c: (k, 0, 0)),
                  pl.BlockSpec((1, 2 * n2, 2 * n2), lambda k, c: (k, 0, 0)),
                  pl.BlockSpec((1, HY_WIDTH), lambda k, c: (0, c)),
                  pl.BlockSpec((1, 2, 1, n2, HY_WIDTH), lambda k, c: (0, 0, k, 0, c))],
        out_specs=pl.BlockSpec((2, 1, n2, HY_WIDTH), lambda k, c: (0, k, 0, c)),
        out_shape=_sds((2, n1, n2, ch)),
        compiler_params=_cp("parallel", "parallel"),
        name="fft_filter",
    )(*inner, s, a5)


def _fft_mid_body(mh_ref, ml_ref, th_ref, tl_ref, h_ref, a_ref, o_ref, *, nb):
    n2 = FFT_N2
    m = (mh_ref[0], ml_ref[0])
    mt = (th_ref[0], tl_ref[0])
    hr = h_ref[0, 0]
    hi = h_ref[1, 0]
    for b in range(nb):
        x = _dot_split(m, a_ref[b, :, 0].reshape(2 * n2, HY_WIDTH))
        xr, xi = x[:n2], x[n2:]
        y = jnp.concatenate([xr * hr - xi * hi, xr * hi + xi * hr], axis=0)
        o_ref[b, :, 0] = _dot_split(mt, y).reshape(2, n2, HY_WIDTH)


def fft_mid(a5, inner, inner_t, hf, order):
    nb, _, n1, n2, ch = a5.shape
    mat = pl.BlockSpec((1, 2 * n2, 2 * n2), lambda k: (k, 0, 0))
    blk = pl.BlockSpec((nb, 2, 1, n2, ch), lambda k: (0, 0, k, 0, 0))
    return pl.pallas_call(
        functools.partial(_fft_mid_body, nb=nb),
        grid=(n1,),
        in_specs=[mat, mat, mat, mat, pl.BlockSpec((2, 1, n2, ch), lambda k: (0, k, 0, order)), blk],
        out_specs=blk,
        out_shape=_sds(a5.shape),
        compiler_params=_cp("parallel"),
        name="fft_mid",
    )(*inner, *inner_t, hf, a5)


def _fft_inv_body(g_ref, b_ref, gate_ref, y_ref, fb_ref, o_ref):
    n1h, rows = g_ref.shape
    b2 = b_ref.reshape(rows * FFT_ROWS, LANES)
    gate2 = gate_ref.reshape(n1h * FFT_ROWS, LANES)
    y2 = y_ref.reshape(n1h * FFT_ROWS, LANES)
    o2 = o_ref.reshape(n1h * FFT_ROWS, LANES)
    bm = jnp.concatenate([_every(b2, s, rows) for s in range(FFT_ROWS)], axis=1)
    c = jnp.dot(g_ref[...], bm, precision=HI, preferred_element_type=F32)
    fb = fb_ref[...]
    for s in range(FFT_ROWS):
        conv = c[:, s * LANES:(s + 1) * LANES]
        o2[pl.ds(s, n1h, stride=FFT_ROWS), :] = _every(gate2, s, n1h) * (conv + fb * _every(y2, s, n1h))


def fft_inv_gate(b5, table, gate4, y4, fbias):
    batch, _, nk, n2, ch = b5.shape
    n1h = table.shape[0]
    blk = pl.BlockSpec((None, n1h, FFT_ROWS, LANES), lambda b, c, l: (b, 0, c, l))
    return pl.pallas_call(
        _fft_inv_body,
        grid=(batch, n2 // FFT_ROWS, ch // LANES),
        in_specs=[pl.BlockSpec((n1h, 2 * nk), lambda b, c, l: (0, 0)),
                  pl.BlockSpec((None, 2, nk, FFT_ROWS, LANES), lambda b, c, l: (b, 0, 0, c, l)), blk, blk,
                  pl.BlockSpec((1, LANES), lambda b, c, l: (0, l))],
        out_specs=blk,
        out_shape=_sds((batch, n1h, n2, ch)),
        compiler_params=_cp("parallel", "parallel", "parallel"),
        name="fft_inv_gate",
    )(table, b5, gate4, y4, fbias.reshape(1, ch))


def hyena_spectrum(seq, tabs, w1, b1, f1, w2, b2, f2, w3, decay):
    outer_fwd, _, inner, _ = tabs
    n1h = outer_fwd.shape[1]
    hw = HY_ORDER * HY_WIDTH
    h, s = hyena_filter(seq, w1, b1, f1, w2, b2, f2, w3, decay)
    return fft_filter(fft_outer(h.reshape(1, n1h, FFT_N2, hw), outer_fwd), inner, s)


def hyena_mixer(z_hy, conv_w, conv_b, hf, f_bias, tabs, batch, seq):
    outer_fwd, outer_inv, inner, inner_t = tabs
    n1h = outer_fwd.shape[1]
    shape4 = (batch, n1h, FFT_N2, HY_WIDTH)
    v, g0, g1 = hyena_conv3(z_hy, conv_w, conv_b, batch, seq)
    y = v.reshape(shape4)
    for order, gate in enumerate((g0, g1)):
        bm = fft_mid(fft_outer(y, outer_fwd), inner, inner_t, hf, order)
        y = fft_inv_gate(bm, outer_inv, gate.reshape(shape4), y, f_bias[order])
    return y.reshape(batch * seq, HY_WIDTH)


def _mix_body(lam_ref, h_ref, s5_ref, gqa_ref, diff_ref, hy_ref, gs5, ggqa, gsub, ghy, w_ref, o_ref, *, keep):
    lam = lam_ref[0]
    a = _rms(s5_ref[...], gs5[...])
    b = _rms(jnp.concatenate([gqa_ref[0, h] for h in range(GQA_HEADS)], axis=1), ggqa[...])
    c = jnp.concatenate(
        [_rms(diff_ref[0, 2 * h] - lam * diff_ref[0, 2 * h + 1], gsub[...]) * keep for h in range(DIFF_HEADS)],
        axis=1)
    d = _rms(hy_ref[...], ghy[...])
    mixed = jnp.concatenate([a, b, c, d], axis=1).astype(BF16)
    o_ref[...] = h_ref[...] + jnp.dot(mixed, w_ref[...], preferred_element_type=F32)


def mix_out(lam, h, y_s5, o_gqa, o_diff, y_hy, gs5, ggqa, gsub, ghy, w_out, batch, seq, lambda_init):
    tm = min(ROW_TILE, seq)
    nb = seq // tm
    row = lambda w: pl.BlockSpec((tm, w), lambda b, i: (b * nb + i, 0))
    vec = lambda w: pl.BlockSpec((1, w), lambda b, i: (0, 0))
    heads = lambda n: pl.BlockSpec((1, n, tm, HEAD_DIM), lambda b, i: (b, 0, i, 0))
    return pl.pallas_call(
        functools.partial(_mix_body, keep=1.0 - lambda_init),
        grid=(batch, nb),
        in_specs=[pl.BlockSpec(memory_space=pltpu.SMEM), row(D_MODEL), row(GROUP_WIDTH), heads(GQA_HEADS),
                  heads(2 * DIFF_HEADS), row(GROUP_WIDTH), vec(GROUP_WIDTH), vec(GROUP_WIDTH), vec(HEAD_DIM),
                  vec(GROUP_WIDTH), pl.BlockSpec((D_MODEL, D_MODEL), lambda b, i: (0, 0))],
        out_specs=row(D_MODEL),
        out_shape=_sds((batch * seq, D_MODEL)),
        compiler_params=_cp("parallel", "parallel"),
        name="mix_out",
    )(lam, h, y_s5, o_gqa, o_diff, y_hy, gs5, ggqa, gsub, ghy, w_out)


def _router_body(h_ref, g_ref, w_ref, b_ref, xn_ref, ti_ref, tw_ref, cnt_ref):
    xn = _rms(h_ref[...], g_ref[...])
    xn_ref[...] = xn
    logits = jnp.dot(xn, w_ref[...], precision=HI, preferred_element_type=F32) + b_ref[...]
    lane = lax.broadcasted_iota(I32, logits.shape, 1)
    vals = logits
    tops, idxs = [], []
    hot = jnp.zeros(logits.shape, F32)
    for _ in range(TOP_K):
        m = jnp.max(vals, axis=-1, keepdims=True)
        idx = jnp.min(jnp.where(vals == m, lane, N_EXPERTS), axis=-1, keepdims=True)
        sel = lane == idx
        tops.append(m)
        idxs.append(idx)
        hot = hot + sel.astype(F32)
        vals = jnp.where(sel, -jnp.inf, vals)
    es = [jnp.exp(t - tops[0]) for t in tops]
    den = es[0] + es[1] + es[2] + es[3]
    ti_ref[...] = jnp.concatenate(idxs, axis=1)
    tw_ref[...] = jnp.concatenate([e / den for e in es], axis=1)

    @pl.when(pl.program_id(0) == 0)
    def _():
        cnt_ref[...] = jnp.zeros_like(cnt_ref)

    cnt_ref[...] += jnp.sum(hot, axis=0, keepdims=True)


def moe_router(h, g, w_router, b_router):
    t = h.shape[0]
    tm = min(ROW_TILE, t)
    const = lambda i: (0, 0)
    return pl.pallas_call(
        _router_body,
        grid=(t // tm,),
        in_specs=[pl.BlockSpec((tm, D_MODEL), lambda i: (i, 0)), pl.BlockSpec((1, D_MODEL), const),
                  pl.BlockSpec((D_MODEL, N_EXPERTS), const), pl.BlockSpec((1, N_EXPERTS), const)],
        out_specs=[pl.BlockSpec((tm, D_MODEL), lambda i: (i, 0)), pl.BlockSpec((tm, TOP_K), lambda i: (i, 0)),
                   pl.BlockSpec((tm, TOP_K), lambda i: (i, 0)), pl.BlockSpec((1, N_EXPERTS), const)],
        out_shape=[_sds((t, D_MODEL)), _sds((t, TOP_K), I32), _sds((t, TOP_K)), _sds((1, N_EXPERTS))],
        compiler_params=_cp("arbitrary"),
        name="moe_router",
    )(h, g, w_router, b_router)


def _rank_body(ti_ref, off_ref, pos_ref, carry_ref):
    @pl.when(pl.program_id(0) == 0)
    def _():
        carry_ref[...] = jnp.zeros_like(carry_ref)

    ti = ti_ref[...]
    tm = ti.shape[0]
    lane = lax.broadcasted_iota(I32, (tm, N_EXPERTS), 1)
    hots = [lane == ti[:, k:k + 1] for k in range(TOP_K)]
    hot = sum(h.astype(F32) for h in hots)
    r = lax.broadcasted_iota(I32, (tm, tm), 0)
    c = lax.broadcasted_iota(I32, (tm, tm), 1)
    below = jnp.where(r > c, 1.0, 0.0).astype(BF16)
    before = jnp.dot(below, hot.astype(BF16), preferred_element_type=F32) + carry_ref[...] + off_ref[...]
    pos = [jnp.sum(jnp.where(h, before, 0.0), axis=-1, keepdims=True) for h in hots]
    pos_ref[...] = jnp.concatenate(pos, axis=1).astype(I32)
    carry_ref[...] += jnp.sum(hot, axis=0, keepdims=True)


def moe_rank(top_i, offsets):
    t = top_i.shape[0]
    tm = min(ROW_TILE, t)
    return pl.pallas_call(
        _rank_body,
        grid=(t // tm,),
        in_specs=[pl.BlockSpec((tm, TOP_K), lambda i: (i, 0)), pl.BlockSpec((1, N_EXPERTS), lambda i: (0, 0))],
        out_specs=pl.BlockSpec((tm, TOP_K), lambda i: (i, 0)),
        out_shape=_sds((t, TOP_K), I32),
        scratch_shapes=[pltpu.VMEM((1, N_EXPERTS), F32)],
        compiler_params=_cp("arbitrary"),
        name="moe_rank",
    )(top_i, offsets)


def _row_copy(src, dst, sem):
    return pltpu.make_async_copy(src, dst, sem)


def _slot_rows(ref, slot):
    return ref.at[pl.ds(pl.multiple_of(slot * ROW_TILES, ROW_TILES), ROW_TILES)]


def _scatter_body(pos_ref, x_ref, zero_hbm, xs_hbm, buf, sem, *, tm):
    del zero_hbm
    x = x_ref[...]
    for s in range(ROW_TILES):
        buf[pl.ds(s, tm, stride=ROW_TILES), :] = x[:, s * LANES:(s + 1) * LANES]

    def start(r, c):
        for k in range(TOP_K):
            _row_copy(_slot_rows(buf, r), _slot_rows(xs_hbm, pos_ref[r * TOP_K + k]), sem).start()
        return c

    lax.fori_loop(0, tm, start, 0)
    for _ in range(TOP_K):
        _row_copy(buf, xs_hbm.at[pl.ds(0, tm * ROW_TILES)], sem).wait()


def moe_scatter(pos_flat, x, n_slots):
    t = x.shape[0]
    tm = min(MOE_TOK_TILE, t)
    zeros = jnp.zeros((n_slots * ROW_TILES, LANES), F32)
    return pl.pallas_call(
        functools.partial(_scatter_body, tm=tm),
        grid=(t // tm,),
        in_specs=[pl.BlockSpec((tm * TOP_K,), lambda i: (i,), memory_space=pltpu.SMEM),
                  pl.BlockSpec((tm, D_MODEL), lambda i: (i, 0)), pl.BlockSpec(memory_space=pl.ANY)],
        out_specs=pl.BlockSpec(memory_space=pl.ANY),
        out_shape=_sds((n_slots * ROW_TILES, LANES)),
        scratch_shapes=[pltpu.VMEM((tm * ROW_TILES, LANES), F32), pltpu.SemaphoreType.DMA(())],
        input_output_aliases={2: 0},
        compiler_params=_cp("arbitrary"),
        name="moe_scatter",
    )(pos_flat, x, zeros)


def _experts_body(te_ref, nv_ref, x_ref, wgu_ref, bgu_ref, wd_ref, bd_ref, o_ref):
    del te_ref
    j = pl.program_id(0)
    dff = wd_ref.shape[1]

    @pl.when(j < nv_ref[0])
    def _():
        tm = x_ref.shape[0] // ROW_TILES
        x = jnp.concatenate([x_ref[pl.ds(s, tm, stride=ROW_TILES), :] for s in range(ROW_TILES)],
                            axis=1).astype(BF16)
        hgu = jnp.dot(x, wgu_ref[0], preferred_element_type=F32) + bgu_ref[0]
        glu = jnp.minimum(hgu[:, :dff], SWIGLU_LIMIT)
        lin = jnp.clip(hgu[:, dff:], -SWIGLU_LIMIT, SWIGLU_LIMIT)
        act = glu * jax.nn.sigmoid(SWIGLU_ALPHA * glu) * (lin + 1.0)
        y = jnp.dot(act.astype(BF16), wd_ref[0], preferred_element_type=F32) + bd_ref[0]
        for s in range(ROW_TILES):
            o_ref[pl.ds(s, tm, stride=ROW_TILES), :] = y[:, s * LANES:(s + 1) * LANES]

    @pl.when(j >= nv_ref[0])
    def _():
        o_ref[...] = jnp.zeros_like(o_ref)


def moe_experts(tile_expert, n_valid, xs, w_gu, b_gu, w_down, b_down):
    n_slots = xs.shape[0] // ROW_TILES
    tm = MOE_TILE
    dff = w_down.shape[1]
    blk = pl.BlockSpec((tm * ROW_TILES, LANES), lambda j, te, nv: (j, 0))
    return pl.pallas_call(
        _experts_body,
        grid_spec=pltpu.PrefetchScalarGridSpec(
            num_scalar_prefetch=2,
            grid=(n_slots // tm,),
            in_specs=[blk,
                      pl.BlockSpec((1, D_MODEL, 2 * dff), lambda j, te, nv: (te[j], 0, 0)),
                      pl.BlockSpec((1, 1, 2 * dff), lambda j, te, nv: (te[j], 0, 0)),
                      pl.BlockSpec((1, dff, D_MODEL), lambda j, te, nv: (te[j], 0, 0)),
                      pl.BlockSpec((1, 1, D_MODEL), lambda j, te, nv: (te[j], 0, 0))],
            out_specs=blk),
        out_shape=_sds((n_slots * ROW_TILES, LANES)),
        compiler_params=_cp("arbitrary"),
        name="moe_experts",
    )(tile_expert, n_valid, xs, w_gu, b_gu, w_down, b_down)


def _combine_body(pos_ref, w_ref, ys_hbm, o_ref, b0, b1, b2, b3, mixed, sem, *, tm):
    bufs = (b0, b1, b2, b3)

    def start(r, c):
        for k in range(TOP_K):
            _row_copy(_slot_rows(ys_hbm, pos_ref[r * TOP_K + k]), _slot_rows(bufs[k], r), sem).start()
        return c

    lax.fori_loop(0, tm, start, 0)
    for k in range(TOP_K):
        _row_copy(ys_hbm.at[pl.ds(0, tm * ROW_TILES)], bufs[k], sem).wait()

    def mix(q, c):
        for u in range(MIX_UNROLL):
            r = q * MIX_UNROLL + u
            rows = pl.ds(pl.multiple_of(r * ROW_TILES, ROW_TILES), ROW_TILES)
            acc = w_ref[r * TOP_K] * b0[rows, :]
            for k in range(1, TOP_K):
                acc = acc + w_ref[r * TOP_K + k] * bufs[k][rows, :]
            mixed[rows, :] = acc
        return c

    lax.fori_loop(0, tm // MIX_UNROLL, mix, 0)
    o_ref[...] = jnp.concatenate([mixed[pl.ds(s, tm, stride=ROW_TILES), :] for s in range(ROW_TILES)], axis=1)


def moe_combine(pos_flat, w_flat, ys, t):
    tm = min(MOE_TOK_TILE, t)
    smem = pl.BlockSpec((tm * TOP_K,), lambda i: (i,), memory_space=pltpu.SMEM)
    tile_buf = pltpu.VMEM((tm * ROW_TILES, LANES), F32)
    return pl.pallas_call(
        functools.partial(_combine_body, tm=tm),
        grid=(t // tm,),
        in_specs=[smem, smem, pl.BlockSpec(memory_space=pl.ANY)],
        out_specs=pl.BlockSpec((tm, D_MODEL), lambda i: (i, 0)),
        out_shape=_sds((t, D_MODEL)),
        scratch_shapes=[tile_buf] * (TOP_K + 1) + [pltpu.SemaphoreType.DMA(())],
        compiler_params=_cp("arbitrary"),
        name="moe_combine",
    )(pos_flat, w_flat, ys)


def moe(h, g, w_router, b_router, w_gu, b_gu, w_down, b_down):
    t = h.shape[0]
    xn, top_i, top_w, counts = moe_router(h, g, w_router, b_router)
    n_tiles = (t * TOP_K) // MOE_TILE + N_EXPERTS
    padded = jnp.ceil(counts[0] / MOE_TILE) * MOE_TILE
    ends = jnp.cumsum(padded)
    offsets = (ends - padded)[None]
    tile_start = jnp.arange(n_tiles, dtype=F32) * MOE_TILE
    tile_expert = jnp.minimum(jnp.sum(ends[None, :] <= tile_start[:, None], axis=1), N_EXPERTS - 1).astype(I32)
    n_valid = (ends[-1:] / MOE_TILE).astype(I32)
    pos = moe_rank(top_i, offsets).reshape(t * TOP_K)
    xs = moe_scatter(pos, xn, n_tiles * MOE_TILE)
    ys = moe_experts(tile_expert, n_valid, xs, w_gu, b_gu, w_down, b_down)
    return moe_combine(pos, top_w.reshape(t * TOP_K), ys, t)


def _ple_body(h_ref, m_ref, p_ref, wp_ref, wg_ref, o_ref):
    h = h_ref[...] + m_ref[...]
    e = jnp.dot(p_ref[...].astype(BF16), wp_ref[...], preferred_element_type=F32)
    gate = jnp.dot(_rms(h).astype(BF16), wg_ref[...], preferred_element_type=F32)
    o_ref[...] = h + e * jax.nn.sigmoid(gate)


def ple(h, moe_out, p, w_ple, w_gate):
    t = h.shape[0]
    tm = min(ROW_TILE, t)
    row = pl.BlockSpec((tm, D_MODEL), lambda i: (i, 0))
    return pl.pallas_call(
        _ple_body,
        grid=(t // tm,),
        in_specs=[row, row, pl.BlockSpec((tm, PLE_DIM), lambda i: (i, 0)),
                  pl.BlockSpec((PLE_DIM, D_MODEL), lambda i: (0, 0)),
                  pl.BlockSpec((D_MODEL, D_MODEL), lambda i: (0, 0))],
        out_specs=row,
        out_shape=_sds((t, D_MODEL)),
        compiler_params=_cp("parallel"),
        name="ple",
    )(h, moe_out, p, w_ple, w_gate)


def _deinterleave(n):
    return np.concatenate([np.arange(0, n, 2), np.arange(1, n, 2)])


def _halves(w):
    pairs = w.reshape(*w.shape[:-1], w.shape[-1] // 2, 2)
    return jnp.concatenate([pairs[..., 0], pairs[..., 1]], axis=-1)


def prepare_layer(i, prm):
    pair = _deinterleave(HEAD_DIM)
    cols = np.arange(IN_COLS)
    for h in range(GQA_HEADS + GQA_KV_HEADS):
        lo = OFF_GQA + h * HEAD_DIM
        cols[lo:lo + HEAD_DIM] = lo + pair
    lam = (jnp.exp(jnp.sum(prm["diff_lam_q1"][i] * prm["diff_lam_k1"][i]))
           - jnp.exp(jnp.sum(prm["diff_lam_q2"][i] * prm["diff_lam_k2"][i])) + _lambda_init(i))
    return dict(
        norm1_g=prm["norm1_g"][i][None],
        w_in=prm["w_in"][i][:, cols].astype(BF16),
        s5=s5_tables(prm["s5_lam_re"][i], prm["s5_lam_im"][i], prm["s5_log_dt"][i], prm["s5_b_re"][i],
                     prm["s5_b_im"][i], prm["s5_c_re"][i], prm["s5_c_im"][i]),
        s5_d=prm["s5_d"][i][None],
        s5_w_glu=prm["s5_w_glu"][i].astype(BF16),
        s5_b_glu=prm["s5_b_glu"][i][None],
        gqa_q_g=prm["gqa_q_g"][i][pair][None],
        gqa_k_g=prm["gqa_k_g"][i][pair][None],
        diff_q_g=prm["diff_q_g"][i][None],
        diff_k_g=prm["diff_k_g"][i][None],
        diff_lam=lam.reshape(1).astype(F32),
        diff_subln_g=prm["diff_subln_g"][i][None],
        out_g_s5=prm["out_g_s5"][i][None],
        out_g_gqa=prm["out_g_gqa"][i][None],
        out_g_hy=prm["out_g_hy"][i][None],
        w_out=prm["w_out"][i].astype(BF16),
        norm2_g=prm["norm2_g"][i][None],
        w_router=prm["w_router"][i],
        b_router=prm["b_router"][i][None],
        w_gu=_halves(prm["w_gu"][i]).astype(BF16),
        b_gu=_halves(prm["b_gu"][i])[:, None, :],
        w_down=prm["w_down"][i].astype(BF16),
        b_down=prm["b_down"][i][:, None, :],
        w_ple=prm["w_ple"][i].astype(BF16),
        w_ple_gate=prm["w_ple_gate"][i].astype(BF16),
    )


def _lambda_init(i):
    return 0.8 - 0.6 * math.exp(-0.3 * i)


def run_trunk(x, p, prm, layers):
    batch, seq, _ = x.shape
    t = batch * seq
    cos_t, sin_t = rope_tables(seq)
    tabs = dft_tables(seq)
    bias = diff_bias_tiles(prm["rel_bias"], min(ATTN_UNIT, seq))
    h = x.reshape(t, D_MODEL)
    for i, lw in enumerate(layers):
        z_s5, z_gqa, z_diff, z_hy = in_proj(h, lw["norm1_g"], lw["w_in"])
        y_s5 = s5_mixer(z_s5, lw["s5"], lw["s5_d"], lw["s5_w_glu"], lw["s5_b_glu"], batch, seq)
        rep = GQA_HEADS // GQA_KV_HEADS
        q, k, v = attn_prep(z_gqa, lw["gqa_q_g"], lw["gqa_k_g"], cos_t, sin_t, batch, seq,
                            GQA_HEADS, GQA_KV_HEADS, GQA_KV_HEADS, True, lambda hq: hq // rep)
        o_gqa = flash_attention(q, k, v, lambda hq: hq // rep)
        q, k, v = attn_prep(z_diff, lw["diff_q_g"], lw["diff_k_g"], cos_t, sin_t, batch, seq,
                            2 * DIFF_HEADS, 2 * DIFF_HEADS, DIFF_HEADS, False, lambda hq: hq)
        o_diff = flash_attention(q, k, v, lambda hq: hq // 2, bias)
        hf = hyena_spectrum(seq, tabs, prm["hy_w1"][i], prm["hy_b1"][i], prm["hy_freq1"][i], prm["hy_w2"][i],
                            prm["hy_b2"][i], prm["hy_freq2"][i], prm["hy_w3"][i], prm["hy_decay"][i])
        y_hy = hyena_mixer(z_hy, prm["hy_conv_w"][i], prm["hy_conv_b"][i][None], hf, prm["hy_bias"][i],
                           tabs, batch, seq)
        h = mix_out(lw["diff_lam"], h, y_s5, o_gqa, o_diff, y_hy, lw["out_g_s5"], lw["out_g_gqa"],
                    lw["diff_subln_g"], lw["out_g_hy"], lw["w_out"], batch, seq, _lambda_init(i))
        m = moe(h, lw["norm2_g"], lw["w_router"], lw["b_router"], lw["w_gu"], lw["b_gu"], lw["w_down"],
                lw["b_down"])
        h = ple(h, m, p[i].reshape(t, PLE_DIM), lw["w_ple"], lw["w_ple_gate"])
    return h.reshape(batch, seq, D_MODEL)


def kernel(x_prompt, x_sample, p_prompt, p_sample, rel_bias, norm1_g, w_in, s5_lam_re, s5_lam_im, s5_log_dt,
           s5_b_re, s5_b_im, s5_c_re, s5_c_im, s5_d, s5_w_glu, s5_b_glu, gqa_q_g, gqa_k_g, diff_q_g, diff_k_g,
           diff_lam_q1, diff_lam_k1, diff_lam_q2, diff_lam_k2, diff_subln_g, hy_conv_w, hy_conv_b, hy_w1, hy_b1,
           hy_freq1, hy_w2, hy_b2, hy_freq2, hy_w3, hy_decay, hy_bias, out_g_s5, out_g_gqa, out_g_hy, w_out,
           norm2_g, w_router, b_router, w_gu, b_gu, w_down, b_down, w_ple, w_ple_gate):
    prm = dict(rel_bias=rel_bias, norm1_g=norm1_g, w_in=w_in, s5_lam_re=s5_lam_re, s5_lam_im=s5_lam_im,
               s5_log_dt=s5_log_dt, s5_b_re=s5_b_re, s5_b_im=s5_b_im, s5_c_re=s5_c_re, s5_c_im=s5_c_im,
               s5_d=s5_d, s5_w_glu=s5_w_glu, s5_b_glu=s5_b_glu, gqa_q_g=gqa_q_g, gqa_k_g=gqa_k_g,
               diff_q_g=diff_q_g, diff_k_g=diff_k_g, diff_lam_q1=diff_lam_q1, diff_lam_k1=diff_lam_k1,
               diff_lam_q2=diff_lam_q2, diff_lam_k2=diff_lam_k2, diff_subln_g=diff_subln_g,
               hy_conv_w=hy_conv_w, hy_conv_b=hy_conv_b, hy_w1=hy_w1, hy_b1=hy_b1, hy_freq1=hy_freq1,
               hy_w2=hy_w2, hy_b2=hy_b2, hy_freq2=hy_freq2, hy_w3=hy_w3, hy_decay=hy_decay, hy_bias=hy_bias,
               out_g_s5=out_g_s5, out_g_gqa=out_g_gqa, out_g_hy=out_g_hy, w_out=w_out, norm2_g=norm2_g,
               w_router=w_router, b_router=b_router, w_gu=w_gu, b_gu=b_gu, w_down=w_down, b_down=b_down,
               w_ple=w_ple, w_ple_gate=w_ple_gate)
    layers = [prepare_layer(i, prm) for i in range(DEPTH)]
    return (run_trunk(x_prompt, p_prompt, prm, layers), run_trunk(x_sample, p_sample, prm, layers))
```

```python
import functools
import math

import numpy as np
import jax
import jax.numpy as jnp
from jax import lax
from jax.experimental import pallas as pl
from jax.experimental.pallas import tpu as pltpu

F32 = jnp.float32
BF16 = jnp.bfloat16
I32 = jnp.int32
HI = lax.Precision.HIGHEST

D_MODEL = 1024
DEPTH = 2
GROUP_WIDTH = 256
HEAD_DIM = 64
S5_GROUP_DIM = 16
S5_GROUPS = 16
S5_STATE = 64
S5_NSTATE = S5_GROUPS * S5_STATE
GQA_HEADS = 4
GQA_KV_HEADS = 2
DIFF_HEADS = 4
HY_WIDTH = 256
HY_ORDER = 2
HY_BANDS = 16
HY_EMB = 2 * HY_BANDS + 1
HY_FFN = 64
N_EXPERTS = 32
TOP_K = 4
SWIGLU_LIMIT = 7.0
SWIGLU_ALPHA = 1.702
NUM_BUCKETS = 32
MAX_DISTANCE = 128
GRID_W = 64
ROPE_THETA = 10000.0
ROPE_AXIS_DIM = HEAD_DIM // 2
PLE_DIM = 256
NORM_EPS = 1e-6
ATTN_SCALE = HEAD_DIM ** -0.5
LOG2E = math.log2(math.e)

S5_COLS = GROUP_WIDTH
GQA_QW = GQA_HEADS * HEAD_DIM
GQA_KW = GQA_KV_HEADS * HEAD_DIM
GQA_COLS = GQA_QW + 2 * GQA_KW
DIFF_QW = DIFF_HEADS * 2 * HEAD_DIM
DIFF_COLS = 2 * DIFF_QW + DIFF_HEADS * HEAD_DIM
HY_COLS = (HY_ORDER + 1) * HY_WIDTH
OFF_GQA = S5_COLS
OFF_DIFF = OFF_GQA + GQA_COLS
OFF_HY = OFF_DIFF + DIFF_COLS
IN_COLS = OFF_HY + HY_COLS
IN_SPLITS = ((0, OFF_GQA), (OFF_GQA, OFF_DIFF), (OFF_DIFF, OFF_HY), (OFF_HY, IN_COLS))

LANES = 128
SUBLANES = 8
ROW_TILES = D_MODEL // LANES
FFT_N2 = 128
FFT_ROWS = 32
VMEM_LIMIT = 48 * 1024 * 1024

ROW_TILE = 256
ATTN_Q_TILE = 2048
ATTN_K_TILE = 2048
ATTN_UNIT = 512
ATTN_SUB = 512
ATTN_SUB_PLAIN = 1024
S5_TIME_BLOCK = 64
MOE_TILE = 512
MOE_TOK_TILE = 256
MIX_UNROLL = 4


def _cp(*sem):
    return pltpu.CompilerParams(dimension_semantics=sem, vmem_limit_bytes=VMEM_LIMIT)


def _sds(shape, dtype=F32):
    return jax.ShapeDtypeStruct(shape, dtype)


def _rms(x, g=None):
    y = x * lax.rsqrt(jnp.mean(x * x, axis=-1, keepdims=True) + NORM_EPS)
    return y if g is None else y * g


def _inproj_body(x_ref, g_ref, w_ref, *o_refs):
    u = _rms(x_ref[...], g_ref[...]).astype(BF16)
    for o_ref, (lo, hi) in zip(o_refs, IN_SPLITS):
        o_ref[...] = jnp.dot(u, w_ref[:, lo:hi], preferred_element_type=F32)


def in_proj(h, g, w):
    t = h.shape[0]
    tm = min(ROW_TILE, t)
    return pl.pallas_call(
        _inproj_body,
        grid=(t // tm,),
        in_specs=[pl.BlockSpec((tm, D_MODEL), lambda i: (i, 0)),
                  pl.BlockSpec((1, D_MODEL), lambda i: (0, 0)),
                  pl.BlockSpec((D_MODEL, IN_COLS), lambda i: (0, 0))],
        out_specs=[pl.BlockSpec((tm, hi - lo), lambda i: (i, 0)) for lo, hi in IN_SPLITS],
        out_shape=[_sds((t, hi - lo)) for lo, hi in IN_SPLITS],
        compiler_params=_cp("parallel"),
        name="in_proj",
    )(h, g, w)


def s5_tables(lam_re, lam_im, log_dt, b_re, b_im, c_re, c_im):
    lr = jnp.minimum(lam_re, -1e-4)
    li = lam_im
    dt = jnp.exp(log_dt)[..., None]
    mag = jnp.exp(lr * dt)
    ar = mag * jnp.cos(li * dt)
    ai = mag * jnp.sin(li * dt)
    den = lr * lr + li * li
    cr = ((ar - 1.0) * lr + ai * li) / den
    ci = (ai * lr - (ar - 1.0) * li) / den
    bbr = cr[..., None] * b_re - ci[..., None] * b_im
    bbi = cr[..., None] * b_im + ci[..., None] * b_re
    eye = jnp.eye(S5_GROUPS, dtype=F32)
    bb = jnp.stack([bbr, bbi], axis=1)
    wb = jnp.einsum("dpgnh,gk->dghpkn", bb, eye).reshape(2 * S5_COLS, 2 * S5_NSTATE)
    cc = jnp.stack([c_re, -c_im], axis=0)
    wc = jnp.einsum("pdghn,gk->pgndkh", cc, eye).reshape(2 * S5_NSTATE, 2 * S5_COLS)
    a_re = jnp.repeat(ar.reshape(2, S5_NSTATE), 4, axis=0)
    a_im = jnp.repeat(ai.reshape(2, S5_NSTATE), 4, axis=0)
    return wb.astype(BF16), wc.astype(BF16), a_re, a_im


def _s5_body(u_ref, wb_ref, wc_ref, ar_ref, ai_ref, y_ref, bu_ref, xr_ref, xi_ref, *, tb):
    @pl.when(pl.program_id(0) == 0)
    def _():
        xr_ref[...] = jnp.zeros_like(xr_ref)
        xi_ref[...] = jnp.zeros_like(xi_ref)

    u = u_ref[...]
    fwd = (lax.broadcasted_iota(I32, (tb * SUBLANES, 1), 0) % SUBLANES) < 4
    uu = jnp.concatenate([jnp.where(fwd, u, 0.0), jnp.where(fwd, 0.0, u)], axis=1).astype(BF16)
    bu_ref[...] = jnp.dot(uu, wb_ref[...], preferred_element_type=F32)
    ar = ar_ref[...]
    ai = ai_ref[...]

    def step(t, carry):
        xr, xi = carry
        r0 = pl.multiple_of(t * SUBLANES, SUBLANES)
        rows = pl.ds(r0, SUBLANES)
        nxr = ar * xr - ai * xi + bu_ref[rows, 0:S5_NSTATE]
        nxi = ar * xi + ai * xr + bu_ref[rows, S5_NSTATE:2 * S5_NSTATE]
        bu_ref[rows, 0:S5_NSTATE] = nxr
        bu_ref[rows, S5_NSTATE:2 * S5_NSTATE] = nxi
        return nxr, nxi

    xr, xi = lax.fori_loop(0, tb, step, (xr_ref[...], xi_ref[...]))
    xr_ref[...] = xr
    xi_ref[...] = xi
    y2 = jnp.dot(bu_ref[...].astype(BF16), wc_ref[...], preferred_element_type=F32)
    y_ref[...] = jnp.where(fwd, y2[:, :S5_COLS], y2[:, S5_COLS:])


def s5_scan(u8, wb, wc, a_re, a_im, seq):
    tb = min(S5_TIME_BLOCK, seq)
    rows = tb * SUBLANES
    const = lambda i: (0, 0)
    return pl.pallas_call(
        functools.partial(_s5_body, tb=tb),
        grid=(seq // tb,),
        in_specs=[pl.BlockSpec((rows, S5_COLS), lambda i: (i, 0)),
                  pl.BlockSpec((2 * S5_COLS, 2 * S5_NSTATE), const),
                  pl.BlockSpec((2 * S5_NSTATE, 2 * S5_COLS), const),
                  pl.BlockSpec((SUBLANES, S5_NSTATE), const),
                  pl.BlockSpec((SUBLANES, S5_NSTATE), const)],
        out_specs=pl.BlockSpec((rows, S5_COLS), lambda i: (i, 0)),
        out_shape=_sds((seq * SUBLANES, S5_COLS)),
        scratch_shapes=[pltpu.VMEM((rows, 2 * S5_NSTATE), F32),
                        pltpu.VMEM((SUBLANES, S5_NSTATE), F32),
                        pltpu.VMEM((SUBLANES, S5_NSTATE), F32)],
        compiler_params=_cp("arbitrary"),
        name="s5_scan",
    )(u8, wb, wc, a_re, a_im)


def _s5_out_body(yf_ref, yb_ref, u_ref, d_ref, w_ref, b_ref, o_ref):
    y = jax.nn.gelu(yf_ref[...] + yb_ref[...] + u_ref[...] * d_ref[...])
    gate = jnp.dot(y.astype(BF16), w_ref[...], preferred_element_type=F32) + b_ref[...]
    o_ref[...] = y * jax.nn.sigmoid(gate)


def s5_out(yf, yb, u, d, w_glu, b_glu):
    t = u.shape[0]
    tm = min(ROW_TILE, t)
    row = pl.BlockSpec((tm, S5_COLS), lambda i: (i, 0))
    vec = pl.BlockSpec((1, S5_COLS), lambda i: (0, 0))
    return pl.pallas_call(
        _s5_out_body,
        grid=(t // tm,),
        in_specs=[row, row, row, vec, pl.BlockSpec((S5_COLS, S5_COLS), lambda i: (0, 0)), vec],
        out_specs=row,
        out_shape=_sds((t, S5_COLS)),
        compiler_params=_cp("parallel"),
        name="s5_out",
    )(yf, yb, u, d, w_glu, b_glu)


def s5_mixer(z_s5, tabs, d, w_glu, b_glu, batch, seq):
    assert batch == 4, "the scan packs 4 sequences x 2 directions onto the 8 sublanes"
    u = z_s5.reshape(batch, seq, S5_COLS)
    u8 = jnp.concatenate([u.transpose(1, 0, 2), u[:, ::-1].transpose(1, 0, 2)], axis=1)
    y8 = s5_scan(u8.reshape(seq * SUBLANES, S5_COLS), *tabs, seq).reshape(seq, SUBLANES, S5_COLS)
    yf = y8[:, :4].transpose(1, 0, 2).reshape(batch * seq, S5_COLS)
    yb = y8[::-1, 4:].transpose(1, 0, 2).reshape(batch * seq, S5_COLS)
    return s5_out(yf, yb, z_s5, d, w_glu, b_glu)


def _prep_body(z_ref, qg_ref, kg_ref, c_ref, s_ref, q_ref, k_ref, v_ref, *, nq, nk, nv, rope, k_src):
    tm = z_ref.shape[0]
    lane = lax.broadcasted_iota(I32, (tm, LANES), 1)
    low = lane < HEAD_DIM

    def norm_pair(x, g):
        sq = x * x
        ss_lo = jnp.sum(jnp.where(low, sq, 0.0), axis=-1, keepdims=True)
        ss_hi = jnp.sum(jnp.where(low, 0.0, sq), axis=-1, keepdims=True)
        y = x * lax.rsqrt(jnp.where(low, ss_lo, ss_hi) * (1.0 / HEAD_DIM) + NORM_EPS) * g
        if rope:
            half = HEAD_DIM // 2
            partner = jnp.where(lane % HEAD_DIM < half, pltpu.roll(y, LANES - half, 1), pltpu.roll(y, half, 1))
            y = y * c_ref[...] + partner * s_ref[...]
        return y

    qg = qg_ref[...]
    kg = kg_ref[...]
    for p in range(nq // 2):
        q_ref[0, p] = (norm_pair(z_ref[:, p * LANES:(p + 1) * LANES], qg) * (ATTN_SCALE * LOG2E)).astype(BF16)
    k_lo = nq * HEAD_DIM
    pairs = [norm_pair(z_ref[:, k_lo + p * LANES:k_lo + (p + 1) * LANES], kg) for p in range(nk // 2)]
    for h in range(nq):
        src = k_src(h)
        x = pairs[src // 2]
        if src % 2 != h % 2:
            x = pltpu.roll(x, HEAD_DIM, 1)
        k_ref[0, h] = jnp.where(low if h % 2 == 0 else jnp.logical_not(low), x, 0.0).astype(BF16)
    ones_col = jnp.where(lax.broadcasted_iota(I32, (tm, LANES - HEAD_DIM), 1) == 0, 1.0, 0.0)
    for h in range(nv):
        lo = (nq + nk + h) * HEAD_DIM
        v_ref[0, h] = jnp.concatenate([z_ref[:, lo:lo + HEAD_DIM], ones_col], axis=1).astype(BF16)


def attn_prep(z, qg, kg, cos_t, sin_t, batch, seq, nq, nk, nv, rope, k_src):
    width = (nq + nk + nv) * HEAD_DIM
    tm = min(ROW_TILE, seq)
    nb = seq // tm
    vec = pl.BlockSpec((1, LANES), lambda b, i: (0, 0))
    tab = pl.BlockSpec((tm, LANES), lambda b, i: (i, 0))
    heads = lambda n: pl.BlockSpec((1, n, tm, LANES), lambda b, i: (b, 0, i, 0))
    return pl.pallas_call(
        functools.partial(_prep_body, nq=nq, nk=nk, nv=nv, rope=rope, k_src=k_src),
        grid=(batch, nb),
        in_specs=[pl.BlockSpec((tm, width), lambda b, i: (b * nb + i, 0)), vec, vec, tab, tab],
        out_specs=[heads(nq // 2), heads(nq), heads(nv)],
        out_shape=[_sds((batch, nq // 2, seq, LANES), BF16), _sds((batch, nq, seq, LANES), BF16),
                   _sds((batch, nv, seq, LANES), BF16)],
        compiler_params=_cp("parallel", "parallel"),
        name="attn_prep",
    )(z, jnp.tile(qg, (1, 2)), jnp.tile(kg, (1, 2)), cos_t, sin_t)


def _flash_body(q_ref, k_ref, v_ref, *rest, unit, sub, has_bias):
    if has_bias:
        b_ref, o_ref, m_ref, acc_ref = rest
    else:
        o_ref, m_ref, acc_ref = rest
    i = pl.program_id(2)
    j = pl.program_id(3)
    tq = q_ref.shape[2]
    tk = k_ref.shape[2]

    @pl.when(j == 0)
    def _():
        m_ref[...] = jnp.full_like(m_ref, -jnp.inf)
        acc_ref[...] = jnp.zeros_like(acc_ref)

    k = k_ref[0, 0]
    v = v_ref[0, 0]
    for g in range(tq // sub):
        r0 = g * sub
        s = lax.dot_general(q_ref[0, 0, r0:r0 + sub, :], k, (((1,), (1,)), ((), ())),
                            preferred_element_type=F32)
        if has_bias:
            row_unit = i * (tq // unit) + r0 // unit
            tiles = []
            for c in range(tk // unit):
                sel = jnp.clip(j * (tk // unit) + c - row_unit, -2, 2) + 2
                tiles.append(b_ref[0, sel, r0 % unit:r0 % unit + sub, :])
            s = s + (tiles[0] if len(tiles) == 1 else jnp.concatenate(tiles, axis=1))
        blocks = [s[:, b * LANES:(b + 1) * LANES] for b in range(tk // LANES)]
        bmax = blocks[0]
        for blk in blocks[1:]:
            bmax = jnp.maximum(bmax, blk)
        m_prev = m_ref[r0:r0 + sub, :]
        m_new = jnp.maximum(m_prev, jnp.max(bmax, axis=-1, keepdims=True))
        alpha = jnp.exp2(m_prev - m_new)
        p = jnp.concatenate([jnp.exp2(blk - m_new) for blk in blocks], axis=1).astype(BF16)
        acc_ref[r0:r0 + sub, :] = alpha * acc_ref[r0:r0 + sub, :] + jnp.dot(p, v, preferred_element_type=F32)
        m_ref[r0:r0 + sub, :] = m_new

    @pl.when(j == pl.num_programs(3) - 1)
    def _():
        acc = acc_ref[...]
        o_ref[0, 0] = acc[:, :HEAD_DIM] / acc[:, HEAD_DIM:HEAD_DIM + 1]


def flash_attention(q, k, v, v_of_q, bias=None):
    batch, nq, seq, _ = k.shape
    tq = min(ATTN_Q_TILE, seq)
    tk = min(ATTN_K_TILE, seq)
    unit = min(ATTN_UNIT, seq)
    sub = min(ATTN_SUB if bias is not None else ATTN_SUB_PLAIN, seq)
    in_specs = [pl.BlockSpec((1, 1, tq, LANES), lambda b, h, i, j: (b, h // 2, i, 0)),
                pl.BlockSpec((1, 1, tk, LANES), lambda b, h, i, j: (b, h, j, 0)),
                pl.BlockSpec((1, 1, tk, LANES), lambda b, h, i, j: (b, v_of_q(h), j, 0))]
    args = [q, k, v]
    if bias is not None:
        assert bias.shape[2] == unit
        in_specs.append(pl.BlockSpec((1, 5, unit, unit), lambda b, h, i, j: (v_of_q(h), 0, 0, 0)))
        args.append(bias)
    return pl.pallas_call(
        functools.partial(_flash_body, unit=unit, sub=sub, has_bias=bias is not None),
        grid=(batch, nq, seq // tq, seq // tk),
        in_specs=in_specs,
        out_specs=pl.BlockSpec((1, 1, tq, HEAD_DIM), lambda b, h, i, j: (b, h, i, 0)),
        out_shape=_sds((batch, nq, seq, HEAD_DIM)),
        scratch_shapes=[pltpu.VMEM((tq, LANES), F32), pltpu.VMEM((tq, LANES), F32)],
        compiler_params=_cp("parallel", "parallel", "parallel", "arbitrary"),
        name="flash_bias" if bias is not None else "flash",
    )(*args)


def _t5_bucket_table(t):
    rel = np.arange(-(3 * t - 1), 3 * t)
    half = NUM_BUCKETS // 2
    max_exact = half // 2
    ret = np.where(rel > 0, half, 0)
    n = np.abs(rel)
    nf = np.maximum(n, 1).astype(np.float64)
    large = max_exact + (np.log(nf / max_exact) / math.log(MAX_DISTANCE / max_exact) * (half - max_exact)).astype(np.int64)
    large = np.minimum(large, half - 1)
    return ret + np.where(n < max_exact, n, large)


def diff_bias_tiles(rel_bias, t):
    assert t >= MAX_DISTANCE, "offsets of two or more tiles must lie in the saturated buckets"
    heads = rel_bias.shape[1]
    vec = rel_bias[jnp.asarray(_t5_bucket_table(t), dtype=I32)].T * LOG2E
    tiles = []
    for d in range(-2, 3):
        lo = d * t + 2 * t
        w = vec[:, lo:lo + 2 * t - 1]
        u = jnp.concatenate([w[:, t - 1:], jnp.zeros((heads, 1), F32), w[:, :t - 1]], axis=1)
        skew = jnp.tile(u, (1, t))[:, :t * (2 * t - 1)].reshape(heads, t, 2 * t - 1)
        tiles.append(skew[:, :, :t])
    return jnp.stack(tiles, axis=1)


def rope_tables(seq):
    rows = seq // GRID_W
    row = jnp.repeat(jnp.arange(rows, dtype=F32), GRID_W)
    col = jnp.tile(jnp.arange(GRID_W, dtype=F32), rows)
    freq = ROPE_THETA ** (-jnp.arange(0, ROPE_AXIS_DIM, 2, dtype=F32) / ROPE_AXIS_DIM)
    ang = jnp.concatenate([row[:, None] * freq, col[:, None] * freq], -1)
    cos, sin = jnp.cos(ang), jnp.sin(ang)
    return jnp.concatenate([cos, cos] * 2, -1), jnp.concatenate([-sin, sin] * 2, -1)


def _conv3_body(x0, x1, x2, w0, w1, w2, b0, b1, b2, o0, o1, o2):
    for x_ref, w_ref, b_ref, o_ref in ((x0, w0, b0, o0), (x1, w1, b1, o1), (x2, w2, b2, o2)):
        x = x_ref[0]
        n = x.shape[0]
        t = lax.broadcasted_iota(I32, (n, 1), 0)
        prev = jnp.where(t == 0, 0.0, pltpu.roll(x, 1, 0))
        nxt = jnp.where(t == n - 1, 0.0, pltpu.roll(x, n - 1, 0))
        o_ref[0] = w_ref[0:1] * prev + w_ref[1:2] * x + w_ref[2:3] * nxt + b_ref[...]


def hyena_conv3(z_hy, conv_w, conv_b, batch, seq):
    z = z_hy.reshape(batch, seq, HY_COLS)
    nc = HY_WIDTH // LANES
    x_spec = lambda g: pl.BlockSpec((1, seq, LANES), lambda b, c: (b, 0, g * nc + c))
    w_spec = lambda g: pl.BlockSpec((3, LANES), lambda b, c: (0, g * nc + c))
    b_spec = lambda g: pl.BlockSpec((1, LANES), lambda b, c: (0, g * nc + c))
    out = pl.BlockSpec((1, seq, LANES), lambda b, c: (b, 0, c))
    return pl.pallas_call(
        _conv3_body,
        grid=(batch, nc),
        in_specs=[x_spec(0), x_spec(1), x_spec(2), w_spec(0), w_spec(1), w_spec(2),
                  b_spec(0), b_spec(1), b_spec(2)],
        out_specs=[out, out, out],
        out_shape=[_sds((batch, seq, HY_WIDTH))] * 3,
        compiler_params=_cp("parallel", "parallel"),
        name="hyena_conv3",
    )(z, z, z, conv_w, conv_w, conv_w, conv_b, conv_b, conv_b)


def _filt_body(feat_ref, dist_ref, w1, b1, f1, w2, b2, f2, w3, dec, h_ref, s_ref):
    dot = functools.partial(jnp.dot, precision=HI, preferred_element_type=F32)
    hid = jnp.sin(f1[...] * (dot(feat_ref[...], w1[...]) + b1[...]))
    hid = jnp.sin(f2[...] * (dot(hid, w2[...]) + b2[...]))
    h = dot(hid, w3[...]) * jnp.exp(-dist_ref[...] * jnp.abs(dec[...]))
    h_ref[...] = h

    @pl.when(pl.program_id(0) == 0)
    def _():
        s_ref[...] = jnp.zeros_like(s_ref)

    s_ref[...] += jnp.sum(jnp.abs(h), axis=0, keepdims=True)


def hyena_filter(seq, w1, b1, f1, w2, b2, f2, w3, decay):
    j = jnp.arange(seq, dtype=F32)
    tt = j / seq
    ang = 2.0 * jnp.pi * tt[:, None] * jnp.arange(1, HY_BANDS + 1, dtype=F32)
    feat = jnp.concatenate([tt[:, None], jnp.cos(ang), jnp.sin(ang)], -1)
    feat = jnp.pad(feat, ((0, 0), (0, LANES - HY_EMB)))
    half = seq // 2
    dist = (jnp.abs(j - half) / half)[:, None]
    w1p = jnp.pad(w1, ((0, LANES - HY_EMB), (0, 0)))
    tm = min(1024, seq)
    hw = HY_ORDER * HY_WIDTH
    const = lambda i: (0, 0)
    return pl.pallas_call(
        _filt_body,
        grid=(seq // tm,),
        in_specs=[pl.BlockSpec((tm, LANES), lambda i: (i, 0)), pl.BlockSpec((tm, 1), lambda i: (i, 0)),
                  pl.BlockSpec((LANES, HY_FFN), const), pl.BlockSpec((1, HY_FFN), const),
                  pl.BlockSpec((1, HY_FFN), const), pl.BlockSpec((HY_FFN, HY_FFN), const),
                  pl.BlockSpec((1, HY_FFN), const), pl.BlockSpec((1, HY_FFN), const),
                  pl.BlockSpec((HY_FFN, hw), const), pl.BlockSpec((1, hw), const)],
        out_specs=[pl.BlockSpec((tm, hw), lambda i: (i, 0)), pl.BlockSpec((1, hw), const)],
        out_shape=[_sds((seq, hw)), _sds((1, hw))],
        compiler_params=_cp("arbitrary"),
        name="hyena_filter",
    )(feat, dist, w1p, b1[None], f1[None], w2, b2[None], f2[None], w3, decay.reshape(1, hw))


def dft_tables(seq):
    n = 2 * seq
    n2 = FFT_N2
    n1 = n // n2
    n1h = n1 // 2
    nk = -(-(n1h + 1) // SUBLANES) * SUBLANES
    k1 = jnp.arange(nk, dtype=I32)
    live = (k1 <= n1h).astype(F32)
    ang = (2.0 * jnp.pi / n1) * ((k1[:, None] * jnp.arange(n1h, dtype=I32)[None, :]) % n1).astype(F32)
    outer_fwd = jnp.concatenate([jnp.cos(ang), -jnp.sin(ang)], axis=0) * jnp.tile(live, 2)[:, None]
    t1 = jnp.arange(n1h, dtype=I32) + n1 // 4
    ang = (2.0 * jnp.pi / n1) * ((t1[:, None] * k1[None, :]) % n1).astype(F32)
    weight = live * jnp.where((k1 == 0) | (k1 == n1h), 1.0, 2.0) / n
    outer_inv = jnp.concatenate([jnp.cos(ang), -jnp.sin(ang)], axis=1) * jnp.tile(weight, 2)[None, :]
    k2 = jnp.arange(n2, dtype=I32)
    t2 = jnp.arange(n2, dtype=I32)
    phase = (t2[None, None, :] * (k2[None, :, None] * n1 + k1[:, None, None])) % n
    ang = (2.0 * jnp.pi / n) * phase.astype(F32)
    fr, fi = jnp.cos(ang), -jnp.sin(ang)
    inner = jnp.concatenate([jnp.concatenate([fr, -fi], axis=2),
                             jnp.concatenate([fi, fr], axis=2)], axis=1)
    return _split_bf16(outer_fwd), _split_bf16(outer_inv), _split_bf16(inner), _split_bf16(jnp.swapaxes(inner, 1, 2))


def _split_bf16(x):
    hi = x.astype(BF16)
    return hi, (x - hi.astype(F32)).astype(BF16)


def _dot_split(w, a):
    w_hi, w_lo = w
    a_hi, a_lo = _split_bf16(a)
    dot = functools.partial(jnp.dot, preferred_element_type=F32)
    return dot(w_hi, a_hi) + (dot(w_hi, a_lo) + dot(w_lo, a_hi))


def _dot_split_rhs(a, w):
    w_hi, w_lo = w
    a_hi, a_lo = _split_bf16(a)
    dot = functools.partial(jnp.dot, preferred_element_type=F32)
    return dot(a_hi, w_hi) + (dot(a_lo, w_hi) + dot(a_hi, w_lo))


def _every(ref, s, n):
    return ref[pl.ds(s, n, stride=FFT_ROWS), :]


def _fft_outer_body(f_ref, fl_ref, x_ref, o_ref):
    n1h = x_ref.shape[0]
    rows = f_ref.shape[0]
    x2 = x_ref.reshape(n1h * FFT_ROWS, LANES)
    o2 = o_ref.reshape(rows * FFT_ROWS, LANES)
    x = jnp.concatenate([_every(x2, s, n1h) for s in range(FFT_ROWS)], axis=1)
    r = _dot_split((f_ref[...], fl_ref[...]), x)
    for s in range(FFT_ROWS):
        o2[pl.ds(s, rows, stride=FFT_ROWS), :] = r[:, s * LANES:(s + 1) * LANES]


def fft_outer(x4, table):
    batch, n1h, n2, ch = x4.shape
    nk = table[0].shape[0] // 2
    tab = pl.BlockSpec((2 * nk, n1h), lambda b, c, l: (0, 0))
    return pl.pallas_call(
        _fft_outer_body,
        grid=(batch, n2 // FFT_ROWS, ch // LANES),
        in_specs=[tab, tab, pl.BlockSpec((None, n1h, FFT_ROWS, LANES), lambda b, c, l: (b, 0, c, l))],
        out_specs=pl.BlockSpec((None, 2, nk, FFT_ROWS, LANES), lambda b, c, l: (b, 0, 0, c, l)),
        out_shape=_sds((batch, 2, nk, n2, ch)),
        compiler_params=_cp("parallel", "parallel", "parallel"),
        name="fft_outer",
    )(*table, x4)


def _fft_filter_body(mh_ref, ml_ref, s_ref, a_ref, o_ref):
    n2 = FFT_N2
    a = a_ref[0, :, 0].reshape(2 * n2, HY_WIDTH)
    x = _dot_split((mh_ref[0], ml_ref[0]), a) * (1.0 / s_ref[...])
    o_ref[:, 0] = x.reshape(2, n2, HY_WIDTH)


def fft_filter(a5, inner, s):
    _, _, n1, n2, ch = a5.shape
    return pl.pallas_call(
        _fft_filter_body,
        grid=(n1, ch // HY_WIDTH),
        in_specs=[pl.BlockSpec((1, 2 * n2, 2 * n2), lambda k, c: (k, 0, 0)),
                  pl.BlockSpec((1, 2 * n2, 2 * n2), lambda k, c: (k, 0, 0)),
                  pl.BlockSpec((1, HY_WIDTH), lambda k, c: (0, c)),
                  pl.BlockSpec((1, 2, 1, n2, HY_WIDTH), lambda k, c: (0, 0, k, 0, c))],
        out_specs=pl.BlockSpec((2, 1, n2, HY_WIDTH), lambda k, c: (0, k, 0, c)),
        out_shape=_sds((2, n1, n2, ch)),
        compiler_params=_cp("parallel", "parallel"),
        name="fft_filter",
    )(*inner, s, a5)


def _fft_mid_body(mh_ref, ml_ref, th_ref, tl_ref, h_ref, a_ref, o_ref, *, nb):
    n2 = FFT_N2
    m = (mh_ref[0], ml_ref[0])
    mt = (th_ref[0], tl_ref[0])
    hr = h_ref[0, 0]
    hi = h_ref[1, 0]
    for b in range(nb):
        x = _dot_split(m, a_ref[b, :, 0].reshape(2 * n2, HY_WIDTH))
        xr, xi = x[:n2], x[n2:]
        y = jnp.concatenate([xr * hr - xi * hi, xr * hi + xi * hr], axis=0)
        o_ref[b, :, 0] = _dot_split(mt, y).reshape(2, n2, HY_WIDTH)


def fft_mid(a5, inner, inner_t, hf, order):
    nb, _, n1, n2, ch = a5.shape
    mat = pl.BlockSpec((1, 2 * n2, 2 * n2), lambda k: (k, 0, 0))
    blk = pl.BlockSpec((nb, 2, 1, n2, ch), lambda k: (0, 0, k, 0, 0))
    return pl.pallas_call(
        functools.partial(_fft_mid_body, nb=nb),
        grid=(n1,),
        in_specs=[mat, mat, mat, mat, pl.BlockSpec((2, 1, n2, ch), lambda k: (0, k, 0, order)), blk],
        out_specs=blk,
        out_shape=_sds(a5.shape),
        compiler_params=_cp("parallel"),
        name="fft_mid",
    )(*inner, *inner_t, hf, a5)


def _fft_inv_body(g_ref, gl_ref, b_ref, gate_ref, y_ref, fb_ref, o_ref):
    n1h, rows = g_ref.shape
    b2 = b_ref.reshape(rows * FFT_ROWS, LANES)
    gate2 = gate_ref.reshape(n1h * FFT_ROWS, LANES)
    y2 = y_ref.reshape(n1h * FFT_ROWS, LANES)
    o2 = o_ref.reshape(n1h * FFT_ROWS, LANES)
    bm = jnp.concatenate([_every(b2, s, rows) for s in range(FFT_ROWS)], axis=1)
    c = _dot_split((g_ref[...], gl_ref[...]), bm)
    fb = fb_ref[...]
    for s in range(FFT_ROWS):
        conv = c[:, s * LANES:(s + 1) * LANES]
        o2[pl.ds(s, n1h, stride=FFT_ROWS), :] = _every(gate2, s, n1h) * (conv + fb * _every(y2, s, n1h))


def fft_inv_gate(b5, table, gate4, y4, fbias):
    batch, _, nk, n2, ch = b5.shape
    n1h = table[0].shape[0]
    blk = pl.BlockSpec((None, n1h, FFT_ROWS, LANES), lambda b, c, l: (b, 0, c, l))
    tab = pl.BlockSpec((n1h, 2 * nk), lambda b, c, l: (0, 0))
    return pl.pallas_call(
        _fft_inv_body,
        grid=(batch, n2 // FFT_ROWS, ch // LANES),
        in_specs=[tab, tab,
                  pl.BlockSpec((None, 2, nk, FFT_ROWS, LANES), lambda b, c, l: (b, 0, 0, c, l)), blk, blk,
                  pl.BlockSpec((1, LANES), lambda b, c, l: (0, l))],
        out_specs=blk,
        out_shape=_sds((batch, n1h, n2, ch)),
        compiler_params=_cp("parallel", "parallel", "parallel"),
        name="fft_inv_gate",
    )(*table, b5, gate4, y4, fbias.reshape(1, ch))


def hyena_spectrum(seq, tabs, w1, b1, f1, w2, b2, f2, w3, decay):
    outer_fwd, _, inner, _ = tabs
    n1h = outer_fwd[0].shape[1]
    hw = HY_ORDER * HY_WIDTH
    h, s = hyena_filter(seq, w1, b1, f1, w2, b2, f2, w3, decay)
    return fft_filter(fft_outer(h.reshape(1, n1h, FFT_N2, hw), outer_fwd), inner, s)


def hyena_mixer(z_hy, conv_w, conv_b, hf, f_bias, tabs, batch, seq):
    outer_fwd, outer_inv, inner, inner_t = tabs
    n1h = outer_fwd[0].shape[1]
    shape4 = (batch, n1h, FFT_N2, HY_WIDTH)
    v, g0, g1 = hyena_conv3(z_hy, conv_w, conv_b, batch, seq)
    y = v.reshape(shape4)
    for order, gate in enumerate((g0, g1)):
        bm = fft_mid(fft_outer(y, outer_fwd), inner, inner_t, hf, order)
        y = fft_inv_gate(bm, outer_inv, gate.reshape(shape4), y, f_bias[order])
    return y.reshape(batch * seq, HY_WIDTH)


def _mix_body(lam_ref, h_ref, s5_ref, gqa_ref, diff_ref, hy_ref, gs5, ggqa, gsub, ghy, w_ref, o_ref, *, keep):
    lam = lam_ref[0]
    a = _rms(s5_ref[...], gs5[...])
    b = _rms(jnp.concatenate([gqa_ref[0, h] for h in range(GQA_HEADS)], axis=1), ggqa[...])
    c = jnp.concatenate(
        [_rms(diff_ref[0, 2 * h] - lam * diff_ref[0, 2 * h + 1], gsub[...]) * keep for h in range(DIFF_HEADS)],
        axis=1)
    d = _rms(hy_ref[...], ghy[...])
    mixed = jnp.concatenate([a, b, c, d], axis=1).astype(BF16)
    o_ref[...] = h_ref[...] + jnp.dot(mixed, w_ref[...], preferred_element_type=F32)


def mix_out(lam, h, y_s5, o_gqa, o_diff, y_hy, gs5, ggqa, gsub, ghy, w_out, batch, seq, lambda_init):
    tm = min(ROW_TILE, seq)
    nb = seq // tm
    row = lambda w: pl.BlockSpec((tm, w), lambda b, i: (b * nb + i, 0))
    vec = lambda w: pl.BlockSpec((1, w), lambda b, i: (0, 0))
    heads = lambda n: pl.BlockSpec((1, n, tm, HEAD_DIM), lambda b, i: (b, 0, i, 0))
    return pl.pallas_call(
        functools.partial(_mix_body, keep=1.0 - lambda_init),
        grid=(batch, nb),
        in_specs=[pl.BlockSpec(memory_space=pltpu.SMEM), row(D_MODEL), row(GROUP_WIDTH), heads(GQA_HEADS),
                  heads(2 * DIFF_HEADS), row(GROUP_WIDTH), vec(GROUP_WIDTH), vec(GROUP_WIDTH), vec(HEAD_DIM),
                  vec(GROUP_WIDTH), pl.BlockSpec((D_MODEL, D_MODEL), lambda b, i: (0, 0))],
        out_specs=row(D_MODEL),
        out_shape=_sds((batch * seq, D_MODEL)),
        compiler_params=_cp("parallel", "parallel"),
        name="mix_out",
    )(lam, h, y_s5, o_gqa, o_diff, y_hy, gs5, ggqa, gsub, ghy, w_out)


def _router_body(h_ref, g_ref, w_ref, wl_ref, b_ref, xn_ref, ti_ref, tw_ref, cnt_ref):
    xn = _rms(h_ref[...], g_ref[...])
    xn_ref[...] = xn
    logits = _dot_split_rhs(xn, (w_ref[...], wl_ref[...])) + b_ref[...]
    lane = lax.broadcasted_iota(I32, logits.shape, 1)
    vals = logits
    tops, idxs = [], []
    hot = jnp.zeros(logits.shape, F32)
    for _ in range(TOP_K):
        m = jnp.max(vals, axis=-1, keepdims=True)
        idx = jnp.min(jnp.where(vals == m, lane, N_EXPERTS), axis=-1, keepdims=True)
        sel = lane == idx
        tops.append(m)
        idxs.append(idx)
        hot = hot + sel.astype(F32)
        vals = jnp.where(sel, -jnp.inf, vals)
    es = [jnp.exp(t - tops[0]) for t in tops]
    den = es[0] + es[1] + es[2] + es[3]
    ti_ref[...] = jnp.concatenate(idxs, axis=1)
    tw_ref[...] = jnp.concatenate([e / den for e in es], axis=1)

    @pl.when(pl.program_id(0) == 0)
    def _():
        cnt_ref[...] = jnp.zeros_like(cnt_ref)

    cnt_ref[...] += jnp.sum(hot, axis=0, keepdims=True)


def moe_router(h, g, w_router, b_router):
    t = h.shape[0]
    tm = min(ROW_TILE, t)
    const = lambda i: (0, 0)
    return pl.pallas_call(
        _router_body,
        grid=(t // tm,),
        in_specs=[pl.BlockSpec((tm, D_MODEL), lambda i: (i, 0)), pl.BlockSpec((1, D_MODEL), const),
                  pl.BlockSpec((D_MODEL, N_EXPERTS), const), pl.BlockSpec((D_MODEL, N_EXPERTS), const),
                  pl.BlockSpec((1, N_EXPERTS), const)],
        out_specs=[pl.BlockSpec((tm, D_MODEL), lambda i: (i, 0)), pl.BlockSpec((tm, TOP_K), lambda i: (i, 0)),
                   pl.BlockSpec((tm, TOP_K), lambda i: (i, 0)), pl.BlockSpec((1, N_EXPERTS), const)],
        out_shape=[_sds((t, D_MODEL)), _sds((t, TOP_K), I32), _sds((t, TOP_K)), _sds((1, N_EXPERTS))],
        compiler_params=_cp("arbitrary"),
        name="moe_router",
    )(h, g, *w_router, b_router)


def _rank_body(ti_ref, off_ref, pos_ref, carry_ref):
    @pl.when(pl.program_id(0) == 0)
    def _():
        carry_ref[...] = jnp.zeros_like(carry_ref)

    ti = ti_ref[...]
    tm = ti.shape[0]
    lane = lax.broadcasted_iota(I32, (tm, N_EXPERTS), 1)
    hots = [lane == ti[:, k:k + 1] for k in range(TOP_K)]
    hot = sum(h.astype(F32) for h in hots)
    r = lax.broadcasted_iota(I32, (tm, tm), 0)
    c = lax.broadcasted_iota(I32, (tm, tm), 1)
    below = jnp.where(r > c, 1.0, 0.0).astype(BF16)
    before = jnp.dot(below, hot.astype(BF16), preferred_element_type=F32) + carry_ref[...] + off_ref[...]
    pos = [jnp.sum(jnp.where(h, before, 0.0), axis=-1, keepdims=True) for h in hots]
    pos_ref[...] = jnp.concatenate(pos, axis=1).astype(I32)
    carry_ref[...] += jnp.sum(hot, axis=0, keepdims=True)


def moe_rank(top_i, offsets):
    t = top_i.shape[0]
    tm = min(ROW_TILE, t)
    return pl.pallas_call(
        _rank_body,
        grid=(t // tm,),
        in_specs=[pl.BlockSpec((tm, TOP_K), lambda i: (i, 0)), pl.BlockSpec((1, N_EXPERTS), lambda i: (0, 0))],
        out_specs=pl.BlockSpec((tm, TOP_K), lambda i: (i, 0)),
        out_shape=_sds((t, TOP_K), I32),
        scratch_shapes=[pltpu.VMEM((1, N_EXPERTS), F32)],
        compiler_params=_cp("arbitrary"),
        name="moe_rank",
    )(top_i, offsets)


def _row_copy(src, dst, sem):
    return pltpu.make_async_copy(src, dst, sem)


def _pad_zero_body(lt_ref, o_ref):
    del lt_ref
    o_ref[...] = jnp.zeros_like(o_ref)


def moe_pad_tiles(last_tile, n_slots):
    rows = MOE_TILE * ROW_TILES
    return pl.pallas_call(
        _pad_zero_body,
        grid_spec=pltpu.PrefetchScalarGridSpec(
            num_scalar_prefetch=1, grid=(N_EXPERTS,), in_specs=[],
            out_specs=pl.BlockSpec((rows, LANES), lambda e, lt: (lt[e], 0))),
        out_shape=_sds((n_slots * ROW_TILES, LANES)),
        compiler_params=_cp("arbitrary"),
        name="moe_pad_tiles",
    )(last_tile)


def _slot_rows(ref, slot):
    return ref.at[pl.ds(pl.multiple_of(slot * ROW_TILES, ROW_TILES), ROW_TILES)]


def _scatter_body(pos_ref, x_ref, padded_hbm, xs_hbm, buf, sem, *, tm):
    del padded_hbm
    x = x_ref[...]
    for s in range(ROW_TILES):
        buf[pl.ds(s, tm, stride=ROW_TILES), :] = x[:, s * LANES:(s + 1) * LANES]

    def start(r, c):
        for k in range(TOP_K):
            _row_copy(_slot_rows(buf, r), _slot_rows(xs_hbm, pos_ref[r * TOP_K + k]), sem).start()
        return c

    lax.fori_loop(0, tm, start, 0)
    for _ in range(TOP_K):
        _row_copy(buf, xs_hbm.at[pl.ds(0, tm * ROW_TILES)], sem).wait()


def moe_scatter(pos_flat, x, padded):
    t = x.shape[0]
    tm = min(MOE_TOK_TILE, t)
    return pl.pallas_call(
        functools.partial(_scatter_body, tm=tm),
        grid=(t // tm,),
        in_specs=[pl.BlockSpec((tm * TOP_K,), lambda i: (i,), memory_space=pltpu.SMEM),
                  pl.BlockSpec((tm, D_MODEL), lambda i: (i, 0)), pl.BlockSpec(memory_space=pl.ANY)],
        out_specs=pl.BlockSpec(memory_space=pl.ANY),
        out_shape=_sds(padded.shape),
        scratch_shapes=[pltpu.VMEM((tm * ROW_TILES, LANES), F32), pltpu.SemaphoreType.DMA(())],
        input_output_aliases={2: 0},
        compiler_params=_cp("arbitrary"),
        name="moe_scatter",
    )(pos_flat, x, padded)


def _experts_body(te_ref, nv_ref, x_ref, wgu_ref, bgu_ref, wd_ref, bd_ref, o_ref):
    del te_ref
    j = pl.program_id(0)
    dff = wd_ref.shape[1]

    @pl.when(j < nv_ref[0])
    def _():
        tm = x_ref.shape[0] // ROW_TILES
        x = jnp.concatenate([x_ref[pl.ds(s, tm, stride=ROW_TILES), :] for s in range(ROW_TILES)],
                            axis=1).astype(BF16)
        hgu = jnp.dot(x, wgu_ref[0], preferred_element_type=F32) + bgu_ref[0]
        glu = jnp.minimum(hgu[:, :dff], SWIGLU_LIMIT)
        lin = jnp.clip(hgu[:, dff:], -SWIGLU_LIMIT, SWIGLU_LIMIT)
        act = glu * jax.nn.sigmoid(SWIGLU_ALPHA * glu) * (lin + 1.0)
        y = jnp.dot(act.astype(BF16), wd_ref[0], preferred_element_type=F32) + bd_ref[0]
        for s in range(ROW_TILES):
            o_ref[pl.ds(s, tm, stride=ROW_TILES), :] = y[:, s * LANES:(s + 1) * LANES]

    @pl.when(j >= nv_ref[0])
    def _():
        o_ref[...] = jnp.zeros_like(o_ref)


def moe_experts(tile_expert, n_valid, xs, w_gu, b_gu, w_down, b_down):
    n_slots = xs.shape[0] // ROW_TILES
    tm = MOE_TILE
    dff = w_down.shape[1]
    blk = pl.BlockSpec((tm * ROW_TILES, LANES), lambda j, te, nv: (j, 0))
    return pl.pallas_call(
        _experts_body,
        grid_spec=pltpu.PrefetchScalarGridSpec(
            num_scalar_prefetch=2,
            grid=(n_slots // tm,),
            in_specs=[pl.BlockSpec((tm * ROW_TILES, LANES), lambda j, te, nv: (jnp.minimum(j, nv[0] - 1), 0)),
                      pl.BlockSpec((1, D_MODEL, 2 * dff), lambda j, te, nv: (te[j], 0, 0)),
                      pl.BlockSpec((1, 1, 2 * dff), lambda j, te, nv: (te[j], 0, 0)),
                      pl.BlockSpec((1, dff, D_MODEL), lambda j, te, nv: (te[j], 0, 0)),
                      pl.BlockSpec((1, 1, D_MODEL), lambda j, te, nv: (te[j], 0, 0))],
            out_specs=blk),
        out_shape=_sds((n_slots * ROW_TILES, LANES)),
        compiler_params=_cp("arbitrary"),
        name="moe_experts",
    )(tile_expert, n_valid, xs, w_gu, b_gu, w_down, b_down)


def _combine_body(pos_ref, w_ref, ys_hbm, o_ref, b0, b1, b2, b3, mixed, sem, *, tm):
    bufs = (b0, b1, b2, b3)

    def start(r, c):
        for k in range(TOP_K):
            _row_copy(_slot_rows(ys_hbm, pos_ref[r * TOP_K + k]), _slot_rows(bufs[k], r), sem).start()
        return c

    lax.fori_loop(0, tm, start, 0)
    for k in range(TOP_K):
        _row_copy(ys_hbm.at[pl.ds(0, tm * ROW_TILES)], bufs[k], sem).wait()

    def mix(q, c):
        for u in range(MIX_UNROLL):
            r = q * MIX_UNROLL + u
            rows = pl.ds(pl.multiple_of(r * ROW_TILES, ROW_TILES), ROW_TILES)
            acc = w_ref[r * TOP_K] * b0[rows, :]
            for k in range(1, TOP_K):
                acc = acc + w_ref[r * TOP_K + k] * bufs[k][rows, :]
            mixed[rows, :] = acc
        return c

    lax.fori_loop(0, tm // MIX_UNROLL, mix, 0)
    o_ref[...] = jnp.concatenate([mixed[pl.ds(s, tm, stride=ROW_TILES), :] for s in range(ROW_TILES)], axis=1)


def moe_combine(pos_flat, w_flat, ys, t):
    tm = min(MOE_TOK_TILE, t)
    smem = pl.BlockSpec((tm * TOP_K,), lambda i: (i,), memory_space=pltpu.SMEM)
    tile_buf = pltpu.VMEM((tm * ROW_TILES, LANES), F32)
    return pl.pallas_call(
        functools.partial(_combine_body, tm=tm),
        grid=(t // tm,),
        in_specs=[smem, smem, pl.BlockSpec(memory_space=pl.ANY)],
        out_specs=pl.BlockSpec((tm, D_MODEL), lambda i: (i, 0)),
        out_shape=_sds((t, D_MODEL)),
        scratch_shapes=[tile_buf] * (TOP_K + 1) + [pltpu.SemaphoreType.DMA(())],
        compiler_params=_cp("arbitrary"),
        name="moe_combine",
    )(pos_flat, w_flat, ys)


def moe(h, g, w_router, b_router, w_gu, b_gu, w_down, b_down):
    t = h.shape[0]
    xn, top_i, top_w, counts = moe_router(h, g, w_router, b_router)
    n_tiles = (t * TOP_K) // MOE_TILE + N_EXPERTS
    padded = jnp.ceil(counts[0] / MOE_TILE) * MOE_TILE
    ends = jnp.cumsum(padded)
    offsets = (ends - padded)[None]
    tile_start = jnp.arange(n_tiles, dtype=F32) * MOE_TILE
    tile_expert = jnp.minimum(jnp.sum(ends[None, :] <= tile_start[:, None], axis=1), N_EXPERTS - 1).astype(I32)
    n_valid = (ends[-1:] / MOE_TILE).astype(I32)
    own_last = jnp.where(padded > 0, ends / MOE_TILE - 1, -1.0)
    filled = lax.cummax(own_last, axis=0)
    first = jnp.min(jnp.where(padded > 0, own_last, jnp.inf))
    last_tile = jnp.where(filled < 0, first, filled).astype(I32)
    pos = moe_rank(top_i, offsets).reshape(t * TOP_K)
    xs = moe_scatter(pos, xn, moe_pad_tiles(last_tile, n_tiles * MOE_TILE))
    ys = moe_experts(tile_expert, n_valid, xs, w_gu, b_gu, w_down, b_down)
    return moe_combine(pos, top_w.reshape(t * TOP_K), ys, t)


def _ple_body(h_ref, m_ref, p_ref, wp_ref, wg_ref, o_ref):
    h = h_ref[...] + m_ref[...]
    e = jnp.dot(p_ref[...].astype(BF16), wp_ref[...], preferred_element_type=F32)
    gate = jnp.dot(_rms(h).astype(BF16), wg_ref[...], preferred_element_type=F32)
    o_ref[...] = h + e * jax.nn.sigmoid(gate)


def ple(h, moe_out, p, w_ple, w_gate):
    t = h.shape[0]
    tm = min(ROW_TILE, t)
    row = pl.BlockSpec((tm, D_MODEL), lambda i: (i, 0))
    return pl.pallas_call(
        _ple_body,
        grid=(t // tm,),
        in_specs=[row, row, pl.BlockSpec((tm, PLE_DIM), lambda i: (i, 0)),
                  pl.BlockSpec((PLE_DIM, D_MODEL), lambda i: (0, 0)),
                  pl.BlockSpec((D_MODEL, D_MODEL), lambda i: (0, 0))],
        out_specs=row,
        out_shape=_sds((t, D_MODEL)),
        compiler_params=_cp("parallel"),
        name="ple",
    )(h, moe_out, p, w_ple, w_gate)


def _deinterleave(n):
    return np.concatenate([np.arange(0, n, 2), np.arange(1, n, 2)])


def _halves(w):
    pairs = w.reshape(*w.shape[:-1], w.shape[-1] // 2, 2)
    return jnp.concatenate([pairs[..., 0], pairs[..., 1]], axis=-1)


def prepare_layer(i, prm):
    pair = _deinterleave(HEAD_DIM)
    cols = np.arange(IN_COLS)
    for h in range(GQA_HEADS + GQA_KV_HEADS):
        lo = OFF_GQA + h * HEAD_DIM
        cols[lo:lo + HEAD_DIM] = lo + pair
    lam = (jnp.exp(jnp.sum(prm["diff_lam_q1"][i] * prm["diff_lam_k1"][i]))
           - jnp.exp(jnp.sum(prm["diff_lam_q2"][i] * prm["diff_lam_k2"][i])) + _lambda_init(i))
    return dict(
        norm1_g=prm["norm1_g"][i][None],
        w_in=prm["w_in"][i][:, cols].astype(BF16),
        s5=s5_tables(prm["s5_lam_re"][i], prm["s5_lam_im"][i], prm["s5_log_dt"][i], prm["s5_b_re"][i],
                     prm["s5_b_im"][i], prm["s5_c_re"][i], prm["s5_c_im"][i]),
        s5_d=prm["s5_d"][i][None],
        s5_w_glu=prm["s5_w_glu"][i].astype(BF16),
        s5_b_glu=prm["s5_b_glu"][i][None],
        gqa_q_g=prm["gqa_q_g"][i][pair][None],
        gqa_k_g=prm["gqa_k_g"][i][pair][None],
        diff_q_g=prm["diff_q_g"][i][None],
        diff_k_g=prm["diff_k_g"][i][None],
        diff_lam=lam.reshape(1).astype(F32),
        diff_subln_g=prm["diff_subln_g"][i][None],
        out_g_s5=prm["out_g_s5"][i][None],
        out_g_gqa=prm["out_g_gqa"][i][None],
        out_g_hy=prm["out_g_hy"][i][None],
        w_out=prm["w_out"][i].astype(BF16),
        norm2_g=prm["norm2_g"][i][None],
        w_router=_split_bf16(prm["w_router"][i]),
        b_router=prm["b_router"][i][None],
        w_gu=_halves(prm["w_gu"][i]).astype(BF16),
        b_gu=_halves(prm["b_gu"][i])[:, None, :],
        w_down=prm["w_down"][i].astype(BF16),
        b_down=prm["b_down"][i][:, None, :],
        w_ple=prm["w_ple"][i].astype(BF16),
        w_ple_gate=prm["w_ple_gate"][i].astype(BF16),
    )


def _lambda_init(i):
    return 0.8 - 0.6 * math.exp(-0.3 * i)


def run_trunk(x, p, prm, layers):
    batch, seq, _ = x.shape
    t = batch * seq
    cos_t, sin_t = rope_tables(seq)
    tabs = dft_tables(seq)
    bias = diff_bias_tiles(prm["rel_bias"], min(ATTN_UNIT, seq))
    h = x.reshape(t, D_MODEL)
    for i, lw in enumerate(layers):
        z_s5, z_gqa, z_diff, z_hy = in_proj(h, lw["norm1_g"], lw["w_in"])
        y_s5 = s5_mixer(z_s5, lw["s5"], lw["s5_d"], lw["s5_w_glu"], lw["s5_b_glu"], batch, seq)
        rep = GQA_HEADS // GQA_KV_HEADS
        q, k, v = attn_prep(z_gqa, lw["gqa_q_g"], lw["gqa_k_g"], cos_t, sin_t, batch, seq,
                            GQA_HEADS, GQA_KV_HEADS, GQA_KV_HEADS, True, lambda hq: hq // rep)
        o_gqa = flash_attention(q, k, v, lambda hq: hq // rep)
        q, k, v = attn_prep(z_diff, lw["diff_q_g"], lw["diff_k_g"], cos_t, sin_t, batch, seq,
                            2 * DIFF_HEADS, 2 * DIFF_HEADS, DIFF_HEADS, False, lambda hq: hq)
        o_diff = flash_attention(q, k, v, lambda hq: hq // 2, bias)
        hf = hyena_spectrum(seq, tabs, prm["hy_w1"][i], prm["hy_b1"][i], prm["hy_freq1"][i], prm["hy_w2"][i],
                            prm["hy_b2"][i], prm["hy_freq2"][i], prm["hy_w3"][i], prm["hy_decay"][i])
        y_hy = hyena_mixer(z_hy, prm["hy_conv_w"][i], prm["hy_conv_b"][i][None], hf, prm["hy_bias"][i],
                           tabs, batch, seq)
        h = mix_out(lw["diff_lam"], h, y_s5, o_gqa, o_diff, y_hy, lw["out_g_s5"], lw["out_g_gqa"],
                    lw["diff_subln_g"], lw["out_g_hy"], lw["w_out"], batch, seq, _lambda_init(i))
        m = moe(h, lw["norm2_g"], lw["w_router"], lw["b_router"], lw["w_gu"], lw["b_gu"], lw["w_down"],
                lw["b_down"])
        h = ple(h, m, p[i].reshape(t, PLE_DIM), lw["w_ple"], lw["w_ple_gate"])
    return h.reshape(batch, seq, D_MODEL)


def kernel(x_prompt, x_sample, p_prompt, p_sample, rel_bias, norm1_g, w_in, s5_lam_re, s5_lam_im, s5_log_dt,
           s5_b_re, s5_b_im, s5_c_re, s5_c_im, s5_d, s5_w_glu, s5_b_glu, gqa_q_g, gqa_k_g, diff_q_g, diff_k_g,
           diff_lam_q1, diff_lam_k1, diff_lam_q2, diff_lam_k2, diff_subln_g, hy_conv_w, hy_conv_b, hy_w1, hy_b1,
           hy_freq1, hy_w2, hy_b2, hy_freq2, hy_w3, hy_decay, hy_bias, out_g_s5, out_g_gqa, out_g_hy, w_out,
           norm2_g, w_router, b_router, w_gu, b_gu, w_down, b_down, w_ple, w_ple_gate):
    prm = dict(rel_bias=rel_bias, norm1_g=norm1_g, w_in=w_in, s5_lam_re=s5_lam_re, s5_lam_im=s5_lam_im,
               s5_log_dt=s5_log_dt, s5_b_re=s5_b_re, s5_b_im=s5_b_im, s5_c_re=s5_c_re, s5_c_im=s5_c_im,
               s5_d=s5_d, s5_w_glu=s5_w_glu, s5_b_glu=s5_b_glu, gqa_q_g=gqa_q_g, gqa_k_g=gqa_k_g,
               diff_q_g=diff_q_g, diff_k_g=diff_k_g, diff_lam_q1=diff_lam_q1, diff_lam_k1=diff_lam_k1,
               diff_lam_q2=diff_lam_q2, diff_lam_k2=diff_lam_k2, diff_subln_g=diff_subln_g,
               hy_conv_w=hy_conv_w, hy_conv_b=hy_conv_b, hy_w1=hy_w1, hy_b1=hy_b1, hy_freq1=hy_freq1,
               hy_w2=hy_w2, hy_b2=hy_b2, hy_freq2=hy_freq2, hy_w3=hy_w3, hy_decay=hy_decay, hy_bias=hy_bias,
               out_g_s5=out_g_s5, out_g_gqa=out_g_gqa, out_g_hy=out_g_hy, w_out=w_out, norm2_g=norm2_g,
               w_router=w_router, b_router=b_router, w_gu=w_gu, b_gu=b_gu, w_down=w_down, b_down=b_down,
               w_ple=w_ple, w_ple_gate=w_ple_gate)
    layers = [prepare_layer(i, prm) for i in range(DEPTH)]
    return (run_trunk(x_prompt, p_prompt, prm, layers), run_trunk(x_sample, p_sample, prm, layers))
```

```python
import functools
import math

import numpy as np
import jax
import jax.numpy as jnp
from jax import lax
from jax.experimental import pallas as pl
from jax.experimental.pallas import tpu as pltpu

F32 = jnp.float32
BF16 = jnp.bfloat16
I32 = jnp.int32
HI = lax.Precision.HIGHEST

D_MODEL = 1024
DEPTH = 2
GROUP_WIDTH = 256
HEAD_DIM = 64
S5_GROUP_DIM = 16
S5_GROUPS = 16
S5_STATE = 64
S5_NSTATE = S5_GROUPS * S5_STATE
GQA_HEADS = 4
GQA_KV_HEADS = 2
DIFF_HEADS = 4
HY_WIDTH = 256
HY_ORDER = 2
HY_BANDS = 16
HY_EMB = 2 * HY_BANDS + 1
HY_FFN = 64
N_EXPERTS = 32
TOP_K = 4
SWIGLU_LIMIT = 7.0
SWIGLU_ALPHA = 1.702
NUM_BUCKETS = 32
MAX_DISTANCE = 128
GRID_W = 64
ROPE_THETA = 10000.0
ROPE_AXIS_DIM = HEAD_DIM // 2
PLE_DIM = 256
NORM_EPS = 1e-6
ATTN_SCALE = HEAD_DIM ** -0.5
LOG2E = math.log2(math.e)

S5_COLS = GROUP_WIDTH
GQA_QW = GQA_HEADS * HEAD_DIM
GQA_KW = GQA_KV_HEADS * HEAD_DIM
GQA_COLS = GQA_QW + 2 * GQA_KW
DIFF_QW = DIFF_HEADS * 2 * HEAD_DIM
DIFF_COLS = 2 * DIFF_QW + DIFF_HEADS * HEAD_DIM
HY_COLS = (HY_ORDER + 1) * HY_WIDTH
OFF_GQA = S5_COLS
OFF_DIFF = OFF_GQA + GQA_COLS
OFF_HY = OFF_DIFF + DIFF_COLS
IN_COLS = OFF_HY + HY_COLS
IN_SPLITS = ((0, OFF_GQA), (OFF_GQA, OFF_DIFF), (OFF_DIFF, OFF_HY), (OFF_HY, IN_COLS))

LANES = 128
SUBLANES = 8
ROW_TILES = D_MODEL // LANES
FFT_N2 = 128
FFT_ROWS = 32
VMEM_LIMIT = 48 * 1024 * 1024

ROW_TILE = 256
ATTN_Q_TILE = 2048
ATTN_K_TILE = 2048
ATTN_UNIT = 512
ATTN_SUB = 512
ATTN_SUB_PLAIN = 1024
S5_TIME_BLOCK = 64
MOE_TILE = 512
MOE_TOK_TILE = 256
MIX_UNROLL = 4


def _cp(*sem):
    return pltpu.CompilerParams(dimension_semantics=sem, vmem_limit_bytes=VMEM_LIMIT)


def _sds(shape, dtype=F32):
    return jax.ShapeDtypeStruct(shape, dtype)


def _rms(x, g=None):
    y = x * lax.rsqrt(jnp.mean(x * x, axis=-1, keepdims=True) + NORM_EPS)
    return y if g is None else y * g


def _inproj_body(x_ref, g_ref, w_ref, *o_refs):
    u = _rms(x_ref[...], g_ref[...]).astype(BF16)
    for o_ref, (lo, hi) in zip(o_refs, IN_SPLITS):
        o_ref[...] = jnp.dot(u, w_ref[:, lo:hi], preferred_element_type=F32)


def in_proj(h, g, w, batch, seq):
    t = h.shape[0]
    tm = min(ROW_TILE, seq)
    nb = seq // tm
    row_major = lambda i: (i, 0)
    time_major = lambda i: (i % nb, i // nb)
    widths = [hi - lo for lo, hi in IN_SPLITS]
    return pl.pallas_call(
        _inproj_body,
        grid=(t // tm,),
        in_specs=[pl.BlockSpec((tm, D_MODEL), row_major),
                  pl.BlockSpec((1, D_MODEL), lambda i: (0, 0)),
                  pl.BlockSpec((D_MODEL, IN_COLS), lambda i: (0, 0))],
        out_specs=[pl.BlockSpec((tm, widths[0]), time_major)] + [pl.BlockSpec((tm, w_), row_major) for w_ in widths[1:]],
        out_shape=[_sds((seq, batch * widths[0]))] + [_sds((t, w_)) for w_ in widths[1:]],
        compiler_params=_cp("parallel"),
        name="in_proj",
    )(h, g, w)


def s5_tables(lam_re, lam_im, log_dt, b_re, b_im, c_re, c_im):
    lr = jnp.minimum(lam_re, -1e-4)
    li = lam_im
    dt = jnp.exp(log_dt)[..., None]
    mag = jnp.exp(lr * dt)
    ar = mag * jnp.cos(li * dt)
    ai = mag * jnp.sin(li * dt)
    den = lr * lr + li * li
    cr = ((ar - 1.0) * lr + ai * li) / den
    ci = (ai * lr - (ar - 1.0) * li) / den
    bbr = cr[..., None] * b_re - ci[..., None] * b_im
    bbi = cr[..., None] * b_im + ci[..., None] * b_re
    eye = jnp.eye(S5_GROUPS, dtype=F32)
    bb = jnp.stack([bbr, bbi], axis=1)
    wb = jnp.einsum("dpgnh,gk->dghpkn", bb, eye).reshape(2, S5_COLS, 2 * S5_NSTATE)
    cc = jnp.stack([c_re, -c_im], axis=0)
    wc = jnp.einsum("pdghn,gk->dpgnkh", cc, eye).reshape(2, 2 * S5_NSTATE, S5_COLS)
    a_re = jnp.repeat(ar.reshape(2, S5_NSTATE), 4, axis=0)
    a_im = jnp.repeat(ai.reshape(2, S5_NSTATE), 4, axis=0)
    return wb.astype(BF16), wc.astype(BF16), a_re, a_im


def _s5_body(uf_ref, ub_ref, wb_ref, wc_ref, ar_ref, ai_ref, yf_ref, yb_ref, bf_ref, bb_ref, xr_ref, xi_ref, *, tb):
    @pl.when(pl.program_id(0) == 0)
    def _():
        xr_ref[...] = jnp.zeros_like(xr_ref)
        xi_ref[...] = jnp.zeros_like(xi_ref)

    bf_ref[...] = jnp.dot(uf_ref[...].astype(BF16), wb_ref[0], preferred_element_type=F32)
    bb_ref[...] = jnp.dot(ub_ref[...].astype(BF16), wb_ref[1], preferred_element_type=F32)
    ar = ar_ref[...]
    ai = ai_ref[...]
    low = lax.broadcasted_iota(I32, (SUBLANES, 1), 0) < 4
    half = SUBLANES // 2
    re = slice(0, S5_NSTATE)
    im = slice(S5_NSTATE, 2 * S5_NSTATE)

    def pair(m, carry):
        xr, xi = carry
        rf = pl.ds(pl.multiple_of(m * SUBLANES, SUBLANES), SUBLANES)
        rb = pl.ds(pl.multiple_of((tb // 2 - 1 - m) * SUBLANES, SUBLANES), SUBLANES)
        f_r, f_i, b_r, b_i = bf_ref[rf, re], bf_ref[rf, im], bb_ref[rb, re], bb_ref[rb, im]
        in_r = jnp.where(low, f_r, b_r)
        in_i = jnp.where(low, f_i, b_i)
        x1r = ar * xr - ai * xi + in_r
        x1i = ar * xi + ai * xr + in_i
        in_r = pltpu.roll(jnp.where(low, b_r, f_r), half, 0)
        in_i = pltpu.roll(jnp.where(low, b_i, f_i), half, 0)
        x2r = ar * x1r - ai * x1i + in_r
        x2i = ar * x1i + ai * x1r + in_i
        s2r = pltpu.roll(x2r, half, 0)
        s2i = pltpu.roll(x2i, half, 0)
        bf_ref[rf, re] = jnp.where(low, x1r, s2r)
        bf_ref[rf, im] = jnp.where(low, x1i, s2i)
        bb_ref[rb, re] = jnp.where(low, s2r, x1r)
        bb_ref[rb, im] = jnp.where(low, s2i, x1i)
        return x2r, x2i

    xr, xi = lax.fori_loop(0, tb // 2, pair, (xr_ref[...], xi_ref[...]))
    xr_ref[...] = xr
    xi_ref[...] = xi
    yf_ref[...] = jnp.dot(bf_ref[...].astype(BF16), wc_ref[0], preferred_element_type=F32)
    yb_ref[...] = jnp.dot(bb_ref[...].astype(BF16), wc_ref[1], preferred_element_type=F32)


def s5_scan(u, wb, wc, a_re, a_im, seq):
    tb = min(S5_TIME_BLOCK, seq)
    nb = seq // tb
    rows = tb * 4
    fwd = pl.BlockSpec((rows, S5_COLS), lambda i: (i, 0))
    bwd = pl.BlockSpec((rows, S5_COLS), lambda i: (nb - 1 - i, 0))
    return pl.pallas_call(
        functools.partial(_s5_body, tb=tb),
        grid=(nb,),
        in_specs=[fwd, bwd,
                  pl.BlockSpec((2, S5_COLS, 2 * S5_NSTATE), lambda i: (0, 0, 0)),
                  pl.BlockSpec((2, 2 * S5_NSTATE, S5_COLS), lambda i: (0, 0, 0)),
                  pl.BlockSpec((SUBLANES, S5_NSTATE), lambda i: (0, 0)),
                  pl.BlockSpec((SUBLANES, S5_NSTATE), lambda i: (0, 0))],
        out_specs=[fwd, bwd],
        out_shape=[_sds((seq * 4, S5_COLS))] * 2,
        scratch_shapes=[pltpu.VMEM((rows, 2 * S5_NSTATE), F32), pltpu.VMEM((rows, 2 * S5_NSTATE), F32),
                        pltpu.VMEM((SUBLANES, S5_NSTATE), F32), pltpu.VMEM((SUBLANES, S5_NSTATE), F32)],
        compiler_params=_cp("arbitrary"),
        name="s5_scan",
    )(u, u, wb, wc, a_re, a_im)


def _s5_out_body(yf_ref, yb_ref, u_ref, d_ref, w_ref, b_ref, o_ref):
    y = jax.nn.gelu(yf_ref[...] + yb_ref[...] + u_ref[...] * d_ref[...])
    gate = jnp.dot(y.astype(BF16), w_ref[...], preferred_element_type=F32) + b_ref[...]
    o_ref[...] = y * jax.nn.sigmoid(gate)


def s5_out(yf, yb, u, d, w_glu, b_glu):
    t = u.shape[0]
    tm = min(ROW_TILE, t)
    row = pl.BlockSpec((tm, S5_COLS), lambda i: (i, 0))
    vec = pl.BlockSpec((1, S5_COLS), lambda i: (0, 0))
    return pl.pallas_call(
        _s5_out_body,
        grid=(t // tm,),
        in_specs=[row, row, row, vec, pl.BlockSpec((S5_COLS, S5_COLS), lambda i: (0, 0)), vec],
        out_specs=row,
        out_shape=_sds((t, S5_COLS)),
        compiler_params=_cp("parallel"),
        name="s5_out",
    )(yf, yb, u, d, w_glu, b_glu)


def s5_mixer(z_s5, tabs, d, w_glu, b_glu, batch, seq):
    assert batch == 4 and seq % 2 == 0, "the scan packs 4 sequences x 2 directions onto the 8 sublanes"
    u = z_s5.reshape(seq * batch, S5_COLS)
    yf, yb = s5_scan(u, *tabs, seq)
    return s5_out(yf, yb, u, d, w_glu, b_glu).reshape(seq, batch * S5_COLS)


def _prep_body(z_ref, qg_ref, kg_ref, c_ref, s_ref, q_ref, k_ref, v_ref, *, nq, nk, nv, rope, k_src):
    tm = z_ref.shape[0]
    lane = lax.broadcasted_iota(I32, (tm, LANES), 1)
    low = lane < HEAD_DIM

    def norm_pair(x, g):
        sq = x * x
        ss_lo = jnp.sum(jnp.where(low, sq, 0.0), axis=-1, keepdims=True)
        ss_hi = jnp.sum(jnp.where(low, 0.0, sq), axis=-1, keepdims=True)
        y = x * lax.rsqrt(jnp.where(low, ss_lo, ss_hi) * (1.0 / HEAD_DIM) + NORM_EPS) * g
        if rope:
            half = HEAD_DIM // 2
            partner = jnp.where(lane % HEAD_DIM < half, pltpu.roll(y, LANES - half, 1), pltpu.roll(y, half, 1))
            y = y * c_ref[...] + partner * s_ref[...]
        return y

    qg = qg_ref[...]
    kg = kg_ref[...]
    for p in range(nq // 2):
        q_ref[0, p] = (norm_pair(z_ref[:, p * LANES:(p + 1) * LANES], qg) * (ATTN_SCALE * LOG2E)).astype(BF16)
    k_lo = nq * HEAD_DIM
    pairs = [norm_pair(z_ref[:, k_lo + p * LANES:k_lo + (p + 1) * LANES], kg) for p in range(nk // 2)]
    for h in range(nq):
        src = k_src(h)
        x = pairs[src // 2]
        if src % 2 != h % 2:
            x = pltpu.roll(x, HEAD_DIM, 1)
        k_ref[0, h] = jnp.where(low if h % 2 == 0 else jnp.logical_not(low), x, 0.0).astype(BF16)
    ones_col = jnp.where(lax.broadcasted_iota(I32, (tm, LANES - HEAD_DIM), 1) == 0, 1.0, 0.0)
    for h in range(nv):
        lo = (nq + nk + h) * HEAD_DIM
        v_ref[0, h] = jnp.concatenate([z_ref[:, lo:lo + HEAD_DIM], ones_col], axis=1).astype(BF16)


def attn_prep(z, qg, kg, cos_t, sin_t, batch, seq, nq, nk, nv, rope, k_src):
    width = (nq + nk + nv) * HEAD_DIM
    tm = min(ROW_TILE, seq)
    nb = seq // tm
    vec = pl.BlockSpec((1, LANES), lambda b, i: (0, 0))
    tab = pl.BlockSpec((tm, LANES), lambda b, i: (i, 0))
    heads = lambda n: pl.BlockSpec((1, n, tm, LANES), lambda b, i: (b, 0, i, 0))
    return pl.pallas_call(
        functools.partial(_prep_body, nq=nq, nk=nk, nv=nv, rope=rope, k_src=k_src),
        grid=(batch, nb),
        in_specs=[pl.BlockSpec((tm, width), lambda b, i: (b * nb + i, 0)), vec, vec, tab, tab],
        out_specs=[heads(nq // 2), heads(nq), heads(nv)],
        out_shape=[_sds((batch, nq // 2, seq, LANES), BF16), _sds((batch, nq, seq, LANES), BF16),
                   _sds((batch, nv, seq, LANES), BF16)],
        compiler_params=_cp("parallel", "parallel"),
        name="attn_prep",
    )(z, jnp.tile(qg, (1, 2)), jnp.tile(kg, (1, 2)), cos_t, sin_t)


def _flash_body(q_ref, k_ref, v_ref, *rest, unit, sub, has_bias):
    if has_bias:
        b_ref, o_ref, m_ref, acc_ref = rest
    else:
        o_ref, m_ref, acc_ref = rest
    i = pl.program_id(2)
    j = pl.program_id(3)
    tq = q_ref.shape[2]
    tk = k_ref.shape[2]

    @pl.when(j == 0)
    def _():
        m_ref[...] = jnp.full_like(m_ref, -jnp.inf)
        acc_ref[...] = jnp.zeros_like(acc_ref)

    k = k_ref[0, 0]
    v = v_ref[0, 0]
    for g in range(tq // sub):
        r0 = g * sub
        s = lax.dot_general(q_ref[0, 0, r0:r0 + sub, :], k, (((1,), (1,)), ((), ())),
                            preferred_element_type=F32)
        if has_bias:
            row_unit = i * (tq // unit) + r0 // unit
            tiles = []
            for c in range(tk // unit):
                sel = jnp.clip(j * (tk // unit) + c - row_unit, -2, 2) + 2
                tiles.append(b_ref[0, sel, r0 % unit:r0 % unit + sub, :])
            s = s + (tiles[0] if len(tiles) == 1 else jnp.concatenate(tiles, axis=1))
        blocks = [s[:, b * LANES:(b + 1) * LANES] for b in range(tk // LANES)]
        bmax = blocks[0]
        for blk in blocks[1:]:
            bmax = jnp.maximum(bmax, blk)
        m_prev = m_ref[r0:r0 + sub, :]
        m_new = jnp.maximum(m_prev, jnp.max(bmax, axis=-1, keepdims=True))
        alpha = jnp.exp2(m_prev - m_new)
        p = jnp.concatenate([jnp.exp2(blk - m_new) for blk in blocks], axis=1).astype(BF16)
        acc_ref[r0:r0 + sub, :] = alpha * acc_ref[r0:r0 + sub, :] + jnp.dot(p, v, preferred_element_type=F32)
        m_ref[r0:r0 + sub, :] = m_new

    @pl.when(j == pl.num_programs(3) - 1)
    def _():
        acc = acc_ref[...]
        o_ref[0, 0] = acc[:, :HEAD_DIM] / acc[:, HEAD_DIM:HEAD_DIM + 1]


def flash_attention(q, k, v, v_of_q, bias=None):
    batch, nq, seq, _ = k.shape
    tq = min(ATTN_Q_TILE, seq)
    tk = min(ATTN_K_TILE, seq)
    unit = min(ATTN_UNIT, seq)
    sub = min(ATTN_SUB if bias is not None else ATTN_SUB_PLAIN, seq)
    in_specs = [pl.BlockSpec((1, 1, tq, LANES), lambda b, h, i, j: (b, h // 2, i, 0)),
                pl.BlockSpec((1, 1, tk, LANES), lambda b, h, i, j: (b, h, j, 0)),
                pl.BlockSpec((1, 1, tk, LANES), lambda b, h, i, j: (b, v_of_q(h), j, 0))]
    args = [q, k, v]
    if bias is not None:
        assert bias.shape[2] == unit
        in_specs.append(pl.BlockSpec((1, 5, unit, unit), lambda b, h, i, j: (v_of_q(h), 0, 0, 0)))
        args.append(bias)
    return pl.pallas_call(
        functools.partial(_flash_body, unit=unit, sub=sub, has_bias=bias is not None),
        grid=(batch, nq, seq // tq, seq // tk),
        in_specs=in_specs,
        out_specs=pl.BlockSpec((1, 1, tq, HEAD_DIM), lambda b, h, i, j: (b, h, i, 0)),
        out_shape=_sds((batch, nq, seq, HEAD_DIM)),
        scratch_shapes=[pltpu.VMEM((tq, LANES), F32), pltpu.VMEM((tq, LANES), F32)],
        compiler_params=_cp("parallel", "parallel", "parallel", "arbitrary"),
        name="flash_bias" if bias is not None else "flash",
    )(*args)


def _t5_bucket_table(t):
    rel = np.arange(-(3 * t - 1), 3 * t)
    half = NUM_BUCKETS // 2
    max_exact = half // 2
    ret = np.where(rel > 0, half, 0)
    n = np.abs(rel)
    nf = np.maximum(n, 1).astype(np.float64)
    large = max_exact + (np.log(nf / max_exact) / math.log(MAX_DISTANCE / max_exact) * (half - max_exact)).astype(np.int64)
    large = np.minimum(large, half - 1)
    return ret + np.where(n < max_exact, n, large)


def diff_bias_tiles(rel_bias, t):
    assert t >= MAX_DISTANCE, "offsets of two or more tiles must lie in the saturated buckets"
    heads = rel_bias.shape[1]
    vec = rel_bias[jnp.asarray(_t5_bucket_table(t), dtype=I32)].T * LOG2E
    tiles = []
    for d in range(-2, 3):
        lo = d * t + 2 * t
        w = vec[:, lo:lo + 2 * t - 1]
        u = jnp.concatenate([w[:, t - 1:], jnp.zeros((heads, 1), F32), w[:, :t - 1]], axis=1)
        skew = jnp.tile(u, (1, t))[:, :t * (2 * t - 1)].reshape(heads, t, 2 * t - 1)
        tiles.append(skew[:, :, :t])
    return jnp.stack(tiles, axis=1)


def rope_tables(seq):
    rows = seq // GRID_W
    row = jnp.repeat(jnp.arange(rows, dtype=F32), GRID_W)
    col = jnp.tile(jnp.arange(GRID_W, dtype=F32), rows)
    freq = ROPE_THETA ** (-jnp.arange(0, ROPE_AXIS_DIM, 2, dtype=F32) / ROPE_AXIS_DIM)
    ang = jnp.concatenate([row[:, None] * freq, col[:, None] * freq], -1)
    cos, sin = jnp.cos(ang), jnp.sin(ang)
    return jnp.concatenate([cos, cos] * 2, -1), jnp.concatenate([-sin, sin] * 2, -1)


def _conv3_body(x0, x1, x2, w0, w1, w2, b0, b1, b2, o0, o1, o2):
    for x_ref, w_ref, b_ref, o_ref in ((x0, w0, b0, o0), (x1, w1, b1, o1), (x2, w2, b2, o2)):
        x = x_ref[0]
        n = x.shape[0]
        t = lax.broadcasted_iota(I32, (n, 1), 0)
        prev = jnp.where(t == 0, 0.0, pltpu.roll(x, 1, 0))
        nxt = jnp.where(t == n - 1, 0.0, pltpu.roll(x, n - 1, 0))
        o_ref[0] = w_ref[0:1] * prev + w_ref[1:2] * x + w_ref[2:3] * nxt + b_ref[...]


def hyena_conv3(z_hy, conv_w, conv_b, batch, seq):
    z = z_hy.reshape(batch, seq, HY_COLS)
    nc = HY_WIDTH // LANES
    x_spec = lambda g: pl.BlockSpec((1, seq, LANES), lambda b, c: (b, 0, g * nc + c))
    w_spec = lambda g: pl.BlockSpec((3, LANES), lambda b, c: (0, g * nc + c))
    b_spec = lambda g: pl.BlockSpec((1, LANES), lambda b, c: (0, g * nc + c))
    out = pl.BlockSpec((1, seq, LANES), lambda b, c: (b, 0, c))
    return pl.pallas_call(
        _conv3_body,
        grid=(batch, nc),
        in_specs=[x_spec(0), x_spec(1), x_spec(2), w_spec(0), w_spec(1), w_spec(2),
                  b_spec(0), b_spec(1), b_spec(2)],
        out_specs=[out, out, out],
        out_shape=[_sds((batch, seq, HY_WIDTH))] * 3,
        compiler_params=_cp("parallel", "parallel"),
        name="hyena_conv3",
    )(z, z, z, conv_w, conv_w, conv_w, conv_b, conv_b, conv_b)


def _filt_body(feat_ref, dist_ref, w1, b1, f1, w2, b2, f2, w3, dec, h_ref, s_ref):
    dot = functools.partial(jnp.dot, precision=HI, preferred_element_type=F32)
    hid = jnp.sin(f1[...] * (dot(feat_ref[...], w1[...]) + b1[...]))
    hid = jnp.sin(f2[...] * (dot(hid, w2[...]) + b2[...]))
    h = dot(hid, w3[...]) * jnp.exp(-dist_ref[...] * jnp.abs(dec[...]))
    h_ref[...] = h

    @pl.when(pl.program_id(0) == 0)
    def _():
        s_ref[...] = jnp.zeros_like(s_ref)

    s_ref[...] += jnp.sum(jnp.abs(h), axis=0, keepdims=True)


def hyena_filter(seq, w1, b1, f1, w2, b2, f2, w3, decay):
    j = jnp.arange(seq, dtype=F32)
    tt = j / seq
    ang = 2.0 * jnp.pi * tt[:, None] * jnp.arange(1, HY_BANDS + 1, dtype=F32)
    feat = jnp.concatenate([tt[:, None], jnp.cos(ang), jnp.sin(ang)], -1)
    feat = jnp.pad(feat, ((0, 0), (0, LANES - HY_EMB)))
    half = seq // 2
    dist = (jnp.abs(j - half) / half)[:, None]
    w1p = jnp.pad(w1, ((0, LANES - HY_EMB), (0, 0)))
    tm = min(1024, seq)
    hw = HY_ORDER * HY_WIDTH
    const = lambda i: (0, 0)
    return pl.pallas_call(
        _filt_body,
        grid=(seq // tm,),
        in_specs=[pl.BlockSpec((tm, LANES), lambda i: (i, 0)), pl.BlockSpec((tm, 1), lambda i: (i, 0)),
                  pl.BlockSpec((LANES, HY_FFN), const), pl.BlockSpec((1, HY_FFN), const),
                  pl.BlockSpec((1, HY_FFN), const), pl.BlockSpec((HY_FFN, HY_FFN), const),
                  pl.BlockSpec((1, HY_FFN), const), pl.BlockSpec((1, HY_FFN), const),
                  pl.BlockSpec((HY_FFN, hw), const), pl.BlockSpec((1, hw), const)],
        out_specs=[pl.BlockSpec((tm, hw), lambda i: (i, 0)), pl.BlockSpec((1, hw), const)],
        out_shape=[_sds((seq, hw)), _sds((1, hw))],
        compiler_params=_cp("arbitrary"),
        name="hyena_filter",
    )(feat, dist, w1p, b1[None], f1[None], w2, b2[None], f2[None], w3, decay.reshape(1, hw))


def dft_tables(seq):
    n = 2 * seq
    n2 = FFT_N2
    n1 = n // n2
    n1h = n1 // 2
    nk = -(-(n1h + 1) // SUBLANES) * SUBLANES
    k1 = jnp.arange(nk, dtype=I32)
    live = (k1 <= n1h).astype(F32)
    ang = (2.0 * jnp.pi / n1) * ((k1[:, None] * jnp.arange(n1h, dtype=I32)[None, :]) % n1).astype(F32)
    outer_fwd = jnp.concatenate([jnp.cos(ang), -jnp.sin(ang)], axis=0) * jnp.tile(live, 2)[:, None]
    t1 = jnp.arange(n1h, dtype=I32) + n1 // 4
    ang = (2.0 * jnp.pi / n1) * ((t1[:, None] * k1[None, :]) % n1).astype(F32)
    weight = live * jnp.where((k1 == 0) | (k1 == n1h), 1.0, 2.0) / n
    outer_inv = jnp.concatenate([jnp.cos(ang), -jnp.sin(ang)], axis=1) * jnp.tile(weight, 2)[None, :]
    k2 = jnp.arange(n2, dtype=I32)
    t2 = jnp.arange(n2, dtype=I32)
    phase = (t2[None, None, :] * (k2[None, :, None] * n1 + k1[:, None, None])) % n
    ang = (2.0 * jnp.pi / n) * phase.astype(F32)
    fr, fi = jnp.cos(ang), -jnp.sin(ang)
    inner = jnp.concatenate([jnp.concatenate([fr, -fi], axis=2),
                             jnp.concatenate([fi, fr], axis=2)], axis=1)
    return _split_bf16(outer_fwd), _split_bf16(outer_inv), _split_bf16(inner), _split_bf16(jnp.swapaxes(inner, 1, 2))


def _split_bf16(x):
    hi = x.astype(BF16)
    return hi, (x - hi.astype(F32)).astype(BF16)


def _dot_split(w, a):
    w_hi, w_lo = w
    a_hi, a_lo = _split_bf16(a)
    dot = functools.partial(jnp.dot, preferred_element_type=F32)
    return dot(w_hi, a_hi) + (dot(w_hi, a_lo) + dot(w_lo, a_hi))


def _dot_split_rhs(a, w):
    w_hi, w_lo = w
    a_hi, a_lo = _split_bf16(a)
    dot = functools.partial(jnp.dot, preferred_element_type=F32)
    return dot(a_hi, w_hi) + (dot(a_lo, w_hi) + dot(a_hi, w_lo))


def _every(ref, s, n):
    return ref[pl.ds(s, n, stride=FFT_ROWS), :]


def _fft_outer_body(f_ref, fl_ref, x_ref, o_ref):
    n1h = x_ref.shape[0]
    rows = f_ref.shape[0]
    x2 = x_ref.reshape(n1h * FFT_ROWS, LANES)
    o2 = o_ref.reshape(rows * FFT_ROWS, LANES)
    x = jnp.concatenate([_every(x2, s, n1h) for s in range(FFT_ROWS)], axis=1)
    r = _dot_split((f_ref[...], fl_ref[...]), x)
    for s in range(FFT_ROWS):
        o2[pl.ds(s, rows, stride=FFT_ROWS), :] = r[:, s * LANES:(s + 1) * LANES]


def fft_outer(x4, table):
    batch, n1h, n2, ch = x4.shape
    nk = table[0].shape[0] // 2
    tab = pl.BlockSpec((2 * nk, n1h), lambda b, c, l: (0, 0))
    return pl.pallas_call(
        _fft_outer_body,
        grid=(batch, n2 // FFT_ROWS, ch // LANES),
        in_specs=[tab, tab, pl.BlockSpec((None, n1h, FFT_ROWS, LANES), lambda b, c, l: (b, 0, c, l))],
        out_specs=pl.BlockSpec((None, 2, nk, FFT_ROWS, LANES), lambda b, c, l: (b, 0, 0, c, l)),
        out_shape=_sds((batch, 2, nk, n2, ch)),
        compiler_params=_cp("parallel", "parallel", "parallel"),
        name="fft_outer",
    )(*table, x4)


def _fft_filter_body(mh_ref, ml_ref, s_ref, a_ref, o_ref):
    n2 = FFT_N2
    a = a_ref[0, :, 0].reshape(2 * n2, HY_WIDTH)
    x = _dot_split((mh_ref[0], ml_ref[0]), a) * (1.0 / s_ref[...])
    o_ref[:, 0] = x.reshape(2, n2, HY_WIDTH)


def fft_filter(a5, inner, s):
    _, _, n1, n2, ch = a5.shape
    return pl.pallas_call(
        _fft_filter_body,
        grid=(n1, ch // HY_WIDTH),
        in_specs=[pl.BlockSpec((1, 2 * n2, 2 * n2), lambda k, c: (k, 0, 0)),
                  pl.BlockSpec((1, 2 * n2, 2 * n2), lambda k, c: (k, 0, 0)),
                  pl.BlockSpec((1, HY_WIDTH), lambda k, c: (0, c)),
                  pl.BlockSpec((1, 2, 1, n2, HY_WIDTH), lambda k, c: (0, 0, k, 0, c))],
        out_specs=pl.BlockSpec((2, 1, n2, HY_WIDTH), lambda k, c: (0, k, 0, c)),
        out_shape=_sds((2, n1, n2, ch)),
        compiler_params=_cp("parallel", "parallel"),
        name="fft_filter",
    )(*inner, s, a5)


def _fft_mid_body(mh_ref, ml_ref, th_ref, tl_ref, h_ref, a_ref, o_ref, *, nb):
    n2 = FFT_N2
    m = (mh_ref[0], ml_ref[0])
    mt = (th_ref[0], tl_ref[0])
    hr = h_ref[0, 0]
    hi = h_ref[1, 0]
    for b in range(nb):
        x = _dot_split(m, a_ref[b, :, 0].reshape(2 * n2, HY_WIDTH))
        xr, xi = x[:n2], x[n2:]
        y = jnp.concatenate([xr * hr - xi * hi, xr * hi + xi * hr], axis=0)
        o_ref[b, :, 0] = _dot_split(mt, y).reshape(2, n2, HY_WIDTH)


def fft_mid(a5, inner, inner_t, hf, order):
    nb, _, n1, n2, ch = a5.shape
    mat = pl.BlockSpec((1, 2 * n2, 2 * n2), lambda k: (k, 0, 0))
    blk = pl.BlockSpec((nb, 2, 1, n2, ch), lambda k: (0, 0, k, 0, 0))
    return pl.pallas_call(
        functools.partial(_fft_mid_body, nb=nb),
        grid=(n1,),
        in_specs=[mat, mat, mat, mat, pl.BlockSpec((2, 1, n2, ch), lambda k: (0, k, 0, order)), blk],
        out_specs=blk,
        out_shape=_sds(a5.shape),
        compiler_params=_cp("parallel"),
        name="fft_mid",
    )(*inner, *inner_t, hf, a5)


def _fft_inv_body(g_ref, gl_ref, b_ref, gate_ref, y_ref, fb_ref, o_ref):
    n1h, rows = g_ref.shape
    b2 = b_ref.reshape(rows * FFT_ROWS, LANES)
    gate2 = gate_ref.reshape(n1h * FFT_ROWS, LANES)
    y2 = y_ref.reshape(n1h * FFT_ROWS, LANES)
    o2 = o_ref.reshape(n1h * FFT_ROWS, LANES)
    bm = jnp.concatenate([_every(b2, s, rows) for s in range(FFT_ROWS)], axis=1)
    c = _dot_split((g_ref[...], gl_ref[...]), bm)
    fb = fb_ref[...]
    for s in range(FFT_ROWS):
        conv = c[:, s * LANES:(s + 1) * LANES]
        o2[pl.ds(s, n1h, stride=FFT_ROWS), :] = _every(gate2, s, n1h) * (conv + fb * _every(y2, s, n1h))


def fft_inv_gate(b5, table, gate4, y4, fbias):
    batch, _, nk, n2, ch = b5.shape
    n1h = table[0].shape[0]
    blk = pl.BlockSpec((None, n1h, FFT_ROWS, LANES), lambda b, c, l: (b, 0, c, l))
    tab = pl.BlockSpec((n1h, 2 * nk), lambda b, c, l: (0, 0))
    return pl.pallas_call(
        _fft_inv_body,
        grid=(batch, n2 // FFT_ROWS, ch // LANES),
        in_specs=[tab, tab,
                  pl.BlockSpec((None, 2, nk, FFT_ROWS, LANES), lambda b, c, l: (b, 0, 0, c, l)), blk, blk,
                  pl.BlockSpec((1, LANES), lambda b, c, l: (0, l))],
        out_specs=blk,
        out_shape=_sds((batch, n1h, n2, ch)),
        compiler_params=_cp("parallel", "parallel", "parallel"),
        name="fft_inv_gate",
    )(*table, b5, gate4, y4, fbias.reshape(1, ch))


def hyena_spectrum(seq, tabs, w1, b1, f1, w2, b2, f2, w3, decay):
    outer_fwd, _, inner, _ = tabs
    n1h = outer_fwd[0].shape[1]
    hw = HY_ORDER * HY_WIDTH
    h, s = hyena_filter(seq, w1, b1, f1, w2, b2, f2, w3, decay)
    return fft_filter(fft_outer(h.reshape(1, n1h, FFT_N2, hw), outer_fwd), inner, s)


def hyena_mixer(z_hy, conv_w, conv_b, hf, f_bias, tabs, batch, seq):
    outer_fwd, outer_inv, inner, inner_t = tabs
    n1h = outer_fwd[0].shape[1]
    shape4 = (batch, n1h, FFT_N2, HY_WIDTH)
    v, g0, g1 = hyena_conv3(z_hy, conv_w, conv_b, batch, seq)
    y = v.reshape(shape4)
    for order, gate in enumerate((g0, g1)):
        bm = fft_mid(fft_outer(y, outer_fwd), inner, inner_t, hf, order)
        y = fft_inv_gate(bm, outer_inv, gate.reshape(shape4), y, f_bias[order])
    return y.reshape(batch * seq, HY_WIDTH)


def _mix_body(lam_ref, h_ref, s5_ref, gqa_ref, diff_ref, hy_ref, gs5, ggqa, gsub, ghy, w_ref, o_ref, *, keep):
    lam = lam_ref[0]
    a = _rms(s5_ref[...], gs5[...])
    b = _rms(jnp.concatenate([gqa_ref[0, h] for h in range(GQA_HEADS)], axis=1), ggqa[...])
    c = jnp.concatenate(
        [_rms(diff_ref[0, 2 * h] - lam * diff_ref[0, 2 * h + 1], gsub[...]) * keep for h in range(DIFF_HEADS)],
        axis=1)
    d = _rms(hy_ref[...], ghy[...])
    mixed = jnp.concatenate([a, b, c, d], axis=1).astype(BF16)
    o_ref[...] = h_ref[...] + jnp.dot(mixed, w_ref[...], preferred_element_type=F32)


def mix_out(lam, h, y_s5, o_gqa, o_diff, y_hy, gs5, ggqa, gsub, ghy, w_out, batch, seq, lambda_init):
    tm = min(ROW_TILE, seq)
    nb = seq // tm
    row = lambda w: pl.BlockSpec((tm, w), lambda b, i: (b * nb + i, 0))
    vec = lambda w: pl.BlockSpec((1, w), lambda b, i: (0, 0))
    heads = lambda n: pl.BlockSpec((1, n, tm, HEAD_DIM), lambda b, i: (b, 0, i, 0))
    time_major = pl.BlockSpec((tm, GROUP_WIDTH), lambda b, i: (i, b))
    return pl.pallas_call(
        functools.partial(_mix_body, keep=1.0 - lambda_init),
        grid=(batch, nb),
        in_specs=[pl.BlockSpec(memory_space=pltpu.SMEM), row(D_MODEL), time_major, heads(GQA_HEADS),
                  heads(2 * DIFF_HEADS), row(GROUP_WIDTH), vec(GROUP_WIDTH), vec(GROUP_WIDTH), vec(HEAD_DIM),
                  vec(GROUP_WIDTH), pl.BlockSpec((D_MODEL, D_MODEL), lambda b, i: (0, 0))],
        out_specs=row(D_MODEL),
        out_shape=_sds((batch * seq, D_MODEL)),
        compiler_params=_cp("parallel", "parallel"),
        name="mix_out",
    )(lam, h, y_s5, o_gqa, o_diff, y_hy, gs5, ggqa, gsub, ghy, w_out)


def _router_body(h_ref, g_ref, w_ref, wl_ref, b_ref, xn_ref, ti_ref, tw_ref, cnt_ref):
    xn = _rms(h_ref[...], g_ref[...])
    xn_ref[...] = xn
    logits = _dot_split_rhs(xn, (w_ref[...], wl_ref[...])) + b_ref[...]
    lane = lax.broadcasted_iota(I32, logits.shape, 1)
    vals = logits
    tops, idxs = [], []
    hot = jnp.zeros(logits.shape, F32)
    for _ in range(TOP_K):
        m = jnp.max(vals, axis=-1, keepdims=True)
        idx = jnp.min(jnp.where(vals == m, lane, N_EXPERTS), axis=-1, keepdims=True)
        sel = lane == idx
        tops.append(m)
        idxs.append(idx)
        hot = hot + sel.astype(F32)
        vals = jnp.where(sel, -jnp.inf, vals)
    es = [jnp.exp(t - tops[0]) for t in tops]
    den = es[0] + es[1] + es[2] + es[3]
    ti_ref[...] = jnp.concatenate(idxs, axis=1)
    tw_ref[...] = jnp.concatenate([e / den for e in es], axis=1)

    @pl.when(pl.program_id(0) == 0)
    def _():
        cnt_ref[...] = jnp.zeros_like(cnt_ref)

    cnt_ref[...] += jnp.sum(hot, axis=0, keepdims=True)


def moe_router(h, g, w_router, b_router):
    t = h.shape[0]
    tm = min(ROW_TILE, t)
    const = lambda i: (0, 0)
    return pl.pallas_call(
        _router_body,
        grid=(t // tm,),
        in_specs=[pl.BlockSpec((tm, D_MODEL), lambda i: (i, 0)), pl.BlockSpec((1, D_MODEL), const),
                  pl.BlockSpec((D_MODEL, N_EXPERTS), const), pl.BlockSpec((D_MODEL, N_EXPERTS), const),
                  pl.BlockSpec((1, N_EXPERTS), const)],
        out_specs=[pl.BlockSpec((tm, D_MODEL), lambda i: (i, 0)), pl.BlockSpec((tm, TOP_K), lambda i: (i, 0)),
                   pl.BlockSpec((tm, TOP_K), lambda i: (i, 0)), pl.BlockSpec((1, N_EXPERTS), const)],
        out_shape=[_sds((t, D_MODEL)), _sds((t, TOP_K), I32), _sds((t, TOP_K)), _sds((1, N_EXPERTS))],
        compiler_params=_cp("arbitrary"),
        name="moe_router",
    )(h, g, *w_router, b_router)


def _rank_body(ti_ref, off_ref, pos_ref, carry_ref):
    @pl.when(pl.program_id(0) == 0)
    def _():
        carry_ref[...] = jnp.zeros_like(carry_ref)

    ti = ti_ref[...]
    tm = ti.shape[0]
    lane = lax.broadcasted_iota(I32, (tm, N_EXPERTS), 1)
    hots = [lane == ti[:, k:k + 1] for k in range(TOP_K)]
    hot = sum(h.astype(F32) for h in hots)
    r = lax.broadcasted_iota(I32, (tm, tm), 0)
    c = lax.broadcasted_iota(I32, (tm, tm), 1)
    below = jnp.where(r > c, 1.0, 0.0).astype(BF16)
    before = jnp.dot(below, hot.astype(BF16), preferred_element_type=F32) + carry_ref[...] + off_ref[...]
    pos = [jnp.sum(jnp.where(h, before, 0.0), axis=-1, keepdims=True) for h in hots]
    pos_ref[...] = jnp.concatenate(pos, axis=1).astype(I32)
    carry_ref[...] += jnp.sum(hot, axis=0, keepdims=True)


def moe_rank(top_i, offsets):
    t = top_i.shape[0]
    tm = min(ROW_TILE, t)
    return pl.pallas_call(
        _rank_body,
        grid=(t // tm,),
        in_specs=[pl.BlockSpec((tm, TOP_K), lambda i: (i, 0)), pl.BlockSpec((1, N_EXPERTS), lambda i: (0, 0))],
        out_specs=pl.BlockSpec((tm, TOP_K), lambda i: (i, 0)),
        out_shape=_sds((t, TOP_K), I32),
        scratch_shapes=[pltpu.VMEM((1, N_EXPERTS), F32)],
        compiler_params=_cp("arbitrary"),
        name="moe_rank",
    )(top_i, offsets)


def _row_copy(src, dst, sem):
    return pltpu.make_async_copy(src, dst, sem)


def _pad_zero_body(lt_ref, o_ref):
    del lt_ref
    o_ref[...] = jnp.zeros_like(o_ref)


def moe_pad_tiles(last_tile, n_slots):
    rows = MOE_TILE * ROW_TILES
    return pl.pallas_call(
        _pad_zero_body,
        grid_spec=pltpu.PrefetchScalarGridSpec(
            num_scalar_prefetch=1, grid=(N_EXPERTS,), in_specs=[],
            out_specs=pl.BlockSpec((rows, LANES), lambda e, lt: (lt[e], 0))),
        out_shape=_sds((n_slots * ROW_TILES, LANES)),
        compiler_params=_cp("arbitrary"),
        name="moe_pad_tiles",
    )(last_tile)


def _slot_rows(ref, slot):
    return ref.at[pl.ds(pl.multiple_of(slot * ROW_TILES, ROW_TILES), ROW_TILES)]


def _scatter_body(pos_ref, x_ref, padded_hbm, xs_hbm, buf, sem, *, tm):
    del padded_hbm
    x = x_ref[...]
    for s in range(ROW_TILES):
        buf[pl.ds(s, tm, stride=ROW_TILES), :] = x[:, s * LANES:(s + 1) * LANES]

    def start(r, c):
        for k in range(TOP_K):
            _row_copy(_slot_rows(buf, r), _slot_rows(xs_hbm, pos_ref[r * TOP_K + k]), sem).start()
        return c

    lax.fori_loop(0, tm, start, 0)
    for _ in range(TOP_K):
        _row_copy(buf, xs_hbm.at[pl.ds(0, tm * ROW_TILES)], sem).wait()


def moe_scatter(pos_flat, x, padded):
    t = x.shape[0]
    tm = min(MOE_TOK_TILE, t)
    return pl.pallas_call(
        functools.partial(_scatter_body, tm=tm),
        grid=(t // tm,),
        in_specs=[pl.BlockSpec((tm * TOP_K,), lambda i: (i,), memory_space=pltpu.SMEM),
                  pl.BlockSpec((tm, D_MODEL), lambda i: (i, 0)), pl.BlockSpec(memory_space=pl.ANY)],
        out_specs=pl.BlockSpec(memory_space=pl.ANY),
        out_shape=_sds(padded.shape),
        scratch_shapes=[pltpu.VMEM((tm * ROW_TILES, LANES), F32), pltpu.SemaphoreType.DMA(())],
        input_output_aliases={2: 0},
        compiler_params=_cp("arbitrary"),
        name="moe_scatter",
    )(pos_flat, x, padded)


def _experts_body(te_ref, nv_ref, x_ref, wgu_ref, bgu_ref, wd_ref, bd_ref, o_ref):
    del te_ref
    j = pl.program_id(0)
    dff = wd_ref.shape[1]

    @pl.when(j < nv_ref[0])
    def _():
        tm = x_ref.shape[0] // ROW_TILES
        x = jnp.concatenate([x_ref[pl.ds(s, tm, stride=ROW_TILES), :] for s in range(ROW_TILES)],
                            axis=1).astype(BF16)
        hgu = jnp.dot(x, wgu_ref[0], preferred_element_type=F32) + bgu_ref[0]
        glu = jnp.minimum(hgu[:, :dff], SWIGLU_LIMIT)
        lin = jnp.clip(hgu[:, dff:], -SWIGLU_LIMIT, SWIGLU_LIMIT)
        act = glu * jax.nn.sigmoid(SWIGLU_ALPHA * glu) * (lin + 1.0)
        y = jnp.dot(act.astype(BF16), wd_ref[0], preferred_element_type=F32) + bd_ref[0]
        for s in range(ROW_TILES):
            o_ref[pl.ds(s, tm, stride=ROW_TILES), :] = y[:, s * LANES:(s + 1) * LANES]

    @pl.when(j >= nv_ref[0])
    def _():
        o_ref[...] = jnp.zeros_like(o_ref)


def moe_experts(tile_expert, n_valid, xs, w_gu, b_gu, w_down, b_down):
    n_slots = xs.shape[0] // ROW_TILES
    tm = MOE_TILE
    dff = w_down.shape[1]
    blk = pl.BlockSpec((tm * ROW_TILES, LANES), lambda j, te, nv: (j, 0))
    return pl.pallas_call(
        _experts_body,
        grid_spec=pltpu.PrefetchScalarGridSpec(
            num_scalar_prefetch=2,
            grid=(n_slots // tm,),
            in_specs=[pl.BlockSpec((tm * ROW_TILES, LANES), lambda j, te, nv: (jnp.minimum(j, nv[0] - 1), 0)),
                      pl.BlockSpec((1, D_MODEL, 2 * dff), lambda j, te, nv: (te[j], 0, 0)),
                      pl.BlockSpec((1, 1, 2 * dff), lambda j, te, nv: (te[j], 0, 0)),
                      pl.BlockSpec((1, dff, D_MODEL), lambda j, te, nv: (te[j], 0, 0)),
                      pl.BlockSpec((1, 1, D_MODEL), lambda j, te, nv: (te[j], 0, 0))],
            out_specs=blk),
        out_shape=_sds((n_slots * ROW_TILES, LANES)),
        compiler_params=_cp("arbitrary"),
        name="moe_experts",
    )(tile_expert, n_valid, xs, w_gu, b_gu, w_down, b_down)


def _combine_body(pos_ref, w_ref, ys_hbm, o_ref, b0, b1, b2, b3, mixed, sem, *, tm):
    bufs = (b0, b1, b2, b3)

    def start(r, c):
        for k in range(TOP_K):
            _row_copy(_slot_rows(ys_hbm, pos_ref[r * TOP_K + k]), _slot_rows(bufs[k], r), sem).start()
        return c

    lax.fori_loop(0, tm, start, 0)
    for k in range(TOP_K):
        _row_copy(ys_hbm.at[pl.ds(0, tm * ROW_TILES)], bufs[k], sem).wait()

    def mix(q, c):
        for u in range(MIX_UNROLL):
            r = q * MIX_UNROLL + u
            rows = pl.ds(pl.multiple_of(r * ROW_TILES, ROW_TILES), ROW_TILES)
            acc = w_ref[r * TOP_K] * b0[rows, :]
            for k in range(1, TOP_K):
                acc = acc + w_ref[r * TOP_K + k] * bufs[k][rows, :]
            mixed[rows, :] = acc
        return c

    lax.fori_loop(0, tm // MIX_UNROLL, mix, 0)
    o_ref[...] = jnp.concatenate([mixed[pl.ds(s, tm, stride=ROW_TILES), :] for s in range(ROW_TILES)], axis=1)


def moe_combine(pos_flat, w_flat, ys, t):
    tm = min(MOE_TOK_TILE, t)
    smem = pl.BlockSpec((tm * TOP_K,), lambda i: (i,), memory_space=pltpu.SMEM)
    tile_buf = pltpu.VMEM((tm * ROW_TILES, LANES), F32)
    return pl.pallas_call(
        functools.partial(_combine_body, tm=tm),
        grid=(t // tm,),
        in_specs=[smem, smem, pl.BlockSpec(memory_space=pl.ANY)],
        out_specs=pl.BlockSpec((tm, D_MODEL), lambda i: (i, 0)),
        out_shape=_sds((t, D_MODEL)),
        scratch_shapes=[tile_buf] * (TOP_K + 1) + [pltpu.SemaphoreType.DMA(())],
        compiler_params=_cp("arbitrary"),
        name="moe_combine",
    )(pos_flat, w_flat, ys)


def moe(h, g, w_router, b_router, w_gu, b_gu, w_down, b_down):
    t = h.shape[0]
    xn, top_i, top_w, counts = moe_router(h, g, w_router, b_router)
    n_tiles = (t * TOP_K) // MOE_TILE + N_EXPERTS
    padded = jnp.ceil(counts[0] / MOE_TILE) * MOE_TILE
    ends = jnp.cumsum(padded)
    offsets = (ends - padded)[None]
    tile_start = jnp.arange(n_tiles, dtype=F32) * MOE_TILE
    tile_expert = jnp.minimum(jnp.sum(ends[None, :] <= tile_start[:, None], axis=1), N_EXPERTS - 1).astype(I32)
    n_valid = (ends[-1:] / MOE_TILE).astype(I32)
    own_last = jnp.where(padded > 0, ends / MOE_TILE - 1, -1.0)
    filled = lax.cummax(own_last, axis=0)
    first = jnp.min(jnp.where(padded > 0, own_last, jnp.inf))
    last_tile = jnp.where(filled < 0, first, filled).astype(I32)
    pos = moe_rank(top_i, offsets).reshape(t * TOP_K)
    xs = moe_scatter(pos, xn, moe_pad_tiles(last_tile, n_tiles * MOE_TILE))
    ys = moe_experts(tile_expert, n_valid, xs, w_gu, b_gu, w_down, b_down)
    return moe_combine(pos, top_w.reshape(t * TOP_K), ys, t)


def _ple_body(h_ref, m_ref, p_ref, wp_ref, wg_ref, o_ref):
    h = h_ref[...] + m_ref[...]
    e = jnp.dot(p_ref[...].astype(BF16), wp_ref[...], preferred_element_type=F32)
    gate = jnp.dot(_rms(h).astype(BF16), wg_ref[...], preferred_element_type=F32)
    o_ref[...] = h + e * jax.nn.sigmoid(gate)


def ple(h, moe_out, p, w_ple, w_gate):
    t = h.shape[0]
    tm = min(ROW_TILE, t)
    row = pl.BlockSpec((tm, D_MODEL), lambda i: (i, 0))
    return pl.pallas_call(
        _ple_body,
        grid=(t // tm,),
        in_specs=[row, row, pl.BlockSpec((tm, PLE_DIM), lambda i: (i, 0)),
                  pl.BlockSpec((PLE_DIM, D_MODEL), lambda i: (0, 0)),
                  pl.BlockSpec((D_MODEL, D_MODEL), lambda i: (0, 0))],
        out_specs=row,
        out_shape=_sds((t, D_MODEL)),
        compiler_params=_cp("parallel"),
        name="ple",
    )(h, moe_out, p, w_ple, w_gate)


def _deinterleave(n):
    return np.concatenate([np.arange(0, n, 2), np.arange(1, n, 2)])


def _halves(w):
    pairs = w.reshape(*w.shape[:-1], w.shape[-1] // 2, 2)
    return jnp.concatenate([pairs[..., 0], pairs[..., 1]], axis=-1)


def prepare_layer(i, prm):
    pair = _deinterleave(HEAD_DIM)
    cols = np.arange(IN_COLS)
    for h in range(GQA_HEADS + GQA_KV_HEADS):
        lo = OFF_GQA + h * HEAD_DIM
        cols[lo:lo + HEAD_DIM] = lo + pair
    lam = (jnp.exp(jnp.sum(prm["diff_lam_q1"][i] * prm["diff_lam_k1"][i]))
           - jnp.exp(jnp.sum(prm["diff_lam_q2"][i] * prm["diff_lam_k2"][i])) + _lambda_init(i))
    return dict(
        norm1_g=prm["norm1_g"][i][None],
        w_in=prm["w_in"][i][:, cols].astype(BF16),
        s5=s5_tables(prm["s5_lam_re"][i], prm["s5_lam_im"][i], prm["s5_log_dt"][i], prm["s5_b_re"][i],
                     prm["s5_b_im"][i], prm["s5_c_re"][i], prm["s5_c_im"][i]),
        s5_d=prm["s5_d"][i][None],
        s5_w_glu=prm["s5_w_glu"][i].astype(BF16),
        s5_b_glu=prm["s5_b_glu"][i][None],
        gqa_q_g=prm["gqa_q_g"][i][pair][None],
        gqa_k_g=prm["gqa_k_g"][i][pair][None],
        diff_q_g=prm["diff_q_g"][i][None],
        diff_k_g=prm["diff_k_g"][i][None],
        diff_lam=lam.reshape(1).astype(F32),
        diff_subln_g=prm["diff_subln_g"][i][None],
        out_g_s5=prm["out_g_s5"][i][None],
        out_g_gqa=prm["out_g_gqa"][i][None],
        out_g_hy=prm["out_g_hy"][i][None],
        w_out=prm["w_out"][i].astype(BF16),
        norm2_g=prm["norm2_g"][i][None],
        w_router=_split_bf16(prm["w_router"][i]),
        b_router=prm["b_router"][i][None],
        w_gu=_halves(prm["w_gu"][i]).astype(BF16),
        b_gu=_halves(prm["b_gu"][i])[:, None, :],
        w_down=prm["w_down"][i].astype(BF16),
        b_down=prm["b_down"][i][:, None, :],
        w_ple=prm["w_ple"][i].astype(BF16),
        w_ple_gate=prm["w_ple_gate"][i].astype(BF16),
    )


def _lambda_init(i):
    return 0.8 - 0.6 * math.exp(-0.3 * i)


def run_trunk(x, p, prm, layers):
    batch, seq, _ = x.shape
    t = batch * seq
    cos_t, sin_t = rope_tables(seq)
    tabs = dft_tables(seq)
    bias = diff_bias_tiles(prm["rel_bias"], min(ATTN_UNIT, seq))
    h = x.reshape(t, D_MODEL)
    for i, lw in enumerate(layers):
        z_s5, z_gqa, z_diff, z_hy = in_proj(h, lw["norm1_g"], lw["w_in"], batch, seq)
        y_s5 = s5_mixer(z_s5, lw["s5"], lw["s5_d"], lw["s5_w_glu"], lw["s5_b_glu"], batch, seq)
        rep = GQA_HEADS // GQA_KV_HEADS
        q, k, v = attn_prep(z_gqa, lw["gqa_q_g"], lw["gqa_k_g"], cos_t, sin_t, batch, seq,
                            GQA_HEADS, GQA_KV_HEADS, GQA_KV_HEADS, True, lambda hq: hq // rep)
        o_gqa = flash_attention(q, k, v, lambda hq: hq // rep)
        q, k, v = attn_prep(z_diff, lw["diff_q_g"], lw["diff_k_g"], cos_t, sin_t, batch, seq,
                            2 * DIFF_HEADS, 2 * DIFF_HEADS, DIFF_HEADS, False, lambda hq: hq)
        o_diff = flash_attention(q, k, v, lambda hq: hq // 2, bias)
        hf = hyena_spectrum(seq, tabs, prm["hy_w1"][i], prm["hy_b1"][i], prm["hy_freq1"][i], prm["hy_w2"][i],
                            prm["hy_b2"][i], prm["hy_freq2"][i], prm["hy_w3"][i], prm["hy_decay"][i])
        y_hy = hyena_mixer(z_hy, prm["hy_conv_w"][i], prm["hy_conv_b"][i][None], hf, prm["hy_bias"][i],
                           tabs, batch, seq)
        h = mix_out(lw["diff_lam"], h, y_s5, o_gqa, o_diff, y_hy, lw["out_g_s5"], lw["out_g_gqa"],
                    lw["diff_subln_g"], lw["out_g_hy"], lw["w_out"], batch, seq, _lambda_init(i))
        m = moe(h, lw["norm2_g"], lw["w_router"], lw["b_router"], lw["w_gu"], lw["b_gu"], lw["w_down"],
                lw["b_down"])
        h = ple(h, m, p[i].reshape(t, PLE_DIM), lw["w_ple"], lw["w_ple_gate"])
    return h.reshape(batch, seq, D_MODEL)


def kernel(x_prompt, x_sample, p_prompt, p_sample, rel_bias, norm1_g, w_in, s5_lam_re, s5_lam_im, s5_log_dt,
           s5_b_re, s5_b_im, s5_c_re, s5_c_im, s5_d, s5_w_glu, s5_b_glu, gqa_q_g, gqa_k_g, diff_q_g, diff_k_g,
           diff_lam_q1, diff_lam_k1, diff_lam_q2, diff_lam_k2, diff_subln_g, hy_conv_w, hy_conv_b, hy_w1, hy_b1,
           hy_freq1, hy_w2, hy_b2, hy_freq2, hy_w3, hy_decay, hy_bias, out_g_s5, out_g_gqa, out_g_hy, w_out,
           norm2_g, w_router, b_router, w_gu, b_gu, w_down, b_down, w_ple, w_ple_gate):
    prm = dict(rel_bias=rel_bias, norm1_g=norm1_g, w_in=w_in, s5_lam_re=s5_lam_re, s5_lam_im=s5_lam_im,
               s5_log_dt=s5_log_dt, s5_b_re=s5_b_re, s5_b_im=s5_b_im, s5_c_re=s5_c_re, s5_c_im=s5_c_im,
               s5_d=s5_d, s5_w_glu=s5_w_glu, s5_b_glu=s5_b_glu, gqa_q_g=gqa_q_g, gqa_k_g=gqa_k_g,
               diff_q_g=diff_q_g, diff_k_g=diff_k_g, diff_lam_q1=diff_lam_q1, diff_lam_k1=diff_lam_k1,
               diff_lam_q2=diff_lam_q2, diff_lam_k2=diff_lam_k2, diff_subln_g=diff_subln_g,
               hy_conv_w=hy_conv_w, hy_conv_b=hy_conv_b, hy_w1=hy_w1, hy_b1=hy_b1, hy_freq1=hy_freq1,
               hy_w2=hy_w2, hy_b2=hy_b2, hy_freq2=hy_freq2, hy_w3=hy_w3, hy_decay=hy_decay, hy_bias=hy_bias,
               out_g_s5=out_g_s5, out_g_gqa=out_g_gqa, out_g_hy=out_g_hy, w_out=w_out, norm2_g=norm2_g,
               w_router=w_router, b_router=b_router, w_gu=w_gu, b_gu=b_gu, w_down=w_down, b_down=b_down,
               w_ple=w_ple, w_ple_gate=w_ple_gate)
    layers = [prepare_layer(i, prm) for i in range(DEPTH)]
    return (run_trunk(x_prompt, p_prompt, prm, layers), run_trunk(x_sample, p_sample, prm, layers))
```

```python
import functools
import math

import numpy as np
import jax
import jax.numpy as jnp
from jax import lax
from jax.experimental import pallas as pl
from jax.experimental.pallas import tpu as pltpu

F32 = jnp.float32
BF16 = jnp.bfloat16
I32 = jnp.int32
HI = lax.Precision.HIGHEST

D_MODEL = 1024
DEPTH = 2
GROUP_WIDTH = 256
HEAD_DIM = 64
S5_GROUP_DIM = 16
S5_GROUPS = 16
S5_STATE = 64
S5_NSTATE = S5_GROUPS * S5_STATE
GQA_HEADS = 4
GQA_KV_HEADS = 2
DIFF_HEADS = 4
HY_WIDTH = 256
HY_ORDER = 2
HY_BANDS = 16
HY_EMB = 2 * HY_BANDS + 1
HY_FFN = 64
N_EXPERTS = 32
TOP_K = 4
SWIGLU_LIMIT = 7.0
SWIGLU_ALPHA = 1.702
NUM_BUCKETS = 32
MAX_DISTANCE = 128
GRID_W = 64
ROPE_THETA = 10000.0
ROPE_AXIS_DIM = HEAD_DIM // 2
PLE_DIM = 256
NORM_EPS = 1e-6
ATTN_SCALE = HEAD_DIM ** -0.5
LOG2E = math.log2(math.e)

S5_COLS = GROUP_WIDTH
GQA_QW = GQA_HEADS * HEAD_DIM
GQA_KW = GQA_KV_HEADS * HEAD_DIM
GQA_COLS = GQA_QW + 2 * GQA_KW
DIFF_QW = DIFF_HEADS * 2 * HEAD_DIM
DIFF_COLS = 2 * DIFF_QW + DIFF_HEADS * HEAD_DIM
HY_COLS = (HY_ORDER + 1) * HY_WIDTH
OFF_GQA = S5_COLS
OFF_DIFF = OFF_GQA + GQA_COLS
OFF_HY = OFF_DIFF + DIFF_COLS
IN_COLS = OFF_HY + HY_COLS
IN_SPLITS = ((0, OFF_GQA), (OFF_GQA, OFF_DIFF), (OFF_DIFF, OFF_HY), (OFF_HY, IN_COLS))

LANES = 128
SUBLANES = 8
ROW_TILES = D_MODEL // LANES
FFT_N2 = 128
FFT_ROWS = 32
VMEM_LIMIT = 48 * 1024 * 1024

ROW_TILE = 256
ATTN_Q_TILE = 2048
ATTN_K_TILE = 2048
ATTN_UNIT = 512
ATTN_SUB = 512
ATTN_SUB_PLAIN = 1024
S5_TIME_BLOCK = 64
MOE_TILE = 512
MOE_TOK_TILE = 256
MIX_UNROLL = 4
GATHER_CHUNK = 8


def _cp(*sem):
    return pltpu.CompilerParams(dimension_semantics=sem, vmem_limit_bytes=VMEM_LIMIT)


def _sds(shape, dtype=F32):
    return jax.ShapeDtypeStruct(shape, dtype)


def _rms(x, g=None):
    y = x * lax.rsqrt(jnp.mean(x * x, axis=-1, keepdims=True) + NORM_EPS)
    return y if g is None else y * g


def _inproj_body(x_ref, g_ref, w_ref, *o_refs):
    u = _rms(x_ref[...], g_ref[...]).astype(BF16)
    for o_ref, (lo, hi) in zip(o_refs, IN_SPLITS):
        o_ref[...] = jnp.dot(u, w_ref[:, lo:hi], preferred_element_type=F32)


def in_proj(h, g, w, batch, seq):
    t = h.shape[0]
    tm = min(ROW_TILE, seq)
    nb = seq // tm
    row_major = lambda i: (i, 0)
    time_major = lambda i: (i % nb, i // nb)
    widths = [hi - lo for lo, hi in IN_SPLITS]
    return pl.pallas_call(
        _inproj_body,
        grid=(t // tm,),
        in_specs=[pl.BlockSpec((tm, D_MODEL), row_major),
                  pl.BlockSpec((1, D_MODEL), lambda i: (0, 0)),
                  pl.BlockSpec((D_MODEL, IN_COLS), lambda i: (0, 0))],
        out_specs=[pl.BlockSpec((tm, widths[0]), time_major)] + [pl.BlockSpec((tm, w_), row_major) for w_ in widths[1:]],
        out_shape=[_sds((seq, batch * widths[0]))] + [_sds((t, w_)) for w_ in widths[1:]],
        compiler_params=_cp("parallel"),
        name="in_proj",
    )(h, g, w)


def s5_tables(lam_re, lam_im, log_dt, b_re, b_im, c_re, c_im):
    lr = jnp.minimum(lam_re, -1e-4)
    li = lam_im
    dt = jnp.exp(log_dt)[..., None]
    mag = jnp.exp(lr * dt)
    ar = mag * jnp.cos(li * dt)
    ai = mag * jnp.sin(li * dt)
    den = lr * lr + li * li
    cr = ((ar - 1.0) * lr + ai * li) / den
    ci = (ai * lr - (ar - 1.0) * li) / den
    bbr = cr[..., None] * b_re - ci[..., None] * b_im
    bbi = cr[..., None] * b_im + ci[..., None] * b_re
    eye = jnp.eye(S5_GROUPS, dtype=F32)
    bb = jnp.stack([bbr, bbi], axis=1)
    wb = jnp.einsum("dpgnh,gk->dghpkn", bb, eye).reshape(2, S5_COLS, 2 * S5_NSTATE)
    cc = jnp.stack([c_re, -c_im], axis=0)
    wc = jnp.einsum("pdghn,gk->dpgnkh", cc, eye).reshape(2, 2 * S5_NSTATE, S5_COLS)
    a_re = jnp.repeat(ar.reshape(2, S5_NSTATE), 4, axis=0)
    a_im = jnp.repeat(ai.reshape(2, S5_NSTATE), 4, axis=0)
    return wb.astype(BF16), wc.astype(BF16), a_re, a_im


def _s5_body(uf_ref, ub_ref, wb_ref, wc_ref, ar_ref, ai_ref, yf_ref, yb_ref, bf_ref, bb_ref, xr_ref, xi_ref, *, tb):
    @pl.when(pl.program_id(0) == 0)
    def _():
        xr_ref[...] = jnp.zeros_like(xr_ref)
        xi_ref[...] = jnp.zeros_like(xi_ref)

    bf_ref[...] = jnp.dot(uf_ref[...].astype(BF16), wb_ref[0], preferred_element_type=F32)
    bb_ref[...] = jnp.dot(ub_ref[...].astype(BF16), wb_ref[1], preferred_element_type=F32)
    ar = ar_ref[...]
    ai = ai_ref[...]
    low = lax.broadcasted_iota(I32, (SUBLANES, 1), 0) < 4
    half = SUBLANES // 2
    re = slice(0, S5_NSTATE)
    im = slice(S5_NSTATE, 2 * S5_NSTATE)

    def pair(m, carry):
        xr, xi = carry
        rf = pl.ds(pl.multiple_of(m * SUBLANES, SUBLANES), SUBLANES)
        rb = pl.ds(pl.multiple_of((tb // 2 - 1 - m) * SUBLANES, SUBLANES), SUBLANES)
        f_r, f_i, b_r, b_i = bf_ref[rf, re], bf_ref[rf, im], bb_ref[rb, re], bb_ref[rb, im]
        in_r = jnp.where(low, f_r, b_r)
        in_i = jnp.where(low, f_i, b_i)
        x1r = ar * xr - ai * xi + in_r
        x1i = ar * xi + ai * xr + in_i
        in_r = pltpu.roll(jnp.where(low, b_r, f_r), half, 0)
        in_i = pltpu.roll(jnp.where(low, b_i, f_i), half, 0)
        x2r = ar * x1r - ai * x1i + in_r
        x2i = ar * x1i + ai * x1r + in_i
        s2r = pltpu.roll(x2r, half, 0)
        s2i = pltpu.roll(x2i, half, 0)
        bf_ref[rf, re] = jnp.where(low, x1r, s2r)
        bf_ref[rf, im] = jnp.where(low, x1i, s2i)
        bb_ref[rb, re] = jnp.where(low, s2r, x1r)
        bb_ref[rb, im] = jnp.where(low, s2i, x1i)
        return x2r, x2i

    xr, xi = lax.fori_loop(0, tb // 2, pair, (xr_ref[...], xi_ref[...]))
    xr_ref[...] = xr
    xi_ref[...] = xi
    yf_ref[...] = jnp.dot(bf_ref[...].astype(BF16), wc_ref[0], preferred_element_type=F32)
    yb_ref[...] = jnp.dot(bb_ref[...].astype(BF16), wc_ref[1], preferred_element_type=F32)


def s5_scan(u, wb, wc, a_re, a_im, seq):
    tb = min(S5_TIME_BLOCK, seq)
    nb = seq // tb
    rows = tb * 4
    fwd = pl.BlockSpec((rows, S5_COLS), lambda i: (i, 0))
    bwd = pl.BlockSpec((rows, S5_COLS), lambda i: (nb - 1 - i, 0))
    return pl.pallas_call(
        functools.partial(_s5_body, tb=tb),
        grid=(nb,),
        in_specs=[fwd, bwd,
                  pl.BlockSpec((2, S5_COLS, 2 * S5_NSTATE), lambda i: (0, 0, 0)),
                  pl.BlockSpec((2, 2 * S5_NSTATE, S5_COLS), lambda i: (0, 0, 0)),
                  pl.BlockSpec((SUBLANES, S5_NSTATE), lambda i: (0, 0)),
                  pl.BlockSpec((SUBLANES, S5_NSTATE), lambda i: (0, 0))],
        out_specs=[fwd, bwd],
        out_shape=[_sds((seq * 4, S5_COLS))] * 2,
        scratch_shapes=[pltpu.VMEM((rows, 2 * S5_NSTATE), F32), pltpu.VMEM((rows, 2 * S5_NSTATE), F32),
                        pltpu.VMEM((SUBLANES, S5_NSTATE), F32), pltpu.VMEM((SUBLANES, S5_NSTATE), F32)],
        compiler_params=_cp("arbitrary"),
        name="s5_scan",
    )(u, u, wb, wc, a_re, a_im)


def _s5_out_body(yf_ref, yb_ref, u_ref, d_ref, w_ref, b_ref, o_ref):
    y = jax.nn.gelu(yf_ref[...] + yb_ref[...] + u_ref[...] * d_ref[...])
    gate = jnp.dot(y.astype(BF16), w_ref[...], preferred_element_type=F32) + b_ref[...]
    o_ref[...] = y * jax.nn.sigmoid(gate)


def s5_out(yf, yb, u, d, w_glu, b_glu):
    t = u.shape[0]
    tm = min(ROW_TILE, t)
    row = pl.BlockSpec((tm, S5_COLS), lambda i: (i, 0))
    vec = pl.BlockSpec((1, S5_COLS), lambda i: (0, 0))
    return pl.pallas_call(
        _s5_out_body,
        grid=(t // tm,),
        in_specs=[row, row, row, vec, pl.BlockSpec((S5_COLS, S5_COLS), lambda i: (0, 0)), vec],
        out_specs=row,
        out_shape=_sds((t, S5_COLS)),
        compiler_params=_cp("parallel"),
        name="s5_out",
    )(yf, yb, u, d, w_glu, b_glu)


def s5_mixer(z_s5, tabs, d, w_glu, b_glu, batch, seq):
    assert batch == 4 and seq % 2 == 0, "the scan packs 4 sequences x 2 directions onto the 8 sublanes"
    u = z_s5.reshape(seq * batch, S5_COLS)
    yf, yb = s5_scan(u, *tabs, seq)
    return s5_out(yf, yb, u, d, w_glu, b_glu).reshape(seq, batch * S5_COLS)


def _prep_body(z_ref, qg_ref, kg_ref, c_ref, s_ref, q_ref, k_ref, v_ref, *, nq, nk, nv, rope, k_src):
    tm = z_ref.shape[0]
    lane = lax.broadcasted_iota(I32, (tm, LANES), 1)
    low = lane < HEAD_DIM

    def norm_pair(x, g):
        sq = x * x
        ss_lo = jnp.sum(jnp.where(low, sq, 0.0), axis=-1, keepdims=True)
        ss_hi = jnp.sum(jnp.where(low, 0.0, sq), axis=-1, keepdims=True)
        y = x * lax.rsqrt(jnp.where(low, ss_lo, ss_hi) * (1.0 / HEAD_DIM) + NORM_EPS) * g
        if rope:
            half = HEAD_DIM // 2
            partner = jnp.where(lane % HEAD_DIM < half, pltpu.roll(y, LANES - half, 1), pltpu.roll(y, half, 1))
            y = y * c_ref[...] + partner * s_ref[...]
        return y

    qg = qg_ref[...]
    kg = kg_ref[...]
    for p in range(nq // 2):
        q_ref[0, p] = (norm_pair(z_ref[:, p * LANES:(p + 1) * LANES], qg) * (ATTN_SCALE * LOG2E)).astype(BF16)
    k_lo = nq * HEAD_DIM
    pairs = [norm_pair(z_ref[:, k_lo + p * LANES:k_lo + (p + 1) * LANES], kg) for p in range(nk // 2)]
    for h in range(nq):
        src = k_src(h)
        x = pairs[src // 2]
        if src % 2 != h % 2:
            x = pltpu.roll(x, HEAD_DIM, 1)
        k_ref[0, h] = jnp.where(low if h % 2 == 0 else jnp.logical_not(low), x, 0.0).astype(BF16)
    ones_col = jnp.where(lax.broadcasted_iota(I32, (tm, LANES - HEAD_DIM), 1) == 0, 1.0, 0.0)
    for h in range(nv):
        lo = (nq + nk + h) * HEAD_DIM
        v_ref[0, h] = jnp.concatenate([z_ref[:, lo:lo + HEAD_DIM], ones_col], axis=1).astype(BF16)


def attn_prep(z, qg, kg, cos_t, sin_t, batch, seq, nq, nk, nv, rope, k_src):
    width = (nq + nk + nv) * HEAD_DIM
    tm = min(ROW_TILE, seq)
    nb = seq // tm
    vec = pl.BlockSpec((1, LANES), lambda b, i: (0, 0))
    tab = pl.BlockSpec((tm, LANES), lambda b, i: (i, 0))
    heads = lambda n: pl.BlockSpec((1, n, tm, LANES), lambda b, i: (b, 0, i, 0))
    return pl.pallas_call(
        functools.partial(_prep_body, nq=nq, nk=nk, nv=nv, rope=rope, k_src=k_src),
        grid=(batch, nb),
        in_specs=[pl.BlockSpec((tm, width), lambda b, i: (b * nb + i, 0)), vec, vec, tab, tab],
        out_specs=[heads(nq // 2), heads(nq), heads(nv)],
        out_shape=[_sds((batch, nq // 2, seq, LANES), BF16), _sds((batch, nq, seq, LANES), BF16),
                   _sds((batch, nv, seq, LANES), BF16)],
        compiler_params=_cp("parallel", "parallel"),
        name="attn_prep",
    )(z, jnp.tile(qg, (1, 2)), jnp.tile(kg, (1, 2)), cos_t, sin_t)


def _flash_body(q_ref, k_ref, v_ref, *rest, unit, sub, has_bias):
    if has_bias:
        b_ref, o_ref, m_ref, acc_ref = rest
    else:
        o_ref, m_ref, acc_ref = rest
    i = pl.program_id(2)
    j = pl.program_id(3)
    tq = q_ref.shape[2]
    tk = k_ref.shape[2]

    @pl.when(j == 0)
    def _():
        m_ref[...] = jnp.full_like(m_ref, -jnp.inf)
        acc_ref[...] = jnp.zeros_like(acc_ref)

    k = k_ref[0, 0]
    v = v_ref[0, 0]
    for g in range(tq // sub):
        r0 = g * sub
        s = lax.dot_general(q_ref[0, 0, r0:r0 + sub, :], k, (((1,), (1,)), ((), ())),
                            preferred_element_type=F32)
        if has_bias:
            row_unit = i * (tq // unit) + r0 // unit
            tiles = []
            for c in range(tk // unit):
                sel = jnp.clip(j * (tk // unit) + c - row_unit, -2, 2) + 2
                tiles.append(b_ref[0, sel, r0 % unit:r0 % unit + sub, :])
            s = s + (tiles[0] if len(tiles) == 1 else jnp.concatenate(tiles, axis=1))
        blocks = [s[:, b * LANES:(b + 1) * LANES] for b in range(tk // LANES)]
        bmax = blocks[0]
        for blk in blocks[1:]:
            bmax = jnp.maximum(bmax, blk)
        m_prev = m_ref[r0:r0 + sub, :]
        m_new = jnp.maximum(m_prev, jnp.max(bmax, axis=-1, keepdims=True))
        alpha = jnp.exp2(m_prev - m_new)
        p = jnp.concatenate([jnp.exp2(blk - m_new) for blk in blocks], axis=1).astype(BF16)
        acc_ref[r0:r0 + sub, :] = alpha * acc_ref[r0:r0 + sub, :] + jnp.dot(p, v, preferred_element_type=F32)
        m_ref[r0:r0 + sub, :] = m_new

    @pl.when(j == pl.num_programs(3) - 1)
    def _():
        acc = acc_ref[...]
        o_ref[0, 0] = acc[:, :HEAD_DIM] / acc[:, HEAD_DIM:HEAD_DIM + 1]


def flash_attention(q, k, v, v_of_q, bias=None):
    batch, nq, seq, _ = k.shape
    tq = min(ATTN_Q_TILE, seq)
    tk = min(ATTN_K_TILE, seq)
    unit = min(ATTN_UNIT, seq)
    sub = min(ATTN_SUB if bias is not None else ATTN_SUB_PLAIN, seq)
    in_specs = [pl.BlockSpec((1, 1, tq, LANES), lambda b, h, i, j: (b, h // 2, i, 0)),
                pl.BlockSpec((1, 1, tk, LANES), lambda b, h, i, j: (b, h, j, 0)),
                pl.BlockSpec((1, 1, tk, LANES), lambda b, h, i, j: (b, v_of_q(h), j, 0))]
    args = [q, k, v]
    if bias is not None:
        assert bias.shape[2] == unit
        in_specs.append(pl.BlockSpec((1, 5, unit, unit), lambda b, h, i, j: (v_of_q(h), 0, 0, 0)))
        args.append(bias)
    return pl.pallas_call(
        functools.partial(_flash_body, unit=unit, sub=sub, has_bias=bias is not None),
        grid=(batch, nq, seq // tq, seq // tk),
        in_specs=in_specs,
        out_specs=pl.BlockSpec((1, 1, tq, HEAD_DIM), lambda b, h, i, j: (b, h, i, 0)),
        out_shape=_sds((batch, nq, seq, HEAD_DIM)),
        scratch_shapes=[pltpu.VMEM((tq, LANES), F32), pltpu.VMEM((tq, LANES), F32)],
        compiler_params=_cp("parallel", "parallel", "parallel", "arbitrary"),
        name="flash_bias" if bias is not None else "flash",
    )(*args)


def _t5_bucket_table(t):
    rel = np.arange(-(3 * t - 1), 3 * t)
    half = NUM_BUCKETS // 2
    max_exact = half // 2
    ret = np.where(rel > 0, half, 0)
    n = np.abs(rel)
    nf = np.maximum(n, 1).astype(np.float64)
    large = max_exact + (np.log(nf / max_exact) / math.log(MAX_DISTANCE / max_exact) * (half - max_exact)).astype(np.int64)
    large = np.minimum(large, half - 1)
    return ret + np.where(n < max_exact, n, large)


def diff_bias_tiles(rel_bias, t):
    assert t >= MAX_DISTANCE, "offsets of two or more tiles must lie in the saturated buckets"
    heads = rel_bias.shape[1]
    vec = rel_bias[jnp.asarray(_t5_bucket_table(t), dtype=I32)].T * LOG2E
    tiles = []
    for d in range(-2, 3):
        lo = d * t + 2 * t
        w = vec[:, lo:lo + 2 * t - 1]
        u = jnp.concatenate([w[:, t - 1:], jnp.zeros((heads, 1), F32), w[:, :t - 1]], axis=1)
        skew = jnp.tile(u, (1, t))[:, :t * (2 * t - 1)].reshape(heads, t, 2 * t - 1)
        tiles.append(skew[:, :, :t])
    return jnp.stack(tiles, axis=1)


def rope_tables(seq):
    rows = seq // GRID_W
    row = jnp.repeat(jnp.arange(rows, dtype=F32), GRID_W)
    col = jnp.tile(jnp.arange(GRID_W, dtype=F32), rows)
    freq = ROPE_THETA ** (-jnp.arange(0, ROPE_AXIS_DIM, 2, dtype=F32) / ROPE_AXIS_DIM)
    ang = jnp.concatenate([row[:, None] * freq, col[:, None] * freq], -1)
    cos, sin = jnp.cos(ang), jnp.sin(ang)
    return jnp.concatenate([cos, cos] * 2, -1), jnp.concatenate([-sin, sin] * 2, -1)


def _conv3_body(x0, x1, x2, w0, w1, w2, b0, b1, b2, o0, o1, o2):
    for x_ref, w_ref, b_ref, o_ref in ((x0, w0, b0, o0), (x1, w1, b1, o1), (x2, w2, b2, o2)):
        x = x_ref[0]
        n = x.shape[0]
        t = lax.broadcasted_iota(I32, (n, 1), 0)
        prev = jnp.where(t == 0, 0.0, pltpu.roll(x, 1, 0))
        nxt = jnp.where(t == n - 1, 0.0, pltpu.roll(x, n - 1, 0))
        o_ref[0] = w_ref[0:1] * prev + w_ref[1:2] * x + w_ref[2:3] * nxt + b_ref[...]


def hyena_conv3(z_hy, conv_w, conv_b, batch, seq):
    z = z_hy.reshape(batch, seq, HY_COLS)
    nc = HY_WIDTH // LANES
    x_spec = lambda g: pl.BlockSpec((1, seq, LANES), lambda b, c: (b, 0, g * nc + c))
    w_spec = lambda g: pl.BlockSpec((3, LANES), lambda b, c: (0, g * nc + c))
    b_spec = lambda g: pl.BlockSpec((1, LANES), lambda b, c: (0, g * nc + c))
    out = pl.BlockSpec((1, seq, LANES), lambda b, c: (b, 0, c))
    return pl.pallas_call(
        _conv3_body,
        grid=(batch, nc),
        in_specs=[x_spec(0), x_spec(1), x_spec(2), w_spec(0), w_spec(1), w_spec(2),
                  b_spec(0), b_spec(1), b_spec(2)],
        out_specs=[out, out, out],
        out_shape=[_sds((batch, seq, HY_WIDTH))] * 3,
        compiler_params=_cp("parallel", "parallel"),
        name="hyena_conv3",
    )(z, z, z, conv_w, conv_w, conv_w, conv_b, conv_b, conv_b)


def _filt_body(feat_ref, dist_ref, w1, b1, f1, w2, b2, f2, w3, dec, h_ref, s_ref):
    dot = functools.partial(jnp.dot, precision=HI, preferred_element_type=F32)
    hid = jnp.sin(f1[...] * (dot(feat_ref[...], w1[...]) + b1[...]))
    hid = jnp.sin(f2[...] * (dot(hid, w2[...]) + b2[...]))
    h = dot(hid, w3[...]) * jnp.exp(-dist_ref[...] * jnp.abs(dec[...]))
    h_ref[...] = h

    @pl.when(pl.program_id(0) == 0)
    def _():
        s_ref[...] = jnp.zeros_like(s_ref)

    s_ref[...] += jnp.sum(jnp.abs(h), axis=0, keepdims=True)


def hyena_filter(seq, w1, b1, f1, w2, b2, f2, w3, decay):
    j = jnp.arange(seq, dtype=F32)
    tt = j / seq
    ang = 2.0 * jnp.pi * tt[:, None] * jnp.arange(1, HY_BANDS + 1, dtype=F32)
    feat = jnp.concatenate([tt[:, None], jnp.cos(ang), jnp.sin(ang)], -1)
    feat = jnp.pad(feat, ((0, 0), (0, LANES - HY_EMB)))
    half = seq // 2
    dist = (jnp.abs(j - half) / half)[:, None]
    w1p = jnp.pad(w1, ((0, LANES - HY_EMB), (0, 0)))
    tm = min(1024, seq)
    hw = HY_ORDER * HY_WIDTH
    const = lambda i: (0, 0)
    return pl.pallas_call(
        _filt_body,
        grid=(seq // tm,),
        in_specs=[pl.BlockSpec((tm, LANES), lambda i: (i, 0)), pl.BlockSpec((tm, 1), lambda i: (i, 0)),
                  pl.BlockSpec((LANES, HY_FFN), const), pl.BlockSpec((1, HY_FFN), const),
                  pl.BlockSpec((1, HY_FFN), const), pl.BlockSpec((HY_FFN, HY_FFN), const),
                  pl.BlockSpec((1, HY_FFN), const), pl.BlockSpec((1, HY_FFN), const),
                  pl.BlockSpec((HY_FFN, hw), const), pl.BlockSpec((1, hw), const)],
        out_specs=[pl.BlockSpec((tm, hw), lambda i: (i, 0)), pl.BlockSpec((1, hw), const)],
        out_shape=[_sds((seq, hw)), _sds((1, hw))],
        compiler_params=_cp("arbitrary"),
        name="hyena_filter",
    )(feat, dist, w1p, b1[None], f1[None], w2, b2[None], f2[None], w3, decay.reshape(1, hw))


def dft_tables(seq):
    n = 2 * seq
    n2 = FFT_N2
    n1 = n // n2
    n1h = n1 // 2
    nk = -(-(n1h + 1) // SUBLANES) * SUBLANES
    k1 = jnp.arange(nk, dtype=I32)
    live = (k1 <= n1h).astype(F32)
    ang = (2.0 * jnp.pi / n1) * ((k1[:, None] * jnp.arange(n1h, dtype=I32)[None, :]) % n1).astype(F32)
    outer_fwd = jnp.concatenate([jnp.cos(ang), -jnp.sin(ang)], axis=0) * jnp.tile(live, 2)[:, None]
    t1 = jnp.arange(n1h, dtype=I32) + n1 // 4
    ang = (2.0 * jnp.pi / n1) * ((t1[:, None] * k1[None, :]) % n1).astype(F32)
    weight = live * jnp.where((k1 == 0) | (k1 == n1h), 1.0, 2.0) / n
    outer_inv = jnp.concatenate([jnp.cos(ang), -jnp.sin(ang)], axis=1) * jnp.tile(weight, 2)[None, :]
    k2 = jnp.arange(n2, dtype=I32)
    t2 = jnp.arange(n2, dtype=I32)
    phase = (t2[None, None, :] * (k2[None, :, None] * n1 + k1[:, None, None])) % n
    ang = (2.0 * jnp.pi / n) * phase.astype(F32)
    fr, fi = jnp.cos(ang), -jnp.sin(ang)
    inner = jnp.concatenate([jnp.concatenate([fr, -fi], axis=2),
                             jnp.concatenate([fi, fr], axis=2)], axis=1)
    return _split_bf16(outer_fwd), _split_bf16(outer_inv), _split_bf16(inner), _split_bf16(jnp.swapaxes(inner, 1, 2))


def _split_bf16(x):
    hi = x.astype(BF16)
    return hi, (x - hi.astype(F32)).astype(BF16)


def _dot_split(w, a):
    w_hi, w_lo = w
    a_hi, a_lo = _split_bf16(a)
    dot = functools.partial(jnp.dot, preferred_element_type=F32)
    return dot(w_hi, a_hi) + (dot(w_hi, a_lo) + dot(w_lo, a_hi))


def _dot_split_rhs(a, w):
    w_hi, w_lo = w
    a_hi, a_lo = _split_bf16(a)
    dot = functools.partial(jnp.dot, preferred_element_type=F32)
    return dot(a_hi, w_hi) + (dot(a_lo, w_hi) + dot(a_hi, w_lo))


def _every(ref, s, n):
    return ref[pl.ds(s, n, stride=FFT_ROWS), :]


def _fft_outer_body(f_ref, fl_ref, x_ref, o_ref):
    n1h = x_ref.shape[0]
    rows = f_ref.shape[0]
    x2 = x_ref.reshape(n1h * FFT_ROWS, LANES)
    o2 = o_ref.reshape(rows * FFT_ROWS, LANES)
    x = jnp.concatenate([_every(x2, s, n1h) for s in range(FFT_ROWS)], axis=1)
    r = _dot_split((f_ref[...], fl_ref[...]), x)
    for s in range(FFT_ROWS):
        o2[pl.ds(s, rows, stride=FFT_ROWS), :] = r[:, s * LANES:(s + 1) * LANES]


def fft_outer(x4, table):
    batch, n1h, n2, ch = x4.shape
    nk = table[0].shape[0] // 2
    tab = pl.BlockSpec((2 * nk, n1h), lambda b, c, l: (0, 0))
    return pl.pallas_call(
        _fft_outer_body,
        grid=(batch, n2 // FFT_ROWS, ch // LANES),
        in_specs=[tab, tab, pl.BlockSpec((None, n1h, FFT_ROWS, LANES), lambda b, c, l: (b, 0, c, l))],
        out_specs=pl.BlockSpec((None, 2, nk, FFT_ROWS, LANES), lambda b, c, l: (b, 0, 0, c, l)),
        out_shape=_sds((batch, 2, nk, n2, ch)),
        compiler_params=_cp("parallel", "parallel", "parallel"),
        name="fft_outer",
    )(*table, x4)


def _fft_filter_body(mh_ref, ml_ref, s_ref, a_ref, o_ref):
    n2 = FFT_N2
    a = a_ref[0, :, 0].reshape(2 * n2, HY_WIDTH)
    x = _dot_split((mh_ref[0], ml_ref[0]), a) * (1.0 / s_ref[...])
    o_ref[:, 0] = x.reshape(2, n2, HY_WIDTH)


def fft_filter(a5, inner, s):
    _, _, n1, n2, ch = a5.shape
    return pl.pallas_call(
        _fft_filter_body,
        grid=(n1, ch // HY_WIDTH),
        in_specs=[pl.BlockSpec((1, 2 * n2, 2 * n2), lambda k, c: (k, 0, 0)),
                  pl.BlockSpec((1, 2 * n2, 2 * n2), lambda k, c: (k, 0, 0)),
                  pl.BlockSpec((1, HY_WIDTH), lambda k, c: (0, c)),
                  pl.BlockSpec((1, 2, 1, n2, HY_WIDTH), lambda k, c: (0, 0, k, 0, c))],
        out_specs=pl.BlockSpec((2, 1, n2, HY_WIDTH), lambda k, c: (0, k, 0, c)),
        out_shape=_sds((2, n1, n2, ch)),
        compiler_params=_cp("parallel", "parallel"),
        name="fft_filter",
    )(*inner, s, a5)


def _fft_mid_body(mh_ref, ml_ref, th_ref, tl_ref, h_ref, a_ref, o_ref, *, nb):
    n2 = FFT_N2
    m = (mh_ref[0], ml_ref[0])
    mt = (th_ref[0], tl_ref[0])
    hr = h_ref[0, 0]
    hi = h_ref[1, 0]
    for b in range(nb):
        x = _dot_split(m, a_ref[b, :, 0].reshape(2 * n2, HY_WIDTH))
        xr, xi = x[:n2], x[n2:]
        y = jnp.concatenate([xr * hr - xi * hi, xr * hi + xi * hr], axis=0)
        o_ref[b, :, 0] = _dot_split(mt, y).reshape(2, n2, HY_WIDTH)


def fft_mid(a5, inner, inner_t, hf, order):
    nb, _, n1, n2, ch = a5.shape
    mat = pl.BlockSpec((1, 2 * n2, 2 * n2), lambda k: (k, 0, 0))
    blk = pl.BlockSpec((nb, 2, 1, n2, ch), lambda k: (0, 0, k, 0, 0))
    return pl.pallas_call(
        functools.partial(_fft_mid_body, nb=nb),
        grid=(n1,),
        in_specs=[mat, mat, mat, mat, pl.BlockSpec((2, 1, n2, ch), lambda k: (0, k, 0, order)), blk],
        out_specs=blk,
        out_shape=_sds(a5.shape),
        compiler_params=_cp("parallel"),
        name="fft_mid",
    )(*inner, *inner_t, hf, a5)


def _fft_inv_body(g_ref, gl_ref, b_ref, gate_ref, y_ref, fb_ref, o_ref):
    n1h, rows = g_ref.shape
    b2 = b_ref.reshape(rows * FFT_ROWS, LANES)
    gate2 = gate_ref.reshape(n1h * FFT_ROWS, LANES)
    y2 = y_ref.reshape(n1h * FFT_ROWS, LANES)
    o2 = o_ref.reshape(n1h * FFT_ROWS, LANES)
    bm = jnp.concatenate([_every(b2, s, rows) for s in range(FFT_ROWS)], axis=1)
    c = _dot_split((g_ref[...], gl_ref[...]), bm)
    fb = fb_ref[...]
    for s in range(FFT_ROWS):
        conv = c[:, s * LANES:(s + 1) * LANES]
        o2[pl.ds(s, n1h, stride=FFT_ROWS), :] = _every(gate2, s, n1h) * (conv + fb * _every(y2, s, n1h))


def fft_inv_gate(b5, table, gate4, y4, fbias):
    batch, _, nk, n2, ch = b5.shape
    n1h = table[0].shape[0]
    blk = pl.BlockSpec((None, n1h, FFT_ROWS, LANES), lambda b, c, l: (b, 0, c, l))
    tab = pl.BlockSpec((n1h, 2 * nk), lambda b, c, l: (0, 0))
    return pl.pallas_call(
        _fft_inv_body,
        grid=(batch, n2 // FFT_ROWS, ch // LANES),
        in_specs=[tab, tab,
                  pl.BlockSpec((None, 2, nk, FFT_ROWS, LANES), lambda b, c, l: (b, 0, 0, c, l)), blk, blk,
                  pl.BlockSpec((1, LANES), lambda b, c, l: (0, l))],
        out_specs=blk,
        out_shape=_sds((batch, n1h, n2, ch)),
        compiler_params=_cp("parallel", "parallel", "parallel"),
        name="fft_inv_gate",
    )(*table, b5, gate4, y4, fbias.reshape(1, ch))


def hyena_spectrum(seq, tabs, w1, b1, f1, w2, b2, f2, w3, decay):
    outer_fwd, _, inner, _ = tabs
    n1h = outer_fwd[0].shape[1]
    hw = HY_ORDER * HY_WIDTH
    h, s = hyena_filter(seq, w1, b1, f1, w2, b2, f2, w3, decay)
    return fft_filter(fft_outer(h.reshape(1, n1h, FFT_N2, hw), outer_fwd), inner, s)


def hyena_mixer(z_hy, conv_w, conv_b, hf, f_bias, tabs, batch, seq):
    outer_fwd, outer_inv, inner, inner_t = tabs
    n1h = outer_fwd[0].shape[1]
    shape4 = (batch, n1h, FFT_N2, HY_WIDTH)
    v, g0, g1 = hyena_conv3(z_hy, conv_w, conv_b, batch, seq)
    y = v.reshape(shape4)
    for order, gate in enumerate((g0, g1)):
        bm = fft_mid(fft_outer(y, outer_fwd), inner, inner_t, hf, order)
        y = fft_inv_gate(bm, outer_inv, gate.reshape(shape4), y, f_bias[order])
    return y.reshape(batch * seq, HY_WIDTH)


def _mix_body(lam_ref, h_ref, s5_ref, gqa_ref, diff_ref, hy_ref, gs5, ggqa, gsub, ghy, w_ref, o_ref, *, keep):
    lam = lam_ref[0]
    a = _rms(s5_ref[...], gs5[...])
    b = _rms(jnp.concatenate([gqa_ref[0, h] for h in range(GQA_HEADS)], axis=1), ggqa[...])
    c = jnp.concatenate(
        [_rms(diff_ref[0, 2 * h] - lam * diff_ref[0, 2 * h + 1], gsub[...]) * keep for h in range(DIFF_HEADS)],
        axis=1)
    d = _rms(hy_ref[...], ghy[...])
    mixed = jnp.concatenate([a, b, c, d], axis=1).astype(BF16)
    o_ref[...] = h_ref[...] + jnp.dot(mixed, w_ref[...], preferred_element_type=F32)


def mix_out(lam, h, y_s5, o_gqa, o_diff, y_hy, gs5, ggqa, gsub, ghy, w_out, batch, seq, lambda_init):
    tm = min(ROW_TILE, seq)
    nb = seq // tm
    row = lambda w: pl.BlockSpec((tm, w), lambda b, i: (b * nb + i, 0))
    vec = lambda w: pl.BlockSpec((1, w), lambda b, i: (0, 0))
    heads = lambda n: pl.BlockSpec((1, n, tm, HEAD_DIM), lambda b, i: (b, 0, i, 0))
    time_major = pl.BlockSpec((tm, GROUP_WIDTH), lambda b, i: (i, b))
    return pl.pallas_call(
        functools.partial(_mix_body, keep=1.0 - lambda_init),
        grid=(batch, nb),
        in_specs=[pl.BlockSpec(memory_space=pltpu.SMEM), row(D_MODEL), time_major, heads(GQA_HEADS),
                  heads(2 * DIFF_HEADS), row(GROUP_WIDTH), vec(GROUP_WIDTH), vec(GROUP_WIDTH), vec(HEAD_DIM),
                  vec(GROUP_WIDTH), pl.BlockSpec((D_MODEL, D_MODEL), lambda b, i: (0, 0))],
        out_specs=row(D_MODEL),
        out_shape=_sds((batch * seq, D_MODEL)),
        compiler_params=_cp("parallel", "parallel"),
        name="mix_out",
    )(lam, h, y_s5, o_gqa, o_diff, y_hy, gs5, ggqa, gsub, ghy, w_out)


def _router_body(h_ref, g_ref, w_ref, wl_ref, b_ref, xn_ref, ti_ref, tw_ref, cnt_ref):
    xn = _rms(h_ref[...], g_ref[...])
    xn_ref[...] = xn
    logits = _dot_split_rhs(xn, (w_ref[...], wl_ref[...])) + b_ref[...]
    lane = lax.broadcasted_iota(I32, logits.shape, 1)
    vals = logits
    tops, idxs = [], []
    hot = jnp.zeros(logits.shape, F32)
    for _ in range(TOP_K):
        m = jnp.max(vals, axis=-1, keepdims=True)
        idx = jnp.min(jnp.where(vals == m, lane, N_EXPERTS), axis=-1, keepdims=True)
        sel = lane == idx
        tops.append(m)
        idxs.append(idx)
        hot = hot + sel.astype(F32)
        vals = jnp.where(sel, -jnp.inf, vals)
    es = [jnp.exp(t - tops[0]) for t in tops]
    den = es[0] + es[1] + es[2] + es[3]
    ti_ref[...] = jnp.concatenate(idxs, axis=1)
    tw_ref[...] = jnp.concatenate([e / den for e in es], axis=1)

    @pl.when(pl.program_id(0) == 0)
    def _():
        cnt_ref[...] = jnp.zeros_like(cnt_ref)

    cnt_ref[...] += jnp.sum(hot, axis=0, keepdims=True)


def moe_router(h, g, w_router, b_router):
    t = h.shape[0]
    tm = min(ROW_TILE, t)
    const = lambda i: (0, 0)
    return pl.pallas_call(
        _router_body,
        grid=(t // tm,),
        in_specs=[pl.BlockSpec((tm, D_MODEL), lambda i: (i, 0)), pl.BlockSpec((1, D_MODEL), const),
                  pl.BlockSpec((D_MODEL, N_EXPERTS), const), pl.BlockSpec((D_MODEL, N_EXPERTS), const),
                  pl.BlockSpec((1, N_EXPERTS), const)],
        out_specs=[pl.BlockSpec((tm, D_MODEL), lambda i: (i, 0)), pl.BlockSpec((tm, TOP_K), lambda i: (i, 0)),
                   pl.BlockSpec((tm, TOP_K), lambda i: (i, 0)), pl.BlockSpec((1, N_EXPERTS), const)],
        out_shape=[_sds((t, D_MODEL)), _sds((t, TOP_K), I32), _sds((t, TOP_K)), _sds((1, N_EXPERTS))],
        compiler_params=_cp("arbitrary"),
        name="moe_router",
    )(h, g, *w_router, b_router)


def _rank_body(ti_ref, off_ref, pos_ref, stage_ref, cnt_ref, carry_ref):
    @pl.when(pl.program_id(0) == 0)
    def _():
        carry_ref[...] = jnp.zeros_like(carry_ref)

    ti = ti_ref[...]
    tm = ti.shape[0]
    lane = lax.broadcasted_iota(I32, (tm, N_EXPERTS), 1)
    hots = [lane == ti[:, k:k + 1] for k in range(TOP_K)]
    hot = sum(h.astype(F32) for h in hots)
    r = lax.broadcasted_iota(I32, (tm, tm), 0)
    c = lax.broadcasted_iota(I32, (tm, tm), 1)
    below = jnp.where(r > c, 1.0, 0.0).astype(BF16)
    local = jnp.dot(below, hot.astype(BF16), preferred_element_type=F32)
    before = local + carry_ref[...] + off_ref[...]
    pick = lambda table: jnp.concatenate(
        [jnp.sum(jnp.where(h, table, 0.0), axis=-1, keepdims=True) for h in hots], axis=1).astype(I32)
    pos_ref[...] = pick(before)
    count = jnp.sum(hot, axis=0, keepdims=True)
    er = lax.broadcasted_iota(I32, (N_EXPERTS, N_EXPERTS), 0)
    ec = lax.broadcasted_iota(I32, (N_EXPERTS, N_EXPERTS), 1)
    earlier = jnp.where(er < ec, 1.0, 0.0)
    chunks = jnp.ceil(count * (1.0 / GATHER_CHUNK))
    start = jnp.dot(chunks, earlier, precision=HI, preferred_element_type=F32) * GATHER_CHUNK
    stage_ref[...] = pick(local + start)
    cnt_ref[...] = jnp.broadcast_to(count, cnt_ref.shape)
    carry_ref[...] += count


def moe_rank(top_i, offsets):
    t = top_i.shape[0]
    tm = min(MOE_TOK_TILE, t)
    pairs = pl.BlockSpec((tm, TOP_K), lambda i: (i, 0))
    pos, stage, cnt = pl.pallas_call(
        _rank_body,
        grid=(t // tm,),
        in_specs=[pairs, pl.BlockSpec((1, N_EXPERTS), lambda i: (0, 0))],
        out_specs=[pairs, pairs, pl.BlockSpec((SUBLANES, N_EXPERTS), lambda i: (i, 0))],
        out_shape=[_sds((t, TOP_K), I32), _sds((t, TOP_K), I32), _sds((t // tm * SUBLANES, N_EXPERTS))],
        scratch_shapes=[pltpu.VMEM((1, N_EXPERTS), F32)],
        compiler_params=_cp("arbitrary"),
        name="moe_rank",
    )(top_i, offsets)
    return pos, stage, cnt[::SUBLANES]


def _row_copy(src, dst, sem):
    return pltpu.make_async_copy(src, dst, sem)


def _pad_zero_body(lt_ref, o_ref):
    del lt_ref
    o_ref[...] = jnp.zeros_like(o_ref)


def moe_pad_tiles(last_tile, n_slots):
    rows = MOE_TILE * ROW_TILES
    return pl.pallas_call(
        _pad_zero_body,
        grid_spec=pltpu.PrefetchScalarGridSpec(
            num_scalar_prefetch=1, grid=(N_EXPERTS,), in_specs=[],
            out_specs=pl.BlockSpec((rows, LANES), lambda e, lt: (lt[e], 0))),
        out_shape=_sds((n_slots * ROW_TILES, LANES)),
        compiler_params=_cp("arbitrary"),
        name="moe_pad_tiles",
    )(last_tile)


def _slot_rows(ref, slot):
    return ref.at[pl.ds(pl.multiple_of(slot * ROW_TILES, ROW_TILES), ROW_TILES)]


def _scatter_body(pos_ref, x_ref, padded_hbm, xs_hbm, buf, sem, *, tm):
    del padded_hbm
    x = x_ref[...]
    for s in range(ROW_TILES):
        buf[pl.ds(s, tm, stride=ROW_TILES), :] = x[:, s * LANES:(s + 1) * LANES]

    def start(r, c):
        for k in range(TOP_K):
            _row_copy(_slot_rows(buf, r), _slot_rows(xs_hbm, pos_ref[r * TOP_K + k]), sem).start()
        return c

    lax.fori_loop(0, tm, start, 0)
    for _ in range(TOP_K):
        _row_copy(buf, xs_hbm.at[pl.ds(0, tm * ROW_TILES)], sem).wait()


def moe_scatter(pos_flat, x, padded):
    t = x.shape[0]
    tm = min(MOE_TOK_TILE, t)
    return pl.pallas_call(
        functools.partial(_scatter_body, tm=tm),
        grid=(t // tm,),
        in_specs=[pl.BlockSpec((tm * TOP_K,), lambda i: (i,), memory_space=pltpu.SMEM),
                  pl.BlockSpec((tm, D_MODEL), lambda i: (i, 0)), pl.BlockSpec(memory_space=pl.ANY)],
        out_specs=pl.BlockSpec(memory_space=pl.ANY),
        out_shape=_sds(padded.shape),
        scratch_shapes=[pltpu.VMEM((tm * ROW_TILES, LANES), F32), pltpu.SemaphoreType.DMA(())],
        input_output_aliases={2: 0},
        compiler_params=_cp("arbitrary"),
        name="moe_scatter",
    )(pos_flat, x, padded)


def _experts_body(te_ref, nv_ref, x_ref, wgu_ref, bgu_ref, wd_ref, bd_ref, o_ref):
    del te_ref
    j = pl.program_id(0)
    dff = wd_ref.shape[1]

    @pl.when(j < nv_ref[0])
    def _():
        tm = x_ref.shape[0] // ROW_TILES
        x = jnp.concatenate([x_ref[pl.ds(s, tm, stride=ROW_TILES), :] for s in range(ROW_TILES)],
                            axis=1).astype(BF16)
        hgu = jnp.dot(x, wgu_ref[0], preferred_element_type=F32) + bgu_ref[0]
        glu = jnp.minimum(hgu[:, :dff], SWIGLU_LIMIT)
        lin = jnp.clip(hgu[:, dff:], -SWIGLU_LIMIT, SWIGLU_LIMIT)
        act = glu * jax.nn.sigmoid(SWIGLU_ALPHA * glu) * (lin + 1.0)
        y = jnp.dot(act.astype(BF16), wd_ref[0], preferred_element_type=F32) + bd_ref[0]
        for s in range(ROW_TILES):
            o_ref[pl.ds(s, tm, stride=ROW_TILES), :] = y[:, s * LANES:(s + 1) * LANES]

    @pl.when(j >= nv_ref[0])
    def _():
        o_ref[...] = jnp.zeros_like(o_ref)


def moe_experts(tile_expert, n_valid, xs, w_gu, b_gu, w_down, b_down):
    n_slots = xs.shape[0] // ROW_TILES
    tm = MOE_TILE
    dff = w_down.shape[1]
    blk = pl.BlockSpec((tm * ROW_TILES, LANES), lambda j, te, nv: (j, 0))
    return pl.pallas_call(
        _experts_body,
        grid_spec=pltpu.PrefetchScalarGridSpec(
            num_scalar_prefetch=2,
            grid=(n_slots // tm,),
            in_specs=[pl.BlockSpec((tm * ROW_TILES, LANES), lambda j, te, nv: (jnp.minimum(j, nv[0] - 1), 0)),
                      pl.BlockSpec((1, D_MODEL, 2 * dff), lambda j, te, nv: (te[j], 0, 0)),
                      pl.BlockSpec((1, 1, 2 * dff), lambda j, te, nv: (te[j], 0, 0)),
                      pl.BlockSpec((1, dff, D_MODEL), lambda j, te, nv: (te[j], 0, 0)),
                      pl.BlockSpec((1, 1, D_MODEL), lambda j, te, nv: (te[j], 0, 0))],
            out_specs=blk),
        out_shape=_sds((n_slots * ROW_TILES, LANES)),
        compiler_params=_cp("arbitrary"),
        name="moe_experts",
    )(tile_expert, n_valid, xs, w_gu, b_gu, w_down, b_down)


def _combine_body(slot_ref, w_ref, tab_ref, ys_hbm, o_ref, stage, mixed, sem, *, tm):
    chunk_rows = GATHER_CHUNK * ROW_TILES

    def chunk_copy(src_slot, dst_slot):
        return _row_copy(ys_hbm.at[pl.ds(pl.multiple_of(src_slot * ROW_TILES, ROW_TILES), chunk_rows)],
                         stage.at[pl.ds(pl.multiple_of(dst_slot * ROW_TILES, ROW_TILES), chunk_rows)], sem)

    def per_expert(e, c):
        base, chunks, start = tab_ref[e], tab_ref[N_EXPERTS + e], tab_ref[2 * N_EXPERTS + e]

        def issue(j, c2):
            chunk_copy(base + j * GATHER_CHUNK, start + j * GATHER_CHUNK).start()
            return c2

        return lax.fori_loop(0, chunks, issue, c)

    lax.fori_loop(0, N_EXPERTS, per_expert, 0)

    def wait(j, c):
        chunk_copy(0, 0).wait()
        return c

    lax.fori_loop(0, tab_ref[3 * N_EXPERTS], wait, 0)

    def mix(q, c):
        for u in range(MIX_UNROLL):
            r = q * MIX_UNROLL + u
            acc = None
            for k in range(TOP_K):
                p = r * TOP_K + k
                term = w_ref[p] * stage[pl.ds(pl.multiple_of(slot_ref[p] * ROW_TILES, ROW_TILES), ROW_TILES), :]
                acc = term if acc is None else acc + term
            mixed[pl.ds(pl.multiple_of(r * ROW_TILES, ROW_TILES), ROW_TILES), :] = acc
        return c

    lax.fori_loop(0, tm // MIX_UNROLL, mix, 0)
    o_ref[...] = jnp.concatenate([mixed[pl.ds(s, tm, stride=ROW_TILES), :] for s in range(ROW_TILES)], axis=1)


def moe_combine(stage_flat, w_flat, table, ys, t):
    tm = min(MOE_TOK_TILE, t)
    smem = pl.BlockSpec((tm * TOP_K,), lambda i: (i,), memory_space=pltpu.SMEM)
    stage_slots = tm * TOP_K + N_EXPERTS * GATHER_CHUNK
    return pl.pallas_call(
        functools.partial(_combine_body, tm=tm),
        grid=(t // tm,),
        in_specs=[smem, smem, pl.BlockSpec((LANES,), lambda i: (i,), memory_space=pltpu.SMEM),
                  pl.BlockSpec(memory_space=pl.ANY)],
        out_specs=pl.BlockSpec((tm, D_MODEL), lambda i: (i, 0)),
        out_shape=_sds((t, D_MODEL)),
        scratch_shapes=[pltpu.VMEM((stage_slots * ROW_TILES, LANES), F32),
                        pltpu.VMEM((tm * ROW_TILES, LANES), F32), pltpu.SemaphoreType.DMA(())],
        compiler_params=_cp("arbitrary"),
        name="moe_combine",
    )(stage_flat, w_flat, table, ys)


def gather_table(cnt_tile, offsets):
    base = offsets + jnp.cumsum(cnt_tile, axis=0) - cnt_tile
    chunks = jnp.ceil(cnt_tile / GATHER_CHUNK)
    start = (jnp.cumsum(chunks, axis=1) - chunks) * GATHER_CHUNK
    total = jnp.sum(chunks, axis=1, keepdims=True)
    pad = jnp.zeros((cnt_tile.shape[0], LANES - 3 * N_EXPERTS - 1), F32)
    return jnp.concatenate([base, chunks, start, total, pad], axis=1).astype(I32).reshape(-1)


def moe(h, g, w_router, b_router, w_gu, b_gu, w_down, b_down):
    t = h.shape[0]
    xn, top_i, top_w, counts = moe_router(h, g, w_router, b_router)
    n_tiles = (t * TOP_K) // MOE_TILE + N_EXPERTS + 1
    padded = jnp.ceil(counts[0] / MOE_TILE) * MOE_TILE
    ends = jnp.cumsum(padded)
    offsets = (ends - padded)[None]
    tile_start = jnp.arange(n_tiles, dtype=F32) * MOE_TILE
    tile_expert = jnp.minimum(jnp.sum(ends[None, :] <= tile_start[:, None], axis=1), N_EXPERTS - 1).astype(I32)
    n_valid = (ends[-1:] / MOE_TILE).astype(I32)
    own_last = jnp.where(padded > 0, ends / MOE_TILE - 1, -1.0)
    filled = lax.cummax(own_last, axis=0)
    first = jnp.min(jnp.where(padded > 0, own_last, jnp.inf))
    last_tile = jnp.where(filled < 0, first, filled).astype(I32)
    pos, stage, cnt_tile = moe_rank(top_i, offsets)
    xs = moe_scatter(pos.reshape(t * TOP_K), xn, moe_pad_tiles(last_tile, n_tiles * MOE_TILE))
    ys = moe_experts(tile_expert, n_valid, xs, w_gu, b_gu, w_down, b_down)
    flat = lambda a: a.reshape(t * TOP_K)
    return moe_combine(flat(stage), flat(top_w), gather_table(cnt_tile, offsets), ys, t)


def _ple_body(h_ref, m_ref, p_ref, wp_ref, wg_ref, o_ref):
    h = h_ref[...] + m_ref[...]
    e = jnp.dot(p_ref[...].astype(BF16), wp_ref[...], preferred_element_type=F32)
    gate = jnp.dot(_rms(h).astype(BF16), wg_ref[...], preferred_element_type=F32)
    o_ref[...] = h + e * jax.nn.sigmoid(gate)


def ple(h, moe_out, p, w_ple, w_gate):
    t = h.shape[0]
    tm = min(ROW_TILE, t)
    row = pl.BlockSpec((tm, D_MODEL), lambda i: (i, 0))
    return pl.pallas_call(
        _ple_body,
        grid=(t // tm,),
        in_specs=[row, row, pl.BlockSpec((tm, PLE_DIM), lambda i: (i, 0)),
                  pl.BlockSpec((PLE_DIM, D_MODEL), lambda i: (0, 0)),
                  pl.BlockSpec((D_MODEL, D_MODEL), lambda i: (0, 0))],
        out_specs=row,
        out_shape=_sds((t, D_MODEL)),
        compiler_params=_cp("parallel"),
        name="ple",
    )(h, moe_out, p, w_ple, w_gate)


def _deinterleave(n):
    return np.concatenate([np.arange(0, n, 2), np.arange(1, n, 2)])


def _halves(w):
    pairs = w.reshape(*w.shape[:-1], w.shape[-1] // 2, 2)
    return jnp.concatenate([pairs[..., 0], pairs[..., 1]], axis=-1)


def prepare_layer(i, prm):
    pair = _deinterleave(HEAD_DIM)
    cols = np.arange(IN_COLS)
    for h in range(GQA_HEADS + GQA_KV_HEADS):
        lo = OFF_GQA + h * HEAD_DIM
        cols[lo:lo + HEAD_DIM] = lo + pair
    lam = (jnp.exp(jnp.sum(prm["diff_lam_q1"][i] * prm["diff_lam_k1"][i]))
           - jnp.exp(jnp.sum(prm["diff_lam_q2"][i] * prm["diff_lam_k2"][i])) + _lambda_init(i))
    return dict(
        norm1_g=prm["norm1_g"][i][None],
        w_in=prm["w_in"][i][:, cols].astype(BF16),
        s5=s5_tables(prm["s5_lam_re"][i], prm["s5_lam_im"][i], prm["s5_log_dt"][i], prm["s5_b_re"][i],
                     prm["s5_b_im"][i], prm["s5_c_re"][i], prm["s5_c_im"][i]),
        s5_d=prm["s5_d"][i][None],
        s5_w_glu=prm["s5_w_glu"][i].astype(BF16),
        s5_b_glu=prm["s5_b_glu"][i][None],
        gqa_q_g=prm["gqa_q_g"][i][pair][None],
        gqa_k_g=prm["gqa_k_g"][i][pair][None],
        diff_q_g=prm["diff_q_g"][i][None],
        diff_k_g=prm["diff_k_g"][i][None],
        diff_lam=lam.reshape(1).astype(F32),
        diff_subln_g=prm["diff_subln_g"][i][None],
        out_g_s5=prm["out_g_s5"][i][None],
        out_g_gqa=prm["out_g_gqa"][i][None],
        out_g_hy=prm["out_g_hy"][i][None],
        w_out=prm["w_out"][i].astype(BF16),
        norm2_g=prm["norm2_g"][i][None],
        w_router=_split_bf16(prm["w_router"][i]),
        b_router=prm["b_router"][i][None],
        w_gu=_halves(prm["w_gu"][i]).astype(BF16),
        b_gu=_halves(prm["b_gu"][i])[:, None, :],
        w_down=prm["w_down"][i].astype(BF16),
        b_down=prm["b_down"][i][:, None, :],
        w_ple=prm["w_ple"][i].astype(BF16),
        w_ple_gate=prm["w_ple_gate"][i].astype(BF16),
    )


def _lambda_init(i):
    return 0.8 - 0.6 * math.exp(-0.3 * i)


def run_trunk(x, p, prm, layers):
    batch, seq, _ = x.shape
    t = batch * seq
    cos_t, sin_t = rope_tables(seq)
    tabs = dft_tables(seq)
    bias = diff_bias_tiles(prm["rel_bias"], min(ATTN_UNIT, seq))
    h = x.reshape(t, D_MODEL)
    for i, lw in enumerate(layers):
        z_s5, z_gqa, z_diff, z_hy = in_proj(h, lw["norm1_g"], lw["w_in"], batch, seq)
        y_s5 = s5_mixer(z_s5, lw["s5"], lw["s5_d"], lw["s5_w_glu"], lw["s5_b_glu"], batch, seq)
        rep = GQA_HEADS // GQA_KV_HEADS
        q, k, v = attn_prep(z_gqa, lw["gqa_q_g"], lw["gqa_k_g"], cos_t, sin_t, batch, seq,
                            GQA_HEADS, GQA_KV_HEADS, GQA_KV_HEADS, True, lambda hq: hq // rep)
        o_gqa = flash_attention(q, k, v, lambda hq: hq // rep)
        q, k, v = attn_prep(z_diff, lw["diff_q_g"], lw["diff_k_g"], cos_t, sin_t, batch, seq,
                            2 * DIFF_HEADS, 2 * DIFF_HEADS, DIFF_HEADS, False, lambda hq: hq)
        o_diff = flash_attention(q, k, v, lambda hq: hq // 2, bias)
        hf = hyena_spectrum(seq, tabs, prm["hy_w1"][i], prm["hy_b1"][i], prm["hy_freq1"][i], prm["hy_w2"][i],
                            prm["hy_b2"][i], prm["hy_freq2"][i], prm["hy_w3"][i], prm["hy_decay"][i])
        y_hy = hyena_mixer(z_hy, prm["hy_conv_w"][i], prm["hy_conv_b"][i][None], hf, prm["hy_bias"][i],
                           tabs, batch, seq)
        h = mix_out(lw["diff_lam"], h, y_s5, o_gqa, o_diff, y_hy, lw["out_g_s5"], lw["out_g_gqa"],
                    lw["diff_subln_g"], lw["out_g_hy"], lw["w_out"], batch, seq, _lambda_init(i))
        m = moe(h, lw["norm2_g"], lw["w_router"], lw["b_router"], lw["w_gu"], lw["b_gu"], lw["w_down"],
                lw["b_down"])
        h = ple(h, m, p[i].reshape(t, PLE_DIM), lw["w_ple"], lw["w_ple_gate"])
    return h.reshape(batch, seq, D_MODEL)


def kernel(x_prompt, x_sample, p_prompt, p_sample, rel_bias, norm1_g, w_in, s5_lam_re, s5_lam_im, s5_log_dt,
           s5_b_re, s5_b_im, s5_c_re, s5_c_im, s5_d, s5_w_glu, s5_b_glu, gqa_q_g, gqa_k_g, diff_q_g, diff_k_g,
           diff_lam_q1, diff_lam_k1, diff_lam_q2, diff_lam_k2, diff_subln_g, hy_conv_w, hy_conv_b, hy_w1, hy_b1,
           hy_freq1, hy_w2, hy_b2, hy_freq2, hy_w3, hy_decay, hy_bias, out_g_s5, out_g_gqa, out_g_hy, w_out,
           norm2_g, w_router, b_router, w_gu, b_gu, w_down, b_down, w_ple, w_ple_gate):
    prm = dict(rel_bias=rel_bias, norm1_g=norm1_g, w_in=w_in, s5_lam_re=s5_lam_re, s5_lam_im=s5_lam_im,
               s5_log_dt=s5_log_dt, s5_b_re=s5_b_re, s5_b_im=s5_b_im, s5_c_re=s5_c_re, s5_c_im=s5_c_im,
               s5_d=s5_d, s5_w_glu=s5_w_glu, s5_b_glu=s5_b_glu, gqa_q_g=gqa_q_g, gqa_k_g=gqa_k_g,
               diff_q_g=diff_q_g, diff_k_g=diff_k_g, diff_lam_q1=diff_lam_q1, diff_lam_k1=diff_lam_k1,
               diff_lam_q2=diff_lam_q2, diff_lam_k2=diff_lam_k2, diff_subln_g=diff_subln_g,
               hy_conv_w=hy_conv_w, hy_conv_b=hy_conv_b, hy_w1=hy_w1, hy_b1=hy_b1, hy_freq1=hy_freq1,
               hy_w2=hy_w2, hy_b2=hy_b2, hy_freq2=hy_freq2, hy_w3=hy_w3, hy_decay=hy_decay, hy_bias=hy_bias,
               out_g_s5=out_g_s5, out_g_gqa=out_g_gqa, out_g_hy=out_g_hy, w_out=w_out, norm2_g=norm2_g,
               w_router=w_router, b_router=b_router, w_gu=w_gu, b_gu=b_gu, w_down=w_down, b_down=b_down,
               w_ple=w_ple, w_ple_gate=w_ple_gate)
    layers = [prepare_layer(i, prm) for i in range(DEPTH)]
    return (run_trunk(x_prompt, p_prompt, prm, layers), run_trunk(x_sample, p_sample, prm, layers))
```

```python
import functools
import math

import numpy as np
import jax
import jax.numpy as jnp
from jax import lax
from jax.experimental import pallas as pl
from jax.experimental.pallas import tpu as pltpu

F32 = jnp.float32
BF16 = jnp.bfloat16
I32 = jnp.int32
HI = lax.Precision.HIGHEST

D_MODEL = 1024
DEPTH = 2
GROUP_WIDTH = 256
HEAD_DIM = 64
S5_GROUP_DIM = 16
S5_GROUPS = 16
S5_STATE = 64
S5_NSTATE = S5_GROUPS * S5_STATE
GQA_HEADS = 4
GQA_KV_HEADS = 2
DIFF_HEADS = 4
HY_WIDTH = 256
HY_ORDER = 2
HY_BANDS = 16
HY_EMB = 2 * HY_BANDS + 1
HY_FFN = 64
N_EXPERTS = 32
TOP_K = 4
SWIGLU_LIMIT = 7.0
SWIGLU_ALPHA = 1.702
NUM_BUCKETS = 32
MAX_DISTANCE = 128
GRID_W = 64
ROPE_THETA = 10000.0
ROPE_AXIS_DIM = HEAD_DIM // 2
PLE_DIM = 256
NORM_EPS = 1e-6
ATTN_SCALE = HEAD_DIM ** -0.5
LOG2E = math.log2(math.e)

S5_COLS = GROUP_WIDTH
GQA_QW = GQA_HEADS * HEAD_DIM
GQA_KW = GQA_KV_HEADS * HEAD_DIM
GQA_COLS = GQA_QW + 2 * GQA_KW
DIFF_QW = DIFF_HEADS * 2 * HEAD_DIM
DIFF_COLS = 2 * DIFF_QW + DIFF_HEADS * HEAD_DIM
HY_COLS = (HY_ORDER + 1) * HY_WIDTH
OFF_GQA = S5_COLS
OFF_DIFF = OFF_GQA + GQA_COLS
OFF_HY = OFF_DIFF + DIFF_COLS
IN_COLS = OFF_HY + HY_COLS
IN_SPLITS = ((0, OFF_GQA), (OFF_GQA, OFF_DIFF), (OFF_DIFF, OFF_HY), (OFF_HY, IN_COLS))

LANES = 128
SUBLANES = 8
ROW_TILES = D_MODEL // LANES
FFT_N2 = 128
FFT_ROWS = 32
VMEM_LIMIT = 48 * 1024 * 1024

ROW_TILE = 256
ATTN_Q_TILE = 2048
ATTN_K_TILE = 2048
ATTN_UNIT = 512
ATTN_SUB = 512
ATTN_SUB_PLAIN = 1024
S5_TIME_BLOCK = 64
MOE_TILE = 512
MOE_TOK_TILE = 256
MIX_UNROLL = 4
GATHER_CHUNK = 8


def _cp(*sem):
    return pltpu.CompilerParams(dimension_semantics=sem, vmem_limit_bytes=VMEM_LIMIT)


def _sds(shape, dtype=F32):
    return jax.ShapeDtypeStruct(shape, dtype)


def _rms(x, g=None):
    y = x * lax.rsqrt(jnp.mean(x * x, axis=-1, keepdims=True) + NORM_EPS)
    return y if g is None else y * g


def _inproj_body(x_ref, g_ref, w_ref, *o_refs):
    u = _rms(x_ref[...], g_ref[...]).astype(BF16)
    for o_ref, (lo, hi) in zip(o_refs, IN_SPLITS):
        o_ref[...] = jnp.dot(u, w_ref[:, lo:hi], preferred_element_type=F32)


def in_proj(h, g, w, batch, seq):
    t = h.shape[0]
    tm = min(ROW_TILE, seq)
    nb = seq // tm
    row_major = lambda i: (i, 0)
    time_major = lambda i: (i % nb, i // nb)
    widths = [hi - lo for lo, hi in IN_SPLITS]
    return pl.pallas_call(
        _inproj_body,
        grid=(t // tm,),
        in_specs=[pl.BlockSpec((tm, D_MODEL), row_major),
                  pl.BlockSpec((1, D_MODEL), lambda i: (0, 0)),
                  pl.BlockSpec((D_MODEL, IN_COLS), lambda i: (0, 0))],
        out_specs=[pl.BlockSpec((tm, widths[0]), time_major)] + [pl.BlockSpec((tm, w_), row_major) for w_ in widths[1:]],
        out_shape=[_sds((seq, batch * widths[0]))] + [_sds((t, w_)) for w_ in widths[1:]],
        compiler_params=_cp("parallel"),
        name="in_proj",
    )(h, g, w)


def s5_tables(lam_re, lam_im, log_dt, b_re, b_im, c_re, c_im):
    lr = jnp.minimum(lam_re, -1e-4)
    li = lam_im
    dt = jnp.exp(log_dt)[..., None]
    mag = jnp.exp(lr * dt)
    ar = mag * jnp.cos(li * dt)
    ai = mag * jnp.sin(li * dt)
    den = lr * lr + li * li
    cr = ((ar - 1.0) * lr + ai * li) / den
    ci = (ai * lr - (ar - 1.0) * li) / den
    bbr = cr[..., None] * b_re - ci[..., None] * b_im
    bbi = cr[..., None] * b_im + ci[..., None] * b_re
    eye = jnp.eye(S5_GROUPS, dtype=F32)
    bb = jnp.stack([bbr, bbi], axis=1)
    wb = jnp.einsum("dpgnh,gk->dghpkn", bb, eye).reshape(2, S5_COLS, 2 * S5_NSTATE)
    cc = jnp.stack([c_re, -c_im], axis=0)
    wc = jnp.einsum("pdghn,gk->dpgnkh", cc, eye).reshape(2, 2 * S5_NSTATE, S5_COLS)
    a_re = jnp.repeat(ar.reshape(2, S5_NSTATE), 4, axis=0)
    a_im = jnp.repeat(ai.reshape(2, S5_NSTATE), 4, axis=0)
    return wb.astype(BF16), wc.astype(BF16), a_re, a_im


def _s5_body(uf_ref, ub_ref, wb_ref, wc_ref, ar_ref, ai_ref, yf_ref, yb_ref, bf_ref, bb_ref, xr_ref, xi_ref, *, tb):
    @pl.when(pl.program_id(0) == 0)
    def _():
        xr_ref[...] = jnp.zeros_like(xr_ref)
        xi_ref[...] = jnp.zeros_like(xi_ref)

    bf_ref[...] = jnp.dot(uf_ref[...].astype(BF16), wb_ref[0], preferred_element_type=F32)
    bb_ref[...] = jnp.dot(ub_ref[...].astype(BF16), wb_ref[1], preferred_element_type=F32)
    ar = ar_ref[...]
    ai = ai_ref[...]
    low = lax.broadcasted_iota(I32, (SUBLANES, 1), 0) < 4
    half = SUBLANES // 2
    re = slice(0, S5_NSTATE)
    im = slice(S5_NSTATE, 2 * S5_NSTATE)

    def pair(m, carry):
        xr, xi = carry
        rf = pl.ds(pl.multiple_of(m * SUBLANES, SUBLANES), SUBLANES)
        rb = pl.ds(pl.multiple_of((tb // 2 - 1 - m) * SUBLANES, SUBLANES), SUBLANES)
        f_r, f_i, b_r, b_i = bf_ref[rf, re], bf_ref[rf, im], bb_ref[rb, re], bb_ref[rb, im]
        in_r = jnp.where(low, f_r, b_r)
        in_i = jnp.where(low, f_i, b_i)
        x1r = ar * xr - ai * xi + in_r
        x1i = ar * xi + ai * xr + in_i
        in_r = pltpu.roll(jnp.where(low, b_r, f_r), half, 0)
        in_i = pltpu.roll(jnp.where(low, b_i, f_i), half, 0)
        x2r = ar * x1r - ai * x1i + in_r
        x2i = ar * x1i + ai * x1r + in_i
        s2r = pltpu.roll(x2r, half, 0)
        s2i = pltpu.roll(x2i, half, 0)
        bf_ref[rf, re] = jnp.where(low, x1r, s2r)
        bf_ref[rf, im] = jnp.where(low, x1i, s2i)
        bb_ref[rb, re] = jnp.where(low, s2r, x1r)
        bb_ref[rb, im] = jnp.where(low, s2i, x1i)
        return x2r, x2i

    xr, xi = lax.fori_loop(0, tb // 2, pair, (xr_ref[...], xi_ref[...]))
    xr_ref[...] = xr
    xi_ref[...] = xi
    yf_ref[...] = jnp.dot(bf_ref[...].astype(BF16), wc_ref[0], preferred_element_type=F32)
    yb_ref[...] = jnp.dot(bb_ref[...].astype(BF16), wc_ref[1], preferred_element_type=F32)


def s5_scan(u, wb, wc, a_re, a_im, seq):
    tb = min(S5_TIME_BLOCK, seq)
    nb = seq // tb
    rows = tb * 4
    fwd = pl.BlockSpec((rows, S5_COLS), lambda i: (i, 0))
    bwd = pl.BlockSpec((rows, S5_COLS), lambda i: (nb - 1 - i, 0))
    return pl.pallas_call(
        functools.partial(_s5_body, tb=tb),
        grid=(nb,),
        in_specs=[fwd, bwd,
                  pl.BlockSpec((2, S5_COLS, 2 * S5_NSTATE), lambda i: (0, 0, 0)),
                  pl.BlockSpec((2, 2 * S5_NSTATE, S5_COLS), lambda i: (0, 0, 0)),
                  pl.BlockSpec((SUBLANES, S5_NSTATE), lambda i: (0, 0)),
                  pl.BlockSpec((SUBLANES, S5_NSTATE), lambda i: (0, 0))],
        out_specs=[fwd, bwd],
        out_shape=[_sds((seq * 4, S5_COLS))] * 2,
        scratch_shapes=[pltpu.VMEM((rows, 2 * S5_NSTATE), F32), pltpu.VMEM((rows, 2 * S5_NSTATE), F32),
                        pltpu.VMEM((SUBLANES, S5_NSTATE), F32), pltpu.VMEM((SUBLANES, S5_NSTATE), F32)],
        compiler_params=_cp("arbitrary"),
        name="s5_scan",
    )(u, u, wb, wc, a_re, a_im)


def _s5_out_body(yf_ref, yb_ref, u_ref, d_ref, w_ref, b_ref, o_ref):
    y = jax.nn.gelu(yf_ref[...] + yb_ref[...] + u_ref[...] * d_ref[...])
    gate = jnp.dot(y.astype(BF16), w_ref[...], preferred_element_type=F32) + b_ref[...]
    o_ref[...] = y * jax.nn.sigmoid(gate)


def s5_out(yf, yb, u, d, w_glu, b_glu):
    t = u.shape[0]
    tm = min(ROW_TILE, t)
    row = pl.BlockSpec((tm, S5_COLS), lambda i: (i, 0))
    vec = pl.BlockSpec((1, S5_COLS), lambda i: (0, 0))
    return pl.pallas_call(
        _s5_out_body,
        grid=(t // tm,),
        in_specs=[row, row, row, vec, pl.BlockSpec((S5_COLS, S5_COLS), lambda i: (0, 0)), vec],
        out_specs=row,
        out_shape=_sds((t, S5_COLS)),
        compiler_params=_cp("parallel"),
        name="s5_out",
    )(yf, yb, u, d, w_glu, b_glu)


def s5_mixer(z_s5, tabs, d, w_glu, b_glu, batch, seq):
    assert batch == 4 and seq % 2 == 0, "the scan packs 4 sequences x 2 directions onto the 8 sublanes"
    u = z_s5.reshape(seq * batch, S5_COLS)
    yf, yb = s5_scan(u, *tabs, seq)
    return s5_out(yf, yb, u, d, w_glu, b_glu).reshape(seq, batch * S5_COLS)


def _prep_body(z_ref, qg_ref, kg_ref, c_ref, s_ref, q_ref, k_ref, v_ref, *, nq, nk, nv, rope, k_src):
    tm = z_ref.shape[0]
    lane = lax.broadcasted_iota(I32, (tm, LANES), 1)
    low = lane < HEAD_DIM

    def norm_pair(x, g):
        sq = x * x
        ss_lo = jnp.sum(jnp.where(low, sq, 0.0), axis=-1, keepdims=True)
        ss_hi = jnp.sum(jnp.where(low, 0.0, sq), axis=-1, keepdims=True)
        y = x * lax.rsqrt(jnp.where(low, ss_lo, ss_hi) * (1.0 / HEAD_DIM) + NORM_EPS) * g
        if rope:
            half = HEAD_DIM // 2
            partner = jnp.where(lane % HEAD_DIM < half, pltpu.roll(y, LANES - half, 1), pltpu.roll(y, half, 1))
            y = y * c_ref[...] + partner * s_ref[...]
        return y

    qg = qg_ref[...]
    kg = kg_ref[...]
    for p in range(nq // 2):
        q_ref[0, p] = (norm_pair(z_ref[:, p * LANES:(p + 1) * LANES], qg) * (ATTN_SCALE * LOG2E)).astype(BF16)
    k_lo = nq * HEAD_DIM
    pairs = [norm_pair(z_ref[:, k_lo + p * LANES:k_lo + (p + 1) * LANES], kg) for p in range(nk // 2)]
    for h in range(nq):
        src = k_src(h)
        x = pairs[src // 2]
        if src % 2 != h % 2:
            x = pltpu.roll(x, HEAD_DIM, 1)
        k_ref[0, h] = jnp.where(low if h % 2 == 0 else jnp.logical_not(low), x, 0.0).astype(BF16)
    ones_col = jnp.where(lax.broadcasted_iota(I32, (tm, LANES - HEAD_DIM), 1) == 0, 1.0, 0.0)
    for h in range(nv):
        lo = (nq + nk + h) * HEAD_DIM
        v_ref[0, h] = jnp.concatenate([z_ref[:, lo:lo + HEAD_DIM], ones_col], axis=1).astype(BF16)


def attn_prep(z, qg, kg, cos_t, sin_t, batch, seq, nq, nk, nv, rope, k_src):
    width = (nq + nk + nv) * HEAD_DIM
    tm = min(ROW_TILE, seq)
    nb = seq // tm
    vec = pl.BlockSpec((1, LANES), lambda b, i: (0, 0))
    tab = pl.BlockSpec((tm, LANES), lambda b, i: (i, 0))
    heads = lambda n: pl.BlockSpec((1, n, tm, LANES), lambda b, i: (b, 0, i, 0))
    return pl.pallas_call(
        functools.partial(_prep_body, nq=nq, nk=nk, nv=nv, rope=rope, k_src=k_src),
        grid=(batch, nb),
        in_specs=[pl.BlockSpec((tm, width), lambda b, i: (b * nb + i, 0)), vec, vec, tab, tab],
        out_specs=[heads(nq // 2), heads(nq), heads(nv)],
        out_shape=[_sds((batch, nq // 2, seq, LANES), BF16), _sds((batch, nq, seq, LANES), BF16),
                   _sds((batch, nv, seq, LANES), BF16)],
        compiler_params=_cp("parallel", "parallel"),
        name="attn_prep",
    )(z, jnp.tile(qg, (1, 2)), jnp.tile(kg, (1, 2)), cos_t, sin_t)


def _flash_body(q_ref, k_ref, v_ref, *rest, unit, sub, sub_far, has_bias):
    if has_bias:
        b_ref, o_ref, m_ref, acc_ref = rest
    else:
        o_ref, m_ref, acc_ref = rest
    i = pl.program_id(2)
    j = pl.program_id(3)
    tq = q_ref.shape[2]
    tk = k_ref.shape[2]
    q_units = tq // unit
    k_units = tk // unit

    @pl.when(j == 0)
    def _():
        m_ref[...] = jnp.full_like(m_ref, -jnp.inf)
        acc_ref[...] = jnp.zeros_like(acc_ref)

    def attend(rows, tiled, far_sel=None):
        k = k_ref[0, 0]
        v = v_ref[0, 0]
        for g in range(tq // rows):
            r0 = g * rows
            s = lax.dot_general(q_ref[0, 0, r0:r0 + rows, :], k, (((1,), (1,)), ((), ())),
                                preferred_element_type=F32)
            if tiled:
                row_unit = i * q_units + r0 // unit
                tiles = []
                for c in range(k_units):
                    sel = jnp.clip(j * k_units + c - row_unit, -2, 2) + 2
                    tiles.append(b_ref[0, sel, r0 % unit:r0 % unit + rows, :])
                s = s + (tiles[0] if len(tiles) == 1 else jnp.concatenate(tiles, axis=1))
            blocks = [s[:, b * LANES:(b + 1) * LANES] for b in range(tk // LANES)]
            bmax = blocks[0]
            for blk in blocks[1:]:
                bmax = jnp.maximum(bmax, blk)
            row_max = jnp.max(bmax, axis=-1, keepdims=True)
            m_prev = m_ref[r0:r0 + rows, :]
            if far_sel is None:
                m_new = jnp.maximum(m_prev, row_max)
                shift = m_new
            else:
                const = b_ref[0, far_sel, 0:1, 0:LANES]
                m_new = jnp.maximum(m_prev, row_max + const)
                shift = m_new - const
            alpha = jnp.exp2(m_prev - m_new)
            p = jnp.concatenate([jnp.exp2(blk - shift) for blk in blocks], axis=1).astype(BF16)
            acc_ref[r0:r0 + rows, :] = alpha * acc_ref[r0:r0 + rows, :] + jnp.dot(p, v, preferred_element_type=F32)
            m_ref[r0:r0 + rows, :] = m_new

    if has_bias:
        lo = j * k_units - (i * q_units + q_units - 1)
        hi = j * k_units + k_units - 1 - i * q_units
        far = jnp.logical_or(lo >= 2, hi <= -2)
        pl.when(far)(lambda: attend(sub_far, False, jnp.where(lo >= 2, 4, 0)))
        pl.when(jnp.logical_not(far))(lambda: attend(sub, True))
    else:
        attend(sub, False)

    @pl.when(j == pl.num_programs(3) - 1)
    def _():
        acc = acc_ref[...]
        o_ref[0, 0] = acc[:, :HEAD_DIM] / acc[:, HEAD_DIM:HEAD_DIM + 1]


def flash_attention(q, k, v, v_of_q, bias=None):
    batch, nq, seq, _ = k.shape
    tq = min(ATTN_Q_TILE, seq)
    tk = min(ATTN_K_TILE, seq)
    unit = min(ATTN_UNIT, seq)
    sub = min(ATTN_SUB if bias is not None else ATTN_SUB_PLAIN, tq)
    sub_far = min(ATTN_SUB_PLAIN, tq)
    in_specs = [pl.BlockSpec((1, 1, tq, LANES), lambda b, h, i, j: (b, h // 2, i, 0)),
                pl.BlockSpec((1, 1, tk, LANES), lambda b, h, i, j: (b, h, j, 0)),
                pl.BlockSpec((1, 1, tk, LANES), lambda b, h, i, j: (b, v_of_q(h), j, 0))]
    args = [q, k, v]
    if bias is not None:
        assert bias.shape[2] == unit
        in_specs.append(pl.BlockSpec((1, 5, unit, unit), lambda b, h, i, j: (v_of_q(h), 0, 0, 0)))
        args.append(bias)
    return pl.pallas_call(
        functools.partial(_flash_body, unit=unit, sub=sub, sub_far=sub_far, has_bias=bias is not None),
        grid=(batch, nq, seq // tq, seq // tk),
        in_specs=in_specs,
        out_specs=pl.BlockSpec((1, 1, tq, HEAD_DIM), lambda b, h, i, j: (b, h, i, 0)),
        out_shape=_sds((batch, nq, seq, HEAD_DIM)),
        scratch_shapes=[pltpu.VMEM((tq, LANES), F32), pltpu.VMEM((tq, LANES), F32)],
        compiler_params=_cp("parallel", "parallel", "parallel", "arbitrary"),
        name="flash_bias" if bias is not None else "flash",
    )(*args)


def _t5_bucket_table(t):
    rel = np.arange(-(3 * t - 1), 3 * t)
    half = NUM_BUCKETS // 2
    max_exact = half // 2
    ret = np.where(rel > 0, half, 0)
    n = np.abs(rel)
    nf = np.maximum(n, 1).astype(np.float64)
    large = max_exact + (np.log(nf / max_exact) / math.log(MAX_DISTANCE / max_exact) * (half - max_exact)).astype(np.int64)
    large = np.minimum(large, half - 1)
    return ret + np.where(n < max_exact, n, large)


def diff_bias_tiles(rel_bias, t):
    assert t >= MAX_DISTANCE, "offsets of two or more tiles must lie in the saturated buckets"
    heads = rel_bias.shape[1]
    vec = rel_bias[jnp.asarray(_t5_bucket_table(t), dtype=I32)].T * LOG2E
    tiles = []
    for d in range(-2, 3):
        lo = d * t + 2 * t
        w = vec[:, lo:lo + 2 * t - 1]
        u = jnp.concatenate([w[:, t - 1:], jnp.zeros((heads, 1), F32), w[:, :t - 1]], axis=1)
        skew = jnp.tile(u, (1, t))[:, :t * (2 * t - 1)].reshape(heads, t, 2 * t - 1)
        tiles.append(skew[:, :, :t])
    return jnp.stack(tiles, axis=1)


def rope_tables(seq):
    rows = seq // GRID_W
    row = jnp.repeat(jnp.arange(rows, dtype=F32), GRID_W)
    col = jnp.tile(jnp.arange(GRID_W, dtype=F32), rows)
    freq = ROPE_THETA ** (-jnp.arange(0, ROPE_AXIS_DIM, 2, dtype=F32) / ROPE_AXIS_DIM)
    ang = jnp.concatenate([row[:, None] * freq, col[:, None] * freq], -1)
    cos, sin = jnp.cos(ang), jnp.sin(ang)
    return jnp.concatenate([cos, cos] * 2, -1), jnp.concatenate([-sin, sin] * 2, -1)


def _conv3_body(x0, x1, x2, w0, w1, w2, b0, b1, b2, o0, o1, o2):
    for x_ref, w_ref, b_ref, o_ref in ((x0, w0, b0, o0), (x1, w1, b1, o1), (x2, w2, b2, o2)):
        x = x_ref[0]
        n = x.shape[0]
        t = lax.broadcasted_iota(I32, (n, 1), 0)
        prev = jnp.where(t == 0, 0.0, pltpu.roll(x, 1, 0))
        nxt = jnp.where(t == n - 1, 0.0, pltpu.roll(x, n - 1, 0))
        o_ref[0] = w_ref[0:1] * prev + w_ref[1:2] * x + w_ref[2:3] * nxt + b_ref[...]


def hyena_conv3(z_hy, conv_w, conv_b, batch, seq):
    z = z_hy.reshape(batch, seq, HY_COLS)
    nc = HY_WIDTH // LANES
    x_spec = lambda g: pl.BlockSpec((1, seq, LANES), lambda b, c: (b, 0, g * nc + c))
    w_spec = lambda g: pl.BlockSpec((3, LANES), lambda b, c: (0, g * nc + c))
    b_spec = lambda g: pl.BlockSpec((1, LANES), lambda b, c: (0, g * nc + c))
    out = pl.BlockSpec((1, seq, LANES), lambda b, c: (b, 0, c))
    return pl.pallas_call(
        _conv3_body,
        grid=(batch, nc),
        in_specs=[x_spec(0), x_spec(1), x_spec(2), w_spec(0), w_spec(1), w_spec(2),
                  b_spec(0), b_spec(1), b_spec(2)],
        out_specs=[out, out, out],
        out_shape=[_sds((batch, seq, HY_WIDTH))] * 3,
        compiler_params=_cp("parallel", "parallel"),
        name="hyena_conv3",
    )(z, z, z, conv_w, conv_w, conv_w, conv_b, conv_b, conv_b)


def _filt_body(feat_ref, dist_ref, w1, b1, f1, w2, b2, f2, w3, dec, h_ref, s_ref):
    dot = functools.partial(jnp.dot, precision=HI, preferred_element_type=F32)
    hid = jnp.sin(f1[...] * (dot(feat_ref[...], w1[...]) + b1[...]))
    hid = jnp.sin(f2[...] * (dot(hid, w2[...]) + b2[...]))
    h = dot(hid, w3[...]) * jnp.exp(-dist_ref[...] * jnp.abs(dec[...]))
    h_ref[...] = h

    @pl.when(pl.program_id(0) == 0)
    def _():
        s_ref[...] = jnp.zeros_like(s_ref)

    s_ref[...] += jnp.sum(jnp.abs(h), axis=0, keepdims=True)


def hyena_filter(seq, w1, b1, f1, w2, b2, f2, w3, decay):
    j = jnp.arange(seq, dtype=F32)
    tt = j / seq
    ang = 2.0 * jnp.pi * tt[:, None] * jnp.arange(1, HY_BANDS + 1, dtype=F32)
    feat = jnp.concatenate([tt[:, None], jnp.cos(ang), jnp.sin(ang)], -1)
    feat = jnp.pad(feat, ((0, 0), (0, LANES - HY_EMB)))
    half = seq // 2
    dist = (jnp.abs(j - half) / half)[:, None]
    w1p = jnp.pad(w1, ((0, LANES - HY_EMB), (0, 0)))
    tm = min(1024, seq)
    hw = HY_ORDER * HY_WIDTH
    const = lambda i: (0, 0)
    return pl.pallas_call(
        _filt_body,
        grid=(seq // tm,),
        in_specs=[pl.BlockSpec((tm, LANES), lambda i: (i, 0)), pl.BlockSpec((tm, 1), lambda i: (i, 0)),
                  pl.BlockSpec((LANES, HY_FFN), const), pl.BlockSpec((1, HY_FFN), const),
                  pl.BlockSpec((1, HY_FFN), const), pl.BlockSpec((HY_FFN, HY_FFN), const),
                  pl.BlockSpec((1, HY_FFN), const), pl.BlockSpec((1, HY_FFN), const),
                  pl.BlockSpec((HY_FFN, hw), const), pl.BlockSpec((1, hw), const)],
        out_specs=[pl.BlockSpec((tm, hw), lambda i: (i, 0)), pl.BlockSpec((1, hw), const)],
        out_shape=[_sds((seq, hw)), _sds((1, hw))],
        compiler_params=_cp("arbitrary"),
        name="hyena_filter",
    )(feat, dist, w1p, b1[None], f1[None], w2, b2[None], f2[None], w3, decay.reshape(1, hw))


def dft_tables(seq):
    n = 2 * seq
    n2 = FFT_N2
    n1 = n // n2
    n1h = n1 // 2
    nk = -(-(n1h + 1) // SUBLANES) * SUBLANES
    k1 = jnp.arange(nk, dtype=I32)
    live = (k1 <= n1h).astype(F32)
    ang = (2.0 * jnp.pi / n1) * ((k1[:, None] * jnp.arange(n1h, dtype=I32)[None, :]) % n1).astype(F32)
    outer_fwd = jnp.concatenate([jnp.cos(ang), -jnp.sin(ang)], axis=0) * jnp.tile(live, 2)[:, None]
    t1 = jnp.arange(n1h, dtype=I32) + n1 // 4
    ang = (2.0 * jnp.pi / n1) * ((t1[:, None] * k1[None, :]) % n1).astype(F32)
    weight = live * jnp.where((k1 == 0) | (k1 == n1h), 1.0, 2.0) / n
    outer_inv = jnp.concatenate([jnp.cos(ang), -jnp.sin(ang)], axis=1) * jnp.tile(weight, 2)[None, :]
    k2 = jnp.arange(n2, dtype=I32)
    t2 = jnp.arange(n2, dtype=I32)
    phase = (t2[None, None, :] * (k2[None, :, None] * n1 + k1[:, None, None])) % n
    ang = (2.0 * jnp.pi / n) * phase.astype(F32)
    fr, fi = jnp.cos(ang), -jnp.sin(ang)
    inner = jnp.concatenate([jnp.concatenate([fr, -fi], axis=2),
                             jnp.concatenate([fi, fr], axis=2)], axis=1)
    return _split_bf16(outer_fwd), _split_bf16(outer_inv), _split_bf16(inner), _split_bf16(jnp.swapaxes(inner, 1, 2))


def _split_bf16(x):
    hi = x.astype(BF16)
    return hi, (x - hi.astype(F32)).astype(BF16)


def _dot_split(w, a):
    w_hi, w_lo = w
    a_hi, a_lo = _split_bf16(a)
    dot = functools.partial(jnp.dot, preferred_element_type=F32)
    return dot(w_hi, a_hi) + (dot(w_hi, a_lo) + dot(w_lo, a_hi))


def _dot_split_rhs(a, w):
    w_hi, w_lo = w
    a_hi, a_lo = _split_bf16(a)
    dot = functools.partial(jnp.dot, preferred_element_type=F32)
    return dot(a_hi, w_hi) + (dot(a_lo, w_hi) + dot(a_hi, w_lo))


def _every(ref, s, n):
    return ref[pl.ds(s, n, stride=FFT_ROWS), :]


def _fft_outer_body(f_ref, fl_ref, x_ref, o_ref):
    n1h = x_ref.shape[0]
    rows = f_ref.shape[0]
    x2 = x_ref.reshape(n1h * FFT_ROWS, LANES)
    o2 = o_ref.reshape(rows * FFT_ROWS, LANES)
    x = jnp.concatenate([_every(x2, s, n1h) for s in range(FFT_ROWS)], axis=1)
    r = _dot_split((f_ref[...], fl_ref[...]), x)
    for s in range(FFT_ROWS):
        o2[pl.ds(s, rows, stride=FFT_ROWS), :] = r[:, s * LANES:(s + 1) * LANES]


def fft_outer(x4, table):
    batch, n1h, n2, ch = x4.shape
    nk = table[0].shape[0] // 2
    tab = pl.BlockSpec((2 * nk, n1h), lambda b, c, l: (0, 0))
    return pl.pallas_call(
        _fft_outer_body,
        grid=(batch, n2 // FFT_ROWS, ch // LANES),
        in_specs=[tab, tab, pl.BlockSpec((None, n1h, FFT_ROWS, LANES), lambda b, c, l: (b, 0, c, l))],
        out_specs=pl.BlockSpec((None, 2, nk, FFT_ROWS, LANES), lambda b, c, l: (b, 0, 0, c, l)),
        out_shape=_sds((batch, 2, nk, n2, ch)),
        compiler_params=_cp("parallel", "parallel", "parallel"),
        name="fft_outer",
    )(*table, x4)


def _fft_filter_body(mh_ref, ml_ref, s_ref, a_ref, o_ref):
    n2 = FFT_N2
    a = a_ref[0, :, 0].reshape(2 * n2, HY_WIDTH)
    x = _dot_split((mh_ref[0], ml_ref[0]), a) * (1.0 / s_ref[...])
    o_ref[:, 0] = x.reshape(2, n2, HY_WIDTH)


def fft_filter(a5, inner, s):
    _, _, n1, n2, ch = a5.shape
    return pl.pallas_call(
        _fft_filter_body,
        grid=(n1, ch // HY_WIDTH),
        in_specs=[pl.BlockSpec((1, 2 * n2, 2 * n2), lambda k, c: (k, 0, 0)),
                  pl.BlockSpec((1, 2 * n2, 2 * n2), lambda k, c: (k, 0, 0)),
                  pl.BlockSpec((1, HY_WIDTH), lambda k, c: (0, c)),
                  pl.BlockSpec((1, 2, 1, n2, HY_WIDTH), lambda k, c: (0, 0, k, 0, c))],
        out_specs=pl.BlockSpec((2, 1, n2, HY_WIDTH), lambda k, c: (0, k, 0, c)),
        out_shape=_sds((2, n1, n2, ch)),
        compiler_params=_cp("parallel", "parallel"),
        name="fft_filter",
    )(*inner, s, a5)


def _fft_mid_body(mh_ref, ml_ref, th_ref, tl_ref, h_ref, a_ref, o_ref, *, nb):
    n2 = FFT_N2
    m = (mh_ref[0], ml_ref[0])
    mt = (th_ref[0], tl_ref[0])
    hr = h_ref[0, 0]
    hi = h_ref[1, 0]
    for b in range(nb):
        x = _dot_split(m, a_ref[b, :, 0].reshape(2 * n2, HY_WIDTH))
        xr, xi = x[:n2], x[n2:]
        y = jnp.concatenate([xr * hr - xi * hi, xr * hi + xi * hr], axis=0)
        o_ref[b, :, 0] = _dot_split(mt, y).reshape(2, n2, HY_WIDTH)


def fft_mid(a5, inner, inner_t, hf, order):
    nb, _, n1, n2, ch = a5.shape
    mat = pl.BlockSpec((1, 2 * n2, 2 * n2), lambda k: (k, 0, 0))
    blk = pl.BlockSpec((nb, 2, 1, n2, ch), lambda k: (0, 0, k, 0, 0))
    return pl.pallas_call(
        functools.partial(_fft_mid_body, nb=nb),
        grid=(n1,),
        in_specs=[mat, mat, mat, mat, pl.BlockSpec((2, 1, n2, ch), lambda k: (0, k, 0, order)), blk],
        out_specs=blk,
        out_shape=_sds(a5.shape),
        compiler_params=_cp("parallel"),
        name="fft_mid",
    )(*inner, *inner_t, hf, a5)


def _fft_inv_body(g_ref, gl_ref, b_ref, gate_ref, y_ref, fb_ref, o_ref):
    n1h, rows = g_ref.shape
    b2 = b_ref.reshape(rows * FFT_ROWS, LANES)
    gate2 = gate_ref.reshape(n1h * FFT_ROWS, LANES)
    y2 = y_ref.reshape(n1h * FFT_ROWS, LANES)
    o2 = o_ref.reshape(n1h * FFT_ROWS, LANES)
    bm = jnp.concatenate([_every(b2, s, rows) for s in range(FFT_ROWS)], axis=1)
    c = _dot_split((g_ref[...], gl_ref[...]), bm)
    fb = fb_ref[...]
    for s in range(FFT_ROWS):
        conv = c[:, s * LANES:(s + 1) * LANES]
        o2[pl.ds(s, n1h, stride=FFT_ROWS), :] = _every(gate2, s, n1h) * (conv + fb * _every(y2, s, n1h))


def fft_inv_gate(b5, table, gate4, y4, fbias):
    batch, _, nk, n2, ch = b5.shape
    n1h = table[0].shape[0]
    blk = pl.BlockSpec((None, n1h, FFT_ROWS, LANES), lambda b, c, l: (b, 0, c, l))
    tab = pl.BlockSpec((n1h, 2 * nk), lambda b, c, l: (0, 0))
    return pl.pallas_call(
        _fft_inv_body,
        grid=(batch, n2 // FFT_ROWS, ch // LANES),
        in_specs=[tab, tab,
                  pl.BlockSpec((None, 2, nk, FFT_ROWS, LANES), lambda b, c, l: (b, 0, 0, c, l)), blk, blk,
                  pl.BlockSpec((1, LANES), lambda b, c, l: (0, l))],
        out_specs=blk,
        out_shape=_sds((batch, n1h, n2, ch)),
        compiler_params=_cp("parallel", "parallel", "parallel"),
        name="fft_inv_gate",
    )(*table, b5, gate4, y4, fbias.reshape(1, ch))


def hyena_spectrum(seq, tabs, w1, b1, f1, w2, b2, f2, w3, decay):
    outer_fwd, _, inner, _ = tabs
    n1h = outer_fwd[0].shape[1]
    hw = HY_ORDER * HY_WIDTH
    h, s = hyena_filter(seq, w1, b1, f1, w2, b2, f2, w3, decay)
    return fft_filter(fft_outer(h.reshape(1, n1h, FFT_N2, hw), outer_fwd), inner, s)


def hyena_mixer(z_hy, conv_w, conv_b, hf, f_bias, tabs, batch, seq):
    outer_fwd, outer_inv, inner, inner_t = tabs
    n1h = outer_fwd[0].shape[1]
    shape4 = (batch, n1h, FFT_N2, HY_WIDTH)
    v, g0, g1 = hyena_conv3(z_hy, conv_w, conv_b, batch, seq)
    y = v.reshape(shape4)
    for order, gate in enumerate((g0, g1)):
        bm = fft_mid(fft_outer(y, outer_fwd), inner, inner_t, hf, order)
        y = fft_inv_gate(bm, outer_inv, gate.reshape(shape4), y, f_bias[order])
    return y.reshape(batch * seq, HY_WIDTH)


def _mix_body(lam_ref, h_ref, s5_ref, gqa_ref, diff_ref, hy_ref, gs5, ggqa, gsub, ghy, w_ref, o_ref, *, keep):
    lam = lam_ref[0]
    a = _rms(s5_ref[...], gs5[...])
    b = _rms(jnp.concatenate([gqa_ref[0, h] for h in range(GQA_HEADS)], axis=1), ggqa[...])
    c = jnp.concatenate(
        [_rms(diff_ref[0, 2 * h] - lam * diff_ref[0, 2 * h + 1], gsub[...]) * keep for h in range(DIFF_HEADS)],
        axis=1)
    d = _rms(hy_ref[...], ghy[...])
    mixed = jnp.concatenate([a, b, c, d], axis=1).astype(BF16)
    o_ref[...] = h_ref[...] + jnp.dot(mixed, w_ref[...], preferred_element_type=F32)


def mix_out(lam, h, y_s5, o_gqa, o_diff, y_hy, gs5, ggqa, gsub, ghy, w_out, batch, seq, lambda_init):
    tm = min(ROW_TILE, seq)
    nb = seq // tm
    row = lambda w: pl.BlockSpec((tm, w), lambda b, i: (b * nb + i, 0))
    vec = lambda w: pl.BlockSpec((1, w), lambda b, i: (0, 0))
    heads = lambda n: pl.BlockSpec((1, n, tm, HEAD_DIM), lambda b, i: (b, 0, i, 0))
    time_major = pl.BlockSpec((tm, GROUP_WIDTH), lambda b, i: (i, b))
    return pl.pallas_call(
        functools.partial(_mix_body, keep=1.0 - lambda_init),
        grid=(batch, nb),
        in_specs=[pl.BlockSpec(memory_space=pltpu.SMEM), row(D_MODEL), time_major, heads(GQA_HEADS),
                  heads(2 * DIFF_HEADS), row(GROUP_WIDTH), vec(GROUP_WIDTH), vec(GROUP_WIDTH), vec(HEAD_DIM),
                  vec(GROUP_WIDTH), pl.BlockSpec((D_MODEL, D_MODEL), lambda b, i: (0, 0))],
        out_specs=row(D_MODEL),
        out_shape=_sds((batch * seq, D_MODEL)),
        compiler_params=_cp("parallel", "parallel"),
        name="mix_out",
    )(lam, h, y_s5, o_gqa, o_diff, y_hy, gs5, ggqa, gsub, ghy, w_out)


def _router_body(h_ref, g_ref, w_ref, wl_ref, b_ref, xn_ref, ti_ref, tw_ref, cnt_ref):
    xn = _rms(h_ref[...], g_ref[...])
    xn_ref[...] = xn
    logits = _dot_split_rhs(xn, (w_ref[...], wl_ref[...])) + b_ref[...]
    lane = lax.broadcasted_iota(I32, logits.shape, 1)
    vals = logits
    tops, idxs = [], []
    hot = jnp.zeros(logits.shape, F32)
    for _ in range(TOP_K):
        m = jnp.max(vals, axis=-1, keepdims=True)
        idx = jnp.min(jnp.where(vals == m, lane, N_EXPERTS), axis=-1, keepdims=True)
        sel = lane == idx
        tops.append(m)
        idxs.append(idx)
        hot = hot + sel.astype(F32)
        vals = jnp.where(sel, -jnp.inf, vals)
    es = [jnp.exp(t - tops[0]) for t in tops]
    den = es[0] + es[1] + es[2] + es[3]
    ti_ref[...] = jnp.concatenate(idxs, axis=1)
    tw_ref[...] = jnp.concatenate([e / den for e in es], axis=1)

    @pl.when(pl.program_id(0) == 0)
    def _():
        cnt_ref[...] = jnp.zeros_like(cnt_ref)

    cnt_ref[...] += jnp.sum(hot, axis=0, keepdims=True)


def moe_router(h, g, w_router, b_router):
    t = h.shape[0]
    tm = min(ROW_TILE, t)
    const = lambda i: (0, 0)
    return pl.pallas_call(
        _router_body,
        grid=(t // tm,),
        in_specs=[pl.BlockSpec((tm, D_MODEL), lambda i: (i, 0)), pl.BlockSpec((1, D_MODEL), const),
                  pl.BlockSpec((D_MODEL, N_EXPERTS), const), pl.BlockSpec((D_MODEL, N_EXPERTS), const),
                  pl.BlockSpec((1, N_EXPERTS), const)],
        out_specs=[pl.BlockSpec((tm, D_MODEL), lambda i: (i, 0)), pl.BlockSpec((tm, TOP_K), lambda i: (i, 0)),
                   pl.BlockSpec((tm, TOP_K), lambda i: (i, 0)), pl.BlockSpec((1, N_EXPERTS), const)],
        out_shape=[_sds((t, D_MODEL)), _sds((t, TOP_K), I32), _sds((t, TOP_K)), _sds((1, N_EXPERTS))],
        compiler_params=_cp("arbitrary"),
        name="moe_router",
    )(h, g, *w_router, b_router)


def _rank_body(ti_ref, off_ref, pos_ref, stage_ref, cnt_ref, carry_ref):
    @pl.when(pl.program_id(0) == 0)
    def _():
        carry_ref[...] = jnp.zeros_like(carry_ref)

    ti = ti_ref[...]
    tm = ti.shape[0]
    lane = lax.broadcasted_iota(I32, (tm, N_EXPERTS), 1)
    hots = [lane == ti[:, k:k + 1] for k in range(TOP_K)]
    hot = sum(h.astype(F32) for h in hots)
    r = lax.broadcasted_iota(I32, (tm, tm), 0)
    c = lax.broadcasted_iota(I32, (tm, tm), 1)
    below = jnp.where(r > c, 1.0, 0.0).astype(BF16)
    local = jnp.dot(below, hot.astype(BF16), preferred_element_type=F32)
    before = local + carry_ref[...] + off_ref[...]
    pick = lambda table: jnp.concatenate(
        [jnp.sum(jnp.where(h, table, 0.0), axis=-1, keepdims=True) for h in hots], axis=1).astype(I32)
    pos_ref[...] = pick(before)
    count = jnp.sum(hot, axis=0, keepdims=True)
    er = lax.broadcasted_iota(I32, (N_EXPERTS, N_EXPERTS), 0)
    ec = lax.broadcasted_iota(I32, (N_EXPERTS, N_EXPERTS), 1)
    earlier = jnp.where(er < ec, 1.0, 0.0)
    chunks = jnp.ceil(count * (1.0 / GATHER_CHUNK))
    start = jnp.dot(chunks, earlier, precision=HI, preferred_element_type=F32) * GATHER_CHUNK
    stage_ref[...] = pick(local + start)
    cnt_ref[...] = jnp.broadcast_to(count, cnt_ref.shape)
    carry_ref[...] += count


def moe_rank(top_i, offsets):
    t = top_i.shape[0]
    tm = min(MOE_TOK_TILE, t)
    pairs = pl.BlockSpec((tm, TOP_K), lambda i: (i, 0))
    pos, stage, cnt = pl.pallas_call(
        _rank_body,
        grid=(t // tm,),
        in_specs=[pairs, pl.BlockSpec((1, N_EXPERTS), lambda i: (0, 0))],
        out_specs=[pairs, pairs, pl.BlockSpec((SUBLANES, N_EXPERTS), lambda i: (i, 0))],
        out_shape=[_sds((t, TOP_K), I32), _sds((t, TOP_K), I32), _sds((t // tm * SUBLANES, N_EXPERTS))],
        scratch_shapes=[pltpu.VMEM((1, N_EXPERTS), F32)],
        compiler_params=_cp("arbitrary"),
        name="moe_rank",
    )(top_i, offsets)
    return pos, stage, cnt[::SUBLANES]


def _row_copy(src, dst, sem):
    return pltpu.make_async_copy(src, dst, sem)


def _pad_zero_body(lt_ref, o_ref):
    del lt_ref
    o_ref[...] = jnp.zeros_like(o_ref)


def moe_pad_tiles(last_tile, n_slots):
    rows = MOE_TILE * ROW_TILES
    return pl.pallas_call(
        _pad_zero_body,
        grid_spec=pltpu.PrefetchScalarGridSpec(
            num_scalar_prefetch=1, grid=(N_EXPERTS,), in_specs=[],
            out_specs=pl.BlockSpec((rows, LANES), lambda e, lt: (lt[e], 0))),
        out_shape=_sds((n_slots * ROW_TILES, LANES)),
        compiler_params=_cp("arbitrary"),
        name="moe_pad_tiles",
    )(last_tile)


def _slot_rows(ref, slot):
    return ref.at[pl.ds(pl.multiple_of(slot * ROW_TILES, ROW_TILES), ROW_TILES)]


def _scatter_body(pos_ref, x_ref, padded_hbm, xs_hbm, buf, sem, *, tm):
    del padded_hbm
    x = x_ref[...]
    for s in range(ROW_TILES):
        buf[pl.ds(s, tm, stride=ROW_TILES), :] = x[:, s * LANES:(s + 1) * LANES]

    def start(r, c):
        for k in range(TOP_K):
            _row_copy(_slot_rows(buf, r), _slot_rows(xs_hbm, pos_ref[r * TOP_K + k]), sem).start()
        return c

    lax.fori_loop(0, tm, start, 0)
    for _ in range(TOP_K):
        _row_copy(buf, xs_hbm.at[pl.ds(0, tm * ROW_TILES)], sem).wait()


def moe_scatter(pos_flat, x, padded):
    t = x.shape[0]
    tm = min(MOE_TOK_TILE, t)
    return pl.pallas_call(
        functools.partial(_scatter_body, tm=tm),
        grid=(t // tm,),
        in_specs=[pl.BlockSpec((tm * TOP_K,), lambda i: (i,), memory_space=pltpu.SMEM),
                  pl.BlockSpec((tm, D_MODEL), lambda i: (i, 0)), pl.BlockSpec(memory_space=pl.ANY)],
        out_specs=pl.BlockSpec(memory_space=pl.ANY),
        out_shape=_sds(padded.shape),
        scratch_shapes=[pltpu.VMEM((tm * ROW_TILES, LANES), F32), pltpu.SemaphoreType.DMA(())],
        input_output_aliases={2: 0},
        compiler_params=_cp("arbitrary"),
        name="moe_scatter",
    )(pos_flat, x, padded)


def _experts_body(te_ref, nv_ref, x_ref, wgu_ref, bgu_ref, wd_ref, bd_ref, o_ref):
    del te_ref
    j = pl.program_id(0)
    dff = wd_ref.shape[1]

    @pl.when(j < nv_ref[0])
    def _():
        tm = x_ref.shape[0] // ROW_TILES
        x = jnp.concatenate([x_ref[pl.ds(s, tm, stride=ROW_TILES), :] for s in range(ROW_TILES)],
                            axis=1).astype(BF16)
        hgu = jnp.dot(x, wgu_ref[0], preferred_element_type=F32) + bgu_ref[0]
        glu = jnp.minimum(hgu[:, :dff], SWIGLU_LIMIT)
        lin = jnp.clip(hgu[:, dff:], -SWIGLU_LIMIT, SWIGLU_LIMIT)
        act = glu * jax.nn.sigmoid(SWIGLU_ALPHA * glu) * (lin + 1.0)
        y = jnp.dot(act.astype(BF16), wd_ref[0], preferred_element_type=F32) + bd_ref[0]
        for s in range(ROW_TILES):
            o_ref[pl.ds(s, tm, stride=ROW_TILES), :] = y[:, s * LANES:(s + 1) * LANES]

    @pl.when(j >= nv_ref[0])
    def _():
        o_ref[...] = jnp.zeros_like(o_ref)


def moe_experts(tile_expert, n_valid, xs, w_gu, b_gu, w_down, b_down):
    n_slots = xs.shape[0] // ROW_TILES
    tm = MOE_TILE
    dff = w_down.shape[1]
    blk = pl.BlockSpec((tm * ROW_TILES, LANES), lambda j, te, nv: (j, 0))
    return pl.pallas_call(
        _experts_body,
        grid_spec=pltpu.PrefetchScalarGridSpec(
            num_scalar_prefetch=2,
            grid=(n_slots // tm,),
            in_specs=[pl.BlockSpec((tm * ROW_TILES, LANES), lambda j, te, nv: (jnp.minimum(j, nv[0] - 1), 0)),
                      pl.BlockSpec((1, D_MODEL, 2 * dff), lambda j, te, nv: (te[j], 0, 0)),
                      pl.BlockSpec((1, 1, 2 * dff), lambda j, te, nv: (te[j], 0, 0)),
                      pl.BlockSpec((1, dff, D_MODEL), lambda j, te, nv: (te[j], 0, 0)),
                      pl.BlockSpec((1, 1, D_MODEL), lambda j, te, nv: (te[j], 0, 0))],
            out_specs=blk),
        out_shape=_sds((n_slots * ROW_TILES, LANES)),
        compiler_params=_cp("arbitrary"),
        name="moe_experts",
    )(tile_expert, n_valid, xs, w_gu, b_gu, w_down, b_down)


def _combine_body(slot_ref, w_ref, tab_ref, ys_hbm, o_ref, stage, mixed, sem, *, tm):
    chunk_rows = GATHER_CHUNK * ROW_TILES

    def chunk_copy(src_slot, dst_slot):
        return _row_copy(ys_hbm.at[pl.ds(pl.multiple_of(src_slot * ROW_TILES, ROW_TILES), chunk_rows)],
                         stage.at[pl.ds(pl.multiple_of(dst_slot * ROW_TILES, ROW_TILES), chunk_rows)], sem)

    def per_expert(e, c):
        base, chunks, start = tab_ref[e], tab_ref[N_EXPERTS + e], tab_ref[2 * N_EXPERTS + e]

        def issue(j, c2):
            chunk_copy(base + j * GATHER_CHUNK, start + j * GATHER_CHUNK).start()
            return c2

        return lax.fori_loop(0, chunks, issue, c)

    lax.fori_loop(0, N_EXPERTS, per_expert, 0)

    def wait(j, c):
        chunk_copy(0, 0).wait()
        return c

    lax.fori_loop(0, tab_ref[3 * N_EXPERTS], wait, 0)

    def mix(q, c):
        for u in range(MIX_UNROLL):
            r = q * MIX_UNROLL + u
            acc = None
            for k in range(TOP_K):
                p = r * TOP_K + k
                term = w_ref[p] * stage[pl.ds(pl.multiple_of(slot_ref[p] * ROW_TILES, ROW_TILES), ROW_TILES), :]
                acc = term if acc is None else acc + term
            mixed[pl.ds(pl.multiple_of(r * ROW_TILES, ROW_TILES), ROW_TILES), :] = acc
        return c

    lax.fori_loop(0, tm // MIX_UNROLL, mix, 0)
    o_ref[...] = jnp.concatenate([mixed[pl.ds(s, tm, stride=ROW_TILES), :] for s in range(ROW_TILES)], axis=1)


def moe_combine(stage_flat, w_flat, table, ys, t):
    tm = min(MOE_TOK_TILE, t)
    smem = pl.BlockSpec((tm * TOP_K,), lambda i: (i,), memory_space=pltpu.SMEM)
    stage_slots = tm * TOP_K + N_EXPERTS * GATHER_CHUNK
    return pl.pallas_call(
        functools.partial(_combine_body, tm=tm),
        grid=(t // tm,),
        in_specs=[smem, smem, pl.BlockSpec((LANES,), lambda i: (i,), memory_space=pltpu.SMEM),
                  pl.BlockSpec(memory_space=pl.ANY)],
        out_specs=pl.BlockSpec((tm, D_MODEL), lambda i: (i, 0)),
        out_shape=_sds((t, D_MODEL)),
        scratch_shapes=[pltpu.VMEM((stage_slots * ROW_TILES, LANES), F32),
                        pltpu.VMEM((tm * ROW_TILES, LANES), F32), pltpu.SemaphoreType.DMA(())],
        compiler_params=_cp("arbitrary"),
        name="moe_combine",
    )(stage_flat, w_flat, table, ys)


def gather_table(cnt_tile, offsets):
    base = offsets + jnp.cumsum(cnt_tile, axis=0) - cnt_tile
    chunks = jnp.ceil(cnt_tile / GATHER_CHUNK)
    start = (jnp.cumsum(chunks, axis=1) - chunks) * GATHER_CHUNK
    total = jnp.sum(chunks, axis=1, keepdims=True)
    pad = jnp.zeros((cnt_tile.shape[0], LANES - 3 * N_EXPERTS - 1), F32)
    return jnp.concatenate([base, chunks, start, total, pad], axis=1).astype(I32).reshape(-1)


def moe(h, g, w_router, b_router, w_gu, b_gu, w_down, b_down):
    t = h.shape[0]
    xn, top_i, top_w, counts = moe_router(h, g, w_router, b_router)
    n_tiles = (t * TOP_K) // MOE_TILE + N_EXPERTS + 1
    padded = jnp.ceil(counts[0] / MOE_TILE) * MOE_TILE
    ends = jnp.cumsum(padded)
    offsets = (ends - padded)[None]
    tile_start = jnp.arange(n_tiles, dtype=F32) * MOE_TILE
    tile_expert = jnp.minimum(jnp.sum(ends[None, :] <= tile_start[:, None], axis=1), N_EXPERTS - 1).astype(I32)
    n_valid = (ends[-1:] / MOE_TILE).astype(I32)
    own_last = jnp.where(padded > 0, ends / MOE_TILE - 1, -1.0)
    filled = lax.cummax(own_last, axis=0)
    first = jnp.min(jnp.where(padded > 0, own_last, jnp.inf))
    last_tile = jnp.where(filled < 0, first, filled).astype(I32)
    pos, stage, cnt_tile = moe_rank(top_i, offsets)
    xs = moe_scatter(pos.reshape(t * TOP_K), xn, moe_pad_tiles(last_tile, n_tiles * MOE_TILE))
    ys = moe_experts(tile_expert, n_valid, xs, w_gu, b_gu, w_down, b_down)
    flat = lambda a: a.reshape(t * TOP_K)
    return moe_combine(flat(stage), flat(top_w), gather_table(cnt_tile, offsets), ys, t)


def _ple_body(h_ref, m_ref, p_ref, wp_ref, wg_ref, o_ref):
    h = h_ref[...] + m_ref[...]
    e = jnp.dot(p_ref[...].astype(BF16), wp_ref[...], preferred_element_type=F32)
    gate = jnp.dot(_rms(h).astype(BF16), wg_ref[...], preferred_element_type=F32)
    o_ref[...] = h + e * jax.nn.sigmoid(gate)


def ple(h, moe_out, p, w_ple, w_gate):
    t = h.shape[0]
    tm = min(ROW_TILE, t)
    row = pl.BlockSpec((tm, D_MODEL), lambda i: (i, 0))
    return pl.pallas_call(
        _ple_body,
        grid=(t // tm,),
        in_specs=[row, row, pl.BlockSpec((tm, PLE_DIM), lambda i: (i, 0)),
                  pl.BlockSpec((PLE_DIM, D_MODEL), lambda i: (0, 0)),
                  pl.BlockSpec((D_MODEL, D_MODEL), lambda i: (0, 0))],
        out_specs=row,
        out_shape=_sds((t, D_MODEL)),
        compiler_params=_cp("parallel"),
        name="ple",
    )(h, moe_out, p, w_ple, w_gate)


def _deinterleave(n):
    return np.concatenate([np.arange(0, n, 2), np.arange(1, n, 2)])


def _halves(w):
    pairs = w.reshape(*w.shape[:-1], w.shape[-1] // 2, 2)
    return jnp.concatenate([pairs[..., 0], pairs[..., 1]], axis=-1)


def prepare_layer(i, prm):
    pair = _deinterleave(HEAD_DIM)
    cols = np.arange(IN_COLS)
    for h in range(GQA_HEADS + GQA_KV_HEADS):
        lo = OFF_GQA + h * HEAD_DIM
        cols[lo:lo + HEAD_DIM] = lo + pair
    lam = (jnp.exp(jnp.sum(prm["diff_lam_q1"][i] * prm["diff_lam_k1"][i]))
           - jnp.exp(jnp.sum(prm["diff_lam_q2"][i] * prm["diff_lam_k2"][i])) + _lambda_init(i))
    return dict(
        norm1_g=prm["norm1_g"][i][None],
        w_in=prm["w_in"][i][:, cols].astype(BF16),
        s5=s5_tables(prm["s5_lam_re"][i], prm["s5_lam_im"][i], prm["s5_log_dt"][i], prm["s5_b_re"][i],
                     prm["s5_b_im"][i], prm["s5_c_re"][i], prm["s5_c_im"][i]),
        s5_d=prm["s5_d"][i][None],
        s5_w_glu=prm["s5_w_glu"][i].astype(BF16),
        s5_b_glu=prm["s5_b_glu"][i][None],
        gqa_q_g=prm["gqa_q_g"][i][pair][None],
        gqa_k_g=prm["gqa_k_g"][i][pair][None],
        diff_q_g=prm["diff_q_g"][i][None],
        diff_k_g=prm["diff_k_g"][i][None],
        diff_lam=lam.reshape(1).astype(F32),
        diff_subln_g=prm["diff_subln_g"][i][None],
        out_g_s5=prm["out_g_s5"][i][None],
        out_g_gqa=prm["out_g_gqa"][i][None],
        out_g_hy=prm["out_g_hy"][i][None],
        w_out=prm["w_out"][i].astype(BF16),
        norm2_g=prm["norm2_g"][i][None],
        w_router=_split_bf16(prm["w_router"][i]),
        b_router=prm["b_router"][i][None],
        w_gu=_halves(prm["w_gu"][i]).astype(BF16),
        b_gu=_halves(prm["b_gu"][i])[:, None, :],
        w_down=prm["w_down"][i].astype(BF16),
        b_down=prm["b_down"][i][:, None, :],
        w_ple=prm["w_ple"][i].astype(BF16),
        w_ple_gate=prm["w_ple_gate"][i].astype(BF16),
    )


def _lambda_init(i):
    return 0.8 - 0.6 * math.exp(-0.3 * i)


def run_trunk(x, p, prm, layers):
    batch, seq, _ = x.shape
    t = batch * seq
    cos_t, sin_t = rope_tables(seq)
    tabs = dft_tables(seq)
    bias = diff_bias_tiles(prm["rel_bias"], min(ATTN_UNIT, seq))
    h = x.reshape(t, D_MODEL)
    for i, lw in enumerate(layers):
        z_s5, z_gqa, z_diff, z_hy = in_proj(h, lw["norm1_g"], lw["w_in"], batch, seq)
        y_s5 = s5_mixer(z_s5, lw["s5"], lw["s5_d"], lw["s5_w_glu"], lw["s5_b_glu"], batch, seq)
        rep = GQA_HEADS // GQA_KV_HEADS
        q, k, v = attn_prep(z_gqa, lw["gqa_q_g"], lw["gqa_k_g"], cos_t, sin_t, batch, seq,
                            GQA_HEADS, GQA_KV_HEADS, GQA_KV_HEADS, True, lambda hq: hq // rep)
        o_gqa = flash_attention(q, k, v, lambda hq: hq // rep)
        q, k, v = attn_prep(z_diff, lw["diff_q_g"], lw["diff_k_g"], cos_t, sin_t, batch, seq,
                            2 * DIFF_HEADS, 2 * DIFF_HEADS, DIFF_HEADS, False, lambda hq: hq)
        o_diff = flash_attention(q, k, v, lambda hq: hq // 2, bias)
        hf = hyena_spectrum(seq, tabs, prm["hy_w1"][i], prm["hy_b1"][i], prm["hy_freq1"][i], prm["hy_w2"][i],
                            prm["hy_b2"][i], prm["hy_freq2"][i], prm["hy_w3"][i], prm["hy_decay"][i])
        y_hy = hyena_mixer(z_hy, prm["hy_conv_w"][i], prm["hy_conv_b"][i][None], hf, prm["hy_bias"][i],
                           tabs, batch, seq)
        h = mix_out(lw["diff_lam"], h, y_s5, o_gqa, o_diff, y_hy, lw["out_g_s5"], lw["out_g_gqa"],
                    lw["diff_subln_g"], lw["out_g_hy"], lw["w_out"], batch, seq, _lambda_init(i))
        m = moe(h, lw["norm2_g"], lw["w_router"], lw["b_router"], lw["w_gu"], lw["b_gu"], lw["w_down"],
                lw["b_down"])
        h = ple(h, m, p[i].reshape(t, PLE_DIM), lw["w_ple"], lw["w_ple_gate"])
    return h.reshape(batch, seq, D_MODEL)


def kernel(x_prompt, x_sample, p_prompt, p_sample, rel_bias, norm1_g, w_in, s5_lam_re, s5_lam_im, s5_log_dt,
           s5_b_re, s5_b_im, s5_c_re, s5_c_im, s5_d, s5_w_glu, s5_b_glu, gqa_q_g, gqa_k_g, diff_q_g, diff_k_g,
           diff_lam_q1, diff_lam_k1, diff_lam_q2, diff_lam_k2, diff_subln_g, hy_conv_w, hy_conv_b, hy_w1, hy_b1,
           hy_freq1, hy_w2, hy_b2, hy_freq2, hy_w3, hy_decay, hy_bias, out_g_s5, out_g_gqa, out_g_hy, w_out,
           norm2_g, w_router, b_router, w_gu, b_gu, w_down, b_down, w_ple, w_ple_gate):
    prm = dict(rel_bias=rel_bias, norm1_g=norm1_g, w_in=w_in, s5_lam_re=s5_lam_re, s5_lam_im=s5_lam_im,
               s5_log_dt=s5_log_dt, s5_b_re=s5_b_re, s5_b_im=s5_b_im, s5_c_re=s5_c_re, s5_c_im=s5_c_im,
               s5_d=s5_d, s5_w_glu=s5_w_glu, s5_b_glu=s5_b_glu, gqa_q_g=gqa_q_g, gqa_k_g=gqa_k_g,
               diff_q_g=diff_q_g, diff_k_g=diff_k_g, diff_lam_q1=diff_lam_q1, diff_lam_k1=diff_lam_k1,
               diff_lam_q2=diff_lam_q2, diff_lam_k2=diff_lam_k2, diff_subln_g=diff_subln_g,
               hy_conv_w=hy_conv_w, hy_conv_b=hy_conv_b, hy_w1=hy_w1, hy_b1=hy_b1, hy_freq1=hy_freq1,
               hy_w2=hy_w2, hy_b2=hy_b2, hy_freq2=hy_freq2, hy_w3=hy_w3, hy_decay=hy_decay, hy_bias=hy_bias,
               out_g_s5=out_g_s5, out_g_gqa=out_g_gqa, out_g_hy=out_g_hy, w_out=w_out, norm2_g=norm2_g,
               w_router=w_router, b_router=b_router, w_gu=w_gu, b_gu=b_gu, w_down=w_down, b_down=b_down,
               w_ple=w_ple, w_ple_gate=w_ple_gate)
    layers = [prepare_layer(i, prm) for i in range(DEPTH)]
    return (run_trunk(x_prompt, p_prompt, prm, layers), run_trunk(x_sample, p_sample, prm, layers))
```

```python
import functools
import math

import numpy as np
import jax
import jax.numpy as jnp
from jax import lax
from jax.experimental import pallas as pl
from jax.experimental.pallas import tpu as pltpu

F32 = jnp.float32
BF16 = jnp.bfloat16
I32 = jnp.int32
HI = lax.Precision.HIGHEST

D_MODEL = 1024
DEPTH = 2
GROUP_WIDTH = 256
HEAD_DIM = 64
S5_GROUP_DIM = 16
S5_GROUPS = 16
S5_STATE = 64
S5_NSTATE = S5_GROUPS * S5_STATE
GQA_HEADS = 4
GQA_KV_HEADS = 2
DIFF_HEADS = 4
HY_WIDTH = 256
HY_ORDER = 2
HY_BANDS = 16
HY_EMB = 2 * HY_BANDS + 1
HY_FFN = 64
N_EXPERTS = 32
TOP_K = 4
SWIGLU_LIMIT = 7.0
SWIGLU_ALPHA = 1.702
NUM_BUCKETS = 32
MAX_DISTANCE = 128
GRID_W = 64
ROPE_THETA = 10000.0
ROPE_AXIS_DIM = HEAD_DIM // 2
PLE_DIM = 256
NORM_EPS = 1e-6
ATTN_SCALE = HEAD_DIM ** -0.5
LOG2E = math.log2(math.e)

S5_COLS = GROUP_WIDTH
GQA_QW = GQA_HEADS * HEAD_DIM
GQA_KW = GQA_KV_HEADS * HEAD_DIM
GQA_COLS = GQA_QW + 2 * GQA_KW
DIFF_QW = DIFF_HEADS * 2 * HEAD_DIM
DIFF_COLS = 2 * DIFF_QW + DIFF_HEADS * HEAD_DIM
HY_COLS = (HY_ORDER + 1) * HY_WIDTH
OFF_GQA = S5_COLS
OFF_DIFF = OFF_GQA + GQA_COLS
OFF_HY = OFF_DIFF + DIFF_COLS
IN_COLS = OFF_HY + HY_COLS
IN_SPLITS = ((0, OFF_GQA), (OFF_GQA, OFF_DIFF), (OFF_DIFF, OFF_HY), (OFF_HY, IN_COLS))

LANES = 128
SUBLANES = 8
ROW_TILES = D_MODEL // LANES
FFT_N2 = 128
FFT_ROWS = 32
VMEM_LIMIT = 48 * 1024 * 1024

ROW_TILE = 256
ATTN_Q_TILE = 2048
ATTN_K_TILE = 2048
ATTN_UNIT = 512
ATTN_SUB = 512
ATTN_SUB_PLAIN = 1024
S5_TIME_BLOCK = 64
MOE_TILE = 512
MOE_TOK_TILE = 256
MIX_UNROLL = 4
GATHER_CHUNK = 8


def _cp(*sem):
    return pltpu.CompilerParams(dimension_semantics=sem, vmem_limit_bytes=VMEM_LIMIT)


def _sds(shape, dtype=F32):
    return jax.ShapeDtypeStruct(shape, dtype)


def _rms(x, g=None):
    y = x * lax.rsqrt(jnp.mean(x * x, axis=-1, keepdims=True) + NORM_EPS)
    return y if g is None else y * g


def _inproj_body(x_ref, g_ref, w_ref, *o_refs):
    u = _rms(x_ref[...], g_ref[...]).astype(BF16)
    for o_ref, (lo, hi) in zip(o_refs, IN_SPLITS):
        o_ref[...] = jnp.dot(u, w_ref[:, lo:hi], preferred_element_type=F32)


def in_proj(h, g, w, batch, seq):
    t = h.shape[0]
    tm = min(ROW_TILE, seq)
    nb = seq // tm
    row_major = lambda i: (i, 0)
    time_major = lambda i: (i % nb, i // nb)
    widths = [hi - lo for lo, hi in IN_SPLITS]
    return pl.pallas_call(
        _inproj_body,
        grid=(t // tm,),
        in_specs=[pl.BlockSpec((tm, D_MODEL), row_major),
                  pl.BlockSpec((1, D_MODEL), lambda i: (0, 0)),
                  pl.BlockSpec((D_MODEL, IN_COLS), lambda i: (0, 0))],
        out_specs=[pl.BlockSpec((tm, widths[0]), time_major)] + [pl.BlockSpec((tm, w_), row_major) for w_ in widths[1:]],
        out_shape=[_sds((seq, batch * widths[0]))] + [_sds((t, w_)) for w_ in widths[1:]],
        compiler_params=_cp("parallel"),
        name="in_proj",
    )(h, g, w)


def s5_tables(lam_re, lam_im, log_dt, b_re, b_im, c_re, c_im):
    lr = jnp.minimum(lam_re, -1e-4)
    li = lam_im
    dt = jnp.exp(log_dt)[..., None]
    mag = jnp.exp(lr * dt)
    ar = mag * jnp.cos(li * dt)
    ai = mag * jnp.sin(li * dt)
    den = lr * lr + li * li
    cr = ((ar - 1.0) * lr + ai * li) / den
    ci = (ai * lr - (ar - 1.0) * li) / den
    bbr = cr[..., None] * b_re - ci[..., None] * b_im
    bbi = cr[..., None] * b_im + ci[..., None] * b_re
    eye = jnp.eye(S5_GROUPS, dtype=F32)
    bb = jnp.stack([bbr, bbi], axis=1)
    wb = jnp.einsum("dpgnh,gk->dghpkn", bb, eye).reshape(2, S5_COLS, 2 * S5_NSTATE)
    cc = jnp.stack([c_re, -c_im], axis=0)
    wc = jnp.einsum("pdghn,gk->dpgnkh", cc, eye).reshape(2, 2 * S5_NSTATE, S5_COLS)
    a_re = jnp.repeat(ar.reshape(2, S5_NSTATE), 4, axis=0)
    a_im = jnp.repeat(ai.reshape(2, S5_NSTATE), 4, axis=0)
    return wb.astype(BF16), wc.astype(BF16), a_re, a_im


def _s5_body(uf_ref, ub_ref, wb_ref, wc_ref, ar_ref, ai_ref, yf_ref, yb_ref, bf_ref, bb_ref, xr_ref, xi_ref, *, tb):
    @pl.when(pl.program_id(0) == 0)
    def _():
        xr_ref[...] = jnp.zeros_like(xr_ref)
        xi_ref[...] = jnp.zeros_like(xi_ref)

    bf_ref[...] = jnp.dot(uf_ref[...].astype(BF16), wb_ref[0], preferred_element_type=F32)
    bb_ref[...] = jnp.dot(ub_ref[...].astype(BF16), wb_ref[1], preferred_element_type=F32)
    ar = ar_ref[...]
    ai = ai_ref[...]
    low = lax.broadcasted_iota(I32, (SUBLANES, 1), 0) < 4
    half = SUBLANES // 2
    re = slice(0, S5_NSTATE)
    im = slice(S5_NSTATE, 2 * S5_NSTATE)

    def pair(m, carry):
        xr, xi = carry
        rf = pl.ds(pl.multiple_of(m * SUBLANES, SUBLANES), SUBLANES)
        rb = pl.ds(pl.multiple_of((tb // 2 - 1 - m) * SUBLANES, SUBLANES), SUBLANES)
        f_r, f_i, b_r, b_i = bf_ref[rf, re], bf_ref[rf, im], bb_ref[rb, re], bb_ref[rb, im]
        in_r = jnp.where(low, f_r, b_r)
        in_i = jnp.where(low, f_i, b_i)
        x1r = ar * xr - ai * xi + in_r
        x1i = ar * xi + ai * xr + in_i
        in_r = pltpu.roll(jnp.where(low, b_r, f_r), half, 0)
        in_i = pltpu.roll(jnp.where(low, b_i, f_i), half, 0)
        x2r = ar * x1r - ai * x1i + in_r
        x2i = ar * x1i + ai * x1r + in_i
        s2r = pltpu.roll(x2r, half, 0)
        s2i = pltpu.roll(x2i, half, 0)
        bf_ref[rf, re] = jnp.where(low, x1r, s2r)
        bf_ref[rf, im] = jnp.where(low, x1i, s2i)
        bb_ref[rb, re] = jnp.where(low, s2r, x1r)
        bb_ref[rb, im] = jnp.where(low, s2i, x1i)
        return x2r, x2i

    xr, xi = lax.fori_loop(0, tb // 2, pair, (xr_ref[...], xi_ref[...]))
    xr_ref[...] = xr
    xi_ref[...] = xi
    yf_ref[...] = jnp.dot(bf_ref[...].astype(BF16), wc_ref[0], preferred_element_type=F32)
    yb_ref[...] = jnp.dot(bb_ref[...].astype(BF16), wc_ref[1], preferred_element_type=F32)


def s5_scan(u, wb, wc, a_re, a_im, seq):
    tb = min(S5_TIME_BLOCK, seq)
    nb = seq // tb
    rows = tb * 4
    fwd = pl.BlockSpec((rows, S5_COLS), lambda i: (i, 0))
    bwd = pl.BlockSpec((rows, S5_COLS), lambda i: (nb - 1 - i, 0))
    return pl.pallas_call(
        functools.partial(_s5_body, tb=tb),
        grid=(nb,),
        in_specs=[fwd, bwd,
                  pl.BlockSpec((2, S5_COLS, 2 * S5_NSTATE), lambda i: (0, 0, 0)),
                  pl.BlockSpec((2, 2 * S5_NSTATE, S5_COLS), lambda i: (0, 0, 0)),
                  pl.BlockSpec((SUBLANES, S5_NSTATE), lambda i: (0, 0)),
                  pl.BlockSpec((SUBLANES, S5_NSTATE), lambda i: (0, 0))],
        out_specs=[fwd, bwd],
        out_shape=[_sds((seq * 4, S5_COLS))] * 2,
        scratch_shapes=[pltpu.VMEM((rows, 2 * S5_NSTATE), F32), pltpu.VMEM((rows, 2 * S5_NSTATE), F32),
                        pltpu.VMEM((SUBLANES, S5_NSTATE), F32), pltpu.VMEM((SUBLANES, S5_NSTATE), F32)],
        compiler_params=_cp("arbitrary"),
        name="s5_scan",
    )(u, u, wb, wc, a_re, a_im)


def _s5_out_body(yf_ref, yb_ref, u_ref, d_ref, w_ref, b_ref, o_ref):
    y = jax.nn.gelu(yf_ref[...] + yb_ref[...] + u_ref[...] * d_ref[...])
    gate = jnp.dot(y.astype(BF16), w_ref[...], preferred_element_type=F32) + b_ref[...]
    o_ref[...] = y * jax.nn.sigmoid(gate)


def s5_out(yf, yb, u, d, w_glu, b_glu):
    t = u.shape[0]
    tm = min(ROW_TILE, t)
    row = pl.BlockSpec((tm, S5_COLS), lambda i: (i, 0))
    vec = pl.BlockSpec((1, S5_COLS), lambda i: (0, 0))
    return pl.pallas_call(
        _s5_out_body,
        grid=(t // tm,),
        in_specs=[row, row, row, vec, pl.BlockSpec((S5_COLS, S5_COLS), lambda i: (0, 0)), vec],
        out_specs=row,
        out_shape=_sds((t, S5_COLS)),
        compiler_params=_cp("parallel"),
        name="s5_out",
    )(yf, yb, u, d, w_glu, b_glu)


def s5_mixer(z_s5, tabs, d, w_glu, b_glu, batch, seq):
    assert batch == 4 and seq % 2 == 0, "the scan packs 4 sequences x 2 directions onto the 8 sublanes"
    u = z_s5.reshape(seq * batch, S5_COLS)
    yf, yb = s5_scan(u, *tabs, seq)
    return s5_out(yf, yb, u, d, w_glu, b_glu).reshape(seq, batch * S5_COLS)


def _prep_body(z_ref, qg_ref, kg_ref, c_ref, s_ref, q_ref, k_ref, v_ref, *, nq, nk, nv, rope, k_src):
    tm = z_ref.shape[0]
    lane = lax.broadcasted_iota(I32, (tm, LANES), 1)
    low = lane < HEAD_DIM

    def norm_pair(x, g):
        sq = x * x
        ss_lo = jnp.sum(jnp.where(low, sq, 0.0), axis=-1, keepdims=True)
        ss_hi = jnp.sum(jnp.where(low, 0.0, sq), axis=-1, keepdims=True)
        y = x * lax.rsqrt(jnp.where(low, ss_lo, ss_hi) * (1.0 / HEAD_DIM) + NORM_EPS) * g
        if rope:
            half = HEAD_DIM // 2
            partner = jnp.where(lane % HEAD_DIM < half, pltpu.roll(y, LANES - half, 1), pltpu.roll(y, half, 1))
            y = y * c_ref[...] + partner * s_ref[...]
        return y

    qg = qg_ref[...]
    kg = kg_ref[...]
    for p in range(nq // 2):
        q_ref[0, p] = (norm_pair(z_ref[:, p * LANES:(p + 1) * LANES], qg) * (ATTN_SCALE * LOG2E)).astype(BF16)
    k_lo = nq * HEAD_DIM
    pairs = [norm_pair(z_ref[:, k_lo + p * LANES:k_lo + (p + 1) * LANES], kg) for p in range(nk // 2)]
    for h in range(nq):
        src = k_src(h)
        x = pairs[src // 2]
        if src % 2 != h % 2:
            x = pltpu.roll(x, HEAD_DIM, 1)
        k_ref[0, h] = jnp.where(low if h % 2 == 0 else jnp.logical_not(low), x, 0.0).astype(BF16)
    ones_col = jnp.where(lax.broadcasted_iota(I32, (tm, LANES - HEAD_DIM), 1) == 0, 1.0, 0.0)
    for h in range(nv):
        lo = (nq + nk + h) * HEAD_DIM
        v_ref[0, h] = jnp.concatenate([z_ref[:, lo:lo + HEAD_DIM], ones_col], axis=1).astype(BF16)


def attn_prep(z, qg, kg, cos_t, sin_t, batch, seq, nq, nk, nv, rope, k_src):
    width = (nq + nk + nv) * HEAD_DIM
    tm = min(ROW_TILE, seq)
    nb = seq // tm
    vec = pl.BlockSpec((1, LANES), lambda b, i: (0, 0))
    tab = pl.BlockSpec((tm, LANES), lambda b, i: (i, 0))
    heads = lambda n: pl.BlockSpec((1, n, tm, LANES), lambda b, i: (b, 0, i, 0))
    return pl.pallas_call(
        functools.partial(_prep_body, nq=nq, nk=nk, nv=nv, rope=rope, k_src=k_src),
        grid=(batch, nb),
        in_specs=[pl.BlockSpec((tm, width), lambda b, i: (b * nb + i, 0)), vec, vec, tab, tab],
        out_specs=[heads(nq // 2), heads(nq), heads(nv)],
        out_shape=[_sds((batch, nq // 2, seq, LANES), BF16), _sds((batch, nq, seq, LANES), BF16),
                   _sds((batch, nv, seq, LANES), BF16)],
        compiler_params=_cp("parallel", "parallel"),
        name="attn_prep",
    )(z, jnp.tile(qg, (1, 2)), jnp.tile(kg, (1, 2)), cos_t, sin_t)


def _flash_body(q_ref, k_ref, v_ref, *rest, unit, sub, sub_far, has_bias):
    if has_bias:
        b_ref, o_ref, m_ref, acc_ref = rest
    else:
        o_ref, m_ref, acc_ref = rest
    i = pl.program_id(2)
    j = pl.program_id(3)
    tq = q_ref.shape[2]
    tk = k_ref.shape[2]
    q_units = tq // unit
    k_units = tk // unit

    @pl.when(j == 0)
    def _():
        m_ref[...] = jnp.full_like(m_ref, -jnp.inf)
        acc_ref[...] = jnp.zeros_like(acc_ref)

    def attend(rows, offset=None, far_sel=None):
        k = k_ref[0, 0]
        v = v_ref[0, 0]
        for g in range(tq // rows):
            r0 = g * rows
            s = lax.dot_general(q_ref[0, 0, r0:r0 + rows, :], k, (((1,), (1,)), ((), ())),
                                preferred_element_type=F32)
            m_prev = m_ref[r0:r0 + rows, :]
            kinds = [None] * k_units
            if offset is not None:
                tile_of = [offset * k_units + c - r0 // unit for c in range(k_units)]
                kinds = [None if abs(d) <= 1 else (0 if d < 0 else 4) for d in tile_of]
                s = jnp.concatenate(
                    [s[:, c * unit:(c + 1) * unit] + b_ref[0, d + 2, r0 % unit:r0 % unit + rows, :]
                     if abs(d) <= 1 else s[:, c * unit:(c + 1) * unit] for c, d in enumerate(tile_of)], axis=1)
            elif far_sel is not None:
                kinds = [far_sel] * k_units
            consts = {kind: b_ref[0, kind, 0:1, 0:LANES] for kind in set(kinds) if kind is not None}
            per_unit = unit // LANES
            blocks = [s[:, b * LANES:(b + 1) * LANES] for b in range(tk // LANES)]
            m_new = m_prev
            for kind in sorted(set(kinds), key=str):
                members = [blk for b, blk in enumerate(blocks) if kinds[b // per_unit] == kind]
                bmax = members[0]
                for blk in members[1:]:
                    bmax = jnp.maximum(bmax, blk)
                kind_max = jnp.max(bmax, axis=-1, keepdims=True)
                m_new = jnp.maximum(m_new, kind_max if kind is None else kind_max + consts[kind])
            shifts = [m_new if kind is None else m_new - consts[kind] for kind in kinds]
            alpha = jnp.exp2(m_prev - m_new)
            p = jnp.concatenate([jnp.exp2(blk - shifts[b // per_unit]) for b, blk in enumerate(blocks)],
                                axis=1).astype(BF16)
            acc_ref[r0:r0 + rows, :] = alpha * acc_ref[r0:r0 + rows, :] + jnp.dot(p, v, preferred_element_type=F32)
            m_ref[r0:r0 + rows, :] = m_new

    if has_bias:
        assert q_units == k_units and sub == unit, "the near-diagonal variants assume square steps of whole units"
        pl.when(j - i >= 2)(lambda: attend(sub_far, far_sel=4))
        pl.when(j - i <= -2)(lambda: attend(sub_far, far_sel=0))
        for offset in (-1, 0, 1):
            pl.when(j - i == offset)(functools.partial(attend, sub, offset=offset))
    else:
        attend(sub)

    @pl.when(j == pl.num_programs(3) - 1)
    def _():
        acc = acc_ref[...]
        o_ref[0, 0] = acc[:, :HEAD_DIM] / acc[:, HEAD_DIM:HEAD_DIM + 1]


def flash_attention(q, k, v, v_of_q, bias=None):
    batch, nq, seq, _ = k.shape
    tq = min(ATTN_Q_TILE, seq)
    tk = min(ATTN_K_TILE, seq)
    unit = min(ATTN_UNIT, seq)
    sub = min(ATTN_SUB if bias is not None else ATTN_SUB_PLAIN, tq)
    sub_far = min(ATTN_SUB_PLAIN, tq)
    in_specs = [pl.BlockSpec((1, 1, tq, LANES), lambda b, h, i, j: (b, h // 2, i, 0)),
                pl.BlockSpec((1, 1, tk, LANES), lambda b, h, i, j: (b, h, j, 0)),
                pl.BlockSpec((1, 1, tk, LANES), lambda b, h, i, j: (b, v_of_q(h), j, 0))]
    args = [q, k, v]
    if bias is not None:
        assert bias.shape[2] == unit
        in_specs.append(pl.BlockSpec((1, 5, unit, unit), lambda b, h, i, j: (v_of_q(h), 0, 0, 0)))
        args.append(bias)
    return pl.pallas_call(
        functools.partial(_flash_body, unit=unit, sub=sub, sub_far=sub_far, has_bias=bias is not None),
        grid=(batch, nq, seq // tq, seq // tk),
        in_specs=in_specs,
        out_specs=pl.BlockSpec((1, 1, tq, HEAD_DIM), lambda b, h, i, j: (b, h, i, 0)),
        out_shape=_sds((batch, nq, seq, HEAD_DIM)),
        scratch_shapes=[pltpu.VMEM((tq, LANES), F32), pltpu.VMEM((tq, LANES), F32)],
        compiler_params=_cp("parallel", "parallel", "parallel", "arbitrary"),
        name="flash_bias" if bias is not None else "flash",
    )(*args)


def _t5_bucket_table(t):
    rel = np.arange(-(3 * t - 1), 3 * t)
    half = NUM_BUCKETS // 2
    max_exact = half // 2
    ret = np.where(rel > 0, half, 0)
    n = np.abs(rel)
    nf = np.maximum(n, 1).astype(np.float64)
    large = max_exact + (np.log(nf / max_exact) / math.log(MAX_DISTANCE / max_exact) * (half - max_exact)).astype(np.int64)
    large = np.minimum(large, half - 1)
    return ret + np.where(n < max_exact, n, large)


def diff_bias_tiles(rel_bias, t):
    assert t >= MAX_DISTANCE, "offsets of two or more tiles must lie in the saturated buckets"
    heads = rel_bias.shape[1]
    vec = rel_bias[jnp.asarray(_t5_bucket_table(t), dtype=I32)].T * LOG2E
    tiles = []
    for d in range(-2, 3):
        lo = d * t + 2 * t
        w = vec[:, lo:lo + 2 * t - 1]
        u = jnp.concatenate([w[:, t - 1:], jnp.zeros((heads, 1), F32), w[:, :t - 1]], axis=1)
        skew = jnp.tile(u, (1, t))[:, :t * (2 * t - 1)].reshape(heads, t, 2 * t - 1)
        tiles.append(skew[:, :, :t])
    return jnp.stack(tiles, axis=1)


def rope_tables(seq):
    rows = seq // GRID_W
    row = jnp.repeat(jnp.arange(rows, dtype=F32), GRID_W)
    col = jnp.tile(jnp.arange(GRID_W, dtype=F32), rows)
    freq = ROPE_THETA ** (-jnp.arange(0, ROPE_AXIS_DIM, 2, dtype=F32) / ROPE_AXIS_DIM)
    ang = jnp.concatenate([row[:, None] * freq, col[:, None] * freq], -1)
    cos, sin = jnp.cos(ang), jnp.sin(ang)
    return jnp.concatenate([cos, cos] * 2, -1), jnp.concatenate([-sin, sin] * 2, -1)


def _conv3_body(x0, x1, x2, w0, w1, w2, b0, b1, b2, o0, o1, o2):
    for x_ref, w_ref, b_ref, o_ref in ((x0, w0, b0, o0), (x1, w1, b1, o1), (x2, w2, b2, o2)):
        x = x_ref[0]
        n = x.shape[0]
        t = lax.broadcasted_iota(I32, (n, 1), 0)
        prev = jnp.where(t == 0, 0.0, pltpu.roll(x, 1, 0))
        nxt = jnp.where(t == n - 1, 0.0, pltpu.roll(x, n - 1, 0))
        o_ref[0] = w_ref[0:1] * prev + w_ref[1:2] * x + w_ref[2:3] * nxt + b_ref[...]


def hyena_conv3(z_hy, conv_w, conv_b, batch, seq):
    z = z_hy.reshape(batch, seq, HY_COLS)
    nc = HY_WIDTH // LANES
    x_spec = lambda g: pl.BlockSpec((1, seq, LANES), lambda b, c: (b, 0, g * nc + c))
    w_spec = lambda g: pl.BlockSpec((3, LANES), lambda b, c: (0, g * nc + c))
    b_spec = lambda g: pl.BlockSpec((1, LANES), lambda b, c: (0, g * nc + c))
    out = pl.BlockSpec((1, seq, LANES), lambda b, c: (b, 0, c))
    return pl.pallas_call(
        _conv3_body,
        grid=(batch, nc),
        in_specs=[x_spec(0), x_spec(1), x_spec(2), w_spec(0), w_spec(1), w_spec(2),
                  b_spec(0), b_spec(1), b_spec(2)],
        out_specs=[out, out, out],
        out_shape=[_sds((batch, seq, HY_WIDTH))] * 3,
        compiler_params=_cp("parallel", "parallel"),
        name="hyena_conv3",
    )(z, z, z, conv_w, conv_w, conv_w, conv_b, conv_b, conv_b)


def _filt_body(feat_ref, dist_ref, w1, b1, f1, w2, b2, f2, w3, dec, h_ref, s_ref):
    dot = functools.partial(jnp.dot, precision=HI, preferred_element_type=F32)
    hid = jnp.sin(f1[...] * (dot(feat_ref[...], w1[...]) + b1[...]))
    hid = jnp.sin(f2[...] * (dot(hid, w2[...]) + b2[...]))
    h = dot(hid, w3[...]) * jnp.exp(-dist_ref[...] * jnp.abs(dec[...]))
    h_ref[...] = h

    @pl.when(pl.program_id(0) == 0)
    def _():
        s_ref[...] = jnp.zeros_like(s_ref)

    s_ref[...] += jnp.sum(jnp.abs(h), axis=0, keepdims=True)


def hyena_filter(seq, w1, b1, f1, w2, b2, f2, w3, decay):
    j = jnp.arange(seq, dtype=F32)
    tt = j / seq
    ang = 2.0 * jnp.pi * tt[:, None] * jnp.arange(1, HY_BANDS + 1, dtype=F32)
    feat = jnp.concatenate([tt[:, None], jnp.cos(ang), jnp.sin(ang)], -1)
    feat = jnp.pad(feat, ((0, 0), (0, LANES - HY_EMB)))
    half = seq // 2
    dist = (jnp.abs(j - half) / half)[:, None]
    w1p = jnp.pad(w1, ((0, LANES - HY_EMB), (0, 0)))
    tm = min(1024, seq)
    hw = HY_ORDER * HY_WIDTH
    const = lambda i: (0, 0)
    return pl.pallas_call(
        _filt_body,
        grid=(seq // tm,),
        in_specs=[pl.BlockSpec((tm, LANES), lambda i: (i, 0)), pl.BlockSpec((tm, 1), lambda i: (i, 0)),
                  pl.BlockSpec((LANES, HY_FFN), const), pl.BlockSpec((1, HY_FFN), const),
                  pl.BlockSpec((1, HY_FFN), const), pl.BlockSpec((HY_FFN, HY_FFN), const),
                  pl.BlockSpec((1, HY_FFN), const), pl.BlockSpec((1, HY_FFN), const),
                  pl.BlockSpec((HY_FFN, hw), const), pl.BlockSpec((1, hw), const)],
        out_specs=[pl.BlockSpec((tm, hw), lambda i: (i, 0)), pl.BlockSpec((1, hw), const)],
        out_shape=[_sds((seq, hw)), _sds((1, hw))],
        compiler_params=_cp("arbitrary"),
        name="hyena_filter",
    )(feat, dist, w1p, b1[None], f1[None], w2, b2[None], f2[None], w3, decay.reshape(1, hw))


def dft_tables(seq):
    n = 2 * seq
    n2 = FFT_N2
    n1 = n // n2
    n1h = n1 // 2
    nk = -(-(n1h + 1) // SUBLANES) * SUBLANES
    k1 = jnp.arange(nk, dtype=I32)
    live = (k1 <= n1h).astype(F32)
    ang = (2.0 * jnp.pi / n1) * ((k1[:, None] * jnp.arange(n1h, dtype=I32)[None, :]) % n1).astype(F32)
    outer_fwd = jnp.concatenate([jnp.cos(ang), -jnp.sin(ang)], axis=0) * jnp.tile(live, 2)[:, None]
    t1 = jnp.arange(n1h, dtype=I32) + n1 // 4
    ang = (2.0 * jnp.pi / n1) * ((t1[:, None] * k1[None, :]) % n1).astype(F32)
    weight = live * jnp.where((k1 == 0) | (k1 == n1h), 1.0, 2.0) / n
    outer_inv = jnp.concatenate([jnp.cos(ang), -jnp.sin(ang)], axis=1) * jnp.tile(weight, 2)[None, :]
    k2 = jnp.arange(n2, dtype=I32)
    t2 = jnp.arange(n2, dtype=I32)
    phase = (t2[None, None, :] * (k2[None, :, None] * n1 + k1[:, None, None])) % n
    ang = (2.0 * jnp.pi / n) * phase.astype(F32)
    fr, fi = jnp.cos(ang), -jnp.sin(ang)
    inner = jnp.concatenate([jnp.concatenate([fr, -fi], axis=2),
                             jnp.concatenate([fi, fr], axis=2)], axis=1)
    return _split_bf16(outer_fwd), _split_bf16(outer_inv), _split_bf16(inner), _split_bf16(jnp.swapaxes(inner, 1, 2))


def _split_bf16(x):
    hi = x.astype(BF16)
    return hi, (x - hi.astype(F32)).astype(BF16)


def _dot_split(w, a):
    w_hi, w_lo = w
    a_hi, a_lo = _split_bf16(a)
    dot = functools.partial(jnp.dot, preferred_element_type=F32)
    return dot(w_hi, a_hi) + (dot(w_hi, a_lo) + dot(w_lo, a_hi))


def _dot_split_rhs(a, w):
    w_hi, w_lo = w
    a_hi, a_lo = _split_bf16(a)
    dot = functools.partial(jnp.dot, preferred_element_type=F32)
    return dot(a_hi, w_hi) + (dot(a_lo, w_hi) + dot(a_hi, w_lo))


def _every(ref, s, n):
    return ref[pl.ds(s, n, stride=FFT_ROWS), :]


def _fft_outer_body(f_ref, fl_ref, x_ref, o_ref):
    n1h = x_ref.shape[0]
    rows = f_ref.shape[0]
    x2 = x_ref.reshape(n1h * FFT_ROWS, LANES)
    o2 = o_ref.reshape(rows * FFT_ROWS, LANES)
    x = jnp.concatenate([_every(x2, s, n1h) for s in range(FFT_ROWS)], axis=1)
    r = _dot_split((f_ref[...], fl_ref[...]), x)
    for s in range(FFT_ROWS):
        o2[pl.ds(s, rows, stride=FFT_ROWS), :] = r[:, s * LANES:(s + 1) * LANES]


def fft_outer(x4, table):
    batch, n1h, n2, ch = x4.shape
    nk = table[0].shape[0] // 2
    tab = pl.BlockSpec((2 * nk, n1h), lambda b, c, l: (0, 0))
    return pl.pallas_call(
        _fft_outer_body,
        grid=(batch, n2 // FFT_ROWS, ch // LANES),
        in_specs=[tab, tab, pl.BlockSpec((None, n1h, FFT_ROWS, LANES), lambda b, c, l: (b, 0, c, l))],
        out_specs=pl.BlockSpec((None, 2, nk, FFT_ROWS, LANES), lambda b, c, l: (b, 0, 0, c, l)),
        out_shape=_sds((batch, 2, nk, n2, ch)),
        compiler_params=_cp("parallel", "parallel", "parallel"),
        name="fft_outer",
    )(*table, x4)


def _fft_filter_body(mh_ref, ml_ref, s_ref, a_ref, o_ref):
    n2 = FFT_N2
    a = a_ref[0, :, 0].reshape(2 * n2, HY_WIDTH)
    x = _dot_split((mh_ref[0], ml_ref[0]), a) * (1.0 / s_ref[...])
    o_ref[:, 0] = x.reshape(2, n2, HY_WIDTH)


def fft_filter(a5, inner, s):
    _, _, n1, n2, ch = a5.shape
    return pl.pallas_call(
        _fft_filter_body,
        grid=(n1, ch // HY_WIDTH),
        in_specs=[pl.BlockSpec((1, 2 * n2, 2 * n2), lambda k, c: (k, 0, 0)),
                  pl.BlockSpec((1, 2 * n2, 2 * n2), lambda k, c: (k, 0, 0)),
                  pl.BlockSpec((1, HY_WIDTH), lambda k, c: (0, c)),
                  pl.BlockSpec((1, 2, 1, n2, HY_WIDTH), lambda k, c: (0, 0, k, 0, c))],
        out_specs=pl.BlockSpec((2, 1, n2, HY_WIDTH), lambda k, c: (0, k, 0, c)),
        out_shape=_sds((2, n1, n2, ch)),
        compiler_params=_cp("parallel", "parallel"),
        name="fft_filter",
    )(*inner, s, a5)


def _fft_mid_body(mh_ref, ml_ref, th_ref, tl_ref, h_ref, a_ref, o_ref, *, nb):
    n2 = FFT_N2
    m = (mh_ref[0], ml_ref[0])
    mt = (th_ref[0], tl_ref[0])
    hr = h_ref[0, 0]
    hi = h_ref[1, 0]
    for b in range(nb):
        x = _dot_split(m, a_ref[b, :, 0].reshape(2 * n2, HY_WIDTH))
        xr, xi = x[:n2], x[n2:]
        y = jnp.concatenate([xr * hr - xi * hi, xr * hi + xi * hr], axis=0)
        o_ref[b, :, 0] = _dot_split(mt, y).reshape(2, n2, HY_WIDTH)


def fft_mid(a5, inner, inner_t, hf, order):
    nb, _, n1, n2, ch = a5.shape
    mat = pl.BlockSpec((1, 2 * n2, 2 * n2), lambda k: (k, 0, 0))
    blk = pl.BlockSpec((nb, 2, 1, n2, ch), lambda k: (0, 0, k, 0, 0))
    return pl.pallas_call(
        functools.partial(_fft_mid_body, nb=nb),
        grid=(n1,),
        in_specs=[mat, mat, mat, mat, pl.BlockSpec((2, 1, n2, ch), lambda k: (0, k, 0, order)), blk],
        out_specs=blk,
        out_shape=_sds(a5.shape),
        compiler_params=_cp("parallel"),
        name="fft_mid",
    )(*inner, *inner_t, hf, a5)


def _fft_inv_body(g_ref, gl_ref, b_ref, gate_ref, y_ref, fb_ref, o_ref):
    n1h, rows = g_ref.shape
    b2 = b_ref.reshape(rows * FFT_ROWS, LANES)
    gate2 = gate_ref.reshape(n1h * FFT_ROWS, LANES)
    y2 = y_ref.reshape(n1h * FFT_ROWS, LANES)
    o2 = o_ref.reshape(n1h * FFT_ROWS, LANES)
    bm = jnp.concatenate([_every(b2, s, rows) for s in range(FFT_ROWS)], axis=1)
    c = _dot_split((g_ref[...], gl_ref[...]), bm)
    fb = fb_ref[...]
    for s in range(FFT_ROWS):
        conv = c[:, s * LANES:(s + 1) * LANES]
        o2[pl.ds(s, n1h, stride=FFT_ROWS), :] = _every(gate2, s, n1h) * (conv + fb * _every(y2, s, n1h))


def fft_inv_gate(b5, table, gate4, y4, fbias):
    batch, _, nk, n2, ch = b5.shape
    n1h = table[0].shape[0]
    blk = pl.BlockSpec((None, n1h, FFT_ROWS, LANES), lambda b, c, l: (b, 0, c, l))
    tab = pl.BlockSpec((n1h, 2 * nk), lambda b, c, l: (0, 0))
    return pl.pallas_call(
        _fft_inv_body,
        grid=(batch, n2 // FFT_ROWS, ch // LANES),
        in_specs=[tab, tab,
                  pl.BlockSpec((None, 2, nk, FFT_ROWS, LANES), lambda b, c, l: (b, 0, 0, c, l)), blk, blk,
                  pl.BlockSpec((1, LANES), lambda b, c, l: (0, l))],
        out_specs=blk,
        out_shape=_sds((batch, n1h, n2, ch)),
        compiler_params=_cp("parallel", "parallel", "parallel"),
        name="fft_inv_gate",
    )(*table, b5, gate4, y4, fbias.reshape(1, ch))


def hyena_spectrum(seq, tabs, w1, b1, f1, w2, b2, f2, w3, decay):
    outer_fwd, _, inner, _ = tabs
    n1h = outer_fwd[0].shape[1]
    hw = HY_ORDER * HY_WIDTH
    h, s = hyena_filter(seq, w1, b1, f1, w2, b2, f2, w3, decay)
    return fft_filter(fft_outer(h.reshape(1, n1h, FFT_N2, hw), outer_fwd), inner, s)


def hyena_mixer(z_hy, conv_w, conv_b, hf, f_bias, tabs, batch, seq):
    outer_fwd, outer_inv, inner, inner_t = tabs
    n1h = outer_fwd[0].shape[1]
    shape4 = (batch, n1h, FFT_N2, HY_WIDTH)
    v, g0, g1 = hyena_conv3(z_hy, conv_w, conv_b, batch, seq)
    y = v.reshape(shape4)
    for order, gate in enumerate((g0, g1)):
        bm = fft_mid(fft_outer(y, outer_fwd), inner, inner_t, hf, order)
        y = fft_inv_gate(bm, outer_inv, gate.reshape(shape4), y, f_bias[order])
    return y.reshape(batch * seq, HY_WIDTH)


def _mix_body(lam_ref, h_ref, s5_ref, gqa_ref, diff_ref, hy_ref, gs5, ggqa, gsub, ghy, w_ref, o_ref, *, keep):
    lam = lam_ref[0]
    a = _rms(s5_ref[...], gs5[...])
    b = _rms(jnp.concatenate([gqa_ref[0, h] for h in range(GQA_HEADS)], axis=1), ggqa[...])
    c = jnp.concatenate(
        [_rms(diff_ref[0, 2 * h] - lam * diff_ref[0, 2 * h + 1], gsub[...]) * keep for h in range(DIFF_HEADS)],
        axis=1)
    d = _rms(hy_ref[...], ghy[...])
    mixed = jnp.concatenate([a, b, c, d], axis=1).astype(BF16)
    o_ref[...] = h_ref[...] + jnp.dot(mixed, w_ref[...], preferred_element_type=F32)


def mix_out(lam, h, y_s5, o_gqa, o_diff, y_hy, gs5, ggqa, gsub, ghy, w_out, batch, seq, lambda_init):
    tm = min(ROW_TILE, seq)
    nb = seq // tm
    row = lambda w: pl.BlockSpec((tm, w), lambda b, i: (b * nb + i, 0))
    vec = lambda w: pl.BlockSpec((1, w), lambda b, i: (0, 0))
    heads = lambda n: pl.BlockSpec((1, n, tm, HEAD_DIM), lambda b, i: (b, 0, i, 0))
    time_major = pl.BlockSpec((tm, GROUP_WIDTH), lambda b, i: (i, b))
    return pl.pallas_call(
        functools.partial(_mix_body, keep=1.0 - lambda_init),
        grid=(batch, nb),
        in_specs=[pl.BlockSpec(memory_space=pltpu.SMEM), row(D_MODEL), time_major, heads(GQA_HEADS),
                  heads(2 * DIFF_HEADS), row(GROUP_WIDTH), vec(GROUP_WIDTH), vec(GROUP_WIDTH), vec(HEAD_DIM),
                  vec(GROUP_WIDTH), pl.BlockSpec((D_MODEL, D_MODEL), lambda b, i: (0, 0))],
        out_specs=row(D_MODEL),
        out_shape=_sds((batch * seq, D_MODEL)),
        compiler_params=_cp("parallel", "parallel"),
        name="mix_out",
    )(lam, h, y_s5, o_gqa, o_diff, y_hy, gs5, ggqa, gsub, ghy, w_out)


def _router_body(h_ref, g_ref, w_ref, wl_ref, b_ref, xn_ref, ti_ref, tw_ref, cnt_ref):
    xn = _rms(h_ref[...], g_ref[...])
    xn_ref[...] = xn
    logits = _dot_split_rhs(xn, (w_ref[...], wl_ref[...])) + b_ref[...]
    lane = lax.broadcasted_iota(I32, logits.shape, 1)
    vals = logits
    tops, idxs = [], []
    hot = jnp.zeros(logits.shape, F32)
    for _ in range(TOP_K):
        m = jnp.max(vals, axis=-1, keepdims=True)
        idx = jnp.min(jnp.where(vals == m, lane, N_EXPERTS), axis=-1, keepdims=True)
        sel = lane == idx
        tops.append(m)
        idxs.append(idx)
        hot = hot + sel.astype(F32)
        vals = jnp.where(sel, -jnp.inf, vals)
    es = [jnp.exp(t - tops[0]) for t in tops]
    den = es[0] + es[1] + es[2] + es[3]
    ti_ref[...] = jnp.concatenate(idxs, axis=1)
    tw_ref[...] = jnp.concatenate([e / den for e in es], axis=1)

    @pl.when(pl.program_id(0) == 0)
    def _():
        cnt_ref[...] = jnp.zeros_like(cnt_ref)

    cnt_ref[...] += jnp.sum(hot, axis=0, keepdims=True)


def moe_router(h, g, w_router, b_router):
    t = h.shape[0]
    tm = min(ROW_TILE, t)
    const = lambda i: (0, 0)
    return pl.pallas_call(
        _router_body,
        grid=(t // tm,),
        in_specs=[pl.BlockSpec((tm, D_MODEL), lambda i: (i, 0)), pl.BlockSpec((1, D_MODEL), const),
                  pl.BlockSpec((D_MODEL, N_EXPERTS), const), pl.BlockSpec((D_MODEL, N_EXPERTS), const),
                  pl.BlockSpec((1, N_EXPERTS), const)],
        out_specs=[pl.BlockSpec((tm, D_MODEL), lambda i: (i, 0)), pl.BlockSpec((tm, TOP_K), lambda i: (i, 0)),
                   pl.BlockSpec((tm, TOP_K), lambda i: (i, 0)), pl.BlockSpec((1, N_EXPERTS), const)],
        out_shape=[_sds((t, D_MODEL)), _sds((t, TOP_K), I32), _sds((t, TOP_K)), _sds((1, N_EXPERTS))],
        compiler_params=_cp("arbitrary"),
        name="moe_router",
    )(h, g, *w_router, b_router)


def _rank_body(ti_ref, off_ref, pos_ref, stage_ref, cnt_ref, carry_ref):
    @pl.when(pl.program_id(0) == 0)
    def _():
        carry_ref[...] = jnp.zeros_like(carry_ref)

    ti = ti_ref[...]
    tm = ti.shape[0]
    lane = lax.broadcasted_iota(I32, (tm, N_EXPERTS), 1)
    hots = [lane == ti[:, k:k + 1] for k in range(TOP_K)]
    hot = sum(h.astype(F32) for h in hots)
    r = lax.broadcasted_iota(I32, (tm, tm), 0)
    c = lax.broadcasted_iota(I32, (tm, tm), 1)
    below = jnp.where(r > c, 1.0, 0.0).astype(BF16)
    local = jnp.dot(below, hot.astype(BF16), preferred_element_type=F32)
    before = local + carry_ref[...] + off_ref[...]
    pick = lambda table: jnp.concatenate(
        [jnp.sum(jnp.where(h, table, 0.0), axis=-1, keepdims=True) for h in hots], axis=1).astype(I32)
    pos_ref[...] = pick(before)
    count = jnp.sum(hot, axis=0, keepdims=True)
    er = lax.broadcasted_iota(I32, (N_EXPERTS, N_EXPERTS), 0)
    ec = lax.broadcasted_iota(I32, (N_EXPERTS, N_EXPERTS), 1)
    earlier = jnp.where(er < ec, 1.0, 0.0)
    chunks = jnp.ceil(count * (1.0 / GATHER_CHUNK))
    start = jnp.dot(chunks, earlier, precision=HI, preferred_element_type=F32) * GATHER_CHUNK
    stage_ref[...] = pick(local + start)
    cnt_ref[...] = jnp.broadcast_to(count, cnt_ref.shape)
    carry_ref[...] += count


def moe_rank(top_i, offsets):
    t = top_i.shape[0]
    tm = min(MOE_TOK_TILE, t)
    pairs = pl.BlockSpec((tm, TOP_K), lambda i: (i, 0))
    pos, stage, cnt = pl.pallas_call(
        _rank_body,
        grid=(t // tm,),
        in_specs=[pairs, pl.BlockSpec((1, N_EXPERTS), lambda i: (0, 0))],
        out_specs=[pairs, pairs, pl.BlockSpec((SUBLANES, N_EXPERTS), lambda i: (i, 0))],
        out_shape=[_sds((t, TOP_K), I32), _sds((t, TOP_K), I32), _sds((t // tm * SUBLANES, N_EXPERTS))],
        scratch_shapes=[pltpu.VMEM((1, N_EXPERTS), F32)],
        compiler_params=_cp("arbitrary"),
        name="moe_rank",
    )(top_i, offsets)
    return pos, stage, cnt[::SUBLANES]


def _row_copy(src, dst, sem):
    return pltpu.make_async_copy(src, dst, sem)


def _pad_zero_body(lt_ref, o_ref):
    del lt_ref
    o_ref[...] = jnp.zeros_like(o_ref)


def moe_pad_tiles(last_tile, n_slots):
    rows = MOE_TILE * ROW_TILES
    return pl.pallas_call(
        _pad_zero_body,
        grid_spec=pltpu.PrefetchScalarGridSpec(
            num_scalar_prefetch=1, grid=(N_EXPERTS,), in_specs=[],
            out_specs=pl.BlockSpec((rows, LANES), lambda e, lt: (lt[e], 0))),
        out_shape=_sds((n_slots * ROW_TILES, LANES)),
        compiler_params=_cp("arbitrary"),
        name="moe_pad_tiles",
    )(last_tile)


def _slot_rows(ref, slot):
    return ref.at[pl.ds(pl.multiple_of(slot * ROW_TILES, ROW_TILES), ROW_TILES)]


def _scatter_body(pos_ref, x_ref, padded_hbm, xs_hbm, buf, sem, *, tm):
    del padded_hbm
    x = x_ref[...]
    for s in range(ROW_TILES):
        buf[pl.ds(s, tm, stride=ROW_TILES), :] = x[:, s * LANES:(s + 1) * LANES]

    def start(r, c):
        for k in range(TOP_K):
            _row_copy(_slot_rows(buf, r), _slot_rows(xs_hbm, pos_ref[r * TOP_K + k]), sem).start()
        return c

    lax.fori_loop(0, tm, start, 0)
    for _ in range(TOP_K):
        _row_copy(buf, xs_hbm.at[pl.ds(0, tm * ROW_TILES)], sem).wait()


def moe_scatter(pos_flat, x, padded):
    t = x.shape[0]
    tm = min(MOE_TOK_TILE, t)
    return pl.pallas_call(
        functools.partial(_scatter_body, tm=tm),
        grid=(t // tm,),
        in_specs=[pl.BlockSpec((tm * TOP_K,), lambda i: (i,), memory_space=pltpu.SMEM),
                  pl.BlockSpec((tm, D_MODEL), lambda i: (i, 0)), pl.BlockSpec(memory_space=pl.ANY)],
        out_specs=pl.BlockSpec(memory_space=pl.ANY),
        out_shape=_sds(padded.shape),
        scratch_shapes=[pltpu.VMEM((tm * ROW_TILES, LANES), F32), pltpu.SemaphoreType.DMA(())],
        input_output_aliases={2: 0},
        compiler_params=_cp("arbitrary"),
        name="moe_scatter",
    )(pos_flat, x, padded)


def _experts_body(te_ref, nv_ref, x_ref, wgu_ref, bgu_ref, wd_ref, bd_ref, o_ref):
    del te_ref
    j = pl.program_id(0)
    dff = wd_ref.shape[1]

    @pl.when(j < nv_ref[0])
    def _():
        tm = x_ref.shape[0] // ROW_TILES
        x = jnp.concatenate([x_ref[pl.ds(s, tm, stride=ROW_TILES), :] for s in range(ROW_TILES)],
                            axis=1).astype(BF16)
        hgu = jnp.dot(x, wgu_ref[0], preferred_element_type=F32) + bgu_ref[0]
        glu = jnp.minimum(hgu[:, :dff], SWIGLU_LIMIT)
        lin = jnp.clip(hgu[:, dff:], -SWIGLU_LIMIT, SWIGLU_LIMIT)
        act = glu * jax.nn.sigmoid(SWIGLU_ALPHA * glu) * (lin + 1.0)
        y = jnp.dot(act.astype(BF16), wd_ref[0], preferred_element_type=F32) + bd_ref[0]
        for s in range(ROW_TILES):
            o_ref[pl.ds(s, tm, stride=ROW_TILES), :] = y[:, s * LANES:(s + 1) * LANES]

    @pl.when(j >= nv_ref[0])
    def _():
        o_ref[...] = jnp.zeros_like(o_ref)


def moe_experts(tile_expert, n_valid, xs, w_gu, b_gu, w_down, b_down):
    n_slots = xs.shape[0] // ROW_TILES
    tm = MOE_TILE
    dff = w_down.shape[1]
    blk = pl.BlockSpec((tm * ROW_TILES, LANES), lambda j, te, nv: (j, 0))
    return pl.pallas_call(
        _experts_body,
        grid_spec=pltpu.PrefetchScalarGridSpec(
            num_scalar_prefetch=2,
            grid=(n_slots // tm,),
            in_specs=[pl.BlockSpec((tm * ROW_TILES, LANES), lambda j, te, nv: (jnp.minimum(j, nv[0] - 1), 0)),
                      pl.BlockSpec((1, D_MODEL, 2 * dff), lambda j, te, nv: (te[j], 0, 0)),
                      pl.BlockSpec((1, 1, 2 * dff), lambda j, te, nv: (te[j], 0, 0)),
                      pl.BlockSpec((1, dff, D_MODEL), lambda j, te, nv: (te[j], 0, 0)),
                      pl.BlockSpec((1, 1, D_MODEL), lambda j, te, nv: (te[j], 0, 0))],
            out_specs=blk),
        out_shape=_sds((n_slots * ROW_TILES, LANES)),
        compiler_params=_cp("arbitrary"),
        name="moe_experts",
    )(tile_expert, n_valid, xs, w_gu, b_gu, w_down, b_down)


def _combine_body(slot_ref, w_ref, tab_ref, ys_hbm, o_ref, stage, mixed, sem, *, tm):
    chunk_rows = GATHER_CHUNK * ROW_TILES

    def chunk_copy(src_slot, dst_slot):
        return _row_copy(ys_hbm.at[pl.ds(pl.multiple_of(src_slot * ROW_TILES, ROW_TILES), chunk_rows)],
                         stage.at[pl.ds(pl.multiple_of(dst_slot * ROW_TILES, ROW_TILES), chunk_rows)], sem)

    def per_expert(e, c):
        base, chunks, start = tab_ref[e], tab_ref[N_EXPERTS + e], tab_ref[2 * N_EXPERTS + e]

        def issue(j, c2):
            chunk_copy(base + j * GATHER_CHUNK, start + j * GATHER_CHUNK).start()
            return c2

        return lax.fori_loop(0, chunks, issue, c)

    lax.fori_loop(0, N_EXPERTS, per_expert, 0)

    def wait(j, c):
        chunk_copy(0, 0).wait()
        return c

    lax.fori_loop(0, tab_ref[3 * N_EXPERTS], wait, 0)

    def mix(q, c):
        for u in range(MIX_UNROLL):
            r = q * MIX_UNROLL + u
            acc = None
            for k in range(TOP_K):
                p = r * TOP_K + k
                term = w_ref[p] * stage[pl.ds(pl.multiple_of(slot_ref[p] * ROW_TILES, ROW_TILES), ROW_TILES), :]
                acc = term if acc is None else acc + term
            mixed[pl.ds(pl.multiple_of(r * ROW_TILES, ROW_TILES), ROW_TILES), :] = acc
        return c

    lax.fori_loop(0, tm // MIX_UNROLL, mix, 0)
    o_ref[...] = jnp.concatenate([mixed[pl.ds(s, tm, stride=ROW_TILES), :] for s in range(ROW_TILES)], axis=1)


def moe_combine(stage_flat, w_flat, table, ys, t):
    tm = min(MOE_TOK_TILE, t)
    smem = pl.BlockSpec((tm * TOP_K,), lambda i: (i,), memory_space=pltpu.SMEM)
    stage_slots = tm * TOP_K + N_EXPERTS * GATHER_CHUNK
    return pl.pallas_call(
        functools.partial(_combine_body, tm=tm),
        grid=(t // tm,),
        in_specs=[smem, smem, pl.BlockSpec((LANES,), lambda i: (i,), memory_space=pltpu.SMEM),
                  pl.BlockSpec(memory_space=pl.ANY)],
        out_specs=pl.BlockSpec((tm, D_MODEL), lambda i: (i, 0)),
        out_shape=_sds((t, D_MODEL)),
        scratch_shapes=[pltpu.VMEM((stage_slots * ROW_TILES, LANES), F32),
                        pltpu.VMEM((tm * ROW_TILES, LANES), F32), pltpu.SemaphoreType.DMA(())],
        compiler_params=_cp("arbitrary"),
        name="moe_combine",
    )(stage_flat, w_flat, table, ys)


def gather_table(cnt_tile, offsets):
    base = offsets + jnp.cumsum(cnt_tile, axis=0) - cnt_tile
    chunks = jnp.ceil(cnt_tile / GATHER_CHUNK)
    start = (jnp.cumsum(chunks, axis=1) - chunks) * GATHER_CHUNK
    total = jnp.sum(chunks, axis=1, keepdims=True)
    pad = jnp.zeros((cnt_tile.shape[0], LANES - 3 * N_EXPERTS - 1), F32)
    return jnp.concatenate([base, chunks, start, total, pad], axis=1).astype(I32).reshape(-1)


def moe(h, g, w_router, b_router, w_gu, b_gu, w_down, b_down):
    t = h.shape[0]
    xn, top_i, top_w, counts = moe_router(h, g, w_router, b_router)
    n_tiles = (t * TOP_K) // MOE_TILE + N_EXPERTS + 1
    padded = jnp.ceil(counts[0] / MOE_TILE) * MOE_TILE
    ends = jnp.cumsum(padded)
    offsets = (ends - padded)[None]
    tile_start = jnp.arange(n_tiles, dtype=F32) * MOE_TILE
    tile_expert = jnp.minimum(jnp.sum(ends[None, :] <= tile_start[:, None], axis=1), N_EXPERTS - 1).astype(I32)
    n_valid = (ends[-1:] / MOE_TILE).astype(I32)
    own_last = jnp.where(padded > 0, ends / MOE_TILE - 1, -1.0)
    filled = lax.cummax(own_last, axis=0)
    first = jnp.min(jnp.where(padded > 0, own_last, jnp.inf))
    last_tile = jnp.where(filled < 0, first, filled).astype(I32)
    pos, stage, cnt_tile = moe_rank(top_i, offsets)
    xs = moe_scatter(pos.reshape(t * TOP_K), xn, moe_pad_tiles(last_tile, n_tiles * MOE_TILE))
    ys = moe_experts(tile_expert, n_valid, xs, w_gu, b_gu, w_down, b_down)
    flat = lambda a: a.reshape(t * TOP_K)
    return moe_combine(flat(stage), flat(top_w), gather_table(cnt_tile, offsets), ys, t)


def _ple_body(h_ref, m_ref, p_ref, wp_ref, wg_ref, o_ref):
    h = h_ref[...] + m_ref[...]
    e = jnp.dot(p_ref[...].astype(BF16), wp_ref[...], preferred_element_type=F32)
    gate = jnp.dot(_rms(h).astype(BF16), wg_ref[...], preferred_element_type=F32)
    o_ref[...] = h + e * jax.nn.sigmoid(gate)


def ple(h, moe_out, p, w_ple, w_gate):
    t = h.shape[0]
    tm = min(ROW_TILE, t)
    row = pl.BlockSpec((tm, D_MODEL), lambda i: (i, 0))
    return pl.pallas_call(
        _ple_body,
        grid=(t // tm,),
        in_specs=[row, row, pl.BlockSpec((tm, PLE_DIM), lambda i: (i, 0)),
                  pl.BlockSpec((PLE_DIM, D_MODEL), lambda i: (0, 0)),
                  pl.BlockSpec((D_MODEL, D_MODEL), lambda i: (0, 0))],
        out_specs=row,
        out_shape=_sds((t, D_MODEL)),
        compiler_params=_cp("parallel"),
        name="ple",
    )(h, moe_out, p, w_ple, w_gate)


def _deinterleave(n):
    return np.concatenate([np.arange(0, n, 2), np.arange(1, n, 2)])


def _halves(w):
    pairs = w.reshape(*w.shape[:-1], w.shape[-1] // 2, 2)
    return jnp.concatenate([pairs[..., 0], pairs[..., 1]], axis=-1)


def prepare_layer(i, prm):
    pair = _deinterleave(HEAD_DIM)
    cols = np.arange(IN_COLS)
    for h in range(GQA_HEADS + GQA_KV_HEADS):
        lo = OFF_GQA + h * HEAD_DIM
        cols[lo:lo + HEAD_DIM] = lo + pair
    lam = (jnp.exp(jnp.sum(prm["diff_lam_q1"][i] * prm["diff_lam_k1"][i]))
           - jnp.exp(jnp.sum(prm["diff_lam_q2"][i] * prm["diff_lam_k2"][i])) + _lambda_init(i))
    return dict(
        norm1_g=prm["norm1_g"][i][None],
        w_in=prm["w_in"][i][:, cols].astype(BF16),
        s5=s5_tables(prm["s5_lam_re"][i], prm["s5_lam_im"][i], prm["s5_log_dt"][i], prm["s5_b_re"][i],
                     prm["s5_b_im"][i], prm["s5_c_re"][i], prm["s5_c_im"][i]),
        s5_d=prm["s5_d"][i][None],
        s5_w_glu=prm["s5_w_glu"][i].astype(BF16),
        s5_b_glu=prm["s5_b_glu"][i][None],
        gqa_q_g=prm["gqa_q_g"][i][pair][None],
        gqa_k_g=prm["gqa_k_g"][i][pair][None],
        diff_q_g=prm["diff_q_g"][i][None],
        diff_k_g=prm["diff_k_g"][i][None],
        diff_lam=lam.reshape(1).astype(F32),
        diff_subln_g=prm["diff_subln_g"][i][None],
        out_g_s5=prm["out_g_s5"][i][None],
        out_g_gqa=prm["out_g_gqa"][i][None],
        out_g_hy=prm["out_g_hy"][i][None],
        w_out=prm["w_out"][i].astype(BF16),
        norm2_g=prm["norm2_g"][i][None],
        w_router=_split_bf16(prm["w_router"][i]),
        b_router=prm["b_router"][i][None],
        w_gu=_halves(prm["w_gu"][i]).astype(BF16),
        b_gu=_halves(prm["b_gu"][i])[:, None, :],
        w_down=prm["w_down"][i].astype(BF16),
        b_down=prm["b_down"][i][:, None, :],
        w_ple=prm["w_ple"][i].astype(BF16),
        w_ple_gate=prm["w_ple_gate"][i].astype(BF16),
    )


def _lambda_init(i):
    return 0.8 - 0.6 * math.exp(-0.3 * i)


def run_trunk(x, p, prm, layers):
    batch, seq, _ = x.shape
    t = batch * seq
    cos_t, sin_t = rope_tables(seq)
    tabs = dft_tables(seq)
    bias = diff_bias_tiles(prm["rel_bias"], min(ATTN_UNIT, seq))
    h = x.reshape(t, D_MODEL)
    for i, lw in enumerate(layers):
        z_s5, z_gqa, z_diff, z_hy = in_proj(h, lw["norm1_g"], lw["w_in"], batch, seq)
        y_s5 = s5_mixer(z_s5, lw["s5"], lw["s5_d"], lw["s5_w_glu"], lw["s5_b_glu"], batch, seq)
        rep = GQA_HEADS // GQA_KV_HEADS
        q, k, v = attn_prep(z_gqa, lw["gqa_q_g"], lw["gqa_k_g"], cos_t, sin_t, batch, seq,
                            GQA_HEADS, GQA_KV_HEADS, GQA_KV_HEADS, True, lambda hq: hq // rep)
        o_gqa = flash_attention(q, k, v, lambda hq: hq // rep)
        q, k, v = attn_prep(z_diff, lw["diff_q_g"], lw["diff_k_g"], cos_t, sin_t, batch, seq,
                            2 * DIFF_HEADS, 2 * DIFF_HEADS, DIFF_HEADS, False, lambda hq: hq)
        o_diff = flash_attention(q, k, v, lambda hq: hq // 2, bias)
        hf = hyena_spectrum(seq, tabs, prm["hy_w1"][i], prm["hy_b1"][i], prm["hy_freq1"][i], prm["hy_w2"][i],
                            prm["hy_b2"][i], prm["hy_freq2"][i], prm["hy_w3"][i], prm["hy_decay"][i])
        y_hy = hyena_mixer(z_hy, prm["hy_conv_w"][i], prm["hy_conv_b"][i][None], hf, prm["hy_bias"][i],
                           tabs, batch, seq)
        h = mix_out(lw["diff_lam"], h, y_s5, o_gqa, o_diff, y_hy, lw["out_g_s5"], lw["out_g_gqa"],
                    lw["diff_subln_g"], lw["out_g_hy"], lw["w_out"], batch, seq, _lambda_init(i))
        m = moe(h, lw["norm2_g"], lw["w_router"], lw["b_router"], lw["w_gu"], lw["b_gu"], lw["w_down"],
                lw["b_down"])
        h = ple(h, m, p[i].reshape(t, PLE_DIM), lw["w_ple"], lw["w_ple_gate"])
    return h.reshape(batch, seq, D_MODEL)


def kernel(x_prompt, x_sample, p_prompt, p_sample, rel_bias, norm1_g, w_in, s5_lam_re, s5_lam_im, s5_log_dt,
           s5_b_re, s5_b_im, s5_c_re, s5_c_im, s5_d, s5_w_glu, s5_b_glu, gqa_q_g, gqa_k_g, diff_q_g, diff_k_g,
           diff_lam_q1, diff_lam_k1, diff_lam_q2, diff_lam_k2, diff_subln_g, hy_conv_w, hy_conv_b, hy_w1, hy_b1,
           hy_freq1, hy_w2, hy_b2, hy_freq2, hy_w3, hy_decay, hy_bias, out_g_s5, out_g_gqa, out_g_hy, w_out,
           norm2_g, w_router, b_router, w_gu, b_gu, w_down, b_down, w_ple, w_ple_gate):
    prm = dict(rel_bias=rel_bias, norm1_g=norm1_g, w_in=w_in, s5_lam_re=s5_lam_re, s5_lam_im=s5_lam_im,
               s5_log_dt=s5_log_dt, s5_b_re=s5_b_re, s5_b_im=s5_b_im, s5_c_re=s5_c_re, s5_c_im=s5_c_im,
               s5_d=s5_d, s5_w_glu=s5_w_glu, s5_b_glu=s5_b_glu, gqa_q_g=gqa_q_g, gqa_k_g=gqa_k_g,
               diff_q_g=diff_q_g, diff_k_g=diff_k_g, diff_lam_q1=diff_lam_q1, diff_lam_k1=diff_lam_k1,
               diff_lam_q2=diff_lam_q2, diff_lam_k2=diff_lam_k2, diff_subln_g=diff_subln_g,
               hy_conv_w=hy_conv_w, hy_conv_b=hy_conv_b, hy_w1=hy_w1, hy_b1=hy_b1, hy_freq1=hy_freq1,
               hy_w2=hy_w2, hy_b2=hy_b2, hy_freq2=hy_freq2, hy_w3=hy_w3, hy_decay=hy_decay, hy_bias=hy_bias,
               out_g_s5=out_g_s5, out_g_gqa=out_g_gqa, out_g_hy=out_g_hy, w_out=w_out, norm2_g=norm2_g,
               w_router=w_router, b_router=b_router, w_gu=w_gu, b_gu=b_gu, w_down=w_down, b_down=b_down,
               w_ple=w_ple, w_ple_gate=w_ple_gate)
    layers = [prepare_layer(i, prm) for i in range(DEPTH)]
    return (run_trunk(x_prompt, p_prompt, prm, layers), run_trunk(x_sample, p_sample, prm, layers))
```

```python
import functools
import math

import numpy as np
import jax
import jax.numpy as jnp
from jax import lax
from jax.experimental import pallas as pl
from jax.experimental.pallas import tpu as pltpu

F32 = jnp.float32
BF16 = jnp.bfloat16
I32 = jnp.int32
HI = lax.Precision.HIGHEST

D_MODEL = 1024
DEPTH = 2
GROUP_WIDTH = 256
HEAD_DIM = 64
S5_GROUP_DIM = 16
S5_GROUPS = 16
S5_STATE = 64
S5_NSTATE = S5_GROUPS * S5_STATE
GQA_HEADS = 4
GQA_KV_HEADS = 2
DIFF_HEADS = 4
HY_WIDTH = 256
HY_ORDER = 2
HY_BANDS = 16
HY_EMB = 2 * HY_BANDS + 1
HY_FFN = 64
N_EXPERTS = 32
TOP_K = 4
SWIGLU_LIMIT = 7.0
SWIGLU_ALPHA = 1.702
NUM_BUCKETS = 32
MAX_DISTANCE = 128
GRID_W = 64
ROPE_THETA = 10000.0
ROPE_AXIS_DIM = HEAD_DIM // 2
PLE_DIM = 256
NORM_EPS = 1e-6
ATTN_SCALE = HEAD_DIM ** -0.5
LOG2E = math.log2(math.e)

S5_COLS = GROUP_WIDTH
GQA_QW = GQA_HEADS * HEAD_DIM
GQA_KW = GQA_KV_HEADS * HEAD_DIM
GQA_COLS = GQA_QW + 2 * GQA_KW
DIFF_QW = DIFF_HEADS * 2 * HEAD_DIM
DIFF_COLS = 2 * DIFF_QW + DIFF_HEADS * HEAD_DIM
HY_COLS = (HY_ORDER + 1) * HY_WIDTH
OFF_GQA = S5_COLS
OFF_DIFF = OFF_GQA + GQA_COLS
OFF_HY = OFF_DIFF + DIFF_COLS
IN_COLS = OFF_HY + HY_COLS
IN_SPLITS = ((0, OFF_GQA), (OFF_GQA, OFF_DIFF), (OFF_DIFF, OFF_HY), (OFF_HY, IN_COLS))

LANES = 128
SUBLANES = 8
ROW_TILES = D_MODEL // LANES
FFT_N2 = 128
FFT_ROWS = 32
VMEM_LIMIT = 48 * 1024 * 1024

ROW_TILE = 512
ATTN_Q_TILE = 2048
ATTN_K_TILE = 2048
ATTN_UNIT = 512
ATTN_SUB = 512
ATTN_SUB_PLAIN = 1024
S5_TIME_BLOCK = 128
MOE_TILE = 512
MOE_TOK_TILE = 512
MIX_UNROLL = 4
GATHER_CHUNK = 16


def _cp(*sem):
    return pltpu.CompilerParams(dimension_semantics=sem, vmem_limit_bytes=VMEM_LIMIT)


def _sds(shape, dtype=F32):
    return jax.ShapeDtypeStruct(shape, dtype)


def _rms(x, g=None):
    y = x * lax.rsqrt(jnp.mean(x * x, axis=-1, keepdims=True) + NORM_EPS)
    return y if g is None else y * g


def _inproj_body(x_ref, g_ref, w_ref, *o_refs):
    u = _rms(x_ref[...], g_ref[...]).astype(BF16)
    for o_ref, (lo, hi) in zip(o_refs, IN_SPLITS):
        o_ref[...] = jnp.dot(u, w_ref[:, lo:hi], preferred_element_type=F32)


def in_proj(h, g, w, batch, seq):
    t = h.shape[0]
    tm = min(ROW_TILE, seq)
    nb = seq // tm
    row_major = lambda i: (i, 0)
    time_major = lambda i: (i % nb, i // nb)
    widths = [hi - lo for lo, hi in IN_SPLITS]
    return pl.pallas_call(
        _inproj_body,
        grid=(t // tm,),
        in_specs=[pl.BlockSpec((tm, D_MODEL), row_major),
                  pl.BlockSpec((1, D_MODEL), lambda i: (0, 0)),
                  pl.BlockSpec((D_MODEL, IN_COLS), lambda i: (0, 0))],
        out_specs=[pl.BlockSpec((tm, widths[0]), time_major)] + [pl.BlockSpec((tm, w_), row_major) for w_ in widths[1:]],
        out_shape=[_sds((seq, batch * widths[0]))] + [_sds((t, w_)) for w_ in widths[1:]],
        compiler_params=_cp("parallel"),
        name="in_proj",
    )(h, g, w)


def s5_tables(lam_re, lam_im, log_dt, b_re, b_im, c_re, c_im):
    lr = jnp.minimum(lam_re, -1e-4)
    li = lam_im
    dt = jnp.exp(log_dt)[..., None]
    mag = jnp.exp(lr * dt)
    ar = mag * jnp.cos(li * dt)
    ai = mag * jnp.sin(li * dt)
    den = lr * lr + li * li
    cr = ((ar - 1.0) * lr + ai * li) / den
    ci = (ai * lr - (ar - 1.0) * li) / den
    bbr = cr[..., None] * b_re - ci[..., None] * b_im
    bbi = cr[..., None] * b_im + ci[..., None] * b_re
    eye = jnp.eye(S5_GROUPS, dtype=F32)
    bb = jnp.stack([bbr, bbi], axis=1)
    wb = jnp.einsum("dpgnh,gk->dghpkn", bb, eye).reshape(2, S5_COLS, 2 * S5_NSTATE)
    cc = jnp.stack([c_re, -c_im], axis=0)
    wc = jnp.einsum("pdghn,gk->dpgnkh", cc, eye).reshape(2, 2 * S5_NSTATE, S5_COLS)
    a_re = jnp.repeat(ar.reshape(2, S5_NSTATE), 4, axis=0)
    a_im = jnp.repeat(ai.reshape(2, S5_NSTATE), 4, axis=0)
    return wb.astype(BF16), wc.astype(BF16), a_re, a_im


def _s5_body(uf_ref, ub_ref, wb_ref, wc_ref, ar_ref, ai_ref, yf_ref, yb_ref, bf_ref, bb_ref, xr_ref, xi_ref, *, tb):
    @pl.when(pl.program_id(0) == 0)
    def _():
        xr_ref[...] = jnp.zeros_like(xr_ref)
        xi_ref[...] = jnp.zeros_like(xi_ref)

    bf_ref[...] = jnp.dot(uf_ref[...].astype(BF16), wb_ref[0], preferred_element_type=F32)
    bb_ref[...] = jnp.dot(ub_ref[...].astype(BF16), wb_ref[1], preferred_element_type=F32)
    ar = ar_ref[...]
    ai = ai_ref[...]
    low = lax.broadcasted_iota(I32, (SUBLANES, 1), 0) < 4
    half = SUBLANES // 2
    re = slice(0, S5_NSTATE)
    im = slice(S5_NSTATE, 2 * S5_NSTATE)

    def pair(m, carry):
        xr, xi = carry
        rf = pl.ds(pl.multiple_of(m * SUBLANES, SUBLANES), SUBLANES)
        rb = pl.ds(pl.multiple_of((tb // 2 - 1 - m) * SUBLANES, SUBLANES), SUBLANES)
        f_r, f_i, b_r, b_i = bf_ref[rf, re], bf_ref[rf, im], bb_ref[rb, re], bb_ref[rb, im]
        in_r = jnp.where(low, f_r, b_r)
        in_i = jnp.where(low, f_i, b_i)
        x1r = ar * xr - ai * xi + in_r
        x1i = ar * xi + ai * xr + in_i
        in_r = pltpu.roll(jnp.where(low, b_r, f_r), half, 0)
        in_i = pltpu.roll(jnp.where(low, b_i, f_i), half, 0)
        x2r = ar * x1r - ai * x1i + in_r
        x2i = ar * x1i + ai * x1r + in_i
        s2r = pltpu.roll(x2r, half, 0)
        s2i = pltpu.roll(x2i, half, 0)
        bf_ref[rf, re] = jnp.where(low, x1r, s2r)
        bf_ref[rf, im] = jnp.where(low, x1i, s2i)
        bb_ref[rb, re] = jnp.where(low, s2r, x1r)
        bb_ref[rb, im] = jnp.where(low, s2i, x1i)
        return x2r, x2i

    xr, xi = lax.fori_loop(0, tb // 2, pair, (xr_ref[...], xi_ref[...]))
    xr_ref[...] = xr
    xi_ref[...] = xi
    yf_ref[...] = jnp.dot(bf_ref[...].astype(BF16), wc_ref[0], preferred_element_type=F32)
    yb_ref[...] = jnp.dot(bb_ref[...].astype(BF16), wc_ref[1], preferred_element_type=F32)


def s5_scan(u, wb, wc, a_re, a_im, seq):
    tb = min(S5_TIME_BLOCK, seq)
    nb = seq // tb
    rows = tb * 4
    fwd = pl.BlockSpec((rows, S5_COLS), lambda i: (i, 0))
    bwd = pl.BlockSpec((rows, S5_COLS), lambda i: (nb - 1 - i, 0))
    return pl.pallas_call(
        functools.partial(_s5_body, tb=tb),
        grid=(nb,),
        in_specs=[fwd, bwd,
                  pl.BlockSpec((2, S5_COLS, 2 * S5_NSTATE), lambda i: (0, 0, 0)),
                  pl.BlockSpec((2, 2 * S5_NSTATE, S5_COLS), lambda i: (0, 0, 0)),
                  pl.BlockSpec((SUBLANES, S5_NSTATE), lambda i: (0, 0)),
                  pl.BlockSpec((SUBLANES, S5_NSTATE), lambda i: (0, 0))],
        out_specs=[fwd, bwd],
        out_shape=[_sds((seq * 4, S5_COLS))] * 2,
        scratch_shapes=[pltpu.VMEM((rows, 2 * S5_NSTATE), F32), pltpu.VMEM((rows, 2 * S5_NSTATE), F32),
                        pltpu.VMEM((SUBLANES, S5_NSTATE), F32), pltpu.VMEM((SUBLANES, S5_NSTATE), F32)],
        compiler_params=_cp("arbitrary"),
        name="s5_scan",
    )(u, u, wb, wc, a_re, a_im)


def _s5_out_body(yf_ref, yb_ref, u_ref, d_ref, w_ref, b_ref, o_ref):
    y = jax.nn.gelu(yf_ref[...] + yb_ref[...] + u_ref[...] * d_ref[...])
    gate = jnp.dot(y.astype(BF16), w_ref[...], preferred_element_type=F32) + b_ref[...]
    o_ref[...] = y * jax.nn.sigmoid(gate)


def s5_out(yf, yb, u, d, w_glu, b_glu):
    t = u.shape[0]
    tm = min(ROW_TILE, t)
    row = pl.BlockSpec((tm, S5_COLS), lambda i: (i, 0))
    vec = pl.BlockSpec((1, S5_COLS), lambda i: (0, 0))
    return pl.pallas_call(
        _s5_out_body,
        grid=(t // tm,),
        in_specs=[row, row, row, vec, pl.BlockSpec((S5_COLS, S5_COLS), lambda i: (0, 0)), vec],
        out_specs=row,
        out_shape=_sds((t, S5_COLS)),
        compiler_params=_cp("parallel"),
        name="s5_out",
    )(yf, yb, u, d, w_glu, b_glu)


def s5_mixer(z_s5, tabs, d, w_glu, b_glu, batch, seq):
    assert batch == 4 and seq % 2 == 0, "the scan packs 4 sequences x 2 directions onto the 8 sublanes"
    u = z_s5.reshape(seq * batch, S5_COLS)
    yf, yb = s5_scan(u, *tabs, seq)
    return s5_out(yf, yb, u, d, w_glu, b_glu).reshape(seq, batch * S5_COLS)


def _prep_body(z_ref, qg_ref, kg_ref, c_ref, s_ref, q_ref, k_ref, v_ref, *, nq, nk, nv, rope, k_src):
    tm = z_ref.shape[0]
    lane = lax.broadcasted_iota(I32, (tm, LANES), 1)
    low = lane < HEAD_DIM

    def norm_pair(x, g):
        sq = x * x
        ss_lo = jnp.sum(jnp.where(low, sq, 0.0), axis=-1, keepdims=True)
        ss_hi = jnp.sum(jnp.where(low, 0.0, sq), axis=-1, keepdims=True)
        y = x * lax.rsqrt(jnp.where(low, ss_lo, ss_hi) * (1.0 / HEAD_DIM) + NORM_EPS) * g
        if rope:
            half = HEAD_DIM // 2
            partner = jnp.where(lane % HEAD_DIM < half, pltpu.roll(y, LANES - half, 1), pltpu.roll(y, half, 1))
            y = y * c_ref[...] + partner * s_ref[...]
        return y

    qg = qg_ref[...]
    kg = kg_ref[...]
    for p in range(nq // 2):
        q_ref[0, p] = (norm_pair(z_ref[:, p * LANES:(p + 1) * LANES], qg) * (ATTN_SCALE * LOG2E)).astype(BF16)
    k_lo = nq * HEAD_DIM
    pairs = [norm_pair(z_ref[:, k_lo + p * LANES:k_lo + (p + 1) * LANES], kg) for p in range(nk // 2)]
    for h in range(nq):
        src = k_src(h)
        x = pairs[src // 2]
        if src % 2 != h % 2:
            x = pltpu.roll(x, HEAD_DIM, 1)
        k_ref[0, h] = jnp.where(low if h % 2 == 0 else jnp.logical_not(low), x, 0.0).astype(BF16)
    ones_col = jnp.where(lax.broadcasted_iota(I32, (tm, LANES - HEAD_DIM), 1) == 0, 1.0, 0.0)
    for h in range(nv):
        lo = (nq + nk + h) * HEAD_DIM
        v_ref[0, h] = jnp.concatenate([z_ref[:, lo:lo + HEAD_DIM], ones_col], axis=1).astype(BF16)


def attn_prep(z, qg, kg, cos_t, sin_t, batch, seq, nq, nk, nv, rope, k_src):
    width = (nq + nk + nv) * HEAD_DIM
    tm = min(ROW_TILE, seq)
    nb = seq // tm
    vec = pl.BlockSpec((1, LANES), lambda b, i: (0, 0))
    tab = pl.BlockSpec((tm, LANES), lambda b, i: (i, 0))
    heads = lambda n: pl.BlockSpec((1, n, tm, LANES), lambda b, i: (b, 0, i, 0))
    return pl.pallas_call(
        functools.partial(_prep_body, nq=nq, nk=nk, nv=nv, rope=rope, k_src=k_src),
        grid=(batch, nb),
        in_specs=[pl.BlockSpec((tm, width), lambda b, i: (b * nb + i, 0)), vec, vec, tab, tab],
        out_specs=[heads(nq // 2), heads(nq), heads(nv)],
        out_shape=[_sds((batch, nq // 2, seq, LANES), BF16), _sds((batch, nq, seq, LANES), BF16),
                   _sds((batch, nv, seq, LANES), BF16)],
        compiler_params=_cp("parallel", "parallel"),
        name="attn_prep",
    )(z, jnp.tile(qg, (1, 2)), jnp.tile(kg, (1, 2)), cos_t, sin_t)


def _flash_body(q_ref, k_ref, v_ref, *rest, unit, sub, sub_far, has_bias):
    if has_bias:
        b_ref, o_ref, m_ref, acc_ref = rest
    else:
        o_ref, m_ref, acc_ref = rest
    i = pl.program_id(2)
    j = pl.program_id(3)
    tq = q_ref.shape[2]
    tk = k_ref.shape[2]
    q_units = tq // unit
    k_units = tk // unit

    @pl.when(j == 0)
    def _():
        m_ref[...] = jnp.full_like(m_ref, -jnp.inf)
        acc_ref[...] = jnp.zeros_like(acc_ref)

    def attend(rows, offset=None, far_sel=None):
        k = k_ref[0, 0]
        v = v_ref[0, 0]
        for g in range(tq // rows):
            r0 = g * rows
            s = lax.dot_general(q_ref[0, 0, r0:r0 + rows, :], k, (((1,), (1,)), ((), ())),
                                preferred_element_type=F32)
            m_prev = m_ref[r0:r0 + rows, :]
            kinds = [None] * k_units
            if offset is not None:
                tile_of = [offset * k_units + c - r0 // unit for c in range(k_units)]
                kinds = [None if abs(d) <= 1 else (0 if d < 0 else 4) for d in tile_of]
                s = jnp.concatenate(
                    [s[:, c * unit:(c + 1) * unit] + b_ref[0, d + 2, r0 % unit:r0 % unit + rows, :]
                     if abs(d) <= 1 else s[:, c * unit:(c + 1) * unit] for c, d in enumerate(tile_of)], axis=1)
            elif far_sel is not None:
                kinds = [far_sel] * k_units
            consts = {kind: b_ref[0, kind, 0:1, 0:LANES] for kind in set(kinds) if kind is not None}
            per_unit = unit // LANES
            blocks = [s[:, b * LANES:(b + 1) * LANES] for b in range(tk // LANES)]
            m_new = m_prev
            for kind in sorted(set(kinds), key=str):
                members = [blk for b, blk in enumerate(blocks) if kinds[b // per_unit] == kind]
                bmax = members[0]
                for blk in members[1:]:
                    bmax = jnp.maximum(bmax, blk)
                kind_max = jnp.max(bmax, axis=-1, keepdims=True)
                m_new = jnp.maximum(m_new, kind_max if kind is None else kind_max + consts[kind])
            shifts = [m_new if kind is None else m_new - consts[kind] for kind in kinds]
            alpha = jnp.exp2(m_prev - m_new)
            p = jnp.concatenate([jnp.exp2(blk - shifts[b // per_unit]) for b, blk in enumerate(blocks)],
                                axis=1).astype(BF16)
            acc_ref[r0:r0 + rows, :] = alpha * acc_ref[r0:r0 + rows, :] + jnp.dot(p, v, preferred_element_type=F32)
            m_ref[r0:r0 + rows, :] = m_new

    if has_bias:
        assert q_units == k_units and sub == unit, "the near-diagonal variants assume square steps of whole units"
        pl.when(j - i >= 2)(lambda: attend(sub_far, far_sel=4))
        pl.when(j - i <= -2)(lambda: attend(sub_far, far_sel=0))
        for offset in (-1, 0, 1):
            pl.when(j - i == offset)(functools.partial(attend, sub, offset=offset))
    else:
        attend(sub)

    @pl.when(j == pl.num_programs(3) - 1)
    def _():
        acc = acc_ref[...]
        o_ref[0, 0] = acc[:, :HEAD_DIM] / acc[:, HEAD_DIM:HEAD_DIM + 1]


def flash_attention(q, k, v, v_of_q, bias=None):
    batch, nq, seq, _ = k.shape
    tq = min(ATTN_Q_TILE, seq)
    tk = min(ATTN_K_TILE, seq)
    unit = min(ATTN_UNIT, seq)
    sub = min(ATTN_SUB if bias is not None else ATTN_SUB_PLAIN, tq)
    sub_far = min(ATTN_SUB_PLAIN, tq)
    in_specs = [pl.BlockSpec((1, 1, tq, LANES), lambda b, h, i, j: (b, h // 2, i, 0)),
                pl.BlockSpec((1, 1, tk, LANES), lambda b, h, i, j: (b, h, j, 0)),
                pl.BlockSpec((1, 1, tk, LANES), lambda b, h, i, j: (b, v_of_q(h), j, 0))]
    args = [q, k, v]
    if bias is not None:
        assert bias.shape[2] == unit
        in_specs.append(pl.BlockSpec((1, 5, unit, unit), lambda b, h, i, j: (v_of_q(h), 0, 0, 0)))
        args.append(bias)
    return pl.pallas_call(
        functools.partial(_flash_body, unit=unit, sub=sub, sub_far=sub_far, has_bias=bias is not None),
        grid=(batch, nq, seq // tq, seq // tk),
        in_specs=in_specs,
        out_specs=pl.BlockSpec((1, 1, tq, HEAD_DIM), lambda b, h, i, j: (b, h, i, 0)),
        out_shape=_sds((batch, nq, seq, HEAD_DIM)),
        scratch_shapes=[pltpu.VMEM((tq, LANES), F32), pltpu.VMEM((tq, LANES), F32)],
        compiler_params=_cp("parallel", "parallel", "parallel", "arbitrary"),
        name="flash_bias" if bias is not None else "flash",
    )(*args)


def _t5_bucket_table(t):
    rel = np.arange(-(3 * t - 1), 3 * t)
    half = NUM_BUCKETS // 2
    max_exact = half // 2
    ret = np.where(rel > 0, half, 0)
    n = np.abs(rel)
    nf = np.maximum(n, 1).astype(np.float64)
    large = max_exact + (np.log(nf / max_exact) / math.log(MAX_DISTANCE / max_exact) * (half - max_exact)).astype(np.int64)
    large = np.minimum(large, half - 1)
    return ret + np.where(n < max_exact, n, large)


def diff_bias_tiles(rel_bias, t):
    assert t >= MAX_DISTANCE, "offsets of two or more tiles must lie in the saturated buckets"
    heads = rel_bias.shape[1]
    vec = rel_bias[jnp.asarray(_t5_bucket_table(t), dtype=I32)].T * LOG2E
    tiles = []
    for d in range(-2, 3):
        lo = d * t + 2 * t
        w = vec[:, lo:lo + 2 * t - 1]
        u = jnp.concatenate([w[:, t - 1:], jnp.zeros((heads, 1), F32), w[:, :t - 1]], axis=1)
        skew = jnp.tile(u, (1, t))[:, :t * (2 * t - 1)].reshape(heads, t, 2 * t - 1)
        tiles.append(skew[:, :, :t])
    return jnp.stack(tiles, axis=1)


def rope_tables(seq):
    rows = seq // GRID_W
    row = jnp.repeat(jnp.arange(rows, dtype=F32), GRID_W)
    col = jnp.tile(jnp.arange(GRID_W, dtype=F32), rows)
    freq = ROPE_THETA ** (-jnp.arange(0, ROPE_AXIS_DIM, 2, dtype=F32) / ROPE_AXIS_DIM)
    ang = jnp.concatenate([row[:, None] * freq, col[:, None] * freq], -1)
    cos, sin = jnp.cos(ang), jnp.sin(ang)
    return jnp.concatenate([cos, cos] * 2, -1), jnp.concatenate([-sin, sin] * 2, -1)


def _conv3_body(x0, x1, x2, w0, w1, w2, b0, b1, b2, o0, o1, o2):
    for x_ref, w_ref, b_ref, o_ref in ((x0, w0, b0, o0), (x1, w1, b1, o1), (x2, w2, b2, o2)):
        x = x_ref[0]
        n = x.shape[0]
        t = lax.broadcasted_iota(I32, (n, 1), 0)
        prev = jnp.where(t == 0, 0.0, pltpu.roll(x, 1, 0))
        nxt = jnp.where(t == n - 1, 0.0, pltpu.roll(x, n - 1, 0))
        o_ref[0] = w_ref[0:1] * prev + w_ref[1:2] * x + w_ref[2:3] * nxt + b_ref[...]


def hyena_conv3(z_hy, conv_w, conv_b, batch, seq):
    z = z_hy.reshape(batch, seq, HY_COLS)
    nc = HY_WIDTH // LANES
    x_spec = lambda g: pl.BlockSpec((1, seq, LANES), lambda b, c: (b, 0, g * nc + c))
    w_spec = lambda g: pl.BlockSpec((3, LANES), lambda b, c: (0, g * nc + c))
    b_spec = lambda g: pl.BlockSpec((1, LANES), lambda b, c: (0, g * nc + c))
    out = pl.BlockSpec((1, seq, LANES), lambda b, c: (b, 0, c))
    return pl.pallas_call(
        _conv3_body,
        grid=(batch, nc),
        in_specs=[x_spec(0), x_spec(1), x_spec(2), w_spec(0), w_spec(1), w_spec(2),
                  b_spec(0), b_spec(1), b_spec(2)],
        out_specs=[out, out, out],
        out_shape=[_sds((batch, seq, HY_WIDTH))] * 3,
        compiler_params=_cp("parallel", "parallel"),
        name="hyena_conv3",
    )(z, z, z, conv_w, conv_w, conv_w, conv_b, conv_b, conv_b)


def _filt_body(feat_ref, dist_ref, w1, b1, f1, w2, b2, f2, w3, dec, h_ref, s_ref):
    dot = functools.partial(jnp.dot, precision=HI, preferred_element_type=F32)
    hid = jnp.sin(f1[...] * (dot(feat_ref[...], w1[...]) + b1[...]))
    hid = jnp.sin(f2[...] * (dot(hid, w2[...]) + b2[...]))
    h = dot(hid, w3[...]) * jnp.exp(-dist_ref[...] * jnp.abs(dec[...]))
    h_ref[...] = h

    @pl.when(pl.program_id(0) == 0)
    def _():
        s_ref[...] = jnp.zeros_like(s_ref)

    s_ref[...] += jnp.sum(jnp.abs(h), axis=0, keepdims=True)


def hyena_filter(seq, w1, b1, f1, w2, b2, f2, w3, decay):
    j = jnp.arange(seq, dtype=F32)
    tt = j / seq
    ang = 2.0 * jnp.pi * tt[:, None] * jnp.arange(1, HY_BANDS + 1, dtype=F32)
    feat = jnp.concatenate([tt[:, None], jnp.cos(ang), jnp.sin(ang)], -1)
    feat = jnp.pad(feat, ((0, 0), (0, LANES - HY_EMB)))
    half = seq // 2
    dist = (jnp.abs(j - half) / half)[:, None]
    w1p = jnp.pad(w1, ((0, LANES - HY_EMB), (0, 0)))
    tm = min(1024, seq)
    hw = HY_ORDER * HY_WIDTH
    const = lambda i: (0, 0)
    return pl.pallas_call(
        _filt_body,
        grid=(seq // tm,),
        in_specs=[pl.BlockSpec((tm, LANES), lambda i: (i, 0)), pl.BlockSpec((tm, 1), lambda i: (i, 0)),
                  pl.BlockSpec((LANES, HY_FFN), const), pl.BlockSpec((1, HY_FFN), const),
                  pl.BlockSpec((1, HY_FFN), const), pl.BlockSpec((HY_FFN, HY_FFN), const),
                  pl.BlockSpec((1, HY_FFN), const), pl.BlockSpec((1, HY_FFN), const),
                  pl.BlockSpec((HY_FFN, hw), const), pl.BlockSpec((1, hw), const)],
        out_specs=[pl.BlockSpec((tm, hw), lambda i: (i, 0)), pl.BlockSpec((1, hw), const)],
        out_shape=[_sds((seq, hw)), _sds((1, hw))],
        compiler_params=_cp("arbitrary"),
        name="hyena_filter",
    )(feat, dist, w1p, b1[None], f1[None], w2, b2[None], f2[None], w3, decay.reshape(1, hw))


def dft_tables(seq):
    n = 2 * seq
    n2 = FFT_N2
    n1 = n // n2
    n1h = n1 // 2
    nk = -(-(n1h + 1) // SUBLANES) * SUBLANES
    k1 = jnp.arange(nk, dtype=I32)
    live = (k1 <= n1h).astype(F32)
    ang = (2.0 * jnp.pi / n1) * ((k1[:, None] * jnp.arange(n1h, dtype=I32)[None, :]) % n1).astype(F32)
    outer_fwd = jnp.concatenate([jnp.cos(ang), -jnp.sin(ang)], axis=0) * jnp.tile(live, 2)[:, None]
    t1 = jnp.arange(n1h, dtype=I32) + n1 // 4
    ang = (2.0 * jnp.pi / n1) * ((t1[:, None] * k1[None, :]) % n1).astype(F32)
    weight = live * jnp.where((k1 == 0) | (k1 == n1h), 1.0, 2.0) / n
    outer_inv = jnp.concatenate([jnp.cos(ang), -jnp.sin(ang)], axis=1) * jnp.tile(weight, 2)[None, :]
    k2 = jnp.arange(n2, dtype=I32)
    t2 = jnp.arange(n2, dtype=I32)
    phase = (t2[None, None, :] * (k2[None, :, None] * n1 + k1[:, None, None])) % n
    ang = (2.0 * jnp.pi / n) * phase.astype(F32)
    fr, fi = jnp.cos(ang), -jnp.sin(ang)
    inner = jnp.concatenate([jnp.concatenate([fr, -fi], axis=2),
                             jnp.concatenate([fi, fr], axis=2)], axis=1)
    return _split_bf16(outer_fwd), _split_bf16(outer_inv), _split_bf16(inner), _split_bf16(jnp.swapaxes(inner, 1, 2))


def _split_bf16(x):
    hi = x.astype(BF16)
    return hi, (x - hi.astype(F32)).astype(BF16)


def _dot_split(w, a):
    w_hi, w_lo = w
    a_hi, a_lo = _split_bf16(a)
    dot = functools.partial(jnp.dot, preferred_element_type=F32)
    return dot(w_hi, a_hi) + (dot(w_hi, a_lo) + dot(w_lo, a_hi))


def _dot_split_rhs(a, w):
    w_hi, w_lo = w
    a_hi, a_lo = _split_bf16(a)
    dot = functools.partial(jnp.dot, preferred_element_type=F32)
    return dot(a_hi, w_hi) + (dot(a_lo, w_hi) + dot(a_hi, w_lo))


def _every(ref, s, n):
    return ref[pl.ds(s, n, stride=FFT_ROWS), :]


def _fft_outer_body(f_ref, fl_ref, x_ref, o_ref):
    n1h = x_ref.shape[0]
    rows = f_ref.shape[0]
    x2 = x_ref.reshape(n1h * FFT_ROWS, LANES)
    o2 = o_ref.reshape(rows * FFT_ROWS, LANES)
    x = jnp.concatenate([_every(x2, s, n1h) for s in range(FFT_ROWS)], axis=1)
    r = _dot_split((f_ref[...], fl_ref[...]), x)
    for s in range(FFT_ROWS):
        o2[pl.ds(s, rows, stride=FFT_ROWS), :] = r[:, s * LANES:(s + 1) * LANES]


def fft_outer(x4, table):
    batch, n1h, n2, ch = x4.shape
    nk = table[0].shape[0] // 2
    tab = pl.BlockSpec((2 * nk, n1h), lambda b, c, l: (0, 0))
    return pl.pallas_call(
        _fft_outer_body,
        grid=(batch, n2 // FFT_ROWS, ch // LANES),
        in_specs=[tab, tab, pl.BlockSpec((None, n1h, FFT_ROWS, LANES), lambda b, c, l: (b, 0, c, l))],
        out_specs=pl.BlockSpec((None, 2, nk, FFT_ROWS, LANES), lambda b, c, l: (b, 0, 0, c, l)),
        out_shape=_sds((batch, 2, nk, n2, ch)),
        compiler_params=_cp("parallel", "parallel", "parallel"),
        name="fft_outer",
    )(*table, x4)


def _fft_filter_body(mh_ref, ml_ref, s_ref, a_ref, o_ref):
    n2 = FFT_N2
    a = a_ref[0, :, 0].reshape(2 * n2, HY_WIDTH)
    x = _dot_split((mh_ref[0], ml_ref[0]), a) * (1.0 / s_ref[...])
    o_ref[:, 0] = x.reshape(2, n2, HY_WIDTH)


def fft_filter(a5, inner, s):
    _, _, n1, n2, ch = a5.shape
    return pl.pallas_call(
        _fft_filter_body,
        grid=(n1, ch // HY_WIDTH),
        in_specs=[pl.BlockSpec((1, 2 * n2, 2 * n2), lambda k, c: (k, 0, 0)),
                  pl.BlockSpec((1, 2 * n2, 2 * n2), lambda k, c: (k, 0, 0)),
                  pl.BlockSpec((1, HY_WIDTH), lambda k, c: (0, c)),
                  pl.BlockSpec((1, 2, 1, n2, HY_WIDTH), lambda k, c: (0, 0, k, 0, c))],
        out_specs=pl.BlockSpec((2, 1, n2, HY_WIDTH), lambda k, c: (0, k, 0, c)),
        out_shape=_sds((2, n1, n2, ch)),
        compiler_params=_cp("parallel", "parallel"),
        name="fft_filter",
    )(*inner, s, a5)


def _fft_mid_body(mh_ref, ml_ref, th_ref, tl_ref, h_ref, a_ref, o_ref, *, nb):
    n2 = FFT_N2
    m = (mh_ref[0], ml_ref[0])
    mt = (th_ref[0], tl_ref[0])
    hr = h_ref[0, 0]
    hi = h_ref[1, 0]
    for b in range(nb):
        x = _dot_split(m, a_ref[b, :, 0].reshape(2 * n2, HY_WIDTH))
        xr, xi = x[:n2], x[n2:]
        y = jnp.concatenate([xr * hr - xi * hi, xr * hi + xi * hr], axis=0)
        o_ref[b, :, 0] = _dot_split(mt, y).reshape(2, n2, HY_WIDTH)


def fft_mid(a5, inner, inner_t, hf, order):
    nb, _, n1, n2, ch = a5.shape
    mat = pl.BlockSpec((1, 2 * n2, 2 * n2), lambda k: (k, 0, 0))
    blk = pl.BlockSpec((nb, 2, 1, n2, ch), lambda k: (0, 0, k, 0, 0))
    return pl.pallas_call(
        functools.partial(_fft_mid_body, nb=nb),
        grid=(n1,),
        in_specs=[mat, mat, mat, mat, pl.BlockSpec((2, 1, n2, ch), lambda k: (0, k, 0, order)), blk],
        out_specs=blk,
        out_shape=_sds(a5.shape),
        compiler_params=_cp("parallel"),
        name="fft_mid",
    )(*inner, *inner_t, hf, a5)


def _fft_inv_body(g_ref, gl_ref, b_ref, gate_ref, y_ref, fb_ref, o_ref):
    n1h, rows = g_ref.shape
    b2 = b_ref.reshape(rows * FFT_ROWS, LANES)
    gate2 = gate_ref.reshape(n1h * FFT_ROWS, LANES)
    y2 = y_ref.reshape(n1h * FFT_ROWS, LANES)
    o2 = o_ref.reshape(n1h * FFT_ROWS, LANES)
    bm = jnp.concatenate([_every(b2, s, rows) for s in range(FFT_ROWS)], axis=1)
    c = _dot_split((g_ref[...], gl_ref[...]), bm)
    fb = fb_ref[...]
    for s in range(FFT_ROWS):
        conv = c[:, s * LANES:(s + 1) * LANES]
        o2[pl.ds(s, n1h, stride=FFT_ROWS), :] = _every(gate2, s, n1h) * (conv + fb * _every(y2, s, n1h))


def fft_inv_gate(b5, table, gate4, y4, fbias):
    batch, _, nk, n2, ch = b5.shape
    n1h = table[0].shape[0]
    blk = pl.BlockSpec((None, n1h, FFT_ROWS, LANES), lambda b, c, l: (b, 0, c, l))
    tab = pl.BlockSpec((n1h, 2 * nk), lambda b, c, l: (0, 0))
    return pl.pallas_call(
        _fft_inv_body,
        grid=(batch, n2 // FFT_ROWS, ch // LANES),
        in_specs=[tab, tab,
                  pl.BlockSpec((None, 2, nk, FFT_ROWS, LANES), lambda b, c, l: (b, 0, 0, c, l)), blk, blk,
                  pl.BlockSpec((1, LANES), lambda b, c, l: (0, l))],
        out_specs=blk,
        out_shape=_sds((batch, n1h, n2, ch)),
        compiler_params=_cp("parallel", "parallel", "parallel"),
        name="fft_inv_gate",
    )(*table, b5, gate4, y4, fbias.reshape(1, ch))


def hyena_spectrum(seq, tabs, w1, b1, f1, w2, b2, f2, w3, decay):
    outer_fwd, _, inner, _ = tabs
    n1h = outer_fwd[0].shape[1]
    hw = HY_ORDER * HY_WIDTH
    h, s = hyena_filter(seq, w1, b1, f1, w2, b2, f2, w3, decay)
    return fft_filter(fft_outer(h.reshape(1, n1h, FFT_N2, hw), outer_fwd), inner, s)


def hyena_mixer(z_hy, conv_w, conv_b, hf, f_bias, tabs, batch, seq):
    outer_fwd, outer_inv, inner, inner_t = tabs
    n1h = outer_fwd[0].shape[1]
    shape4 = (batch, n1h, FFT_N2, HY_WIDTH)
    v, g0, g1 = hyena_conv3(z_hy, conv_w, conv_b, batch, seq)
    y = v.reshape(shape4)
    for order, gate in enumerate((g0, g1)):
        bm = fft_mid(fft_outer(y, outer_fwd), inner, inner_t, hf, order)
        y = fft_inv_gate(bm, outer_inv, gate.reshape(shape4), y, f_bias[order])
    return y.reshape(batch * seq, HY_WIDTH)


def _mix_body(lam_ref, h_ref, s5_ref, gqa_ref, diff_ref, hy_ref, gs5, ggqa, gsub, ghy, w_ref, o_ref, *, keep):
    lam = lam_ref[0]
    a = _rms(s5_ref[...], gs5[...])
    b = _rms(jnp.concatenate([gqa_ref[0, h] for h in range(GQA_HEADS)], axis=1), ggqa[...])
    c = jnp.concatenate(
        [_rms(diff_ref[0, 2 * h] - lam * diff_ref[0, 2 * h + 1], gsub[...]) * keep for h in range(DIFF_HEADS)],
        axis=1)
    d = _rms(hy_ref[...], ghy[...])
    mixed = jnp.concatenate([a, b, c, d], axis=1).astype(BF16)
    o_ref[...] = h_ref[...] + jnp.dot(mixed, w_ref[...], preferred_element_type=F32)


def mix_out(lam, h, y_s5, o_gqa, o_diff, y_hy, gs5, ggqa, gsub, ghy, w_out, batch, seq, lambda_init):
    tm = min(ROW_TILE, seq)
    nb = seq // tm
    row = lambda w: pl.BlockSpec((tm, w), lambda b, i: (b * nb + i, 0))
    vec = lambda w: pl.BlockSpec((1, w), lambda b, i: (0, 0))
    heads = lambda n: pl.BlockSpec((1, n, tm, HEAD_DIM), lambda b, i: (b, 0, i, 0))
    time_major = pl.BlockSpec((tm, GROUP_WIDTH), lambda b, i: (i, b))
    return pl.pallas_call(
        functools.partial(_mix_body, keep=1.0 - lambda_init),
        grid=(batch, nb),
        in_specs=[pl.BlockSpec(memory_space=pltpu.SMEM), row(D_MODEL), time_major, heads(GQA_HEADS),
                  heads(2 * DIFF_HEADS), row(GROUP_WIDTH), vec(GROUP_WIDTH), vec(GROUP_WIDTH), vec(HEAD_DIM),
                  vec(GROUP_WIDTH), pl.BlockSpec((D_MODEL, D_MODEL), lambda b, i: (0, 0))],
        out_specs=row(D_MODEL),
        out_shape=_sds((batch * seq, D_MODEL)),
        compiler_params=_cp("parallel", "parallel"),
        name="mix_out",
    )(lam, h, y_s5, o_gqa, o_diff, y_hy, gs5, ggqa, gsub, ghy, w_out)


def _router_body(h_ref, g_ref, w_ref, wl_ref, b_ref, xn_ref, ti_ref, tw_ref, cnt_ref):
    xn = _rms(h_ref[...], g_ref[...])
    xn_ref[...] = xn
    logits = _dot_split_rhs(xn, (w_ref[...], wl_ref[...])) + b_ref[...]
    lane = lax.broadcasted_iota(I32, logits.shape, 1)
    vals = logits
    tops, idxs = [], []
    hot = jnp.zeros(logits.shape, F32)
    for _ in range(TOP_K):
        m = jnp.max(vals, axis=-1, keepdims=True)
        idx = jnp.min(jnp.where(vals == m, lane, N_EXPERTS), axis=-1, keepdims=True)
        sel = lane == idx
        tops.append(m)
        idxs.append(idx)
        hot = hot + sel.astype(F32)
        vals = jnp.where(sel, -jnp.inf, vals)
    es = [jnp.exp(t - tops[0]) for t in tops]
    den = es[0] + es[1] + es[2] + es[3]
    ti_ref[...] = jnp.concatenate(idxs, axis=1)
    tw_ref[...] = jnp.concatenate([e / den for e in es], axis=1)

    @pl.when(pl.program_id(0) == 0)
    def _():
        cnt_ref[...] = jnp.zeros_like(cnt_ref)

    cnt_ref[...] += jnp.sum(hot, axis=0, keepdims=True)


def moe_router(h, g, w_router, b_router):
    t = h.shape[0]
    tm = min(ROW_TILE, t)
    const = lambda i: (0, 0)
    return pl.pallas_call(
        _router_body,
        grid=(t // tm,),
        in_specs=[pl.BlockSpec((tm, D_MODEL), lambda i: (i, 0)), pl.BlockSpec((1, D_MODEL), const),
                  pl.BlockSpec((D_MODEL, N_EXPERTS), const), pl.BlockSpec((D_MODEL, N_EXPERTS), const),
                  pl.BlockSpec((1, N_EXPERTS), const)],
        out_specs=[pl.BlockSpec((tm, D_MODEL), lambda i: (i, 0)), pl.BlockSpec((tm, TOP_K), lambda i: (i, 0)),
                   pl.BlockSpec((tm, TOP_K), lambda i: (i, 0)), pl.BlockSpec((1, N_EXPERTS), const)],
        out_shape=[_sds((t, D_MODEL)), _sds((t, TOP_K), I32), _sds((t, TOP_K)), _sds((1, N_EXPERTS))],
        compiler_params=_cp("arbitrary"),
        name="moe_router",
    )(h, g, *w_router, b_router)


def _rank_body(ti_ref, off_ref, pos_ref, stage_ref, cnt_ref, carry_ref):
    @pl.when(pl.program_id(0) == 0)
    def _():
        carry_ref[...] = jnp.zeros_like(carry_ref)

    ti = ti_ref[...]
    tm = ti.shape[0]
    lane = lax.broadcasted_iota(I32, (tm, N_EXPERTS), 1)
    hots = [lane == ti[:, k:k + 1] for k in range(TOP_K)]
    hot = sum(h.astype(F32) for h in hots)
    r = lax.broadcasted_iota(I32, (tm, tm), 0)
    c = lax.broadcasted_iota(I32, (tm, tm), 1)
    below = jnp.where(r > c, 1.0, 0.0).astype(BF16)
    local = jnp.dot(below, hot.astype(BF16), preferred_element_type=F32)
    before = local + carry_ref[...] + off_ref[...]
    pick = lambda table: jnp.concatenate(
        [jnp.sum(jnp.where(h, table, 0.0), axis=-1, keepdims=True) for h in hots], axis=1).astype(I32)
    pos_ref[...] = pick(before)
    count = jnp.sum(hot, axis=0, keepdims=True)
    er = lax.broadcasted_iota(I32, (N_EXPERTS, N_EXPERTS), 0)
    ec = lax.broadcasted_iota(I32, (N_EXPERTS, N_EXPERTS), 1)
    earlier = jnp.where(er < ec, 1.0, 0.0)
    chunks = jnp.ceil(count * (1.0 / GATHER_CHUNK))
    start = jnp.dot(chunks, earlier, precision=HI, preferred_element_type=F32) * GATHER_CHUNK
    stage_ref[...] = pick(local + start)
    cnt_ref[...] = jnp.broadcast_to(count, cnt_ref.shape)
    carry_ref[...] += count


def moe_rank(top_i, offsets):
    t = top_i.shape[0]
    tm = min(MOE_TOK_TILE, t)
    pairs = pl.BlockSpec((tm, TOP_K), lambda i: (i, 0))
    pos, stage, cnt = pl.pallas_call(
        _rank_body,
        grid=(t // tm,),
        in_specs=[pairs, pl.BlockSpec((1, N_EXPERTS), lambda i: (0, 0))],
        out_specs=[pairs, pairs, pl.BlockSpec((SUBLANES, N_EXPERTS), lambda i: (i, 0))],
        out_shape=[_sds((t, TOP_K), I32), _sds((t, TOP_K), I32), _sds((t // tm * SUBLANES, N_EXPERTS))],
        scratch_shapes=[pltpu.VMEM((1, N_EXPERTS), F32)],
        compiler_params=_cp("arbitrary"),
        name="moe_rank",
    )(top_i, offsets)
    return pos, stage, cnt[::SUBLANES]


def _row_copy(src, dst, sem):
    return pltpu.make_async_copy(src, dst, sem)


def _pad_zero_body(lt_ref, o_ref):
    del lt_ref
    o_ref[...] = jnp.zeros_like(o_ref)


def moe_pad_tiles(last_tile, n_slots):
    rows = MOE_TILE * ROW_TILES
    return pl.pallas_call(
        _pad_zero_body,
        grid_spec=pltpu.PrefetchScalarGridSpec(
            num_scalar_prefetch=1, grid=(N_EXPERTS,), in_specs=[],
            out_specs=pl.BlockSpec((rows, LANES), lambda e, lt: (lt[e], 0))),
        out_shape=_sds((n_slots * ROW_TILES, LANES)),
        compiler_params=_cp("arbitrary"),
        name="moe_pad_tiles",
    )(last_tile)


def _slot_rows(ref, slot):
    return ref.at[pl.ds(pl.multiple_of(slot * ROW_TILES, ROW_TILES), ROW_TILES)]


def _scatter_body(pos_ref, x_ref, padded_hbm, xs_hbm, buf, sem, *, tm):
    del padded_hbm
    x = x_ref[...]
    for s in range(ROW_TILES):
        buf[pl.ds(s, tm, stride=ROW_TILES), :] = x[:, s * LANES:(s + 1) * LANES]

    def start(r, c):
        for k in range(TOP_K):
            _row_copy(_slot_rows(buf, r), _slot_rows(xs_hbm, pos_ref[r * TOP_K + k]), sem).start()
        return c

    lax.fori_loop(0, tm, start, 0)
    for _ in range(TOP_K):
        _row_copy(buf, xs_hbm.at[pl.ds(0, tm * ROW_TILES)], sem).wait()


def moe_scatter(pos_flat, x, padded):
    t = x.shape[0]
    tm = min(MOE_TOK_TILE, t)
    return pl.pallas_call(
        functools.partial(_scatter_body, tm=tm),
        grid=(t // tm,),
        in_specs=[pl.BlockSpec((tm * TOP_K,), lambda i: (i,), memory_space=pltpu.SMEM),
                  pl.BlockSpec((tm, D_MODEL), lambda i: (i, 0)), pl.BlockSpec(memory_space=pl.ANY)],
        out_specs=pl.BlockSpec(memory_space=pl.ANY),
        out_shape=_sds(padded.shape),
        scratch_shapes=[pltpu.VMEM((tm * ROW_TILES, LANES), F32), pltpu.SemaphoreType.DMA(())],
        input_output_aliases={2: 0},
        compiler_params=_cp("arbitrary"),
        name="moe_scatter",
    )(pos_flat, x, padded)


def _experts_body(te_ref, nv_ref, x_ref, wgu_ref, bgu_ref, wd_ref, bd_ref, o_ref):
    del te_ref
    j = pl.program_id(0)
    dff = wd_ref.shape[1]

    @pl.when(j < nv_ref[0])
    def _():
        tm = x_ref.shape[0] // ROW_TILES
        x = jnp.concatenate([x_ref[pl.ds(s, tm, stride=ROW_TILES), :] for s in range(ROW_TILES)],
                            axis=1).astype(BF16)
        hgu = jnp.dot(x, wgu_ref[0], preferred_element_type=F32) + bgu_ref[0]
        glu = jnp.minimum(hgu[:, :dff], SWIGLU_LIMIT)
        lin = jnp.clip(hgu[:, dff:], -SWIGLU_LIMIT, SWIGLU_LIMIT)
        act = glu * jax.nn.sigmoid(SWIGLU_ALPHA * glu) * (lin + 1.0)
        y = jnp.dot(act.astype(BF16), wd_ref[0], preferred_element_type=F32) + bd_ref[0]
        for s in range(ROW_TILES):
            o_ref[pl.ds(s, tm, stride=ROW_TILES), :] = y[:, s * LANES:(s + 1) * LANES]

    @pl.when(j >= nv_ref[0])
    def _():
        o_ref[...] = jnp.zeros_like(o_ref)


def moe_experts(tile_expert, n_valid, xs, w_gu, b_gu, w_down, b_down):
    n_slots = xs.shape[0] // ROW_TILES
    tm = MOE_TILE
    dff = w_down.shape[1]
    blk = pl.BlockSpec((tm * ROW_TILES, LANES), lambda j, te, nv: (j, 0))
    return pl.pallas_call(
        _experts_body,
        grid_spec=pltpu.PrefetchScalarGridSpec(
            num_scalar_prefetch=2,
            grid=(n_slots // tm,),
            in_specs=[pl.BlockSpec((tm * ROW_TILES, LANES), lambda j, te, nv: (jnp.minimum(j, nv[0] - 1), 0)),
                      pl.BlockSpec((1, D_MODEL, 2 * dff), lambda j, te, nv: (te[j], 0, 0)),
                      pl.BlockSpec((1, 1, 2 * dff), lambda j, te, nv: (te[j], 0, 0)),
                      pl.BlockSpec((1, dff, D_MODEL), lambda j, te, nv: (te[j], 0, 0)),
                      pl.BlockSpec((1, 1, D_MODEL), lambda j, te, nv: (te[j], 0, 0))],
            out_specs=blk),
        out_shape=_sds((n_slots * ROW_TILES, LANES)),
        compiler_params=_cp("arbitrary"),
        name="moe_experts",
    )(tile_expert, n_valid, xs, w_gu, b_gu, w_down, b_down)


def _combine_body(slot_ref, w_ref, tab_ref, ys_hbm, o_ref, stage, mixed, sem, *, tm):
    chunk_rows = GATHER_CHUNK * ROW_TILES

    def chunk_copy(src_slot, dst_slot):
        return _row_copy(ys_hbm.at[pl.ds(pl.multiple_of(src_slot * ROW_TILES, ROW_TILES), chunk_rows)],
                         stage.at[pl.ds(pl.multiple_of(dst_slot * ROW_TILES, ROW_TILES), chunk_rows)], sem)

    def per_expert(e, c):
        base, chunks, start = tab_ref[e], tab_ref[N_EXPERTS + e], tab_ref[2 * N_EXPERTS + e]

        def issue(j, c2):
            chunk_copy(base + j * GATHER_CHUNK, start + j * GATHER_CHUNK).start()
            return c2

        return lax.fori_loop(0, chunks, issue, c)

    lax.fori_loop(0, N_EXPERTS, per_expert, 0)

    def wait(j, c):
        chunk_copy(0, 0).wait()
        return c

    lax.fori_loop(0, tab_ref[3 * N_EXPERTS], wait, 0)

    def mix(q, c):
        for u in range(MIX_UNROLL):
            r = q * MIX_UNROLL + u
            acc = None
            for k in range(TOP_K):
                p = r * TOP_K + k
                term = w_ref[p] * stage[pl.ds(pl.multiple_of(slot_ref[p] * ROW_TILES, ROW_TILES), ROW_TILES), :]
                acc = term if acc is None else acc + term
            mixed[pl.ds(pl.multiple_of(r * ROW_TILES, ROW_TILES), ROW_TILES), :] = acc
        return c

    lax.fori_loop(0, tm // MIX_UNROLL, mix, 0)
    o_ref[...] = jnp.concatenate([mixed[pl.ds(s, tm, stride=ROW_TILES), :] for s in range(ROW_TILES)], axis=1)


def moe_combine(stage_flat, w_flat, table, ys, t):
    tm = min(MOE_TOK_TILE, t)
    smem = pl.BlockSpec((tm * TOP_K,), lambda i: (i,), memory_space=pltpu.SMEM)
    stage_slots = tm * TOP_K + N_EXPERTS * GATHER_CHUNK
    return pl.pallas_call(
        functools.partial(_combine_body, tm=tm),
        grid=(t // tm,),
        in_specs=[smem, smem, pl.BlockSpec((LANES,), lambda i: (i,), memory_space=pltpu.SMEM),
                  pl.BlockSpec(memory_space=pl.ANY)],
        out_specs=pl.BlockSpec((tm, D_MODEL), lambda i: (i, 0)),
        out_shape=_sds((t, D_MODEL)),
        scratch_shapes=[pltpu.VMEM((stage_slots * ROW_TILES, LANES), F32),
                        pltpu.VMEM((tm * ROW_TILES, LANES), F32), pltpu.SemaphoreType.DMA(())],
        compiler_params=_cp("arbitrary"),
        name="moe_combine",
    )(stage_flat, w_flat, table, ys)


def gather_table(cnt_tile, offsets):
    base = offsets + jnp.cumsum(cnt_tile, axis=0) - cnt_tile
    chunks = jnp.ceil(cnt_tile / GATHER_CHUNK)
    start = (jnp.cumsum(chunks, axis=1) - chunks) * GATHER_CHUNK
    total = jnp.sum(chunks, axis=1, keepdims=True)
    pad = jnp.zeros((cnt_tile.shape[0], LANES - 3 * N_EXPERTS - 1), F32)
    return jnp.concatenate([base, chunks, start, total, pad], axis=1).astype(I32).reshape(-1)


def moe(h, g, w_router, b_router, w_gu, b_gu, w_down, b_down):
    t = h.shape[0]
    xn, top_i, top_w, counts = moe_router(h, g, w_router, b_router)
    n_tiles = (t * TOP_K) // MOE_TILE + N_EXPERTS + 1
    padded = jnp.ceil(counts[0] / MOE_TILE) * MOE_TILE
    ends = jnp.cumsum(padded)
    offsets = (ends - padded)[None]
    tile_start = jnp.arange(n_tiles, dtype=F32) * MOE_TILE
    tile_expert = jnp.minimum(jnp.sum(ends[None, :] <= tile_start[:, None], axis=1), N_EXPERTS - 1).astype(I32)
    n_valid = (ends[-1:] / MOE_TILE).astype(I32)
    own_last = jnp.where(padded > 0, ends / MOE_TILE - 1, -1.0)
    filled = lax.cummax(own_last, axis=0)
    first = jnp.min(jnp.where(padded > 0, own_last, jnp.inf))
    last_tile = jnp.where(filled < 0, first, filled).astype(I32)
    pos, stage, cnt_tile = moe_rank(top_i, offsets)
    xs = moe_scatter(pos.reshape(t * TOP_K), xn, moe_pad_tiles(last_tile, n_tiles * MOE_TILE))
    ys = moe_experts(tile_expert, n_valid, xs, w_gu, b_gu, w_down, b_down)
    flat = lambda a: a.reshape(t * TOP_K)
    return moe_combine(flat(stage), flat(top_w), gather_table(cnt_tile, offsets), ys, t)


def _ple_body(h_ref, m_ref, p_ref, wp_ref, wg_ref, o_ref):
    h = h_ref[...] + m_ref[...]
    e = jnp.dot(p_ref[...].astype(BF16), wp_ref[...], preferred_element_type=F32)
    gate = jnp.dot(_rms(h).astype(BF16), wg_ref[...], preferred_element_type=F32)
    o_ref[...] = h + e * jax.nn.sigmoid(gate)


def ple(h, moe_out, p, w_ple, w_gate):
    t = h.shape[0]
    tm = min(ROW_TILE, t)
    row = pl.BlockSpec((tm, D_MODEL), lambda i: (i, 0))
    return pl.pallas_call(
        _ple_body,
        grid=(t // tm,),
        in_specs=[row, row, pl.BlockSpec((tm, PLE_DIM), lambda i: (i, 0)),
                  pl.BlockSpec((PLE_DIM, D_MODEL), lambda i: (0, 0)),
                  pl.BlockSpec((D_MODEL, D_MODEL), lambda i: (0, 0))],
        out_specs=row,
        out_shape=_sds((t, D_MODEL)),
        compiler_params=_cp("parallel"),
        name="ple",
    )(h, moe_out, p, w_ple, w_gate)


def _deinterleave(n):
    return np.concatenate([np.arange(0, n, 2), np.arange(1, n, 2)])


def _halves(w):
    pairs = w.reshape(*w.shape[:-1], w.shape[-1] // 2, 2)
    return jnp.concatenate([pairs[..., 0], pairs[..., 1]], axis=-1)


def prepare_layer(i, prm):
    pair = _deinterleave(HEAD_DIM)
    cols = np.arange(IN_COLS)
    for h in range(GQA_HEADS + GQA_KV_HEADS):
        lo = OFF_GQA + h * HEAD_DIM
        cols[lo:lo + HEAD_DIM] = lo + pair
    lam = (jnp.exp(jnp.sum(prm["diff_lam_q1"][i] * prm["diff_lam_k1"][i]))
           - jnp.exp(jnp.sum(prm["diff_lam_q2"][i] * prm["diff_lam_k2"][i])) + _lambda_init(i))
    return dict(
        norm1_g=prm["norm1_g"][i][None],
        w_in=prm["w_in"][i][:, cols].astype(BF16),
        s5=s5_tables(prm["s5_lam_re"][i], prm["s5_lam_im"][i], prm["s5_log_dt"][i], prm["s5_b_re"][i],
                     prm["s5_b_im"][i], prm["s5_c_re"][i], prm["s5_c_im"][i]),
        s5_d=prm["s5_d"][i][None],
        s5_w_glu=prm["s5_w_glu"][i].astype(BF16),
        s5_b_glu=prm["s5_b_glu"][i][None],
        gqa_q_g=prm["gqa_q_g"][i][pair][None],
        gqa_k_g=prm["gqa_k_g"][i][pair][None],
        diff_q_g=prm["diff_q_g"][i][None],
        diff_k_g=prm["diff_k_g"][i][None],
        diff_lam=lam.reshape(1).astype(F32),
        diff_subln_g=prm["diff_subln_g"][i][None],
        out_g_s5=prm["out_g_s5"][i][None],
        out_g_gqa=prm["out_g_gqa"][i][None],
        out_g_hy=prm["out_g_hy"][i][None],
        w_out=prm["w_out"][i].astype(BF16),
        norm2_g=prm["norm2_g"][i][None],
        w_router=_split_bf16(prm["w_router"][i]),
        b_router=prm["b_router"][i][None],
        w_gu=_halves(prm["w_gu"][i]).astype(BF16),
        b_gu=_halves(prm["b_gu"][i])[:, None, :],
        w_down=prm["w_down"][i].astype(BF16),
        b_down=prm["b_down"][i][:, None, :],
        w_ple=prm["w_ple"][i].astype(BF16),
        w_ple_gate=prm["w_ple_gate"][i].astype(BF16),
    )


def _lambda_init(i):
    return 0.8 - 0.6 * math.exp(-0.3 * i)


def run_trunk(x, p, prm, layers):
    batch, seq, _ = x.shape
    t = batch * seq
    cos_t, sin_t = rope_tables(seq)
    tabs = dft_tables(seq)
    bias = diff_bias_tiles(prm["rel_bias"], min(ATTN_UNIT, seq))
    h = x.reshape(t, D_MODEL)
    for i, lw in enumerate(layers):
        z_s5, z_gqa, z_diff, z_hy = in_proj(h, lw["norm1_g"], lw["w_in"], batch, seq)
        y_s5 = s5_mixer(z_s5, lw["s5"], lw["s5_d"], lw["s5_w_glu"], lw["s5_b_glu"], batch, seq)
        rep = GQA_HEADS // GQA_KV_HEADS
        q, k, v = attn_prep(z_gqa, lw["gqa_q_g"], lw["gqa_k_g"], cos_t, sin_t, batch, seq,
                            GQA_HEADS, GQA_KV_HEADS, GQA_KV_HEADS, True, lambda hq: hq // rep)
        o_gqa = flash_attention(q, k, v, lambda hq: hq // rep)
        q, k, v = attn_prep(z_diff, lw["diff_q_g"], lw["diff_k_g"], cos_t, sin_t, batch, seq,
                            2 * DIFF_HEADS, 2 * DIFF_HEADS, DIFF_HEADS, False, lambda hq: hq)
        o_diff = flash_attention(q, k, v, lambda hq: hq // 2, bias)
        hf = hyena_spectrum(seq, tabs, prm["hy_w1"][i], prm["hy_b1"][i], prm["hy_freq1"][i], prm["hy_w2"][i],
                            prm["hy_b2"][i], prm["hy_freq2"][i], prm["hy_w3"][i], prm["hy_decay"][i])
        y_hy = hyena_mixer(z_hy, prm["hy_conv_w"][i], prm["hy_conv_b"][i][None], hf, prm["hy_bias"][i],
                           tabs, batch, seq)
        h = mix_out(lw["diff_lam"], h, y_s5, o_gqa, o_diff, y_hy, lw["out_g_s5"], lw["out_g_gqa"],
                    lw["diff_subln_g"], lw["out_g_hy"], lw["w_out"], batch, seq, _lambda_init(i))
        m = moe(h, lw["norm2_g"], lw["w_router"], lw["b_router"], lw["w_gu"], lw["b_gu"], lw["w_down"],
                lw["b_down"])
        h = ple(h, m, p[i].reshape(t, PLE_DIM), lw["w_ple"], lw["w_ple_gate"])
    return h.reshape(batch, seq, D_MODEL)


def kernel(x_prompt, x_sample, p_prompt, p_sample, rel_bias, norm1_g, w_in, s5_lam_re, s5_lam_im, s5_log_dt,
           s5_b_re, s5_b_im, s5_c_re, s5_c_im, s5_d, s5_w_glu, s5_b_glu, gqa_q_g, gqa_k_g, diff_q_g, diff_k_g,
           diff_lam_q1, diff_lam_k1, diff_lam_q2, diff_lam_k2, diff_subln_g, hy_conv_w, hy_conv_b, hy_w1, hy_b1,
           hy_freq1, hy_w2, hy_b2, hy_freq2, hy_w3, hy_decay, hy_bias, out_g_s5, out_g_gqa, out_g_hy, w_out,
           norm2_g, w_router, b_router, w_gu, b_gu, w_down, b_down, w_ple, w_ple_gate):
    prm = dict(rel_bias=rel_bias, norm1_g=norm1_g, w_in=w_in, s5_lam_re=s5_lam_re, s5_lam_im=s5_lam_im,
               s5_log_dt=s5_log_dt, s5_b_re=s5_b_re, s5_b_im=s5_b_im, s5_c_re=s5_c_re, s5_c_im=s5_c_im,
               s5_d=s5_d, s5_w_glu=s5_w_glu, s5_b_glu=s5_b_glu, gqa_q_g=gqa_q_g, gqa_k_g=gqa_k_g,
               diff_q_g=diff_q_g, diff_k_g=diff_k_g, diff_lam_q1=diff_lam_q1, diff_lam_k1=diff_lam_k1,
               diff_lam_q2=diff_lam_q2, diff_lam_k2=diff_lam_k2, diff_subln_g=diff_subln_g,
               hy_conv_w=hy_conv_w, hy_conv_b=hy_conv_b, hy_w1=hy_w1, hy_b1=hy_b1, hy_freq1=hy_freq1,
               hy_w2=hy_w2, hy_b2=hy_b2, hy_freq2=hy_freq2, hy_w3=hy_w3, hy_decay=hy_decay, hy_bias=hy_bias,
               out_g_s5=out_g_s5, out_g_gqa=out_g_gqa, out_g_hy=out_g_hy, w_out=w_out, norm2_g=norm2_g,
               w_router=w_router, b_router=b_router, w_gu=w_gu, b_gu=b_gu, w_down=w_down, b_down=b_down,
               w_ple=w_ple, w_ple_gate=w_ple_gate)
    layers = [prepare_layer(i, prm) for i in range(DEPTH)]
    return (run_trunk(x_prompt, p_prompt, prm, layers), run_trunk(x_sample, p_sample, prm, layers))
```

```python
import functools
import math

import numpy as np
import jax
import jax.numpy as jnp
from jax import lax
from jax.experimental import pallas as pl
from jax.experimental.pallas import tpu as pltpu

F32 = jnp.float32
BF16 = jnp.bfloat16
I32 = jnp.int32
HI = lax.Precision.HIGHEST

D_MODEL = 1024
DEPTH = 2
GROUP_WIDTH = 256
HEAD_DIM = 64
S5_GROUP_DIM = 16
S5_GROUPS = 16
S5_STATE = 64
S5_NSTATE = S5_GROUPS * S5_STATE
GQA_HEADS = 4
GQA_KV_HEADS = 2
DIFF_HEADS = 4
HY_WIDTH = 256
HY_ORDER = 2
HY_BANDS = 16
HY_EMB = 2 * HY_BANDS + 1
HY_FFN = 64
N_EXPERTS = 32
TOP_K = 4
SWIGLU_LIMIT = 7.0
SWIGLU_ALPHA = 1.702
NUM_BUCKETS = 32
MAX_DISTANCE = 128
GRID_W = 64
ROPE_THETA = 10000.0
ROPE_AXIS_DIM = HEAD_DIM // 2
PLE_DIM = 256
NORM_EPS = 1e-6
ATTN_SCALE = HEAD_DIM ** -0.5
LOG2E = math.log2(math.e)

S5_COLS = GROUP_WIDTH
GQA_QW = GQA_HEADS * HEAD_DIM
GQA_KW = GQA_KV_HEADS * HEAD_DIM
GQA_COLS = GQA_QW + 2 * GQA_KW
DIFF_QW = DIFF_HEADS * 2 * HEAD_DIM
DIFF_COLS = 2 * DIFF_QW + DIFF_HEADS * HEAD_DIM
HY_COLS = (HY_ORDER + 1) * HY_WIDTH
OFF_GQA = S5_COLS
OFF_DIFF = OFF_GQA + GQA_COLS
OFF_HY = OFF_DIFF + DIFF_COLS
IN_COLS = OFF_HY + HY_COLS
IN_SPLITS = ((0, OFF_GQA), (OFF_GQA, OFF_DIFF), (OFF_DIFF, OFF_HY), (OFF_HY, IN_COLS))

LANES = 128
SUBLANES = 8
ROW_TILES = D_MODEL // LANES
FFT_N2 = 128
FFT_ROWS = 64
VMEM_LIMIT = 56 * 1024 * 1024

ROW_TILE = 1024
ATTN_Q_TILE = 2048
ATTN_K_TILE = 2048
ATTN_UNIT = 512
ATTN_SUB = 512
ATTN_SUB_PLAIN = 1024
S5_TIME_BLOCK = 128
MOE_TILE = 512
MOE_TOK_TILE = 1024
MIX_UNROLL = 4
GATHER_CHUNK = 16


def _cp(*sem):
    return pltpu.CompilerParams(dimension_semantics=sem, vmem_limit_bytes=VMEM_LIMIT)


def _sds(shape, dtype=F32):
    return jax.ShapeDtypeStruct(shape, dtype)


def _rms(x, g=None):
    y = x * lax.rsqrt(jnp.mean(x * x, axis=-1, keepdims=True) + NORM_EPS)
    return y if g is None else y * g


def _inproj_body(x_ref, g_ref, w_ref, *o_refs):
    u = _rms(x_ref[...], g_ref[...]).astype(BF16)
    for o_ref, (lo, hi) in zip(o_refs, IN_SPLITS):
        o_ref[...] = jnp.dot(u, w_ref[:, lo:hi], preferred_element_type=F32)


def in_proj(h, g, w, batch, seq):
    t = h.shape[0]
    tm = min(ROW_TILE, seq)
    nb = seq // tm
    row_major = lambda i: (i, 0)
    time_major = lambda i: (i % nb, i // nb)
    widths = [hi - lo for lo, hi in IN_SPLITS]
    return pl.pallas_call(
        _inproj_body,
        grid=(t // tm,),
        in_specs=[pl.BlockSpec((tm, D_MODEL), row_major),
                  pl.BlockSpec((1, D_MODEL), lambda i: (0, 0)),
                  pl.BlockSpec((D_MODEL, IN_COLS), lambda i: (0, 0))],
        out_specs=[pl.BlockSpec((tm, widths[0]), time_major)] + [pl.BlockSpec((tm, w_), row_major) for w_ in widths[1:]],
        out_shape=[_sds((seq, batch * widths[0]))] + [_sds((t, w_)) for w_ in widths[1:]],
        compiler_params=_cp("parallel"),
        name="in_proj",
    )(h, g, w)


def s5_tables(lam_re, lam_im, log_dt, b_re, b_im, c_re, c_im):
    lr = jnp.minimum(lam_re, -1e-4)
    li = lam_im
    dt = jnp.exp(log_dt)[..., None]
    mag = jnp.exp(lr * dt)
    ar = mag * jnp.cos(li * dt)
    ai = mag * jnp.sin(li * dt)
    den = lr * lr + li * li
    cr = ((ar - 1.0) * lr + ai * li) / den
    ci = (ai * lr - (ar - 1.0) * li) / den
    bbr = cr[..., None] * b_re - ci[..., None] * b_im
    bbi = cr[..., None] * b_im + ci[..., None] * b_re
    eye = jnp.eye(S5_GROUPS, dtype=F32)
    bb = jnp.stack([bbr, bbi], axis=1)
    wb = jnp.einsum("dpgnh,gk->dghpkn", bb, eye).reshape(2, S5_COLS, 2 * S5_NSTATE)
    cc = jnp.stack([c_re, -c_im], axis=0)
    wc = jnp.einsum("pdghn,gk->dpgnkh", cc, eye).reshape(2, 2 * S5_NSTATE, S5_COLS)
    a_re = jnp.repeat(ar.reshape(2, S5_NSTATE), 4, axis=0)
    a_im = jnp.repeat(ai.reshape(2, S5_NSTATE), 4, axis=0)
    return wb.astype(BF16), wc.astype(BF16), a_re, a_im


def _s5_body(uf_ref, ub_ref, wb_ref, wc_ref, ar_ref, ai_ref, yf_ref, yb_ref, bf_ref, bb_ref, xr_ref, xi_ref, *, tb):
    @pl.when(pl.program_id(0) == 0)
    def _():
        xr_ref[...] = jnp.zeros_like(xr_ref)
        xi_ref[...] = jnp.zeros_like(xi_ref)

    bf_ref[...] = jnp.dot(uf_ref[...].astype(BF16), wb_ref[0], preferred_element_type=F32)
    bb_ref[...] = jnp.dot(ub_ref[...].astype(BF16), wb_ref[1], preferred_element_type=F32)
    ar = ar_ref[...]
    ai = ai_ref[...]
    low = lax.broadcasted_iota(I32, (SUBLANES, 1), 0) < 4
    half = SUBLANES // 2
    re = slice(0, S5_NSTATE)
    im = slice(S5_NSTATE, 2 * S5_NSTATE)

    def pair(m, carry):
        xr, xi = carry
        rf = pl.ds(pl.multiple_of(m * SUBLANES, SUBLANES), SUBLANES)
        rb = pl.ds(pl.multiple_of((tb // 2 - 1 - m) * SUBLANES, SUBLANES), SUBLANES)
        f_r, f_i, b_r, b_i = bf_ref[rf, re], bf_ref[rf, im], bb_ref[rb, re], bb_ref[rb, im]
        in_r = jnp.where(low, f_r, b_r)
        in_i = jnp.where(low, f_i, b_i)
        x1r = ar * xr - ai * xi + in_r
        x1i = ar * xi + ai * xr + in_i
        in_r = pltpu.roll(jnp.where(low, b_r, f_r), half, 0)
        in_i = pltpu.roll(jnp.where(low, b_i, f_i), half, 0)
        x2r = ar * x1r - ai * x1i + in_r
        x2i = ar * x1i + ai * x1r + in_i
        s2r = pltpu.roll(x2r, half, 0)
        s2i = pltpu.roll(x2i, half, 0)
        bf_ref[rf, re] = jnp.where(low, x1r, s2r)
        bf_ref[rf, im] = jnp.where(low, x1i, s2i)
        bb_ref[rb, re] = jnp.where(low, s2r, x1r)
        bb_ref[rb, im] = jnp.where(low, s2i, x1i)
        return x2r, x2i

    xr, xi = lax.fori_loop(0, tb // 2, pair, (xr_ref[...], xi_ref[...]))
    xr_ref[...] = xr
    xi_ref[...] = xi
    yf_ref[...] = jnp.dot(bf_ref[...].astype(BF16), wc_ref[0], preferred_element_type=F32)
    yb_ref[...] = jnp.dot(bb_ref[...].astype(BF16), wc_ref[1], preferred_element_type=F32)


def s5_scan(u, wb, wc, a_re, a_im, seq):
    tb = min(S5_TIME_BLOCK, seq)
    nb = seq // tb
    rows = tb * 4
    fwd = pl.BlockSpec((rows, S5_COLS), lambda i: (i, 0))
    bwd = pl.BlockSpec((rows, S5_COLS), lambda i: (nb - 1 - i, 0))
    return pl.pallas_call(
        functools.partial(_s5_body, tb=tb),
        grid=(nb,),
        in_specs=[fwd, bwd,
                  pl.BlockSpec((2, S5_COLS, 2 * S5_NSTATE), lambda i: (0, 0, 0)),
                  pl.BlockSpec((2, 2 * S5_NSTATE, S5_COLS), lambda i: (0, 0, 0)),
                  pl.BlockSpec((SUBLANES, S5_NSTATE), lambda i: (0, 0)),
                  pl.BlockSpec((SUBLANES, S5_NSTATE), lambda i: (0, 0))],
        out_specs=[fwd, bwd],
        out_shape=[_sds((seq * 4, S5_COLS))] * 2,
        scratch_shapes=[pltpu.VMEM((rows, 2 * S5_NSTATE), F32), pltpu.VMEM((rows, 2 * S5_NSTATE), F32),
                        pltpu.VMEM((SUBLANES, S5_NSTATE), F32), pltpu.VMEM((SUBLANES, S5_NSTATE), F32)],
        compiler_params=_cp("arbitrary"),
        name="s5_scan",
    )(u, u, wb, wc, a_re, a_im)


def _s5_out_body(yf_ref, yb_ref, u_ref, d_ref, w_ref, b_ref, o_ref):
    y = jax.nn.gelu(yf_ref[...] + yb_ref[...] + u_ref[...] * d_ref[...])
    gate = jnp.dot(y.astype(BF16), w_ref[...], preferred_element_type=F32) + b_ref[...]
    o_ref[...] = y * jax.nn.sigmoid(gate)


def s5_out(yf, yb, u, d, w_glu, b_glu):
    t = u.shape[0]
    tm = min(ROW_TILE, t)
    row = pl.BlockSpec((tm, S5_COLS), lambda i: (i, 0))
    vec = pl.BlockSpec((1, S5_COLS), lambda i: (0, 0))
    return pl.pallas_call(
        _s5_out_body,
        grid=(t // tm,),
        in_specs=[row, row, row, vec, pl.BlockSpec((S5_COLS, S5_COLS), lambda i: (0, 0)), vec],
        out_specs=row,
        out_shape=_sds((t, S5_COLS)),
        compiler_params=_cp("parallel"),
        name="s5_out",
    )(yf, yb, u, d, w_glu, b_glu)


def s5_mixer(z_s5, tabs, d, w_glu, b_glu, batch, seq):
    assert batch == 4 and seq % 2 == 0, "the scan packs 4 sequences x 2 directions onto the 8 sublanes"
    u = z_s5.reshape(seq * batch, S5_COLS)
    yf, yb = s5_scan(u, *tabs, seq)
    return s5_out(yf, yb, u, d, w_glu, b_glu).reshape(seq, batch * S5_COLS)


def _prep_body(z_ref, qg_ref, kg_ref, c_ref, s_ref, q_ref, k_ref, v_ref, *, nq, nk, nv, rope, k_src):
    tm = z_ref.shape[0]
    lane = lax.broadcasted_iota(I32, (tm, LANES), 1)
    low = lane < HEAD_DIM

    def norm_pair(x, g):
        sq = x * x
        ss_lo = jnp.sum(jnp.where(low, sq, 0.0), axis=-1, keepdims=True)
        ss_hi = jnp.sum(jnp.where(low, 0.0, sq), axis=-1, keepdims=True)
        y = x * lax.rsqrt(jnp.where(low, ss_lo, ss_hi) * (1.0 / HEAD_DIM) + NORM_EPS) * g
        if rope:
            half = HEAD_DIM // 2
            partner = jnp.where(lane % HEAD_DIM < half, pltpu.roll(y, LANES - half, 1), pltpu.roll(y, half, 1))
            y = y * c_ref[...] + partner * s_ref[...]
        return y

    qg = qg_ref[...]
    kg = kg_ref[...]
    for p in range(nq // 2):
        q_ref[0, p] = (norm_pair(z_ref[:, p * LANES:(p + 1) * LANES], qg) * (ATTN_SCALE * LOG2E)).astype(BF16)
    k_lo = nq * HEAD_DIM
    pairs = [norm_pair(z_ref[:, k_lo + p * LANES:k_lo + (p + 1) * LANES], kg) for p in range(nk // 2)]
    for h in range(nq):
        src = k_src(h)
        x = pairs[src // 2]
        if src % 2 != h % 2:
            x = pltpu.roll(x, HEAD_DIM, 1)
        k_ref[0, h] = jnp.where(low if h % 2 == 0 else jnp.logical_not(low), x, 0.0).astype(BF16)
    ones_col = jnp.where(lax.broadcasted_iota(I32, (tm, LANES - HEAD_DIM), 1) == 0, 1.0, 0.0)
    for h in range(nv):
        lo = (nq + nk + h) * HEAD_DIM
        v_ref[0, h] = jnp.concatenate([z_ref[:, lo:lo + HEAD_DIM], ones_col], axis=1).astype(BF16)


def attn_prep(z, qg, kg, cos_t, sin_t, batch, seq, nq, nk, nv, rope, k_src):
    width = (nq + nk + nv) * HEAD_DIM
    tm = min(ROW_TILE, seq)
    nb = seq // tm
    vec = pl.BlockSpec((1, LANES), lambda b, i: (0, 0))
    tab = pl.BlockSpec((tm, LANES), lambda b, i: (i, 0))
    heads = lambda n: pl.BlockSpec((1, n, tm, LANES), lambda b, i: (b, 0, i, 0))
    return pl.pallas_call(
        functools.partial(_prep_body, nq=nq, nk=nk, nv=nv, rope=rope, k_src=k_src),
        grid=(batch, nb),
        in_specs=[pl.BlockSpec((tm, width), lambda b, i: (b * nb + i, 0)), vec, vec, tab, tab],
        out_specs=[heads(nq // 2), heads(nq), heads(nv)],
        out_shape=[_sds((batch, nq // 2, seq, LANES), BF16), _sds((batch, nq, seq, LANES), BF16),
                   _sds((batch, nv, seq, LANES), BF16)],
        compiler_params=_cp("parallel", "parallel"),
        name="attn_prep",
    )(z, jnp.tile(qg, (1, 2)), jnp.tile(kg, (1, 2)), cos_t, sin_t)


def _flash_body(q_ref, k_ref, v_ref, *rest, unit, sub, sub_far, has_bias):
    if has_bias:
        b_ref, o_ref, m_ref, acc_ref = rest
    else:
        o_ref, m_ref, acc_ref = rest
    i = pl.program_id(2)
    j = pl.program_id(3)
    tq = q_ref.shape[2]
    tk = k_ref.shape[2]
    q_units = tq // unit
    k_units = tk // unit

    @pl.when(j == 0)
    def _():
        m_ref[...] = jnp.full_like(m_ref, -jnp.inf)
        acc_ref[...] = jnp.zeros_like(acc_ref)

    def attend(rows, offset=None, far_sel=None):
        k = k_ref[0, 0]
        v = v_ref[0, 0]
        for g in range(tq // rows):
            r0 = g * rows
            s = lax.dot_general(q_ref[0, 0, r0:r0 + rows, :], k, (((1,), (1,)), ((), ())),
                                preferred_element_type=F32)
            m_prev = m_ref[r0:r0 + rows, :]
            kinds = [None] * k_units
            if offset is not None:
                tile_of = [offset * k_units + c - r0 // unit for c in range(k_units)]
                kinds = [None if abs(d) <= 1 else (0 if d < 0 else 4) for d in tile_of]
                s = jnp.concatenate(
                    [s[:, c * unit:(c + 1) * unit] + b_ref[0, d + 2, r0 % unit:r0 % unit + rows, :]
                     if abs(d) <= 1 else s[:, c * unit:(c + 1) * unit] for c, d in enumerate(tile_of)], axis=1)
            elif far_sel is not None:
                kinds = [far_sel] * k_units
            consts = {kind: b_ref[0, kind, 0:1, 0:LANES] for kind in set(kinds) if kind is not None}
            per_unit = unit // LANES
            blocks = [s[:, b * LANES:(b + 1) * LANES] for b in range(tk // LANES)]
            m_new = m_prev
            for kind in sorted(set(kinds), key=str):
                members = [blk for b, blk in enumerate(blocks) if kinds[b // per_unit] == kind]
                bmax = members[0]
                for blk in members[1:]:
                    bmax = jnp.maximum(bmax, blk)
                kind_max = jnp.max(bmax, axis=-1, keepdims=True)
                m_new = jnp.maximum(m_new, kind_max if kind is None else kind_max + consts[kind])
            shifts = [m_new if kind is None else m_new - consts[kind] for kind in kinds]
            alpha = jnp.exp2(m_prev - m_new)
            p = jnp.concatenate([jnp.exp2(blk - shifts[b // per_unit]) for b, blk in enumerate(blocks)],
                                axis=1).astype(BF16)
            acc_ref[r0:r0 + rows, :] = alpha * acc_ref[r0:r0 + rows, :] + jnp.dot(p, v, preferred_element_type=F32)
            m_ref[r0:r0 + rows, :] = m_new

    if has_bias:
        assert q_units == k_units and sub == unit, "the near-diagonal variants assume square steps of whole units"
        pl.when(j - i >= 2)(lambda: attend(sub_far, far_sel=4))
        pl.when(j - i <= -2)(lambda: attend(sub_far, far_sel=0))
        for offset in (-1, 0, 1):
            pl.when(j - i == offset)(functools.partial(attend, sub, offset=offset))
    else:
        attend(sub)

    @pl.when(j == pl.num_programs(3) - 1)
    def _():
        acc = acc_ref[...]
        o_ref[0, 0] = acc[:, :HEAD_DIM] / acc[:, HEAD_DIM:HEAD_DIM + 1]


def flash_attention(q, k, v, v_of_q, bias=None):
    batch, nq, seq, _ = k.shape
    tq = min(ATTN_Q_TILE, seq)
    tk = min(ATTN_K_TILE, seq)
    unit = min(ATTN_UNIT, seq)
    sub = min(ATTN_SUB if bias is not None else ATTN_SUB_PLAIN, tq)
    sub_far = min(ATTN_SUB_PLAIN, tq)
    in_specs = [pl.BlockSpec((1, 1, tq, LANES), lambda b, h, i, j: (b, h // 2, i, 0)),
                pl.BlockSpec((1, 1, tk, LANES), lambda b, h, i, j: (b, h, j, 0)),
                pl.BlockSpec((1, 1, tk, LANES), lambda b, h, i, j: (b, v_of_q(h), j, 0))]
    args = [q, k, v]
    if bias is not None:
        assert bias.shape[2] == unit
        in_specs.append(pl.BlockSpec((1, 5, unit, unit), lambda b, h, i, j: (v_of_q(h), 0, 0, 0)))
        args.append(bias)
    return pl.pallas_call(
        functools.partial(_flash_body, unit=unit, sub=sub, sub_far=sub_far, has_bias=bias is not None),
        grid=(batch, nq, seq // tq, seq // tk),
        in_specs=in_specs,
        out_specs=pl.BlockSpec((1, 1, tq, HEAD_DIM), lambda b, h, i, j: (b, h, i, 0)),
        out_shape=_sds((batch, nq, seq, HEAD_DIM)),
        scratch_shapes=[pltpu.VMEM((tq, LANES), F32), pltpu.VMEM((tq, LANES), F32)],
        compiler_params=_cp("parallel", "parallel", "parallel", "arbitrary"),
        name="flash_bias" if bias is not None else "flash",
    )(*args)


def _t5_bucket_table(t):
    rel = np.arange(-(3 * t - 1), 3 * t)
    half = NUM_BUCKETS // 2
    max_exact = half // 2
    ret = np.where(rel > 0, half, 0)
    n = np.abs(rel)
    nf = np.maximum(n, 1).astype(np.float64)
    large = max_exact + (np.log(nf / max_exact) / math.log(MAX_DISTANCE / max_exact) * (half - max_exact)).astype(np.int64)
    large = np.minimum(large, half - 1)
    return ret + np.where(n < max_exact, n, large)


def diff_bias_tiles(rel_bias, t):
    assert t >= MAX_DISTANCE, "offsets of two or more tiles must lie in the saturated buckets"
    heads = rel_bias.shape[1]
    vec = rel_bias[jnp.asarray(_t5_bucket_table(t), dtype=I32)].T * LOG2E
    tiles = []
    for d in range(-2, 3):
        lo = d * t + 2 * t
        w = vec[:, lo:lo + 2 * t - 1]
        u = jnp.concatenate([w[:, t - 1:], jnp.zeros((heads, 1), F32), w[:, :t - 1]], axis=1)
        skew = jnp.tile(u, (1, t))[:, :t * (2 * t - 1)].reshape(heads, t, 2 * t - 1)
        tiles.append(skew[:, :, :t])
    return jnp.stack(tiles, axis=1)


def rope_tables(seq):
    rows = seq // GRID_W
    row = jnp.repeat(jnp.arange(rows, dtype=F32), GRID_W)
    col = jnp.tile(jnp.arange(GRID_W, dtype=F32), rows)
    freq = ROPE_THETA ** (-jnp.arange(0, ROPE_AXIS_DIM, 2, dtype=F32) / ROPE_AXIS_DIM)
    ang = jnp.concatenate([row[:, None] * freq, col[:, None] * freq], -1)
    cos, sin = jnp.cos(ang), jnp.sin(ang)
    return jnp.concatenate([cos, cos] * 2, -1), jnp.concatenate([-sin, sin] * 2, -1)


def _conv3_body(x0, x1, x2, w0, w1, w2, b0, b1, b2, o0, o1, o2):
    for x_ref, w_ref, b_ref, o_ref in ((x0, w0, b0, o0), (x1, w1, b1, o1), (x2, w2, b2, o2)):
        x = x_ref[0]
        n = x.shape[0]
        t = lax.broadcasted_iota(I32, (n, 1), 0)
        prev = jnp.where(t == 0, 0.0, pltpu.roll(x, 1, 0))
        nxt = jnp.where(t == n - 1, 0.0, pltpu.roll(x, n - 1, 0))
        o_ref[0] = w_ref[0:1] * prev + w_ref[1:2] * x + w_ref[2:3] * nxt + b_ref[...]


def hyena_conv3(z_hy, conv_w, conv_b, batch, seq):
    z = z_hy.reshape(batch, seq, HY_COLS)
    nc = HY_WIDTH // LANES
    x_spec = lambda g: pl.BlockSpec((1, seq, LANES), lambda b, c: (b, 0, g * nc + c))
    w_spec = lambda g: pl.BlockSpec((3, LANES), lambda b, c: (0, g * nc + c))
    b_spec = lambda g: pl.BlockSpec((1, LANES), lambda b, c: (0, g * nc + c))
    out = pl.BlockSpec((1, seq, LANES), lambda b, c: (b, 0, c))
    return pl.pallas_call(
        _conv3_body,
        grid=(batch, nc),
        in_specs=[x_spec(0), x_spec(1), x_spec(2), w_spec(0), w_spec(1), w_spec(2),
                  b_spec(0), b_spec(1), b_spec(2)],
        out_specs=[out, out, out],
        out_shape=[_sds((batch, seq, HY_WIDTH))] * 3,
        compiler_params=_cp("parallel", "parallel"),
        name="hyena_conv3",
    )(z, z, z, conv_w, conv_w, conv_w, conv_b, conv_b, conv_b)


def _filt_body(feat_ref, dist_ref, w1, b1, f1, w2, b2, f2, w3, dec, h_ref, s_ref):
    dot = functools.partial(jnp.dot, precision=HI, preferred_element_type=F32)
    hid = jnp.sin(f1[...] * (dot(feat_ref[...], w1[...]) + b1[...]))
    hid = jnp.sin(f2[...] * (dot(hid, w2[...]) + b2[...]))
    h = dot(hid, w3[...]) * jnp.exp(-dist_ref[...] * jnp.abs(dec[...]))
    h_ref[...] = h

    @pl.when(pl.program_id(0) == 0)
    def _():
        s_ref[...] = jnp.zeros_like(s_ref)

    s_ref[...] += jnp.sum(jnp.abs(h), axis=0, keepdims=True)


def hyena_filter(seq, w1, b1, f1, w2, b2, f2, w3, decay):
    j = jnp.arange(seq, dtype=F32)
    tt = j / seq
    ang = 2.0 * jnp.pi * tt[:, None] * jnp.arange(1, HY_BANDS + 1, dtype=F32)
    feat = jnp.concatenate([tt[:, None], jnp.cos(ang), jnp.sin(ang)], -1)
    feat = jnp.pad(feat, ((0, 0), (0, LANES - HY_EMB)))
    half = seq // 2
    dist = (jnp.abs(j - half) / half)[:, None]
    w1p = jnp.pad(w1, ((0, LANES - HY_EMB), (0, 0)))
    tm = min(1024, seq)
    hw = HY_ORDER * HY_WIDTH
    const = lambda i: (0, 0)
    return pl.pallas_call(
        _filt_body,
        grid=(seq // tm,),
        in_specs=[pl.BlockSpec((tm, LANES), lambda i: (i, 0)), pl.BlockSpec((tm, 1), lambda i: (i, 0)),
                  pl.BlockSpec((LANES, HY_FFN), const), pl.BlockSpec((1, HY_FFN), const),
                  pl.BlockSpec((1, HY_FFN), const), pl.BlockSpec((HY_FFN, HY_FFN), const),
                  pl.BlockSpec((1, HY_FFN), const), pl.BlockSpec((1, HY_FFN), const),
                  pl.BlockSpec((HY_FFN, hw), const), pl.BlockSpec((1, hw), const)],
        out_specs=[pl.BlockSpec((tm, hw), lambda i: (i, 0)), pl.BlockSpec((1, hw), const)],
        out_shape=[_sds((seq, hw)), _sds((1, hw))],
        compiler_params=_cp("arbitrary"),
        name="hyena_filter",
    )(feat, dist, w1p, b1[None], f1[None], w2, b2[None], f2[None], w3, decay.reshape(1, hw))


def dft_tables(seq):
    n = 2 * seq
    n2 = FFT_N2
    n1 = n // n2
    n1h = n1 // 2
    nk = -(-(n1h + 1) // SUBLANES) * SUBLANES
    k1 = jnp.arange(nk, dtype=I32)
    live = (k1 <= n1h).astype(F32)
    ang = (2.0 * jnp.pi / n1) * ((k1[:, None] * jnp.arange(n1h, dtype=I32)[None, :]) % n1).astype(F32)
    outer_fwd = jnp.concatenate([jnp.cos(ang), -jnp.sin(ang)], axis=0) * jnp.tile(live, 2)[:, None]
    t1 = jnp.arange(n1h, dtype=I32) + n1 // 4
    ang = (2.0 * jnp.pi / n1) * ((t1[:, None] * k1[None, :]) % n1).astype(F32)
    weight = live * jnp.where((k1 == 0) | (k1 == n1h), 1.0, 2.0) / n
    outer_inv = jnp.concatenate([jnp.cos(ang), -jnp.sin(ang)], axis=1) * jnp.tile(weight, 2)[None, :]
    k2 = jnp.arange(n2, dtype=I32)
    t2 = jnp.arange(n2, dtype=I32)
    phase = (t2[None, None, :] * (k2[None, :, None] * n1 + k1[:, None, None])) % n
    ang = (2.0 * jnp.pi / n) * phase.astype(F32)
    fr, fi = jnp.cos(ang), -jnp.sin(ang)
    inner = jnp.concatenate([jnp.concatenate([fr, -fi], axis=2),
                             jnp.concatenate([fi, fr], axis=2)], axis=1)
    return _split_bf16(outer_fwd), _split_bf16(outer_inv), _split_bf16(inner), _split_bf16(jnp.swapaxes(inner, 1, 2))


def _split_bf16(x):
    hi = x.astype(BF16)
    return hi, (x - hi.astype(F32)).astype(BF16)


def _dot_split(w, a):
    w_hi, w_lo = w
    a_hi, a_lo = _split_bf16(a)
    dot = functools.partial(jnp.dot, preferred_element_type=F32)
    return dot(w_hi, a_hi) + (dot(w_hi, a_lo) + dot(w_lo, a_hi))


def _dot_split_rhs(a, w):
    w_hi, w_lo = w
    a_hi, a_lo = _split_bf16(a)
    dot = functools.partial(jnp.dot, preferred_element_type=F32)
    return dot(a_hi, w_hi) + (dot(a_lo, w_hi) + dot(a_hi, w_lo))


def _every(ref, s, n):
    return ref[pl.ds(s, n, stride=FFT_ROWS), :]


def _fft_outer_body(f_ref, fl_ref, x_ref, o_ref):
    n1h = x_ref.shape[0]
    rows = f_ref.shape[0]
    x2 = x_ref.reshape(n1h * FFT_ROWS, LANES)
    o2 = o_ref.reshape(rows * FFT_ROWS, LANES)
    x = jnp.concatenate([_every(x2, s, n1h) for s in range(FFT_ROWS)], axis=1)
    r = _dot_split((f_ref[...], fl_ref[...]), x)
    for s in range(FFT_ROWS):
        o2[pl.ds(s, rows, stride=FFT_ROWS), :] = r[:, s * LANES:(s + 1) * LANES]


def fft_outer(x4, table):
    batch, n1h, n2, ch = x4.shape
    nk = table[0].shape[0] // 2
    tab = pl.BlockSpec((2 * nk, n1h), lambda b, c, l: (0, 0))
    return pl.pallas_call(
        _fft_outer_body,
        grid=(batch, n2 // FFT_ROWS, ch // LANES),
        in_specs=[tab, tab, pl.BlockSpec((None, n1h, FFT_ROWS, LANES), lambda b, c, l: (b, 0, c, l))],
        out_specs=pl.BlockSpec((None, 2, nk, FFT_ROWS, LANES), lambda b, c, l: (b, 0, 0, c, l)),
        out_shape=_sds((batch, 2, nk, n2, ch)),
        compiler_params=_cp("parallel", "parallel", "parallel"),
        name="fft_outer",
    )(*table, x4)


def _fft_filter_body(mh_ref, ml_ref, s_ref, a_ref, o_ref):
    n2 = FFT_N2
    a = a_ref[0, :, 0].reshape(2 * n2, HY_WIDTH)
    x = _dot_split((mh_ref[0], ml_ref[0]), a) * (1.0 / s_ref[...])
    o_ref[:, 0] = x.reshape(2, n2, HY_WIDTH)


def fft_filter(a5, inner, s):
    _, _, n1, n2, ch = a5.shape
    return pl.pallas_call(
        _fft_filter_body,
        grid=(n1, ch // HY_WIDTH),
        in_specs=[pl.BlockSpec((1, 2 * n2, 2 * n2), lambda k, c: (k, 0, 0)),
                  pl.BlockSpec((1, 2 * n2, 2 * n2), lambda k, c: (k, 0, 0)),
                  pl.BlockSpec((1, HY_WIDTH), lambda k, c: (0, c)),
                  pl.BlockSpec((1, 2, 1, n2, HY_WIDTH), lambda k, c: (0, 0, k, 0, c))],
        out_specs=pl.BlockSpec((2, 1, n2, HY_WIDTH), lambda k, c: (0, k, 0, c)),
        out_shape=_sds((2, n1, n2, ch)),
        compiler_params=_cp("parallel", "parallel"),
        name="fft_filter",
    )(*inner, s, a5)


def _fft_mid_body(mh_ref, ml_ref, th_ref, tl_ref, h_ref, a_ref, o_ref, *, nb):
    n2 = FFT_N2
    m = (mh_ref[0], ml_ref[0])
    mt = (th_ref[0], tl_ref[0])
    hr = h_ref[0, 0]
    hi = h_ref[1, 0]
    for b in range(nb):
        x = _dot_split(m, a_ref[b, :, 0].reshape(2 * n2, HY_WIDTH))
        xr, xi = x[:n2], x[n2:]
        y = jnp.concatenate([xr * hr - xi * hi, xr * hi + xi * hr], axis=0)
        o_ref[b, :, 0] = _dot_split(mt, y).reshape(2, n2, HY_WIDTH)


def fft_mid(a5, inner, inner_t, hf, order):
    nb, _, n1, n2, ch = a5.shape
    mat = pl.BlockSpec((1, 2 * n2, 2 * n2), lambda k: (k, 0, 0))
    blk = pl.BlockSpec((nb, 2, 1, n2, ch), lambda k: (0, 0, k, 0, 0))
    return pl.pallas_call(
        functools.partial(_fft_mid_body, nb=nb),
        grid=(n1,),
        in_specs=[mat, mat, mat, mat, pl.BlockSpec((2, 1, n2, ch), lambda k: (0, k, 0, order)), blk],
        out_specs=blk,
        out_shape=_sds(a5.shape),
        compiler_params=_cp("parallel"),
        name="fft_mid",
    )(*inner, *inner_t, hf, a5)


def _fft_inv_body(g_ref, gl_ref, b_ref, gate_ref, y_ref, fb_ref, o_ref):
    n1h, rows = g_ref.shape
    b2 = b_ref.reshape(rows * FFT_ROWS, LANES)
    gate2 = gate_ref.reshape(n1h * FFT_ROWS, LANES)
    y2 = y_ref.reshape(n1h * FFT_ROWS, LANES)
    o2 = o_ref.reshape(n1h * FFT_ROWS, LANES)
    bm = jnp.concatenate([_every(b2, s, rows) for s in range(FFT_ROWS)], axis=1)
    c = _dot_split((g_ref[...], gl_ref[...]), bm)
    fb = fb_ref[...]
    for s in range(FFT_ROWS):
        conv = c[:, s * LANES:(s + 1) * LANES]
        o2[pl.ds(s, n1h, stride=FFT_ROWS), :] = _every(gate2, s, n1h) * (conv + fb * _every(y2, s, n1h))


def fft_inv_gate(b5, table, gate4, y4, fbias):
    batch, _, nk, n2, ch = b5.shape
    n1h = table[0].shape[0]
    blk = pl.BlockSpec((None, n1h, FFT_ROWS, LANES), lambda b, c, l: (b, 0, c, l))
    tab = pl.BlockSpec((n1h, 2 * nk), lambda b, c, l: (0, 0))
    return pl.pallas_call(
        _fft_inv_body,
        grid=(batch, n2 // FFT_ROWS, ch // LANES),
        in_specs=[tab, tab,
                  pl.BlockSpec((None, 2, nk, FFT_ROWS, LANES), lambda b, c, l: (b, 0, 0, c, l)), blk, blk,
                  pl.BlockSpec((1, LANES), lambda b, c, l: (0, l))],
        out_specs=blk,
        out_shape=_sds((batch, n1h, n2, ch)),
        compiler_params=_cp("parallel", "parallel", "parallel"),
        name="fft_inv_gate",
    )(*table, b5, gate4, y4, fbias.reshape(1, ch))


def hyena_spectrum(seq, tabs, w1, b1, f1, w2, b2, f2, w3, decay):
    outer_fwd, _, inner, _ = tabs
    n1h = outer_fwd[0].shape[1]
    hw = HY_ORDER * HY_WIDTH
    h, s = hyena_filter(seq, w1, b1, f1, w2, b2, f2, w3, decay)
    return fft_filter(fft_outer(h.reshape(1, n1h, FFT_N2, hw), outer_fwd), inner, s)


def hyena_mixer(z_hy, conv_w, conv_b, hf, f_bias, tabs, batch, seq):
    outer_fwd, outer_inv, inner, inner_t = tabs
    n1h = outer_fwd[0].shape[1]
    shape4 = (batch, n1h, FFT_N2, HY_WIDTH)
    v, g0, g1 = hyena_conv3(z_hy, conv_w, conv_b, batch, seq)
    y = v.reshape(shape4)
    for order, gate in enumerate((g0, g1)):
        bm = fft_mid(fft_outer(y, outer_fwd), inner, inner_t, hf, order)
        y = fft_inv_gate(bm, outer_inv, gate.reshape(shape4), y, f_bias[order])
    return y.reshape(batch * seq, HY_WIDTH)


def _mix_body(lam_ref, h_ref, s5_ref, gqa_ref, diff_ref, hy_ref, gs5, ggqa, gsub, ghy, w_ref, o_ref, *, keep):
    lam = lam_ref[0]
    a = _rms(s5_ref[...], gs5[...])
    b = _rms(jnp.concatenate([gqa_ref[0, h] for h in range(GQA_HEADS)], axis=1), ggqa[...])
    c = jnp.concatenate(
        [_rms(diff_ref[0, 2 * h] - lam * diff_ref[0, 2 * h + 1], gsub[...]) * keep for h in range(DIFF_HEADS)],
        axis=1)
    d = _rms(hy_ref[...], ghy[...])
    mixed = jnp.concatenate([a, b, c, d], axis=1).astype(BF16)
    o_ref[...] = h_ref[...] + jnp.dot(mixed, w_ref[...], preferred_element_type=F32)


def mix_out(lam, h, y_s5, o_gqa, o_diff, y_hy, gs5, ggqa, gsub, ghy, w_out, batch, seq, lambda_init):
    tm = min(ROW_TILE, seq)
    nb = seq // tm
    row = lambda w: pl.BlockSpec((tm, w), lambda b, i: (b * nb + i, 0))
    vec = lambda w: pl.BlockSpec((1, w), lambda b, i: (0, 0))
    heads = lambda n: pl.BlockSpec((1, n, tm, HEAD_DIM), lambda b, i: (b, 0, i, 0))
    time_major = pl.BlockSpec((tm, GROUP_WIDTH), lambda b, i: (i, b))
    return pl.pallas_call(
        functools.partial(_mix_body, keep=1.0 - lambda_init),
        grid=(batch, nb),
        in_specs=[pl.BlockSpec(memory_space=pltpu.SMEM), row(D_MODEL), time_major, heads(GQA_HEADS),
                  heads(2 * DIFF_HEADS), row(GROUP_WIDTH), vec(GROUP_WIDTH), vec(GROUP_WIDTH), vec(HEAD_DIM),
                  vec(GROUP_WIDTH), pl.BlockSpec((D_MODEL, D_MODEL), lambda b, i: (0, 0))],
        out_specs=row(D_MODEL),
        out_shape=_sds((batch * seq, D_MODEL)),
        compiler_params=_cp("parallel", "parallel"),
        name="mix_out",
    )(lam, h, y_s5, o_gqa, o_diff, y_hy, gs5, ggqa, gsub, ghy, w_out)


def _router_body(h_ref, g_ref, w_ref, wl_ref, b_ref, xn_ref, ti_ref, tw_ref, cnt_ref):
    xn = _rms(h_ref[...], g_ref[...])
    xn_ref[...] = xn
    logits = _dot_split_rhs(xn, (w_ref[...], wl_ref[...])) + b_ref[...]
    lane = lax.broadcasted_iota(I32, logits.shape, 1)
    vals = logits
    tops, idxs = [], []
    hot = jnp.zeros(logits.shape, F32)
    for _ in range(TOP_K):
        m = jnp.max(vals, axis=-1, keepdims=True)
        idx = jnp.min(jnp.where(vals == m, lane, N_EXPERTS), axis=-1, keepdims=True)
        sel = lane == idx
        tops.append(m)
        idxs.append(idx)
        hot = hot + sel.astype(F32)
        vals = jnp.where(sel, -jnp.inf, vals)
    es = [jnp.exp(t - tops[0]) for t in tops]
    den = es[0] + es[1] + es[2] + es[3]
    ti_ref[...] = jnp.concatenate(idxs, axis=1)
    tw_ref[...] = jnp.concatenate([e / den for e in es], axis=1)

    @pl.when(pl.program_id(0) == 0)
    def _():
        cnt_ref[...] = jnp.zeros_like(cnt_ref)

    cnt_ref[...] += jnp.sum(hot, axis=0, keepdims=True)


def moe_router(h, g, w_router, b_router):
    t = h.shape[0]
    tm = min(ROW_TILE, t)
    const = lambda i: (0, 0)
    return pl.pallas_call(
        _router_body,
        grid=(t // tm,),
        in_specs=[pl.BlockSpec((tm, D_MODEL), lambda i: (i, 0)), pl.BlockSpec((1, D_MODEL), const),
                  pl.BlockSpec((D_MODEL, N_EXPERTS), const), pl.BlockSpec((D_MODEL, N_EXPERTS), const),
                  pl.BlockSpec((1, N_EXPERTS), const)],
        out_specs=[pl.BlockSpec((tm, D_MODEL), lambda i: (i, 0)), pl.BlockSpec((tm, TOP_K), lambda i: (i, 0)),
                   pl.BlockSpec((tm, TOP_K), lambda i: (i, 0)), pl.BlockSpec((1, N_EXPERTS), const)],
        out_shape=[_sds((t, D_MODEL)), _sds((t, TOP_K), I32), _sds((t, TOP_K)), _sds((1, N_EXPERTS))],
        compiler_params=_cp("arbitrary"),
        name="moe_router",
    )(h, g, *w_router, b_router)


def _rank_body(ti_ref, off_ref, pos_ref, stage_ref, cnt_ref, carry_ref):
    @pl.when(pl.program_id(0) == 0)
    def _():
        carry_ref[...] = jnp.zeros_like(carry_ref)

    ti = ti_ref[...]
    tm = ti.shape[0]
    lane = lax.broadcasted_iota(I32, (tm, N_EXPERTS), 1)
    hots = [lane == ti[:, k:k + 1] for k in range(TOP_K)]
    hot = sum(h.astype(F32) for h in hots)
    r = lax.broadcasted_iota(I32, (tm, tm), 0)
    c = lax.broadcasted_iota(I32, (tm, tm), 1)
    below = jnp.where(r > c, 1.0, 0.0).astype(BF16)
    local = jnp.dot(below, hot.astype(BF16), preferred_element_type=F32)
    before = local + carry_ref[...] + off_ref[...]
    pick = lambda table: jnp.concatenate(
        [jnp.sum(jnp.where(h, table, 0.0), axis=-1, keepdims=True) for h in hots], axis=1).astype(I32)
    pos_ref[...] = pick(before)
    count = jnp.sum(hot, axis=0, keepdims=True)
    er = lax.broadcasted_iota(I32, (N_EXPERTS, N_EXPERTS), 0)
    ec = lax.broadcasted_iota(I32, (N_EXPERTS, N_EXPERTS), 1)
    earlier = jnp.where(er < ec, 1.0, 0.0)
    chunks = jnp.ceil(count * (1.0 / GATHER_CHUNK))
    start = jnp.dot(chunks, earlier, precision=HI, preferred_element_type=F32) * GATHER_CHUNK
    stage_ref[...] = pick(local + start)
    cnt_ref[...] = jnp.broadcast_to(count, cnt_ref.shape)
    carry_ref[...] += count


def moe_rank(top_i, offsets):
    t = top_i.shape[0]
    tm = min(MOE_TOK_TILE, t)
    pairs = pl.BlockSpec((tm, TOP_K), lambda i: (i, 0))
    pos, stage, cnt = pl.pallas_call(
        _rank_body,
        grid=(t // tm,),
        in_specs=[pairs, pl.BlockSpec((1, N_EXPERTS), lambda i: (0, 0))],
        out_specs=[pairs, pairs, pl.BlockSpec((SUBLANES, N_EXPERTS), lambda i: (i, 0))],
        out_shape=[_sds((t, TOP_K), I32), _sds((t, TOP_K), I32), _sds((t // tm * SUBLANES, N_EXPERTS))],
        scratch_shapes=[pltpu.VMEM((1, N_EXPERTS), F32)],
        compiler_params=_cp("arbitrary"),
        name="moe_rank",
    )(top_i, offsets)
    return pos, stage, cnt[::SUBLANES]


def _row_copy(src, dst, sem):
    return pltpu.make_async_copy(src, dst, sem)


def _pad_zero_body(lt_ref, o_ref):
    del lt_ref
    o_ref[...] = jnp.zeros_like(o_ref)


def moe_pad_tiles(last_tile, n_slots):
    rows = MOE_TILE * ROW_TILES
    return pl.pallas_call(
        _pad_zero_body,
        grid_spec=pltpu.PrefetchScalarGridSpec(
            num_scalar_prefetch=1, grid=(N_EXPERTS,), in_specs=[],
            out_specs=pl.BlockSpec((rows, LANES), lambda e, lt: (lt[e], 0))),
        out_shape=_sds((n_slots * ROW_TILES, LANES)),
        compiler_params=_cp("arbitrary"),
        name="moe_pad_tiles",
    )(last_tile)


def _slot_rows(ref, slot):
    return ref.at[pl.ds(pl.multiple_of(slot * ROW_TILES, ROW_TILES), ROW_TILES)]


def _scatter_body(pos_ref, x_ref, padded_hbm, xs_hbm, buf, sem, *, tm):
    del padded_hbm
    x = x_ref[...]
    for s in range(ROW_TILES):
        buf[pl.ds(s, tm, stride=ROW_TILES), :] = x[:, s * LANES:(s + 1) * LANES]

    def start(r, c):
        for k in range(TOP_K):
            _row_copy(_slot_rows(buf, r), _slot_rows(xs_hbm, pos_ref[r * TOP_K + k]), sem).start()
        return c

    lax.fori_loop(0, tm, start, 0)
    for _ in range(TOP_K):
        _row_copy(buf, xs_hbm.at[pl.ds(0, tm * ROW_TILES)], sem).wait()


def moe_scatter(pos_flat, x, padded):
    t = x.shape[0]
    tm = min(MOE_TOK_TILE, t)
    return pl.pallas_call(
        functools.partial(_scatter_body, tm=tm),
        grid=(t // tm,),
        in_specs=[pl.BlockSpec((tm * TOP_K,), lambda i: (i,), memory_space=pltpu.SMEM),
                  pl.BlockSpec((tm, D_MODEL), lambda i: (i, 0)), pl.BlockSpec(memory_space=pl.ANY)],
        out_specs=pl.BlockSpec(memory_space=pl.ANY),
        out_shape=_sds(padded.shape),
        scratch_shapes=[pltpu.VMEM((tm * ROW_TILES, LANES), F32), pltpu.SemaphoreType.DMA(())],
        input_output_aliases={2: 0},
        compiler_params=_cp("arbitrary"),
        name="moe_scatter",
    )(pos_flat, x, padded)


def _experts_body(te_ref, nv_ref, x_ref, wgu_ref, bgu_ref, wd_ref, bd_ref, o_ref):
    del te_ref
    j = pl.program_id(0)
    dff = wd_ref.shape[1]

    @pl.when(j < nv_ref[0])
    def _():
        tm = x_ref.shape[0] // ROW_TILES
        x = jnp.concatenate([x_ref[pl.ds(s, tm, stride=ROW_TILES), :] for s in range(ROW_TILES)],
                            axis=1).astype(BF16)
        hgu = jnp.dot(x, wgu_ref[0], preferred_element_type=F32) + bgu_ref[0]
        glu = jnp.minimum(hgu[:, :dff], SWIGLU_LIMIT)
        lin = jnp.clip(hgu[:, dff:], -SWIGLU_LIMIT, SWIGLU_LIMIT)
        act = glu * jax.nn.sigmoid(SWIGLU_ALPHA * glu) * (lin + 1.0)
        y = jnp.dot(act.astype(BF16), wd_ref[0], preferred_element_type=F32) + bd_ref[0]
        for s in range(ROW_TILES):
            o_ref[pl.ds(s, tm, stride=ROW_TILES), :] = y[:, s * LANES:(s + 1) * LANES]

    @pl.when(j >= nv_ref[0])
    def _():
        o_ref[...] = jnp.zeros_like(o_ref)


def moe_experts(tile_expert, n_valid, xs, w_gu, b_gu, w_down, b_down):
    n_slots = xs.shape[0] // ROW_TILES
    tm = MOE_TILE
    dff = w_down.shape[1]
    blk = pl.BlockSpec((tm * ROW_TILES, LANES), lambda j, te, nv: (j, 0))
    return pl.pallas_call(
        _experts_body,
        grid_spec=pltpu.PrefetchScalarGridSpec(
            num_scalar_prefetch=2,
            grid=(n_slots // tm,),
            in_specs=[pl.BlockSpec((tm * ROW_TILES, LANES), lambda j, te, nv: (jnp.minimum(j, nv[0] - 1), 0)),
                      pl.BlockSpec((1, D_MODEL, 2 * dff), lambda j, te, nv: (te[j], 0, 0)),
                      pl.BlockSpec((1, 1, 2 * dff), lambda j, te, nv: (te[j], 0, 0)),
                      pl.BlockSpec((1, dff, D_MODEL), lambda j, te, nv: (te[j], 0, 0)),
                      pl.BlockSpec((1, 1, D_MODEL), lambda j, te, nv: (te[j], 0, 0))],
            out_specs=blk),
        out_shape=_sds((n_slots * ROW_TILES, LANES)),
        compiler_params=_cp("arbitrary"),
        name="moe_experts",
    )(tile_expert, n_valid, xs, w_gu, b_gu, w_down, b_down)


def _combine_body(slot_ref, w_ref, tab_ref, ys_hbm, o_ref, stage, mixed, sem, *, tm):
    chunk_rows = GATHER_CHUNK * ROW_TILES

    def chunk_copy(src_slot, dst_slot):
        return _row_copy(ys_hbm.at[pl.ds(pl.multiple_of(src_slot * ROW_TILES, ROW_TILES), chunk_rows)],
                         stage.at[pl.ds(pl.multiple_of(dst_slot * ROW_TILES, ROW_TILES), chunk_rows)], sem)

    def per_expert(e, c):
        base, chunks, start = tab_ref[e], tab_ref[N_EXPERTS + e], tab_ref[2 * N_EXPERTS + e]

        def issue(j, c2):
            chunk_copy(base + j * GATHER_CHUNK, start + j * GATHER_CHUNK).start()
            return c2

        return lax.fori_loop(0, chunks, issue, c)

    lax.fori_loop(0, N_EXPERTS, per_expert, 0)

    def wait(j, c):
        chunk_copy(0, 0).wait()
        return c

    lax.fori_loop(0, tab_ref[3 * N_EXPERTS], wait, 0)

    def mix(q, c):
        for u in range(MIX_UNROLL):
            r = q * MIX_UNROLL + u
            acc = None
            for k in range(TOP_K):
                p = r * TOP_K + k
                term = w_ref[p] * stage[pl.ds(pl.multiple_of(slot_ref[p] * ROW_TILES, ROW_TILES), ROW_TILES), :]
                acc = term if acc is None else acc + term
            mixed[pl.ds(pl.multiple_of(r * ROW_TILES, ROW_TILES), ROW_TILES), :] = acc
        return c

    lax.fori_loop(0, tm // MIX_UNROLL, mix, 0)
    o_ref[...] = jnp.concatenate([mixed[pl.ds(s, tm, stride=ROW_TILES), :] for s in range(ROW_TILES)], axis=1)


def moe_combine(stage_flat, w_flat, table, ys, t):
    tm = min(MOE_TOK_TILE, t)
    smem = pl.BlockSpec((tm * TOP_K,), lambda i: (i,), memory_space=pltpu.SMEM)
    stage_slots = tm * TOP_K + N_EXPERTS * GATHER_CHUNK
    return pl.pallas_call(
        functools.partial(_combine_body, tm=tm),
        grid=(t // tm,),
        in_specs=[smem, smem, pl.BlockSpec((LANES,), lambda i: (i,), memory_space=pltpu.SMEM),
                  pl.BlockSpec(memory_space=pl.ANY)],
        out_specs=pl.BlockSpec((tm, D_MODEL), lambda i: (i, 0)),
        out_shape=_sds((t, D_MODEL)),
        scratch_shapes=[pltpu.VMEM((stage_slots * ROW_TILES, LANES), F32),
                        pltpu.VMEM((tm * ROW_TILES, LANES), F32), pltpu.SemaphoreType.DMA(())],
        compiler_params=_cp("arbitrary"),
        name="moe_combine",
    )(stage_flat, w_flat, table, ys)


def gather_table(cnt_tile, offsets):
    base = offsets + jnp.cumsum(cnt_tile, axis=0) - cnt_tile
    chunks = jnp.ceil(cnt_tile / GATHER_CHUNK)
    start = (jnp.cumsum(chunks, axis=1) - chunks) * GATHER_CHUNK
    total = jnp.sum(chunks, axis=1, keepdims=True)
    pad = jnp.zeros((cnt_tile.shape[0], LANES - 3 * N_EXPERTS - 1), F32)
    return jnp.concatenate([base, chunks, start, total, pad], axis=1).astype(I32).reshape(-1)


def moe(h, g, w_router, b_router, w_gu, b_gu, w_down, b_down):
    t = h.shape[0]
    xn, top_i, top_w, counts = moe_router(h, g, w_router, b_router)
    n_tiles = (t * TOP_K) // MOE_TILE + N_EXPERTS + 1
    padded = jnp.ceil(counts[0] / MOE_TILE) * MOE_TILE
    ends = jnp.cumsum(padded)
    offsets = (ends - padded)[None]
    tile_start = jnp.arange(n_tiles, dtype=F32) * MOE_TILE
    tile_expert = jnp.minimum(jnp.sum(ends[None, :] <= tile_start[:, None], axis=1), N_EXPERTS - 1).astype(I32)
    n_valid = (ends[-1:] / MOE_TILE).astype(I32)
    own_last = jnp.where(padded > 0, ends / MOE_TILE - 1, -1.0)
    filled = lax.cummax(own_last, axis=0)
    first = jnp.min(jnp.where(padded > 0, own_last, jnp.inf))
    last_tile = jnp.where(filled < 0, first, filled).astype(I32)
    pos, stage, cnt_tile = moe_rank(top_i, offsets)
    xs = moe_scatter(pos.reshape(t * TOP_K), xn, moe_pad_tiles(last_tile, n_tiles * MOE_TILE))
    ys = moe_experts(tile_expert, n_valid, xs, w_gu, b_gu, w_down, b_down)
    flat = lambda a: a.reshape(t * TOP_K)
    return moe_combine(flat(stage), flat(top_w), gather_table(cnt_tile, offsets), ys, t)


def _ple_body(h_ref, m_ref, p_ref, wp_ref, wg_ref, o_ref):
    h = h_ref[...] + m_ref[...]
    e = jnp.dot(p_ref[...].astype(BF16), wp_ref[...], preferred_element_type=F32)
    gate = jnp.dot(_rms(h).astype(BF16), wg_ref[...], preferred_element_type=F32)
    o_ref[...] = h + e * jax.nn.sigmoid(gate)


def ple(h, moe_out, p, w_ple, w_gate):
    t = h.shape[0]
    tm = min(ROW_TILE, t)
    row = pl.BlockSpec((tm, D_MODEL), lambda i: (i, 0))
    return pl.pallas_call(
        _ple_body,
        grid=(t // tm,),
        in_specs=[row, row, pl.BlockSpec((tm, PLE_DIM), lambda i: (i, 0)),
                  pl.BlockSpec((PLE_DIM, D_MODEL), lambda i: (0, 0)),
                  pl.BlockSpec((D_MODEL, D_MODEL), lambda i: (0, 0))],
        out_specs=row,
        out_shape=_sds((t, D_MODEL)),
        compiler_params=_cp("parallel"),
        name="ple",
    )(h, moe_out, p, w_ple, w_gate)


def _deinterleave(n):
    return np.concatenate([np.arange(0, n, 2), np.arange(1, n, 2)])


def _halves(w):
    pairs = w.reshape(*w.shape[:-1], w.shape[-1] // 2, 2)
    return jnp.concatenate([pairs[..., 0], pairs[..., 1]], axis=-1)


def prepare_layer(i, prm):
    pair = _deinterleave(HEAD_DIM)
    cols = np.arange(IN_COLS)
    for h in range(GQA_HEADS + GQA_KV_HEADS):
        lo = OFF_GQA + h * HEAD_DIM
        cols[lo:lo + HEAD_DIM] = lo + pair
    lam = (jnp.exp(jnp.sum(prm["diff_lam_q1"][i] * prm["diff_lam_k1"][i]))
           - jnp.exp(jnp.sum(prm["diff_lam_q2"][i] * prm["diff_lam_k2"][i])) + _lambda_init(i))
    return dict(
        norm1_g=prm["norm1_g"][i][None],
        w_in=prm["w_in"][i][:, cols].astype(BF16),
        s5=s5_tables(prm["s5_lam_re"][i], prm["s5_lam_im"][i], prm["s5_log_dt"][i], prm["s5_b_re"][i],
                     prm["s5_b_im"][i], prm["s5_c_re"][i], prm["s5_c_im"][i]),
        s5_d=prm["s5_d"][i][None],
        s5_w_glu=prm["s5_w_glu"][i].astype(BF16),
        s5_b_glu=prm["s5_b_glu"][i][None],
        gqa_q_g=prm["gqa_q_g"][i][pair][None],
        gqa_k_g=prm["gqa_k_g"][i][pair][None],
        diff_q_g=prm["diff_q_g"][i][None],
        diff_k_g=prm["diff_k_g"][i][None],
        diff_lam=lam.reshape(1).astype(F32),
        diff_subln_g=prm["diff_subln_g"][i][None],
        out_g_s5=prm["out_g_s5"][i][None],
        out_g_gqa=prm["out_g_gqa"][i][None],
        out_g_hy=prm["out_g_hy"][i][None],
        w_out=prm["w_out"][i].astype(BF16),
        norm2_g=prm["norm2_g"][i][None],
        w_router=_split_bf16(prm["w_router"][i]),
        b_router=prm["b_router"][i][None],
        w_gu=_halves(prm["w_gu"][i]).astype(BF16),
        b_gu=_halves(prm["b_gu"][i])[:, None, :],
        w_down=prm["w_down"][i].astype(BF16),
        b_down=prm["b_down"][i][:, None, :],
        w_ple=prm["w_ple"][i].astype(BF16),
        w_ple_gate=prm["w_ple_gate"][i].astype(BF16),
    )


def _lambda_init(i):
    return 0.8 - 0.6 * math.exp(-0.3 * i)


def run_trunk(x, p, prm, layers):
    batch, seq, _ = x.shape
    t = batch * seq
    cos_t, sin_t = rope_tables(seq)
    tabs = dft_tables(seq)
    bias = diff_bias_tiles(prm["rel_bias"], min(ATTN_UNIT, seq))
    h = x.reshape(t, D_MODEL)
    for i, lw in enumerate(layers):
        z_s5, z_gqa, z_diff, z_hy = in_proj(h, lw["norm1_g"], lw["w_in"], batch, seq)
        y_s5 = s5_mixer(z_s5, lw["s5"], lw["s5_d"], lw["s5_w_glu"], lw["s5_b_glu"], batch, seq)
        rep = GQA_HEADS // GQA_KV_HEADS
        q, k, v = attn_prep(z_gqa, lw["gqa_q_g"], lw["gqa_k_g"], cos_t, sin_t, batch, seq,
                            GQA_HEADS, GQA_KV_HEADS, GQA_KV_HEADS, True, lambda hq: hq // rep)
        o_gqa = flash_attention(q, k, v, lambda hq: hq // rep)
        q, k, v = attn_prep(z_diff, lw["diff_q_g"], lw["diff_k_g"], cos_t, sin_t, batch, seq,
                            2 * DIFF_HEADS, 2 * DIFF_HEADS, DIFF_HEADS, False, lambda hq: hq)
        o_diff = flash_attention(q, k, v, lambda hq: hq // 2, bias)
        hf = hyena_spectrum(seq, tabs, prm["hy_w1"][i], prm["hy_b1"][i], prm["hy_freq1"][i], prm["hy_w2"][i],
                            prm["hy_b2"][i], prm["hy_freq2"][i], prm["hy_w3"][i], prm["hy_decay"][i])
        y_hy = hyena_mixer(z_hy, prm["hy_conv_w"][i], prm["hy_conv_b"][i][None], hf, prm["hy_bias"][i],
                           tabs, batch, seq)
        h = mix_out(lw["diff_lam"], h, y_s5, o_gqa, o_diff, y_hy, lw["out_g_s5"], lw["out_g_gqa"],
                    lw["diff_subln_g"], lw["out_g_hy"], lw["w_out"], batch, seq, _lambda_init(i))
        m = moe(h, lw["norm2_g"], lw["w_router"], lw["b_router"], lw["w_gu"], lw["b_gu"], lw["w_down"],
                lw["b_down"])
        h = ple(h, m, p[i].reshape(t, PLE_DIM), lw["w_ple"], lw["w_ple_gate"])
    return h.reshape(batch, seq, D_MODEL)


def kernel(x_prompt, x_sample, p_prompt, p_sample, rel_bias, norm1_g, w_in, s5_lam_re, s5_lam_im, s5_log_dt,
           s5_b_re, s5_b_im, s5_c_re, s5_c_im, s5_d, s5_w_glu, s5_b_glu, gqa_q_g, gqa_k_g, diff_q_g, diff_k_g,
           diff_lam_q1, diff_lam_k1, diff_lam_q2, diff_lam_k2, diff_subln_g, hy_conv_w, hy_conv_b, hy_w1, hy_b1,
           hy_freq1, hy_w2, hy_b2, hy_freq2, hy_w3, hy_decay, hy_bias, out_g_s5, out_g_gqa, out_g_hy, w_out,
           norm2_g, w_router, b_router, w_gu, b_gu, w_down, b_down, w_ple, w_ple_gate):
    prm = dict(rel_bias=rel_bias, norm1_g=norm1_g, w_in=w_in, s5_lam_re=s5_lam_re, s5_lam_im=s5_lam_im,
               s5_log_dt=s5_log_dt, s5_b_re=s5_b_re, s5_b_im=s5_b_im, s5_c_re=s5_c_re, s5_c_im=s5_c_im,
               s5_d=s5_d, s5_w_glu=s5_w_glu, s5_b_glu=s5_b_glu, gqa_q_g=gqa_q_g, gqa_k_g=gqa_k_g,
               diff_q_g=diff_q_g, diff_k_g=diff_k_g, diff_lam_q1=diff_lam_q1, diff_lam_k1=diff_lam_k1,
               diff_lam_q2=diff_lam_q2, diff_lam_k2=diff_lam_k2, diff_subln_g=diff_subln_g,
               hy_conv_w=hy_conv_w, hy_conv_b=hy_conv_b, hy_w1=hy_w1, hy_b1=hy_b1, hy_freq1=hy_freq1,
               hy_w2=hy_w2, hy_b2=hy_b2, hy_freq2=hy_freq2, hy_w3=hy_w3, hy_decay=hy_decay, hy_bias=hy_bias,
               out_g_s5=out_g_s5, out_g_gqa=out_g_gqa, out_g_hy=out_g_hy, w_out=w_out, norm2_g=norm2_g,
               w_router=w_router, b_router=b_router, w_gu=w_gu, b_gu=b_gu, w_down=w_down, b_down=b_down,
               w_ple=w_ple, w_ple_gate=w_ple_gate)
    layers = [prepare_layer(i, prm) for i in range(DEPTH)]
    return (run_trunk(x_prompt, p_prompt, prm, layers), run_trunk(x_sample, p_sample, prm, layers))
```

```python
import functools
import math

import numpy as np
import jax
import jax.numpy as jnp
from jax import lax
from jax.experimental import pallas as pl
from jax.experimental.pallas import tpu as pltpu

F32 = jnp.float32
BF16 = jnp.bfloat16
I32 = jnp.int32
HI = lax.Precision.HIGHEST

D_MODEL = 1024
DEPTH = 2
GROUP_WIDTH = 256
HEAD_DIM = 64
S5_GROUP_DIM = 16
S5_GROUPS = 16
S5_STATE = 64
S5_NSTATE = S5_GROUPS * S5_STATE
GQA_HEADS = 4
GQA_KV_HEADS = 2
DIFF_HEADS = 4
HY_WIDTH = 256
HY_ORDER = 2
HY_BANDS = 16
HY_EMB = 2 * HY_BANDS + 1
HY_FFN = 64
N_EXPERTS = 32
TOP_K = 4
SWIGLU_LIMIT = 7.0
SWIGLU_ALPHA = 1.702
NUM_BUCKETS = 32
MAX_DISTANCE = 128
GRID_W = 64
ROPE_THETA = 10000.0
ROPE_AXIS_DIM = HEAD_DIM // 2
PLE_DIM = 256
NORM_EPS = 1e-6
ATTN_SCALE = HEAD_DIM ** -0.5
LOG2E = math.log2(math.e)

S5_COLS = GROUP_WIDTH
GQA_QW = GQA_HEADS * HEAD_DIM
GQA_KW = GQA_KV_HEADS * HEAD_DIM
GQA_COLS = GQA_QW + 2 * GQA_KW
DIFF_QW = DIFF_HEADS * 2 * HEAD_DIM
DIFF_COLS = 2 * DIFF_QW + DIFF_HEADS * HEAD_DIM
HY_COLS = (HY_ORDER + 1) * HY_WIDTH
OFF_GQA = S5_COLS
OFF_DIFF = OFF_GQA + GQA_COLS
OFF_HY = OFF_DIFF + DIFF_COLS
IN_COLS = OFF_HY + HY_COLS
IN_SPLITS = ((0, OFF_GQA), (OFF_GQA, OFF_DIFF), (OFF_DIFF, OFF_HY), (OFF_HY, IN_COLS))

LANES = 128
SUBLANES = 8
ROW_TILES = D_MODEL // LANES
FFT_N2 = 128
FFT_ROWS = 64
VMEM_LIMIT = 56 * 1024 * 1024

ROW_TILE = 1024
ATTN_Q_TILE = 2048
ATTN_K_TILE = 2048
ATTN_UNIT = 512
ATTN_SUB = 512
ATTN_SUB_PLAIN = 1024
S5_TIME_BLOCK = 128
MOE_TILE = 512
MOE_TOK_TILE = 1024
MIX_UNROLL = 4
GATHER_CHUNK = 16


def _cp(*sem):
    return pltpu.CompilerParams(dimension_semantics=sem, vmem_limit_bytes=VMEM_LIMIT)


def _sds(shape, dtype=F32):
    return jax.ShapeDtypeStruct(shape, dtype)


def _rms(x, g=None):
    y = x * lax.rsqrt(jnp.mean(x * x, axis=-1, keepdims=True) + NORM_EPS)
    return y if g is None else y * g


def _inproj_body(x_ref, g_ref, w_ref, *o_refs):
    u = _rms(x_ref[...], g_ref[...]).astype(BF16)
    for o_ref, (lo, hi) in zip(o_refs, IN_SPLITS):
        o_ref[...] = jnp.dot(u, w_ref[:, lo:hi], preferred_element_type=F32)


def in_proj(h, g, w, batch, seq):
    t = h.shape[0]
    tm = min(ROW_TILE, seq)
    nb = seq // tm
    row_major = lambda i: (i, 0)
    time_major = lambda i: (i % nb, i // nb)
    widths = [hi - lo for lo, hi in IN_SPLITS]
    return pl.pallas_call(
        _inproj_body,
        grid=(t // tm,),
        in_specs=[pl.BlockSpec((tm, D_MODEL), row_major),
                  pl.BlockSpec((1, D_MODEL), lambda i: (0, 0)),
                  pl.BlockSpec((D_MODEL, IN_COLS), lambda i: (0, 0))],
        out_specs=[pl.BlockSpec((tm, widths[0]), time_major)] + [pl.BlockSpec((tm, w_), row_major) for w_ in widths[1:]],
        out_shape=[_sds((seq, batch * widths[0]))] + [_sds((t, w_)) for w_ in widths[1:]],
        compiler_params=_cp("parallel"),
        name="in_proj",
    )(h, g, w)


def s5_tables(lam_re, lam_im, log_dt, b_re, b_im, c_re, c_im):
    lr = jnp.minimum(lam_re, -1e-4)
    li = lam_im
    dt = jnp.exp(log_dt)[..., None]
    mag = jnp.exp(lr * dt)
    ar = mag * jnp.cos(li * dt)
    ai = mag * jnp.sin(li * dt)
    den = lr * lr + li * li
    cr = ((ar - 1.0) * lr + ai * li) / den
    ci = (ai * lr - (ar - 1.0) * li) / den
    bbr = cr[..., None] * b_re - ci[..., None] * b_im
    bbi = cr[..., None] * b_im + ci[..., None] * b_re
    eye = jnp.eye(S5_GROUPS, dtype=F32)
    bb = jnp.stack([bbr, bbi], axis=1)
    wb = jnp.einsum("dpgnh,gk->dghpkn", bb, eye).reshape(2, S5_COLS, 2 * S5_NSTATE)
    cc = jnp.stack([c_re, -c_im], axis=0)
    wc = jnp.einsum("pdghn,gk->dpgnkh", cc, eye).reshape(2, 2 * S5_NSTATE, S5_COLS)
    a_re = jnp.repeat(ar.reshape(2, S5_NSTATE), 4, axis=0)
    a_im = jnp.repeat(ai.reshape(2, S5_NSTATE), 4, axis=0)
    return wb.astype(BF16), wc.astype(BF16), a_re, a_im


def _s5_body(uf_ref, ub_ref, wb_ref, wc_ref, ar_ref, ai_ref, yf_ref, yb_ref, rows_ref, bf_ref, bb_ref, xr_ref, xi_ref,
             *, tb):
    @pl.when(pl.program_id(0) == 0)
    def _():
        xr_ref[...] = jnp.zeros_like(xr_ref)
        xi_ref[...] = jnp.zeros_like(xi_ref)

    halves = range(S5_COLS // LANES)
    wide = lambda r0: jnp.concatenate([rows_ref[h, r0:r0 + tb * 4, :] for h in halves], axis=1)
    for b in range(4):
        for h in halves:
            cols = slice(b * S5_COLS + h * LANES, b * S5_COLS + (h + 1) * LANES)
            rows_ref[h, pl.ds(b, tb, stride=4), :] = uf_ref[:, cols]
            rows_ref[h, pl.ds(tb * 4 + b, tb, stride=4), :] = ub_ref[:, cols]
    bf_ref[...] = jnp.dot(wide(0).astype(BF16), wb_ref[0], preferred_element_type=F32)
    bb_ref[...] = jnp.dot(wide(tb * 4).astype(BF16), wb_ref[1], preferred_element_type=F32)
    ar = ar_ref[...]
    ai = ai_ref[...]
    low = lax.broadcasted_iota(I32, (SUBLANES, 1), 0) < 4
    half = SUBLANES // 2
    re = slice(0, S5_NSTATE)
    im = slice(S5_NSTATE, 2 * S5_NSTATE)

    def pair(m, carry):
        xr, xi = carry
        rf = pl.ds(pl.multiple_of(m * SUBLANES, SUBLANES), SUBLANES)
        rb = pl.ds(pl.multiple_of((tb // 2 - 1 - m) * SUBLANES, SUBLANES), SUBLANES)
        f_r, f_i, b_r, b_i = bf_ref[rf, re], bf_ref[rf, im], bb_ref[rb, re], bb_ref[rb, im]
        in_r = jnp.where(low, f_r, b_r)
        in_i = jnp.where(low, f_i, b_i)
        x1r = ar * xr - ai * xi + in_r
        x1i = ar * xi + ai * xr + in_i
        in_r = pltpu.roll(jnp.where(low, b_r, f_r), half, 0)
        in_i = pltpu.roll(jnp.where(low, b_i, f_i), half, 0)
        x2r = ar * x1r - ai * x1i + in_r
        x2i = ar * x1i + ai * x1r + in_i
        s2r = pltpu.roll(x2r, half, 0)
        s2i = pltpu.roll(x2i, half, 0)
        bf_ref[rf, re] = jnp.where(low, x1r, s2r)
        bf_ref[rf, im] = jnp.where(low, x1i, s2i)
        bb_ref[rb, re] = jnp.where(low, s2r, x1r)
        bb_ref[rb, im] = jnp.where(low, s2i, x1i)
        return x2r, x2i

    xr, xi = lax.fori_loop(0, tb // 2, pair, (xr_ref[...], xi_ref[...]))
    xr_ref[...] = xr
    xi_ref[...] = xi
    yf = jnp.dot(bf_ref[...].astype(BF16), wc_ref[0], preferred_element_type=F32)
    yb = jnp.dot(bb_ref[...].astype(BF16), wc_ref[1], preferred_element_type=F32)
    for h in halves:
        rows_ref[h, 0:tb * 4, :] = yf[:, h * LANES:(h + 1) * LANES]
        rows_ref[h, tb * 4:tb * 8, :] = yb[:, h * LANES:(h + 1) * LANES]
    for b in range(4):
        for h in halves:
            cols = slice(b * S5_COLS + h * LANES, b * S5_COLS + (h + 1) * LANES)
            yf_ref[:, cols] = rows_ref[h, pl.ds(b, tb, stride=4), :]
            yb_ref[:, cols] = rows_ref[h, pl.ds(tb * 4 + b, tb, stride=4), :]


def s5_scan(u, wb, wc, a_re, a_im, seq):
    tb = min(S5_TIME_BLOCK, seq)
    nb = seq // tb
    rows = tb * 4
    fwd = pl.BlockSpec((tb, 4 * S5_COLS), lambda i: (i, 0))
    bwd = pl.BlockSpec((tb, 4 * S5_COLS), lambda i: (nb - 1 - i, 0))
    return pl.pallas_call(
        functools.partial(_s5_body, tb=tb),
        grid=(nb,),
        in_specs=[fwd, bwd,
                  pl.BlockSpec((2, S5_COLS, 2 * S5_NSTATE), lambda i: (0, 0, 0)),
                  pl.BlockSpec((2, 2 * S5_NSTATE, S5_COLS), lambda i: (0, 0, 0)),
                  pl.BlockSpec((SUBLANES, S5_NSTATE), lambda i: (0, 0)),
                  pl.BlockSpec((SUBLANES, S5_NSTATE), lambda i: (0, 0))],
        out_specs=[fwd, bwd],
        out_shape=[_sds((seq, 4 * S5_COLS))] * 2,
        scratch_shapes=[pltpu.VMEM((S5_COLS // LANES, 2 * rows, LANES), F32),
                        pltpu.VMEM((rows, 2 * S5_NSTATE), F32), pltpu.VMEM((rows, 2 * S5_NSTATE), F32),
                        pltpu.VMEM((SUBLANES, S5_NSTATE), F32), pltpu.VMEM((SUBLANES, S5_NSTATE), F32)],
        compiler_params=_cp("arbitrary"),
        name="s5_scan",
    )(u, u, wb, wc, a_re, a_im)


def _s5_out_body(yf_ref, yb_ref, u_ref, d_ref, w_ref, b_ref, o_ref):
    for b in range(o_ref.shape[1] // S5_COLS):
        cols = slice(b * S5_COLS, (b + 1) * S5_COLS)
        y = jax.nn.gelu(yf_ref[:, cols] + yb_ref[:, cols] + u_ref[:, cols] * d_ref[...])
        gate = jnp.dot(y.astype(BF16), w_ref[...], preferred_element_type=F32) + b_ref[...]
        o_ref[:, cols] = y * jax.nn.sigmoid(gate)


def s5_out(yf, yb, u, d, w_glu, b_glu):
    t, width = u.shape
    tm = min(ROW_TILE // 4, t)
    row = pl.BlockSpec((tm, width), lambda i: (i, 0))
    vec = pl.BlockSpec((1, S5_COLS), lambda i: (0, 0))
    return pl.pallas_call(
        _s5_out_body,
        grid=(t // tm,),
        in_specs=[row, row, row, vec, pl.BlockSpec((S5_COLS, S5_COLS), lambda i: (0, 0)), vec],
        out_specs=row,
        out_shape=_sds((t, width)),
        compiler_params=_cp("parallel"),
        name="s5_out",
    )(yf, yb, u, d, w_glu, b_glu)


def s5_mixer(z_s5, tabs, d, w_glu, b_glu, batch, seq):
    assert batch == 4 and seq % 2 == 0, "the scan packs 4 sequences x 2 directions onto the 8 sublanes"
    yf, yb = s5_scan(z_s5, *tabs, seq)
    return s5_out(yf, yb, z_s5, d, w_glu, b_glu)


def _prep_body(z_ref, qg_ref, kg_ref, c_ref, s_ref, q_ref, k_ref, v_ref, *, nq, nk, nv, rope, k_src):
    tm = z_ref.shape[0]
    lane = lax.broadcasted_iota(I32, (tm, LANES), 1)
    low = lane < HEAD_DIM

    def norm_pair(x, g):
        sq = x * x
        ss_lo = jnp.sum(jnp.where(low, sq, 0.0), axis=-1, keepdims=True)
        ss_hi = jnp.sum(jnp.where(low, 0.0, sq), axis=-1, keepdims=True)
        y = x * lax.rsqrt(jnp.where(low, ss_lo, ss_hi) * (1.0 / HEAD_DIM) + NORM_EPS) * g
        if rope:
            half = HEAD_DIM // 2
            partner = jnp.where(lane % HEAD_DIM < half, pltpu.roll(y, LANES - half, 1), pltpu.roll(y, half, 1))
            y = y * c_ref[...] + partner * s_ref[...]
        return y

    qg = qg_ref[...]
    kg = kg_ref[...]
    for p in range(nq // 2):
        q_ref[0, p] = (norm_pair(z_ref[:, p * LANES:(p + 1) * LANES], qg) * (ATTN_SCALE * LOG2E)).astype(BF16)
    k_lo = nq * HEAD_DIM
    pairs = [norm_pair(z_ref[:, k_lo + p * LANES:k_lo + (p + 1) * LANES], kg) for p in range(nk // 2)]
    for h in range(nq):
        src = k_src(h)
        x = pairs[src // 2]
        if src % 2 != h % 2:
            x = pltpu.roll(x, HEAD_DIM, 1)
        k_ref[0, h] = jnp.where(low if h % 2 == 0 else jnp.logical_not(low), x, 0.0).astype(BF16)
    ones_col = jnp.where(lax.broadcasted_iota(I32, (tm, LANES - HEAD_DIM), 1) == 0, 1.0, 0.0)
    for h in range(nv):
        lo = (nq + nk + h) * HEAD_DIM
        v_ref[0, h] = jnp.concatenate([z_ref[:, lo:lo + HEAD_DIM], ones_col], axis=1).astype(BF16)


def attn_prep(z, qg, kg, cos_t, sin_t, batch, seq, nq, nk, nv, rope, k_src):
    width = (nq + nk + nv) * HEAD_DIM
    tm = min(ROW_TILE, seq)
    nb = seq // tm
    vec = pl.BlockSpec((1, LANES), lambda b, i: (0, 0))
    tab = pl.BlockSpec((tm, LANES), lambda b, i: (i, 0))
    heads = lambda n: pl.BlockSpec((1, n, tm, LANES), lambda b, i: (b, 0, i, 0))
    return pl.pallas_call(
        functools.partial(_prep_body, nq=nq, nk=nk, nv=nv, rope=rope, k_src=k_src),
        grid=(batch, nb),
        in_specs=[pl.BlockSpec((tm, width), lambda b, i: (b * nb + i, 0)), vec, vec, tab, tab],
        out_specs=[heads(nq // 2), heads(nq), heads(nv)],
        out_shape=[_sds((batch, nq // 2, seq, LANES), BF16), _sds((batch, nq, seq, LANES), BF16),
                   _sds((batch, nv, seq, LANES), BF16)],
        compiler_params=_cp("parallel", "parallel"),
        name="attn_prep",
    )(z, jnp.tile(qg, (1, 2)), jnp.tile(kg, (1, 2)), cos_t, sin_t)


def _flash_body(q_ref, k_ref, v_ref, *rest, unit, sub, sub_far, has_bias):
    if has_bias:
        b_ref, o_ref, m_ref, acc_ref = rest
    else:
        o_ref, m_ref, acc_ref = rest
    i = pl.program_id(2)
    j = pl.program_id(3)
    tq = q_ref.shape[2]
    tk = k_ref.shape[2]
    q_units = tq // unit
    k_units = tk // unit

    @pl.when(j == 0)
    def _():
        m_ref[...] = jnp.full_like(m_ref, -jnp.inf)
        acc_ref[...] = jnp.zeros_like(acc_ref)

    def attend(rows, offset=None, far_sel=None):
        k = k_ref[0, 0]
        v = v_ref[0, 0]
        for g in range(tq // rows):
            r0 = g * rows
            s = lax.dot_general(q_ref[0, 0, r0:r0 + rows, :], k, (((1,), (1,)), ((), ())),
                                preferred_element_type=F32)
            m_prev = m_ref[r0:r0 + rows, :]
            kinds = [None] * k_units
            if offset is not None:
                tile_of = [offset * k_units + c - r0 // unit for c in range(k_units)]
                kinds = [None if abs(d) <= 1 else (0 if d < 0 else 4) for d in tile_of]
                s = jnp.concatenate(
                    [s[:, c * unit:(c + 1) * unit] + b_ref[0, d + 2, r0 % unit:r0 % unit + rows, :]
                     if abs(d) <= 1 else s[:, c * unit:(c + 1) * unit] for c, d in enumerate(tile_of)], axis=1)
            elif far_sel is not None:
                kinds = [far_sel] * k_units
            consts = {kind: b_ref[0, kind, 0:1, 0:LANES] for kind in set(kinds) if kind is not None}
            per_unit = unit // LANES
            blocks = [s[:, b * LANES:(b + 1) * LANES] for b in range(tk // LANES)]
            m_new = m_prev
            for kind in sorted(set(kinds), key=str):
                members = [blk for b, blk in enumerate(blocks) if kinds[b // per_unit] == kind]
                bmax = members[0]
                for blk in members[1:]:
                    bmax = jnp.maximum(bmax, blk)
                kind_max = jnp.max(bmax, axis=-1, keepdims=True)
                m_new = jnp.maximum(m_new, kind_max if kind is None else kind_max + consts[kind])
            shifts = [m_new if kind is None else m_new - consts[kind] for kind in kinds]
            alpha = jnp.exp2(m_prev - m_new)
            p = jnp.concatenate([jnp.exp2(blk - shifts[b // per_unit]) for b, blk in enumerate(blocks)],
                                axis=1).astype(BF16)
            acc_ref[r0:r0 + rows, :] = alpha * acc_ref[r0:r0 + rows, :] + jnp.dot(p, v, preferred_element_type=F32)
            m_ref[r0:r0 + rows, :] = m_new

    if has_bias:
        assert q_units == k_units and sub == unit, "the near-diagonal variants assume square steps of whole units"
        pl.when(j - i >= 2)(lambda: attend(sub_far, far_sel=4))
        pl.when(j - i <= -2)(lambda: attend(sub_far, far_sel=0))
        for offset in (-1, 0, 1):
            pl.when(j - i == offset)(functools.partial(attend, sub, offset=offset))
    else:
        attend(sub)

    @pl.when(j == pl.num_programs(3) - 1)
    def _():
        acc = acc_ref[...]
        o_ref[0, 0] = acc[:, :HEAD_DIM] / acc[:, HEAD_DIM:HEAD_DIM + 1]


def flash_attention(q, k, v, v_of_q, bias=None):
    batch, nq, seq, _ = k.shape
    tq = min(ATTN_Q_TILE, seq)
    tk = min(ATTN_K_TILE, seq)
    unit = min(ATTN_UNIT, seq)
    sub = min(ATTN_SUB if bias is not None else ATTN_SUB_PLAIN, tq)
    sub_far = min(ATTN_SUB_PLAIN, tq)
    in_specs = [pl.BlockSpec((1, 1, tq, LANES), lambda b, h, i, j: (b, h // 2, i, 0)),
                pl.BlockSpec((1, 1, tk, LANES), lambda b, h, i, j: (b, h, j, 0)),
                pl.BlockSpec((1, 1, tk, LANES), lambda b, h, i, j: (b, v_of_q(h), j, 0))]
    args = [q, k, v]
    if bias is not None:
        assert bias.shape[2] == unit
        in_specs.append(pl.BlockSpec((1, 5, unit, unit), lambda b, h, i, j: (v_of_q(h), 0, 0, 0)))
        args.append(bias)
    return pl.pallas_call(
        functools.partial(_flash_body, unit=unit, sub=sub, sub_far=sub_far, has_bias=bias is not None),
        grid=(batch, nq, seq // tq, seq // tk),
        in_specs=in_specs,
        out_specs=pl.BlockSpec((1, 1, tq, HEAD_DIM), lambda b, h, i, j: (b, h, i, 0)),
        out_shape=_sds((batch, nq, seq, HEAD_DIM)),
        scratch_shapes=[pltpu.VMEM((tq, LANES), F32), pltpu.VMEM((tq, LANES), F32)],
        compiler_params=_cp("parallel", "parallel", "parallel", "arbitrary"),
        name="flash_bias" if bias is not None else "flash",
    )(*args)


def _t5_bucket_table(t):
    rel = np.arange(-(3 * t - 1), 3 * t)
    half = NUM_BUCKETS // 2
    max_exact = half // 2
    ret = np.where(rel > 0, half, 0)
    n = np.abs(rel)
    nf = np.maximum(n, 1).astype(np.float64)
    large = max_exact + (np.log(nf / max_exact) / math.log(MAX_DISTANCE / max_exact) * (half - max_exact)).astype(np.int64)
    large = np.minimum(large, half - 1)
    return ret + np.where(n < max_exact, n, large)


def diff_bias_tiles(rel_bias, t):
    assert t >= MAX_DISTANCE, "offsets of two or more tiles must lie in the saturated buckets"
    heads = rel_bias.shape[1]
    vec = rel_bias[jnp.asarray(_t5_bucket_table(t), dtype=I32)].T * LOG2E
    tiles = []
    for d in range(-2, 3):
        lo = d * t + 2 * t
        w = vec[:, lo:lo + 2 * t - 1]
        u = jnp.concatenate([w[:, t - 1:], jnp.zeros((heads, 1), F32), w[:, :t - 1]], axis=1)
        skew = jnp.tile(u, (1, t))[:, :t * (2 * t - 1)].reshape(heads, t, 2 * t - 1)
        tiles.append(skew[:, :, :t])
    return jnp.stack(tiles, axis=1)


def rope_tables(seq):
    rows = seq // GRID_W
    row = jnp.repeat(jnp.arange(rows, dtype=F32), GRID_W)
    col = jnp.tile(jnp.arange(GRID_W, dtype=F32), rows)
    freq = ROPE_THETA ** (-jnp.arange(0, ROPE_AXIS_DIM, 2, dtype=F32) / ROPE_AXIS_DIM)
    ang = jnp.concatenate([row[:, None] * freq, col[:, None] * freq], -1)
    cos, sin = jnp.cos(ang), jnp.sin(ang)
    return jnp.concatenate([cos, cos] * 2, -1), jnp.concatenate([-sin, sin] * 2, -1)


def _conv3_body(x0, x1, x2, w0, w1, w2, b0, b1, b2, o0, o1, o2):
    for x_ref, w_ref, b_ref, o_ref in ((x0, w0, b0, o0), (x1, w1, b1, o1), (x2, w2, b2, o2)):
        x = x_ref[0]
        n = x.shape[0]
        t = lax.broadcasted_iota(I32, (n, 1), 0)
        prev = jnp.where(t == 0, 0.0, pltpu.roll(x, 1, 0))
        nxt = jnp.where(t == n - 1, 0.0, pltpu.roll(x, n - 1, 0))
        o_ref[0] = w_ref[0:1] * prev + w_ref[1:2] * x + w_ref[2:3] * nxt + b_ref[...]


def hyena_conv3(z_hy, conv_w, conv_b, batch, seq):
    z = z_hy.reshape(batch, seq, HY_COLS)
    nc = HY_WIDTH // LANES
    x_spec = lambda g: pl.BlockSpec((1, seq, LANES), lambda b, c: (b, 0, g * nc + c))
    w_spec = lambda g: pl.BlockSpec((3, LANES), lambda b, c: (0, g * nc + c))
    b_spec = lambda g: pl.BlockSpec((1, LANES), lambda b, c: (0, g * nc + c))
    out = pl.BlockSpec((1, seq, LANES), lambda b, c: (b, 0, c))
    return pl.pallas_call(
        _conv3_body,
        grid=(batch, nc),
        in_specs=[x_spec(0), x_spec(1), x_spec(2), w_spec(0), w_spec(1), w_spec(2),
                  b_spec(0), b_spec(1), b_spec(2)],
        out_specs=[out, out, out],
        out_shape=[_sds((batch, seq, HY_WIDTH))] * 3,
        compiler_params=_cp("parallel", "parallel"),
        name="hyena_conv3",
    )(z, z, z, conv_w, conv_w, conv_w, conv_b, conv_b, conv_b)


def _filt_body(feat_ref, dist_ref, w1, b1, f1, w2, b2, f2, w3, dec, h_ref, s_ref):
    dot = functools.partial(jnp.dot, precision=HI, preferred_element_type=F32)
    hid = jnp.sin(f1[...] * (dot(feat_ref[...], w1[...]) + b1[...]))
    hid = jnp.sin(f2[...] * (dot(hid, w2[...]) + b2[...]))
    h = dot(hid, w3[...]) * jnp.exp(-dist_ref[...] * jnp.abs(dec[...]))
    h_ref[...] = h

    @pl.when(pl.program_id(0) == 0)
    def _():
        s_ref[...] = jnp.zeros_like(s_ref)

    s_ref[...] += jnp.sum(jnp.abs(h), axis=0, keepdims=True)


def hyena_filter(seq, w1, b1, f1, w2, b2, f2, w3, decay):
    j = jnp.arange(seq, dtype=F32)
    tt = j / seq
    ang = 2.0 * jnp.pi * tt[:, None] * jnp.arange(1, HY_BANDS + 1, dtype=F32)
    feat = jnp.concatenate([tt[:, None], jnp.cos(ang), jnp.sin(ang)], -1)
    feat = jnp.pad(feat, ((0, 0), (0, LANES - HY_EMB)))
    half = seq // 2
    dist = (jnp.abs(j - half) / half)[:, None]
    w1p = jnp.pad(w1, ((0, LANES - HY_EMB), (0, 0)))
    tm = min(1024, seq)
    hw = HY_ORDER * HY_WIDTH
    const = lambda i: (0, 0)
    return pl.pallas_call(
        _filt_body,
        grid=(seq // tm,),
        in_specs=[pl.BlockSpec((tm, LANES), lambda i: (i, 0)), pl.BlockSpec((tm, 1), lambda i: (i, 0)),
                  pl.BlockSpec((LANES, HY_FFN), const), pl.BlockSpec((1, HY_FFN), const),
                  pl.BlockSpec((1, HY_FFN), const), pl.BlockSpec((HY_FFN, HY_FFN), const),
                  pl.BlockSpec((1, HY_FFN), const), pl.BlockSpec((1, HY_FFN), const),
                  pl.BlockSpec((HY_FFN, hw), const), pl.BlockSpec((1, hw), const)],
        out_specs=[pl.BlockSpec((tm, hw), lambda i: (i, 0)), pl.BlockSpec((1, hw), const)],
        out_shape=[_sds((seq, hw)), _sds((1, hw))],
        compiler_params=_cp("arbitrary"),
        name="hyena_filter",
    )(feat, dist, w1p, b1[None], f1[None], w2, b2[None], f2[None], w3, decay.reshape(1, hw))


def dft_tables(seq):
    n = 2 * seq
    n2 = FFT_N2
    n1 = n // n2
    n1h = n1 // 2
    nk = -(-(n1h + 1) // SUBLANES) * SUBLANES
    k1 = jnp.arange(nk, dtype=I32)
    live = (k1 <= n1h).astype(F32)
    ang = (2.0 * jnp.pi / n1) * ((k1[:, None] * jnp.arange(n1h, dtype=I32)[None, :]) % n1).astype(F32)
    outer_fwd = jnp.concatenate([jnp.cos(ang), -jnp.sin(ang)], axis=0) * jnp.tile(live, 2)[:, None]
    t1 = jnp.arange(n1h, dtype=I32) + n1 // 4
    ang = (2.0 * jnp.pi / n1) * ((t1[:, None] * k1[None, :]) % n1).astype(F32)
    weight = live * jnp.where((k1 == 0) | (k1 == n1h), 1.0, 2.0) / n
    outer_inv = jnp.concatenate([jnp.cos(ang), -jnp.sin(ang)], axis=1) * jnp.tile(weight, 2)[None, :]
    k2 = jnp.arange(n2, dtype=I32)
    t2 = jnp.arange(n2, dtype=I32)
    phase = (t2[None, None, :] * (k2[None, :, None] * n1 + k1[:, None, None])) % n
    ang = (2.0 * jnp.pi / n) * phase.astype(F32)
    fr, fi = jnp.cos(ang), -jnp.sin(ang)
    inner = jnp.concatenate([jnp.concatenate([fr, -fi], axis=2),
                             jnp.concatenate([fi, fr], axis=2)], axis=1)
    return _split_bf16(outer_fwd), _split_bf16(outer_inv), _split_bf16(inner), _split_bf16(jnp.swapaxes(inner, 1, 2))


def _split_bf16(x):
    hi = x.astype(BF16)
    return hi, (x - hi.astype(F32)).astype(BF16)


def _dot_split(w, a):
    w_hi, w_lo = w
    a_hi, a_lo = _split_bf16(a)
    dot = functools.partial(jnp.dot, preferred_element_type=F32)
    return dot(w_hi, a_hi) + (dot(w_hi, a_lo) + dot(w_lo, a_hi))


def _dot_split_rhs(a, w):
    w_hi, w_lo = w
    a_hi, a_lo = _split_bf16(a)
    dot = functools.partial(jnp.dot, preferred_element_type=F32)
    return dot(a_hi, w_hi) + (dot(a_lo, w_hi) + dot(a_hi, w_lo))


def _every(ref, s, n):
    return ref[pl.ds(s, n, stride=FFT_ROWS), :]


def _fft_outer_body(f_ref, fl_ref, x_ref, o_ref):
    n1h = x_ref.shape[0]
    rows = f_ref.shape[0]
    x2 = x_ref.reshape(n1h * FFT_ROWS, LANES)
    o2 = o_ref.reshape(rows * FFT_ROWS, LANES)
    x = jnp.concatenate([_every(x2, s, n1h) for s in range(FFT_ROWS)], axis=1)
    r = _dot_split((f_ref[...], fl_ref[...]), x)
    for s in range(FFT_ROWS):
        o2[pl.ds(s, rows, stride=FFT_ROWS), :] = r[:, s * LANES:(s + 1) * LANES]


def fft_outer(x4, table):
    batch, n1h, n2, ch = x4.shape
    nk = table[0].shape[0] // 2
    tab = pl.BlockSpec((2 * nk, n1h), lambda b, c, l: (0, 0))
    return pl.pallas_call(
        _fft_outer_body,
        grid=(batch, n2 // FFT_ROWS, ch // LANES),
        in_specs=[tab, tab, pl.BlockSpec((None, n1h, FFT_ROWS, LANES), lambda b, c, l: (b, 0, c, l))],
        out_specs=pl.BlockSpec((None, 2, nk, FFT_ROWS, LANES), lambda b, c, l: (b, 0, 0, c, l)),
        out_shape=_sds((batch, 2, nk, n2, ch)),
        compiler_params=_cp("parallel", "parallel", "parallel"),
        name="fft_outer",
    )(*table, x4)


def _fft_filter_body(mh_ref, ml_ref, s_ref, a_ref, o_ref):
    n2 = FFT_N2
    a = a_ref[0, :, 0].reshape(2 * n2, HY_WIDTH)
    x = _dot_split((mh_ref[0], ml_ref[0]), a) * (1.0 / s_ref[...])
    o_ref[:, 0] = x.reshape(2, n2, HY_WIDTH)


def fft_filter(a5, inner, s):
    _, _, n1, n2, ch = a5.shape
    return pl.pallas_call(
        _fft_filter_body,
        grid=(n1, ch // HY_WIDTH),
        in_specs=[pl.BlockSpec((1, 2 * n2, 2 * n2), lambda k, c: (k, 0, 0)),
                  pl.BlockSpec((1, 2 * n2, 2 * n2), lambda k, c: (k, 0, 0)),
                  pl.BlockSpec((1, HY_WIDTH), lambda k, c: (0, c)),
                  pl.BlockSpec((1, 2, 1, n2, HY_WIDTH), lambda k, c: (0, 0, k, 0, c))],
        out_specs=pl.BlockSpec((2, 1, n2, HY_WIDTH), lambda k, c: (0, k, 0, c)),
        out_shape=_sds((2, n1, n2, ch)),
        compiler_params=_cp("parallel", "parallel"),
        name="fft_filter",
    )(*inner, s, a5)


def _fft_mid_body(mh_ref, ml_ref, th_ref, tl_ref, h_ref, a_ref, o_ref, *, nb):
    n2 = FFT_N2
    m = (mh_ref[0], ml_ref[0])
    mt = (th_ref[0], tl_ref[0])
    hr = h_ref[0, 0]
    hi = h_ref[1, 0]
    for b in range(nb):
        x = _dot_split(m, a_ref[b, :, 0].reshape(2 * n2, HY_WIDTH))
        xr, xi = x[:n2], x[n2:]
        y = jnp.concatenate([xr * hr - xi * hi, xr * hi + xi * hr], axis=0)
        o_ref[b, :, 0] = _dot_split(mt, y).reshape(2, n2, HY_WIDTH)


def fft_mid(a5, inner, inner_t, hf, order):
    nb, _, n1, n2, ch = a5.shape
    mat = pl.BlockSpec((1, 2 * n2, 2 * n2), lambda k: (k, 0, 0))
    blk = pl.BlockSpec((nb, 2, 1, n2, ch), lambda k: (0, 0, k, 0, 0))
    return pl.pallas_call(
        functools.partial(_fft_mid_body, nb=nb),
        grid=(n1,),
        in_specs=[mat, mat, mat, mat, pl.BlockSpec((2, 1, n2, ch), lambda k: (0, k, 0, order)), blk],
        out_specs=blk,
        out_shape=_sds(a5.shape),
        compiler_params=_cp("parallel"),
        name="fft_mid",
    )(*inner, *inner_t, hf, a5)


def _fft_inv_body(g_ref, gl_ref, b_ref, gate_ref, y_ref, fb_ref, o_ref):
    n1h, rows = g_ref.shape
    b2 = b_ref.reshape(rows * FFT_ROWS, LANES)
    gate2 = gate_ref.reshape(n1h * FFT_ROWS, LANES)
    y2 = y_ref.reshape(n1h * FFT_ROWS, LANES)
    o2 = o_ref.reshape(n1h * FFT_ROWS, LANES)
    bm = jnp.concatenate([_every(b2, s, rows) for s in range(FFT_ROWS)], axis=1)
    c = _dot_split((g_ref[...], gl_ref[...]), bm)
    fb = fb_ref[...]
    for s in range(FFT_ROWS):
        conv = c[:, s * LANES:(s + 1) * LANES]
        o2[pl.ds(s, n1h, stride=FFT_ROWS), :] = _every(gate2, s, n1h) * (conv + fb * _every(y2, s, n1h))


def fft_inv_gate(b5, table, gate4, y4, fbias):
    batch, _, nk, n2, ch = b5.shape
    n1h = table[0].shape[0]
    blk = pl.BlockSpec((None, n1h, FFT_ROWS, LANES), lambda b, c, l: (b, 0, c, l))
    tab = pl.BlockSpec((n1h, 2 * nk), lambda b, c, l: (0, 0))
    return pl.pallas_call(
        _fft_inv_body,
        grid=(batch, n2 // FFT_ROWS, ch // LANES),
        in_specs=[tab, tab,
                  pl.BlockSpec((None, 2, nk, FFT_ROWS, LANES), lambda b, c, l: (b, 0, 0, c, l)), blk, blk,
                  pl.BlockSpec((1, LANES), lambda b, c, l: (0, l))],
        out_specs=blk,
        out_shape=_sds((batch, n1h, n2, ch)),
        compiler_params=_cp("parallel", "parallel", "parallel"),
        name="fft_inv_gate",
    )(*table, b5, gate4, y4, fbias.reshape(1, ch))


def hyena_spectrum(seq, tabs, w1, b1, f1, w2, b2, f2, w3, decay):
    outer_fwd, _, inner, _ = tabs
    n1h = outer_fwd[0].shape[1]
    hw = HY_ORDER * HY_WIDTH
    h, s = hyena_filter(seq, w1, b1, f1, w2, b2, f2, w3, decay)
    return fft_filter(fft_outer(h.reshape(1, n1h, FFT_N2, hw), outer_fwd), inner, s)


def hyena_mixer(z_hy, conv_w, conv_b, hf, f_bias, tabs, batch, seq):
    outer_fwd, outer_inv, inner, inner_t = tabs
    n1h = outer_fwd[0].shape[1]
    shape4 = (batch, n1h, FFT_N2, HY_WIDTH)
    v, g0, g1 = hyena_conv3(z_hy, conv_w, conv_b, batch, seq)
    y = v.reshape(shape4)
    for order, gate in enumerate((g0, g1)):
        bm = fft_mid(fft_outer(y, outer_fwd), inner, inner_t, hf, order)
        y = fft_inv_gate(bm, outer_inv, gate.reshape(shape4), y, f_bias[order])
    return y.reshape(batch * seq, HY_WIDTH)


def _mix_body(lam_ref, h_ref, s5_ref, gqa_ref, diff_ref, hy_ref, gs5, ggqa, gsub, ghy, w_ref, o_ref, *, keep):
    lam = lam_ref[0]
    a = _rms(s5_ref[...], gs5[...])
    b = _rms(jnp.concatenate([gqa_ref[0, h] for h in range(GQA_HEADS)], axis=1), ggqa[...])
    c = jnp.concatenate(
        [_rms(diff_ref[0, 2 * h] - lam * diff_ref[0, 2 * h + 1], gsub[...]) * keep for h in range(DIFF_HEADS)],
        axis=1)
    d = _rms(hy_ref[...], ghy[...])
    mixed = jnp.concatenate([a, b, c, d], axis=1).astype(BF16)
    o_ref[...] = h_ref[...] + jnp.dot(mixed, w_ref[...], preferred_element_type=F32)


def mix_out(lam, h, y_s5, o_gqa, o_diff, y_hy, gs5, ggqa, gsub, ghy, w_out, batch, seq, lambda_init):
    tm = min(ROW_TILE, seq)
    nb = seq // tm
    row = lambda w: pl.BlockSpec((tm, w), lambda b, i: (b * nb + i, 0))
    vec = lambda w: pl.BlockSpec((1, w), lambda b, i: (0, 0))
    heads = lambda n: pl.BlockSpec((1, n, tm, HEAD_DIM), lambda b, i: (b, 0, i, 0))
    time_major = pl.BlockSpec((tm, GROUP_WIDTH), lambda b, i: (i, b))
    return pl.pallas_call(
        functools.partial(_mix_body, keep=1.0 - lambda_init),
        grid=(batch, nb),
        in_specs=[pl.BlockSpec(memory_space=pltpu.SMEM), row(D_MODEL), time_major, heads(GQA_HEADS),
                  heads(2 * DIFF_HEADS), row(GROUP_WIDTH), vec(GROUP_WIDTH), vec(GROUP_WIDTH), vec(HEAD_DIM),
                  vec(GROUP_WIDTH), pl.BlockSpec((D_MODEL, D_MODEL), lambda b, i: (0, 0))],
        out_specs=row(D_MODEL),
        out_shape=_sds((batch * seq, D_MODEL)),
        compiler_params=_cp("parallel", "parallel"),
        name="mix_out",
    )(lam, h, y_s5, o_gqa, o_diff, y_hy, gs5, ggqa, gsub, ghy, w_out)


def _router_body(h_ref, g_ref, w_ref, wl_ref, b_ref, xn_ref, ti_ref, tw_ref, cnt_ref):
    xn = _rms(h_ref[...], g_ref[...])
    xn_ref[...] = xn
    logits = _dot_split_rhs(xn, (w_ref[...], wl_ref[...])) + b_ref[...]
    lane = lax.broadcasted_iota(I32, logits.shape, 1)
    vals = logits
    tops, idxs = [], []
    hot = jnp.zeros(logits.shape, F32)
    for _ in range(TOP_K):
        m = jnp.max(vals, axis=-1, keepdims=True)
        idx = jnp.min(jnp.where(vals == m, lane, N_EXPERTS), axis=-1, keepdims=True)
        sel = lane == idx
        tops.append(m)
        idxs.append(idx)
        hot = hot + sel.astype(F32)
        vals = jnp.where(sel, -jnp.inf, vals)
    es = [jnp.exp(t - tops[0]) for t in tops]
    den = es[0] + es[1] + es[2] + es[3]
    ti_ref[...] = jnp.concatenate(idxs, axis=1)
    tw_ref[...] = jnp.concatenate([e / den for e in es], axis=1)

    @pl.when(pl.program_id(0) == 0)
    def _():
        cnt_ref[...] = jnp.zeros_like(cnt_ref)

    cnt_ref[...] += jnp.sum(hot, axis=0, keepdims=True)


def moe_router(h, g, w_router, b_router):
    t = h.shape[0]
    tm = min(ROW_TILE, t)
    const = lambda i: (0, 0)
    return pl.pallas_call(
        _router_body,
        grid=(t // tm,),
        in_specs=[pl.BlockSpec((tm, D_MODEL), lambda i: (i, 0)), pl.BlockSpec((1, D_MODEL), const),
                  pl.BlockSpec((D_MODEL, N_EXPERTS), const), pl.BlockSpec((D_MODEL, N_EXPERTS), const),
                  pl.BlockSpec((1, N_EXPERTS), const)],
        out_specs=[pl.BlockSpec((tm, D_MODEL), lambda i: (i, 0)), pl.BlockSpec((tm, TOP_K), lambda i: (i, 0)),
                   pl.BlockSpec((tm, TOP_K), lambda i: (i, 0)), pl.BlockSpec((1, N_EXPERTS), const)],
        out_shape=[_sds((t, D_MODEL)), _sds((t, TOP_K), I32), _sds((t, TOP_K)), _sds((1, N_EXPERTS))],
        compiler_params=_cp("arbitrary"),
        name="moe_router",
    )(h, g, *w_router, b_router)


def _rank_body(ti_ref, off_ref, pos_ref, stage_ref, cnt_ref, carry_ref):
    @pl.when(pl.program_id(0) == 0)
    def _():
        carry_ref[...] = jnp.zeros_like(carry_ref)

    ti = ti_ref[...]
    tm = ti.shape[0]
    lane = lax.broadcasted_iota(I32, (tm, N_EXPERTS), 1)
    hots = [lane == ti[:, k:k + 1] for k in range(TOP_K)]
    hot = sum(h.astype(F32) for h in hots)
    r = lax.broadcasted_iota(I32, (tm, tm), 0)
    c = lax.broadcasted_iota(I32, (tm, tm), 1)
    below = jnp.where(r > c, 1.0, 0.0).astype(BF16)
    local = jnp.dot(below, hot.astype(BF16), preferred_element_type=F32)
    before = local + carry_ref[...] + off_ref[...]
    pick = lambda table: jnp.concatenate(
        [jnp.sum(jnp.where(h, table, 0.0), axis=-1, keepdims=True) for h in hots], axis=1).astype(I32)
    pos_ref[...] = pick(before)
    count = jnp.sum(hot, axis=0, keepdims=True)
    er = lax.broadcasted_iota(I32, (N_EXPERTS, N_EXPERTS), 0)
    ec = lax.broadcasted_iota(I32, (N_EXPERTS, N_EXPERTS), 1)
    earlier = jnp.where(er < ec, 1.0, 0.0)
    chunks = jnp.ceil(count * (1.0 / GATHER_CHUNK))
    start = jnp.dot(chunks, earlier, precision=HI, preferred_element_type=F32) * GATHER_CHUNK
    stage_ref[...] = pick(local + start)
    cnt_ref[...] = jnp.broadcast_to(count, cnt_ref.shape)
    carry_ref[...] += count


def moe_rank(top_i, offsets):
    t = top_i.shape[0]
    tm = min(MOE_TOK_TILE, t)
    pairs = pl.BlockSpec((tm, TOP_K), lambda i: (i, 0))
    pos, stage, cnt = pl.pallas_call(
        _rank_body,
        grid=(t // tm,),
        in_specs=[pairs, pl.BlockSpec((1, N_EXPERTS), lambda i: (0, 0))],
        out_specs=[pairs, pairs, pl.BlockSpec((SUBLANES, N_EXPERTS), lambda i: (i, 0))],
        out_shape=[_sds((t, TOP_K), I32), _sds((t, TOP_K), I32), _sds((t // tm * SUBLANES, N_EXPERTS))],
        scratch_shapes=[pltpu.VMEM((1, N_EXPERTS), F32)],
        compiler_params=_cp("arbitrary"),
        name="moe_rank",
    )(top_i, offsets)
    return pos, stage, cnt[::SUBLANES]


def _row_copy(src, dst, sem):
    return pltpu.make_async_copy(src, dst, sem)


def _pad_zero_body(lt_ref, o_ref):
    del lt_ref
    o_ref[...] = jnp.zeros_like(o_ref)


def moe_pad_tiles(last_tile, n_slots):
    rows = MOE_TILE * ROW_TILES
    return pl.pallas_call(
        _pad_zero_body,
        grid_spec=pltpu.PrefetchScalarGridSpec(
            num_scalar_prefetch=1, grid=(N_EXPERTS,), in_specs=[],
            out_specs=pl.BlockSpec((rows, LANES), lambda e, lt: (lt[e], 0))),
        out_shape=_sds((n_slots * ROW_TILES, LANES)),
        compiler_params=_cp("arbitrary"),
        name="moe_pad_tiles",
    )(last_tile)


def _slot_rows(ref, slot):
    return ref.at[pl.ds(pl.multiple_of(slot * ROW_TILES, ROW_TILES), ROW_TILES)]


def _scatter_body(pos_ref, x_ref, padded_hbm, xs_hbm, buf, sem, *, tm):
    del padded_hbm
    x = x_ref[...]
    for s in range(ROW_TILES):
        buf[pl.ds(s, tm, stride=ROW_TILES), :] = x[:, s * LANES:(s + 1) * LANES]

    def start(r, c):
        for k in range(TOP_K):
            _row_copy(_slot_rows(buf, r), _slot_rows(xs_hbm, pos_ref[r * TOP_K + k]), sem).start()
        return c

    lax.fori_loop(0, tm, start, 0)
    for _ in range(TOP_K):
        _row_copy(buf, xs_hbm.at[pl.ds(0, tm * ROW_TILES)], sem).wait()


def moe_scatter(pos_flat, x, padded):
    t = x.shape[0]
    tm = min(MOE_TOK_TILE, t)
    return pl.pallas_call(
        functools.partial(_scatter_body, tm=tm),
        grid=(t // tm,),
        in_specs=[pl.BlockSpec((tm * TOP_K,), lambda i: (i,), memory_space=pltpu.SMEM),
                  pl.BlockSpec((tm, D_MODEL), lambda i: (i, 0)), pl.BlockSpec(memory_space=pl.ANY)],
        out_specs=pl.BlockSpec(memory_space=pl.ANY),
        out_shape=_sds(padded.shape),
        scratch_shapes=[pltpu.VMEM((tm * ROW_TILES, LANES), F32), pltpu.SemaphoreType.DMA(())],
        input_output_aliases={2: 0},
        compiler_params=_cp("arbitrary"),
        name="moe_scatter",
    )(pos_flat, x, padded)


def _experts_body(te_ref, nv_ref, x_ref, wgu_ref, bgu_ref, wd_ref, bd_ref, o_ref):
    del te_ref
    j = pl.program_id(0)
    dff = wd_ref.shape[1]

    @pl.when(j < nv_ref[0])
    def _():
        tm = x_ref.shape[0] // ROW_TILES
        x = jnp.concatenate([x_ref[pl.ds(s, tm, stride=ROW_TILES), :] for s in range(ROW_TILES)],
                            axis=1).astype(BF16)
        hgu = jnp.dot(x, wgu_ref[0], preferred_element_type=F32) + bgu_ref[0]
        glu = jnp.minimum(hgu[:, :dff], SWIGLU_LIMIT)
        lin = jnp.clip(hgu[:, dff:], -SWIGLU_LIMIT, SWIGLU_LIMIT)
        act = glu * jax.nn.sigmoid(SWIGLU_ALPHA * glu) * (lin + 1.0)
        y = jnp.dot(act.astype(BF16), wd_ref[0], preferred_element_type=F32) + bd_ref[0]
        for s in range(ROW_TILES):
            o_ref[pl.ds(s, tm, stride=ROW_TILES), :] = y[:, s * LANES:(s + 1) * LANES]

    @pl.when(j >= nv_ref[0])
    def _():
        o_ref[...] = jnp.zeros_like(o_ref)


def moe_experts(tile_expert, n_valid, xs, w_gu, b_gu, w_down, b_down):
    n_slots = xs.shape[0] // ROW_TILES
    tm = MOE_TILE
    dff = w_down.shape[1]
    blk = pl.BlockSpec((tm * ROW_TILES, LANES), lambda j, te, nv: (j, 0))
    return pl.pallas_call(
        _experts_body,
        grid_spec=pltpu.PrefetchScalarGridSpec(
            num_scalar_prefetch=2,
            grid=(n_slots // tm,),
            in_specs=[pl.BlockSpec((tm * ROW_TILES, LANES), lambda j, te, nv: (jnp.minimum(j, nv[0] - 1), 0)),
                      pl.BlockSpec((1, D_MODEL, 2 * dff), lambda j, te, nv: (te[j], 0, 0)),
                      pl.BlockSpec((1, 1, 2 * dff), lambda j, te, nv: (te[j], 0, 0)),
                      pl.BlockSpec((1, dff, D_MODEL), lambda j, te, nv: (te[j], 0, 0)),
                      pl.BlockSpec((1, 1, D_MODEL), lambda j, te, nv: (te[j], 0, 0))],
            out_specs=blk),
        out_shape=_sds((n_slots * ROW_TILES, LANES)),
        compiler_params=_cp("arbitrary"),
        name="moe_experts",
    )(tile_expert, n_valid, xs, w_gu, b_gu, w_down, b_down)


def _combine_body(slot_ref, w_ref, tab_ref, ys_hbm, o_ref, stage, mixed, sem, *, tm):
    chunk_rows = GATHER_CHUNK * ROW_TILES

    def chunk_copy(src_slot, dst_slot):
        return _row_copy(ys_hbm.at[pl.ds(pl.multiple_of(src_slot * ROW_TILES, ROW_TILES), chunk_rows)],
                         stage.at[pl.ds(pl.multiple_of(dst_slot * ROW_TILES, ROW_TILES), chunk_rows)], sem)

    def per_expert(e, c):
        base, chunks, start = tab_ref[e], tab_ref[N_EXPERTS + e], tab_ref[2 * N_EXPERTS + e]

        def issue(j, c2):
            chunk_copy(base + j * GATHER_CHUNK, start + j * GATHER_CHUNK).start()
            return c2

        return lax.fori_loop(0, chunks, issue, c)

    lax.fori_loop(0, N_EXPERTS, per_expert, 0)

    def wait(j, c):
        chunk_copy(0, 0).wait()
        return c

    lax.fori_loop(0, tab_ref[3 * N_EXPERTS], wait, 0)

    def mix(q, c):
        for u in range(MIX_UNROLL):
            r = q * MIX_UNROLL + u
            acc = None
            for k in range(TOP_K):
                p = r * TOP_K + k
                term = w_ref[p] * stage[pl.ds(pl.multiple_of(slot_ref[p] * ROW_TILES, ROW_TILES), ROW_TILES), :]
                acc = term if acc is None else acc + term
            mixed[pl.ds(pl.multiple_of(r * ROW_TILES, ROW_TILES), ROW_TILES), :] = acc
        return c

    lax.fori_loop(0, tm // MIX_UNROLL, mix, 0)
    o_ref[...] = jnp.concatenate([mixed[pl.ds(s, tm, stride=ROW_TILES), :] for s in range(ROW_TILES)], axis=1)


def moe_combine(stage_flat, w_flat, table, ys, t):
    tm = min(MOE_TOK_TILE, t)
    smem = pl.BlockSpec((tm * TOP_K,), lambda i: (i,), memory_space=pltpu.SMEM)
    stage_slots = tm * TOP_K + N_EXPERTS * GATHER_CHUNK
    return pl.pallas_call(
        functools.partial(_combine_body, tm=tm),
        grid=(t // tm,),
        in_specs=[smem, smem, pl.BlockSpec((LANES,), lambda i: (i,), memory_space=pltpu.SMEM),
                  pl.BlockSpec(memory_space=pl.ANY)],
        out_specs=pl.BlockSpec((tm, D_MODEL), lambda i: (i, 0)),
        out_shape=_sds((t, D_MODEL)),
        scratch_shapes=[pltpu.VMEM((stage_slots * ROW_TILES, LANES), F32),
                        pltpu.VMEM((tm * ROW_TILES, LANES), F32), pltpu.SemaphoreType.DMA(())],
        compiler_params=_cp("arbitrary"),
        name="moe_combine",
    )(stage_flat, w_flat, table, ys)


def gather_table(cnt_tile, offsets):
    base = offsets + jnp.cumsum(cnt_tile, axis=0) - cnt_tile
    chunks = jnp.ceil(cnt_tile / GATHER_CHUNK)
    start = (jnp.cumsum(chunks, axis=1) - chunks) * GATHER_CHUNK
    total = jnp.sum(chunks, axis=1, keepdims=True)
    pad = jnp.zeros((cnt_tile.shape[0], LANES - 3 * N_EXPERTS - 1), F32)
    return jnp.concatenate([base, chunks, start, total, pad], axis=1).astype(I32).reshape(-1)


def moe(h, g, w_router, b_router, w_gu, b_gu, w_down, b_down):
    t = h.shape[0]
    xn, top_i, top_w, counts = moe_router(h, g, w_router, b_router)
    n_tiles = (t * TOP_K) // MOE_TILE + N_EXPERTS + 1
    padded = jnp.ceil(counts[0] / MOE_TILE) * MOE_TILE
    ends = jnp.cumsum(padded)
    offsets = (ends - padded)[None]
    tile_start = jnp.arange(n_tiles, dtype=F32) * MOE_TILE
    tile_expert = jnp.minimum(jnp.sum(ends[None, :] <= tile_start[:, None], axis=1), N_EXPERTS - 1).astype(I32)
    n_valid = (ends[-1:] / MOE_TILE).astype(I32)
    own_last = jnp.where(padded > 0, ends / MOE_TILE - 1, -1.0)
    filled = lax.cummax(own_last, axis=0)
    first = jnp.min(jnp.where(padded > 0, own_last, jnp.inf))
    last_tile = jnp.where(filled < 0, first, filled).astype(I32)
    pos, stage, cnt_tile = moe_rank(top_i, offsets)
    xs = moe_scatter(pos.reshape(t * TOP_K), xn, moe_pad_tiles(last_tile, n_tiles * MOE_TILE))
    ys = moe_experts(tile_expert, n_valid, xs, w_gu, b_gu, w_down, b_down)
    flat = lambda a: a.reshape(t * TOP_K)
    return moe_combine(flat(stage), flat(top_w), gather_table(cnt_tile, offsets), ys, t)


def _ple_body(h_ref, m_ref, p_ref, wp_ref, wg_ref, o_ref):
    h = h_ref[...] + m_ref[...]
    e = jnp.dot(p_ref[...].astype(BF16), wp_ref[...], preferred_element_type=F32)
    gate = jnp.dot(_rms(h).astype(BF16), wg_ref[...], preferred_element_type=F32)
    o_ref[...] = h + e * jax.nn.sigmoid(gate)


def ple(h, moe_out, p, w_ple, w_gate):
    t = h.shape[0]
    tm = min(ROW_TILE, t)
    row = pl.BlockSpec((tm, D_MODEL), lambda i: (i, 0))
    return pl.pallas_call(
        _ple_body,
        grid=(t // tm,),
        in_specs=[row, row, pl.BlockSpec((tm, PLE_DIM), lambda i: (i, 0)),
                  pl.BlockSpec((PLE_DIM, D_MODEL), lambda i: (0, 0)),
                  pl.BlockSpec((D_MODEL, D_MODEL), lambda i: (0, 0))],
        out_specs=row,
        out_shape=_sds((t, D_MODEL)),
        compiler_params=_cp("parallel"),
        name="ple",
    )(h, moe_out, p, w_ple, w_gate)


def _deinterleave(n):
    return np.concatenate([np.arange(0, n, 2), np.arange(1, n, 2)])


def _halves(w):
    pairs = w.reshape(*w.shape[:-1], w.shape[-1] // 2, 2)
    return jnp.concatenate([pairs[..., 0], pairs[..., 1]], axis=-1)


def prepare_layer(i, prm):
    pair = _deinterleave(HEAD_DIM)
    cols = np.arange(IN_COLS)
    for h in range(GQA_HEADS + GQA_KV_HEADS):
        lo = OFF_GQA + h * HEAD_DIM
        cols[lo:lo + HEAD_DIM] = lo + pair
    lam = (jnp.exp(jnp.sum(prm["diff_lam_q1"][i] * prm["diff_lam_k1"][i]))
           - jnp.exp(jnp.sum(prm["diff_lam_q2"][i] * prm["diff_lam_k2"][i])) + _lambda_init(i))
    return dict(
        norm1_g=prm["norm1_g"][i][None],
        w_in=prm["w_in"][i][:, cols].astype(BF16),
        s5=s5_tables(prm["s5_lam_re"][i], prm["s5_lam_im"][i], prm["s5_log_dt"][i], prm["s5_b_re"][i],
                     prm["s5_b_im"][i], prm["s5_c_re"][i], prm["s5_c_im"][i]),
        s5_d=prm["s5_d"][i][None],
        s5_w_glu=prm["s5_w_glu"][i].astype(BF16),
        s5_b_glu=prm["s5_b_glu"][i][None],
        gqa_q_g=prm["gqa_q_g"][i][pair][None],
        gqa_k_g=prm["gqa_k_g"][i][pair][None],
        diff_q_g=prm["diff_q_g"][i][None],
        diff_k_g=prm["diff_k_g"][i][None],
        diff_lam=lam.reshape(1).astype(F32),
        diff_subln_g=prm["diff_subln_g"][i][None],
        out_g_s5=prm["out_g_s5"][i][None],
        out_g_gqa=prm["out_g_gqa"][i][None],
        out_g_hy=prm["out_g_hy"][i][None],
        w_out=prm["w_out"][i].astype(BF16),
        norm2_g=prm["norm2_g"][i][None],
        w_router=_split_bf16(prm["w_router"][i]),
        b_router=prm["b_router"][i][None],
        w_gu=_halves(prm["w_gu"][i]).astype(BF16),
        b_gu=_halves(prm["b_gu"][i])[:, None, :],
        w_down=prm["w_down"][i].astype(BF16),
        b_down=prm["b_down"][i][:, None, :],
        w_ple=prm["w_ple"][i].astype(BF16),
        w_ple_gate=prm["w_ple_gate"][i].astype(BF16),
    )


def _lambda_init(i):
    return 0.8 - 0.6 * math.exp(-0.3 * i)


def run_trunk(x, p, prm, layers):
    batch, seq, _ = x.shape
    t = batch * seq
    cos_t, sin_t = rope_tables(seq)
    tabs = dft_tables(seq)
    bias = diff_bias_tiles(prm["rel_bias"], min(ATTN_UNIT, seq))
    h = x.reshape(t, D_MODEL)
    for i, lw in enumerate(layers):
        z_s5, z_gqa, z_diff, z_hy = in_proj(h, lw["norm1_g"], lw["w_in"], batch, seq)
        y_s5 = s5_mixer(z_s5, lw["s5"], lw["s5_d"], lw["s5_w_glu"], lw["s5_b_glu"], batch, seq)
        rep = GQA_HEADS // GQA_KV_HEADS
        q, k, v = attn_prep(z_gqa, lw["gqa_q_g"], lw["gqa_k_g"], cos_t, sin_t, batch, seq,
                            GQA_HEADS, GQA_KV_HEADS, GQA_KV_HEADS, True, lambda hq: hq // rep)
        o_gqa = flash_attention(q, k, v, lambda hq: hq // rep)
        q, k, v = attn_prep(z_diff, lw["diff_q_g"], lw["diff_k_g"], cos_t, sin_t, batch, seq,
                            2 * DIFF_HEADS, 2 * DIFF_HEADS, DIFF_HEADS, False, lambda hq: hq)
        o_diff = flash_attention(q, k, v, lambda hq: hq // 2, bias)
        hf = hyena_spectrum(seq, tabs, prm["hy_w1"][i], prm["hy_b1"][i], prm["hy_freq1"][i], prm["hy_w2"][i],
                            prm["hy_b2"][i], prm["hy_freq2"][i], prm["hy_w3"][i], prm["hy_decay"][i])
        y_hy = hyena_mixer(z_hy, prm["hy_conv_w"][i], prm["hy_conv_b"][i][None], hf, prm["hy_bias"][i],
                           tabs, batch, seq)
        h = mix_out(lw["diff_lam"], h, y_s5, o_gqa, o_diff, y_hy, lw["out_g_s5"], lw["out_g_gqa"],
                    lw["diff_subln_g"], lw["out_g_hy"], lw["w_out"], batch, seq, _lambda_init(i))
        m = moe(h, lw["norm2_g"], lw["w_router"], lw["b_router"], lw["w_gu"], lw["b_gu"], lw["w_down"],
                lw["b_down"])
        h = ple(h, m, p[i].reshape(t, PLE_DIM), lw["w_ple"], lw["w_ple_gate"])
    return h.reshape(batch, seq, D_MODEL)


def kernel(x_prompt, x_sample, p_prompt, p_sample, rel_bias, norm1_g, w_in, s5_lam_re, s5_lam_im, s5_log_dt,
           s5_b_re, s5_b_im, s5_c_re, s5_c_im, s5_d, s5_w_glu, s5_b_glu, gqa_q_g, gqa_k_g, diff_q_g, diff_k_g,
           diff_lam_q1, diff_lam_k1, diff_lam_q2, diff_lam_k2, diff_subln_g, hy_conv_w, hy_conv_b, hy_w1, hy_b1,
           hy_freq1, hy_w2, hy_b2, hy_freq2, hy_w3, hy_decay, hy_bias, out_g_s5, out_g_gqa, out_g_hy, w_out,
           norm2_g, w_router, b_router, w_gu, b_gu, w_down, b_down, w_ple, w_ple_gate):
    prm = dict(rel_bias=rel_bias, norm1_g=norm1_g, w_in=w_in, s5_lam_re=s5_lam_re, s5_lam_im=s5_lam_im,
               s5_log_dt=s5_log_dt, s5_b_re=s5_b_re, s5_b_im=s5_b_im, s5_c_re=s5_c_re, s5_c_im=s5_c_im,
               s5_d=s5_d, s5_w_glu=s5_w_glu, s5_b_glu=s5_b_glu, gqa_q_g=gqa_q_g, gqa_k_g=gqa_k_g,
               diff_q_g=diff_q_g, diff_k_g=diff_k_g, diff_lam_q1=diff_lam_q1, diff_lam_k1=diff_lam_k1,
               diff_lam_q2=diff_lam_q2, diff_lam_k2=diff_lam_k2, diff_subln_g=diff_subln_g,
               hy_conv_w=hy_conv_w, hy_conv_b=hy_conv_b, hy_w1=hy_w1, hy_b1=hy_b1, hy_freq1=hy_freq1,
               hy_w2=hy_w2, hy_b2=hy_b2, hy_freq2=hy_freq2, hy_w3=hy_w3, hy_decay=hy_decay, hy_bias=hy_bias,
               out_g_s5=out_g_s5, out_g_gqa=out_g_gqa, out_g_hy=out_g_hy, w_out=w_out, norm2_g=norm2_g,
               w_router=w_router, b_router=b_router, w_gu=w_gu, b_gu=b_gu, w_down=w_down, b_down=b_down,
               w_ple=w_ple, w_ple_gate=w_ple_gate)
    layers = [prepare_layer(i, prm) for i in range(DEPTH)]
    return (run_trunk(x_prompt, p_prompt, prm, layers), run_trunk(x_sample, p_sample, prm, layers))
```

```python
import functools
import math

import numpy as np
import jax
import jax.numpy as jnp
from jax import lax
from jax.experimental import pallas as pl
from jax.experimental.pallas import tpu as pltpu

F32 = jnp.float32
BF16 = jnp.bfloat16
I32 = jnp.int32
HI = lax.Precision.HIGHEST

D_MODEL = 1024
DEPTH = 2
GROUP_WIDTH = 256
HEAD_DIM = 64
S5_GROUP_DIM = 16
S5_GROUPS = 16
S5_STATE = 64
S5_NSTATE = S5_GROUPS * S5_STATE
GQA_HEADS = 4
GQA_KV_HEADS = 2
DIFF_HEADS = 4
HY_WIDTH = 256
HY_ORDER = 2
HY_BANDS = 16
HY_EMB = 2 * HY_BANDS + 1
HY_FFN = 64
N_EXPERTS = 32
TOP_K = 4
SWIGLU_LIMIT = 7.0
SWIGLU_ALPHA = 1.702
NUM_BUCKETS = 32
MAX_DISTANCE = 128
GRID_W = 64
ROPE_THETA = 10000.0
ROPE_AXIS_DIM = HEAD_DIM // 2
PLE_DIM = 256
NORM_EPS = 1e-6
ATTN_SCALE = HEAD_DIM ** -0.5
LOG2E = math.log2(math.e)

S5_COLS = GROUP_WIDTH
GQA_QW = GQA_HEADS * HEAD_DIM
GQA_KW = GQA_KV_HEADS * HEAD_DIM
GQA_COLS = GQA_QW + 2 * GQA_KW
DIFF_QW = DIFF_HEADS * 2 * HEAD_DIM
DIFF_COLS = 2 * DIFF_QW + DIFF_HEADS * HEAD_DIM
HY_COLS = (HY_ORDER + 1) * HY_WIDTH
OFF_GQA = S5_COLS
OFF_DIFF = OFF_GQA + GQA_COLS
OFF_HY = OFF_DIFF + DIFF_COLS
IN_COLS = OFF_HY + HY_COLS
IN_SPLITS = ((0, OFF_GQA), (OFF_GQA, OFF_DIFF), (OFF_DIFF, OFF_HY), (OFF_HY, IN_COLS))

LANES = 128
SUBLANES = 8
ROW_TILES = D_MODEL // LANES
FFT_N2 = 128
FFT_ROWS = 64
VMEM_LIMIT = 56 * 1024 * 1024

ROW_TILE = 1024
ATTN_Q_TILE = 2048
ATTN_K_TILE = 2048
ATTN_UNIT = 512
ATTN_SUB = 512
ATTN_SUB_PLAIN = 1024
S5_TIME_BLOCK = 128
MOE_TILE = 512
MOE_TOK_TILE = 1024
MIX_UNROLL = 8
GATHER_CHUNK = 16


def _cp(*sem):
    return pltpu.CompilerParams(dimension_semantics=sem, vmem_limit_bytes=VMEM_LIMIT)


def _sds(shape, dtype=F32):
    return jax.ShapeDtypeStruct(shape, dtype)


def _rms(x, g=None):
    y = x * lax.rsqrt(jnp.mean(x * x, axis=-1, keepdims=True) + NORM_EPS)
    return y if g is None else y * g


def _inproj_body(x_ref, g_ref, w_ref, *o_refs):
    u = _rms(x_ref[...], g_ref[...]).astype(BF16)
    for o_ref, (lo, hi) in zip(o_refs, IN_SPLITS):
        o_ref[...] = jnp.dot(u, w_ref[:, lo:hi], preferred_element_type=F32)


def in_proj(h, g, w, batch, seq):
    t = h.shape[0]
    tm = min(ROW_TILE, seq)
    nb = seq // tm
    row_major = lambda i: (i, 0)
    time_major = lambda i: (i % nb, i // nb)
    widths = [hi - lo for lo, hi in IN_SPLITS]
    return pl.pallas_call(
        _inproj_body,
        grid=(t // tm,),
        in_specs=[pl.BlockSpec((tm, D_MODEL), row_major),
                  pl.BlockSpec((1, D_MODEL), lambda i: (0, 0)),
                  pl.BlockSpec((D_MODEL, IN_COLS), lambda i: (0, 0))],
        out_specs=[pl.BlockSpec((tm, widths[0]), time_major)] + [pl.BlockSpec((tm, w_), row_major) for w_ in widths[1:]],
        out_shape=[_sds((seq, batch * widths[0]))] + [_sds((t, w_)) for w_ in widths[1:]],
        compiler_params=_cp("parallel"),
        name="in_proj",
    )(h, g, w)


def s5_tables(lam_re, lam_im, log_dt, b_re, b_im, c_re, c_im):
    lr = jnp.minimum(lam_re, -1e-4)
    li = lam_im
    dt = jnp.exp(log_dt)[..., None]
    mag = jnp.exp(lr * dt)
    ar = mag * jnp.cos(li * dt)
    ai = mag * jnp.sin(li * dt)
    den = lr * lr + li * li
    cr = ((ar - 1.0) * lr + ai * li) / den
    ci = (ai * lr - (ar - 1.0) * li) / den
    bbr = cr[..., None] * b_re - ci[..., None] * b_im
    bbi = cr[..., None] * b_im + ci[..., None] * b_re
    eye = jnp.eye(S5_GROUPS, dtype=F32)
    bb = jnp.stack([bbr, bbi], axis=1)
    wb = jnp.einsum("dpgnh,gk->dghpkn", bb, eye).reshape(2, S5_COLS, 2 * S5_NSTATE)
    cc = jnp.stack([c_re, -c_im], axis=0)
    wc = jnp.einsum("pdghn,gk->dpgnkh", cc, eye).reshape(2, 2 * S5_NSTATE, S5_COLS)
    a_re = jnp.repeat(ar.reshape(2, S5_NSTATE), 4, axis=0)
    a_im = jnp.repeat(ai.reshape(2, S5_NSTATE), 4, axis=0)
    return wb.astype(BF16), wc.astype(BF16), a_re, a_im


def _s5_body(uf_ref, ub_ref, wb_ref, wc_ref, ar_ref, ai_ref, yf_ref, yb_ref, rows_ref, bf_ref, bb_ref, xr_ref, xi_ref,
             *, tb):
    @pl.when(pl.program_id(0) == 0)
    def _():
        xr_ref[...] = jnp.zeros_like(xr_ref)
        xi_ref[...] = jnp.zeros_like(xi_ref)

    halves = range(S5_COLS // LANES)
    wide = lambda r0: jnp.concatenate([rows_ref[h, r0:r0 + tb * 4, :] for h in halves], axis=1)
    for b in range(4):
        for h in halves:
            cols = slice(b * S5_COLS + h * LANES, b * S5_COLS + (h + 1) * LANES)
            rows_ref[h, pl.ds(b, tb, stride=4), :] = uf_ref[:, cols]
            rows_ref[h, pl.ds(tb * 4 + b, tb, stride=4), :] = ub_ref[:, cols]
    bf_ref[...] = jnp.dot(wide(0).astype(BF16), wb_ref[0], preferred_element_type=F32)
    bb_ref[...] = jnp.dot(wide(tb * 4).astype(BF16), wb_ref[1], preferred_element_type=F32)
    ar = ar_ref[...]
    ai = ai_ref[...]
    low = lax.broadcasted_iota(I32, (SUBLANES, 1), 0) < 4
    half = SUBLANES // 2
    re = slice(0, S5_NSTATE)
    im = slice(S5_NSTATE, 2 * S5_NSTATE)

    def pair(m, carry):
        xr, xi = carry
        rf = pl.ds(pl.multiple_of(m * SUBLANES, SUBLANES), SUBLANES)
        rb = pl.ds(pl.multiple_of((tb // 2 - 1 - m) * SUBLANES, SUBLANES), SUBLANES)
        f_r, f_i, b_r, b_i = bf_ref[rf, re], bf_ref[rf, im], bb_ref[rb, re], bb_ref[rb, im]
        in_r = jnp.where(low, f_r, b_r)
        in_i = jnp.where(low, f_i, b_i)
        x1r = ar * xr - ai * xi + in_r
        x1i = ar * xi + ai * xr + in_i
        in_r = pltpu.roll(jnp.where(low, b_r, f_r), half, 0)
        in_i = pltpu.roll(jnp.where(low, b_i, f_i), half, 0)
        x2r = ar * x1r - ai * x1i + in_r
        x2i = ar * x1i + ai * x1r + in_i
        s2r = pltpu.roll(x2r, half, 0)
        s2i = pltpu.roll(x2i, half, 0)
        bf_ref[rf, re] = jnp.where(low, x1r, s2r)
        bf_ref[rf, im] = jnp.where(low, x1i, s2i)
        bb_ref[rb, re] = jnp.where(low, s2r, x1r)
        bb_ref[rb, im] = jnp.where(low, s2i, x1i)
        return x2r, x2i

    xr, xi = lax.fori_loop(0, tb // 2, pair, (xr_ref[...], xi_ref[...]))
    xr_ref[...] = xr
    xi_ref[...] = xi
    yf = jnp.dot(bf_ref[...].astype(BF16), wc_ref[0], preferred_element_type=F32)
    yb = jnp.dot(bb_ref[...].astype(BF16), wc_ref[1], preferred_element_type=F32)
    for h in halves:
        rows_ref[h, 0:tb * 4, :] = yf[:, h * LANES:(h + 1) * LANES]
        rows_ref[h, tb * 4:tb * 8, :] = yb[:, h * LANES:(h + 1) * LANES]
    for b in range(4):
        for h in halves:
            cols = slice(b * S5_COLS + h * LANES, b * S5_COLS + (h + 1) * LANES)
            yf_ref[:, cols] = rows_ref[h, pl.ds(b, tb, stride=4), :]
            yb_ref[:, cols] = rows_ref[h, pl.ds(tb * 4 + b, tb, stride=4), :]


def s5_scan(u, wb, wc, a_re, a_im, seq):
    tb = min(S5_TIME_BLOCK, seq)
    nb = seq // tb
    rows = tb * 4
    fwd = pl.BlockSpec((tb, 4 * S5_COLS), lambda i: (i, 0))
    bwd = pl.BlockSpec((tb, 4 * S5_COLS), lambda i: (nb - 1 - i, 0))
    return pl.pallas_call(
        functools.partial(_s5_body, tb=tb),
        grid=(nb,),
        in_specs=[fwd, bwd,
                  pl.BlockSpec((2, S5_COLS, 2 * S5_NSTATE), lambda i: (0, 0, 0)),
                  pl.BlockSpec((2, 2 * S5_NSTATE, S5_COLS), lambda i: (0, 0, 0)),
                  pl.BlockSpec((SUBLANES, S5_NSTATE), lambda i: (0, 0)),
                  pl.BlockSpec((SUBLANES, S5_NSTATE), lambda i: (0, 0))],
        out_specs=[fwd, bwd],
        out_shape=[_sds((seq, 4 * S5_COLS))] * 2,
        scratch_shapes=[pltpu.VMEM((S5_COLS // LANES, 2 * rows, LANES), F32),
                        pltpu.VMEM((rows, 2 * S5_NSTATE), F32), pltpu.VMEM((rows, 2 * S5_NSTATE), F32),
                        pltpu.VMEM((SUBLANES, S5_NSTATE), F32), pltpu.VMEM((SUBLANES, S5_NSTATE), F32)],
        compiler_params=_cp("arbitrary"),
        name="s5_scan",
    )(u, u, wb, wc, a_re, a_im)


def _s5_out_body(yf_ref, yb_ref, u_ref, d_ref, w_ref, b_ref, o_ref):
    for b in range(o_ref.shape[1] // S5_COLS):
        cols = slice(b * S5_COLS, (b + 1) * S5_COLS)
        y = jax.nn.gelu(yf_ref[:, cols] + yb_ref[:, cols] + u_ref[:, cols] * d_ref[...])
        gate = jnp.dot(y.astype(BF16), w_ref[...], preferred_element_type=F32) + b_ref[...]
        o_ref[:, cols] = y * jax.nn.sigmoid(gate)


def s5_out(yf, yb, u, d, w_glu, b_glu):
    t, width = u.shape
    tm = min(ROW_TILE // 4, t)
    row = pl.BlockSpec((tm, width), lambda i: (i, 0))
    vec = pl.BlockSpec((1, S5_COLS), lambda i: (0, 0))
    return pl.pallas_call(
        _s5_out_body,
        grid=(t // tm,),
        in_specs=[row, row, row, vec, pl.BlockSpec((S5_COLS, S5_COLS), lambda i: (0, 0)), vec],
        out_specs=row,
        out_shape=_sds((t, width)),
        compiler_params=_cp("parallel"),
        name="s5_out",
    )(yf, yb, u, d, w_glu, b_glu)


def s5_mixer(z_s5, tabs, d, w_glu, b_glu, batch, seq):
    assert batch == 4 and seq % 2 == 0, "the scan packs 4 sequences x 2 directions onto the 8 sublanes"
    yf, yb = s5_scan(z_s5, *tabs, seq)
    return s5_out(yf, yb, z_s5, d, w_glu, b_glu)


def _prep_body(z_ref, qg_ref, kg_ref, c_ref, s_ref, q_ref, k_ref, v_ref, *, nq, nk, nv, rope, k_src):
    tm = z_ref.shape[0]
    lane = lax.broadcasted_iota(I32, (tm, LANES), 1)
    low = lane < HEAD_DIM

    def norm_pair(x, g):
        sq = x * x
        ss_lo = jnp.sum(jnp.where(low, sq, 0.0), axis=-1, keepdims=True)
        ss_hi = jnp.sum(jnp.where(low, 0.0, sq), axis=-1, keepdims=True)
        y = x * lax.rsqrt(jnp.where(low, ss_lo, ss_hi) * (1.0 / HEAD_DIM) + NORM_EPS) * g
        if rope:
            half = HEAD_DIM // 2
            partner = jnp.where(lane % HEAD_DIM < half, pltpu.roll(y, LANES - half, 1), pltpu.roll(y, half, 1))
            y = y * c_ref[...] + partner * s_ref[...]
        return y

    qg = qg_ref[...]
    kg = kg_ref[...]
    for p in range(nq // 2):
        q_ref[0, p] = (norm_pair(z_ref[:, p * LANES:(p + 1) * LANES], qg) * (ATTN_SCALE * LOG2E)).astype(BF16)
    k_lo = nq * HEAD_DIM
    pairs = [norm_pair(z_ref[:, k_lo + p * LANES:k_lo + (p + 1) * LANES], kg) for p in range(nk // 2)]
    for h in range(nq):
        src = k_src(h)
        x = pairs[src // 2]
        if src % 2 != h % 2:
            x = pltpu.roll(x, HEAD_DIM, 1)
        k_ref[0, h] = jnp.where(low if h % 2 == 0 else jnp.logical_not(low), x, 0.0).astype(BF16)
    ones_col = jnp.where(lax.broadcasted_iota(I32, (tm, LANES - HEAD_DIM), 1) == 0, 1.0, 0.0)
    for h in range(nv):
        lo = (nq + nk + h) * HEAD_DIM
        v_ref[0, h] = jnp.concatenate([z_ref[:, lo:lo + HEAD_DIM], ones_col], axis=1).astype(BF16)


def attn_prep(z, qg, kg, cos_t, sin_t, batch, seq, nq, nk, nv, rope, k_src):
    width = (nq + nk + nv) * HEAD_DIM
    tm = min(ROW_TILE, seq)
    nb = seq // tm
    vec = pl.BlockSpec((1, LANES), lambda b, i: (0, 0))
    tab = pl.BlockSpec((tm, LANES), lambda b, i: (i, 0))
    heads = lambda n: pl.BlockSpec((1, n, tm, LANES), lambda b, i: (b, 0, i, 0))
    return pl.pallas_call(
        functools.partial(_prep_body, nq=nq, nk=nk, nv=nv, rope=rope, k_src=k_src),
        grid=(batch, nb),
        in_specs=[pl.BlockSpec((tm, width), lambda b, i: (b * nb + i, 0)), vec, vec, tab, tab],
        out_specs=[heads(nq // 2), heads(nq), heads(nv)],
        out_shape=[_sds((batch, nq // 2, seq, LANES), BF16), _sds((batch, nq, seq, LANES), BF16),
                   _sds((batch, nv, seq, LANES), BF16)],
        compiler_params=_cp("parallel", "parallel"),
        name="attn_prep",
    )(z, jnp.tile(qg, (1, 2)), jnp.tile(kg, (1, 2)), cos_t, sin_t)


def _flash_body(q_ref, k_ref, v_ref, *rest, unit, sub, sub_far, has_bias):
    if has_bias:
        b_ref, o_ref, m_ref, acc_ref = rest
    else:
        o_ref, m_ref, acc_ref = rest
    i = pl.program_id(2)
    j = pl.program_id(3)
    tq = q_ref.shape[2]
    tk = k_ref.shape[2]
    q_units = tq // unit
    k_units = tk // unit

    @pl.when(j == 0)
    def _():
        m_ref[...] = jnp.full_like(m_ref, -jnp.inf)
        acc_ref[...] = jnp.zeros_like(acc_ref)

    def attend(rows, offset=None, far_sel=None):
        k = k_ref[0, 0]
        v = v_ref[0, 0]
        for g in range(tq // rows):
            r0 = g * rows
            s = lax.dot_general(q_ref[0, 0, r0:r0 + rows, :], k, (((1,), (1,)), ((), ())),
                                preferred_element_type=F32)
            m_prev = m_ref[r0:r0 + rows, :]
            kinds = [None] * k_units
            if offset is not None:
                tile_of = [offset * k_units + c - r0 // unit for c in range(k_units)]
                kinds = [None if abs(d) <= 1 else (0 if d < 0 else 4) for d in tile_of]
                s = jnp.concatenate(
                    [s[:, c * unit:(c + 1) * unit] + b_ref[0, d + 2, r0 % unit:r0 % unit + rows, :]
                     if abs(d) <= 1 else s[:, c * unit:(c + 1) * unit] for c, d in enumerate(tile_of)], axis=1)
            elif far_sel is not None:
                kinds = [far_sel] * k_units
            consts = {kind: b_ref[0, kind, 0:1, 0:LANES] for kind in set(kinds) if kind is not None}
            per_unit = unit // LANES
            blocks = [s[:, b * LANES:(b + 1) * LANES] for b in range(tk // LANES)]
            m_new = m_prev
            for kind in sorted(set(kinds), key=str):
                members = [blk for b, blk in enumerate(blocks) if kinds[b // per_unit] == kind]
                bmax = members[0]
                for blk in members[1:]:
                    bmax = jnp.maximum(bmax, blk)
                kind_max = jnp.max(bmax, axis=-1, keepdims=True)
                m_new = jnp.maximum(m_new, kind_max if kind is None else kind_max + consts[kind])
            shifts = [m_new if kind is None else m_new - consts[kind] for kind in kinds]
            alpha = jnp.exp2(m_prev - m_new)
            p = jnp.concatenate([jnp.exp2(blk - shifts[b // per_unit]) for b, blk in enumerate(blocks)],
                                axis=1).astype(BF16)
            acc_ref[r0:r0 + rows, :] = alpha * acc_ref[r0:r0 + rows, :] + jnp.dot(p, v, preferred_element_type=F32)
            m_ref[r0:r0 + rows, :] = m_new

    if has_bias:
        assert q_units == k_units and sub == unit, "the near-diagonal variants assume square steps of whole units"
        pl.when(j - i >= 2)(lambda: attend(sub_far, far_sel=4))
        pl.when(j - i <= -2)(lambda: attend(sub_far, far_sel=0))
        for offset in (-1, 0, 1):
            pl.when(j - i == offset)(functools.partial(attend, sub, offset=offset))
    else:
        attend(sub)

    @pl.when(j == pl.num_programs(3) - 1)
    def _():
        acc = acc_ref[...]
        o_ref[0, 0] = acc[:, :HEAD_DIM] / acc[:, HEAD_DIM:HEAD_DIM + 1]


def flash_attention(q, k, v, v_of_q, bias=None):
    batch, nq, seq, _ = k.shape
    tq = min(ATTN_Q_TILE, seq)
    tk = min(ATTN_K_TILE, seq)
    unit = min(ATTN_UNIT, seq)
    sub = min(ATTN_SUB if bias is not None else ATTN_SUB_PLAIN, tq)
    sub_far = min(ATTN_SUB_PLAIN, tq)
    in_specs = [pl.BlockSpec((1, 1, tq, LANES), lambda b, h, i, j: (b, h // 2, i, 0)),
                pl.BlockSpec((1, 1, tk, LANES), lambda b, h, i, j: (b, h, j, 0)),
                pl.BlockSpec((1, 1, tk, LANES), lambda b, h, i, j: (b, v_of_q(h), j, 0))]
    args = [q, k, v]
    if bias is not None:
        assert bias.shape[2] == unit
        in_specs.append(pl.BlockSpec((1, 5, unit, unit), lambda b, h, i, j: (v_of_q(h), 0, 0, 0)))
        args.append(bias)
    return pl.pallas_call(
        functools.partial(_flash_body, unit=unit, sub=sub, sub_far=sub_far, has_bias=bias is not None),
        grid=(batch, nq, seq // tq, seq // tk),
        in_specs=in_specs,
        out_specs=pl.BlockSpec((1, 1, tq, HEAD_DIM), lambda b, h, i, j: (b, h, i, 0)),
        out_shape=_sds((batch, nq, seq, HEAD_DIM)),
        scratch_shapes=[pltpu.VMEM((tq, LANES), F32), pltpu.VMEM((tq, LANES), F32)],
        compiler_params=_cp("parallel", "parallel", "parallel", "arbitrary"),
        name="flash_bias" if bias is not None else "flash",
    )(*args)


def _t5_bucket_table(t):
    rel = np.arange(-(3 * t - 1), 3 * t)
    half = NUM_BUCKETS // 2
    max_exact = half // 2
    ret = np.where(rel > 0, half, 0)
    n = np.abs(rel)
    nf = np.maximum(n, 1).astype(np.float64)
    large = max_exact + (np.log(nf / max_exact) / math.log(MAX_DISTANCE / max_exact) * (half - max_exact)).astype(np.int64)
    large = np.minimum(large, half - 1)
    return ret + np.where(n < max_exact, n, large)


def diff_bias_tiles(rel_bias, t):
    assert t >= MAX_DISTANCE, "offsets of two or more tiles must lie in the saturated buckets"
    heads = rel_bias.shape[1]
    vec = rel_bias[jnp.asarray(_t5_bucket_table(t), dtype=I32)].T * LOG2E
    tiles = []
    for d in range(-2, 3):
        lo = d * t + 2 * t
        w = vec[:, lo:lo + 2 * t - 1]
        u = jnp.concatenate([w[:, t - 1:], jnp.zeros((heads, 1), F32), w[:, :t - 1]], axis=1)
        skew = jnp.tile(u, (1, t))[:, :t * (2 * t - 1)].reshape(heads, t, 2 * t - 1)
        tiles.append(skew[:, :, :t])
    return jnp.stack(tiles, axis=1)


def rope_tables(seq):
    rows = seq // GRID_W
    row = jnp.repeat(jnp.arange(rows, dtype=F32), GRID_W)
    col = jnp.tile(jnp.arange(GRID_W, dtype=F32), rows)
    freq = ROPE_THETA ** (-jnp.arange(0, ROPE_AXIS_DIM, 2, dtype=F32) / ROPE_AXIS_DIM)
    ang = jnp.concatenate([row[:, None] * freq, col[:, None] * freq], -1)
    cos, sin = jnp.cos(ang), jnp.sin(ang)
    return jnp.concatenate([cos, cos] * 2, -1), jnp.concatenate([-sin, sin] * 2, -1)


def _conv3_body(x0, x1, x2, w0, w1, w2, b0, b1, b2, o0, o1, o2):
    for x_ref, w_ref, b_ref, o_ref in ((x0, w0, b0, o0), (x1, w1, b1, o1), (x2, w2, b2, o2)):
        x = x_ref[0]
        n = x.shape[0]
        t = lax.broadcasted_iota(I32, (n, 1), 0)
        prev = jnp.where(t == 0, 0.0, pltpu.roll(x, 1, 0))
        nxt = jnp.where(t == n - 1, 0.0, pltpu.roll(x, n - 1, 0))
        o_ref[0] = w_ref[0:1] * prev + w_ref[1:2] * x + w_ref[2:3] * nxt + b_ref[...]


def hyena_conv3(z_hy, conv_w, conv_b, batch, seq):
    z = z_hy.reshape(batch, seq, HY_COLS)
    nc = HY_WIDTH // LANES
    x_spec = lambda g: pl.BlockSpec((1, seq, LANES), lambda b, c: (b, 0, g * nc + c))
    w_spec = lambda g: pl.BlockSpec((3, LANES), lambda b, c: (0, g * nc + c))
    b_spec = lambda g: pl.BlockSpec((1, LANES), lambda b, c: (0, g * nc + c))
    out = pl.BlockSpec((1, seq, LANES), lambda b, c: (b, 0, c))
    return pl.pallas_call(
        _conv3_body,
        grid=(batch, nc),
        in_specs=[x_spec(0), x_spec(1), x_spec(2), w_spec(0), w_spec(1), w_spec(2),
                  b_spec(0), b_spec(1), b_spec(2)],
        out_specs=[out, out, out],
        out_shape=[_sds((batch, seq, HY_WIDTH))] * 3,
        compiler_params=_cp("parallel", "parallel"),
        name="hyena_conv3",
    )(z, z, z, conv_w, conv_w, conv_w, conv_b, conv_b, conv_b)


def _filt_body(feat_ref, dist_ref, w1, b1, f1, w2, b2, f2, w3, dec, h_ref, s_ref):
    dot = functools.partial(jnp.dot, precision=HI, preferred_element_type=F32)
    hid = jnp.sin(f1[...] * (dot(feat_ref[...], w1[...]) + b1[...]))
    hid = jnp.sin(f2[...] * (dot(hid, w2[...]) + b2[...]))
    h = dot(hid, w3[...]) * jnp.exp(-dist_ref[...] * jnp.abs(dec[...]))
    h_ref[...] = h

    @pl.when(pl.program_id(0) == 0)
    def _():
        s_ref[...] = jnp.zeros_like(s_ref)

    s_ref[...] += jnp.sum(jnp.abs(h), axis=0, keepdims=True)


def hyena_filter(seq, w1, b1, f1, w2, b2, f2, w3, decay):
    j = jnp.arange(seq, dtype=F32)
    tt = j / seq
    ang = 2.0 * jnp.pi * tt[:, None] * jnp.arange(1, HY_BANDS + 1, dtype=F32)
    feat = jnp.concatenate([tt[:, None], jnp.cos(ang), jnp.sin(ang)], -1)
    feat = jnp.pad(feat, ((0, 0), (0, LANES - HY_EMB)))
    half = seq // 2
    dist = (jnp.abs(j - half) / half)[:, None]
    w1p = jnp.pad(w1, ((0, LANES - HY_EMB), (0, 0)))
    tm = min(1024, seq)
    hw = HY_ORDER * HY_WIDTH
    const = lambda i: (0, 0)
    return pl.pallas_call(
        _filt_body,
        grid=(seq // tm,),
        in_specs=[pl.BlockSpec((tm, LANES), lambda i: (i, 0)), pl.BlockSpec((tm, 1), lambda i: (i, 0)),
                  pl.BlockSpec((LANES, HY_FFN), const), pl.BlockSpec((1, HY_FFN), const),
                  pl.BlockSpec((1, HY_FFN), const), pl.BlockSpec((HY_FFN, HY_FFN), const),
                  pl.BlockSpec((1, HY_FFN), const), pl.BlockSpec((1, HY_FFN), const),
                  pl.BlockSpec((HY_FFN, hw), const), pl.BlockSpec((1, hw), const)],
        out_specs=[pl.BlockSpec((tm, hw), lambda i: (i, 0)), pl.BlockSpec((1, hw), const)],
        out_shape=[_sds((seq, hw)), _sds((1, hw))],
        compiler_params=_cp("arbitrary"),
        name="hyena_filter",
    )(feat, dist, w1p, b1[None], f1[None], w2, b2[None], f2[None], w3, decay.reshape(1, hw))


def dft_tables(seq):
    n = 2 * seq
    n2 = FFT_N2
    n1 = n // n2
    n1h = n1 // 2
    nk = -(-(n1h + 1) // SUBLANES) * SUBLANES
    k1 = jnp.arange(nk, dtype=I32)
    live = (k1 <= n1h).astype(F32)
    ang = (2.0 * jnp.pi / n1) * ((k1[:, None] * jnp.arange(n1h, dtype=I32)[None, :]) % n1).astype(F32)
    outer_fwd = jnp.concatenate([jnp.cos(ang), -jnp.sin(ang)], axis=0) * jnp.tile(live, 2)[:, None]
    t1 = jnp.arange(n1h, dtype=I32) + n1 // 4
    ang = (2.0 * jnp.pi / n1) * ((t1[:, None] * k1[None, :]) % n1).astype(F32)
    weight = live * jnp.where((k1 == 0) | (k1 == n1h), 1.0, 2.0) / n
    outer_inv = jnp.concatenate([jnp.cos(ang), -jnp.sin(ang)], axis=1) * jnp.tile(weight, 2)[None, :]
    k2 = jnp.arange(n2, dtype=I32)
    t2 = jnp.arange(n2, dtype=I32)
    phase = (t2[None, None, :] * (k2[None, :, None] * n1 + k1[:, None, None])) % n
    ang = (2.0 * jnp.pi / n) * phase.astype(F32)
    fr, fi = jnp.cos(ang), -jnp.sin(ang)
    inner = jnp.concatenate([jnp.concatenate([fr, -fi], axis=2),
                             jnp.concatenate([fi, fr], axis=2)], axis=1)
    return _split_bf16(outer_fwd), _split_bf16(outer_inv), _split_bf16(inner), _split_bf16(jnp.swapaxes(inner, 1, 2))


def _split_bf16(x):
    hi = x.astype(BF16)
    return hi, (x - hi.astype(F32)).astype(BF16)


def _dot_split(w, a):
    w_hi, w_lo = w
    a_hi, a_lo = _split_bf16(a)
    dot = functools.partial(jnp.dot, preferred_element_type=F32)
    return dot(w_hi, a_hi) + (dot(w_hi, a_lo) + dot(w_lo, a_hi))


def _dot_split_rhs(a, w):
    w_hi, w_lo = w
    a_hi, a_lo = _split_bf16(a)
    dot = functools.partial(jnp.dot, preferred_element_type=F32)
    return dot(a_hi, w_hi) + (dot(a_lo, w_hi) + dot(a_hi, w_lo))


def _every(ref, s, n):
    return ref[pl.ds(s, n, stride=FFT_ROWS), :]


def _fft_outer_body(f_ref, fl_ref, x_ref, o_ref):
    n1h = x_ref.shape[0]
    rows = f_ref.shape[0]
    x2 = x_ref.reshape(n1h * FFT_ROWS, LANES)
    o2 = o_ref.reshape(rows * FFT_ROWS, LANES)
    x = jnp.concatenate([_every(x2, s, n1h) for s in range(FFT_ROWS)], axis=1)
    r = _dot_split((f_ref[...], fl_ref[...]), x)
    for s in range(FFT_ROWS):
        o2[pl.ds(s, rows, stride=FFT_ROWS), :] = r[:, s * LANES:(s + 1) * LANES]


def fft_outer(x4, table):
    batch, n1h, n2, ch = x4.shape
    nk = table[0].shape[0] // 2
    tab = pl.BlockSpec((2 * nk, n1h), lambda b, c, l: (0, 0))
    return pl.pallas_call(
        _fft_outer_body,
        grid=(batch, n2 // FFT_ROWS, ch // LANES),
        in_specs=[tab, tab, pl.BlockSpec((None, n1h, FFT_ROWS, LANES), lambda b, c, l: (b, 0, c, l))],
        out_specs=pl.BlockSpec((None, 2, nk, FFT_ROWS, LANES), lambda b, c, l: (b, 0, 0, c, l)),
        out_shape=_sds((batch, 2, nk, n2, ch)),
        compiler_params=_cp("parallel", "parallel", "parallel"),
        name="fft_outer",
    )(*table, x4)


def _fft_filter_body(mh_ref, ml_ref, s_ref, a_ref, o_ref):
    n2 = FFT_N2
    a = a_ref[0, :, 0].reshape(2 * n2, HY_WIDTH)
    x = _dot_split((mh_ref[0], ml_ref[0]), a) * (1.0 / s_ref[...])
    o_ref[:, 0] = x.reshape(2, n2, HY_WIDTH)


def fft_filter(a5, inner, s):
    _, _, n1, n2, ch = a5.shape
    return pl.pallas_call(
        _fft_filter_body,
        grid=(n1, ch // HY_WIDTH),
        in_specs=[pl.BlockSpec((1, 2 * n2, 2 * n2), lambda k, c: (k, 0, 0)),
                  pl.BlockSpec((1, 2 * n2, 2 * n2), lambda k, c: (k, 0, 0)),
                  pl.BlockSpec((1, HY_WIDTH), lambda k, c: (0, c)),
                  pl.BlockSpec((1, 2, 1, n2, HY_WIDTH), lambda k, c: (0, 0, k, 0, c))],
        out_specs=pl.BlockSpec((2, 1, n2, HY_WIDTH), lambda k, c: (0, k, 0, c)),
        out_shape=_sds((2, n1, n2, ch)),
        compiler_params=_cp("parallel", "parallel"),
        name="fft_filter",
    )(*inner, s, a5)


def _fft_mid_body(mh_ref, ml_ref, th_ref, tl_ref, h_ref, a_ref, o_ref, *, nb):
    n2 = FFT_N2
    m = (mh_ref[0], ml_ref[0])
    mt = (th_ref[0], tl_ref[0])
    hr = h_ref[0, 0]
    hi = h_ref[1, 0]
    for b in range(nb):
        x = _dot_split(m, a_ref[b, :, 0].reshape(2 * n2, HY_WIDTH))
        xr, xi = x[:n2], x[n2:]
        y = jnp.concatenate([xr * hr - xi * hi, xr * hi + xi * hr], axis=0)
        o_ref[b, :, 0] = _dot_split(mt, y).reshape(2, n2, HY_WIDTH)


def fft_mid(a5, inner, inner_t, hf, order):
    nb, _, n1, n2, ch = a5.shape
    mat = pl.BlockSpec((1, 2 * n2, 2 * n2), lambda k: (k, 0, 0))
    blk = pl.BlockSpec((nb, 2, 1, n2, ch), lambda k: (0, 0, k, 0, 0))
    return pl.pallas_call(
        functools.partial(_fft_mid_body, nb=nb),
        grid=(n1,),
        in_specs=[mat, mat, mat, mat, pl.BlockSpec((2, 1, n2, ch), lambda k: (0, k, 0, order)), blk],
        out_specs=blk,
        out_shape=_sds(a5.shape),
        compiler_params=_cp("parallel"),
        name="fft_mid",
    )(*inner, *inner_t, hf, a5)


def _fft_inv_body(g_ref, gl_ref, b_ref, gate_ref, y_ref, fb_ref, o_ref):
    n1h, rows = g_ref.shape
    b2 = b_ref.reshape(rows * FFT_ROWS, LANES)
    gate2 = gate_ref.reshape(n1h * FFT_ROWS, LANES)
    y2 = y_ref.reshape(n1h * FFT_ROWS, LANES)
    o2 = o_ref.reshape(n1h * FFT_ROWS, LANES)
    bm = jnp.concatenate([_every(b2, s, rows) for s in range(FFT_ROWS)], axis=1)
    c = _dot_split((g_ref[...], gl_ref[...]), bm)
    fb = fb_ref[...]
    for s in range(FFT_ROWS):
        conv = c[:, s * LANES:(s + 1) * LANES]
        o2[pl.ds(s, n1h, stride=FFT_ROWS), :] = _every(gate2, s, n1h) * (conv + fb * _every(y2, s, n1h))


def fft_inv_gate(b5, table, gate4, y4, fbias):
    batch, _, nk, n2, ch = b5.shape
    n1h = table[0].shape[0]
    blk = pl.BlockSpec((None, n1h, FFT_ROWS, LANES), lambda b, c, l: (b, 0, c, l))
    tab = pl.BlockSpec((n1h, 2 * nk), lambda b, c, l: (0, 0))
    return pl.pallas_call(
        _fft_inv_body,
        grid=(batch, n2 // FFT_ROWS, ch // LANES),
        in_specs=[tab, tab,
                  pl.BlockSpec((None, 2, nk, FFT_ROWS, LANES), lambda b, c, l: (b, 0, 0, c, l)), blk, blk,
                  pl.BlockSpec((1, LANES), lambda b, c, l: (0, l))],
        out_specs=blk,
        out_shape=_sds((batch, n1h, n2, ch)),
        compiler_params=_cp("parallel", "parallel", "parallel"),
        name="fft_inv_gate",
    )(*table, b5, gate4, y4, fbias.reshape(1, ch))


def hyena_spectrum(seq, tabs, w1, b1, f1, w2, b2, f2, w3, decay):
    outer_fwd, _, inner, _ = tabs
    n1h = outer_fwd[0].shape[1]
    hw = HY_ORDER * HY_WIDTH
    h, s = hyena_filter(seq, w1, b1, f1, w2, b2, f2, w3, decay)
    return fft_filter(fft_outer(h.reshape(1, n1h, FFT_N2, hw), outer_fwd), inner, s)


def hyena_mixer(z_hy, conv_w, conv_b, hf, f_bias, tabs, batch, seq):
    outer_fwd, outer_inv, inner, inner_t = tabs
    n1h = outer_fwd[0].shape[1]
    shape4 = (batch, n1h, FFT_N2, HY_WIDTH)
    v, g0, g1 = hyena_conv3(z_hy, conv_w, conv_b, batch, seq)
    y = v.reshape(shape4)
    for order, gate in enumerate((g0, g1)):
        bm = fft_mid(fft_outer(y, outer_fwd), inner, inner_t, hf, order)
        y = fft_inv_gate(bm, outer_inv, gate.reshape(shape4), y, f_bias[order])
    return y.reshape(batch * seq, HY_WIDTH)


def _mix_body(lam_ref, h_ref, s5_ref, gqa_ref, diff_ref, hy_ref, gs5, ggqa, gsub, ghy, w_ref, o_ref, *, keep):
    lam = lam_ref[0]
    a = _rms(s5_ref[...], gs5[...])
    b = _rms(jnp.concatenate([gqa_ref[0, h] for h in range(GQA_HEADS)], axis=1), ggqa[...])
    c = jnp.concatenate(
        [_rms(diff_ref[0, 2 * h] - lam * diff_ref[0, 2 * h + 1], gsub[...]) * keep for h in range(DIFF_HEADS)],
        axis=1)
    d = _rms(hy_ref[...], ghy[...])
    mixed = jnp.concatenate([a, b, c, d], axis=1).astype(BF16)
    o_ref[...] = h_ref[...] + jnp.dot(mixed, w_ref[...], preferred_element_type=F32)


def mix_out(lam, h, y_s5, o_gqa, o_diff, y_hy, gs5, ggqa, gsub, ghy, w_out, batch, seq, lambda_init):
    tm = min(ROW_TILE, seq)
    nb = seq // tm
    row = lambda w: pl.BlockSpec((tm, w), lambda b, i: (b * nb + i, 0))
    vec = lambda w: pl.BlockSpec((1, w), lambda b, i: (0, 0))
    heads = lambda n: pl.BlockSpec((1, n, tm, HEAD_DIM), lambda b, i: (b, 0, i, 0))
    time_major = pl.BlockSpec((tm, GROUP_WIDTH), lambda b, i: (i, b))
    return pl.pallas_call(
        functools.partial(_mix_body, keep=1.0 - lambda_init),
        grid=(batch, nb),
        in_specs=[pl.BlockSpec(memory_space=pltpu.SMEM), row(D_MODEL), time_major, heads(GQA_HEADS),
                  heads(2 * DIFF_HEADS), row(GROUP_WIDTH), vec(GROUP_WIDTH), vec(GROUP_WIDTH), vec(HEAD_DIM),
                  vec(GROUP_WIDTH), pl.BlockSpec((D_MODEL, D_MODEL), lambda b, i: (0, 0))],
        out_specs=row(D_MODEL),
        out_shape=_sds((batch * seq, D_MODEL)),
        compiler_params=_cp("parallel", "parallel"),
        name="mix_out",
    )(lam, h, y_s5, o_gqa, o_diff, y_hy, gs5, ggqa, gsub, ghy, w_out)


def _router_body(h_ref, g_ref, w_ref, wl_ref, b_ref, xn_ref, ti_ref, tw_ref, cnt_ref):
    xn = _rms(h_ref[...], g_ref[...])
    xn_ref[...] = xn
    logits = _dot_split_rhs(xn, (w_ref[...], wl_ref[...])) + b_ref[...]
    lane = lax.broadcasted_iota(I32, logits.shape, 1)
    vals = logits
    tops, idxs = [], []
    hot = jnp.zeros(logits.shape, F32)
    for _ in range(TOP_K):
        m = jnp.max(vals, axis=-1, keepdims=True)
        idx = jnp.min(jnp.where(vals == m, lane, N_EXPERTS), axis=-1, keepdims=True)
        sel = lane == idx
        tops.append(m)
        idxs.append(idx)
        hot = hot + sel.astype(F32)
        vals = jnp.where(sel, -jnp.inf, vals)
    es = [jnp.exp(t - tops[0]) for t in tops]
    den = es[0] + es[1] + es[2] + es[3]
    ti_ref[...] = jnp.concatenate(idxs, axis=1)
    tw_ref[...] = jnp.concatenate([e / den for e in es], axis=1)

    @pl.when(pl.program_id(0) == 0)
    def _():
        cnt_ref[...] = jnp.zeros_like(cnt_ref)

    cnt_ref[...] += jnp.sum(hot, axis=0, keepdims=True)


def moe_router(h, g, w_router, b_router):
    t = h.shape[0]
    tm = min(ROW_TILE, t)
    const = lambda i: (0, 0)
    return pl.pallas_call(
        _router_body,
        grid=(t // tm,),
        in_specs=[pl.BlockSpec((tm, D_MODEL), lambda i: (i, 0)), pl.BlockSpec((1, D_MODEL), const),
                  pl.BlockSpec((D_MODEL, N_EXPERTS), const), pl.BlockSpec((D_MODEL, N_EXPERTS), const),
                  pl.BlockSpec((1, N_EXPERTS), const)],
        out_specs=[pl.BlockSpec((tm, D_MODEL), lambda i: (i, 0)), pl.BlockSpec((tm, TOP_K), lambda i: (i, 0)),
                   pl.BlockSpec((tm, TOP_K), lambda i: (i, 0)), pl.BlockSpec((1, N_EXPERTS), const)],
        out_shape=[_sds((t, D_MODEL)), _sds((t, TOP_K), I32), _sds((t, TOP_K)), _sds((1, N_EXPERTS))],
        compiler_params=_cp("arbitrary"),
        name="moe_router",
    )(h, g, *w_router, b_router)


def _rank_body(ti_ref, off_ref, pos_ref, stage_ref, cnt_ref, carry_ref):
    @pl.when(pl.program_id(0) == 0)
    def _():
        carry_ref[...] = jnp.zeros_like(carry_ref)

    ti = ti_ref[...]
    tm = ti.shape[0]
    lane = lax.broadcasted_iota(I32, (tm, N_EXPERTS), 1)
    hots = [lane == ti[:, k:k + 1] for k in range(TOP_K)]
    hot = sum(h.astype(F32) for h in hots)
    r = lax.broadcasted_iota(I32, (tm, tm), 0)
    c = lax.broadcasted_iota(I32, (tm, tm), 1)
    below = jnp.where(r > c, 1.0, 0.0).astype(BF16)
    local = jnp.dot(below, hot.astype(BF16), preferred_element_type=F32)
    before = local + carry_ref[...] + off_ref[...]
    pick = lambda table: jnp.concatenate(
        [jnp.sum(jnp.where(h, table, 0.0), axis=-1, keepdims=True) for h in hots], axis=1).astype(I32)
    pos_ref[...] = pick(before)
    count = jnp.sum(hot, axis=0, keepdims=True)
    er = lax.broadcasted_iota(I32, (N_EXPERTS, N_EXPERTS), 0)
    ec = lax.broadcasted_iota(I32, (N_EXPERTS, N_EXPERTS), 1)
    earlier = jnp.where(er < ec, 1.0, 0.0)
    chunks = jnp.ceil(count * (1.0 / GATHER_CHUNK))
    start = jnp.dot(chunks, earlier, precision=HI, preferred_element_type=F32) * GATHER_CHUNK
    stage_ref[...] = pick(local + start)
    cnt_ref[...] = jnp.broadcast_to(count, cnt_ref.shape)
    carry_ref[...] += count


def moe_rank(top_i, offsets):
    t = top_i.shape[0]
    tm = min(MOE_TOK_TILE, t)
    pairs = pl.BlockSpec((tm, TOP_K), lambda i: (i, 0))
    pos, stage, cnt = pl.pallas_call(
        _rank_body,
        grid=(t // tm,),
        in_specs=[pairs, pl.BlockSpec((1, N_EXPERTS), lambda i: (0, 0))],
        out_specs=[pairs, pairs, pl.BlockSpec((SUBLANES, N_EXPERTS), lambda i: (i, 0))],
        out_shape=[_sds((t, TOP_K), I32), _sds((t, TOP_K), I32), _sds((t // tm * SUBLANES, N_EXPERTS))],
        scratch_shapes=[pltpu.VMEM((1, N_EXPERTS), F32)],
        compiler_params=_cp("arbitrary"),
        name="moe_rank",
    )(top_i, offsets)
    return pos, stage, cnt[::SUBLANES]


def _row_copy(src, dst, sem):
    return pltpu.make_async_copy(src, dst, sem)


def _pad_zero_body(lt_ref, o_ref):
    del lt_ref
    o_ref[...] = jnp.zeros_like(o_ref)


def moe_pad_tiles(last_tile, n_slots):
    rows = MOE_TILE * ROW_TILES
    return pl.pallas_call(
        _pad_zero_body,
        grid_spec=pltpu.PrefetchScalarGridSpec(
            num_scalar_prefetch=1, grid=(N_EXPERTS,), in_specs=[],
            out_specs=pl.BlockSpec((rows, LANES), lambda e, lt: (lt[e], 0))),
        out_shape=_sds((n_slots * ROW_TILES, LANES)),
        compiler_params=_cp("arbitrary"),
        name="moe_pad_tiles",
    )(last_tile)


def _slot_rows(ref, slot):
    return ref.at[pl.ds(pl.multiple_of(slot * ROW_TILES, ROW_TILES), ROW_TILES)]


def _scatter_body(pos_ref, x_ref, padded_hbm, xs_hbm, buf, sem, *, tm):
    del padded_hbm
    x = x_ref[...]
    for s in range(ROW_TILES):
        buf[pl.ds(s, tm, stride=ROW_TILES), :] = x[:, s * LANES:(s + 1) * LANES]

    def start(r, c):
        for k in range(TOP_K):
            _row_copy(_slot_rows(buf, r), _slot_rows(xs_hbm, pos_ref[r * TOP_K + k]), sem).start()
        return c

    lax.fori_loop(0, tm, start, 0)
    for _ in range(TOP_K):
        _row_copy(buf, xs_hbm.at[pl.ds(0, tm * ROW_TILES)], sem).wait()


def moe_scatter(pos_flat, x, padded):
    t = x.shape[0]
    tm = min(MOE_TOK_TILE, t)
    return pl.pallas_call(
        functools.partial(_scatter_body, tm=tm),
        grid=(t // tm,),
        in_specs=[pl.BlockSpec((tm * TOP_K,), lambda i: (i,), memory_space=pltpu.SMEM),
                  pl.BlockSpec((tm, D_MODEL), lambda i: (i, 0)), pl.BlockSpec(memory_space=pl.ANY)],
        out_specs=pl.BlockSpec(memory_space=pl.ANY),
        out_shape=_sds(padded.shape),
        scratch_shapes=[pltpu.VMEM((tm * ROW_TILES, LANES), F32), pltpu.SemaphoreType.DMA(())],
        input_output_aliases={2: 0},
        compiler_params=_cp("arbitrary"),
        name="moe_scatter",
    )(pos_flat, x, padded)


def _experts_body(te_ref, nv_ref, x_ref, wgu_ref, bgu_ref, wd_ref, bd_ref, o_ref):
    del te_ref
    j = pl.program_id(0)
    dff = wd_ref.shape[1]

    @pl.when(j < nv_ref[0])
    def _():
        tm = x_ref.shape[0] // ROW_TILES
        x = jnp.concatenate([x_ref[pl.ds(s, tm, stride=ROW_TILES), :] for s in range(ROW_TILES)],
                            axis=1).astype(BF16)
        hgu = jnp.dot(x, wgu_ref[0], preferred_element_type=F32) + bgu_ref[0]
        glu = jnp.minimum(hgu[:, :dff], SWIGLU_LIMIT)
        lin = jnp.clip(hgu[:, dff:], -SWIGLU_LIMIT, SWIGLU_LIMIT)
        act = glu * jax.nn.sigmoid(SWIGLU_ALPHA * glu) * (lin + 1.0)
        y = jnp.dot(act.astype(BF16), wd_ref[0], preferred_element_type=F32) + bd_ref[0]
        for s in range(ROW_TILES):
            o_ref[pl.ds(s, tm, stride=ROW_TILES), :] = y[:, s * LANES:(s + 1) * LANES]

    @pl.when(j >= nv_ref[0])
    def _():
        o_ref[...] = jnp.zeros_like(o_ref)


def moe_experts(tile_expert, n_valid, xs, w_gu, b_gu, w_down, b_down):
    n_slots = xs.shape[0] // ROW_TILES
    tm = MOE_TILE
    dff = w_down.shape[1]
    blk = pl.BlockSpec((tm * ROW_TILES, LANES), lambda j, te, nv: (j, 0))
    return pl.pallas_call(
        _experts_body,
        grid_spec=pltpu.PrefetchScalarGridSpec(
            num_scalar_prefetch=2,
            grid=(n_slots // tm,),
            in_specs=[pl.BlockSpec((tm * ROW_TILES, LANES), lambda j, te, nv: (jnp.minimum(j, nv[0] - 1), 0)),
                      pl.BlockSpec((1, D_MODEL, 2 * dff), lambda j, te, nv: (te[j], 0, 0)),
                      pl.BlockSpec((1, 1, 2 * dff), lambda j, te, nv: (te[j], 0, 0)),
                      pl.BlockSpec((1, dff, D_MODEL), lambda j, te, nv: (te[j], 0, 0)),
                      pl.BlockSpec((1, 1, D_MODEL), lambda j, te, nv: (te[j], 0, 0))],
            out_specs=blk),
        out_shape=_sds((n_slots * ROW_TILES, LANES)),
        compiler_params=_cp("arbitrary"),
        name="moe_experts",
    )(tile_expert, n_valid, xs, w_gu, b_gu, w_down, b_down)


def _combine_body(slot_ref, w_ref, tab_ref, ys_hbm, o_ref, stage, mixed, sem, *, tm):
    chunk_rows = GATHER_CHUNK * ROW_TILES

    def chunk_copy(src_slot, dst_slot):
        return _row_copy(ys_hbm.at[pl.ds(pl.multiple_of(src_slot * ROW_TILES, ROW_TILES), chunk_rows)],
                         stage.at[pl.ds(pl.multiple_of(dst_slot * ROW_TILES, ROW_TILES), chunk_rows)], sem)

    def per_expert(e, c):
        base, chunks, start = tab_ref[e], tab_ref[N_EXPERTS + e], tab_ref[2 * N_EXPERTS + e]

        def issue(j, c2):
            chunk_copy(base + j * GATHER_CHUNK, start + j * GATHER_CHUNK).start()
            return c2

        return lax.fori_loop(0, chunks, issue, c)

    lax.fori_loop(0, N_EXPERTS, per_expert, 0)

    def wait(j, c):
        chunk_copy(0, 0).wait()
        return c

    lax.fori_loop(0, tab_ref[3 * N_EXPERTS], wait, 0)

    def mix(q, c):
        for u in range(MIX_UNROLL):
            r = q * MIX_UNROLL + u
            acc = None
            for k in range(TOP_K):
                p = r * TOP_K + k
                term = w_ref[p] * stage[pl.ds(pl.multiple_of(slot_ref[p] * ROW_TILES, ROW_TILES), ROW_TILES), :]
                acc = term if acc is None else acc + term
            mixed[pl.ds(pl.multiple_of(r * ROW_TILES, ROW_TILES), ROW_TILES), :] = acc
        return c

    lax.fori_loop(0, tm // MIX_UNROLL, mix, 0)
    o_ref[...] = jnp.concatenate([mixed[pl.ds(s, tm, stride=ROW_TILES), :] for s in range(ROW_TILES)], axis=1)


def moe_combine(stage_flat, w_flat, table, ys, t):
    tm = min(MOE_TOK_TILE, t)
    smem = pl.BlockSpec((tm * TOP_K,), lambda i: (i,), memory_space=pltpu.SMEM)
    stage_slots = tm * TOP_K + N_EXPERTS * GATHER_CHUNK
    return pl.pallas_call(
        functools.partial(_combine_body, tm=tm),
        grid=(t // tm,),
        in_specs=[smem, smem, pl.BlockSpec((LANES,), lambda i: (i,), memory_space=pltpu.SMEM),
                  pl.BlockSpec(memory_space=pl.ANY)],
        out_specs=pl.BlockSpec((tm, D_MODEL), lambda i: (i, 0)),
        out_shape=_sds((t, D_MODEL)),
        scratch_shapes=[pltpu.VMEM((stage_slots * ROW_TILES, LANES), F32),
                        pltpu.VMEM((tm * ROW_TILES, LANES), F32), pltpu.SemaphoreType.DMA(())],
        compiler_params=_cp("arbitrary"),
        name="moe_combine",
    )(stage_flat, w_flat, table, ys)


def gather_table(cnt_tile, offsets):
    base = offsets + jnp.cumsum(cnt_tile, axis=0) - cnt_tile
    chunks = jnp.ceil(cnt_tile / GATHER_CHUNK)
    start = (jnp.cumsum(chunks, axis=1) - chunks) * GATHER_CHUNK
    total = jnp.sum(chunks, axis=1, keepdims=True)
    pad = jnp.zeros((cnt_tile.shape[0], LANES - 3 * N_EXPERTS - 1), F32)
    return jnp.concatenate([base, chunks, start, total, pad], axis=1).astype(I32).reshape(-1)


def moe(h, g, w_router, b_router, w_gu, b_gu, w_down, b_down):
    t = h.shape[0]
    xn, top_i, top_w, counts = moe_router(h, g, w_router, b_router)
    n_tiles = (t * TOP_K) // MOE_TILE + N_EXPERTS + 1
    padded = jnp.ceil(counts[0] / MOE_TILE) * MOE_TILE
    ends = jnp.cumsum(padded)
    offsets = (ends - padded)[None]
    tile_start = jnp.arange(n_tiles, dtype=F32) * MOE_TILE
    tile_expert = jnp.minimum(jnp.sum(ends[None, :] <= tile_start[:, None], axis=1), N_EXPERTS - 1).astype(I32)
    n_valid = (ends[-1:] / MOE_TILE).astype(I32)
    own_last = jnp.where(padded > 0, ends / MOE_TILE - 1, -1.0)
    filled = lax.cummax(own_last, axis=0)
    first = jnp.min(jnp.where(padded > 0, own_last, jnp.inf))
    last_tile = jnp.where(filled < 0, first, filled).astype(I32)
    pos, stage, cnt_tile = moe_rank(top_i, offsets)
    xs = moe_scatter(pos.reshape(t * TOP_K), xn, moe_pad_tiles(last_tile, n_tiles * MOE_TILE))
    ys = moe_experts(tile_expert, n_valid, xs, w_gu, b_gu, w_down, b_down)
    flat = lambda a: a.reshape(t * TOP_K)
    return moe_combine(flat(stage), flat(top_w), gather_table(cnt_tile, offsets), ys, t)


def _ple_body(h_ref, m_ref, p_ref, wp_ref, wg_ref, o_ref):
    h = h_ref[...] + m_ref[...]
    e = jnp.dot(p_ref[...].astype(BF16), wp_ref[...], preferred_element_type=F32)
    gate = jnp.dot(_rms(h).astype(BF16), wg_ref[...], preferred_element_type=F32)
    o_ref[...] = h + e * jax.nn.sigmoid(gate)


def ple(h, moe_out, p, w_ple, w_gate):
    t = h.shape[0]
    tm = min(ROW_TILE, t)
    row = pl.BlockSpec((tm, D_MODEL), lambda i: (i, 0))
    return pl.pallas_call(
        _ple_body,
        grid=(t // tm,),
        in_specs=[row, row, pl.BlockSpec((tm, PLE_DIM), lambda i: (i, 0)),
                  pl.BlockSpec((PLE_DIM, D_MODEL), lambda i: (0, 0)),
                  pl.BlockSpec((D_MODEL, D_MODEL), lambda i: (0, 0))],
        out_specs=row,
        out_shape=_sds((t, D_MODEL)),
        compiler_params=_cp("parallel"),
        name="ple",
    )(h, moe_out, p, w_ple, w_gate)


def _deinterleave(n):
    return np.concatenate([np.arange(0, n, 2), np.arange(1, n, 2)])


def _halves(w):
    pairs = w.reshape(*w.shape[:-1], w.shape[-1] // 2, 2)
    return jnp.concatenate([pairs[..., 0], pairs[..., 1]], axis=-1)


def prepare_layer(i, prm):
    pair = _deinterleave(HEAD_DIM)
    cols = np.arange(IN_COLS)
    for h in range(GQA_HEADS + GQA_KV_HEADS):
        lo = OFF_GQA + h * HEAD_DIM
        cols[lo:lo + HEAD_DIM] = lo + pair
    lam = (jnp.exp(jnp.sum(prm["diff_lam_q1"][i] * prm["diff_lam_k1"][i]))
           - jnp.exp(jnp.sum(prm["diff_lam_q2"][i] * prm["diff_lam_k2"][i])) + _lambda_init(i))
    return dict(
        norm1_g=prm["norm1_g"][i][None],
        w_in=prm["w_in"][i][:, cols].astype(BF16),
        s5=s5_tables(prm["s5_lam_re"][i], prm["s5_lam_im"][i], prm["s5_log_dt"][i], prm["s5_b_re"][i],
                     prm["s5_b_im"][i], prm["s5_c_re"][i], prm["s5_c_im"][i]),
        s5_d=prm["s5_d"][i][None],
        s5_w_glu=prm["s5_w_glu"][i].astype(BF16),
        s5_b_glu=prm["s5_b_glu"][i][None],
        gqa_q_g=prm["gqa_q_g"][i][pair][None],
        gqa_k_g=prm["gqa_k_g"][i][pair][None],
        diff_q_g=prm["diff_q_g"][i][None],
        diff_k_g=prm["diff_k_g"][i][None],
        diff_lam=lam.reshape(1).astype(F32),
        diff_subln_g=prm["diff_subln_g"][i][None],
        out_g_s5=prm["out_g_s5"][i][None],
        out_g_gqa=prm["out_g_gqa"][i][None],
        out_g_hy=prm["out_g_hy"][i][None],
        w_out=prm["w_out"][i].astype(BF16),
        norm2_g=prm["norm2_g"][i][None],
        w_router=_split_bf16(prm["w_router"][i]),
        b_router=prm["b_router"][i][None],
        w_gu=_halves(prm["w_gu"][i]).astype(BF16),
        b_gu=_halves(prm["b_gu"][i])[:, None, :],
        w_down=prm["w_down"][i].astype(BF16),
        b_down=prm["b_down"][i][:, None, :],
        w_ple=prm["w_ple"][i].astype(BF16),
        w_ple_gate=prm["w_ple_gate"][i].astype(BF16),
    )


def _lambda_init(i):
    return 0.8 - 0.6 * math.exp(-0.3 * i)


def run_trunk(x, p, prm, layers):
    batch, seq, _ = x.shape
    t = batch * seq
    cos_t, sin_t = rope_tables(seq)
    tabs = dft_tables(seq)
    bias = diff_bias_tiles(prm["rel_bias"], min(ATTN_UNIT, seq))
    h = x.reshape(t, D_MODEL)
    for i, lw in enumerate(layers):
        z_s5, z_gqa, z_diff, z_hy = in_proj(h, lw["norm1_g"], lw["w_in"], batch, seq)
        y_s5 = s5_mixer(z_s5, lw["s5"], lw["s5_d"], lw["s5_w_glu"], lw["s5_b_glu"], batch, seq)
        rep = GQA_HEADS // GQA_KV_HEADS
        q, k, v = attn_prep(z_gqa, lw["gqa_q_g"], lw["gqa_k_g"], cos_t, sin_t, batch, seq,
                            GQA_HEADS, GQA_KV_HEADS, GQA_KV_HEADS, True, lambda hq: hq // rep)
        o_gqa = flash_attention(q, k, v, lambda hq: hq // rep)
        q, k, v = attn_prep(z_diff, lw["diff_q_g"], lw["diff_k_g"], cos_t, sin_t, batch, seq,
                            2 * DIFF_HEADS, 2 * DIFF_HEADS, DIFF_HEADS, False, lambda hq: hq)
        o_diff = flash_attention(q, k, v, lambda hq: hq // 2, bias)
        hf = hyena_spectrum(seq, tabs, prm["hy_w1"][i], prm["hy_b1"][i], prm["hy_freq1"][i], prm["hy_w2"][i],
                            prm["hy_b2"][i], prm["hy_freq2"][i], prm["hy_w3"][i], prm["hy_decay"][i])
        y_hy = hyena_mixer(z_hy, prm["hy_conv_w"][i], prm["hy_conv_b"][i][None], hf, prm["hy_bias"][i],
                           tabs, batch, seq)
        h = mix_out(lw["diff_lam"], h, y_s5, o_gqa, o_diff, y_hy, lw["out_g_s5"], lw["out_g_gqa"],
                    lw["diff_subln_g"], lw["out_g_hy"], lw["w_out"], batch, seq, _lambda_init(i))
        m = moe(h, lw["norm2_g"], lw["w_router"], lw["b_router"], lw["w_gu"], lw["b_gu"], lw["w_down"],
                lw["b_down"])
        h = ple(h, m, p[i].reshape(t, PLE_DIM), lw["w_ple"], lw["w_ple_gate"])
    return h.reshape(batch, seq, D_MODEL)


def kernel(x_prompt, x_sample, p_prompt, p_sample, rel_bias, norm1_g, w_in, s5_lam_re, s5_lam_im, s5_log_dt,
           s5_b_re, s5_b_im, s5_c_re, s5_c_im, s5_d, s5_w_glu, s5_b_glu, gqa_q_g, gqa_k_g, diff_q_g, diff_k_g,
           diff_lam_q1, diff_lam_k1, diff_lam_q2, diff_lam_k2, diff_subln_g, hy_conv_w, hy_conv_b, hy_w1, hy_b1,
           hy_freq1, hy_w2, hy_b2, hy_freq2, hy_w3, hy_decay, hy_bias, out_g_s5, out_g_gqa, out_g_hy, w_out,
           norm2_g, w_router, b_router, w_gu, b_gu, w_down, b_down, w_ple, w_ple_gate):
    prm = dict(rel_bias=rel_bias, norm1_g=norm1_g, w_in=w_in, s5_lam_re=s5_lam_re, s5_lam_im=s5_lam_im,
               s5_log_dt=s5_log_dt, s5_b_re=s5_b_re, s5_b_im=s5_b_im, s5_c_re=s5_c_re, s5_c_im=s5_c_im,
               s5_d=s5_d, s5_w_glu=s5_w_glu, s5_b_glu=s5_b_glu, gqa_q_g=gqa_q_g, gqa_k_g=gqa_k_g,
               diff_q_g=diff_q_g, diff_k_g=diff_k_g, diff_lam_q1=diff_lam_q1, diff_lam_k1=diff_lam_k1,
               diff_lam_q2=diff_lam_q2, diff_lam_k2=diff_lam_k2, diff_subln_g=diff_subln_g,
               hy_conv_w=hy_conv_w, hy_conv_b=hy_conv_b, hy_w1=hy_w1, hy_b1=hy_b1, hy_freq1=hy_freq1,
               hy_w2=hy_w2, hy_b2=hy_b2, hy_freq2=hy_freq2, hy_w3=hy_w3, hy_decay=hy_decay, hy_bias=hy_bias,
               out_g_s5=out_g_s5, out_g_gqa=out_g_gqa, out_g_hy=out_g_hy, w_out=w_out, norm2_g=norm2_g,
               w_router=w_router, b_router=b_router, w_gu=w_gu, b_gu=b_gu, w_down=w_down, b_down=b_down,
               w_ple=w_ple, w_ple_gate=w_ple_gate)
    layers = [prepare_layer(i, prm) for i in range(DEPTH)]
    return (run_trunk(x_prompt, p_prompt, prm, layers), run_trunk(x_sample, p_sample, prm, layers))
```

```python
import functools
import math

import numpy as np
import jax
import jax.numpy as jnp
from jax import lax
from jax.experimental import pallas as pl
from jax.experimental.pallas import tpu as pltpu

F32 = jnp.float32
BF16 = jnp.bfloat16
I32 = jnp.int32
HI = lax.Precision.HIGHEST

D_MODEL = 1024
DEPTH = 2
GROUP_WIDTH = 256
HEAD_DIM = 64
S5_GROUP_DIM = 16
S5_GROUPS = 16
S5_STATE = 64
S5_NSTATE = S5_GROUPS * S5_STATE
GQA_HEADS = 4
GQA_KV_HEADS = 2
DIFF_HEADS = 4
HY_WIDTH = 256
HY_ORDER = 2
HY_BANDS = 16
HY_EMB = 2 * HY_BANDS + 1
HY_FFN = 64
N_EXPERTS = 32
TOP_K = 4
SWIGLU_LIMIT = 7.0
SWIGLU_ALPHA = 1.702
NUM_BUCKETS = 32
MAX_DISTANCE = 128
GRID_W = 64
ROPE_THETA = 10000.0
ROPE_AXIS_DIM = HEAD_DIM // 2
PLE_DIM = 256
NORM_EPS = 1e-6
ATTN_SCALE = HEAD_DIM ** -0.5
LOG2E = math.log2(math.e)

S5_COLS = GROUP_WIDTH
GQA_QW = GQA_HEADS * HEAD_DIM
GQA_KW = GQA_KV_HEADS * HEAD_DIM
GQA_COLS = GQA_QW + 2 * GQA_KW
DIFF_QW = DIFF_HEADS * 2 * HEAD_DIM
DIFF_COLS = 2 * DIFF_QW + DIFF_HEADS * HEAD_DIM
HY_COLS = (HY_ORDER + 1) * HY_WIDTH
OFF_GQA = S5_COLS
OFF_DIFF = OFF_GQA + GQA_COLS
OFF_HY = OFF_DIFF + DIFF_COLS
IN_COLS = OFF_HY + HY_COLS
IN_SPLITS = ((0, OFF_GQA), (OFF_GQA, OFF_DIFF), (OFF_DIFF, OFF_HY), (OFF_HY, IN_COLS))

LANES = 128
SUBLANES = 8
ROW_TILES = D_MODEL // LANES
FFT_N2 = 128
FFT_ROWS = 64
VMEM_LIMIT = 56 * 1024 * 1024

ROW_TILE = 1024
ATTN_Q_TILE = 2048
ATTN_K_TILE = 2048
ATTN_UNIT = 512
ATTN_SUB = 512
ATTN_SUB_PLAIN = 1024
S5_TIME_BLOCK = 256
MOE_TILE = 512
MOE_TOK_TILE = 1024
MIX_UNROLL = 8
GATHER_CHUNK = 32


def _cp(*sem):
    return pltpu.CompilerParams(dimension_semantics=sem, vmem_limit_bytes=VMEM_LIMIT)


def _sds(shape, dtype=F32):
    return jax.ShapeDtypeStruct(shape, dtype)


def _rms(x, g=None):
    y = x * lax.rsqrt(jnp.mean(x * x, axis=-1, keepdims=True) + NORM_EPS)
    return y if g is None else y * g


def _inproj_body(x_ref, g_ref, w_ref, *o_refs):
    u = _rms(x_ref[...], g_ref[...]).astype(BF16)
    for o_ref, (lo, hi) in zip(o_refs, IN_SPLITS):
        o_ref[...] = jnp.dot(u, w_ref[:, lo:hi], preferred_element_type=F32)


def in_proj(h, g, w, batch, seq):
    t = h.shape[0]
    tm = min(ROW_TILE, seq)
    nb = seq // tm
    row_major = lambda i: (i, 0)
    time_major = lambda i: (i % nb, i // nb)
    widths = [hi - lo for lo, hi in IN_SPLITS]
    return pl.pallas_call(
        _inproj_body,
        grid=(t // tm,),
        in_specs=[pl.BlockSpec((tm, D_MODEL), row_major),
                  pl.BlockSpec((1, D_MODEL), lambda i: (0, 0)),
                  pl.BlockSpec((D_MODEL, IN_COLS), lambda i: (0, 0))],
        out_specs=[pl.BlockSpec((tm, widths[0]), time_major)] + [pl.BlockSpec((tm, w_), row_major) for w_ in widths[1:]],
        out_shape=[_sds((seq, batch * widths[0]))] + [_sds((t, w_)) for w_ in widths[1:]],
        compiler_params=_cp("parallel"),
        name="in_proj",
    )(h, g, w)


def s5_tables(lam_re, lam_im, log_dt, b_re, b_im, c_re, c_im):
    lr = jnp.minimum(lam_re, -1e-4)
    li = lam_im
    dt = jnp.exp(log_dt)[..., None]
    mag = jnp.exp(lr * dt)
    ar = mag * jnp.cos(li * dt)
    ai = mag * jnp.sin(li * dt)
    den = lr * lr + li * li
    cr = ((ar - 1.0) * lr + ai * li) / den
    ci = (ai * lr - (ar - 1.0) * li) / den
    bbr = cr[..., None] * b_re - ci[..., None] * b_im
    bbi = cr[..., None] * b_im + ci[..., None] * b_re
    eye = jnp.eye(S5_GROUPS, dtype=F32)
    bb = jnp.stack([bbr, bbi], axis=1)
    wb = jnp.einsum("dpgnh,gk->dghpkn", bb, eye).reshape(2, S5_COLS, 2 * S5_NSTATE)
    cc = jnp.stack([c_re, -c_im], axis=0)
    wc = jnp.einsum("pdghn,gk->dpgnkh", cc, eye).reshape(2, 2 * S5_NSTATE, S5_COLS)
    a_re = jnp.repeat(ar.reshape(2, S5_NSTATE), 4, axis=0)
    a_im = jnp.repeat(ai.reshape(2, S5_NSTATE), 4, axis=0)
    return wb.astype(BF16), wc.astype(BF16), a_re, a_im


def _s5_body(uf_ref, ub_ref, wb_ref, wc_ref, ar_ref, ai_ref, yf_ref, yb_ref, rows_ref, bf_ref, bb_ref, xr_ref, xi_ref,
             *, tb):
    @pl.when(pl.program_id(0) == 0)
    def _():
        xr_ref[...] = jnp.zeros_like(xr_ref)
        xi_ref[...] = jnp.zeros_like(xi_ref)

    halves = range(S5_COLS // LANES)
    wide = lambda r0: jnp.concatenate([rows_ref[h, r0:r0 + tb * 4, :] for h in halves], axis=1)
    for b in range(4):
        for h in halves:
            cols = slice(b * S5_COLS + h * LANES, b * S5_COLS + (h + 1) * LANES)
            rows_ref[h, pl.ds(b, tb, stride=4), :] = uf_ref[:, cols]
            rows_ref[h, pl.ds(tb * 4 + b, tb, stride=4), :] = ub_ref[:, cols]
    bf_ref[...] = jnp.dot(wide(0).astype(BF16), wb_ref[0], preferred_element_type=F32)
    bb_ref[...] = jnp.dot(wide(tb * 4).astype(BF16), wb_ref[1], preferred_element_type=F32)
    ar = ar_ref[...]
    ai = ai_ref[...]
    low = lax.broadcasted_iota(I32, (SUBLANES, 1), 0) < 4
    half = SUBLANES // 2
    re = slice(0, S5_NSTATE)
    im = slice(S5_NSTATE, 2 * S5_NSTATE)

    def pair(m, carry):
        xr, xi = carry
        rf = pl.ds(pl.multiple_of(m * SUBLANES, SUBLANES), SUBLANES)
        rb = pl.ds(pl.multiple_of((tb // 2 - 1 - m) * SUBLANES, SUBLANES), SUBLANES)
        f_r, f_i, b_r, b_i = bf_ref[rf, re], bf_ref[rf, im], bb_ref[rb, re], bb_ref[rb, im]
        in_r = jnp.where(low, f_r, b_r)
        in_i = jnp.where(low, f_i, b_i)
        x1r = ar * xr - ai * xi + in_r
        x1i = ar * xi + ai * xr + in_i
        in_r = pltpu.roll(jnp.where(low, b_r, f_r), half, 0)
        in_i = pltpu.roll(jnp.where(low, b_i, f_i), half, 0)
        x2r = ar * x1r - ai * x1i + in_r
        x2i = ar * x1i + ai * x1r + in_i
        s2r = pltpu.roll(x2r, half, 0)
        s2i = pltpu.roll(x2i, half, 0)
        bf_ref[rf, re] = jnp.where(low, x1r, s2r)
        bf_ref[rf, im] = jnp.where(low, x1i, s2i)
        bb_ref[rb, re] = jnp.where(low, s2r, x1r)
        bb_ref[rb, im] = jnp.where(low, s2i, x1i)
        return x2r, x2i

    xr, xi = lax.fori_loop(0, tb // 2, pair, (xr_ref[...], xi_ref[...]))
    xr_ref[...] = xr
    xi_ref[...] = xi
    yf = jnp.dot(bf_ref[...].astype(BF16), wc_ref[0], preferred_element_type=F32)
    yb = jnp.dot(bb_ref[...].astype(BF16), wc_ref[1], preferred_element_type=F32)
    for h in halves:
        rows_ref[h, 0:tb * 4, :] = yf[:, h * LANES:(h + 1) * LANES]
        rows_ref[h, tb * 4:tb * 8, :] = yb[:, h * LANES:(h + 1) * LANES]
    for b in range(4):
        for h in halves:
            cols = slice(b * S5_COLS + h * LANES, b * S5_COLS + (h + 1) * LANES)
            yf_ref[:, cols] = rows_ref[h, pl.ds(b, tb, stride=4), :]
            yb_ref[:, cols] = rows_ref[h, pl.ds(tb * 4 + b, tb, stride=4), :]


def s5_scan(u, wb, wc, a_re, a_im, seq):
    tb = min(S5_TIME_BLOCK, seq)
    nb = seq // tb
    rows = tb * 4
    fwd = pl.BlockSpec((tb, 4 * S5_COLS), lambda i: (i, 0))
    bwd = pl.BlockSpec((tb, 4 * S5_COLS), lambda i: (nb - 1 - i, 0))
    return pl.pallas_call(
        functools.partial(_s5_body, tb=tb),
        grid=(nb,),
        in_specs=[fwd, bwd,
                  pl.BlockSpec((2, S5_COLS, 2 * S5_NSTATE), lambda i: (0, 0, 0)),
                  pl.BlockSpec((2, 2 * S5_NSTATE, S5_COLS), lambda i: (0, 0, 0)),
                  pl.BlockSpec((SUBLANES, S5_NSTATE), lambda i: (0, 0)),
                  pl.BlockSpec((SUBLANES, S5_NSTATE), lambda i: (0, 0))],
        out_specs=[fwd, bwd],
        out_shape=[_sds((seq, 4 * S5_COLS))] * 2,
        scratch_shapes=[pltpu.VMEM((S5_COLS // LANES, 2 * rows, LANES), F32),
                        pltpu.VMEM((rows, 2 * S5_NSTATE), F32), pltpu.VMEM((rows, 2 * S5_NSTATE), F32),
                        pltpu.VMEM((SUBLANES, S5_NSTATE), F32), pltpu.VMEM((SUBLANES, S5_NSTATE), F32)],
        compiler_params=_cp("arbitrary"),
        name="s5_scan",
    )(u, u, wb, wc, a_re, a_im)


def _s5_out_body(yf_ref, yb_ref, u_ref, d_ref, w_ref, b_ref, o_ref):
    for b in range(o_ref.shape[1] // S5_COLS):
        cols = slice(b * S5_COLS, (b + 1) * S5_COLS)
        y = jax.nn.gelu(yf_ref[:, cols] + yb_ref[:, cols] + u_ref[:, cols] * d_ref[...])
        gate = jnp.dot(y.astype(BF16), w_ref[...], preferred_element_type=F32) + b_ref[...]
        o_ref[:, cols] = y * jax.nn.sigmoid(gate)


def s5_out(yf, yb, u, d, w_glu, b_glu):
    t, width = u.shape
    tm = min(ROW_TILE // 4, t)
    row = pl.BlockSpec((tm, width), lambda i: (i, 0))
    vec = pl.BlockSpec((1, S5_COLS), lambda i: (0, 0))
    return pl.pallas_call(
        _s5_out_body,
        grid=(t // tm,),
        in_specs=[row, row, row, vec, pl.BlockSpec((S5_COLS, S5_COLS), lambda i: (0, 0)), vec],
        out_specs=row,
        out_shape=_sds((t, width)),
        compiler_params=_cp("parallel"),
        name="s5_out",
    )(yf, yb, u, d, w_glu, b_glu)


def s5_mixer(z_s5, tabs, d, w_glu, b_glu, batch, seq):
    assert batch == 4 and seq % 2 == 0, "the scan packs 4 sequences x 2 directions onto the 8 sublanes"
    yf, yb = s5_scan(z_s5, *tabs, seq)
    return s5_out(yf, yb, z_s5, d, w_glu, b_glu)


def _prep_body(z_ref, qg_ref, kg_ref, c_ref, s_ref, q_ref, k_ref, v_ref, *, nq, nk, nv, rope, k_src):
    tm = z_ref.shape[0]
    lane = lax.broadcasted_iota(I32, (tm, LANES), 1)
    low = lane < HEAD_DIM

    def norm_pair(x, g):
        sq = x * x
        ss_lo = jnp.sum(jnp.where(low, sq, 0.0), axis=-1, keepdims=True)
        ss_hi = jnp.sum(jnp.where(low, 0.0, sq), axis=-1, keepdims=True)
        y = x * lax.rsqrt(jnp.where(low, ss_lo, ss_hi) * (1.0 / HEAD_DIM) + NORM_EPS) * g
        if rope:
            half = HEAD_DIM // 2
            partner = jnp.where(lane % HEAD_DIM < half, pltpu.roll(y, LANES - half, 1), pltpu.roll(y, half, 1))
            y = y * c_ref[...] + partner * s_ref[...]
        return y

    qg = qg_ref[...]
    kg = kg_ref[...]
    for p in range(nq // 2):
        q_ref[0, p] = (norm_pair(z_ref[:, p * LANES:(p + 1) * LANES], qg) * (ATTN_SCALE * LOG2E)).astype(BF16)
    k_lo = nq * HEAD_DIM
    pairs = [norm_pair(z_ref[:, k_lo + p * LANES:k_lo + (p + 1) * LANES], kg) for p in range(nk // 2)]
    for h in range(nq):
        src = k_src(h)
        x = pairs[src // 2]
        if src % 2 != h % 2:
            x = pltpu.roll(x, HEAD_DIM, 1)
        k_ref[0, h] = jnp.where(low if h % 2 == 0 else jnp.logical_not(low), x, 0.0).astype(BF16)
    ones_col = jnp.where(lax.broadcasted_iota(I32, (tm, LANES - HEAD_DIM), 1) == 0, 1.0, 0.0)
    for h in range(nv):
        lo = (nq + nk + h) * HEAD_DIM
        v_ref[0, h] = jnp.concatenate([z_ref[:, lo:lo + HEAD_DIM], ones_col], axis=1).astype(BF16)


def attn_prep(z, qg, kg, cos_t, sin_t, batch, seq, nq, nk, nv, rope, k_src):
    width = (nq + nk + nv) * HEAD_DIM
    tm = min(ROW_TILE, seq)
    nb = seq // tm
    vec = pl.BlockSpec((1, LANES), lambda b, i: (0, 0))
    tab = pl.BlockSpec((tm, LANES), lambda b, i: (i, 0))
    heads = lambda n: pl.BlockSpec((1, n, tm, LANES), lambda b, i: (b, 0, i, 0))
    return pl.pallas_call(
        functools.partial(_prep_body, nq=nq, nk=nk, nv=nv, rope=rope, k_src=k_src),
        grid=(batch, nb),
        in_specs=[pl.BlockSpec((tm, width), lambda b, i: (b * nb + i, 0)), vec, vec, tab, tab],
        out_specs=[heads(nq // 2), heads(nq), heads(nv)],
        out_shape=[_sds((batch, nq // 2, seq, LANES), BF16), _sds((batch, nq, seq, LANES), BF16),
                   _sds((batch, nv, seq, LANES), BF16)],
        compiler_params=_cp("parallel", "parallel"),
        name="attn_prep",
    )(z, jnp.tile(qg, (1, 2)), jnp.tile(kg, (1, 2)), cos_t, sin_t)


def _flash_body(q_ref, k_ref, v_ref, *rest, unit, sub, sub_far, has_bias):
    if has_bias:
        b_ref, o_ref, m_ref, acc_ref = rest
    else:
        o_ref, m_ref, acc_ref = rest
    i = pl.program_id(2)
    j = pl.program_id(3)
    tq = q_ref.shape[2]
    tk = k_ref.shape[2]
    q_units = tq // unit
    k_units = tk // unit

    @pl.when(j == 0)
    def _():
        m_ref[...] = jnp.full_like(m_ref, -jnp.inf)
        acc_ref[...] = jnp.zeros_like(acc_ref)

    def attend(rows, offset=None, far_sel=None):
        k = k_ref[0, 0]
        v = v_ref[0, 0]
        for g in range(tq // rows):
            r0 = g * rows
            s = lax.dot_general(q_ref[0, 0, r0:r0 + rows, :], k, (((1,), (1,)), ((), ())),
                                preferred_element_type=F32)
            m_prev = m_ref[r0:r0 + rows, :]
            kinds = [None] * k_units
            if offset is not None:
                tile_of = [offset * k_units + c - r0 // unit for c in range(k_units)]
                kinds = [None if abs(d) <= 1 else (0 if d < 0 else 4) for d in tile_of]
                s = jnp.concatenate(
                    [s[:, c * unit:(c + 1) * unit] + b_ref[0, d + 2, r0 % unit:r0 % unit + rows, :]
                     if abs(d) <= 1 else s[:, c * unit:(c + 1) * unit] for c, d in enumerate(tile_of)], axis=1)
            elif far_sel is not None:
                kinds = [far_sel] * k_units
            consts = {kind: b_ref[0, kind, 0:1, 0:LANES] for kind in set(kinds) if kind is not None}
            per_unit = unit // LANES
            blocks = [s[:, b * LANES:(b + 1) * LANES] for b in range(tk // LANES)]
            m_new = m_prev
            for kind in sorted(set(kinds), key=str):
                members = [blk for b, blk in enumerate(blocks) if kinds[b // per_unit] == kind]
                bmax = members[0]
                for blk in members[1:]:
                    bmax = jnp.maximum(bmax, blk)
                kind_max = jnp.max(bmax, axis=-1, keepdims=True)
                m_new = jnp.maximum(m_new, kind_max if kind is None else kind_max + consts[kind])
            shifts = [m_new if kind is None else m_new - consts[kind] for kind in kinds]
            alpha = jnp.exp2(m_prev - m_new)
            p = jnp.concatenate([jnp.exp2(blk - shifts[b // per_unit]) for b, blk in enumerate(blocks)],
                                axis=1).astype(BF16)
            acc_ref[r0:r0 + rows, :] = alpha * acc_ref[r0:r0 + rows, :] + jnp.dot(p, v, preferred_element_type=F32)
            m_ref[r0:r0 + rows, :] = m_new

    if has_bias:
        assert q_units == k_units and sub == unit, "the near-diagonal variants assume square steps of whole units"
        pl.when(j - i >= 2)(lambda: attend(sub_far, far_sel=4))
        pl.when(j - i <= -2)(lambda: attend(sub_far, far_sel=0))
        for offset in (-1, 0, 1):
            pl.when(j - i == offset)(functools.partial(attend, sub, offset=offset))
    else:
        attend(sub)

    @pl.when(j == pl.num_programs(3) - 1)
    def _():
        acc = acc_ref[...]
        o_ref[0, 0] = acc[:, :HEAD_DIM] / acc[:, HEAD_DIM:HEAD_DIM + 1]


def flash_attention(q, k, v, v_of_q, bias=None):
    batch, nq, seq, _ = k.shape
    tq = min(ATTN_Q_TILE, seq)
    tk = min(ATTN_K_TILE, seq)
    unit = min(ATTN_UNIT, seq)
    sub = min(ATTN_SUB if bias is not None else ATTN_SUB_PLAIN, tq)
    sub_far = min(ATTN_SUB_PLAIN, tq)
    in_specs = [pl.BlockSpec((1, 1, tq, LANES), lambda b, h, i, j: (b, h // 2, i, 0)),
                pl.BlockSpec((1, 1, tk, LANES), lambda b, h, i, j: (b, h, j, 0)),
                pl.BlockSpec((1, 1, tk, LANES), lambda b, h, i, j: (b, v_of_q(h), j, 0))]
    args = [q, k, v]
    if bias is not None:
        assert bias.shape[2] == unit
        in_specs.append(pl.BlockSpec((1, 5, unit, unit), lambda b, h, i, j: (v_of_q(h), 0, 0, 0)))
        args.append(bias)
    return pl.pallas_call(
        functools.partial(_flash_body, unit=unit, sub=sub, sub_far=sub_far, has_bias=bias is not None),
        grid=(batch, nq, seq // tq, seq // tk),
        in_specs=in_specs,
        out_specs=pl.BlockSpec((1, 1, tq, HEAD_DIM), lambda b, h, i, j: (b, h, i, 0)),
        out_shape=_sds((batch, nq, seq, HEAD_DIM)),
        scratch_shapes=[pltpu.VMEM((tq, LANES), F32), pltpu.VMEM((tq, LANES), F32)],
        compiler_params=_cp("parallel", "parallel", "parallel", "arbitrary"),
        name="flash_bias" if bias is not None else "flash",
    )(*args)


def _t5_bucket_table(t):
    rel = np.arange(-(3 * t - 1), 3 * t)
    half = NUM_BUCKETS // 2
    max_exact = half // 2
    ret = np.where(rel > 0, half, 0)
    n = np.abs(rel)
    nf = np.maximum(n, 1).astype(np.float64)
    large = max_exact + (np.log(nf / max_exact) / math.log(MAX_DISTANCE / max_exact) * (half - max_exact)).astype(np.int64)
    large = np.minimum(large, half - 1)
    return ret + np.where(n < max_exact, n, large)


def diff_bias_tiles(rel_bias, t):
    assert t >= MAX_DISTANCE, "offsets of two or more tiles must lie in the saturated buckets"
    heads = rel_bias.shape[1]
    vec = rel_bias[jnp.asarray(_t5_bucket_table(t), dtype=I32)].T * LOG2E
    tiles = []
    for d in range(-2, 3):
        lo = d * t + 2 * t
        w = vec[:, lo:lo + 2 * t - 1]
        u = jnp.concatenate([w[:, t - 1:], jnp.zeros((heads, 1), F32), w[:, :t - 1]], axis=1)
        skew = jnp.tile(u, (1, t))[:, :t * (2 * t - 1)].reshape(heads, t, 2 * t - 1)
        tiles.append(skew[:, :, :t])
    return jnp.stack(tiles, axis=1)


def rope_tables(seq):
    rows = seq // GRID_W
    row = jnp.repeat(jnp.arange(rows, dtype=F32), GRID_W)
    col = jnp.tile(jnp.arange(GRID_W, dtype=F32), rows)
    freq = ROPE_THETA ** (-jnp.arange(0, ROPE_AXIS_DIM, 2, dtype=F32) / ROPE_AXIS_DIM)
    ang = jnp.concatenate([row[:, None] * freq, col[:, None] * freq], -1)
    cos, sin = jnp.cos(ang), jnp.sin(ang)
    return jnp.concatenate([cos, cos] * 2, -1), jnp.concatenate([-sin, sin] * 2, -1)


def _conv3_body(x0, x1, x2, w0, w1, w2, b0, b1, b2, o0, o1, o2):
    for x_ref, w_ref, b_ref, o_ref in ((x0, w0, b0, o0), (x1, w1, b1, o1), (x2, w2, b2, o2)):
        x = x_ref[0]
        n = x.shape[0]
        t = lax.broadcasted_iota(I32, (n, 1), 0)
        prev = jnp.where(t == 0, 0.0, pltpu.roll(x, 1, 0))
        nxt = jnp.where(t == n - 1, 0.0, pltpu.roll(x, n - 1, 0))
        o_ref[0] = w_ref[0:1] * prev + w_ref[1:2] * x + w_ref[2:3] * nxt + b_ref[...]


def hyena_conv3(z_hy, conv_w, conv_b, batch, seq):
    z = z_hy.reshape(batch, seq, HY_COLS)
    nc = HY_WIDTH // LANES
    x_spec = lambda g: pl.BlockSpec((1, seq, LANES), lambda b, c: (b, 0, g * nc + c))
    w_spec = lambda g: pl.BlockSpec((3, LANES), lambda b, c: (0, g * nc + c))
    b_spec = lambda g: pl.BlockSpec((1, LANES), lambda b, c: (0, g * nc + c))
    out = pl.BlockSpec((1, seq, LANES), lambda b, c: (b, 0, c))
    return pl.pallas_call(
        _conv3_body,
        grid=(batch, nc),
        in_specs=[x_spec(0), x_spec(1), x_spec(2), w_spec(0), w_spec(1), w_spec(2),
                  b_spec(0), b_spec(1), b_spec(2)],
        out_specs=[out, out, out],
        out_shape=[_sds((batch, seq, HY_WIDTH))] * 3,
        compiler_params=_cp("parallel", "parallel"),
        name="hyena_conv3",
    )(z, z, z, conv_w, conv_w, conv_w, conv_b, conv_b, conv_b)


def _filt_body(feat_ref, dist_ref, w1, b1, f1, w2, b2, f2, w3, dec, h_ref, s_ref):
    dot = functools.partial(jnp.dot, precision=HI, preferred_element_type=F32)
    hid = jnp.sin(f1[...] * (dot(feat_ref[...], w1[...]) + b1[...]))
    hid = jnp.sin(f2[...] * (dot(hid, w2[...]) + b2[...]))
    h = dot(hid, w3[...]) * jnp.exp(-dist_ref[...] * jnp.abs(dec[...]))
    h_ref[...] = h

    @pl.when(pl.program_id(0) == 0)
    def _():
        s_ref[...] = jnp.zeros_like(s_ref)

    s_ref[...] += jnp.sum(jnp.abs(h), axis=0, keepdims=True)


def hyena_filter(seq, w1, b1, f1, w2, b2, f2, w3, decay):
    j = jnp.arange(seq, dtype=F32)
    tt = j / seq
    ang = 2.0 * jnp.pi * tt[:, None] * jnp.arange(1, HY_BANDS + 1, dtype=F32)
    feat = jnp.concatenate([tt[:, None], jnp.cos(ang), jnp.sin(ang)], -1)
    feat = jnp.pad(feat, ((0, 0), (0, LANES - HY_EMB)))
    half = seq // 2
    dist = (jnp.abs(j - half) / half)[:, None]
    w1p = jnp.pad(w1, ((0, LANES - HY_EMB), (0, 0)))
    tm = min(1024, seq)
    hw = HY_ORDER * HY_WIDTH
    const = lambda i: (0, 0)
    return pl.pallas_call(
        _filt_body,
        grid=(seq // tm,),
        in_specs=[pl.BlockSpec((tm, LANES), lambda i: (i, 0)), pl.BlockSpec((tm, 1), lambda i: (i, 0)),
                  pl.BlockSpec((LANES, HY_FFN), const), pl.BlockSpec((1, HY_FFN), const),
                  pl.BlockSpec((1, HY_FFN), const), pl.BlockSpec((HY_FFN, HY_FFN), const),
                  pl.BlockSpec((1, HY_FFN), const), pl.BlockSpec((1, HY_FFN), const),
                  pl.BlockSpec((HY_FFN, hw), const), pl.BlockSpec((1, hw), const)],
        out_specs=[pl.BlockSpec((tm, hw), lambda i: (i, 0)), pl.BlockSpec((1, hw), const)],
        out_shape=[_sds((seq, hw)), _sds((1, hw))],
        compiler_params=_cp("arbitrary"),
        name="hyena_filter",
    )(feat, dist, w1p, b1[None], f1[None], w2, b2[None], f2[None], w3, decay.reshape(1, hw))


def dft_tables(seq):
    n = 2 * seq
    n2 = FFT_N2
    n1 = n // n2
    n1h = n1 // 2
    nk = -(-(n1h + 1) // SUBLANES) * SUBLANES
    k1 = jnp.arange(nk, dtype=I32)
    live = (k1 <= n1h).astype(F32)
    ang = (2.0 * jnp.pi / n1) * ((k1[:, None] * jnp.arange(n1h, dtype=I32)[None, :]) % n1).astype(F32)
    outer_fwd = jnp.concatenate([jnp.cos(ang), -jnp.sin(ang)], axis=0) * jnp.tile(live, 2)[:, None]
    t1 = jnp.arange(n1h, dtype=I32) + n1 // 4
    ang = (2.0 * jnp.pi / n1) * ((t1[:, None] * k1[None, :]) % n1).astype(F32)
    weight = live * jnp.where((k1 == 0) | (k1 == n1h), 1.0, 2.0) / n
    outer_inv = jnp.concatenate([jnp.cos(ang), -jnp.sin(ang)], axis=1) * jnp.tile(weight, 2)[None, :]
    k2 = jnp.arange(n2, dtype=I32)
    t2 = jnp.arange(n2, dtype=I32)
    phase = (t2[None, None, :] * (k2[None, :, None] * n1 + k1[:, None, None])) % n
    ang = (2.0 * jnp.pi / n) * phase.astype(F32)
    fr, fi = jnp.cos(ang), -jnp.sin(ang)
    inner = jnp.concatenate([jnp.concatenate([fr, -fi], axis=2),
                             jnp.concatenate([fi, fr], axis=2)], axis=1)
    return _split_bf16(outer_fwd), _split_bf16(outer_inv), _split_bf16(inner), _split_bf16(jnp.swapaxes(inner, 1, 2))


def _split_bf16(x):
    hi = x.astype(BF16)
    return hi, (x - hi.astype(F32)).astype(BF16)


def _dot_split(w, a):
    w_hi, w_lo = w
    a_hi, a_lo = _split_bf16(a)
    dot = functools.partial(jnp.dot, preferred_element_type=F32)
    return dot(w_hi, a_hi) + (dot(w_hi, a_lo) + dot(w_lo, a_hi))


def _dot_split_rhs(a, w):
    w_hi, w_lo = w
    a_hi, a_lo = _split_bf16(a)
    dot = functools.partial(jnp.dot, preferred_element_type=F32)
    return dot(a_hi, w_hi) + (dot(a_lo, w_hi) + dot(a_hi, w_lo))


def _every(ref, s, n):
    return ref[pl.ds(s, n, stride=FFT_ROWS), :]


def _fft_outer_body(f_ref, fl_ref, x_ref, o_ref):
    n1h = x_ref.shape[0]
    rows = f_ref.shape[0]
    x2 = x_ref.reshape(n1h * FFT_ROWS, LANES)
    o2 = o_ref.reshape(rows * FFT_ROWS, LANES)
    x = jnp.concatenate([_every(x2, s, n1h) for s in range(FFT_ROWS)], axis=1)
    r = _dot_split((f_ref[...], fl_ref[...]), x)
    for s in range(FFT_ROWS):
        o2[pl.ds(s, rows, stride=FFT_ROWS), :] = r[:, s * LANES:(s + 1) * LANES]


def fft_outer(x4, table):
    batch, n1h, n2, ch = x4.shape
    nk = table[0].shape[0] // 2
    tab = pl.BlockSpec((2 * nk, n1h), lambda b, c, l: (0, 0))
    return pl.pallas_call(
        _fft_outer_body,
        grid=(batch, n2 // FFT_ROWS, ch // LANES),
        in_specs=[tab, tab, pl.BlockSpec((None, n1h, FFT_ROWS, LANES), lambda b, c, l: (b, 0, c, l))],
        out_specs=pl.BlockSpec((None, 2, nk, FFT_ROWS, LANES), lambda b, c, l: (b, 0, 0, c, l)),
        out_shape=_sds((batch, 2, nk, n2, ch)),
        compiler_params=_cp("parallel", "parallel", "parallel"),
        name="fft_outer",
    )(*table, x4)


def _fft_filter_body(mh_ref, ml_ref, s_ref, a_ref, o_ref):
    n2 = FFT_N2
    a = a_ref[0, :, 0].reshape(2 * n2, HY_WIDTH)
    x = _dot_split((mh_ref[0], ml_ref[0]), a) * (1.0 / s_ref[...])
    o_ref[:, 0] = x.reshape(2, n2, HY_WIDTH)


def fft_filter(a5, inner, s):
    _, _, n1, n2, ch = a5.shape
    return pl.pallas_call(
        _fft_filter_body,
        grid=(n1, ch // HY_WIDTH),
        in_specs=[pl.BlockSpec((1, 2 * n2, 2 * n2), lambda k, c: (k, 0, 0)),
                  pl.BlockSpec((1, 2 * n2, 2 * n2), lambda k, c: (k, 0, 0)),
                  pl.BlockSpec((1, HY_WIDTH), lambda k, c: (0, c)),
                  pl.BlockSpec((1, 2, 1, n2, HY_WIDTH), lambda k, c: (0, 0, k, 0, c))],
        out_specs=pl.BlockSpec((2, 1, n2, HY_WIDTH), lambda k, c: (0, k, 0, c)),
        out_shape=_sds((2, n1, n2, ch)),
        compiler_params=_cp("parallel", "parallel"),
        name="fft_filter",
    )(*inner, s, a5)


def _fft_mid_body(mh_ref, ml_ref, th_ref, tl_ref, h_ref, a_ref, o_ref, *, nb):
    n2 = FFT_N2
    m = (mh_ref[0], ml_ref[0])
    mt = (th_ref[0], tl_ref[0])
    hr = h_ref[0, 0]
    hi = h_ref[1, 0]
    for b in range(nb):
        x = _dot_split(m, a_ref[b, :, 0].reshape(2 * n2, HY_WIDTH))
        xr, xi = x[:n2], x[n2:]
        y = jnp.concatenate([xr * hr - xi * hi, xr * hi + xi * hr], axis=0)
        o_ref[b, :, 0] = _dot_split(mt, y).reshape(2, n2, HY_WIDTH)


def fft_mid(a5, inner, inner_t, hf, order):
    nb, _, n1, n2, ch = a5.shape
    mat = pl.BlockSpec((1, 2 * n2, 2 * n2), lambda k: (k, 0, 0))
    blk = pl.BlockSpec((nb, 2, 1, n2, ch), lambda k: (0, 0, k, 0, 0))
    return pl.pallas_call(
        functools.partial(_fft_mid_body, nb=nb),
        grid=(n1,),
        in_specs=[mat, mat, mat, mat, pl.BlockSpec((2, 1, n2, ch), lambda k: (0, k, 0, order)), blk],
        out_specs=blk,
        out_shape=_sds(a5.shape),
        compiler_params=_cp("parallel"),
        name="fft_mid",
    )(*inner, *inner_t, hf, a5)


def _fft_inv_body(g_ref, gl_ref, b_ref, gate_ref, y_ref, fb_ref, o_ref):
    n1h, rows = g_ref.shape
    b2 = b_ref.reshape(rows * FFT_ROWS, LANES)
    gate2 = gate_ref.reshape(n1h * FFT_ROWS, LANES)
    y2 = y_ref.reshape(n1h * FFT_ROWS, LANES)
    o2 = o_ref.reshape(n1h * FFT_ROWS, LANES)
    bm = jnp.concatenate([_every(b2, s, rows) for s in range(FFT_ROWS)], axis=1)
    c = _dot_split((g_ref[...], gl_ref[...]), bm)
    fb = fb_ref[...]
    for s in range(FFT_ROWS):
        conv = c[:, s * LANES:(s + 1) * LANES]
        o2[pl.ds(s, n1h, stride=FFT_ROWS), :] = _every(gate2, s, n1h) * (conv + fb * _every(y2, s, n1h))


def fft_inv_gate(b5, table, gate4, y4, fbias):
    batch, _, nk, n2, ch = b5.shape
    n1h = table[0].shape[0]
    blk = pl.BlockSpec((None, n1h, FFT_ROWS, LANES), lambda b, c, l: (b, 0, c, l))
    tab = pl.BlockSpec((n1h, 2 * nk), lambda b, c, l: (0, 0))
    return pl.pallas_call(
        _fft_inv_body,
        grid=(batch, n2 // FFT_ROWS, ch // LANES),
        in_specs=[tab, tab,
                  pl.BlockSpec((None, 2, nk, FFT_ROWS, LANES), lambda b, c, l: (b, 0, 0, c, l)), blk, blk,
                  pl.BlockSpec((1, LANES), lambda b, c, l: (0, l))],
        out_specs=blk,
        out_shape=_sds((batch, n1h, n2, ch)),
        compiler_params=_cp("parallel", "parallel", "parallel"),
        name="fft_inv_gate",
    )(*table, b5, gate4, y4, fbias.reshape(1, ch))


def hyena_spectrum(seq, tabs, w1, b1, f1, w2, b2, f2, w3, decay):
    outer_fwd, _, inner, _ = tabs
    n1h = outer_fwd[0].shape[1]
    hw = HY_ORDER * HY_WIDTH
    h, s = hyena_filter(seq, w1, b1, f1, w2, b2, f2, w3, decay)
    return fft_filter(fft_outer(h.reshape(1, n1h, FFT_N2, hw), outer_fwd), inner, s)


def hyena_mixer(z_hy, conv_w, conv_b, hf, f_bias, tabs, batch, seq):
    outer_fwd, outer_inv, inner, inner_t = tabs
    n1h = outer_fwd[0].shape[1]
    shape4 = (batch, n1h, FFT_N2, HY_WIDTH)
    v, g0, g1 = hyena_conv3(z_hy, conv_w, conv_b, batch, seq)
    y = v.reshape(shape4)
    for order, gate in enumerate((g0, g1)):
        bm = fft_mid(fft_outer(y, outer_fwd), inner, inner_t, hf, order)
        y = fft_inv_gate(bm, outer_inv, gate.reshape(shape4), y, f_bias[order])
    return y.reshape(batch * seq, HY_WIDTH)


def _mix_body(lam_ref, h_ref, s5_ref, gqa_ref, diff_ref, hy_ref, gs5, ggqa, gsub, ghy, w_ref, o_ref, *, keep):
    lam = lam_ref[0]
    a = _rms(s5_ref[...], gs5[...])
    b = _rms(jnp.concatenate([gqa_ref[0, h] for h in range(GQA_HEADS)], axis=1), ggqa[...])
    c = jnp.concatenate(
        [_rms(diff_ref[0, 2 * h] - lam * diff_ref[0, 2 * h + 1], gsub[...]) * keep for h in range(DIFF_HEADS)],
        axis=1)
    d = _rms(hy_ref[...], ghy[...])
    mixed = jnp.concatenate([a, b, c, d], axis=1).astype(BF16)
    o_ref[...] = h_ref[...] + jnp.dot(mixed, w_ref[...], preferred_element_type=F32)


def mix_out(lam, h, y_s5, o_gqa, o_diff, y_hy, gs5, ggqa, gsub, ghy, w_out, batch, seq, lambda_init):
    tm = min(ROW_TILE, seq)
    nb = seq // tm
    row = lambda w: pl.BlockSpec((tm, w), lambda b, i: (b * nb + i, 0))
    vec = lambda w: pl.BlockSpec((1, w), lambda b, i: (0, 0))
    heads = lambda n: pl.BlockSpec((1, n, tm, HEAD_DIM), lambda b, i: (b, 0, i, 0))
    time_major = pl.BlockSpec((tm, GROUP_WIDTH), lambda b, i: (i, b))
    return pl.pallas_call(
        functools.partial(_mix_body, keep=1.0 - lambda_init),
        grid=(batch, nb),
        in_specs=[pl.BlockSpec(memory_space=pltpu.SMEM), row(D_MODEL), time_major, heads(GQA_HEADS),
                  heads(2 * DIFF_HEADS), row(GROUP_WIDTH), vec(GROUP_WIDTH), vec(GROUP_WIDTH), vec(HEAD_DIM),
                  vec(GROUP_WIDTH), pl.BlockSpec((D_MODEL, D_MODEL), lambda b, i: (0, 0))],
        out_specs=row(D_MODEL),
        out_shape=_sds((batch * seq, D_MODEL)),
        compiler_params=_cp("parallel", "parallel"),
        name="mix_out",
    )(lam, h, y_s5, o_gqa, o_diff, y_hy, gs5, ggqa, gsub, ghy, w_out)


def _router_body(h_ref, g_ref, w_ref, wl_ref, b_ref, xn_ref, ti_ref, tw_ref, cnt_ref):
    xn = _rms(h_ref[...], g_ref[...])
    xn_ref[...] = xn
    logits = _dot_split_rhs(xn, (w_ref[...], wl_ref[...])) + b_ref[...]
    lane = lax.broadcasted_iota(I32, logits.shape, 1)
    vals = logits
    tops, idxs = [], []
    hot = jnp.zeros(logits.shape, F32)
    for _ in range(TOP_K):
        m = jnp.max(vals, axis=-1, keepdims=True)
        idx = jnp.min(jnp.where(vals == m, lane, N_EXPERTS), axis=-1, keepdims=True)
        sel = lane == idx
        tops.append(m)
        idxs.append(idx)
        hot = hot + sel.astype(F32)
        vals = jnp.where(sel, -jnp.inf, vals)
    es = [jnp.exp(t - tops[0]) for t in tops]
    den = es[0] + es[1] + es[2] + es[3]
    ti_ref[...] = jnp.concatenate(idxs, axis=1)
    tw_ref[...] = jnp.concatenate([e / den for e in es], axis=1)

    @pl.when(pl.program_id(0) == 0)
    def _():
        cnt_ref[...] = jnp.zeros_like(cnt_ref)

    cnt_ref[...] += jnp.sum(hot, axis=0, keepdims=True)


def moe_router(h, g, w_router, b_router):
    t = h.shape[0]
    tm = min(ROW_TILE, t)
    const = lambda i: (0, 0)
    return pl.pallas_call(
        _router_body,
        grid=(t // tm,),
        in_specs=[pl.BlockSpec((tm, D_MODEL), lambda i: (i, 0)), pl.BlockSpec((1, D_MODEL), const),
                  pl.BlockSpec((D_MODEL, N_EXPERTS), const), pl.BlockSpec((D_MODEL, N_EXPERTS), const),
                  pl.BlockSpec((1, N_EXPERTS), const)],
        out_specs=[pl.BlockSpec((tm, D_MODEL), lambda i: (i, 0)), pl.BlockSpec((tm, TOP_K), lambda i: (i, 0)),
                   pl.BlockSpec((tm, TOP_K), lambda i: (i, 0)), pl.BlockSpec((1, N_EXPERTS), const)],
        out_shape=[_sds((t, D_MODEL)), _sds((t, TOP_K), I32), _sds((t, TOP_K)), _sds((1, N_EXPERTS))],
        compiler_params=_cp("arbitrary"),
        name="moe_router",
    )(h, g, *w_router, b_router)


def _rank_body(ti_ref, off_ref, pos_ref, stage_ref, cnt_ref, carry_ref):
    @pl.when(pl.program_id(0) == 0)
    def _():
        carry_ref[...] = jnp.zeros_like(carry_ref)

    ti = ti_ref[...]
    tm = ti.shape[0]
    lane = lax.broadcasted_iota(I32, (tm, N_EXPERTS), 1)
    hots = [lane == ti[:, k:k + 1] for k in range(TOP_K)]
    hot = sum(h.astype(F32) for h in hots)
    r = lax.broadcasted_iota(I32, (tm, tm), 0)
    c = lax.broadcasted_iota(I32, (tm, tm), 1)
    below = jnp.where(r > c, 1.0, 0.0).astype(BF16)
    local = jnp.dot(below, hot.astype(BF16), preferred_element_type=F32)
    before = local + carry_ref[...] + off_ref[...]
    pick = lambda table: jnp.concatenate(
        [jnp.sum(jnp.where(h, table, 0.0), axis=-1, keepdims=True) for h in hots], axis=1).astype(I32)
    pos_ref[...] = pick(before)
    count = jnp.sum(hot, axis=0, keepdims=True)
    er = lax.broadcasted_iota(I32, (N_EXPERTS, N_EXPERTS), 0)
    ec = lax.broadcasted_iota(I32, (N_EXPERTS, N_EXPERTS), 1)
    earlier = jnp.where(er < ec, 1.0, 0.0)
    chunks = jnp.ceil(count * (1.0 / GATHER_CHUNK))
    start = jnp.dot(chunks, earlier, precision=HI, preferred_element_type=F32) * GATHER_CHUNK
    stage_ref[...] = pick(local + start)
    cnt_ref[...] = jnp.broadcast_to(count, cnt_ref.shape)
    carry_ref[...] += count


def moe_rank(top_i, offsets):
    t = top_i.shape[0]
    tm = min(MOE_TOK_TILE, t)
    pairs = pl.BlockSpec((tm, TOP_K), lambda i: (i, 0))
    pos, stage, cnt = pl.pallas_call(
        _rank_body,
        grid=(t // tm,),
        in_specs=[pairs, pl.BlockSpec((1, N_EXPERTS), lambda i: (0, 0))],
        out_specs=[pairs, pairs, pl.BlockSpec((SUBLANES, N_EXPERTS), lambda i: (i, 0))],
        out_shape=[_sds((t, TOP_K), I32), _sds((t, TOP_K), I32), _sds((t // tm * SUBLANES, N_EXPERTS))],
        scratch_shapes=[pltpu.VMEM((1, N_EXPERTS), F32)],
        compiler_params=_cp("arbitrary"),
        name="moe_rank",
    )(top_i, offsets)
    return pos, stage, cnt[::SUBLANES]


def _row_copy(src, dst, sem):
    return pltpu.make_async_copy(src, dst, sem)


def _pad_zero_body(lt_ref, o_ref):
    del lt_ref
    o_ref[...] = jnp.zeros_like(o_ref)


def moe_pad_tiles(last_tile, n_slots):
    rows = MOE_TILE * ROW_TILES
    return pl.pallas_call(
        _pad_zero_body,
        grid_spec=pltpu.PrefetchScalarGridSpec(
            num_scalar_prefetch=1, grid=(N_EXPERTS,), in_specs=[],
            out_specs=pl.BlockSpec((rows, LANES), lambda e, lt: (lt[e], 0))),
        out_shape=_sds((n_slots * ROW_TILES, LANES)),
        compiler_params=_cp("arbitrary"),
        name="moe_pad_tiles",
    )(last_tile)


def _slot_rows(ref, slot):
    return ref.at[pl.ds(pl.multiple_of(slot * ROW_TILES, ROW_TILES), ROW_TILES)]


def _scatter_body(pos_ref, x_ref, padded_hbm, xs_hbm, buf, sem, *, tm):
    del padded_hbm
    x = x_ref[...]
    for s in range(ROW_TILES):
        buf[pl.ds(s, tm, stride=ROW_TILES), :] = x[:, s * LANES:(s + 1) * LANES]

    def start(r, c):
        for k in range(TOP_K):
            _row_copy(_slot_rows(buf, r), _slot_rows(xs_hbm, pos_ref[r * TOP_K + k]), sem).start()
        return c

    lax.fori_loop(0, tm, start, 0)
    for _ in range(TOP_K):
        _row_copy(buf, xs_hbm.at[pl.ds(0, tm * ROW_TILES)], sem).wait()


def moe_scatter(pos_flat, x, padded):
    t = x.shape[0]
    tm = min(MOE_TOK_TILE, t)
    return pl.pallas_call(
        functools.partial(_scatter_body, tm=tm),
        grid=(t // tm,),
        in_specs=[pl.BlockSpec((tm * TOP_K,), lambda i: (i,), memory_space=pltpu.SMEM),
                  pl.BlockSpec((tm, D_MODEL), lambda i: (i, 0)), pl.BlockSpec(memory_space=pl.ANY)],
        out_specs=pl.BlockSpec(memory_space=pl.ANY),
        out_shape=_sds(padded.shape),
        scratch_shapes=[pltpu.VMEM((tm * ROW_TILES, LANES), F32), pltpu.SemaphoreType.DMA(())],
        input_output_aliases={2: 0},
        compiler_params=_cp("arbitrary"),
        name="moe_scatter",
    )(pos_flat, x, padded)


def _experts_body(te_ref, nv_ref, x_ref, wgu_ref, bgu_ref, wd_ref, bd_ref, o_ref):
    del te_ref
    j = pl.program_id(0)
    dff = wd_ref.shape[1]

    @pl.when(j < nv_ref[0])
    def _():
        tm = x_ref.shape[0] // ROW_TILES
        x = jnp.concatenate([x_ref[pl.ds(s, tm, stride=ROW_TILES), :] for s in range(ROW_TILES)],
                            axis=1).astype(BF16)
        hgu = jnp.dot(x, wgu_ref[0], preferred_element_type=F32) + bgu_ref[0]
        glu = jnp.minimum(hgu[:, :dff], SWIGLU_LIMIT)
        lin = jnp.clip(hgu[:, dff:], -SWIGLU_LIMIT, SWIGLU_LIMIT)
        act = glu * jax.nn.sigmoid(SWIGLU_ALPHA * glu) * (lin + 1.0)
        y = jnp.dot(act.astype(BF16), wd_ref[0], preferred_element_type=F32) + bd_ref[0]
        for s in range(ROW_TILES):
            o_ref[pl.ds(s, tm, stride=ROW_TILES), :] = y[:, s * LANES:(s + 1) * LANES]

    @pl.when(j >= nv_ref[0])
    def _():
        o_ref[...] = jnp.zeros_like(o_ref)


def moe_experts(tile_expert, n_valid, xs, w_gu, b_gu, w_down, b_down):
    n_slots = xs.shape[0] // ROW_TILES
    tm = MOE_TILE
    dff = w_down.shape[1]
    blk = pl.BlockSpec((tm * ROW_TILES, LANES), lambda j, te, nv: (j, 0))
    return pl.pallas_call(
        _experts_body,
        grid_spec=pltpu.PrefetchScalarGridSpec(
            num_scalar_prefetch=2,
            grid=(n_slots // tm,),
            in_specs=[pl.BlockSpec((tm * ROW_TILES, LANES), lambda j, te, nv: (jnp.minimum(j, nv[0] - 1), 0)),
                      pl.BlockSpec((1, D_MODEL, 2 * dff), lambda j, te, nv: (te[j], 0, 0)),
                      pl.BlockSpec((1, 1, 2 * dff), lambda j, te, nv: (te[j], 0, 0)),
                      pl.BlockSpec((1, dff, D_MODEL), lambda j, te, nv: (te[j], 0, 0)),
                      pl.BlockSpec((1, 1, D_MODEL), lambda j, te, nv: (te[j], 0, 0))],
            out_specs=blk),
        out_shape=_sds((n_slots * ROW_TILES, LANES)),
        compiler_params=_cp("arbitrary"),
        name="moe_experts",
    )(tile_expert, n_valid, xs, w_gu, b_gu, w_down, b_down)


def _combine_body(slot_ref, w_ref, tab_ref, ys_hbm, o_ref, stage, mixed, sem, *, tm):
    chunk_rows = GATHER_CHUNK * ROW_TILES

    def chunk_copy(src_slot, dst_slot):
        return _row_copy(ys_hbm.at[pl.ds(pl.multiple_of(src_slot * ROW_TILES, ROW_TILES), chunk_rows)],
                         stage.at[pl.ds(pl.multiple_of(dst_slot * ROW_TILES, ROW_TILES), chunk_rows)], sem)

    def per_expert(e, c):
        base, chunks, start = tab_ref[e], tab_ref[N_EXPERTS + e], tab_ref[2 * N_EXPERTS + e]

        def issue(j, c2):
            chunk_copy(base + j * GATHER_CHUNK, start + j * GATHER_CHUNK).start()
            return c2

        return lax.fori_loop(0, chunks, issue, c)

    lax.fori_loop(0, N_EXPERTS, per_expert, 0)

    def wait(j, c):
        chunk_copy(0, 0).wait()
        return c

    lax.fori_loop(0, tab_ref[3 * N_EXPERTS], wait, 0)

    def mix(q, c):
        for u in range(MIX_UNROLL):
            r = q * MIX_UNROLL + u
            acc = None
            for k in range(TOP_K):
                p = r * TOP_K + k
                term = w_ref[p] * stage[pl.ds(pl.multiple_of(slot_ref[p] * ROW_TILES, ROW_TILES), ROW_TILES), :]
                acc = term if acc is None else acc + term
            mixed[pl.ds(pl.multiple_of(r * ROW_TILES, ROW_TILES), ROW_TILES), :] = acc
        return c

    lax.fori_loop(0, tm // MIX_UNROLL, mix, 0)
    o_ref[...] = jnp.concatenate([mixed[pl.ds(s, tm, stride=ROW_TILES), :] for s in range(ROW_TILES)], axis=1)


def moe_combine(stage_flat, w_flat, table, ys, t):
    tm = min(MOE_TOK_TILE, t)
    smem = pl.BlockSpec((tm * TOP_K,), lambda i: (i,), memory_space=pltpu.SMEM)
    stage_slots = tm * TOP_K + N_EXPERTS * GATHER_CHUNK
    return pl.pallas_call(
        functools.partial(_combine_body, tm=tm),
        grid=(t // tm,),
        in_specs=[smem, smem, pl.BlockSpec((LANES,), lambda i: (i,), memory_space=pltpu.SMEM),
                  pl.BlockSpec(memory_space=pl.ANY)],
        out_specs=pl.BlockSpec((tm, D_MODEL), lambda i: (i, 0)),
        out_shape=_sds((t, D_MODEL)),
        scratch_shapes=[pltpu.VMEM((stage_slots * ROW_TILES, LANES), F32),
                        pltpu.VMEM((tm * ROW_TILES, LANES), F32), pltpu.SemaphoreType.DMA(())],
        compiler_params=_cp("arbitrary"),
        name="moe_combine",
    )(stage_flat, w_flat, table, ys)


def gather_table(cnt_tile, offsets):
    base = offsets + jnp.cumsum(cnt_tile, axis=0) - cnt_tile
    chunks = jnp.ceil(cnt_tile / GATHER_CHUNK)
    start = (jnp.cumsum(chunks, axis=1) - chunks) * GATHER_CHUNK
    total = jnp.sum(chunks, axis=1, keepdims=True)
    pad = jnp.zeros((cnt_tile.shape[0], LANES - 3 * N_EXPERTS - 1), F32)
    return jnp.concatenate([base, chunks, start, total, pad], axis=1).astype(I32).reshape(-1)


def moe(h, g, w_router, b_router, w_gu, b_gu, w_down, b_down):
    t = h.shape[0]
    xn, top_i, top_w, counts = moe_router(h, g, w_router, b_router)
    n_tiles = (t * TOP_K) // MOE_TILE + N_EXPERTS + 1
    padded = jnp.ceil(counts[0] / MOE_TILE) * MOE_TILE
    ends = jnp.cumsum(padded)
    offsets = (ends - padded)[None]
    tile_start = jnp.arange(n_tiles, dtype=F32) * MOE_TILE
    tile_expert = jnp.minimum(jnp.sum(ends[None, :] <= tile_start[:, None], axis=1), N_EXPERTS - 1).astype(I32)
    n_valid = (ends[-1:] / MOE_TILE).astype(I32)
    own_last = jnp.where(padded > 0, ends / MOE_TILE - 1, -1.0)
    filled = lax.cummax(own_last, axis=0)
    first = jnp.min(jnp.where(padded > 0, own_last, jnp.inf))
    last_tile = jnp.where(filled < 0, first, filled).astype(I32)
    pos, stage, cnt_tile = moe_rank(top_i, offsets)
    xs = moe_scatter(pos.reshape(t * TOP_K), xn, moe_pad_tiles(last_tile, n_tiles * MOE_TILE))
    ys = moe_experts(tile_expert, n_valid, xs, w_gu, b_gu, w_down, b_down)
    flat = lambda a: a.reshape(t * TOP_K)
    return moe_combine(flat(stage), flat(top_w), gather_table(cnt_tile, offsets), ys, t)


def _ple_body(h_ref, m_ref, p_ref, wp_ref, wg_ref, o_ref):
    h = h_ref[...] + m_ref[...]
    e = jnp.dot(p_ref[...].astype(BF16), wp_ref[...], preferred_element_type=F32)
    gate = jnp.dot(_rms(h).astype(BF16), wg_ref[...], preferred_element_type=F32)
    o_ref[...] = h + e * jax.nn.sigmoid(gate)


def ple(h, moe_out, p, w_ple, w_gate):
    t = h.shape[0]
    tm = min(ROW_TILE, t)
    row = pl.BlockSpec((tm, D_MODEL), lambda i: (i, 0))
    return pl.pallas_call(
        _ple_body,
        grid=(t // tm,),
        in_specs=[row, row, pl.BlockSpec((tm, PLE_DIM), lambda i: (i, 0)),
                  pl.BlockSpec((PLE_DIM, D_MODEL), lambda i: (0, 0)),
                  pl.BlockSpec((D_MODEL, D_MODEL), lambda i: (0, 0))],
        out_specs=row,
        out_shape=_sds((t, D_MODEL)),
        compiler_params=_cp("parallel"),
        name="ple",
    )(h, moe_out, p, w_ple, w_gate)


def _deinterleave(n):
    return np.concatenate([np.arange(0, n, 2), np.arange(1, n, 2)])


def _halves(w):
    pairs = w.reshape(*w.shape[:-1], w.shape[-1] // 2, 2)
    return jnp.concatenate([pairs[..., 0], pairs[..., 1]], axis=-1)


def prepare_layer(i, prm):
    pair = _deinterleave(HEAD_DIM)
    cols = np.arange(IN_COLS)
    for h in range(GQA_HEADS + GQA_KV_HEADS):
        lo = OFF_GQA + h * HEAD_DIM
        cols[lo:lo + HEAD_DIM] = lo + pair
    lam = (jnp.exp(jnp.sum(prm["diff_lam_q1"][i] * prm["diff_lam_k1"][i]))
           - jnp.exp(jnp.sum(prm["diff_lam_q2"][i] * prm["diff_lam_k2"][i])) + _lambda_init(i))
    return dict(
        norm1_g=prm["norm1_g"][i][None],
        w_in=prm["w_in"][i][:, cols].astype(BF16),
        s5=s5_tables(prm["s5_lam_re"][i], prm["s5_lam_im"][i], prm["s5_log_dt"][i], prm["s5_b_re"][i],
                     prm["s5_b_im"][i], prm["s5_c_re"][i], prm["s5_c_im"][i]),
        s5_d=prm["s5_d"][i][None],
        s5_w_glu=prm["s5_w_glu"][i].astype(BF16),
        s5_b_glu=prm["s5_b_glu"][i][None],
        gqa_q_g=prm["gqa_q_g"][i][pair][None],
        gqa_k_g=prm["gqa_k_g"][i][pair][None],
        diff_q_g=prm["diff_q_g"][i][None],
        diff_k_g=prm["diff_k_g"][i][None],
        diff_lam=lam.reshape(1).astype(F32),
        diff_subln_g=prm["diff_subln_g"][i][None],
        out_g_s5=prm["out_g_s5"][i][None],
        out_g_gqa=prm["out_g_gqa"][i][None],
        out_g_hy=prm["out_g_hy"][i][None],
        w_out=prm["w_out"][i].astype(BF16),
        norm2_g=prm["norm2_g"][i][None],
        w_router=_split_bf16(prm["w_router"][i]),
        b_router=prm["b_router"][i][None],
        w_gu=_halves(prm["w_gu"][i]).astype(BF16),
        b_gu=_halves(prm["b_gu"][i])[:, None, :],
        w_down=prm["w_down"][i].astype(BF16),
        b_down=prm["b_down"][i][:, None, :],
        w_ple=prm["w_ple"][i].astype(BF16),
        w_ple_gate=prm["w_ple_gate"][i].astype(BF16),
    )


def _lambda_init(i):
    return 0.8 - 0.6 * math.exp(-0.3 * i)


def run_trunk(x, p, prm, layers):
    batch, seq, _ = x.shape
    t = batch * seq
    cos_t, sin_t = rope_tables(seq)
    tabs = dft_tables(seq)
    bias = diff_bias_tiles(prm["rel_bias"], min(ATTN_UNIT, seq))
    h = x.reshape(t, D_MODEL)
    for i, lw in enumerate(layers):
        z_s5, z_gqa, z_diff, z_hy = in_proj(h, lw["norm1_g"], lw["w_in"], batch, seq)
        y_s5 = s5_mixer(z_s5, lw["s5"], lw["s5_d"], lw["s5_w_glu"], lw["s5_b_glu"], batch, seq)
        rep = GQA_HEADS // GQA_KV_HEADS
        q, k, v = attn_prep(z_gqa, lw["gqa_q_g"], lw["gqa_k_g"], cos_t, sin_t, batch, seq,
                            GQA_HEADS, GQA_KV_HEADS, GQA_KV_HEADS, True, lambda hq: hq // rep)
        o_gqa = flash_attention(q, k, v, lambda hq: hq // rep)
        q, k, v = attn_prep(z_diff, lw["diff_q_g"], lw["diff_k_g"], cos_t, sin_t, batch, seq,
                            2 * DIFF_HEADS, 2 * DIFF_HEADS, DIFF_HEADS, False, lambda hq: hq)
        o_diff = flash_attention(q, k, v, lambda hq: hq // 2, bias)
        hf = hyena_spectrum(seq, tabs, prm["hy_w1"][i], prm["hy_b1"][i], prm["hy_freq1"][i], prm["hy_w2"][i],
                            prm["hy_b2"][i], prm["hy_freq2"][i], prm["hy_w3"][i], prm["hy_decay"][i])
        y_hy = hyena_mixer(z_hy, prm["hy_conv_w"][i], prm["hy_conv_b"][i][None], hf, prm["hy_bias"][i],
                           tabs, batch, seq)
        h = mix_out(lw["diff_lam"], h, y_s5, o_gqa, o_diff, y_hy, lw["out_g_s5"], lw["out_g_gqa"],
                    lw["diff_subln_g"], lw["out_g_hy"], lw["w_out"], batch, seq, _lambda_init(i))
        m = moe(h, lw["norm2_g"], lw["w_router"], lw["b_router"], lw["w_gu"], lw["b_gu"], lw["w_down"],
                lw["b_down"])
        h = ple(h, m, p[i].reshape(t, PLE_DIM), lw["w_ple"], lw["w_ple_gate"])
    return h.reshape(batch, seq, D_MODEL)


def kernel(x_prompt, x_sample, p_prompt, p_sample, rel_bias, norm1_g, w_in, s5_lam_re, s5_lam_im, s5_log_dt,
           s5_b_re, s5_b_im, s5_c_re, s5_c_im, s5_d, s5_w_glu, s5_b_glu, gqa_q_g, gqa_k_g, diff_q_g, diff_k_g,
           diff_lam_q1, diff_lam_k1, diff_lam_q2, diff_lam_k2, diff_subln_g, hy_conv_w, hy_conv_b, hy_w1, hy_b1,
           hy_freq1, hy_w2, hy_b2, hy_freq2, hy_w3, hy_decay, hy_bias, out_g_s5, out_g_gqa, out_g_hy, w_out,
           norm2_g, w_router, b_router, w_gu, b_gu, w_down, b_down, w_ple, w_ple_gate):
    prm = dict(rel_bias=rel_bias, norm1_g=norm1_g, w_in=w_in, s5_lam_re=s5_lam_re, s5_lam_im=s5_lam_im,
               s5_log_dt=s5_log_dt, s5_b_re=s5_b_re, s5_b_im=s5_b_im, s5_c_re=s5_c_re, s5_c_im=s5_c_im,
               s5_d=s5_d, s5_w_glu=s5_w_glu, s5_b_glu=s5_b_glu, gqa_q_g=gqa_q_g, gqa_k_g=gqa_k_g,
               diff_q_g=diff_q_g, diff_k_g=diff_k_g, diff_lam_q1=diff_lam_q1, diff_lam_k1=diff_lam_k1,
               diff_lam_q2=diff_lam_q2, diff_lam_k2=diff_lam_k2, diff_subln_g=diff_subln_g,
               hy_conv_w=hy_conv_w, hy_conv_b=hy_conv_b, hy_w1=hy_w1, hy_b1=hy_b1, hy_freq1=hy_freq1,
               hy_w2=hy_w2, hy_b2=hy_b2, hy_freq2=hy_freq2, hy_w3=hy_w3, hy_decay=hy_decay, hy_bias=hy_bias,
               out_g_s5=out_g_s5, out_g_gqa=out_g_gqa, out_g_hy=out_g_hy, w_out=w_out, norm2_g=norm2_g,
               w_router=w_router, b_router=b_router, w_gu=w_gu, b_gu=b_gu, w_down=w_down, b_down=b_down,
               w_ple=w_ple, w_ple_gate=w_ple_gate)
    layers = [prepare_layer(i, prm) for i in range(DEPTH)]
    return (run_trunk(x_prompt, p_prompt, prm, layers), run_trunk(x_sample, p_sample, prm, layers))
```
